```python
import jax, jax.numpy as jnp
from jax import lax
import numpy as np

D_MODEL = 1024
BATCH = 8
SEQ = 8192
DEPTH = 2

GRID_W = 64
CTX_LEN = 256
N_HEADS = 16
N_KV_HEADS = 4
HEAD_DIM = 64
Q_W = N_HEADS * HEAD_DIM
KV_W = N_KV_HEADS * HEAD_DIM
WINDOW = 128
BLK = 128
ROPE_THETA = 10000.0
CHUNK = 128
A_W = D_MODEL
A_GROUPS = 8
A_GW = A_W // A_GROUPS
B_W = D_MODEL
CONV_W = 3
N_BRANCH = 3
BRANCH_W = D_MODEL
D_FF = 2816
EPS = 1e-6
NEG = -1e30
OFF_Q = 0
OFF_K = OFF_Q + Q_W
OFF_V = OFF_K + KV_W
OFF_A = OFF_V + KV_W
OFF_B = OFF_A + 2 * A_W
OFF_G = OFF_B + 3 * B_W
IN_W = OFF_G + N_BRANCH * D_MODEL

kernel_name = "hybrid_gated_parallel_dit_block"


def rmsnorm(x, g):
    xf = x.astype(jnp.float32)
    y = xf * lax.rsqrt(jnp.mean(xf * xf, axis=-1, keepdims=True) + EPS)
    return (y * g.astype(jnp.float32)).astype(x.dtype)


def modulate(h, shift, scale):
    return h * (1.0 + scale) + shift


def dwconv3(x, w):
    ch = x.shape[-1]
    return lax.conv_general_dilated(
        x, w[:, None, :].astype(x.dtype), window_strides=(1,), padding=[(CONV_W // 2, CONV_W // 2)],
        dimension_numbers=("NWC", "WIO", "NWC"), feature_group_count=ch)


def axial_rope_tables(n):
    rows = n // GRID_W
    row = jnp.broadcast_to(jnp.arange(rows, dtype=jnp.float32)[:, None], (rows, GRID_W)).reshape(n)
    col = jnp.broadcast_to(jnp.arange(GRID_W, dtype=jnp.float32)[None, :], (rows, GRID_W)).reshape(n)
    half = HEAD_DIM // 2
    inv = ROPE_THETA ** (-jnp.arange(0, half, 2, dtype=jnp.float32) / half)
    ang = jnp.concatenate([row[:, None] * inv, col[:, None] * inv], axis=-1)
    return jnp.cos(ang), jnp.sin(ang)


def apply_rope(t, cos, sin):
    half = HEAD_DIM // 2
    tf = t.astype(jnp.float32)
    t1, t2 = tf[..., :half], tf[..., half:]
    cs, sn = cos[None, :, None, :], sin[None, :, None, :]
    return jnp.concatenate([t1 * cs - t2 * sn, t1 * sn + t2 * cs], axis=-1).astype(t.dtype)


def latent_attention(q, k, v, kc, vc, sink):
    b, n = q.shape[:2]
    nb = n // BLK
    grp = N_HEADS // N_KV_HEADS
    m = kc.shape[1]
    scale = HEAD_DIM ** -0.5
    qb = q.reshape(b, nb, BLK, N_KV_HEADS, grp, HEAD_DIM)

    def neighbours(t):
        tb = t.reshape(b, nb, BLK, N_KV_HEADS, HEAD_DIM)
        tp = jnp.pad(tb, ((0, 0), (1, 1), (0, 0), (0, 0), (0, 0)))
        return jnp.concatenate([tp[:, :-2], tp[:, 1:-1], tp[:, 2:]], axis=2)

    kn, vn = neighbours(k), neighbours(v)
    rel = (jnp.arange(3 * BLK)[None, :] - BLK) - jnp.arange(BLK)[:, None]
    band = jnp.abs(rel) <= WINDOW
    src_blk = jnp.arange(nb)[:, None] + jnp.arange(3 * BLK)[None, :] // BLK - 1
    in_range = (src_blk >= 0) & (src_blk < nb)
    sink_l = sink.astype(jnp.float32).reshape(N_KV_HEADS, grp)[None, :, :, None, None]

    def block(args):
        qi, ki, vi, ok = args
        s_loc = jnp.einsum("bqkgd,bjkd->bkgqj", qi, ki).astype(jnp.float32) * scale
        s_loc = jnp.where(band & ok[None, :], s_loc, NEG)
        s_ctx = jnp.einsum("bqkgd,bckd->bkgqc", qi, kc).astype(jnp.float32) * scale
        s_snk = jnp.broadcast_to(sink_l, s_ctx.shape[:-1] + (1,))
        p = jax.nn.softmax(jnp.concatenate([s_loc, s_ctx, s_snk], axis=-1), axis=-1).astype(vi.dtype)
        return (jnp.einsum("bkgqj,bjkd->bqkgd", p[..., :3 * BLK], vi)
                + jnp.einsum("bkgqc,bckd->bqkgd", p[..., 3 * BLK:3 * BLK + m], vc))

    xs = (jnp.moveaxis(qb, 1, 0), jnp.moveaxis(kn, 1, 0), jnp.moveaxis(vn, 1, 0), in_range)
    out = lax.map(block, xs)
    return jnp.moveaxis(out, 0, 1).reshape(b, n, Q_W)


def context_attention(qc, kc, vc, sink):
    b, m = qc.shape[:2]
    grp = N_HEADS // N_KV_HEADS
    qg = qc.reshape(b, m, N_KV_HEADS, grp, HEAD_DIM)
    s = jnp.einsum("bqkgd,bckd->bkgqc", qg, kc).astype(jnp.float32) * (HEAD_DIM ** -0.5)
    s_snk = jnp.broadcast_to(sink.astype(jnp.float32).reshape(N_KV_HEADS, grp)[None, :, :, None, None],
                             s.shape[:-1] + (1,))
    p = jax.nn.softmax(jnp.concatenate([s, s_snk], axis=-1), axis=-1).astype(vc.dtype)
    o = jnp.einsum("bkgqc,bckd->bqkgd", p[..., :m], vc)
    return o.reshape(b, m, Q_W)


def chunk_spatial_gating(z_a, w_s, b_s, g_v):
    z = jax.nn.gelu(z_a)
    u, v = z[..., :A_W], z[..., A_W:]
    v = rmsnorm(v, g_v)
    b, n = v.shape[:2]
    vr = v.reshape(b, n // CHUNK, CHUNK, A_GROUPS, A_GW)
    mixed = jnp.einsum("gpq,bnqgc->bnpgc", w_s, vr) + b_s.T[None, None, :, :, None]
    return u * mixed.reshape(b, n, A_W)


def short_conv_mixer(z_b, w_sconv):
    bg, cg, hb = z_b[..., :B_W], z_b[..., B_W:2 * B_W], z_b[..., 2 * B_W:]
    return bg * dwconv3(cg * hb, w_sconv)


def merge_branches(z, y_attn, w_s, b_s, g_v, w_sconv, b_gate, w_branch, w_out):
    y_a = chunk_spatial_gating(z[..., OFF_A:OFF_B], w_s, b_s, g_v)
    y_b = short_conv_mixer(z[..., OFF_B:OFF_G], w_sconv)
    gates = jax.nn.sigmoid(z[..., OFF_G:] + b_gate)
    g_c, g_a, g_b = gates[..., :D_MODEL], gates[..., D_MODEL:2 * D_MODEL], gates[..., 2 * D_MODEL:]
    merged = (g_c * (y_attn @ w_branch[0]) + g_a * (y_a @ w_branch[1]) + g_b * (y_b @ w_branch[2]))
    return merged @ w_out


def conv_ffn(h, w_up, w_fconv, w_down):
    up = h @ w_up
    a, g = up[..., :D_FF], up[..., D_FF:]
    return (jax.nn.silu(dwconv3(a, w_fconv)) * g) @ w_down


def _fwd_setup_inputs(seed: int = 0) -> dict:
    key = jax.random.key(seed)
    ks = jax.random.split(key, 24)
    f32 = jnp.float32

    def nrm(k, shape, scale):
        return jax.random.normal(k, shape, f32) * scale

    return {
        "x": nrm(ks[0], (BATCH, SEQ, D_MODEL), 1.0),
        "c": nrm(ks[1], (BATCH, D_MODEL), 1.0),
        "ctx": nrm(ks[2], (BATCH, CTX_LEN, D_MODEL), 1.0),
        "c_ctx": nrm(ks[3], (D_MODEL,), 1.0),
        "w_mod": nrm(ks[4], (DEPTH, D_MODEL, 6 * D_MODEL), D_MODEL ** -0.5),
        "b_mod": nrm(ks[5], (DEPTH, 6 * D_MODEL), 0.02),
        "g_mix": 1.0 + nrm(ks[6], (DEPTH, D_MODEL), 0.02),
        "w_in": nrm(ks[7], (DEPTH, D_MODEL, IN_W), D_MODEL ** -0.5),
        "b_gate": nrm(ks[8], (DEPTH, N_BRANCH * D_MODEL), 0.02),
        "sink": nrm(ks[9], (DEPTH, N_HEADS), 0.5),
        "w_spatial": nrm(ks[10], (DEPTH, A_GROUPS, CHUNK, CHUNK), CHUNK ** -0.5),
        "b_spatial": 1.0 + nrm(ks[11], (DEPTH, A_GROUPS, CHUNK), 0.02),
        "g_v": 1.0 + nrm(ks[12], (DEPTH, A_W), 0.02),
        "w_sconv": nrm(ks[13], (DEPTH, CONV_W, B_W), CONV_W ** -0.5),
        "w_branch": nrm(ks[14], (DEPTH, N_BRANCH, BRANCH_W, D_MODEL), BRANCH_W ** -0.5),
        "w_out": nrm(ks[15], (DEPTH, D_MODEL, D_MODEL), D_MODEL ** -0.5),
        "g_ffn": 1.0 + nrm(ks[16], (DEPTH, D_MODEL), 0.02),
        "w_up": nrm(ks[17], (DEPTH, D_MODEL, 2 * D_FF), D_MODEL ** -0.5),
        "w_fconv": nrm(ks[18], (DEPTH, CONV_W, D_FF), CONV_W ** -0.5),
        "w_down": nrm(ks[19], (DEPTH, D_FF, D_MODEL), D_FF ** -0.5),
        "g_final": 1.0 + nrm(ks[20], (D_MODEL,), 0.02),
    }


def _fwd_reference(x, c, ctx, c_ctx, w_mod, b_mod, g_mix, w_in, b_gate, sink, w_spatial, b_spatial, g_v,
              w_sconv, w_branch, w_out, g_ffn, w_up, w_fconv, w_down, g_final):
    b, n, _ = x.shape
    m = ctx.shape[1]
    cos, sin = axial_rope_tables(n)
    xc = ctx
    for l in range(DEPTH):
        last = l == DEPTH - 1
        mod = jax.nn.silu(c) @ w_mod[l] + b_mod[l]
        mod_c = jax.nn.silu(c_ctx) @ w_mod[l] + b_mod[l]
        sh1, sc1, gt1, sh2, sc2, gt2 = [t[:, None, :] for t in jnp.split(mod, 6, axis=-1)]
        csh1, csc1, cgt1, csh2, csc2, cgt2 = jnp.split(mod_c, 6, axis=-1)

        h = modulate(rmsnorm(x, g_mix[l]), sh1, sc1)
        hc = modulate(rmsnorm(xc, g_mix[l]), csh1, csc1)
        z = h @ w_in[l]
        q = apply_rope(z[..., OFF_Q:OFF_K].reshape(b, n, N_HEADS, HEAD_DIM), cos, sin)
        k = apply_rope(z[..., OFF_K:OFF_V].reshape(b, n, N_KV_HEADS, HEAD_DIM), cos, sin)
        v = z[..., OFF_V:OFF_A].reshape(b, n, N_KV_HEADS, HEAD_DIM)
        if last:
            zc = hc @ w_in[l][:, OFF_K:OFF_A]
            kc = zc[..., :KV_W].reshape(b, m, N_KV_HEADS, HEAD_DIM)
            vc = zc[..., KV_W:].reshape(b, m, N_KV_HEADS, HEAD_DIM)
        else:
            zc = hc @ w_in[l]
            kc = zc[..., OFF_K:OFF_V].reshape(b, m, N_KV_HEADS, HEAD_DIM)
            vc = zc[..., OFF_V:OFF_A].reshape(b, m, N_KV_HEADS, HEAD_DIM)
        y_attn = latent_attention(q, k, v, kc, vc, sink[l])
        x = x + gt1 * merge_branches(z, y_attn, w_spatial[l], b_spatial[l], g_v[l], w_sconv[l],
                                     b_gate[l], w_branch[l], w_out[l])
        h2 = modulate(rmsnorm(x, g_ffn[l]), sh2, sc2)
        x = x + gt2 * conv_ffn(h2, w_up[l], w_fconv[l], w_down[l])

        if not last:
            qc = zc[..., OFF_Q:OFF_K].reshape(b, m, N_HEADS, HEAD_DIM)
            yc_attn = context_attention(qc, kc, vc, sink[l])
            xc = xc + cgt1 * merge_branches(zc, yc_attn, w_spatial[l], b_spatial[l], g_v[l], w_sconv[l],
                                            b_gate[l], w_branch[l], w_out[l])
            hc2 = modulate(rmsnorm(xc, g_ffn[l]), csh2, csc2)
            xc = xc + cgt2 * conv_ffn(hc2, w_up[l], w_fconv[l], w_down[l])
    return rmsnorm(x, g_final)


import jax as _jax
import jax.numpy as _jnp

TWIN_FORMAT = 'train_step'
FWD_PARAMS = ['x', 'c', 'ctx', 'c_ctx', 'w_mod', 'b_mod', 'g_mix', 'w_in', 'b_gate', 'sink', 'w_spatial', 'b_spatial', 'g_v', 'w_sconv', 'w_branch', 'w_out', 'g_ffn', 'w_up', 'w_fconv', 'w_down', 'g_final']
TWIN_WEIGHTS = ['c_ctx', 'w_mod', 'b_mod', 'g_mix', 'w_in', 'b_gate', 'sink', 'w_spatial', 'b_spatial', 'g_v', 'w_sconv', 'w_branch', 'w_out', 'g_ffn', 'w_up', 'w_fconv', 'w_down', 'g_final']
TWIN_DIFF_INPUT = 'x'
TWIN_INPUTS = ['x', 'c', 'ctx', 'c_ctx', 'w_mod', 'b_mod', 'g_mix', 'w_in', 'b_gate', 'sink', 'w_spatial', 'b_spatial', 'g_v', 'w_sconv', 'w_branch', 'w_out', 'g_ffn', 'w_up', 'w_fconv', 'w_down', 'g_final', 'loss_target', 'm_c_ctx', 'm_w_mod', 'm_b_mod', 'm_g_mix', 'm_w_in', 'm_b_gate', 'm_sink', 'm_w_spatial', 'm_b_spatial', 'm_g_v', 'm_w_sconv', 'm_w_branch', 'm_w_out', 'm_g_ffn', 'm_w_up', 'm_w_fconv', 'm_w_down', 'm_g_final', 'v_c_ctx', 'v_w_mod', 'v_b_mod', 'v_g_mix', 'v_w_in', 'v_b_gate', 'v_sink', 'v_w_spatial', 'v_b_spatial', 'v_g_v', 'v_w_sconv', 'v_w_branch', 'v_w_out', 'v_g_ffn', 'v_w_up', 'v_w_fconv', 'v_w_down', 'v_g_final']
TWIN_OUTPUTS = ['loss', 'grad_x', 'grad_c_ctx', 'grad_w_mod', 'grad_b_mod', 'grad_g_mix', 'grad_w_in', 'grad_b_gate', 'grad_sink', 'grad_w_spatial', 'grad_b_spatial', 'grad_g_v', 'grad_w_sconv', 'grad_w_branch', 'grad_w_out', 'grad_g_ffn', 'grad_w_up', 'grad_w_fconv', 'grad_w_down', 'grad_g_final', 'delta_c_ctx', 'delta_w_mod', 'delta_b_mod', 'delta_g_mix', 'delta_w_in', 'delta_b_gate', 'delta_sink', 'delta_w_spatial', 'delta_b_spatial', 'delta_g_v', 'delta_w_sconv', 'delta_w_branch', 'delta_w_out', 'delta_g_ffn', 'delta_w_up', 'delta_w_fconv', 'delta_w_down', 'delta_g_final', 'new_m_c_ctx', 'new_m_w_mod', 'new_m_b_mod', 'new_m_g_mix', 'new_m_w_in', 'new_m_b_gate', 'new_m_sink', 'new_m_w_spatial', 'new_m_b_spatial', 'new_m_g_v', 'new_m_w_sconv', 'new_m_w_branch', 'new_m_w_out', 'new_m_g_ffn', 'new_m_w_up', 'new_m_w_fconv', 'new_m_w_down', 'new_m_g_final', 'new_v_c_ctx', 'new_v_w_mod', 'new_v_b_mod', 'new_v_g_mix', 'new_v_w_in', 'new_v_b_gate', 'new_v_sink', 'new_v_w_spatial', 'new_v_b_spatial', 'new_v_g_v', 'new_v_w_sconv', 'new_v_w_branch', 'new_v_w_out', 'new_v_g_ffn', 'new_v_w_up', 'new_v_w_fconv', 'new_v_w_down', 'new_v_g_final']
TWIN_LEAF_KINDS = {'loss': 'loss', 'grad_x': 'grad_x', 'grad_c_ctx': 'grad_w', 'grad_w_mod': 'grad_w', 'grad_b_mod': 'grad_w', 'grad_g_mix': 'grad_w', 'grad_w_in': 'grad_w', 'grad_b_gate': 'grad_w', 'grad_sink': 'grad_w', 'grad_w_spatial': 'grad_w', 'grad_b_spatial': 'grad_w', 'grad_g_v': 'grad_w', 'grad_w_sconv': 'grad_w', 'grad_w_branch': 'grad_w', 'grad_w_out': 'grad_w', 'grad_g_ffn': 'grad_w', 'grad_w_up': 'grad_w', 'grad_w_fconv': 'grad_w', 'grad_w_down': 'grad_w', 'grad_g_final': 'grad_w', 'delta_c_ctx': 'delta_w', 'delta_w_mod': 'delta_w', 'delta_b_mod': 'delta_w', 'delta_g_mix': 'delta_w', 'delta_w_in': 'delta_w', 'delta_b_gate': 'delta_w', 'delta_sink': 'delta_w', 'delta_w_spatial': 'delta_w', 'delta_b_spatial': 'delta_w', 'delta_g_v': 'delta_w', 'delta_w_sconv': 'delta_w', 'delta_w_branch': 'delta_w', 'delta_w_out': 'delta_w', 'delta_g_ffn': 'delta_w', 'delta_w_up': 'delta_w', 'delta_w_fconv': 'delta_w', 'delta_w_down': 'delta_w', 'delta_g_final': 'delta_w', 'new_m_c_ctx': 'new_m', 'new_m_w_mod': 'new_m', 'new_m_b_mod': 'new_m', 'new_m_g_mix': 'new_m', 'new_m_w_in': 'new_m', 'new_m_b_gate': 'new_m', 'new_m_sink': 'new_m', 'new_m_w_spatial': 'new_m', 'new_m_b_spatial': 'new_m', 'new_m_g_v': 'new_m', 'new_m_w_sconv': 'new_m', 'new_m_w_branch': 'new_m', 'new_m_w_out': 'new_m', 'new_m_g_ffn': 'new_m', 'new_m_w_up': 'new_m', 'new_m_w_fconv': 'new_m', 'new_m_w_down': 'new_m', 'new_m_g_final': 'new_m', 'new_v_c_ctx': 'new_v', 'new_v_w_mod': 'new_v', 'new_v_b_mod': 'new_v', 'new_v_g_mix': 'new_v', 'new_v_w_in': 'new_v', 'new_v_b_gate': 'new_v', 'new_v_sink': 'new_v', 'new_v_w_spatial': 'new_v', 'new_v_b_spatial': 'new_v', 'new_v_g_v': 'new_v', 'new_v_w_sconv': 'new_v', 'new_v_w_branch': 'new_v', 'new_v_w_out': 'new_v', 'new_v_g_ffn': 'new_v', 'new_v_w_up': 'new_v', 'new_v_w_fconv': 'new_v', 'new_v_w_down': 'new_v', 'new_v_g_final': 'new_v'}


def _forward(args):
    return _fwd_reference(*[args[k] for k in FWD_PARAMS])


def _output_shape():
    def fwd():
        inp = _fwd_setup_inputs(0)
        return _fwd_reference(*[inp[k] for k in FWD_PARAMS])
    out = _jax.eval_shape(fwd)
    return out.shape, out.dtype

N_MICROBATCH = 1
ADAM_LR = 0.001
ADAM_B1 = 0.9
ADAM_B2 = 0.999
ADAM_EPS = 1e-08
ADAM_WD = 0.01
ADAM_STEP = 10
PER_EXAMPLE_BATCH_AXIS = {'x': 0, 'c': 0, 'ctx': 0, 'loss_target': 0}
SHARED_INPUTS = []
_WEIGHT_DTYPES = {'c_ctx': _jnp.float32, 'w_mod': _jnp.float32, 'b_mod': _jnp.float32, 'g_mix': _jnp.float32, 'w_in': _jnp.float32, 'b_gate': _jnp.float32, 'sink': _jnp.float32, 'w_spatial': _jnp.float32, 'b_spatial': _jnp.float32, 'g_v': _jnp.float32, 'w_sconv': _jnp.float32, 'w_branch': _jnp.float32, 'w_out': _jnp.float32, 'g_ffn': _jnp.float32, 'w_up': _jnp.float32, 'w_fconv': _jnp.float32, 'w_down': _jnp.float32, 'g_final': _jnp.float32}
MOMENT_SCALE = {'c_ctx': 4.218731e-02, 'w_mod': 1.344263e-01, 'b_mod': 2.387470e-01, 'g_mix': 2.937131e-01, 'w_in': 9.691065e-02, 'b_gate': 3.595141e-02, 'sink': 4.458123e-04, 'w_spatial': 5.923405e-02, 'b_spatial': 6.004313e-02, 'g_v': 5.968314e-02, 'w_sconv': 1.550400e-01, 'w_branch': 1.019829e-01, 'w_out': 1.767157e-01, 'g_ffn': 1.455584e-01, 'w_up': 6.910860e-02, 'w_fconv': 7.138975e-02, 'w_down': 1.127296e-01, 'g_final': 6.520804e+01}


def _to_microbatches(a, axis):
    t = _jnp.moveaxis(a, axis, 0)
    t = t.reshape((N_MICROBATCH, t.shape[0] // N_MICROBATCH) + t.shape[1:])
    return _jnp.moveaxis(t, 1, axis + 1)


def setup_inputs(seed: int = 0) -> dict:
    inp = _fwd_setup_inputs(seed)
    key = _jax.random.fold_in(_jax.random.key(seed), 7919)
    shape, _ = _output_shape()
    out = dict(inp)
    out["loss_target"] = _jax.random.normal(_jax.random.fold_in(key, 0), shape, _jnp.float32)
    for i, name in enumerate(TWIN_WEIGHTS):
        w = inp[name].astype(_jnp.float32)
        if MOMENT_SCALE is None:
            s = _jnp.sqrt(_jnp.mean(_jnp.square(w)) + 1e-30)
        else:
            s = MOMENT_SCALE[name]
        km, kv = _jax.random.split(_jax.random.fold_in(key, i + 1))
        out[name] = w
        out["m_" + name] = s * _jax.random.normal(km, w.shape, _jnp.float32)
        out["v_" + name] = (s * s) * _jax.random.uniform(kv, w.shape, _jnp.float32, 0.5, 1.5)
    if N_MICROBATCH > 1:
        for name, axis in PER_EXAMPLE_BATCH_AXIS.items():
            out[name] = _to_microbatches(out[name], axis)
    return {'x': out['x'], 'c': out['c'], 'ctx': out['ctx'], 'c_ctx': out['c_ctx'], 'w_mod': out['w_mod'], 'b_mod': out['b_mod'], 'g_mix': out['g_mix'], 'w_in': out['w_in'], 'b_gate': out['b_gate'], 'sink': out['sink'], 'w_spatial': out['w_spatial'], 'b_spatial': out['b_spatial'], 'g_v': out['g_v'], 'w_sconv': out['w_sconv'], 'w_branch': out['w_branch'], 'w_out': out['w_out'], 'g_ffn': out['g_ffn'], 'w_up': out['w_up'], 'w_fconv': out['w_fconv'], 'w_down': out['w_down'], 'g_final': out['g_final'], 'loss_target': out['loss_target'], 'm_c_ctx': out['m_c_ctx'], 'm_w_mod': out['m_w_mod'], 'm_b_mod': out['m_b_mod'], 'm_g_mix': out['m_g_mix'], 'm_w_in': out['m_w_in'], 'm_b_gate': out['m_b_gate'], 'm_sink': out['m_sink'], 'm_w_spatial': out['m_w_spatial'], 'm_b_spatial': out['m_b_spatial'], 'm_g_v': out['m_g_v'], 'm_w_sconv': out['m_w_sconv'], 'm_w_branch': out['m_w_branch'], 'm_w_out': out['m_w_out'], 'm_g_ffn': out['m_g_ffn'], 'm_w_up': out['m_w_up'], 'm_w_fconv': out['m_w_fconv'], 'm_w_down': out['m_w_down'], 'm_g_final': out['m_g_final'], 'v_c_ctx': out['v_c_ctx'], 'v_w_mod': out['v_w_mod'], 'v_b_mod': out['v_b_mod'], 'v_g_mix': out['v_g_mix'], 'v_w_in': out['v_w_in'], 'v_b_gate': out['v_b_gate'], 'v_sink': out['v_sink'], 'v_w_spatial': out['v_w_spatial'], 'v_b_spatial': out['v_b_spatial'], 'v_g_v': out['v_g_v'], 'v_w_sconv': out['v_w_sconv'], 'v_w_branch': out['v_w_branch'], 'v_w_out': out['v_w_out'], 'v_g_ffn': out['v_g_ffn'], 'v_w_up': out['v_w_up'], 'v_w_fconv': out['v_w_fconv'], 'v_w_down': out['v_w_down'], 'v_g_final': out['v_g_final']}


def _loss(weights, diff, rest, loss_target):
    with _jax.named_scope("forward"):
        args = {**rest, TWIN_DIFF_INPUT: diff, **{k: w.astype(_WEIGHT_DTYPES[k]) for k, w in weights.items()}}
        y = _forward(args)
    with _jax.named_scope("loss_head"):
        err = _jnp.square(y.astype(_jnp.float32) - loss_target)
        return 0.5 * _jnp.sum(_jnp.mean(err, axis=-1)) if err.ndim else 0.5 * err


def _adamw(w, g, m, v):
    m = ADAM_B1 * m + (1.0 - ADAM_B1) * g
    v = ADAM_B2 * v + (1.0 - ADAM_B2) * _jnp.square(g)
    m_hat = m / (1.0 - ADAM_B1 ** ADAM_STEP)
    v_hat = v / (1.0 - ADAM_B2 ** ADAM_STEP)
    delta = -ADAM_LR * (m_hat / (_jnp.sqrt(v_hat) + ADAM_EPS) + ADAM_WD * w)
    return delta, m, v


def reference(x, c, ctx, c_ctx, w_mod, b_mod, g_mix, w_in, b_gate, sink, w_spatial, b_spatial, g_v, w_sconv, w_branch, w_out, g_ffn, w_up, w_fconv, w_down, g_final, loss_target, m_c_ctx, m_w_mod, m_b_mod, m_g_mix, m_w_in, m_b_gate, m_sink, m_w_spatial, m_b_spatial, m_g_v, m_w_sconv, m_w_branch, m_w_out, m_g_ffn, m_w_up, m_w_fconv, m_w_down, m_g_final, v_c_ctx, v_w_mod, v_b_mod, v_g_mix, v_w_in, v_b_gate, v_sink, v_w_spatial, v_b_spatial, v_g_v, v_w_sconv, v_w_branch, v_w_out, v_g_ffn, v_w_up, v_w_fconv, v_w_down, v_g_final):
    given = dict(x=x, c=c, ctx=ctx, c_ctx=c_ctx, w_mod=w_mod, b_mod=b_mod, g_mix=g_mix, w_in=w_in, b_gate=b_gate, sink=sink, w_spatial=w_spatial, b_spatial=b_spatial, g_v=g_v, w_sconv=w_sconv, w_branch=w_branch, w_out=w_out, g_ffn=g_ffn, w_up=w_up, w_fconv=w_fconv, w_down=w_down, g_final=g_final, loss_target=loss_target, m_c_ctx=m_c_ctx, m_w_mod=m_w_mod, m_b_mod=m_b_mod, m_g_mix=m_g_mix, m_w_in=m_w_in, m_b_gate=m_b_gate, m_sink=m_sink, m_w_spatial=m_w_spatial, m_b_spatial=m_b_spatial, m_g_v=m_g_v, m_w_sconv=m_w_sconv, m_w_branch=m_w_branch, m_w_out=m_w_out, m_g_ffn=m_g_ffn, m_w_up=m_w_up, m_w_fconv=m_w_fconv, m_w_down=m_w_down, m_g_final=m_g_final, v_c_ctx=v_c_ctx, v_w_mod=v_w_mod, v_b_mod=v_b_mod, v_g_mix=v_g_mix, v_w_in=v_w_in, v_b_gate=v_b_gate, v_sink=v_sink, v_w_spatial=v_w_spatial, v_b_spatial=v_b_spatial, v_g_v=v_g_v, v_w_sconv=v_w_sconv, v_w_branch=v_w_branch, v_w_out=v_w_out, v_g_ffn=v_g_ffn, v_w_up=v_w_up, v_w_fconv=v_w_fconv, v_w_down=v_w_down, v_g_final=v_g_final)
    weights = {n: given[n] for n in TWIN_WEIGHTS}
    shared = {n: given[n] for n in SHARED_INPUTS}
    per_example = {n: given[n] for n in ['x', 'c', 'ctx']}
    grad_fn = _jax.value_and_grad(_loss, argnums=(0, 1))

    def one_microbatch(ex, loss_target):
        ex = dict(ex)
        diff = ex.pop(TWIN_DIFF_INPUT)
        return grad_fn(weights, diff, {**shared, **ex}, loss_target)

    if N_MICROBATCH == 1:
        loss, (grad_w, grad_x) = one_microbatch(per_example, given["loss_target"])
    else:
        def body(carry, xs):
            loss_sum, grad_sum = carry
            l_k, (gw_k, gx_k) = one_microbatch(xs[0], xs[1])
            with _jax.named_scope("update"):
                return (loss_sum + l_k, _jax.tree.map(_jnp.add, grad_sum, gw_k)), gx_k

        init = (_jnp.zeros((), _jnp.float32), _jax.tree.map(_jnp.zeros_like, weights))
        (loss, grad_w), grad_x = _jax.lax.scan(body, init, (per_example, given["loss_target"]))
    with _jax.named_scope("update"):
        delta_w, new_m, new_v = {}, {}, {}
        for n in TWIN_WEIGHTS:
            delta_w[n], new_m[n], new_v[n] = _adamw(weights[n], grad_w[n], given["m_" + n], given["v_" + n])
    return (loss, grad_x, *[grad_w[n] for n in TWIN_WEIGHTS], *[delta_w[n] for n in TWIN_WEIGHTS],
            *[new_m[n] for n in TWIN_WEIGHTS], *[new_v[n] for n in TWIN_WEIGHTS])
```

```python
import functools
import math

import jax
import jax.numpy as jnp
from jax import lax
from jax.experimental import pallas as pl
from jax.experimental.pallas import tpu as pltpu

F32 = jnp.float32
BF16 = jnp.bfloat16

D = 1024
DEPTH = 2
GRID_W = 64
N_HEADS = 16
N_KV = 4
GRP = N_HEADS // N_KV
HEAD_DIM = 64
KV_W = N_KV * HEAD_DIM
WINDOW = 128
BLK = 128
ROPE_THETA = 10000.0
A_GROUPS = 8
D_FF = 2816
EPS = 1e-6
NEG = -1e30
QKV_W = D + 2 * KV_W
A_COLS = 2 * D
B_COLS = 3 * D
G_COLS = 3 * D
IN_W = QKV_W + A_COLS + B_COLS + G_COLS
SEG = (0, QKV_W, QKV_W + A_COLS, QKV_W + A_COLS + B_COLS, IN_W)
N_CHIPS = 4
N_DEV = 8
LANES = 128
SUBLANES = 8
VMEM_LIMIT = 48 * 1024 * 1024
ADAM_LR = 0.001
ADAM_B1 = 0.9
ADAM_B2 = 0.999
ADAM_EPS = 1e-08
ADAM_WD = 0.01
ADAM_STEP = 10
MESH = pl.DeviceIdType.MESH
ANY = pl.BlockSpec(memory_space=pl.ANY)


def _params(sem=None):
    return pltpu.CompilerParams(dimension_semantics=sem, vmem_limit_bytes=VMEM_LIMIT)


def _pick(n, cands):
    for c in cands:
        if n % c == 0:
            return c
    return n


def _rows8(rows, width):
    r = lax.broadcasted_iota(jnp.int32, (SUBLANES, width), 0)
    out = jnp.zeros((SUBLANES, width), F32)
    for idx, v in rows:
        out = out + jnp.where(r == idx, v, 0.0)
    return out


def _sel(mod_ref, k, is_ctx):
    return jnp.where(is_ctx, mod_ref[1:2, k * D:(k + 1) * D], mod_ref[0:1, k * D:(k + 1) * D])


def _colsum(v):
    return jnp.sum(v, axis=0, keepdims=True)


def _mm(a, b, *, name, ta=False, tb=False, out_dtype=F32):
    if ta:
        k_dim, m = a.shape
    else:
        m, k_dim = a.shape
    if tb:
        n, kb = b.shape
    else:
        kb, n = b.shape
    assert k_dim == kb, (a.shape, b.shape, ta, tb)
    tm = _pick(m, (768, 512, 256, 128))
    tn = _pick(n, (1024, 768, 512, 1408, 256, 128))
    tk = _pick(k_dim, (1024, 768, 512, 1408, 256, 128))
    nk = k_dim // tk
    dims = (((0 if ta else 1,), (1 if tb else 0,)), ((), ()))

    def body(a_ref, b_ref, o_ref, acc_ref):
        k = pl.program_id(2)

        @pl.when(k == 0)
        def _():
            acc_ref[...] = jnp.zeros_like(acc_ref)

        acc_ref[...] += lax.dot_general(a_ref[...].astype(BF16), b_ref[...].astype(BF16), dims,
                                        preferred_element_type=F32)

        @pl.when(k == nk - 1)
        def _():
            o_ref[...] = acc_ref[...].astype(o_ref.dtype)

    a_spec = pl.BlockSpec((tk, tm), lambda i, j, k: (k, i)) if ta else pl.BlockSpec((tm, tk), lambda i, j, k: (i, k))
    b_spec = pl.BlockSpec((tn, tk), lambda i, j, k: (j, k)) if tb else pl.BlockSpec((tk, tn), lambda i, j, k: (k, j))
    return pl.pallas_call(
        body, name=name, grid=(m // tm, n // tn, nk),
        in_specs=[a_spec, b_spec], out_specs=pl.BlockSpec((tm, tn), lambda i, j, k: (i, j)),
        out_shape=jax.ShapeDtypeStruct((m, n), out_dtype),
        scratch_shapes=[pltpu.VMEM((tm, tn), F32)],
        compiler_params=_params(("parallel", "parallel", "arbitrary")),
    )(a, b)


def _small(fn, out_shape, *arrays, name):
    def body(*refs):
        refs[-1][...] = fn(*[r[...] for r in refs[:-1]]).astype(refs[-1].dtype)

    return pl.pallas_call(body, name=name, out_shape=jax.ShapeDtypeStruct(out_shape, F32))(*arrays)


def _silu(v):
    return v * jax.nn.sigmoid(v)


def _dsilu(v):
    s = jax.nn.sigmoid(v)
    return s * (1.0 + v * (1.0 - s))


def _row_spec(tm, width, col=0):
    return pl.BlockSpec((tm, width), lambda i: (i, col))


def _full_spec(shape):
    nd = len(shape)
    return pl.BlockSpec(shape, lambda i: (0,) * nd)


def _halo_specs(tm, width, t_rows, col=0):
    per = tm // SUBLANES
    last = t_rows // SUBLANES - 1
    prev = pl.BlockSpec((SUBLANES, width), lambda i: (jnp.maximum(i * per - 1, 0), col))
    nxt = pl.BlockSpec((SUBLANES, width), lambda i: (jnp.minimum((i + 1) * per, last), col))
    return prev, nxt


def _shift_rows(cur, prev8, next8, n_lat, t_rows, tm):
    i = pl.program_id(0)
    row = lax.broadcasted_iota(jnp.int32, (tm, 1), 0)
    g = row + i * tm
    up = pltpu.roll(cur, 1, 0)
    up = jnp.where(row == 0, prev8[SUBLANES - 1:SUBLANES, :], up)
    up = jnp.where((g == 0) | (g == n_lat), 0.0, up)
    dn = pltpu.roll(cur, tm - 1, 0)
    dn = jnp.where(row == tm - 1, next8[0:1, :], dn)
    dn = jnp.where((g == n_lat - 1) | (g == t_rows - 1), 0.0, dn)
    return up, dn


def _norm_mod_fwd(x, g, mod8, sh_idx, sc_idx, n_lat, *, name):
    t_rows = x.shape[0]
    tm = 256

    def body(x_ref, g_ref, mod_ref, o_ref):
        is_ctx = pl.program_id(0) * tm >= n_lat
        xv = x_ref[...]
        rstd = lax.rsqrt(jnp.mean(xv * xv, axis=-1, keepdims=True) + EPS)
        y = xv * rstd * g_ref[...]
        o_ref[...] = (y * (1.0 + _sel(mod_ref, sc_idx, is_ctx)) + _sel(mod_ref, sh_idx, is_ctx)).astype(BF16)

    return pl.pallas_call(
        body, name=name, grid=(t_rows // tm,),
        in_specs=[_row_spec(tm, D), _full_spec((1, D)), _full_spec((SUBLANES, 6 * D))],
        out_specs=_row_spec(tm, D), out_shape=jax.ShapeDtypeStruct((t_rows, D), BF16),
        compiler_params=_params(("parallel",)),
    )(x, g, mod8)


def _norm_mod_bwd(x, dh_parts, dres, g, mod8, sc_idx, n_lat, *, name):
    t_rows = x.shape[0]
    tm = 256
    n_parts = len(dh_parts)

    def body(*refs):
        x_ref, dres_ref, g_ref, mod_ref = refs[:4]
        part_refs = refs[4:4 + n_parts]
        dx_ref, st_ref = refs[4 + n_parts:]
        i = pl.program_id(0)
        is_ctx = i * tm >= n_lat
        dh = part_refs[0][...]
        for p in part_refs[1:]:
            dh = dh + p[...]
        xv = x_ref[...]
        gv = g_ref[...]
        rstd = lax.rsqrt(jnp.mean(xv * xv, axis=-1, keepdims=True) + EPS)
        rn = xv * rstd
        dy = dh * (1.0 + _sel(mod_ref, sc_idx, is_ctx))
        e = dy * gv
        dx_ref[...] = dres_ref[...] + rstd * (e - rn * jnp.mean(e * rn, axis=-1, keepdims=True))
        dsh = _colsum(dh)
        dsc = _colsum(dh * (rn * gv))
        dg = _colsum(dy * rn)
        zero = jnp.zeros_like(dsh)
        upd = _rows8([(0, jnp.where(is_ctx, zero, dsh)), (1, jnp.where(is_ctx, dsh, zero)),
                      (2, jnp.where(is_ctx, zero, dsc)), (3, jnp.where(is_ctx, dsc, zero)), (4, dg)], D)

        @pl.when(i == 0)
        def _():
            st_ref[...] = upd

        @pl.when(i > 0)
        def _():
            st_ref[...] += upd

    return pl.pallas_call(
        body, name=name, grid=(t_rows // tm,),
        in_specs=[_row_spec(tm, D), _row_spec(tm, D), _full_spec((1, D)), _full_spec((SUBLANES, 6 * D))]
        + [_row_spec(tm, D)] * n_parts,
        out_specs=[_row_spec(tm, D), _full_spec((SUBLANES, D))],
        out_shape=[jax.ShapeDtypeStruct((t_rows, D), F32), jax.ShapeDtypeStruct((SUBLANES, D), F32)],
        compiler_params=_params(("arbitrary",)),
    )(x, dres, g, mod8, *dh_parts)


def _residual_fwd(x, branch, mod8, gt_idx, n_lat, *, name):
    t_rows = x.shape[0]
    tm = 256

    def body(x_ref, b_ref, mod_ref, o_ref):
        is_ctx = pl.program_id(0) * tm >= n_lat
        o_ref[...] = x_ref[...] + _sel(mod_ref, gt_idx, is_ctx) * b_ref[...]

    return pl.pallas_call(
        body, name=name, grid=(t_rows // tm,),
        in_specs=[_row_spec(tm, D), _row_spec(tm, D), _full_spec((SUBLANES, 6 * D))],
        out_specs=_row_spec(tm, D), out_shape=jax.ShapeDtypeStruct((t_rows, D), F32),
        compiler_params=_params(("parallel",)),
    )(x, branch, mod8)


def _residual_bwd(dx, branch, mod8, gt_idx, n_lat, *, name):
    t_rows = dx.shape[0]
    tm = 256

    def body(dx_ref, b_ref, mod_ref, o_ref, st_ref):
        i = pl.program_id(0)
        is_ctx = i * tm >= n_lat
        dxv = dx_ref[...]
        o_ref[...] = (dxv * _sel(mod_ref, gt_idx, is_ctx)).astype(BF16)
        dgt = _colsum(dxv * b_ref[...])
        zero = jnp.zeros_like(dgt)
        upd = _rows8([(0, jnp.where(is_ctx, zero, dgt)), (1, jnp.where(is_ctx, dgt, zero))], D)

        @pl.when(i == 0)
        def _():
            st_ref[...] = upd

        @pl.when(i > 0)
        def _():
            st_ref[...] += upd

    return pl.pallas_call(
        body, name=name, grid=(t_rows // tm,),
        in_specs=[_row_spec(tm, D), _row_spec(tm, D), _full_spec((SUBLANES, 6 * D))],
        out_specs=[_row_spec(tm, D), _full_spec((SUBLANES, D))],
        out_shape=[jax.ShapeDtypeStruct((t_rows, D), BF16), jax.ShapeDtypeStruct((SUBLANES, D), F32)],
        compiler_params=_params(("arbitrary",)),
    )(dx, branch, mod8)


def _rope_tables(n_lat, n_ctx):
    rows = n_lat // GRID_W
    row = jnp.broadcast_to(jnp.arange(rows, dtype=F32)[:, None], (rows, GRID_W)).reshape(n_lat)
    col = jnp.broadcast_to(jnp.arange(GRID_W, dtype=F32)[None, :], (rows, GRID_W)).reshape(n_lat)
    half = HEAD_DIM // 2
    inv = ROPE_THETA ** (-jnp.arange(0, half, 2, dtype=F32) / half)
    ang = jnp.concatenate([row[:, None] * inv, col[:, None] * inv], axis=-1)
    cos, sin = jnp.cos(ang), jnp.sin(ang)
    c64 = jnp.concatenate([cos, cos], axis=-1)
    s64 = jnp.concatenate([-sin, sin], axis=-1)
    c64 = jnp.concatenate([c64, jnp.ones((n_ctx, HEAD_DIM), F32)], axis=0)
    s64 = jnp.concatenate([s64, jnp.zeros((n_ctx, HEAD_DIM), F32)], axis=0)
    return jnp.tile(c64, (1, 2)), jnp.tile(s64, (1, 2))


def _swap_halves(v):
    lane = lax.broadcasted_iota(jnp.int32, v.shape, 1)
    return jnp.where(lane % HEAD_DIM < HEAD_DIM // 2, pltpu.roll(v, LANES - HEAD_DIM // 2, 1),
                     pltpu.roll(v, HEAD_DIM // 2, 1))


def _low_half(shape):
    return lax.broadcasted_iota(jnp.int32, shape, 1) < HEAD_DIM


def _qkv_prep(z_qkv, cos_t, sin_t, *, name):
    t_rows = z_qkv.shape[0]
    tm = 256

    def body(z_ref, c_ref, s_ref, q_ref, k_ref, v_ref):
        cv, sv = c_ref[...], s_ref[...]

        def rope(chunk):
            return chunk * cv + _swap_halves(chunk) * sv

        for ch in range(D // LANES):
            q_ref[:, ch * LANES:(ch + 1) * LANES] = rope(z_ref[:, ch * LANES:(ch + 1) * LANES]).astype(BF16)
        low = _low_half((tm, LANES))
        for pair in range(N_KV // 2):
            for which, ref, roped in ((0, k_ref, True), (1, v_ref, False)):
                off = D + which * KV_W + pair * LANES
                chunk = z_ref[:, off:off + LANES]
                if roped:
                    chunk = rope(chunk)
                other = pltpu.roll(chunk, HEAD_DIM, 1)
                even = jnp.where(low, chunk, other)
                odd = jnp.where(low, other, chunk)
                ref[:, (2 * pair) * LANES:(2 * pair + 1) * LANES] = even.astype(BF16)
                ref[:, (2 * pair + 1) * LANES:(2 * pair + 2) * LANES] = odd.astype(BF16)

    dup_w = N_KV * LANES
    return pl.pallas_call(
        body, name=name, grid=(t_rows // tm,),
        in_specs=[_row_spec(tm, QKV_W), _row_spec(tm, LANES), _row_spec(tm, LANES)],
        out_specs=[_row_spec(tm, D), _row_spec(tm, dup_w), _row_spec(tm, dup_w)],
        out_shape=[jax.ShapeDtypeStruct((t_rows, D), BF16), jax.ShapeDtypeStruct((t_rows, dup_w), BF16),
                   jax.ShapeDtypeStruct((t_rows, dup_w), BF16)],
        compiler_params=_params(("parallel",)),
    )(z_qkv, cos_t, sin_t)


def _qkv_unprep(dq, dk, dv, cos_t, sin_t, *, name):
    t_rows = dq.shape[0]
    tm = 256

    def body(dq_ref, dk_ref, dv_ref, c_ref, s_ref, o_ref):
        cv, sv = c_ref[...], s_ref[...]

        def unrope(chunk):
            return chunk * cv + _swap_halves(chunk * sv)

        for ch in range(D // LANES):
            o_ref[:, ch * LANES:(ch + 1) * LANES] = unrope(dq_ref[:, ch * LANES:(ch + 1) * LANES]).astype(BF16)
        for pair in range(N_KV // 2):
            for which, ref, roped in ((0, dk_ref, True), (1, dv_ref, False)):
                chunk = ref[:, pair * LANES:(pair + 1) * LANES]
                if roped:
                    chunk = unrope(chunk)
                off = D + which * KV_W + pair * LANES
                o_ref[:, off:off + LANES] = chunk.astype(BF16)

    return pl.pallas_call(
        body, name=name, grid=(t_rows // tm,),
        in_specs=[_row_spec(tm, D), _row_spec(tm, KV_W), _row_spec(tm, KV_W), _row_spec(tm, LANES),
                  _row_spec(tm, LANES)],
        out_specs=_row_spec(tm, QKV_W), out_shape=jax.ShapeDtypeStruct((t_rows, QKV_W), BF16),
        compiler_params=_params(("parallel",)),
    )(dq, dk, dv, cos_t, sin_t)


def _attn_specs(n_lat, n_ctx):
    nb = n_lat // BLK
    dup_w = N_KV * LANES

    def ws(j):
        return jnp.clip(j - 1, 0, nb - 3)

    win = [pl.BlockSpec((BLK, dup_w), functools.partial(lambda j, o: (ws(j) + o, 0), o=o)) for o in range(3)]
    ctx = pl.BlockSpec((n_ctx, dup_w), lambda j: (n_lat // n_ctx, 0))
    return nb, ws, win, ctx


def _attn_probs(q_ref, kk, kh, j, ws_j, nb, n_ctx, sink_ref):
    low = _low_half((BLK, LANES))
    qs = []
    for g in range(GRP):
        h = GRP * kh + g
        chunk = q_ref[:, (h // 2) * LANES:(h // 2 + 1) * LANES]
        qs.append(jnp.where(low if h % 2 == 0 else ~low, chunk, jnp.zeros_like(chunk)))
    qs = jnp.concatenate(qs, axis=0)
    s = lax.dot_general(qs, kk, (((1,), (1,)), ((), ())), preferred_element_type=F32) * (HEAD_DIM ** -0.5)
    n_keys = 3 * BLK + n_ctx
    row = lax.broadcasted_iota(jnp.int32, (GRP * BLK, n_keys), 0) % BLK
    col = lax.broadcasted_iota(jnp.int32, (GRP * BLK, n_keys), 1)
    rel = (ws_j - j) * BLK + col - row
    valid = (col >= 3 * BLK) | ((jnp.abs(rel) <= WINDOW) & (j < nb))
    s = jnp.where(valid, s, NEG)
    snk = jnp.concatenate(
        [jnp.broadcast_to(jnp.max(sink_ref[GRP * kh + g:GRP * kh + g + 1, :], axis=1, keepdims=True), (BLK, 1))
         for g in range(GRP)], axis=0)
    m = jnp.maximum(jnp.max(s, axis=-1, keepdims=True), snk)
    p = jnp.exp(s - m)
    p_snk = jnp.exp(snk - m)
    inv = 1.0 / (jnp.sum(p, axis=-1, keepdims=True) + p_snk)
    return qs, p * inv, p_snk * inv


def _attention_fwd(q, kd, vd, sink_b, n_lat, n_ctx, *, name):
    t_rows = q.shape[0]
    nb, ws, win, ctx = _attn_specs(n_lat, n_ctx)

    def body(q_ref, k0, k1, k2, kc, v0, v1, v2, vc, sink_ref, o_ref):
        j = pl.program_id(0)
        ws_j = ws(j)
        low = _low_half((BLK, LANES))
        for kh in range(N_KV):
            sl = slice(kh * LANES, (kh + 1) * LANES)
            kk = jnp.concatenate([k0[:, sl], k1[:, sl], k2[:, sl], kc[:, sl]], axis=0)
            vv = jnp.concatenate([v0[:, sl], v1[:, sl], v2[:, sl], vc[:, sl]], axis=0)
            _, p, _ = _attn_probs(q_ref, kk, kh, j, ws_j, nb, n_ctx, sink_ref)
            o = jnp.dot(p.astype(BF16), vv, preferred_element_type=F32)
            for half in range(2):
                even = o[(2 * half) * BLK:(2 * half + 1) * BLK]
                odd = o[(2 * half + 1) * BLK:(2 * half + 2) * BLK]
                ch = 2 * kh + half
                o_ref[:, ch * LANES:(ch + 1) * LANES] = jnp.where(low, even, odd).astype(BF16)

    return pl.pallas_call(
        body, name=name, grid=(t_rows // BLK,),
        in_specs=[_row_spec(BLK, D)] + win + [ctx] + win + [ctx] + [_full_spec((N_HEADS, LANES))],
        out_specs=_row_spec(BLK, D), out_shape=jax.ShapeDtypeStruct((t_rows, D), BF16),
        compiler_params=_params(("parallel",)),
    )(q, kd, kd, kd, kd, vd, vd, vd, vd, sink_b)


def _attention_bwd(q, kd, vd, sink_b, dy, n_lat, n_ctx, *, name):
    t_rows = q.shape[0]
    nb, ws, win, ctx = _attn_specs(n_lat, n_ctx)
    n_steps = t_rows // BLK

    def body(q_ref, k0, k1, k2, kc, v0, v1, v2, vc, sink_ref, dy_ref, dq_ref, dk_hbm, dv_hbm, ds_ref,
             dk_acc, dv_acc, sem):
        j = pl.program_id(0)
        ws_j = ws(j)

        @pl.when(j == 0)
        def _():
            dk_acc[...] = jnp.zeros_like(dk_acc)
            dv_acc[...] = jnp.zeros_like(dv_acc)
            ds_ref[...] = jnp.zeros_like(ds_ref)

        low = _low_half((BLK, LANES))
        low_keys = _low_half((3 * BLK + n_ctx, LANES))
        win_start = pl.multiple_of(ws_j * BLK, BLK)
        scale = HEAD_DIM ** -0.5
        dk_heads, dv_heads = [], []
        for kh in range(N_KV):
            sl = slice(kh * LANES, (kh + 1) * LANES)
            kk = jnp.concatenate([k0[:, sl], k1[:, sl], k2[:, sl], kc[:, sl]], axis=0)
            vv = jnp.concatenate([v0[:, sl], v1[:, sl], v2[:, sl], vc[:, sl]], axis=0)
            qs, p, p_snk = _attn_probs(q_ref, kk, kh, j, ws_j, nb, n_ctx, sink_ref)
            dos = []
            for g in range(GRP):
                h = GRP * kh + g
                chunk = dy_ref[:, (h // 2) * LANES:(h // 2 + 1) * LANES]
                dos.append(jnp.where(low if h % 2 == 0 else ~low, chunk, jnp.zeros_like(chunk)).astype(BF16))
            dos = jnp.concatenate(dos, axis=0)
            dp = lax.dot_general(dos, vv, (((1,), (1,)), ((), ())), preferred_element_type=F32)
            dsum = jnp.sum(p * dp, axis=-1, keepdims=True)
            ds = (p * (dp - dsum)).astype(BF16)
            for g in range(GRP):
                contrib = -jnp.sum(p_snk[g * BLK:(g + 1) * BLK] * dsum[g * BLK:(g + 1) * BLK], axis=0, keepdims=True)
                ds_ref[GRP * kh + g:GRP * kh + g + 1, :] += jnp.broadcast_to(contrib, (1, LANES))
            dqs = jnp.dot(ds, kk, preferred_element_type=F32) * scale
            for half in range(2):
                even = dqs[(2 * half) * BLK:(2 * half + 1) * BLK]
                odd = dqs[(2 * half + 1) * BLK:(2 * half + 2) * BLK]
                ch = 2 * kh + half
                dq_ref[:, ch * LANES:(ch + 1) * LANES] = jnp.where(low, even, odd)
            dkk = lax.dot_general(ds, qs, (((0,), (0,)), ((), ())), preferred_element_type=F32) * scale
            dvv = lax.dot_general(p.astype(BF16), dos, (((0,), (0,)), ((), ())), preferred_element_type=F32)
            dk_heads.append(dkk + pltpu.roll(dkk, HEAD_DIM, 1))
            dv_heads.append(dvv + pltpu.roll(dvv, HEAD_DIM, 1))
        for pair in range(N_KV // 2):
            sl = slice(pair * LANES, (pair + 1) * LANES)
            for acc, heads in ((dk_acc, dk_heads), (dv_acc, dv_heads)):
                both = jnp.where(low_keys, heads[2 * pair], heads[2 * pair + 1])
                acc[pl.ds(win_start, 3 * BLK), sl] += both[:3 * BLK]
                acc[n_lat:n_lat + n_ctx, sl] += both[3 * BLK:]

        @pl.when(j == n_steps - 1)
        def _():
            ck = pltpu.make_async_copy(dk_acc, dk_hbm, sem.at[0])
            cv = pltpu.make_async_copy(dv_acc, dv_hbm, sem.at[1])
            ck.start()
            cv.start()
            ck.wait()
            cv.wait()

    return pl.pallas_call(
        body, name=name, grid=(n_steps,),
        in_specs=[_row_spec(BLK, D)] + win + [ctx] + win + [ctx] + [_full_spec((N_HEADS, LANES)), _row_spec(BLK, D)],
        out_specs=[_row_spec(BLK, D), ANY, ANY, _full_spec((N_HEADS, LANES))],
        out_shape=[jax.ShapeDtypeStruct((t_rows, D), F32), jax.ShapeDtypeStruct((t_rows, KV_W), F32),
                   jax.ShapeDtypeStruct((t_rows, KV_W), F32), jax.ShapeDtypeStruct((N_HEADS, LANES), F32)],
        scratch_shapes=[pltpu.VMEM((t_rows, KV_W), F32), pltpu.VMEM((t_rows, KV_W), F32),
                        pltpu.SemaphoreType.DMA((2,))],
        compiler_params=_params(("arbitrary",)),
    )(q, kd, kd, kd, kd, vd, vd, vd, vd, sink_b, dy)


_GELU_K = math.sqrt(2.0 / math.pi)


def _gelu(v):
    return jax.nn.gelu(v)


def _dgelu(v):
    t = jnp.tanh(_GELU_K * (v + 0.044715 * v * v * v))
    return 0.5 * (1.0 + t) + 0.5 * v * (1.0 - t * t) * _GELU_K * (1.0 + 3.0 * 0.044715 * v * v)


def _gating_fwd(z_a, w_s, b_sb, g_v, *, name):
    t_rows = z_a.shape[0]

    def body(z_ref, w_ref, b_ref, g_ref, o_ref):
        u = _gelu(z_ref[:, :D])
        v = _gelu(z_ref[:, D:])
        vn = v * lax.rsqrt(jnp.mean(v * v, axis=-1, keepdims=True) + EPS) * g_ref[...]
        for g in range(A_GROUPS):
            sl = slice(g * LANES, (g + 1) * LANES)
            mixed = jnp.dot(w_ref[g].astype(BF16), vn[:, sl].astype(BF16), preferred_element_type=F32) + b_ref[g]
            o_ref[:, sl] = (u[:, sl] * mixed).astype(BF16)

    return pl.pallas_call(
        body, name=name, grid=(t_rows // BLK,),
        in_specs=[_row_spec(BLK, A_COLS), _full_spec((A_GROUPS, BLK, BLK)), _full_spec((A_GROUPS, BLK, LANES)),
                  _full_spec((1, D))],
        out_specs=_row_spec(BLK, D), out_shape=jax.ShapeDtypeStruct((t_rows, D), BF16),
        compiler_params=_params(("parallel",)),
    )(z_a, w_s, b_sb, g_v)


def _gating_bwd(z_a, dy, w_s, b_sb, g_v, *, name):
    t_rows = z_a.shape[0]

    def body(z_ref, dy_ref, w_ref, b_ref, g_ref, dz_ref, dw_ref, db_ref, st_ref):
        i = pl.program_id(0)

        @pl.when(i == 0)
        def _():
            dw_ref[...] = jnp.zeros_like(dw_ref)
            db_ref[...] = jnp.zeros_like(db_ref)
            st_ref[...] = jnp.zeros_like(st_ref)

        zu = z_ref[:, :D]
        zv = z_ref[:, D:]
        u = _gelu(zu)
        v = _gelu(zv)
        gv = g_ref[...]
        rstd = lax.rsqrt(jnp.mean(v * v, axis=-1, keepdims=True) + EPS)
        vh = v * rstd
        vn = vh * gv
        dyv = dy_ref[...]
        dvn = []
        for g in range(A_GROUPS):
            sl = slice(g * LANES, (g + 1) * LANES)
            wg = w_ref[g].astype(BF16)
            vg = vn[:, sl].astype(BF16)
            mixed = jnp.dot(wg, vg, preferred_element_type=F32) + b_ref[g]
            dz_ref[:, sl] = (dyv[:, sl] * mixed * _dgelu(zu[:, sl])).astype(BF16)
            dmixed = dyv[:, sl] * u[:, sl]
            dmb = dmixed.astype(BF16)
            dvn.append(lax.dot_general(wg, dmb, (((0,), (0,)), ((), ())), preferred_element_type=F32))
            dw_ref[g] += lax.dot_general(dmb, vg, (((1,), (1,)), ((), ())), preferred_element_type=F32)
            db_ref[g] += jnp.broadcast_to(jnp.sum(dmixed, axis=-1, keepdims=True), (BLK, LANES))
        dvn = jnp.concatenate(dvn, axis=1)
        st_ref[...] += _rows8([(0, _colsum(dvn * vh))], D)
        e = dvn * gv
        dv = rstd * (e - vh * jnp.mean(e * vh, axis=-1, keepdims=True))
        dz_ref[:, D:] = (dv * _dgelu(zv)).astype(BF16)

    return pl.pallas_call(
        body, name=name, grid=(t_rows // BLK,),
        in_specs=[_row_spec(BLK, A_COLS), _row_spec(BLK, D), _full_spec((A_GROUPS, BLK, BLK)),
                  _full_spec((A_GROUPS, BLK, LANES)), _full_spec((1, D))],
        out_specs=[_row_spec(BLK, A_COLS), _full_spec((A_GROUPS, BLK, BLK)), _full_spec((A_GROUPS, BLK, LANES)),
                   _full_spec((SUBLANES, D))],
        out_shape=[jax.ShapeDtypeStruct((t_rows, A_COLS), BF16), jax.ShapeDtypeStruct((A_GROUPS, BLK, BLK), F32),
                   jax.ShapeDtypeStruct((A_GROUPS, BLK, LANES), F32), jax.ShapeDtypeStruct((SUBLANES, D), F32)],
        compiler_params=_params(("arbitrary",)),
    )(z_a, dy, w_s, b_sb, g_v)


def _sconv_fwd(z_b, w8, n_lat, *, name):
    t_rows = z_b.shape[0]
    tm = 256
    prev, nxt = _halo_specs(tm, B_COLS, t_rows)

    def body(z_ref, zp_ref, zn_ref, w_ref, o_ref):
        p = z_ref[:, D:2 * D] * z_ref[:, 2 * D:]
        pp = zp_ref[:, D:2 * D] * zp_ref[:, 2 * D:]
        pn = zn_ref[:, D:2 * D] * zn_ref[:, 2 * D:]
        up, dn = _shift_rows(p, pp, pn, n_lat, t_rows, tm)
        conv = w_ref[0:1, :] * up + w_ref[1:2, :] * p + w_ref[2:3, :] * dn
        o_ref[...] = (z_ref[:, :D] * conv).astype(BF16)

    return pl.pallas_call(
        body, name=name, grid=(t_rows // tm,),
        in_specs=[_row_spec(tm, B_COLS), prev, nxt, _full_spec((SUBLANES, D))],
        out_specs=_row_spec(tm, D), out_shape=jax.ShapeDtypeStruct((t_rows, D), BF16),
        compiler_params=_params(("parallel",)),
    )(z_b, z_b, z_b, w8)


def _sconv_bwd(z_b, dy, w8, n_lat, *, name):
    t_rows = z_b.shape[0]
    tm = 256
    prev, nxt = _halo_specs(tm, B_COLS, t_rows)
    dprev, dnxt = _halo_specs(tm, D, t_rows)

    def body(z_ref, zp_ref, zn_ref, dy_ref, dyp_ref, dyn_ref, w_ref, dz_ref, st_ref):
        i = pl.program_id(0)
        bg, cg, hb = z_ref[:, :D], z_ref[:, D:2 * D], z_ref[:, 2 * D:]
        p = cg * hb
        pp = zp_ref[:, D:2 * D] * zp_ref[:, 2 * D:]
        pn = zn_ref[:, D:2 * D] * zn_ref[:, 2 * D:]
        up, dn = _shift_rows(p, pp, pn, n_lat, t_rows, tm)
        w0, w1, w2 = w_ref[0:1, :], w_ref[1:2, :], w_ref[2:3, :]
        conv = w0 * up + w1 * p + w2 * dn
        dyv = dy_ref[...]
        dz_ref[:, :D] = (dyv * conv).astype(BF16)
        dcv = dyv * bg
        dcv_up, dcv_dn = _shift_rows(dcv, dyp_ref[...] * zp_ref[:, :D], dyn_ref[...] * zn_ref[:, :D], n_lat, t_rows, tm)
        dp = w0 * dcv_dn + w1 * dcv + w2 * dcv_up
        dz_ref[:, D:2 * D] = (dp * hb).astype(BF16)
        dz_ref[:, 2 * D:] = (dp * cg).astype(BF16)
        upd = _rows8([(0, _colsum(dcv * up)), (1, _colsum(dcv * p)), (2, _colsum(dcv * dn))], D)

        @pl.when(i == 0)
        def _():
            st_ref[...] = upd

        @pl.when(i > 0)
        def _():
            st_ref[...] += upd

    return pl.pallas_call(
        body, name=name, grid=(t_rows // tm,),
        in_specs=[_row_spec(tm, B_COLS), prev, nxt, _row_spec(tm, D), dprev, dnxt, _full_spec((SUBLANES, D))],
        out_specs=[_row_spec(tm, B_COLS), _full_spec((SUBLANES, D))],
        out_shape=[jax.ShapeDtypeStruct((t_rows, B_COLS), BF16), jax.ShapeDtypeStruct((SUBLANES, D), F32)],
        compiler_params=_params(("arbitrary",)),
    )(z_b, z_b, z_b, dy, dy, dy, w8)


def _merge_fwd(t0, t1, t2, z_g, b_gate, *, name):
    t_rows = t0.shape[0]
    tm = 256

    def body(t0_ref, t1_ref, t2_ref, z_ref, b_ref, o_ref):
        acc = None
        for k, t_ref in enumerate((t0_ref, t1_ref, t2_ref)):
            gate = jax.nn.sigmoid(z_ref[:, k * D:(k + 1) * D] + b_ref[:, k * D:(k + 1) * D])
            term = gate * t_ref[...]
            acc = term if acc is None else acc + term
        o_ref[...] = acc.astype(BF16)

    return pl.pallas_call(
        body, name=name, grid=(t_rows // tm,),
        in_specs=[_row_spec(tm, D)] * 3 + [_row_spec(tm, G_COLS), _full_spec((1, G_COLS))],
        out_specs=_row_spec(tm, D), out_shape=jax.ShapeDtypeStruct((t_rows, D), BF16),
        compiler_params=_params(("parallel",)),
    )(t0, t1, t2, z_g, b_gate)


def _merge_bwd(dmerged, t0, t1, t2, z_g, b_gate, *, name):
    t_rows = t0.shape[0]
    tm = 256

    def body(dm_ref, t0_ref, t1_ref, t2_ref, z_ref, b_ref, d0_ref, d1_ref, d2_ref, dz_ref, st_ref):
        i = pl.program_id(0)
        dm = dm_ref[...]
        sums = []
        for k, (t_ref, d_ref) in enumerate(((t0_ref, d0_ref), (t1_ref, d1_ref), (t2_ref, d2_ref))):
            gate = jax.nn.sigmoid(z_ref[:, k * D:(k + 1) * D] + b_ref[:, k * D:(k + 1) * D])
            d_ref[...] = (dm * gate).astype(BF16)
            dzg = dm * t_ref[...] * gate * (1.0 - gate)
            dz_ref[:, k * D:(k + 1) * D] = dzg.astype(BF16)
            sums.append(_colsum(dzg))
        upd = _rows8([(0, jnp.concatenate(sums, axis=1))], G_COLS)

        @pl.when(i == 0)
        def _():
            st_ref[...] = upd

        @pl.when(i > 0)
        def _():
            st_ref[...] += upd

    return pl.pallas_call(
        body, name=name, grid=(t_rows // tm,),
        in_specs=[_row_spec(tm, D)] * 4 + [_row_spec(tm, G_COLS), _full_spec((1, G_COLS))],
        out_specs=[_row_spec(tm, D)] * 3 + [_row_spec(tm, G_COLS), _full_spec((SUBLANES, G_COLS))],
        out_shape=[jax.ShapeDtypeStruct((t_rows, D), BF16)] * 3
        + [jax.ShapeDtypeStruct((t_rows, G_COLS), BF16), jax.ShapeDtypeStruct((SUBLANES, G_COLS), F32)],
        compiler_params=_params(("arbitrary",)),
    )(dmerged, t0, t1, t2, z_g, b_gate)


def _ffn_mid_fwd(up, w8, n_lat, *, name):
    t_rows = up.shape[0]
    tm = 128
    prev, nxt = _halo_specs(tm, D_FF, t_rows)

    def body(a_ref, ap_ref, an_ref, g_ref, w_ref, cv_ref, f_ref):
        a = a_ref[...]
        au, ad = _shift_rows(a, ap_ref[...], an_ref[...], n_lat, t_rows, tm)
        cv = w_ref[0:1, :] * au + w_ref[1:2, :] * a + w_ref[2:3, :] * ad
        cv_ref[...] = cv
        f_ref[...] = (_silu(cv) * g_ref[...]).astype(BF16)

    return pl.pallas_call(
        body, name=name, grid=(t_rows // tm,),
        in_specs=[_row_spec(tm, D_FF), prev, nxt, _row_spec(tm, D_FF, 1), _full_spec((SUBLANES, D_FF))],
        out_specs=[_row_spec(tm, D_FF), _row_spec(tm, D_FF)],
        out_shape=[jax.ShapeDtypeStruct((t_rows, D_FF), F32), jax.ShapeDtypeStruct((t_rows, D_FF), BF16)],
        compiler_params=_params(("parallel",)),
    )(up, up, up, up, w8)


def _ffn_mid_bwd(up, cv, df, w8, n_lat, *, name):
    t_rows = up.shape[0]
    tm = 128
    prev, nxt = _halo_specs(tm, D_FF, t_rows)
    gprev, gnxt = _halo_specs(tm, D_FF, t_rows, 1)

    def body(a_ref, ap_ref, an_ref, g_ref, gp_ref, gn_ref, cv_ref, cp_ref, cn_ref, df_ref, dfp_ref, dfn_ref,
             w_ref, o_ref, st_ref):
        i = pl.program_id(0)
        a = a_ref[...]
        au, ad = _shift_rows(a, ap_ref[...], an_ref[...], n_lat, t_rows, tm)
        cvv = cv_ref[...]
        dfv = df_ref[...]
        o_ref[:, D_FF:] = (dfv * _silu(cvv)).astype(BF16)
        dcv = dfv * g_ref[...] * _dsilu(cvv)
        dcv_p = dfp_ref[...] * gp_ref[...] * _dsilu(cp_ref[...])
        dcv_n = dfn_ref[...] * gn_ref[...] * _dsilu(cn_ref[...])
        du, dd = _shift_rows(dcv, dcv_p, dcv_n, n_lat, t_rows, tm)
        o_ref[:, :D_FF] = (w_ref[0:1, :] * dd + w_ref[1:2, :] * dcv + w_ref[2:3, :] * du).astype(BF16)
        upd = _rows8([(0, _colsum(dcv * au)), (1, _colsum(dcv * a)), (2, _colsum(dcv * ad))], D_FF)

        @pl.when(i == 0)
        def _():
            st_ref[...] = upd

        @pl.when(i > 0)
        def _():
            st_ref[...] += upd

    row = _row_spec(tm, D_FF)
    return pl.pallas_call(
        body, name=name, grid=(t_rows // tm,),
        in_specs=[row, prev, nxt, _row_spec(tm, D_FF, 1), gprev, gnxt, row, prev, nxt, row, prev, nxt,
                  _full_spec((SUBLANES, D_FF))],
        out_specs=[_row_spec(tm, 2 * D_FF), _full_spec((SUBLANES, D_FF))],
        out_shape=[jax.ShapeDtypeStruct((t_rows, 2 * D_FF), BF16), jax.ShapeDtypeStruct((SUBLANES, D_FF), F32)],
        compiler_params=_params(("arbitrary",)),
    )(up, up, up, up, up, up, cv, cv, cv, df, df, df, w8)


def _loss_head(x, target, g_final, n_lat, *, name):
    t_rows = x.shape[0]
    tm = 256
    last = n_lat // tm - 1

    def body(x_ref, t_ref, g_ref, dx_ref, st_ref):
        i = pl.program_id(0)
        is_ctx = i * tm >= n_lat
        xv = x_ref[...]
        gv = g_ref[...]
        rstd = lax.rsqrt(jnp.mean(xv * xv, axis=-1, keepdims=True) + EPS)
        rn = xv * rstd
        err = rn * gv - t_ref[...]
        dy = err / D
        e = dy * gv
        dx = rstd * (e - rn * jnp.mean(e * rn, axis=-1, keepdims=True))
        dx_ref[...] = jnp.where(is_ctx, 0.0, dx)
        keep = jnp.where(is_ctx, 0.0, 1.0)
        upd = _rows8([(0, keep * _colsum(dy * rn)), (1, keep * _colsum(err * err))], D)

        @pl.when(i == 0)
        def _():
            st_ref[...] = upd

        @pl.when(i > 0)
        def _():
            st_ref[...] += upd

    return pl.pallas_call(
        body, name=name, grid=(t_rows // tm,),
        in_specs=[_row_spec(tm, D), pl.BlockSpec((tm, D), lambda i: (jnp.minimum(i, last), 0)), _full_spec((1, D))],
        out_specs=[_row_spec(tm, D), _full_spec((SUBLANES, D))],
        out_shape=[jax.ShapeDtypeStruct((t_rows, D), F32), jax.ShapeDtypeStruct((SUBLANES, D), F32)],
        compiler_params=_params(("arbitrary",)),
    )(x, target, g_final)


def _sum_slabs(x, out_dtype, *, name):
    n_slabs, rows, cols = x.shape
    tm = _pick(rows, (256, 192, 128, 64, 32, 24, 16, 8))

    def body(x_ref, o_ref):
        acc = x_ref[0].astype(F32)
        for s in range(1, n_slabs):
            acc = acc + x_ref[s].astype(F32)
        o_ref[...] = acc.astype(o_ref.dtype)

    return pl.pallas_call(
        body, name=name, grid=(rows // tm,),
        in_specs=[pl.BlockSpec((n_slabs, tm, cols), lambda i: (0, i, 0))],
        out_specs=pl.BlockSpec((tm, cols), lambda i: (i, 0)),
        out_shape=jax.ShapeDtypeStruct((rows, cols), out_dtype),
        compiler_params=_params(("parallel",)),
    )(x)


def _add_pairs(a, b, *, name):
    n_slabs, rows, cols = a.shape
    tm = _pick(rows, (256, 192, 128, 64, 32, 16))

    def body(a_ref, b_ref, o_ref):
        o_ref[...] = (a_ref[...].astype(F32) + b_ref[...].astype(F32)).astype(BF16)

    spec = pl.BlockSpec((1, tm, cols), lambda s, i: (s, i, 0))
    return pl.pallas_call(
        body, name=name, grid=(n_slabs, rows // tm), in_specs=[spec, spec], out_specs=spec,
        out_shape=jax.ShapeDtypeStruct(a.shape, BF16), compiler_params=_params(("parallel", "parallel")),
    )(a, b)


def _adamw(w, g, m, v, *, name):
    rows, cols = w.shape
    tm = _pick(rows, (128, 64, 32, 16, 8))

    def body(w_ref, g_ref, m_ref, v_ref, d_ref, nm_ref, nv_ref):
        gv = g_ref[...]
        nm = ADAM_B1 * m_ref[...] + (1.0 - ADAM_B1) * gv
        nv = ADAM_B2 * v_ref[...] + (1.0 - ADAM_B2) * jnp.square(gv)
        m_hat = nm / (1.0 - ADAM_B1 ** ADAM_STEP)
        v_hat = nv / (1.0 - ADAM_B2 ** ADAM_STEP)
        d_ref[...] = -ADAM_LR * (m_hat / (jnp.sqrt(v_hat) + ADAM_EPS) + ADAM_WD * w_ref[...])
        nm_ref[...] = nm
        nv_ref[...] = nv

    spec = pl.BlockSpec((tm, cols), lambda i: (i, 0))
    shape = jax.ShapeDtypeStruct((rows, cols), F32)
    return pl.pallas_call(
        body, name=name, grid=(rows // tm,), in_specs=[spec] * 4, out_specs=[spec] * 3, out_shape=[shape] * 3,
        compiler_params=_params(("parallel",)),
    )(w, g, m, v)


def _place():
    x, y, c = lax.axis_index("x"), lax.axis_index("y"), lax.axis_index("c")
    chips = [(1 - x, y), (x, 1 - y), (1 - x, 1 - y)]
    return x, y, c, chips


def _remote(src, dst, send_sems, recv_sems, k, to):
    return pltpu.make_async_remote_copy(src_ref=src, dst_ref=dst, send_sem=send_sems.at[k], recv_sem=recv_sems.at[k],
                                        device_id=to, device_id_type=MESH)


def _gather_weights(pack):
    _, rows, cols = pack.shape

    def body(p_ref, o_ref, send_sems, recv_sems, loc_sems):
        x, y, c, chips = _place()
        me = 2 * x + y
        sib = (x, y, 1 - c)
        local = [pltpu.make_async_copy(p_ref.at[l], o_ref.at[l, me], loc_sems.at[l]) for l in range(DEPTH)]
        for cp in local:
            cp.start()
        first = [_remote(p_ref.at[c], o_ref.at[c, me], send_sems, recv_sems, j, (*chip, c))
                 for j, chip in enumerate(chips)]
        for cp in first:
            cp.start()
        slots = [2 * cx + cy for cx, cy in chips]
        passed = [_remote(o_ref.at[c, s], o_ref.at[c, s], send_sems, recv_sems, 3 + j, sib)
                  for j, s in enumerate(slots)]
        for j, s in enumerate(slots):
            _remote(p_ref.at[c], o_ref.at[c, s], send_sems, recv_sems, j, sib).wait_recv()
            passed[j].start()
        for j, s in enumerate(slots):
            _remote(p_ref.at[c], o_ref.at[1 - c, s], send_sems, recv_sems, 3 + j, sib).wait_recv()
        for cp in first + passed:
            cp.wait_send()
        for cp in local:
            cp.wait()

    return pl.pallas_call(
        body, name="gather_weights", in_specs=[ANY], out_specs=ANY,
        out_shape=jax.ShapeDtypeStruct((DEPTH, N_CHIPS, rows, cols), pack.dtype),
        scratch_shapes=[pltpu.SemaphoreType.DMA((6,)), pltpu.SemaphoreType.DMA((6,)), pltpu.SemaphoreType.DMA((2,))],
    )(pack)


def _sibling_reduce_start(g_pack):
    _, n_slabs, rows, cols = g_pack.shape

    def body(g_ref, mine_ref, got_ref, send_sems, recv_sems, loc_sem):
        x, y, c, _ = _place()
        sib = (x, y, 1 - c)
        local = pltpu.make_async_copy(g_ref.at[c], mine_ref, loc_sem.at[0])
        local.start()
        swap = _remote(g_ref.at[1 - c], got_ref, send_sems, recv_sems, 0, sib)
        swap.start()
        swap.wait_recv()
        swap.wait_send()
        local.wait()

    shape = jax.ShapeDtypeStruct((n_slabs, rows, cols), g_pack.dtype)
    return pl.pallas_call(
        body, name="grad_sibling_swap", in_specs=[ANY], out_specs=[ANY, ANY], out_shape=[shape, shape],
        scratch_shapes=[pltpu.SemaphoreType.DMA((1,)), pltpu.SemaphoreType.DMA((1,)), pltpu.SemaphoreType.DMA((1,))],
    )(g_pack)


def _chip_exchange(s):
    n_slabs, rows, cols = s.shape

    def body(s_ref, o_ref, send_sems, recv_sems, loc_sem):
        x, y, c, chips = _place()
        me = 2 * x + y
        local = pltpu.make_async_copy(s_ref.at[me], o_ref.at[me], loc_sem.at[0])
        local.start()
        slots = [2 * cx + cy for cx, cy in chips]
        sends = [_remote(s_ref.at[sl], o_ref.at[me], send_sems, recv_sems, j, (*chip, c))
                 for j, (chip, sl) in enumerate(zip(chips, slots))]
        for cp in sends:
            cp.start()
        for j, sl in enumerate(slots):
            _remote(s_ref.at[sl], o_ref.at[sl], send_sems, recv_sems, j, (x, y, c)).wait_recv()
        for cp in sends:
            cp.wait_send()
        local.wait()

    return pl.pallas_call(
        body, name="grad_chip_exchange", in_specs=[ANY], out_specs=ANY, out_shape=jax.ShapeDtypeStruct(s.shape, s.dtype),
        scratch_shapes=[pltpu.SemaphoreType.DMA((3,)), pltpu.SemaphoreType.DMA((3,)), pltpu.SemaphoreType.DMA((1,))],
    )(s)


def _sibling_share(t):
    rows, cols = t.shape

    def body(t_ref, o_ref, send_sems, recv_sems, loc_sem):
        x, y, c, _ = _place()
        sib = (x, y, 1 - c)
        local = pltpu.make_async_copy(t_ref, o_ref.at[c], loc_sem.at[0])
        local.start()
        _remote(t_ref, o_ref.at[c], send_sems, recv_sems, 0, sib).start()
        _remote(t_ref, o_ref.at[1 - c], send_sems, recv_sems, 0, sib).wait_recv()
        _remote(t_ref, o_ref.at[c], send_sems, recv_sems, 0, sib).wait_send()
        local.wait()

    return pl.pallas_call(
        body, name="grad_sibling_share", in_specs=[ANY], out_specs=ANY,
        out_shape=jax.ShapeDtypeStruct((DEPTH, rows, cols), t.dtype),
        scratch_shapes=[pltpu.SemaphoreType.DMA((1,)), pltpu.SemaphoreType.DMA((1,)), pltpu.SemaphoreType.DMA((1,))],
    )(t)


def _allgather8(v, *, name):
    rows, cols = v.shape

    def body(v_ref, o_ref, send_sems, recv_sems, loc_sem):
        x, y, c, chips = _place()
        sib = (x, y, 1 - c)

        def slot(px, py, pc):
            return o_ref.at[4 * px + 2 * py + pc]

        local = pltpu.make_async_copy(v_ref, slot(x, y, c), loc_sem.at[0])
        local.start()
        first = [_remote(v_ref, slot(x, y, c), send_sems, recv_sems, 0, sib)]
        first += [_remote(v_ref, slot(x, y, c), send_sems, recv_sems, 1 + j, (*chip, c)) for j, chip in enumerate(chips)]
        for cp in first:
            cp.start()
        passed = [_remote(slot(*chip, c), slot(*chip, c), send_sems, recv_sems, 4 + j, sib)
                  for j, chip in enumerate(chips)]
        for j, chip in enumerate(chips):
            _remote(v_ref, slot(*chip, c), send_sems, recv_sems, 1 + j, sib).wait_recv()
            passed[j].start()
        _remote(v_ref, slot(x, y, 1 - c), send_sems, recv_sems, 0, sib).wait_recv()
        for j, chip in enumerate(chips):
            _remote(v_ref, slot(*chip, 1 - c), send_sems, recv_sems, 4 + j, sib).wait_recv()
        for cp in first + passed:
            cp.wait_send()
        local.wait()

    return pl.pallas_call(
        body, name=name, in_specs=[ANY], out_specs=ANY, out_shape=jax.ShapeDtypeStruct((N_DEV, rows, cols), v.dtype),
        scratch_shapes=[pltpu.SemaphoreType.DMA((7,)), pltpu.SemaphoreType.DMA((7,)), pltpu.SemaphoreType.DMA((1,))],
    )(v)


_BIG = (("w_mod", (D, 6 * D), 1), ("w_in", (D, IN_W), 1), ("w_branch", (3 * D, D), None), ("w_out", (D, D), 0),
        ("w_up", (D, 2 * D_FF), 1), ("w_down", (D_FF, D), 0))


def _shard_rows(name):
    shape = dict((n, s) for n, s, _ in _BIG)[name]
    return shape[0] * shape[1] // N_CHIPS // D


PACK_ROWS = sum(_shard_rows(n) for n, _, _ in _BIG)


def _pack_shards(shards):
    return jnp.concatenate([shards[n].reshape(DEPTH, _shard_rows(n), D) for n, _, _ in _BIG], axis=1)


def _unpack_full(gathered, layer):
    out, off = {}, 0
    for name, shape, axis in _BIG:
        r = _shard_rows(name)
        blk = gathered[layer, :, off:off + r, :]
        off += r
        if name == "w_branch":
            out[name] = blk.reshape(N_CHIPS, 3, D // N_CHIPS, D).transpose(1, 0, 2, 3).reshape(3, D, D)
        elif axis == 0:
            out[name] = blk.reshape(shape)
        else:
            out[name] = blk.reshape(N_CHIPS, shape[0], shape[1] // N_CHIPS).transpose(1, 0, 2).reshape(shape)
    return out


def _pack_grads(grads):
    parts = []
    for name, shape, axis in _BIG:
        g = grads[name]
        r = _shard_rows(name)
        if name == "w_branch":
            g = g.reshape(3, N_CHIPS, D // N_CHIPS, D).transpose(1, 0, 2, 3)
        elif axis == 1:
            g = g.reshape(shape[0], N_CHIPS, shape[1] // N_CHIPS).transpose(1, 0, 2)
        parts.append(g.reshape(N_CHIPS, r, D))
    return jnp.concatenate(parts, axis=1)


def _unpack_shards(total, like):
    out, off = {}, 0
    for name, _, _ in _BIG:
        r = _shard_rows(name)
        out[name] = total[:, off:off + r, :].reshape(like[name].shape)
        off += r
    return out


def _pad_rows(v, rows):
    return jnp.concatenate([v, jnp.zeros((rows - v.shape[0],) + v.shape[1:], v.dtype)], axis=0)


def _local_step(x_tok, target, c_vec, c_ctx, wfull, small, n_lat, n_ctx):
    cos_t, sin_t = _rope_tables(n_lat, n_ctx)
    a_in = _pad_rows(jnp.stack([c_vec, c_ctx]), LANES)
    a128 = _small(_silu, (LANES, D), a_in, name="cond_silu")

    saved = []
    xs = x_tok
    for l in range(DEPTH):
        w = wfull[l]
        sm = {k: v[l] for k, v in small.items()}
        mod128 = _mm(a128, w["w_mod"], name=f"mod{l}")
        mod8 = _small(lambda m, b: m + b, (SUBLANES, 6 * D), mod128[:SUBLANES], sm["b_mod"][None, :], name=f"mod_bias{l}")
        g_mix = sm["g_mix"][None, :]
        g_ffn = sm["g_ffn"][None, :]
        g_v = sm["g_v"][None, :]
        b_gate = sm["b_gate"][None, :]
        sink_b = jnp.broadcast_to(sm["sink"][:, None], (N_HEADS, LANES))
        b_sb = jnp.broadcast_to(sm["b_spatial"][:, :, None], (A_GROUPS, BLK, LANES))
        w_sconv8 = _pad_rows(sm["w_sconv"], SUBLANES)
        w_fconv8 = _pad_rows(sm["w_fconv"], SUBLANES)
        w_in = w["w_in"]
        w_seg = [w_in[:, SEG[k]:SEG[k + 1]] for k in range(4)]

        h = _norm_mod_fwd(xs, g_mix, mod8, 0, 1, n_lat, name=f"norm1_{l}")
        z_qkv, z_a, z_b, z_g = [_mm(h, w_seg[k], name=f"in_proj{k}_{l}") for k in range(4)]
        q, kd, vd = _qkv_prep(z_qkv, cos_t, sin_t, name=f"qkv_prep{l}")
        y_attn = _attention_fwd(q, kd, vd, sink_b, n_lat, n_ctx, name=f"attn{l}")
        y_a = _gating_fwd(z_a, sm["w_spatial"], b_sb, g_v, name=f"gating{l}")
        y_b = _sconv_fwd(z_b, w_sconv8, n_lat, name=f"sconv{l}")
        ys = (y_attn, y_a, y_b)
        ts = [_mm(ys[k], w["w_branch"][k], name=f"branch{k}_{l}") for k in range(3)]
        merged = _merge_fwd(*ts, z_g, b_gate, name=f"merge{l}")
        mix_out = _mm(merged, w["w_out"], name=f"out_proj{l}")
        x1 = _residual_fwd(xs, mix_out, mod8, 2, n_lat, name=f"res1_{l}")
        h2 = _norm_mod_fwd(x1, g_ffn, mod8, 3, 4, n_lat, name=f"norm2_{l}")
        up = _mm(h2, w["w_up"], name=f"up_proj{l}")
        cv, f = _ffn_mid_fwd(up, w_fconv8, n_lat, name=f"ffn_mid{l}")
        ffn_out = _mm(f, w["w_down"], name=f"down_proj{l}")
        x2 = _residual_fwd(x1, ffn_out, mod8, 5, n_lat, name=f"res2_{l}")
        saved.append(dict(x0=xs, mod8=mod8, h=h, z_qkv=z_qkv, z_a=z_a, z_b=z_b, z_g=z_g, q=q, kd=kd, vd=vd, ys=ys, ts=ts,
                          merged=merged, mix_out=mix_out, x1=x1, h2=h2, up=up, cv=cv, f=f, ffn_out=ffn_out, w_seg=w_seg,
                          g_mix=g_mix, g_ffn=g_ffn, g_v=g_v, b_gate=b_gate, sink_b=sink_b, b_sb=b_sb,
                          w_sconv8=w_sconv8, w_fconv8=w_fconv8, w_spatial=sm["w_spatial"]))
        xs = x2

    dx, st = _loss_head(xs, target, small["g_final"][None, :], n_lat, name="loss_head")
    sq_err = st[1]
    sgrads = {k: [None] * DEPTH for k in ("b_mod", "g_mix", "b_gate", "sink", "w_spatial", "b_spatial", "g_v",
                                           "w_sconv", "g_ffn", "w_fconv")}
    sgrads["g_final"] = st[0]
    wgrads = [None] * DEPTH
    d_a128 = []

    for l in reversed(range(DEPTH)):
        s = saved[l]
        w = wfull[l]
        mod8 = s["mod8"]
        d_ffn, st_gt2 = _residual_bwd(dx, s["ffn_out"], mod8, 5, n_lat, name=f"res2_bwd{l}")
        df = _mm(d_ffn, w["w_down"], tb=True, name=f"down_bwd_x{l}")
        g_down = _mm(s["f"], d_ffn, ta=True, out_dtype=BF16, name=f"down_bwd_w{l}")
        d_up, st_fc = _ffn_mid_bwd(s["up"], s["cv"], df, s["w_fconv8"], n_lat, name=f"ffn_mid_bwd{l}")
        dh2 = _mm(d_up, w["w_up"], tb=True, name=f"up_bwd_x{l}")
        g_up = _mm(s["h2"], d_up, ta=True, out_dtype=BF16, name=f"up_bwd_w{l}")
        dx1, st_n2 = _norm_mod_bwd(s["x1"], [dh2], dx, s["g_ffn"], mod8, 4, n_lat, name=f"norm2_bwd{l}")
        d_out, st_gt1 = _residual_bwd(dx1, s["mix_out"], mod8, 2, n_lat, name=f"res1_bwd{l}")
        d_merged = _mm(d_out, w["w_out"], tb=True, name=f"out_bwd_x{l}")
        g_out = _mm(s["merged"], d_out, ta=True, out_dtype=BF16, name=f"out_bwd_w{l}")
        dt0, dt1, dt2, dz_g, st_bg = _merge_bwd(d_merged, *s["ts"], s["z_g"], s["b_gate"], name=f"merge_bwd{l}")
        dts = (dt0, dt1, dt2)
        dys = [_mm(dts[k], w["w_branch"][k], tb=True, name=f"branch{k}_bwd_x{l}") for k in range(3)]
        g_branch = jnp.stack([_mm(s["ys"][k], dts[k], ta=True, out_dtype=BF16, name=f"branch{k}_bwd_w{l}")
                              for k in range(3)])
        dq, dkd, dvd, d_sink = _attention_bwd(s["q"], s["kd"], s["vd"], s["sink_b"], dys[0], n_lat, n_ctx,
                                              name=f"attn_bwd{l}")
        dz_qkv = _qkv_unprep(dq, dkd, dvd, cos_t, sin_t, name=f"qkv_unprep{l}")
        dz_a, d_ws, d_bs, st_gv = _gating_bwd(s["z_a"], dys[1], s["w_spatial"], s["b_sb"], s["g_v"], name=f"gating_bwd{l}")
        dz_b, st_sc = _sconv_bwd(s["z_b"], dys[2], s["w_sconv8"], n_lat, name=f"sconv_bwd{l}")
        dzs = (dz_qkv, dz_a, dz_b, dz_g)
        dh_parts = [_mm(dzs[k], s["w_seg"][k], tb=True, name=f"in_bwd_x{k}_{l}") for k in range(4)]
        g_in = jnp.concatenate([_mm(s["h"], dzs[k], ta=True, out_dtype=BF16, name=f"in_bwd_w{k}_{l}")
                                for k in range(4)], axis=1)
        dx, st_n1 = _norm_mod_bwd(s["x0"], dh_parts, dx1, s["g_mix"], mod8, 1, n_lat, name=f"norm1_bwd{l}")
        dmod = jnp.concatenate([st_n1[0:2], st_n1[2:4], st_gt1[0:2], st_n2[0:2], st_n2[2:4], st_gt2[0:2]], axis=1)
        dmod128 = _pad_rows(dmod, LANES)
        g_mod = _mm(a128, dmod128, ta=True, out_dtype=BF16, name=f"mod_bwd_w{l}")
        d_a128.append(_mm(dmod128, w["w_mod"], tb=True, name=f"mod_bwd_x{l}"))
        wgrads[l] = dict(w_mod=g_mod, w_in=g_in, w_branch=g_branch.reshape(3 * D, D), w_out=g_out, w_up=g_up,
                         w_down=g_down)
        sgrads["b_mod"][l] = dmod[0] + dmod[1]
        sgrads["g_mix"][l] = st_n1[4]
        sgrads["g_ffn"][l] = st_n2[4]
        sgrads["b_gate"][l] = st_bg[0]
        sgrads["sink"][l] = d_sink[:, 0]
        sgrads["w_spatial"][l] = d_ws
        sgrads["b_spatial"][l] = d_bs[:, :, 0]
        sgrads["g_v"][l] = st_gv[0]
        sgrads["w_sconv"][l] = st_sc[0:3]
        sgrads["w_fconv"][l] = st_fc[0:3]

    d_cond = _small(lambda a, b, cin: (a + b) * _dsilu(cin), (LANES, D), d_a128[0], d_a128[1], a_in, name="cond_bwd")
    out_small = {k: (jnp.stack(v) if isinstance(v, list) else v) for k, v in sgrads.items()}
    out_small["c_ctx"] = d_cond[1]
    return sq_err, dx, wgrads, out_small


_SMALL_ORDER = ("c_ctx", "b_mod", "g_mix", "b_gate", "sink", "w_spatial", "b_spatial", "g_v", "w_sconv", "g_ffn",
                "w_fconv", "g_final")


def _flat_pack(parts, width):
    flat = jnp.concatenate([p.reshape(-1).astype(F32) for p in parts])
    rows = -(-flat.shape[0] // (width * SUBLANES)) * SUBLANES
    flat = jnp.concatenate([flat, jnp.zeros((rows * width - flat.shape[0],), F32)])
    return flat.reshape(rows, width)


def _flat_unpack(packed, likes):
    flat = packed.reshape(-1)
    out, off = [], 0
    for like in likes:
        n = math.prod(like.shape)
        out.append(flat[off:off + n].reshape(like.shape))
        off += n
    return out


def kernel(x, c, ctx, c_ctx, w_mod, b_mod, g_mix, w_in, b_gate, sink, w_spatial, b_spatial, g_v, w_sconv, w_branch, w_out, g_ffn, w_up, w_fconv, w_down, g_final, loss_target, m_c_ctx, m_w_mod, m_b_mod, m_g_mix, m_w_in, m_b_gate, m_sink, m_w_spatial, m_b_spatial, m_g_v, m_w_sconv, m_w_branch, m_w_out, m_g_ffn, m_w_up, m_w_fconv, m_w_down, m_g_final, v_c_ctx, v_w_mod, v_b_mod, v_g_mix, v_w_in, v_b_gate, v_sink, v_w_spatial, v_b_spatial, v_g_v, v_w_sconv, v_w_branch, v_w_out, v_g_ffn, v_w_up, v_w_fconv, v_w_down, v_g_final):
    n_lat, n_ctx = x.shape[1], ctx.shape[1]
    chip = 2 * lax.axis_index("x") + lax.axis_index("y")
    weights = dict(c_ctx=c_ctx, w_mod=w_mod, b_mod=b_mod, g_mix=g_mix, w_in=w_in, b_gate=b_gate, sink=sink,
                   w_spatial=w_spatial, b_spatial=b_spatial, g_v=g_v, w_sconv=w_sconv, w_branch=w_branch, w_out=w_out,
                   g_ffn=g_ffn, w_up=w_up, w_fconv=w_fconv, w_down=w_down, g_final=g_final)
    m_in = dict(c_ctx=m_c_ctx, w_mod=m_w_mod, b_mod=m_b_mod, g_mix=m_g_mix, w_in=m_w_in, b_gate=m_b_gate, sink=m_sink,
                w_spatial=m_w_spatial, b_spatial=m_b_spatial, g_v=m_g_v, w_sconv=m_w_sconv, w_branch=m_w_branch,
                w_out=m_w_out, g_ffn=m_g_ffn, w_up=m_w_up, w_fconv=m_w_fconv, w_down=m_w_down, g_final=m_g_final)
    v_in = dict(c_ctx=v_c_ctx, w_mod=v_w_mod, b_mod=v_b_mod, g_mix=v_g_mix, w_in=v_w_in, b_gate=v_b_gate, sink=v_sink,
                w_spatial=v_w_spatial, b_spatial=v_b_spatial, g_v=v_g_v, w_sconv=v_w_sconv, w_branch=v_w_branch,
                w_out=v_w_out, g_ffn=v_g_ffn, w_up=v_w_up, w_fconv=v_w_fconv, w_down=v_w_down, g_final=v_g_final)
    big_names = [n for n, _, _ in _BIG]

    gathered = _gather_weights(_pack_shards({n: weights[n].astype(BF16) for n in big_names}))
    wfull = [_unpack_full(gathered, l) for l in range(DEPTH)]
    conv_pack = _flat_pack([w_sconv, w_fconv], LANES)
    conv_all = _allgather8(conv_pack, name="gather_conv_weights")
    conv_parts = [_flat_unpack(conv_all[2 * p], [w_sconv, w_fconv]) for p in range(N_CHIPS)]
    w_sconv_full = jnp.concatenate([cp[0] for cp in conv_parts], axis=-1)
    w_fconv_full = jnp.concatenate([cp[1] for cp in conv_parts], axis=-1)

    small = dict(b_mod=b_mod, g_mix=g_mix, b_gate=b_gate, sink=sink, w_spatial=w_spatial, b_spatial=b_spatial, g_v=g_v,
                 w_sconv=w_sconv_full, g_ffn=g_ffn, w_fconv=w_fconv_full, g_final=g_final)
    x_tok = jnp.concatenate([x[0], ctx[0]], axis=0)
    sq_err, dx, wgrads, sgrads = _local_step(x_tok, loss_target[0], c[0], c_ctx, wfull, small, n_lat, n_ctx)

    loss = lax.psum(0.5 * jnp.sum(sq_err) / D, ("x", "y", "c"))
    grad_x = dx[:n_lat][None]

    g_pack = jnp.stack([_pack_grads(wgrads[l]) for l in range(DEPTH)])
    mine, got = _sibling_reduce_start(g_pack)
    chip_sum = _add_pairs(mine, got, name="grad_pair_sum")
    exchanged = _chip_exchange(chip_sum)
    total_c = _sum_slabs(exchanged, F32, name="grad_chip_sum")
    total = _sibling_share(total_c)
    big_grads = _unpack_shards(total, {n: weights[n] for n in big_names})

    s_likes = [sgrads[n] for n in _SMALL_ORDER]
    s_all = _allgather8(_flat_pack(s_likes, D), name="gather_small_grads")
    s_tot = _flat_unpack(_sum_slabs(s_all, F32, name="small_grad_sum"), s_likes)
    grads = dict(big_grads)
    for n, g in zip(_SMALL_ORDER, s_tot):
        grads[n] = g
    grads["w_sconv"] = lax.dynamic_slice_in_dim(grads["w_sconv"], chip * w_sconv.shape[-1], w_sconv.shape[-1], axis=2)
    grads["w_fconv"] = lax.dynamic_slice_in_dim(grads["w_fconv"], chip * w_fconv.shape[-1], w_fconv.shape[-1], axis=2)

    delta, new_m, new_v = {}, {}, {}
    for n in big_names:
        cols = weights[n].shape[-1]
        view = lambda a: a.reshape(-1, cols)
        d_, m_, v_ = _adamw(view(weights[n]), view(grads[n]), view(m_in[n]), view(v_in[n]), name=f"adamw_{n}")
        delta[n], new_m[n], new_v[n] = (t.reshape(weights[n].shape) for t in (d_, m_, v_))
    likes = [weights[n] for n in _SMALL_ORDER]
    packs = [_flat_pack([src[n] for n in _SMALL_ORDER], LANES) for src in (weights, grads, m_in, v_in)]
    outs = _adamw(*packs, name="adamw_small")
    for dst, packed in zip((delta, new_m, new_v), outs):
        for n, val in zip(_SMALL_ORDER, _flat_unpack(packed, likes)):
            dst[n] = val

    order = ("c_ctx", "w_mod", "b_mod", "g_mix", "w_in", "b_gate", "sink", "w_spatial", "b_spatial", "g_v", "w_sconv",
             "w_branch", "w_out", "g_ffn", "w_up", "w_fconv", "w_down", "g_final")
    return (loss, grad_x, *[grads[n] for n in order], *[delta[n] for n in order], *[new_m[n] for n in order],
            *[new_v[n] for n in order])
```

```python
import functools
import math

import jax
import jax.numpy as jnp
from jax import lax
from jax.experimental import pallas as pl
from jax.experimental.pallas import tpu as pltpu

F32 = jnp.float32
BF16 = jnp.bfloat16

D = 1024
DEPTH = 2
GRID_W = 64
N_HEADS = 16
N_KV = 4
GRP = N_HEADS // N_KV
HEAD_DIM = 64
KV_W = N_KV * HEAD_DIM
WINDOW = 128
BLK = 128
ROPE_THETA = 10000.0
A_GROUPS = 8
D_FF = 2816
EPS = 1e-6
NEG = -1e30
QKV_W = D + 2 * KV_W
A_COLS = 2 * D
B_COLS = 3 * D
G_COLS = 3 * D
IN_W = QKV_W + A_COLS + B_COLS + G_COLS
SEG = (0, QKV_W, QKV_W + A_COLS, QKV_W + A_COLS + B_COLS, IN_W)
N_CHIPS = 4
N_DEV = 8
LANES = 128
SUBLANES = 8
VMEM_LIMIT = 48 * 1024 * 1024
ADAM_LR = 0.001
ADAM_B1 = 0.9
ADAM_B2 = 0.999
ADAM_EPS = 1e-08
ADAM_WD = 0.01
ADAM_STEP = 10
MESH = pl.DeviceIdType.MESH
ANY = pl.BlockSpec(memory_space=pl.ANY)


def _params(sem=None):
    return pltpu.CompilerParams(dimension_semantics=sem, vmem_limit_bytes=VMEM_LIMIT)


def _pick(n, cands):
    for c in cands:
        if n % c == 0:
            return c
    return n


def _rows8(rows, width):
    r = lax.broadcasted_iota(jnp.int32, (SUBLANES, width), 0)
    out = jnp.zeros((SUBLANES, width), F32)
    for idx, v in rows:
        out = out + jnp.where(r == idx, v, 0.0)
    return out


def _sel(mod_ref, k, is_ctx):
    return jnp.where(is_ctx, mod_ref[1:2, k * D:(k + 1) * D], mod_ref[0:1, k * D:(k + 1) * D])


def _colsum(v):
    return jnp.sum(v, axis=0, keepdims=True)


def _mm(a, b, *, name, ta=False, tb=False, out_dtype=F32):
    if ta:
        k_dim, m = a.shape
    else:
        m, k_dim = a.shape
    if tb:
        n, kb = b.shape
    else:
        kb, n = b.shape
    assert k_dim == kb, (a.shape, b.shape, ta, tb)
    tm = _pick(m, (1056, 1024, 1408, 768, 512, 256, 128))
    tn = _pick(n, (1536, 1408, 1024, 768, 512, 256, 128))
    tk = _pick(k_dim, (2048, 1536, 1408, 1024, 768, 512, 256, 128))
    nk = k_dim // tk
    dims = (((0 if ta else 1,), (1 if tb else 0,)), ((), ()))

    def product(a_ref, b_ref):
        return lax.dot_general(a_ref[...].astype(BF16), b_ref[...].astype(BF16), dims, preferred_element_type=F32)

    def body_single(a_ref, b_ref, o_ref):
        o_ref[...] = product(a_ref, b_ref).astype(o_ref.dtype)

    def body_acc(a_ref, b_ref, o_ref, acc_ref):
        k = pl.program_id(2)

        @pl.when(k == 0)
        def _():
            acc_ref[...] = product(a_ref, b_ref)

        @pl.when(k > 0)
        def _():
            acc_ref[...] += product(a_ref, b_ref)

        @pl.when(k == nk - 1)
        def _():
            o_ref[...] = acc_ref[...].astype(o_ref.dtype)

    a_spec = pl.BlockSpec((tk, tm), lambda i, j, k: (k, i)) if ta else pl.BlockSpec((tm, tk), lambda i, j, k: (i, k))
    b_spec = pl.BlockSpec((tn, tk), lambda i, j, k: (j, k)) if tb else pl.BlockSpec((tk, tn), lambda i, j, k: (k, j))
    return pl.pallas_call(
        body_single if nk == 1 else body_acc, name=name, grid=(m // tm, n // tn, nk),
        in_specs=[a_spec, b_spec], out_specs=pl.BlockSpec((tm, tn), lambda i, j, k: (i, j)),
        out_shape=jax.ShapeDtypeStruct((m, n), out_dtype),
        scratch_shapes=[] if nk == 1 else [pltpu.VMEM((tm, tn), F32)],
        compiler_params=_params(("parallel", "parallel", "arbitrary")),
    )(a, b)


def _small(fn, out_shape, *arrays, name):
    def body(*refs):
        refs[-1][...] = fn(*[r[...] for r in refs[:-1]]).astype(refs[-1].dtype)

    return pl.pallas_call(body, name=name, out_shape=jax.ShapeDtypeStruct(out_shape, F32))(*arrays)


def _silu(v):
    return v * jax.nn.sigmoid(v)


def _dsilu(v):
    s = jax.nn.sigmoid(v)
    return s * (1.0 + v * (1.0 - s))


def _row_spec(tm, width, col=0):
    return pl.BlockSpec((tm, width), lambda i: (i, col))


def _full_spec(shape):
    nd = len(shape)
    return pl.BlockSpec(shape, lambda i: (0,) * nd)


def _halo_specs(tm, width, t_rows, col=0):
    per = tm // SUBLANES
    last = t_rows // SUBLANES - 1
    prev = pl.BlockSpec((SUBLANES, width), lambda i: (jnp.maximum(i * per - 1, 0), col))
    nxt = pl.BlockSpec((SUBLANES, width), lambda i: (jnp.minimum((i + 1) * per, last), col))
    return prev, nxt


def _shift_rows(cur, prev8, next8, n_lat, t_rows, tm):
    i = pl.program_id(0)
    row = lax.broadcasted_iota(jnp.int32, (tm, 1), 0)
    g = row + i * tm
    up = pltpu.roll(cur, 1, 0)
    up = jnp.where(row == 0, prev8[SUBLANES - 1:SUBLANES, :], up)
    up = jnp.where((g == 0) | (g == n_lat), 0.0, up)
    dn = pltpu.roll(cur, tm - 1, 0)
    dn = jnp.where(row == tm - 1, next8[0:1, :], dn)
    dn = jnp.where((g == n_lat - 1) | (g == t_rows - 1), 0.0, dn)
    return up, dn


def _norm_mod_fwd(x, g, mod8, sh_idx, sc_idx, n_lat, *, name):
    t_rows = x.shape[0]
    tm = 256

    def body(x_ref, g_ref, mod_ref, o_ref):
        is_ctx = pl.program_id(0) * tm >= n_lat
        xv = x_ref[...]
        rstd = lax.rsqrt(jnp.mean(xv * xv, axis=-1, keepdims=True) + EPS)
        y = xv * rstd * g_ref[...]
        o_ref[...] = (y * (1.0 + _sel(mod_ref, sc_idx, is_ctx)) + _sel(mod_ref, sh_idx, is_ctx)).astype(BF16)

    return pl.pallas_call(
        body, name=name, grid=(t_rows // tm,),
        in_specs=[_row_spec(tm, D), _full_spec((1, D)), _full_spec((SUBLANES, 6 * D))],
        out_specs=_row_spec(tm, D), out_shape=jax.ShapeDtypeStruct((t_rows, D), BF16),
        compiler_params=_params(("parallel",)),
    )(x, g, mod8)


def _norm_mod_bwd(x, dh_parts, dres, g, mod8, sc_idx, n_lat, *, name):
    t_rows = x.shape[0]
    tm = 256
    n_parts = len(dh_parts)

    def body(*refs):
        x_ref, dres_ref, g_ref, mod_ref = refs[:4]
        part_refs = refs[4:4 + n_parts]
        dx_ref, st_ref = refs[4 + n_parts:]
        i = pl.program_id(0)
        is_ctx = i * tm >= n_lat
        dh = part_refs[0][...]
        for p in part_refs[1:]:
            dh = dh + p[...]
        xv = x_ref[...]
        gv = g_ref[...]
        rstd = lax.rsqrt(jnp.mean(xv * xv, axis=-1, keepdims=True) + EPS)
        rn = xv * rstd
        dy = dh * (1.0 + _sel(mod_ref, sc_idx, is_ctx))
        e = dy * gv
        dx_ref[...] = dres_ref[...] + rstd * (e - rn * jnp.mean(e * rn, axis=-1, keepdims=True))
        dsh = _colsum(dh)
        dsc = _colsum(dh * (rn * gv))
        dg = _colsum(dy * rn)
        zero = jnp.zeros_like(dsh)
        upd = _rows8([(0, jnp.where(is_ctx, zero, dsh)), (1, jnp.where(is_ctx, dsh, zero)),
                      (2, jnp.where(is_ctx, zero, dsc)), (3, jnp.where(is_ctx, dsc, zero)), (4, dg)], D)

        @pl.when(i == 0)
        def _():
            st_ref[...] = upd

        @pl.when(i > 0)
        def _():
            st_ref[...] += upd

    return pl.pallas_call(
        body, name=name, grid=(t_rows // tm,),
        in_specs=[_row_spec(tm, D), _row_spec(tm, D), _full_spec((1, D)), _full_spec((SUBLANES, 6 * D))]
        + [_row_spec(tm, D)] * n_parts,
        out_specs=[_row_spec(tm, D), _full_spec((SUBLANES, D))],
        out_shape=[jax.ShapeDtypeStruct((t_rows, D), F32), jax.ShapeDtypeStruct((SUBLANES, D), F32)],
        compiler_params=_params(("arbitrary",)),
    )(x, dres, g, mod8, *dh_parts)


def _residual_fwd(x, branch, mod8, gt_idx, n_lat, *, name):
    t_rows = x.shape[0]
    tm = 256

    def body(x_ref, b_ref, mod_ref, o_ref):
        is_ctx = pl.program_id(0) * tm >= n_lat
        o_ref[...] = x_ref[...] + _sel(mod_ref, gt_idx, is_ctx) * b_ref[...]

    return pl.pallas_call(
        body, name=name, grid=(t_rows // tm,),
        in_specs=[_row_spec(tm, D), _row_spec(tm, D), _full_spec((SUBLANES, 6 * D))],
        out_specs=_row_spec(tm, D), out_shape=jax.ShapeDtypeStruct((t_rows, D), F32),
        compiler_params=_params(("parallel",)),
    )(x, branch, mod8)


def _residual_bwd(dx, branch, mod8, gt_idx, n_lat, *, name):
    t_rows = dx.shape[0]
    tm = 256

    def body(dx_ref, b_ref, mod_ref, o_ref, st_ref):
        i = pl.program_id(0)
        is_ctx = i * tm >= n_lat
        dxv = dx_ref[...]
        o_ref[...] = (dxv * _sel(mod_ref, gt_idx, is_ctx)).astype(BF16)
        dgt = _colsum(dxv * b_ref[...])
        zero = jnp.zeros_like(dgt)
        upd = _rows8([(0, jnp.where(is_ctx, zero, dgt)), (1, jnp.where(is_ctx, dgt, zero))], D)

        @pl.when(i == 0)
        def _():
            st_ref[...] = upd

        @pl.when(i > 0)
        def _():
            st_ref[...] += upd

    return pl.pallas_call(
        body, name=name, grid=(t_rows // tm,),
        in_specs=[_row_spec(tm, D), _row_spec(tm, D), _full_spec((SUBLANES, 6 * D))],
        out_specs=[_row_spec(tm, D), _full_spec((SUBLANES, D))],
        out_shape=[jax.ShapeDtypeStruct((t_rows, D), BF16), jax.ShapeDtypeStruct((SUBLANES, D), F32)],
        compiler_params=_params(("arbitrary",)),
    )(dx, branch, mod8)


def _rope_tables(n_lat, n_ctx):
    rows = n_lat // GRID_W
    row = jnp.broadcast_to(jnp.arange(rows, dtype=F32)[:, None], (rows, GRID_W)).reshape(n_lat)
    col = jnp.broadcast_to(jnp.arange(GRID_W, dtype=F32)[None, :], (rows, GRID_W)).reshape(n_lat)
    half = HEAD_DIM // 2
    inv = ROPE_THETA ** (-jnp.arange(0, half, 2, dtype=F32) / half)
    ang = jnp.concatenate([row[:, None] * inv, col[:, None] * inv], axis=-1)
    cos, sin = jnp.cos(ang), jnp.sin(ang)
    c64 = jnp.concatenate([cos, cos], axis=-1)
    s64 = jnp.concatenate([-sin, sin], axis=-1)
    c64 = jnp.concatenate([c64, jnp.ones((n_ctx, HEAD_DIM), F32)], axis=0)
    s64 = jnp.concatenate([s64, jnp.zeros((n_ctx, HEAD_DIM), F32)], axis=0)
    return jnp.tile(c64, (1, 2)), jnp.tile(s64, (1, 2))


def _swap_halves(v):
    lane = lax.broadcasted_iota(jnp.int32, v.shape, 1)
    return jnp.where(lane % HEAD_DIM < HEAD_DIM // 2, pltpu.roll(v, LANES - HEAD_DIM // 2, 1),
                     pltpu.roll(v, HEAD_DIM // 2, 1))


def _low_half(shape):
    return lax.broadcasted_iota(jnp.int32, shape, 1) < HEAD_DIM


def _qkv_prep(z_qkv, cos_t, sin_t, *, name):
    t_rows = z_qkv.shape[0]
    tm = 256

    def body(z_ref, c_ref, s_ref, q_ref, k_ref, v_ref):
        cv, sv = c_ref[...], s_ref[...]

        def rope(chunk):
            return chunk * cv + _swap_halves(chunk) * sv

        for ch in range(D // LANES):
            q_ref[:, ch * LANES:(ch + 1) * LANES] = rope(z_ref[:, ch * LANES:(ch + 1) * LANES]).astype(BF16)
        low = _low_half((tm, LANES))
        for pair in range(N_KV // 2):
            for which, ref, roped in ((0, k_ref, True), (1, v_ref, False)):
                off = D + which * KV_W + pair * LANES
                chunk = z_ref[:, off:off + LANES]
                if roped:
                    chunk = rope(chunk)
                other = pltpu.roll(chunk, HEAD_DIM, 1)
                even = jnp.where(low, chunk, other)
                odd = jnp.where(low, other, chunk)
                ref[:, (2 * pair) * LANES:(2 * pair + 1) * LANES] = even.astype(BF16)
                ref[:, (2 * pair + 1) * LANES:(2 * pair + 2) * LANES] = odd.astype(BF16)

    dup_w = N_KV * LANES
    return pl.pallas_call(
        body, name=name, grid=(t_rows // tm,),
        in_specs=[_row_spec(tm, QKV_W), _row_spec(tm, LANES), _row_spec(tm, LANES)],
        out_specs=[_row_spec(tm, D), _row_spec(tm, dup_w), _row_spec(tm, dup_w)],
        out_shape=[jax.ShapeDtypeStruct((t_rows, D), BF16), jax.ShapeDtypeStruct((t_rows, dup_w), BF16),
                   jax.ShapeDtypeStruct((t_rows, dup_w), BF16)],
        compiler_params=_params(("parallel",)),
    )(z_qkv, cos_t, sin_t)


def _qkv_unprep(dq, dk, dv, cos_t, sin_t, *, name):
    t_rows = dq.shape[0]
    tm = 256

    def body(dq_ref, dk_ref, dv_ref, c_ref, s_ref, o_ref):
        cv, sv = c_ref[...], s_ref[...]

        def unrope(chunk):
            return chunk * cv + _swap_halves(chunk * sv)

        for ch in range(D // LANES):
            o_ref[:, ch * LANES:(ch + 1) * LANES] = unrope(dq_ref[:, ch * LANES:(ch + 1) * LANES]).astype(BF16)
        for pair in range(N_KV // 2):
            for which, ref, roped in ((0, dk_ref, True), (1, dv_ref, False)):
                chunk = ref[:, pair * LANES:(pair + 1) * LANES]
                if roped:
                    chunk = unrope(chunk)
                off = D + which * KV_W + pair * LANES
                o_ref[:, off:off + LANES] = chunk.astype(BF16)

    return pl.pallas_call(
        body, name=name, grid=(t_rows // tm,),
        in_specs=[_row_spec(tm, D), _row_spec(tm, KV_W), _row_spec(tm, KV_W), _row_spec(tm, LANES),
                  _row_spec(tm, LANES)],
        out_specs=_row_spec(tm, QKV_W), out_shape=jax.ShapeDtypeStruct((t_rows, QKV_W), BF16),
        compiler_params=_params(("parallel",)),
    )(dq, dk, dv, cos_t, sin_t)


def _attn_specs(n_lat, n_ctx):
    nb = n_lat // BLK
    dup_w = N_KV * LANES

    def ws(j):
        return jnp.clip(j - 1, 0, nb - 3)

    win = [pl.BlockSpec((BLK, dup_w), functools.partial(lambda j, o: (ws(j) + o, 0), o=o)) for o in range(3)]
    ctx = pl.BlockSpec((n_ctx, dup_w), lambda j: (n_lat // n_ctx, 0))
    return nb, ws, win, ctx


def _attn_bias(j, ws_j, nb, n_ctx):
    n_keys = 3 * BLK + n_ctx
    row = lax.broadcasted_iota(jnp.int32, (BLK, n_keys), 0)
    col = lax.broadcasted_iota(jnp.int32, (BLK, n_keys), 1)
    rel = (ws_j - j) * BLK + col - row
    valid = (col >= 3 * BLK) | ((jnp.abs(rel) <= WINDOW) & (j < nb))
    bias = jnp.where(valid, 0.0, NEG)
    return jnp.concatenate([bias] * GRP, axis=0)


def _attn_probs(q_ref, kk, kh, bias, sink_ref):
    low = _low_half((BLK, LANES))
    qs = []
    for g in range(GRP):
        h = GRP * kh + g
        chunk = q_ref[:, (h // 2) * LANES:(h // 2 + 1) * LANES]
        qs.append(jnp.where(low if h % 2 == 0 else ~low, chunk, jnp.zeros_like(chunk)))
    qs = jnp.concatenate(qs, axis=0)
    s = lax.dot_general(qs, kk, (((1,), (1,)), ((), ())), preferred_element_type=F32) * (HEAD_DIM ** -0.5) + bias
    snk = jnp.concatenate(
        [jnp.broadcast_to(jnp.max(sink_ref[GRP * kh + g:GRP * kh + g + 1, :], axis=1, keepdims=True), (BLK, 1))
         for g in range(GRP)], axis=0)
    m = jnp.maximum(jnp.max(s, axis=-1, keepdims=True), snk)
    p = jnp.exp(s - m)
    p_snk = jnp.exp(snk - m)
    inv = 1.0 / (jnp.sum(p, axis=-1, keepdims=True) + p_snk)
    return qs, p * inv, p_snk * inv


def _attention_fwd(q, kd, vd, sink_b, n_lat, n_ctx, *, name):
    t_rows = q.shape[0]
    nb, ws, win, ctx = _attn_specs(n_lat, n_ctx)

    def body(q_ref, k0, k1, k2, kc, v0, v1, v2, vc, sink_ref, o_ref):
        j = pl.program_id(0)
        ws_j = ws(j)
        low = _low_half((BLK, LANES))
        bias = _attn_bias(j, ws_j, nb, n_ctx)
        for kh in range(N_KV):
            sl = slice(kh * LANES, (kh + 1) * LANES)
            kk = jnp.concatenate([k0[:, sl], k1[:, sl], k2[:, sl], kc[:, sl]], axis=0)
            vv = jnp.concatenate([v0[:, sl], v1[:, sl], v2[:, sl], vc[:, sl]], axis=0)
            _, p, _ = _attn_probs(q_ref, kk, kh, bias, sink_ref)
            o = jnp.dot(p.astype(BF16), vv, preferred_element_type=F32)
            for half in range(2):
                even = o[(2 * half) * BLK:(2 * half + 1) * BLK]
                odd = o[(2 * half + 1) * BLK:(2 * half + 2) * BLK]
                ch = 2 * kh + half
                o_ref[:, ch * LANES:(ch + 1) * LANES] = jnp.where(low, even, odd).astype(BF16)

    return pl.pallas_call(
        body, name=name, grid=(t_rows // BLK,),
        in_specs=[_row_spec(BLK, D)] + win + [ctx] + win + [ctx] + [_full_spec((N_HEADS, LANES))],
        out_specs=_row_spec(BLK, D), out_shape=jax.ShapeDtypeStruct((t_rows, D), BF16),
        compiler_params=_params(("parallel",)),
    )(q, kd, kd, kd, kd, vd, vd, vd, vd, sink_b)


def _attention_bwd(q, kd, vd, sink_b, dy, n_lat, n_ctx, *, name):
    t_rows = q.shape[0]
    nb, ws, win, ctx = _attn_specs(n_lat, n_ctx)
    n_steps = t_rows // BLK

    def body(q_ref, k0, k1, k2, kc, v0, v1, v2, vc, sink_ref, dy_ref, dq_ref, dk_hbm, dv_hbm, ds_ref,
             dk_acc, dv_acc, sem):
        j = pl.program_id(0)
        ws_j = ws(j)

        @pl.when(j == 0)
        def _():
            dk_acc[...] = jnp.zeros_like(dk_acc)
            dv_acc[...] = jnp.zeros_like(dv_acc)
            ds_ref[...] = jnp.zeros_like(ds_ref)

        low = _low_half((BLK, LANES))
        low_keys = _low_half((3 * BLK + n_ctx, LANES))
        win_start = pl.multiple_of(ws_j * BLK, BLK)
        scale = HEAD_DIM ** -0.5
        dk_heads, dv_heads = [], []
        bias = _attn_bias(j, ws_j, nb, n_ctx)
        for kh in range(N_KV):
            sl = slice(kh * LANES, (kh + 1) * LANES)
            kk = jnp.concatenate([k0[:, sl], k1[:, sl], k2[:, sl], kc[:, sl]], axis=0)
            vv = jnp.concatenate([v0[:, sl], v1[:, sl], v2[:, sl], vc[:, sl]], axis=0)
            qs, p, p_snk = _attn_probs(q_ref, kk, kh, bias, sink_ref)
            dos = []
            for g in range(GRP):
                h = GRP * kh + g
                chunk = dy_ref[:, (h // 2) * LANES:(h // 2 + 1) * LANES]
                dos.append(jnp.where(low if h % 2 == 0 else ~low, chunk, jnp.zeros_like(chunk)).astype(BF16))
            dos = jnp.concatenate(dos, axis=0)
            dp = lax.dot_general(dos, vv, (((1,), (1,)), ((), ())), preferred_element_type=F32)
            dsum = jnp.sum(p * dp, axis=-1, keepdims=True)
            ds = (p * (dp - dsum)).astype(BF16)
            for g in range(GRP):
                contrib = -jnp.sum(p_snk[g * BLK:(g + 1) * BLK] * dsum[g * BLK:(g + 1) * BLK], axis=0, keepdims=True)
                ds_ref[GRP * kh + g:GRP * kh + g + 1, :] += jnp.broadcast_to(contrib, (1, LANES))
            dqs = jnp.dot(ds, kk, preferred_element_type=F32) * scale
            for half in range(2):
                even = dqs[(2 * half) * BLK:(2 * half + 1) * BLK]
                odd = dqs[(2 * half + 1) * BLK:(2 * half + 2) * BLK]
                ch = 2 * kh + half
                dq_ref[:, ch * LANES:(ch + 1) * LANES] = jnp.where(low, even, odd)
            dkk = lax.dot_general(ds, qs, (((0,), (0,)), ((), ())), preferred_element_type=F32) * scale
            dvv = lax.dot_general(p.astype(BF16), dos, (((0,), (0,)), ((), ())), preferred_element_type=F32)
            dk_heads.append(dkk + pltpu.roll(dkk, HEAD_DIM, 1))
            dv_heads.append(dvv + pltpu.roll(dvv, HEAD_DIM, 1))
        for pair in range(N_KV // 2):
            sl = slice(pair * LANES, (pair + 1) * LANES)
            for acc, heads in ((dk_acc, dk_heads), (dv_acc, dv_heads)):
                both = jnp.where(low_keys, heads[2 * pair], heads[2 * pair + 1])
                acc[pl.ds(win_start, 3 * BLK), sl] += both[:3 * BLK]
                acc[n_lat:n_lat + n_ctx, sl] += both[3 * BLK:]

        @pl.when(j == n_steps - 1)
        def _():
            ck = pltpu.make_async_copy(dk_acc, dk_hbm, sem.at[0])
            cv = pltpu.make_async_copy(dv_acc, dv_hbm, sem.at[1])
            ck.start()
            cv.start()
            ck.wait()
            cv.wait()

    return pl.pallas_call(
        body, name=name, grid=(n_steps,),
        in_specs=[_row_spec(BLK, D)] + win + [ctx] + win + [ctx] + [_full_spec((N_HEADS, LANES)), _row_spec(BLK, D)],
        out_specs=[_row_spec(BLK, D), ANY, ANY, _full_spec((N_HEADS, LANES))],
        out_shape=[jax.ShapeDtypeStruct((t_rows, D), F32), jax.ShapeDtypeStruct((t_rows, KV_W), F32),
                   jax.ShapeDtypeStruct((t_rows, KV_W), F32), jax.ShapeDtypeStruct((N_HEADS, LANES), F32)],
        scratch_shapes=[pltpu.VMEM((t_rows, KV_W), F32), pltpu.VMEM((t_rows, KV_W), F32),
                        pltpu.SemaphoreType.DMA((2,))],
        compiler_params=_params(("arbitrary",)),
    )(q, kd, kd, kd, kd, vd, vd, vd, vd, sink_b, dy)


_GELU_K = math.sqrt(2.0 / math.pi)


def _gelu(v):
    return jax.nn.gelu(v)


def _dgelu(v):
    t = jnp.tanh(_GELU_K * (v + 0.044715 * v * v * v))
    return 0.5 * (1.0 + t) + 0.5 * v * (1.0 - t * t) * _GELU_K * (1.0 + 3.0 * 0.044715 * v * v)


def _gating_fwd(z_a, w_s, b_sb, g_v, *, name):
    t_rows = z_a.shape[0]

    def body(z_ref, w_ref, b_ref, g_ref, o_ref):
        u = _gelu(z_ref[:, :D])
        v = _gelu(z_ref[:, D:])
        vn = v * lax.rsqrt(jnp.mean(v * v, axis=-1, keepdims=True) + EPS) * g_ref[...]
        for g in range(A_GROUPS):
            sl = slice(g * LANES, (g + 1) * LANES)
            mixed = jnp.dot(w_ref[g].astype(BF16), vn[:, sl].astype(BF16), preferred_element_type=F32) + b_ref[g]
            o_ref[:, sl] = (u[:, sl] * mixed).astype(BF16)

    return pl.pallas_call(
        body, name=name, grid=(t_rows // BLK,),
        in_specs=[_row_spec(BLK, A_COLS), _full_spec((A_GROUPS, BLK, BLK)), _full_spec((A_GROUPS, BLK, LANES)),
                  _full_spec((1, D))],
        out_specs=_row_spec(BLK, D), out_shape=jax.ShapeDtypeStruct((t_rows, D), BF16),
        compiler_params=_params(("parallel",)),
    )(z_a, w_s, b_sb, g_v)


def _gating_bwd(z_a, dy, w_s, b_sb, g_v, *, name):
    t_rows = z_a.shape[0]

    def body(z_ref, dy_ref, w_ref, b_ref, g_ref, dz_ref, dw_ref, db_ref, st_ref):
        i = pl.program_id(0)

        @pl.when(i == 0)
        def _():
            dw_ref[...] = jnp.zeros_like(dw_ref)
            db_ref[...] = jnp.zeros_like(db_ref)
            st_ref[...] = jnp.zeros_like(st_ref)

        zu = z_ref[:, :D]
        zv = z_ref[:, D:]
        u = _gelu(zu)
        v = _gelu(zv)
        gv = g_ref[...]
        rstd = lax.rsqrt(jnp.mean(v * v, axis=-1, keepdims=True) + EPS)
        vh = v * rstd
        vn = vh * gv
        dyv = dy_ref[...]
        dvn = []
        for g in range(A_GROUPS):
            sl = slice(g * LANES, (g + 1) * LANES)
            wg = w_ref[g].astype(BF16)
            vg = vn[:, sl].astype(BF16)
            mixed = jnp.dot(wg, vg, preferred_element_type=F32) + b_ref[g]
            dz_ref[:, sl] = (dyv[:, sl] * mixed * _dgelu(zu[:, sl])).astype(BF16)
            dmixed = dyv[:, sl] * u[:, sl]
            dmb = dmixed.astype(BF16)
            dvn.append(lax.dot_general(wg, dmb, (((0,), (0,)), ((), ())), preferred_element_type=F32))
            dw_ref[g] += lax.dot_general(dmb, vg, (((1,), (1,)), ((), ())), preferred_element_type=F32)
            db_ref[g] += jnp.broadcast_to(jnp.sum(dmixed, axis=-1, keepdims=True), (BLK, LANES))
        dvn = jnp.concatenate(dvn, axis=1)
        st_ref[...] += _rows8([(0, _colsum(dvn * vh))], D)
        e = dvn * gv
        dv = rstd * (e - vh * jnp.mean(e * vh, axis=-1, keepdims=True))
        dz_ref[:, D:] = (dv * _dgelu(zv)).astype(BF16)

    return pl.pallas_call(
        body, name=name, grid=(t_rows // BLK,),
        in_specs=[_row_spec(BLK, A_COLS), _row_spec(BLK, D), _full_spec((A_GROUPS, BLK, BLK)),
                  _full_spec((A_GROUPS, BLK, LANES)), _full_spec((1, D))],
        out_specs=[_row_spec(BLK, A_COLS), _full_spec((A_GROUPS, BLK, BLK)), _full_spec((A_GROUPS, BLK, LANES)),
                   _full_spec((SUBLANES, D))],
        out_shape=[jax.ShapeDtypeStruct((t_rows, A_COLS), BF16), jax.ShapeDtypeStruct((A_GROUPS, BLK, BLK), F32),
                   jax.ShapeDtypeStruct((A_GROUPS, BLK, LANES), F32), jax.ShapeDtypeStruct((SUBLANES, D), F32)],
        compiler_params=_params(("arbitrary",)),
    )(z_a, dy, w_s, b_sb, g_v)


def _sconv_fwd(z_b, w8, n_lat, *, name):
    t_rows = z_b.shape[0]
    tm = 256
    prev, nxt = _halo_specs(tm, B_COLS, t_rows)

    def body(z_ref, zp_ref, zn_ref, w_ref, o_ref):
        p = z_ref[:, D:2 * D] * z_ref[:, 2 * D:]
        pp = zp_ref[:, D:2 * D] * zp_ref[:, 2 * D:]
        pn = zn_ref[:, D:2 * D] * zn_ref[:, 2 * D:]
        up, dn = _shift_rows(p, pp, pn, n_lat, t_rows, tm)
        conv = w_ref[0:1, :] * up + w_ref[1:2, :] * p + w_ref[2:3, :] * dn
        o_ref[...] = (z_ref[:, :D] * conv).astype(BF16)

    return pl.pallas_call(
        body, name=name, grid=(t_rows // tm,),
        in_specs=[_row_spec(tm, B_COLS), prev, nxt, _full_spec((SUBLANES, D))],
        out_specs=_row_spec(tm, D), out_shape=jax.ShapeDtypeStruct((t_rows, D), BF16),
        compiler_params=_params(("parallel",)),
    )(z_b, z_b, z_b, w8)


def _sconv_bwd(z_b, dy, w8, n_lat, *, name):
    t_rows = z_b.shape[0]
    tm = 256
    prev, nxt = _halo_specs(tm, B_COLS, t_rows)
    dprev, dnxt = _halo_specs(tm, D, t_rows)

    def body(z_ref, zp_ref, zn_ref, dy_ref, dyp_ref, dyn_ref, w_ref, dz_ref, st_ref):
        i = pl.program_id(0)
        bg, cg, hb = z_ref[:, :D], z_ref[:, D:2 * D], z_ref[:, 2 * D:]
        p = cg * hb
        pp = zp_ref[:, D:2 * D] * zp_ref[:, 2 * D:]
        pn = zn_ref[:, D:2 * D] * zn_ref[:, 2 * D:]
        up, dn = _shift_rows(p, pp, pn, n_lat, t_rows, tm)
        w0, w1, w2 = w_ref[0:1, :], w_ref[1:2, :], w_ref[2:3, :]
        conv = w0 * up + w1 * p + w2 * dn
        dyv = dy_ref[...]
        dz_ref[:, :D] = (dyv * conv).astype(BF16)
        dcv = dyv * bg
        dcv_up, dcv_dn = _shift_rows(dcv, dyp_ref[...] * zp_ref[:, :D], dyn_ref[...] * zn_ref[:, :D], n_lat, t_rows, tm)
        dp = w0 * dcv_dn + w1 * dcv + w2 * dcv_up
        dz_ref[:, D:2 * D] = (dp * hb).astype(BF16)
        dz_ref[:, 2 * D:] = (dp * cg).astype(BF16)
        upd = _rows8([(0, _colsum(dcv * up)), (1, _colsum(dcv * p)), (2, _colsum(dcv * dn))], D)

        @pl.when(i == 0)
        def _():
            st_ref[...] = upd

        @pl.when(i > 0)
        def _():
            st_ref[...] += upd

    return pl.pallas_call(
        body, name=name, grid=(t_rows // tm,),
        in_specs=[_row_spec(tm, B_COLS), prev, nxt, _row_spec(tm, D), dprev, dnxt, _full_spec((SUBLANES, D))],
        out_specs=[_row_spec(tm, B_COLS), _full_spec((SUBLANES, D))],
        out_shape=[jax.ShapeDtypeStruct((t_rows, B_COLS), BF16), jax.ShapeDtypeStruct((SUBLANES, D), F32)],
        compiler_params=_params(("arbitrary",)),
    )(z_b, z_b, z_b, dy, dy, dy, w8)


def _merge_fwd(t0, t1, t2, z_g, b_gate, *, name):
    t_rows = t0.shape[0]
    tm = 256

    def body(t0_ref, t1_ref, t2_ref, z_ref, b_ref, o_ref):
        acc = None
        for k, t_ref in enumerate((t0_ref, t1_ref, t2_ref)):
            gate = jax.nn.sigmoid(z_ref[:, k * D:(k + 1) * D] + b_ref[:, k * D:(k + 1) * D])
            term = gate * t_ref[...]
            acc = term if acc is None else acc + term
        o_ref[...] = acc.astype(BF16)

    return pl.pallas_call(
        body, name=name, grid=(t_rows // tm,),
        in_specs=[_row_spec(tm, D)] * 3 + [_row_spec(tm, G_COLS), _full_spec((1, G_COLS))],
        out_specs=_row_spec(tm, D), out_shape=jax.ShapeDtypeStruct((t_rows, D), BF16),
        compiler_params=_params(("parallel",)),
    )(t0, t1, t2, z_g, b_gate)


def _merge_bwd(dmerged, t0, t1, t2, z_g, b_gate, *, name):
    t_rows = t0.shape[0]
    tm = 256

    def body(dm_ref, t0_ref, t1_ref, t2_ref, z_ref, b_ref, d0_ref, d1_ref, d2_ref, dz_ref, st_ref):
        i = pl.program_id(0)
        dm = dm_ref[...]
        sums = []
        for k, (t_ref, d_ref) in enumerate(((t0_ref, d0_ref), (t1_ref, d1_ref), (t2_ref, d2_ref))):
            gate = jax.nn.sigmoid(z_ref[:, k * D:(k + 1) * D] + b_ref[:, k * D:(k + 1) * D])
            d_ref[...] = (dm * gate).astype(BF16)
            dzg = dm * t_ref[...] * gate * (1.0 - gate)
            dz_ref[:, k * D:(k + 1) * D] = dzg.astype(BF16)
            sums.append(_colsum(dzg))
        upd = _rows8([(0, jnp.concatenate(sums, axis=1))], G_COLS)

        @pl.when(i == 0)
        def _():
            st_ref[...] = upd

        @pl.when(i > 0)
        def _():
            st_ref[...] += upd

    return pl.pallas_call(
        body, name=name, grid=(t_rows // tm,),
        in_specs=[_row_spec(tm, D)] * 4 + [_row_spec(tm, G_COLS), _full_spec((1, G_COLS))],
        out_specs=[_row_spec(tm, D)] * 3 + [_row_spec(tm, G_COLS), _full_spec((SUBLANES, G_COLS))],
        out_shape=[jax.ShapeDtypeStruct((t_rows, D), BF16)] * 3
        + [jax.ShapeDtypeStruct((t_rows, G_COLS), BF16), jax.ShapeDtypeStruct((SUBLANES, G_COLS), F32)],
        compiler_params=_params(("arbitrary",)),
    )(dmerged, t0, t1, t2, z_g, b_gate)


def _ffn_mid_fwd(up, w8, n_lat, *, name):
    t_rows = up.shape[0]
    tm = 128
    prev, nxt = _halo_specs(tm, D_FF, t_rows)

    def body(a_ref, ap_ref, an_ref, g_ref, w_ref, cv_ref, f_ref):
        a = a_ref[...]
        au, ad = _shift_rows(a, ap_ref[...], an_ref[...], n_lat, t_rows, tm)
        cv = w_ref[0:1, :] * au + w_ref[1:2, :] * a + w_ref[2:3, :] * ad
        cv_ref[...] = cv
        f_ref[...] = (_silu(cv) * g_ref[...]).astype(BF16)

    return pl.pallas_call(
        body, name=name, grid=(t_rows // tm,),
        in_specs=[_row_spec(tm, D_FF), prev, nxt, _row_spec(tm, D_FF, 1), _full_spec((SUBLANES, D_FF))],
        out_specs=[_row_spec(tm, D_FF), _row_spec(tm, D_FF)],
        out_shape=[jax.ShapeDtypeStruct((t_rows, D_FF), F32), jax.ShapeDtypeStruct((t_rows, D_FF), BF16)],
        compiler_params=_params(("parallel",)),
    )(up, up, up, up, w8)


def _ffn_mid_bwd(up, cv, df, w8, n_lat, *, name):
    t_rows = up.shape[0]
    tm = 128
    prev, nxt = _halo_specs(tm, D_FF, t_rows)
    gprev, gnxt = _halo_specs(tm, D_FF, t_rows, 1)

    def body(a_ref, ap_ref, an_ref, g_ref, gp_ref, gn_ref, cv_ref, cp_ref, cn_ref, df_ref, dfp_ref, dfn_ref,
             w_ref, o_ref, st_ref):
        i = pl.program_id(0)
        a = a_ref[...]
        au, ad = _shift_rows(a, ap_ref[...], an_ref[...], n_lat, t_rows, tm)
        cvv = cv_ref[...]
        dfv = df_ref[...]
        o_ref[:, D_FF:] = (dfv * _silu(cvv)).astype(BF16)
        dcv = dfv * g_ref[...] * _dsilu(cvv)
        dcv_p = dfp_ref[...] * gp_ref[...] * _dsilu(cp_ref[...])
        dcv_n = dfn_ref[...] * gn_ref[...] * _dsilu(cn_ref[...])
        du, dd = _shift_rows(dcv, dcv_p, dcv_n, n_lat, t_rows, tm)
        o_ref[:, :D_FF] = (w_ref[0:1, :] * dd + w_ref[1:2, :] * dcv + w_ref[2:3, :] * du).astype(BF16)
        upd = _rows8([(0, _colsum(dcv * au)), (1, _colsum(dcv * a)), (2, _colsum(dcv * ad))], D_FF)

        @pl.when(i == 0)
        def _():
            st_ref[...] = upd

        @pl.when(i > 0)
        def _():
            st_ref[...] += upd

    row = _row_spec(tm, D_FF)
    return pl.pallas_call(
        body, name=name, grid=(t_rows // tm,),
        in_specs=[row, prev, nxt, _row_spec(tm, D_FF, 1), gprev, gnxt, row, prev, nxt, row, prev, nxt,
                  _full_spec((SUBLANES, D_FF))],
        out_specs=[_row_spec(tm, 2 * D_FF), _full_spec((SUBLANES, D_FF))],
        out_shape=[jax.ShapeDtypeStruct((t_rows, 2 * D_FF), BF16), jax.ShapeDtypeStruct((SUBLANES, D_FF), F32)],
        compiler_params=_params(("arbitrary",)),
    )(up, up, up, up, up, up, cv, cv, cv, df, df, df, w8)


def _loss_head(x, target, g_final, n_lat, *, name):
    t_rows = x.shape[0]
    tm = 256
    last = n_lat // tm - 1

    def body(x_ref, t_ref, g_ref, dx_ref, st_ref):
        i = pl.program_id(0)
        is_ctx = i * tm >= n_lat
        xv = x_ref[...]
        gv = g_ref[...]
        rstd = lax.rsqrt(jnp.mean(xv * xv, axis=-1, keepdims=True) + EPS)
        rn = xv * rstd
        err = rn * gv - t_ref[...]
        dy = err / D
        e = dy * gv
        dx = rstd * (e - rn * jnp.mean(e * rn, axis=-1, keepdims=True))
        dx_ref[...] = jnp.where(is_ctx, 0.0, dx)
        keep = jnp.where(is_ctx, 0.0, 1.0)
        upd = _rows8([(0, keep * _colsum(dy * rn)), (1, keep * _colsum(err * err))], D)

        @pl.when(i == 0)
        def _():
            st_ref[...] = upd

        @pl.when(i > 0)
        def _():
            st_ref[...] += upd

    return pl.pallas_call(
        body, name=name, grid=(t_rows // tm,),
        in_specs=[_row_spec(tm, D), pl.BlockSpec((tm, D), lambda i: (jnp.minimum(i, last), 0)), _full_spec((1, D))],
        out_specs=[_row_spec(tm, D), _full_spec((SUBLANES, D))],
        out_shape=[jax.ShapeDtypeStruct((t_rows, D), F32), jax.ShapeDtypeStruct((SUBLANES, D), F32)],
        compiler_params=_params(("arbitrary",)),
    )(x, target, g_final)


def _sum_slabs(x, out_dtype, *, name):
    n_slabs, rows, cols = x.shape
    tm = _pick(rows, (256, 192, 128, 64, 32, 24, 16, 8))

    def body(x_ref, o_ref):
        acc = x_ref[0].astype(F32)
        for s in range(1, n_slabs):
            acc = acc + x_ref[s].astype(F32)
        o_ref[...] = acc.astype(o_ref.dtype)

    return pl.pallas_call(
        body, name=name, grid=(rows // tm,),
        in_specs=[pl.BlockSpec((n_slabs, tm, cols), lambda i: (0, i, 0))],
        out_specs=pl.BlockSpec((tm, cols), lambda i: (i, 0)),
        out_shape=jax.ShapeDtypeStruct((rows, cols), out_dtype),
        compiler_params=_params(("parallel",)),
    )(x)


def _add_pairs(layer, a, b, *, name):
    n_slabs, rows, cols = b.shape
    tm = _pick(rows, (256, 192, 128, 64, 32, 16))

    def body(layer_ref, a_ref, b_ref, o_ref):
        o_ref[...] = (a_ref[0].astype(F32) + b_ref[...].astype(F32)).astype(BF16)

    spec = pl.BlockSpec((1, tm, cols), lambda s, i, layer_ref: (s, i, 0))
    a_spec = pl.BlockSpec((1, 1, tm, cols), lambda s, i, layer_ref: (layer_ref[0], s, i, 0))
    return pl.pallas_call(
        body, name=name,
        grid_spec=pltpu.PrefetchScalarGridSpec(num_scalar_prefetch=1, grid=(n_slabs, rows // tm),
                                               in_specs=[a_spec, spec], out_specs=spec),
        out_shape=jax.ShapeDtypeStruct(b.shape, BF16), compiler_params=_params(("parallel", "parallel")),
    )(layer, a, b)


def _adamw(w, g, m, v, *, name):
    rows, cols = w.shape
    tm = _pick(rows, (128, 64, 32, 16, 8))

    def body(w_ref, g_ref, m_ref, v_ref, d_ref, nm_ref, nv_ref):
        gv = g_ref[...]
        nm = ADAM_B1 * m_ref[...] + (1.0 - ADAM_B1) * gv
        nv = ADAM_B2 * v_ref[...] + (1.0 - ADAM_B2) * jnp.square(gv)
        m_hat = nm / (1.0 - ADAM_B1 ** ADAM_STEP)
        v_hat = nv / (1.0 - ADAM_B2 ** ADAM_STEP)
        d_ref[...] = -ADAM_LR * (m_hat / (jnp.sqrt(v_hat) + ADAM_EPS) + ADAM_WD * w_ref[...])
        nm_ref[...] = nm
        nv_ref[...] = nv

    spec = pl.BlockSpec((tm, cols), lambda i: (i, 0))
    shape = jax.ShapeDtypeStruct((rows, cols), F32)
    return pl.pallas_call(
        body, name=name, grid=(rows // tm,), in_specs=[spec] * 4, out_specs=[spec] * 3, out_shape=[shape] * 3,
        compiler_params=_params(("parallel",)),
    )(w, g, m, v)


def _place():
    x, y, c = lax.axis_index("x"), lax.axis_index("y"), lax.axis_index("c")
    chips = [(1 - x, y), (x, 1 - y), (1 - x, 1 - y)]
    return x, y, c, chips


def _remote(src, dst, send_sems, recv_sems, k, to):
    return pltpu.make_async_remote_copy(src_ref=src, dst_ref=dst, send_sem=send_sems.at[k], recv_sem=recv_sems.at[k],
                                        device_id=to, device_id_type=MESH)


COPY_CHUNKS = 4


def _chunks(ref, n):
    step = ref.shape[0] // n
    tile_rows = SUBLANES if ref.dtype == F32 else 2 * SUBLANES
    assert step * n == ref.shape[0] and step % tile_rows == 0, (ref.shape, n)
    return [ref.at[pl.ds(k * step, step)] for k in range(n)]


def _gather_weights(pack):
    _, rows, cols = pack.shape
    n_ch = COPY_CHUNKS

    def body(p_ref, o_ref, send_sems, recv_sems, loc_sems):
        x, y, c, chips = _place()
        me = 2 * x + y
        sib = (x, y, 1 - c)
        local = [pltpu.make_async_copy(src, dst, loc_sems.at[l * n_ch + k])
                 for l in range(DEPTH)
                 for k, (src, dst) in enumerate(zip(_chunks(p_ref.at[l], n_ch), _chunks(o_ref.at[l, me], n_ch)))]
        for cp in local:
            cp.start()
        mine = _chunks(p_ref.at[c], n_ch)
        first = [_remote(mine[k], _chunks(o_ref.at[c, me], n_ch)[k], send_sems, recv_sems, j * n_ch + k, (*chip, c))
                 for j, chip in enumerate(chips) for k in range(n_ch)]
        for cp in first:
            cp.start()
        slots = [2 * cx + cy for cx, cy in chips]
        base = len(chips) * n_ch
        passed = []
        for j, s in enumerate(slots):
            for k, landed in enumerate(_chunks(o_ref.at[c, s], n_ch)):
                _remote(mine[k], landed, send_sems, recv_sems, j * n_ch + k, sib).wait_recv()
                fwd = _remote(landed, landed, send_sems, recv_sems, base + j * n_ch + k, sib)
                fwd.start()
                passed.append(fwd)
        for j, s in enumerate(slots):
            for k, landed in enumerate(_chunks(o_ref.at[1 - c, s], n_ch)):
                _remote(mine[k], landed, send_sems, recv_sems, base + j * n_ch + k, sib).wait_recv()
        for cp in first + passed:
            cp.wait_send()
        for cp in local:
            cp.wait()

    n_remote = 2 * (N_CHIPS - 1) * n_ch
    return pl.pallas_call(
        body, name="gather_weights", in_specs=[ANY], out_specs=ANY,
        out_shape=jax.ShapeDtypeStruct((DEPTH, N_CHIPS, rows, cols), pack.dtype),
        scratch_shapes=[pltpu.SemaphoreType.DMA((n_remote,)), pltpu.SemaphoreType.DMA((n_remote,)),
                        pltpu.SemaphoreType.DMA((DEPTH * n_ch,))],
    )(pack)


def _sibling_reduce_start(g_pack):
    _, n_slabs, rows, cols = g_pack.shape
    n_ch = COPY_CHUNKS

    def body(g_ref, got_ref, send_sems, recv_sems):
        x, y, c, _ = _place()
        sib = (x, y, 1 - c)
        swaps = [_remote(src, dst, send_sems, recv_sems, s * n_ch + k, sib)
                 for s in range(n_slabs)
                 for k, (src, dst) in enumerate(zip(_chunks(g_ref.at[1 - c, s], n_ch), _chunks(got_ref.at[s], n_ch)))]
        for cp in swaps:
            cp.start()
        for cp in swaps:
            cp.wait_recv()
        for cp in swaps:
            cp.wait_send()

    return pl.pallas_call(
        body, name="grad_sibling_swap", in_specs=[ANY], out_specs=ANY,
        out_shape=jax.ShapeDtypeStruct((n_slabs, rows, cols), g_pack.dtype),
        scratch_shapes=[pltpu.SemaphoreType.DMA((n_slabs * n_ch,)), pltpu.SemaphoreType.DMA((n_slabs * n_ch,))],
    )(g_pack)


def _chip_exchange(s):
    n_slabs, rows, cols = s.shape
    n_ch = COPY_CHUNKS

    def body(s_ref, o_ref, send_sems, recv_sems, loc_sems):
        x, y, c, chips = _place()
        me = 2 * x + y
        local = [pltpu.make_async_copy(src, dst, loc_sems.at[k])
                 for k, (src, dst) in enumerate(zip(_chunks(s_ref.at[me], n_ch), _chunks(o_ref.at[me], n_ch)))]
        for cp in local:
            cp.start()
        slots = [2 * cx + cy for cx, cy in chips]
        sends = [_remote(src, dst, send_sems, recv_sems, j * n_ch + k, (*chip, c))
                 for j, (chip, sl) in enumerate(zip(chips, slots))
                 for k, (src, dst) in enumerate(zip(_chunks(s_ref.at[sl], n_ch), _chunks(o_ref.at[me], n_ch)))]
        for cp in sends:
            cp.start()
        for j, sl in enumerate(slots):
            for k, landed in enumerate(_chunks(o_ref.at[sl], n_ch)):
                _remote(landed, landed, send_sems, recv_sems, j * n_ch + k, (x, y, c)).wait_recv()
        for cp in sends:
            cp.wait_send()
        for cp in local:
            cp.wait()

    n_remote = (N_CHIPS - 1) * n_ch
    return pl.pallas_call(
        body, name="grad_chip_exchange", in_specs=[ANY], out_specs=ANY, out_shape=jax.ShapeDtypeStruct(s.shape, s.dtype),
        scratch_shapes=[pltpu.SemaphoreType.DMA((n_remote,)), pltpu.SemaphoreType.DMA((n_remote,)),
                        pltpu.SemaphoreType.DMA((n_ch,))],
    )(s)


def _sibling_share(t):
    rows, cols = t.shape
    n_ch = 2 * COPY_CHUNKS

    def body(t_ref, o_ref, send_sems, recv_sems, loc_sems):
        x, y, c, _ = _place()
        sib = (x, y, 1 - c)
        parts = _chunks(t_ref, n_ch)
        local = [pltpu.make_async_copy(src, dst, loc_sems.at[k])
                 for k, (src, dst) in enumerate(zip(parts, _chunks(o_ref.at[c], n_ch)))]
        for cp in local:
            cp.start()
        sends = [_remote(src, dst, send_sems, recv_sems, k, sib)
                 for k, (src, dst) in enumerate(zip(parts, _chunks(o_ref.at[c], n_ch)))]
        for cp in sends:
            cp.start()
        for k, landed in enumerate(_chunks(o_ref.at[1 - c], n_ch)):
            _remote(parts[k], landed, send_sems, recv_sems, k, sib).wait_recv()
        for cp in sends:
            cp.wait_send()
        for cp in local:
            cp.wait()

    return pl.pallas_call(
        body, name="grad_sibling_share", in_specs=[ANY], out_specs=ANY,
        out_shape=jax.ShapeDtypeStruct((DEPTH, rows, cols), t.dtype),
        scratch_shapes=[pltpu.SemaphoreType.DMA((n_ch,)), pltpu.SemaphoreType.DMA((n_ch,)), pltpu.SemaphoreType.DMA((n_ch,))],
    )(t)


def _allgather8(v, *, name):
    rows, cols = v.shape

    def body(v_ref, o_ref, send_sems, recv_sems, loc_sem):
        x, y, c, chips = _place()
        sib = (x, y, 1 - c)

        def slot(px, py, pc):
            return o_ref.at[4 * px + 2 * py + pc]

        local = pltpu.make_async_copy(v_ref, slot(x, y, c), loc_sem.at[0])
        local.start()
        first = [_remote(v_ref, slot(x, y, c), send_sems, recv_sems, 0, sib)]
        first += [_remote(v_ref, slot(x, y, c), send_sems, recv_sems, 1 + j, (*chip, c)) for j, chip in enumerate(chips)]
        for cp in first:
            cp.start()
        passed = [_remote(slot(*chip, c), slot(*chip, c), send_sems, recv_sems, 4 + j, sib)
                  for j, chip in enumerate(chips)]
        for j, chip in enumerate(chips):
            _remote(v_ref, slot(*chip, c), send_sems, recv_sems, 1 + j, sib).wait_recv()
            passed[j].start()
        _remote(v_ref, slot(x, y, 1 - c), send_sems, recv_sems, 0, sib).wait_recv()
        for j, chip in enumerate(chips):
            _remote(v_ref, slot(*chip, 1 - c), send_sems, recv_sems, 4 + j, sib).wait_recv()
        for cp in first + passed:
            cp.wait_send()
        local.wait()

    return pl.pallas_call(
        body, name=name, in_specs=[ANY], out_specs=ANY, out_shape=jax.ShapeDtypeStruct((N_DEV, rows, cols), v.dtype),
        scratch_shapes=[pltpu.SemaphoreType.DMA((7,)), pltpu.SemaphoreType.DMA((7,)), pltpu.SemaphoreType.DMA((1,))],
    )(v)


_BIG = (("w_mod", (D, 6 * D), 1), ("w_in", (D, IN_W), 1), ("w_branch", (3 * D, D), None), ("w_out", (D, D), 0),
        ("w_up", (D, 2 * D_FF), 1), ("w_down", (D_FF, D), 0))


def _shard_rows(name):
    shape = dict((n, s) for n, s, _ in _BIG)[name]
    return shape[0] * shape[1] // N_CHIPS // D


PACK_ROWS = sum(_shard_rows(n) for n, _, _ in _BIG)


def _pack_shards(shards):
    return jnp.concatenate([shards[n].reshape(DEPTH, _shard_rows(n), D) for n, _, _ in _BIG], axis=1)


def _unpack_full(gathered, layer):
    out, off = {}, 0
    for name, shape, axis in _BIG:
        r = _shard_rows(name)
        blk = gathered[layer, :, off:off + r, :]
        off += r
        if name == "w_branch":
            out[name] = blk.reshape(N_CHIPS, 3, D // N_CHIPS, D).transpose(1, 0, 2, 3).reshape(3, D, D)
        elif axis == 0:
            out[name] = blk.reshape(shape)
        else:
            out[name] = blk.reshape(N_CHIPS, shape[0], shape[1] // N_CHIPS).transpose(1, 0, 2).reshape(shape)
    return out


def _pack_grads(grads):
    parts = []
    for name, shape, axis in _BIG:
        g = grads[name]
        r = _shard_rows(name)
        if name == "w_branch":
            g = g.reshape(3, N_CHIPS, D // N_CHIPS, D).transpose(1, 0, 2, 3)
        elif axis == 1:
            g = g.reshape(shape[0], N_CHIPS, shape[1] // N_CHIPS).transpose(1, 0, 2)
        parts.append(g.reshape(N_CHIPS, r, D))
    return jnp.concatenate(parts, axis=1)


def _unpack_shards(total, like):
    out, off = {}, 0
    for name, _, _ in _BIG:
        r = _shard_rows(name)
        out[name] = total[:, off:off + r, :].reshape(like[name].shape)
        off += r
    return out


def _pad_rows(v, rows):
    return jnp.concatenate([v, jnp.zeros((rows - v.shape[0],) + v.shape[1:], v.dtype)], axis=0)


def _local_step(x_tok, target, c_vec, c_ctx, wfull, small, n_lat, n_ctx):
    cos_t, sin_t = _rope_tables(n_lat, n_ctx)
    a_in = _pad_rows(jnp.stack([c_vec, c_ctx]), LANES)
    a128 = _small(_silu, (LANES, D), a_in, name="cond_silu")

    saved = []
    xs = x_tok
    for l in range(DEPTH):
        w = wfull[l]
        sm = {k: v[l] for k, v in small.items()}
        mod128 = _mm(a128, w["w_mod"], name=f"mod{l}")
        mod8 = _small(lambda m, b: m + b, (SUBLANES, 6 * D), mod128[:SUBLANES], sm["b_mod"][None, :], name=f"mod_bias{l}")
        g_mix = sm["g_mix"][None, :]
        g_ffn = sm["g_ffn"][None, :]
        g_v = sm["g_v"][None, :]
        b_gate = sm["b_gate"][None, :]
        sink_b = jnp.broadcast_to(sm["sink"][:, None], (N_HEADS, LANES))
        b_sb = jnp.broadcast_to(sm["b_spatial"][:, :, None], (A_GROUPS, BLK, LANES))
        w_sconv8 = _pad_rows(sm["w_sconv"], SUBLANES)
        w_fconv8 = _pad_rows(sm["w_fconv"], SUBLANES)
        w_in = w["w_in"]
        w_seg = [w_in[:, SEG[k]:SEG[k + 1]] for k in range(4)]

        h = _norm_mod_fwd(xs, g_mix, mod8, 0, 1, n_lat, name=f"norm1_{l}")
        z_qkv, z_a, z_b, z_g = [_mm(h, w_seg[k], name=f"in_proj{k}_{l}") for k in range(4)]
        q, kd, vd = _qkv_prep(z_qkv, cos_t, sin_t, name=f"qkv_prep{l}")
        y_attn = _attention_fwd(q, kd, vd, sink_b, n_lat, n_ctx, name=f"attn{l}")
        y_a = _gating_fwd(z_a, sm["w_spatial"], b_sb, g_v, name=f"gating{l}")
        y_b = _sconv_fwd(z_b, w_sconv8, n_lat, name=f"sconv{l}")
        ys = (y_attn, y_a, y_b)
        ts = [_mm(ys[k], w["w_branch"][k], name=f"branch{k}_{l}") for k in range(3)]
        merged = _merge_fwd(*ts, z_g, b_gate, name=f"merge{l}")
        mix_out = _mm(merged, w["w_out"], name=f"out_proj{l}")
        x1 = _residual_fwd(xs, mix_out, mod8, 2, n_lat, name=f"res1_{l}")
        h2 = _norm_mod_fwd(x1, g_ffn, mod8, 3, 4, n_lat, name=f"norm2_{l}")
        up = _mm(h2, w["w_up"], name=f"up_proj{l}")
        cv, f = _ffn_mid_fwd(up, w_fconv8, n_lat, name=f"ffn_mid{l}")
        ffn_out = _mm(f, w["w_down"], name=f"down_proj{l}")
        x2 = _residual_fwd(x1, ffn_out, mod8, 5, n_lat, name=f"res2_{l}")
        saved.append(dict(x0=xs, mod8=mod8, h=h, z_qkv=z_qkv, z_a=z_a, z_b=z_b, z_g=z_g, q=q, kd=kd, vd=vd, ys=ys, ts=ts,
                          merged=merged, mix_out=mix_out, x1=x1, h2=h2, up=up, cv=cv, f=f, ffn_out=ffn_out, w_seg=w_seg,
                          g_mix=g_mix, g_ffn=g_ffn, g_v=g_v, b_gate=b_gate, sink_b=sink_b, b_sb=b_sb,
                          w_sconv8=w_sconv8, w_fconv8=w_fconv8, w_spatial=sm["w_spatial"]))
        xs = x2

    dx, st = _loss_head(xs, target, small["g_final"][None, :], n_lat, name="loss_head")
    sq_err = st[1]
    sgrads = {k: [None] * DEPTH for k in ("b_mod", "g_mix", "b_gate", "sink", "w_spatial", "b_spatial", "g_v",
                                           "w_sconv", "g_ffn", "w_fconv")}
    sgrads["g_final"] = st[0]
    wgrads = [None] * DEPTH
    d_a128 = []

    for l in reversed(range(DEPTH)):
        s = saved[l]
        w = wfull[l]
        mod8 = s["mod8"]
        d_ffn, st_gt2 = _residual_bwd(dx, s["ffn_out"], mod8, 5, n_lat, name=f"res2_bwd{l}")
        df = _mm(d_ffn, w["w_down"], tb=True, name=f"down_bwd_x{l}")
        g_down = _mm(s["f"], d_ffn, ta=True, out_dtype=BF16, name=f"down_bwd_w{l}")
        d_up, st_fc = _ffn_mid_bwd(s["up"], s["cv"], df, s["w_fconv8"], n_lat, name=f"ffn_mid_bwd{l}")
        dh2 = _mm(d_up, w["w_up"], tb=True, name=f"up_bwd_x{l}")
        g_up = _mm(s["h2"], d_up, ta=True, out_dtype=BF16, name=f"up_bwd_w{l}")
        dx1, st_n2 = _norm_mod_bwd(s["x1"], [dh2], dx, s["g_ffn"], mod8, 4, n_lat, name=f"norm2_bwd{l}")
        d_out, st_gt1 = _residual_bwd(dx1, s["mix_out"], mod8, 2, n_lat, name=f"res1_bwd{l}")
        d_merged = _mm(d_out, w["w_out"], tb=True, name=f"out_bwd_x{l}")
        g_out = _mm(s["merged"], d_out, ta=True, out_dtype=BF16, name=f"out_bwd_w{l}")
        dt0, dt1, dt2, dz_g, st_bg = _merge_bwd(d_merged, *s["ts"], s["z_g"], s["b_gate"], name=f"merge_bwd{l}")
        dts = (dt0, dt1, dt2)
        dys = [_mm(dts[k], w["w_branch"][k], tb=True, name=f"branch{k}_bwd_x{l}") for k in range(3)]
        g_branch = jnp.stack([_mm(s["ys"][k], dts[k], ta=True, out_dtype=BF16, name=f"branch{k}_bwd_w{l}")
                              for k in range(3)])
        dq, dkd, dvd, d_sink = _attention_bwd(s["q"], s["kd"], s["vd"], s["sink_b"], dys[0], n_lat, n_ctx,
                                              name=f"attn_bwd{l}")
        dz_qkv = _qkv_unprep(dq, dkd, dvd, cos_t, sin_t, name=f"qkv_unprep{l}")
        dz_a, d_ws, d_bs, st_gv = _gating_bwd(s["z_a"], dys[1], s["w_spatial"], s["b_sb"], s["g_v"], name=f"gating_bwd{l}")
        dz_b, st_sc = _sconv_bwd(s["z_b"], dys[2], s["w_sconv8"], n_lat, name=f"sconv_bwd{l}")
        dzs = (dz_qkv, dz_a, dz_b, dz_g)
        dh_parts = [_mm(dzs[k], s["w_seg"][k], tb=True, name=f"in_bwd_x{k}_{l}") for k in range(4)]
        g_in = jnp.concatenate([_mm(s["h"], dzs[k], ta=True, out_dtype=BF16, name=f"in_bwd_w{k}_{l}")
                                for k in range(4)], axis=1)
        dx, st_n1 = _norm_mod_bwd(s["x0"], dh_parts, dx1, s["g_mix"], mod8, 1, n_lat, name=f"norm1_bwd{l}")
        dmod = jnp.concatenate([st_n1[0:2], st_n1[2:4], st_gt1[0:2], st_n2[0:2], st_n2[2:4], st_gt2[0:2]], axis=1)
        dmod128 = _pad_rows(dmod, LANES)
        g_mod = _mm(a128, dmod128, ta=True, out_dtype=BF16, name=f"mod_bwd_w{l}")
        d_a128.append(_mm(dmod128, w["w_mod"], tb=True, name=f"mod_bwd_x{l}"))
        wgrads[l] = dict(w_mod=g_mod, w_in=g_in, w_branch=g_branch.reshape(3 * D, D), w_out=g_out, w_up=g_up,
                         w_down=g_down)
        sgrads["b_mod"][l] = dmod[0] + dmod[1]
        sgrads["g_mix"][l] = st_n1[4]
        sgrads["g_ffn"][l] = st_n2[4]
        sgrads["b_gate"][l] = st_bg[0]
        sgrads["sink"][l] = d_sink[:, 0]
        sgrads["w_spatial"][l] = d_ws
        sgrads["b_spatial"][l] = d_bs[:, :, 0]
        sgrads["g_v"][l] = st_gv[0]
        sgrads["w_sconv"][l] = st_sc[0:3]
        sgrads["w_fconv"][l] = st_fc[0:3]

    d_cond = _small(lambda a, b, cin: (a + b) * _dsilu(cin), (LANES, D), d_a128[0], d_a128[1], a_in, name="cond_bwd")
    out_small = {k: (jnp.stack(v) if isinstance(v, list) else v) for k, v in sgrads.items()}
    out_small["c_ctx"] = d_cond[1]
    return sq_err, dx, wgrads, out_small


_SMALL_ORDER = ("c_ctx", "b_mod", "g_mix", "b_gate", "sink", "w_spatial", "b_spatial", "g_v", "w_sconv", "g_ffn",
                "w_fconv", "g_final")


def _flat_pack(parts, width):
    flat = jnp.concatenate([p.reshape(-1).astype(F32) for p in parts])
    rows = -(-flat.shape[0] // (width * SUBLANES)) * SUBLANES
    flat = jnp.concatenate([flat, jnp.zeros((rows * width - flat.shape[0],), F32)])
    return flat.reshape(rows, width)


def _flat_unpack(packed, likes):
    flat = packed.reshape(-1)
    out, off = [], 0
    for like in likes:
        n = math.prod(like.shape)
        out.append(flat[off:off + n].reshape(like.shape))
        off += n
    return out


def kernel(x, c, ctx, c_ctx, w_mod, b_mod, g_mix, w_in, b_gate, sink, w_spatial, b_spatial, g_v, w_sconv, w_branch, w_out, g_ffn, w_up, w_fconv, w_down, g_final, loss_target, m_c_ctx, m_w_mod, m_b_mod, m_g_mix, m_w_in, m_b_gate, m_sink, m_w_spatial, m_b_spatial, m_g_v, m_w_sconv, m_w_branch, m_w_out, m_g_ffn, m_w_up, m_w_fconv, m_w_down, m_g_final, v_c_ctx, v_w_mod, v_b_mod, v_g_mix, v_w_in, v_b_gate, v_sink, v_w_spatial, v_b_spatial, v_g_v, v_w_sconv, v_w_branch, v_w_out, v_g_ffn, v_w_up, v_w_fconv, v_w_down, v_g_final):
    n_lat, n_ctx = x.shape[1], ctx.shape[1]
    chip = 2 * lax.axis_index("x") + lax.axis_index("y")
    weights = dict(c_ctx=c_ctx, w_mod=w_mod, b_mod=b_mod, g_mix=g_mix, w_in=w_in, b_gate=b_gate, sink=sink,
                   w_spatial=w_spatial, b_spatial=b_spatial, g_v=g_v, w_sconv=w_sconv, w_branch=w_branch, w_out=w_out,
                   g_ffn=g_ffn, w_up=w_up, w_fconv=w_fconv, w_down=w_down, g_final=g_final)
    m_in = dict(c_ctx=m_c_ctx, w_mod=m_w_mod, b_mod=m_b_mod, g_mix=m_g_mix, w_in=m_w_in, b_gate=m_b_gate, sink=m_sink,
                w_spatial=m_w_spatial, b_spatial=m_b_spatial, g_v=m_g_v, w_sconv=m_w_sconv, w_branch=m_w_branch,
                w_out=m_w_out, g_ffn=m_g_ffn, w_up=m_w_up, w_fconv=m_w_fconv, w_down=m_w_down, g_final=m_g_final)
    v_in = dict(c_ctx=v_c_ctx, w_mod=v_w_mod, b_mod=v_b_mod, g_mix=v_g_mix, w_in=v_w_in, b_gate=v_b_gate, sink=v_sink,
                w_spatial=v_w_spatial, b_spatial=v_b_spatial, g_v=v_g_v, w_sconv=v_w_sconv, w_branch=v_w_branch,
                w_out=v_w_out, g_ffn=v_g_ffn, w_up=v_w_up, w_fconv=v_w_fconv, w_down=v_w_down, g_final=v_g_final)
    big_names = [n for n, _, _ in _BIG]

    gathered = _gather_weights(_pack_shards({n: weights[n].astype(BF16) for n in big_names}))
    wfull = [_unpack_full(gathered, l) for l in range(DEPTH)]
    conv_pack = _flat_pack([w_sconv, w_fconv], LANES)
    conv_all = _allgather8(conv_pack, name="gather_conv_weights")
    conv_parts = [_flat_unpack(conv_all[2 * p], [w_sconv, w_fconv]) for p in range(N_CHIPS)]
    w_sconv_full = jnp.concatenate([cp[0] for cp in conv_parts], axis=-1)
    w_fconv_full = jnp.concatenate([cp[1] for cp in conv_parts], axis=-1)

    small = dict(b_mod=b_mod, g_mix=g_mix, b_gate=b_gate, sink=sink, w_spatial=w_spatial, b_spatial=b_spatial, g_v=g_v,
                 w_sconv=w_sconv_full, g_ffn=g_ffn, w_fconv=w_fconv_full, g_final=g_final)
    x_tok = jnp.concatenate([x[0], ctx[0]], axis=0)
    sq_err, dx, wgrads, sgrads = _local_step(x_tok, loss_target[0], c[0], c_ctx, wfull, small, n_lat, n_ctx)

    loss = lax.psum(0.5 * jnp.sum(sq_err) / D, ("x", "y", "c"))
    grad_x = dx[:n_lat][None]

    g_pack = jnp.stack([_pack_grads(wgrads[l]) for l in range(DEPTH)])
    got = _sibling_reduce_start(g_pack)
    my_layer = lax.axis_index("c").astype(jnp.int32).reshape(1)
    chip_sum = _add_pairs(my_layer, g_pack, got, name="grad_pair_sum")
    exchanged = _chip_exchange(chip_sum)
    total_c = _sum_slabs(exchanged, F32, name="grad_chip_sum")
    total = _sibling_share(total_c)
    big_grads = _unpack_shards(total, {n: weights[n] for n in big_names})

    s_likes = [sgrads[n] for n in _SMALL_ORDER]
    s_all = _allgather8(_flat_pack(s_likes, D), name="gather_small_grads")
    s_tot = _flat_unpack(_sum_slabs(s_all, F32, name="small_grad_sum"), s_likes)
    grads = dict(big_grads)
    for n, g in zip(_SMALL_ORDER, s_tot):
        grads[n] = g
    grads["w_sconv"] = lax.dynamic_slice_in_dim(grads["w_sconv"], chip * w_sconv.shape[-1], w_sconv.shape[-1], axis=2)
    grads["w_fconv"] = lax.dynamic_slice_in_dim(grads["w_fconv"], chip * w_fconv.shape[-1], w_fconv.shape[-1], axis=2)

    delta, new_m, new_v = {}, {}, {}
    for n in big_names:
        cols = weights[n].shape[-1]
        view = lambda a: a.reshape(-1, cols)
        d_, m_, v_ = _adamw(view(weights[n]), view(grads[n]), view(m_in[n]), view(v_in[n]), name=f"adamw_{n}")
        delta[n], new_m[n], new_v[n] = (t.reshape(weights[n].shape) for t in (d_, m_, v_))
    likes = [weights[n] for n in _SMALL_ORDER]
    packs = [_flat_pack([src[n] for n in _SMALL_ORDER], LANES) for src in (weights, grads, m_in, v_in)]
    outs = _adamw(*packs, name="adamw_small")
    for dst, packed in zip((delta, new_m, new_v), outs):
        for n, val in zip(_SMALL_ORDER, _flat_unpack(packed, likes)):
            dst[n] = val

    order = ("c_ctx", "w_mod", "b_mod", "g_mix", "w_in", "b_gate", "sink", "w_spatial", "b_spatial", "g_v", "w_sconv",
             "w_branch", "w_out", "g_ffn", "w_up", "w_fconv", "w_down", "g_final")
    return (loss, grad_x, *[grads[n] for n in order], *[delta[n] for n in order], *[new_m[n] for n in order],
            *[new_v[n] for n in order])
```

```python
import functools
import math

import jax
import jax.numpy as jnp
from jax import lax
from jax.experimental import pallas as pl
from jax.experimental.pallas import tpu as pltpu

F32 = jnp.float32
BF16 = jnp.bfloat16

D = 1024
DEPTH = 2
GRID_W = 64
N_HEADS = 16
N_KV = 4
GRP = N_HEADS // N_KV
HEAD_DIM = 64
KV_W = N_KV * HEAD_DIM
WINDOW = 128
BLK = 128
ROPE_THETA = 10000.0
A_GROUPS = 8
D_FF = 2816
EPS = 1e-6
NEG = -1e30
QKV_W = D + 2 * KV_W
A_COLS = 2 * D
B_COLS = 3 * D
G_COLS = 3 * D
IN_W = QKV_W + A_COLS + B_COLS + G_COLS
SEG = (0, QKV_W, QKV_W + A_COLS, QKV_W + A_COLS + B_COLS, IN_W)
N_CHIPS = 4
N_DEV = 8
LANES = 128
SUBLANES = 8
VMEM_LIMIT = 48 * 1024 * 1024
ADAM_LR = 0.001
ADAM_B1 = 0.9
ADAM_B2 = 0.999
ADAM_EPS = 1e-08
ADAM_WD = 0.01
ADAM_STEP = 10
MESH = pl.DeviceIdType.MESH
ANY = pl.BlockSpec(memory_space=pl.ANY)


def _params(sem=None):
    return pltpu.CompilerParams(dimension_semantics=sem, vmem_limit_bytes=VMEM_LIMIT)


def _pick(n, cands):
    for c in cands:
        if n % c == 0:
            return c
    return n


def _rows8(rows, width):
    r = lax.broadcasted_iota(jnp.int32, (SUBLANES, width), 0)
    out = jnp.zeros((SUBLANES, width), F32)
    for idx, v in rows:
        out = out + jnp.where(r == idx, v, 0.0)
    return out


def _sel(mod_ref, k, is_ctx):
    return jnp.where(is_ctx, mod_ref[1:2, k * D:(k + 1) * D], mod_ref[0:1, k * D:(k + 1) * D])


def _colsum(v):
    return jnp.sum(v, axis=0, keepdims=True)


def _mm(a, b, *, name, ta=False, tb=False, out_dtype=F32):
    if ta:
        k_dim, m = a.shape
    else:
        m, k_dim = a.shape
    if tb:
        n, kb = b.shape
    else:
        kb, n = b.shape
    assert k_dim == kb, (a.shape, b.shape, ta, tb)
    tm = _pick(m, (1056, 1024, 1408, 768, 512, 256, 128))
    tn = _pick(n, (1536, 1408, 1024, 768, 512, 256, 128))
    tk = _pick(k_dim, (2048, 1536, 1408, 1024, 768, 512, 256, 128))
    nk = k_dim // tk
    dims = (((0 if ta else 1,), (1 if tb else 0,)), ((), ()))

    def product(a_ref, b_ref):
        return lax.dot_general(a_ref[...].astype(BF16), b_ref[...].astype(BF16), dims, preferred_element_type=F32)

    def body_single(a_ref, b_ref, o_ref):
        o_ref[...] = product(a_ref, b_ref).astype(o_ref.dtype)

    def body_acc(a_ref, b_ref, o_ref, acc_ref):
        k = pl.program_id(2)

        @pl.when(k == 0)
        def _():
            acc_ref[...] = product(a_ref, b_ref)

        @pl.when(k > 0)
        def _():
            acc_ref[...] += product(a_ref, b_ref)

        @pl.when(k == nk - 1)
        def _():
            o_ref[...] = acc_ref[...].astype(o_ref.dtype)

    a_spec = pl.BlockSpec((tk, tm), lambda i, j, k: (k, i)) if ta else pl.BlockSpec((tm, tk), lambda i, j, k: (i, k))
    b_spec = pl.BlockSpec((tn, tk), lambda i, j, k: (j, k)) if tb else pl.BlockSpec((tk, tn), lambda i, j, k: (k, j))
    return pl.pallas_call(
        body_single if nk == 1 else body_acc, name=name, grid=(m // tm, n // tn, nk),
        in_specs=[a_spec, b_spec], out_specs=pl.BlockSpec((tm, tn), lambda i, j, k: (i, j)),
        out_shape=jax.ShapeDtypeStruct((m, n), out_dtype),
        scratch_shapes=[] if nk == 1 else [pltpu.VMEM((tm, tn), F32)],
        compiler_params=_params(("parallel", "parallel", "arbitrary")),
    )(a, b)


def _small(fn, out_shape, *arrays, name):
    def body(*refs):
        refs[-1][...] = fn(*[r[...] for r in refs[:-1]]).astype(refs[-1].dtype)

    return pl.pallas_call(body, name=name, out_shape=jax.ShapeDtypeStruct(out_shape, F32))(*arrays)


def _silu(v):
    return v * jax.nn.sigmoid(v)


def _dsilu(v):
    s = jax.nn.sigmoid(v)
    return s * (1.0 + v * (1.0 - s))


def _row_spec(tm, width, col=0):
    return pl.BlockSpec((tm, width), lambda i: (i, col))


def _full_spec(shape):
    nd = len(shape)
    return pl.BlockSpec(shape, lambda i: (0,) * nd)


def _halo_specs(tm, width, t_rows, col=0):
    per = tm // SUBLANES
    last = t_rows // SUBLANES - 1
    prev = pl.BlockSpec((SUBLANES, width), lambda i: (jnp.maximum(i * per - 1, 0), col))
    nxt = pl.BlockSpec((SUBLANES, width), lambda i: (jnp.minimum((i + 1) * per, last), col))
    return prev, nxt


def _shift_rows(cur, prev8, next8, n_lat, t_rows, tm):
    i = pl.program_id(0)
    row = lax.broadcasted_iota(jnp.int32, (tm, 1), 0)
    g = row + i * tm
    up = pltpu.roll(cur, 1, 0)
    up = jnp.where(row == 0, prev8[SUBLANES - 1:SUBLANES, :], up)
    up = jnp.where((g == 0) | (g == n_lat), 0.0, up)
    dn = pltpu.roll(cur, tm - 1, 0)
    dn = jnp.where(row == tm - 1, next8[0:1, :], dn)
    dn = jnp.where((g == n_lat - 1) | (g == t_rows - 1), 0.0, dn)
    return up, dn


def _norm_mod_fwd(x, g, mod8, sh_idx, sc_idx, n_lat, *, name):
    t_rows = x.shape[0]
    tm = 256

    def body(x_ref, g_ref, mod_ref, o_ref):
        is_ctx = pl.program_id(0) * tm >= n_lat
        xv = x_ref[...]
        rstd = lax.rsqrt(jnp.mean(xv * xv, axis=-1, keepdims=True) + EPS)
        y = xv * rstd * g_ref[...]
        o_ref[...] = (y * (1.0 + _sel(mod_ref, sc_idx, is_ctx)) + _sel(mod_ref, sh_idx, is_ctx)).astype(BF16)

    return pl.pallas_call(
        body, name=name, grid=(t_rows // tm,),
        in_specs=[_row_spec(tm, D), _full_spec((1, D)), _full_spec((SUBLANES, 6 * D))],
        out_specs=_row_spec(tm, D), out_shape=jax.ShapeDtypeStruct((t_rows, D), BF16),
        compiler_params=_params(("parallel",)),
    )(x, g, mod8)


def _norm_mod_bwd(x, dh_parts, dres, g, mod8, sc_idx, n_lat, *, name):
    t_rows = x.shape[0]
    tm = 256
    n_parts = len(dh_parts)

    def body(*refs):
        x_ref, dres_ref, g_ref, mod_ref = refs[:4]
        part_refs = refs[4:4 + n_parts]
        dx_ref, st_ref = refs[4 + n_parts:]
        i = pl.program_id(0)
        is_ctx = i * tm >= n_lat
        dh = part_refs[0][...]
        for p in part_refs[1:]:
            dh = dh + p[...]
        xv = x_ref[...]
        gv = g_ref[...]
        rstd = lax.rsqrt(jnp.mean(xv * xv, axis=-1, keepdims=True) + EPS)
        rn = xv * rstd
        dy = dh * (1.0 + _sel(mod_ref, sc_idx, is_ctx))
        e = dy * gv
        dx_ref[...] = dres_ref[...] + rstd * (e - rn * jnp.mean(e * rn, axis=-1, keepdims=True))
        dsh = _colsum(dh)
        dsc = _colsum(dh * (rn * gv))
        dg = _colsum(dy * rn)
        zero = jnp.zeros_like(dsh)
        upd = _rows8([(0, jnp.where(is_ctx, zero, dsh)), (1, jnp.where(is_ctx, dsh, zero)),
                      (2, jnp.where(is_ctx, zero, dsc)), (3, jnp.where(is_ctx, dsc, zero)), (4, dg)], D)

        @pl.when(i == 0)
        def _():
            st_ref[...] = upd

        @pl.when(i > 0)
        def _():
            st_ref[...] += upd

    return pl.pallas_call(
        body, name=name, grid=(t_rows // tm,),
        in_specs=[_row_spec(tm, D), _row_spec(tm, D), _full_spec((1, D)), _full_spec((SUBLANES, 6 * D))]
        + [_row_spec(tm, D)] * n_parts,
        out_specs=[_row_spec(tm, D), _full_spec((SUBLANES, D))],
        out_shape=[jax.ShapeDtypeStruct((t_rows, D), F32), jax.ShapeDtypeStruct((SUBLANES, D), F32)],
        compiler_params=_params(("arbitrary",)),
    )(x, dres, g, mod8, *dh_parts)


def _residual_fwd(x, branch, mod8, gt_idx, n_lat, *, name):
    t_rows = x.shape[0]
    tm = 256

    def body(x_ref, b_ref, mod_ref, o_ref):
        is_ctx = pl.program_id(0) * tm >= n_lat
        o_ref[...] = x_ref[...] + _sel(mod_ref, gt_idx, is_ctx) * b_ref[...]

    return pl.pallas_call(
        body, name=name, grid=(t_rows // tm,),
        in_specs=[_row_spec(tm, D), _row_spec(tm, D), _full_spec((SUBLANES, 6 * D))],
        out_specs=_row_spec(tm, D), out_shape=jax.ShapeDtypeStruct((t_rows, D), F32),
        compiler_params=_params(("parallel",)),
    )(x, branch, mod8)


def _residual_bwd(dx, branch, mod8, gt_idx, n_lat, *, name):
    t_rows = dx.shape[0]
    tm = 256

    def body(dx_ref, b_ref, mod_ref, o_ref, st_ref):
        i = pl.program_id(0)
        is_ctx = i * tm >= n_lat
        dxv = dx_ref[...]
        o_ref[...] = (dxv * _sel(mod_ref, gt_idx, is_ctx)).astype(BF16)
        dgt = _colsum(dxv * b_ref[...])
        zero = jnp.zeros_like(dgt)
        upd = _rows8([(0, jnp.where(is_ctx, zero, dgt)), (1, jnp.where(is_ctx, dgt, zero))], D)

        @pl.when(i == 0)
        def _():
            st_ref[...] = upd

        @pl.when(i > 0)
        def _():
            st_ref[...] += upd

    return pl.pallas_call(
        body, name=name, grid=(t_rows // tm,),
        in_specs=[_row_spec(tm, D), _row_spec(tm, D), _full_spec((SUBLANES, 6 * D))],
        out_specs=[_row_spec(tm, D), _full_spec((SUBLANES, D))],
        out_shape=[jax.ShapeDtypeStruct((t_rows, D), BF16), jax.ShapeDtypeStruct((SUBLANES, D), F32)],
        compiler_params=_params(("arbitrary",)),
    )(dx, branch, mod8)


def _rope_tables(n_lat, n_ctx):
    rows = n_lat // GRID_W
    row = jnp.broadcast_to(jnp.arange(rows, dtype=F32)[:, None], (rows, GRID_W)).reshape(n_lat)
    col = jnp.broadcast_to(jnp.arange(GRID_W, dtype=F32)[None, :], (rows, GRID_W)).reshape(n_lat)
    half = HEAD_DIM // 2
    inv = ROPE_THETA ** (-jnp.arange(0, half, 2, dtype=F32) / half)
    ang = jnp.concatenate([row[:, None] * inv, col[:, None] * inv], axis=-1)
    cos, sin = jnp.cos(ang), jnp.sin(ang)
    c64 = jnp.concatenate([cos, cos], axis=-1)
    s64 = jnp.concatenate([-sin, sin], axis=-1)
    c64 = jnp.concatenate([c64, jnp.ones((n_ctx, HEAD_DIM), F32)], axis=0)
    s64 = jnp.concatenate([s64, jnp.zeros((n_ctx, HEAD_DIM), F32)], axis=0)
    return jnp.tile(c64, (1, 2)), jnp.tile(s64, (1, 2))


def _swap_halves(v):
    lane = lax.broadcasted_iota(jnp.int32, v.shape, 1)
    return jnp.where(lane % HEAD_DIM < HEAD_DIM // 2, pltpu.roll(v, LANES - HEAD_DIM // 2, 1),
                     pltpu.roll(v, HEAD_DIM // 2, 1))


def _low_half(shape):
    return lax.broadcasted_iota(jnp.int32, shape, 1) < HEAD_DIM


def _qkv_prep(z_qkv, cos_t, sin_t, *, name):
    t_rows = z_qkv.shape[0]
    tm = 256

    def body(z_ref, c_ref, s_ref, q_ref, k_ref, v_ref):
        cv, sv = c_ref[...], s_ref[...]

        def rope(chunk):
            return chunk * cv + _swap_halves(chunk) * sv

        for ch in range(D // LANES):
            q_ref[:, ch * LANES:(ch + 1) * LANES] = rope(z_ref[:, ch * LANES:(ch + 1) * LANES]).astype(BF16)
        low = _low_half((tm, LANES))
        for pair in range(N_KV // 2):
            for which, ref, roped in ((0, k_ref, True), (1, v_ref, False)):
                off = D + which * KV_W + pair * LANES
                chunk = z_ref[:, off:off + LANES]
                if roped:
                    chunk = rope(chunk)
                other = pltpu.roll(chunk, HEAD_DIM, 1)
                even = jnp.where(low, chunk, other)
                odd = jnp.where(low, other, chunk)
                ref[:, (2 * pair) * LANES:(2 * pair + 1) * LANES] = even.astype(BF16)
                ref[:, (2 * pair + 1) * LANES:(2 * pair + 2) * LANES] = odd.astype(BF16)

    dup_w = N_KV * LANES
    return pl.pallas_call(
        body, name=name, grid=(t_rows // tm,),
        in_specs=[_row_spec(tm, QKV_W), _row_spec(tm, LANES), _row_spec(tm, LANES)],
        out_specs=[_row_spec(tm, D), _row_spec(tm, dup_w), _row_spec(tm, dup_w)],
        out_shape=[jax.ShapeDtypeStruct((t_rows, D), BF16), jax.ShapeDtypeStruct((t_rows, dup_w), BF16),
                   jax.ShapeDtypeStruct((t_rows, dup_w), BF16)],
        compiler_params=_params(("parallel",)),
    )(z_qkv, cos_t, sin_t)


def _qkv_unprep(dq, dk, dv, cos_t, sin_t, *, name):
    t_rows = dq.shape[0]
    tm = 256

    def body(dq_ref, dk_ref, dv_ref, c_ref, s_ref, o_ref):
        cv, sv = c_ref[...], s_ref[...]

        def unrope(chunk):
            return chunk * cv + _swap_halves(chunk * sv)

        for ch in range(D // LANES):
            o_ref[:, ch * LANES:(ch + 1) * LANES] = unrope(dq_ref[:, ch * LANES:(ch + 1) * LANES]).astype(BF16)
        for pair in range(N_KV // 2):
            for which, ref, roped in ((0, dk_ref, True), (1, dv_ref, False)):
                chunk = ref[:, pair * LANES:(pair + 1) * LANES]
                if roped:
                    chunk = unrope(chunk)
                off = D + which * KV_W + pair * LANES
                o_ref[:, off:off + LANES] = chunk.astype(BF16)

    return pl.pallas_call(
        body, name=name, grid=(t_rows // tm,),
        in_specs=[_row_spec(tm, D), _row_spec(tm, KV_W), _row_spec(tm, KV_W), _row_spec(tm, LANES),
                  _row_spec(tm, LANES)],
        out_specs=_row_spec(tm, QKV_W), out_shape=jax.ShapeDtypeStruct((t_rows, QKV_W), BF16),
        compiler_params=_params(("parallel",)),
    )(dq, dk, dv, cos_t, sin_t)


def _attn_specs(n_lat, n_ctx):
    nb = n_lat // BLK
    dup_w = N_KV * LANES

    def ws(j):
        return jnp.clip(j - 1, 0, nb - 3)

    win = [pl.BlockSpec((BLK, dup_w), functools.partial(lambda j, o: (ws(j) + o, 0), o=o)) for o in range(3)]
    ctx = pl.BlockSpec((n_ctx, dup_w), lambda j: (n_lat // n_ctx, 0))
    return nb, ws, win, ctx


def _attn_bias(j, ws_j, nb, n_ctx):
    n_keys = 3 * BLK + n_ctx
    row = lax.broadcasted_iota(jnp.int32, (BLK, n_keys), 0)
    col = lax.broadcasted_iota(jnp.int32, (BLK, n_keys), 1)
    rel = (ws_j - j) * BLK + col - row
    valid = (col >= 3 * BLK) | ((jnp.abs(rel) <= WINDOW) & (j < nb))
    bias = jnp.where(valid, 0.0, NEG)
    return jnp.concatenate([bias] * GRP, axis=0)


def _attn_probs(q_ref, kk, kh, bias, sink_ref):
    low = _low_half((BLK, LANES))
    qs = []
    for g in range(GRP):
        h = GRP * kh + g
        chunk = q_ref[:, (h // 2) * LANES:(h // 2 + 1) * LANES]
        qs.append(jnp.where(low if h % 2 == 0 else ~low, chunk, jnp.zeros_like(chunk)))
    qs = jnp.concatenate(qs, axis=0)
    s = lax.dot_general(qs, kk, (((1,), (1,)), ((), ())), preferred_element_type=F32) * (HEAD_DIM ** -0.5) + bias
    snk = jnp.concatenate(
        [jnp.broadcast_to(jnp.max(sink_ref[GRP * kh + g:GRP * kh + g + 1, :], axis=1, keepdims=True), (BLK, 1))
         for g in range(GRP)], axis=0)
    m = jnp.maximum(jnp.max(s, axis=-1, keepdims=True), snk)
    p = jnp.exp(s - m)
    p_snk = jnp.exp(snk - m)
    inv = 1.0 / (jnp.sum(p, axis=-1, keepdims=True) + p_snk)
    return qs, p * inv, p_snk * inv


class _Hosted:
    def __init__(self, arrays, out_shapes, scratch, start, finish):
        self.arrays, self.out_shapes, self.scratch, self.start, self.finish = arrays, out_shapes, scratch, start, finish


_NO_EXCHANGE = _Hosted([], [], [], None, None)


def _split_refs(refs, n_in, n_out, n_scratch, hosted):
    hi, ho, hs = len(hosted.arrays), len(hosted.out_shapes), len(hosted.scratch)
    a = n_in + hi
    b = a + n_out + ho
    ins, h_ins = refs[:n_in], refs[n_in:a]
    outs, h_outs = refs[a:a + n_out], refs[a + n_out:b]
    scr, h_scr = refs[b:b + n_scratch], refs[b + n_scratch:b + n_scratch + hs]
    return ins, outs, scr, (h_ins, h_outs, h_scr)


def _run_hosted(hosted, h_refs, step, n_steps):
    if hosted.start is None:
        return

    flat = [r for group in h_refs for r in group]

    @pl.when(step == 0)
    def _():
        hosted.start(*flat)

    @pl.when(step == n_steps - 1)
    def _():
        hosted.finish(*flat)


def _attention_fwd(q, kd, vd, sink_b, n_lat, n_ctx, *, name, hosted=_NO_EXCHANGE):
    t_rows = q.shape[0]
    nb, ws, win, ctx = _attn_specs(n_lat, n_ctx)
    n_steps = t_rows // BLK

    def body(*refs):
        ins, outs, _, h_refs = _split_refs(refs, 10, 1, 0, hosted)
        q_ref, k0, k1, k2, kc, v0, v1, v2, vc, sink_ref = ins
        o_ref, = outs
        j = pl.program_id(0)
        _run_hosted(hosted, h_refs, j, n_steps)
        ws_j = ws(j)
        low = _low_half((BLK, LANES))
        bias = _attn_bias(j, ws_j, nb, n_ctx)
        for kh in range(N_KV):
            sl = slice(kh * LANES, (kh + 1) * LANES)
            kk = jnp.concatenate([k0[:, sl], k1[:, sl], k2[:, sl], kc[:, sl]], axis=0)
            vv = jnp.concatenate([v0[:, sl], v1[:, sl], v2[:, sl], vc[:, sl]], axis=0)
            _, p, _ = _attn_probs(q_ref, kk, kh, bias, sink_ref)
            o = jnp.dot(p.astype(BF16), vv, preferred_element_type=F32)
            for half in range(2):
                even = o[(2 * half) * BLK:(2 * half + 1) * BLK]
                odd = o[(2 * half + 1) * BLK:(2 * half + 2) * BLK]
                ch = 2 * kh + half
                o_ref[:, ch * LANES:(ch + 1) * LANES] = jnp.where(low, even, odd).astype(BF16)

    outs = pl.pallas_call(
        body, name=name, grid=(n_steps,),
        in_specs=[_row_spec(BLK, D)] + win + [ctx] + win + [ctx] + [_full_spec((N_HEADS, LANES))]
        + [ANY] * len(hosted.arrays),
        out_specs=[_row_spec(BLK, D)] + [ANY] * len(hosted.out_shapes),
        out_shape=[jax.ShapeDtypeStruct((t_rows, D), BF16)] + list(hosted.out_shapes),
        scratch_shapes=list(hosted.scratch),
        compiler_params=_params(("arbitrary",)),
    )(q, kd, kd, kd, kd, vd, vd, vd, vd, sink_b, *hosted.arrays)
    return outs[0], outs[1:]


def _attention_bwd(q, kd, vd, sink_b, dy, n_lat, n_ctx, *, name, hosted=_NO_EXCHANGE):
    t_rows = q.shape[0]
    nb, ws, win, ctx = _attn_specs(n_lat, n_ctx)
    n_steps = t_rows // BLK

    def body(*refs):
        ins, outs, scr, h_refs = _split_refs(refs, 11, 4, 3, hosted)
        q_ref, k0, k1, k2, kc, v0, v1, v2, vc, sink_ref, dy_ref = ins
        dq_ref, dk_hbm, dv_hbm, ds_ref = outs
        dk_acc, dv_acc, sem = scr
        j = pl.program_id(0)
        _run_hosted(hosted, h_refs, j, n_steps)
        ws_j = ws(j)

        @pl.when(j == 0)
        def _():
            dk_acc[...] = jnp.zeros_like(dk_acc)
            dv_acc[...] = jnp.zeros_like(dv_acc)
            ds_ref[...] = jnp.zeros_like(ds_ref)

        low = _low_half((BLK, LANES))
        low_keys = _low_half((3 * BLK + n_ctx, LANES))
        win_start = pl.multiple_of(ws_j * BLK, BLK)
        scale = HEAD_DIM ** -0.5
        dk_heads, dv_heads = [], []
        bias = _attn_bias(j, ws_j, nb, n_ctx)
        for kh in range(N_KV):
            sl = slice(kh * LANES, (kh + 1) * LANES)
            kk = jnp.concatenate([k0[:, sl], k1[:, sl], k2[:, sl], kc[:, sl]], axis=0)
            vv = jnp.concatenate([v0[:, sl], v1[:, sl], v2[:, sl], vc[:, sl]], axis=0)
            qs, p, p_snk = _attn_probs(q_ref, kk, kh, bias, sink_ref)
            dos = []
            for g in range(GRP):
                h = GRP * kh + g
                chunk = dy_ref[:, (h // 2) * LANES:(h // 2 + 1) * LANES]
                dos.append(jnp.where(low if h % 2 == 0 else ~low, chunk, jnp.zeros_like(chunk)).astype(BF16))
            dos = jnp.concatenate(dos, axis=0)
            dp = lax.dot_general(dos, vv, (((1,), (1,)), ((), ())), preferred_element_type=F32)
            dsum = jnp.sum(p * dp, axis=-1, keepdims=True)
            ds = (p * (dp - dsum)).astype(BF16)
            for g in range(GRP):
                contrib = -jnp.sum(p_snk[g * BLK:(g + 1) * BLK] * dsum[g * BLK:(g + 1) * BLK], axis=0, keepdims=True)
                ds_ref[GRP * kh + g:GRP * kh + g + 1, :] += jnp.broadcast_to(contrib, (1, LANES))
            dqs = jnp.dot(ds, kk, preferred_element_type=F32) * scale
            for half in range(2):
                even = dqs[(2 * half) * BLK:(2 * half + 1) * BLK]
                odd = dqs[(2 * half + 1) * BLK:(2 * half + 2) * BLK]
                ch = 2 * kh + half
                dq_ref[:, ch * LANES:(ch + 1) * LANES] = jnp.where(low, even, odd)
            dkk = lax.dot_general(ds, qs, (((0,), (0,)), ((), ())), preferred_element_type=F32) * scale
            dvv = lax.dot_general(p.astype(BF16), dos, (((0,), (0,)), ((), ())), preferred_element_type=F32)
            dk_heads.append(dkk + pltpu.roll(dkk, HEAD_DIM, 1))
            dv_heads.append(dvv + pltpu.roll(dvv, HEAD_DIM, 1))
        for pair in range(N_KV // 2):
            sl = slice(pair * LANES, (pair + 1) * LANES)
            for acc, heads in ((dk_acc, dk_heads), (dv_acc, dv_heads)):
                both = jnp.where(low_keys, heads[2 * pair], heads[2 * pair + 1])
                acc[pl.ds(win_start, 3 * BLK), sl] += both[:3 * BLK]
                acc[n_lat:n_lat + n_ctx, sl] += both[3 * BLK:]

        @pl.when(j == n_steps - 1)
        def _():
            ck = pltpu.make_async_copy(dk_acc, dk_hbm, sem.at[0])
            cv = pltpu.make_async_copy(dv_acc, dv_hbm, sem.at[1])
            ck.start()
            cv.start()
            ck.wait()
            cv.wait()

    outs = pl.pallas_call(
        body, name=name, grid=(n_steps,),
        in_specs=[_row_spec(BLK, D)] + win + [ctx] + win + [ctx] + [_full_spec((N_HEADS, LANES)), _row_spec(BLK, D)]
        + [ANY] * len(hosted.arrays),
        out_specs=[_row_spec(BLK, D), ANY, ANY, _full_spec((N_HEADS, LANES))] + [ANY] * len(hosted.out_shapes),
        out_shape=[jax.ShapeDtypeStruct((t_rows, D), F32), jax.ShapeDtypeStruct((t_rows, KV_W), F32),
                   jax.ShapeDtypeStruct((t_rows, KV_W), F32), jax.ShapeDtypeStruct((N_HEADS, LANES), F32)]
        + list(hosted.out_shapes),
        scratch_shapes=[pltpu.VMEM((t_rows, KV_W), F32), pltpu.VMEM((t_rows, KV_W), F32),
                        pltpu.SemaphoreType.DMA((2,))] + list(hosted.scratch),
        compiler_params=_params(("arbitrary",)),
    )(q, kd, kd, kd, kd, vd, vd, vd, vd, sink_b, dy, *hosted.arrays)
    return outs[0], outs[1], outs[2], outs[3], outs[4:]


_GELU_K = math.sqrt(2.0 / math.pi)


def _gelu(v):
    return jax.nn.gelu(v)


def _dgelu(v):
    t = jnp.tanh(_GELU_K * (v + 0.044715 * v * v * v))
    return 0.5 * (1.0 + t) + 0.5 * v * (1.0 - t * t) * _GELU_K * (1.0 + 3.0 * 0.044715 * v * v)


def _gating_fwd(z_a, w_s, b_sb, g_v, *, name):
    t_rows = z_a.shape[0]

    def body(z_ref, w_ref, b_ref, g_ref, o_ref):
        u = _gelu(z_ref[:, :D])
        v = _gelu(z_ref[:, D:])
        vn = v * lax.rsqrt(jnp.mean(v * v, axis=-1, keepdims=True) + EPS) * g_ref[...]
        for g in range(A_GROUPS):
            sl = slice(g * LANES, (g + 1) * LANES)
            mixed = jnp.dot(w_ref[g].astype(BF16), vn[:, sl].astype(BF16), preferred_element_type=F32) + b_ref[g]
            o_ref[:, sl] = (u[:, sl] * mixed).astype(BF16)

    return pl.pallas_call(
        body, name=name, grid=(t_rows // BLK,),
        in_specs=[_row_spec(BLK, A_COLS), _full_spec((A_GROUPS, BLK, BLK)), _full_spec((A_GROUPS, BLK, LANES)),
                  _full_spec((1, D))],
        out_specs=_row_spec(BLK, D), out_shape=jax.ShapeDtypeStruct((t_rows, D), BF16),
        compiler_params=_params(("parallel",)),
    )(z_a, w_s, b_sb, g_v)


def _gating_bwd(z_a, dy, w_s, b_sb, g_v, *, name):
    t_rows = z_a.shape[0]

    def body(z_ref, dy_ref, w_ref, b_ref, g_ref, dz_ref, dw_ref, db_ref, st_ref):
        i = pl.program_id(0)

        @pl.when(i == 0)
        def _():
            dw_ref[...] = jnp.zeros_like(dw_ref)
            db_ref[...] = jnp.zeros_like(db_ref)
            st_ref[...] = jnp.zeros_like(st_ref)

        zu = z_ref[:, :D]
        zv = z_ref[:, D:]
        u = _gelu(zu)
        v = _gelu(zv)
        gv = g_ref[...]
        rstd = lax.rsqrt(jnp.mean(v * v, axis=-1, keepdims=True) + EPS)
        vh = v * rstd
        vn = vh * gv
        dyv = dy_ref[...]
        dvn = []
        for g in range(A_GROUPS):
            sl = slice(g * LANES, (g + 1) * LANES)
            wg = w_ref[g].astype(BF16)
            vg = vn[:, sl].astype(BF16)
            mixed = jnp.dot(wg, vg, preferred_element_type=F32) + b_ref[g]
            dz_ref[:, sl] = (dyv[:, sl] * mixed * _dgelu(zu[:, sl])).astype(BF16)
            dmixed = dyv[:, sl] * u[:, sl]
            dmb = dmixed.astype(BF16)
            dvn.append(lax.dot_general(wg, dmb, (((0,), (0,)), ((), ())), preferred_element_type=F32))
            dw_ref[g] += lax.dot_general(dmb, vg, (((1,), (1,)), ((), ())), preferred_element_type=F32)
            db_ref[g] += jnp.broadcast_to(jnp.sum(dmixed, axis=-1, keepdims=True), (BLK, LANES))
        dvn = jnp.concatenate(dvn, axis=1)
        st_ref[...] += _rows8([(0, _colsum(dvn * vh))], D)
        e = dvn * gv
        dv = rstd * (e - vh * jnp.mean(e * vh, axis=-1, keepdims=True))
        dz_ref[:, D:] = (dv * _dgelu(zv)).astype(BF16)

    return pl.pallas_call(
        body, name=name, grid=(t_rows // BLK,),
        in_specs=[_row_spec(BLK, A_COLS), _row_spec(BLK, D), _full_spec((A_GROUPS, BLK, BLK)),
                  _full_spec((A_GROUPS, BLK, LANES)), _full_spec((1, D))],
        out_specs=[_row_spec(BLK, A_COLS), _full_spec((A_GROUPS, BLK, BLK)), _full_spec((A_GROUPS, BLK, LANES)),
                   _full_spec((SUBLANES, D))],
        out_shape=[jax.ShapeDtypeStruct((t_rows, A_COLS), BF16), jax.ShapeDtypeStruct((A_GROUPS, BLK, BLK), F32),
                   jax.ShapeDtypeStruct((A_GROUPS, BLK, LANES), F32), jax.ShapeDtypeStruct((SUBLANES, D), F32)],
        compiler_params=_params(("arbitrary",)),
    )(z_a, dy, w_s, b_sb, g_v)


def _sconv_fwd(z_b, w8, n_lat, *, name):
    t_rows = z_b.shape[0]
    tm = 256
    prev, nxt = _halo_specs(tm, B_COLS, t_rows)

    def body(z_ref, zp_ref, zn_ref, w_ref, o_ref):
        p = z_ref[:, D:2 * D] * z_ref[:, 2 * D:]
        pp = zp_ref[:, D:2 * D] * zp_ref[:, 2 * D:]
        pn = zn_ref[:, D:2 * D] * zn_ref[:, 2 * D:]
        up, dn = _shift_rows(p, pp, pn, n_lat, t_rows, tm)
        conv = w_ref[0:1, :] * up + w_ref[1:2, :] * p + w_ref[2:3, :] * dn
        o_ref[...] = (z_ref[:, :D] * conv).astype(BF16)

    return pl.pallas_call(
        body, name=name, grid=(t_rows // tm,),
        in_specs=[_row_spec(tm, B_COLS), prev, nxt, _full_spec((SUBLANES, D))],
        out_specs=_row_spec(tm, D), out_shape=jax.ShapeDtypeStruct((t_rows, D), BF16),
        compiler_params=_params(("parallel",)),
    )(z_b, z_b, z_b, w8)


def _sconv_bwd(z_b, dy, w8, n_lat, *, name):
    t_rows = z_b.shape[0]
    tm = 256
    prev, nxt = _halo_specs(tm, B_COLS, t_rows)
    dprev, dnxt = _halo_specs(tm, D, t_rows)

    def body(z_ref, zp_ref, zn_ref, dy_ref, dyp_ref, dyn_ref, w_ref, dz_ref, st_ref):
        i = pl.program_id(0)
        bg, cg, hb = z_ref[:, :D], z_ref[:, D:2 * D], z_ref[:, 2 * D:]
        p = cg * hb
        pp = zp_ref[:, D:2 * D] * zp_ref[:, 2 * D:]
        pn = zn_ref[:, D:2 * D] * zn_ref[:, 2 * D:]
        up, dn = _shift_rows(p, pp, pn, n_lat, t_rows, tm)
        w0, w1, w2 = w_ref[0:1, :], w_ref[1:2, :], w_ref[2:3, :]
        conv = w0 * up + w1 * p + w2 * dn
        dyv = dy_ref[...]
        dz_ref[:, :D] = (dyv * conv).astype(BF16)
        dcv = dyv * bg
        dcv_up, dcv_dn = _shift_rows(dcv, dyp_ref[...] * zp_ref[:, :D], dyn_ref[...] * zn_ref[:, :D], n_lat, t_rows, tm)
        dp = w0 * dcv_dn + w1 * dcv + w2 * dcv_up
        dz_ref[:, D:2 * D] = (dp * hb).astype(BF16)
        dz_ref[:, 2 * D:] = (dp * cg).astype(BF16)
        upd = _rows8([(0, _colsum(dcv * up)), (1, _colsum(dcv * p)), (2, _colsum(dcv * dn))], D)

        @pl.when(i == 0)
        def _():
            st_ref[...] = upd

        @pl.when(i > 0)
        def _():
            st_ref[...] += upd

    return pl.pallas_call(
        body, name=name, grid=(t_rows // tm,),
        in_specs=[_row_spec(tm, B_COLS), prev, nxt, _row_spec(tm, D), dprev, dnxt, _full_spec((SUBLANES, D))],
        out_specs=[_row_spec(tm, B_COLS), _full_spec((SUBLANES, D))],
        out_shape=[jax.ShapeDtypeStruct((t_rows, B_COLS), BF16), jax.ShapeDtypeStruct((SUBLANES, D), F32)],
        compiler_params=_params(("arbitrary",)),
    )(z_b, z_b, z_b, dy, dy, dy, w8)


def _merge_fwd(t0, t1, t2, z_g, b_gate, *, name):
    t_rows = t0.shape[0]
    tm = 256

    def body(t0_ref, t1_ref, t2_ref, z_ref, b_ref, o_ref):
        acc = None
        for k, t_ref in enumerate((t0_ref, t1_ref, t2_ref)):
            gate = jax.nn.sigmoid(z_ref[:, k * D:(k + 1) * D] + b_ref[:, k * D:(k + 1) * D])
            term = gate * t_ref[...]
            acc = term if acc is None else acc + term
        o_ref[...] = acc.astype(BF16)

    return pl.pallas_call(
        body, name=name, grid=(t_rows // tm,),
        in_specs=[_row_spec(tm, D)] * 3 + [_row_spec(tm, G_COLS), _full_spec((1, G_COLS))],
        out_specs=_row_spec(tm, D), out_shape=jax.ShapeDtypeStruct((t_rows, D), BF16),
        compiler_params=_params(("parallel",)),
    )(t0, t1, t2, z_g, b_gate)


def _merge_bwd(dmerged, t0, t1, t2, z_g, b_gate, *, name):
    t_rows = t0.shape[0]
    tm = 256

    def body(dm_ref, t0_ref, t1_ref, t2_ref, z_ref, b_ref, d0_ref, d1_ref, d2_ref, dz_ref, st_ref):
        i = pl.program_id(0)
        dm = dm_ref[...]
        sums = []
        for k, (t_ref, d_ref) in enumerate(((t0_ref, d0_ref), (t1_ref, d1_ref), (t2_ref, d2_ref))):
            gate = jax.nn.sigmoid(z_ref[:, k * D:(k + 1) * D] + b_ref[:, k * D:(k + 1) * D])
            d_ref[...] = (dm * gate).astype(BF16)
            dzg = dm * t_ref[...] * gate * (1.0 - gate)
            dz_ref[:, k * D:(k + 1) * D] = dzg.astype(BF16)
            sums.append(_colsum(dzg))
        upd = _rows8([(0, jnp.concatenate(sums, axis=1))], G_COLS)

        @pl.when(i == 0)
        def _():
            st_ref[...] = upd

        @pl.when(i > 0)
        def _():
            st_ref[...] += upd

    return pl.pallas_call(
        body, name=name, grid=(t_rows // tm,),
        in_specs=[_row_spec(tm, D)] * 4 + [_row_spec(tm, G_COLS), _full_spec((1, G_COLS))],
        out_specs=[_row_spec(tm, D)] * 3 + [_row_spec(tm, G_COLS), _full_spec((SUBLANES, G_COLS))],
        out_shape=[jax.ShapeDtypeStruct((t_rows, D), BF16)] * 3
        + [jax.ShapeDtypeStruct((t_rows, G_COLS), BF16), jax.ShapeDtypeStruct((SUBLANES, G_COLS), F32)],
        compiler_params=_params(("arbitrary",)),
    )(dmerged, t0, t1, t2, z_g, b_gate)


def _ffn_mid_fwd(up, w8, n_lat, *, name):
    t_rows = up.shape[0]
    tm = 128
    prev, nxt = _halo_specs(tm, D_FF, t_rows)

    def body(a_ref, ap_ref, an_ref, g_ref, w_ref, cv_ref, f_ref):
        a = a_ref[...]
        au, ad = _shift_rows(a, ap_ref[...], an_ref[...], n_lat, t_rows, tm)
        cv = w_ref[0:1, :] * au + w_ref[1:2, :] * a + w_ref[2:3, :] * ad
        cv_ref[...] = cv
        f_ref[...] = (_silu(cv) * g_ref[...]).astype(BF16)

    return pl.pallas_call(
        body, name=name, grid=(t_rows // tm,),
        in_specs=[_row_spec(tm, D_FF), prev, nxt, _row_spec(tm, D_FF, 1), _full_spec((SUBLANES, D_FF))],
        out_specs=[_row_spec(tm, D_FF), _row_spec(tm, D_FF)],
        out_shape=[jax.ShapeDtypeStruct((t_rows, D_FF), F32), jax.ShapeDtypeStruct((t_rows, D_FF), BF16)],
        compiler_params=_params(("parallel",)),
    )(up, up, up, up, w8)


def _ffn_mid_bwd(up, cv, df, w8, n_lat, *, name):
    t_rows = up.shape[0]
    tm = 128
    prev, nxt = _halo_specs(tm, D_FF, t_rows)
    gprev, gnxt = _halo_specs(tm, D_FF, t_rows, 1)

    def body(a_ref, ap_ref, an_ref, g_ref, gp_ref, gn_ref, cv_ref, cp_ref, cn_ref, df_ref, dfp_ref, dfn_ref,
             w_ref, o_ref, st_ref):
        i = pl.program_id(0)
        a = a_ref[...]
        au, ad = _shift_rows(a, ap_ref[...], an_ref[...], n_lat, t_rows, tm)
        cvv = cv_ref[...]
        dfv = df_ref[...]
        o_ref[:, D_FF:] = (dfv * _silu(cvv)).astype(BF16)
        dcv = dfv * g_ref[...] * _dsilu(cvv)
        dcv_p = dfp_ref[...] * gp_ref[...] * _dsilu(cp_ref[...])
        dcv_n = dfn_ref[...] * gn_ref[...] * _dsilu(cn_ref[...])
        du, dd = _shift_rows(dcv, dcv_p, dcv_n, n_lat, t_rows, tm)
        o_ref[:, :D_FF] = (w_ref[0:1, :] * dd + w_ref[1:2, :] * dcv + w_ref[2:3, :] * du).astype(BF16)
        upd = _rows8([(0, _colsum(dcv * au)), (1, _colsum(dcv * a)), (2, _colsum(dcv * ad))], D_FF)

        @pl.when(i == 0)
        def _():
            st_ref[...] = upd

        @pl.when(i > 0)
        def _():
            st_ref[...] += upd

    row = _row_spec(tm, D_FF)
    return pl.pallas_call(
        body, name=name, grid=(t_rows // tm,),
        in_specs=[row, prev, nxt, _row_spec(tm, D_FF, 1), gprev, gnxt, row, prev, nxt, row, prev, nxt,
                  _full_spec((SUBLANES, D_FF))],
        out_specs=[_row_spec(tm, 2 * D_FF), _full_spec((SUBLANES, D_FF))],
        out_shape=[jax.ShapeDtypeStruct((t_rows, 2 * D_FF), BF16), jax.ShapeDtypeStruct((SUBLANES, D_FF), F32)],
        compiler_params=_params(("arbitrary",)),
    )(up, up, up, up, up, up, cv, cv, cv, df, df, df, w8)


def _loss_head(x, target, g_final, n_lat, *, name):
    t_rows = x.shape[0]
    tm = 256
    last = n_lat // tm - 1

    def body(x_ref, t_ref, g_ref, dx_ref, st_ref):
        i = pl.program_id(0)
        is_ctx = i * tm >= n_lat
        xv = x_ref[...]
        gv = g_ref[...]
        rstd = lax.rsqrt(jnp.mean(xv * xv, axis=-1, keepdims=True) + EPS)
        rn = xv * rstd
        err = rn * gv - t_ref[...]
        dy = err / D
        e = dy * gv
        dx = rstd * (e - rn * jnp.mean(e * rn, axis=-1, keepdims=True))
        dx_ref[...] = jnp.where(is_ctx, 0.0, dx)
        keep = jnp.where(is_ctx, 0.0, 1.0)
        upd = _rows8([(0, keep * _colsum(dy * rn)), (1, keep * _colsum(err * err))], D)

        @pl.when(i == 0)
        def _():
            st_ref[...] = upd

        @pl.when(i > 0)
        def _():
            st_ref[...] += upd

    return pl.pallas_call(
        body, name=name, grid=(t_rows // tm,),
        in_specs=[_row_spec(tm, D), pl.BlockSpec((tm, D), lambda i: (jnp.minimum(i, last), 0)), _full_spec((1, D))],
        out_specs=[_row_spec(tm, D), _full_spec((SUBLANES, D))],
        out_shape=[jax.ShapeDtypeStruct((t_rows, D), F32), jax.ShapeDtypeStruct((SUBLANES, D), F32)],
        compiler_params=_params(("arbitrary",)),
    )(x, target, g_final)


def _sum_slabs(x, out_dtype, *, name):
    n_slabs, rows, cols = x.shape
    tm = _pick(rows, (592, 256, 192, 128, 64, 32, 24, 16, 8))

    def body(x_ref, o_ref):
        acc = x_ref[0].astype(F32)
        for s in range(1, n_slabs):
            acc = acc + x_ref[s].astype(F32)
        o_ref[...] = acc.astype(o_ref.dtype)

    return pl.pallas_call(
        body, name=name, grid=(rows // tm,),
        in_specs=[pl.BlockSpec((n_slabs, tm, cols), lambda i: (0, i, 0))],
        out_specs=pl.BlockSpec((tm, cols), lambda i: (i, 0)),
        out_shape=jax.ShapeDtypeStruct((rows, cols), out_dtype),
        compiler_params=_params(("parallel",)),
    )(x)


def _add_half(half_idx, a, b, *, name):
    n_slabs, rows, cols = b.shape
    tm = _pick(rows, (592, 256, 192, 128, 96, 64, 32, 16))
    per_half = rows // tm

    def body(half_ref, a_ref, b_ref, o_ref):
        o_ref[...] = (a_ref[...].astype(F32) + b_ref[...].astype(F32)).astype(BF16)

    spec = pl.BlockSpec((1, tm, cols), lambda s, i, half_ref: (s, i, 0))
    a_spec = pl.BlockSpec((1, tm, cols), lambda s, i, half_ref: (s, half_ref[0] * per_half + i, 0))
    return pl.pallas_call(
        body, name=name,
        grid_spec=pltpu.PrefetchScalarGridSpec(num_scalar_prefetch=1, grid=(n_slabs, per_half),
                                               in_specs=[a_spec, spec], out_specs=spec),
        out_shape=jax.ShapeDtypeStruct(b.shape, BF16), compiler_params=_params(("parallel", "parallel")),
    )(half_idx, a, b)


def _adamw(w, g, m, v, *, name):
    rows, cols = w.shape
    tm = _pick(rows, (128, 64, 32, 16, 8))

    def body(w_ref, g_ref, m_ref, v_ref, d_ref, nm_ref, nv_ref):
        gv = g_ref[...]
        nm = ADAM_B1 * m_ref[...] + (1.0 - ADAM_B1) * gv
        nv = ADAM_B2 * v_ref[...] + (1.0 - ADAM_B2) * jnp.square(gv)
        m_hat = nm / (1.0 - ADAM_B1 ** ADAM_STEP)
        v_hat = nv / (1.0 - ADAM_B2 ** ADAM_STEP)
        d_ref[...] = -ADAM_LR * (m_hat / (jnp.sqrt(v_hat) + ADAM_EPS) + ADAM_WD * w_ref[...])
        nm_ref[...] = nm
        nv_ref[...] = nv

    spec = pl.BlockSpec((tm, cols), lambda i: (i, 0))
    shape = jax.ShapeDtypeStruct((rows, cols), F32)
    return pl.pallas_call(
        body, name=name, grid=(rows // tm,), in_specs=[spec] * 4, out_specs=[spec] * 3, out_shape=[shape] * 3,
        compiler_params=_params(("parallel",)),
    )(w, g, m, v)


def _place():
    x, y, c = lax.axis_index("x"), lax.axis_index("y"), lax.axis_index("c")
    chips = [(1 - x, y), (x, 1 - y), (1 - x, 1 - y)]
    return x, y, c, chips


def _remote(src, dst, send_sems, recv_sems, k, to):
    return pltpu.make_async_remote_copy(src_ref=src, dst_ref=dst, send_sem=send_sems.at[k], recv_sem=recv_sems.at[k],
                                        device_id=to, device_id_type=MESH)


HALF_CHUNKS = 2
STAGE_ROWS = 592


def _chunks(ref, n):
    step = ref.shape[0] // n
    tile_rows = SUBLANES if ref.dtype == F32 else 2 * SUBLANES
    assert step * n == ref.shape[0] and step % tile_rows == 0, (ref.shape, n)
    return [ref.at[pl.ds(k * step, step)] for k in range(n)]


def _half(ref, which):
    half = ref.shape[0] // 2
    return ref.at[pl.ds(pl.multiple_of(which * half, 2 * SUBLANES), half)]


def _staged_copy(src, dst, buf, sems):
    step = buf.shape[1]
    n = src.shape[0] // step
    assert n * step == src.shape[0], (src.shape, step)
    ins = [pltpu.make_async_copy(src.at[pl.ds(k * step, step)], buf.at[k % 2], sems.at[k % 2]) for k in range(n)]
    outs = [pltpu.make_async_copy(buf.at[k % 2], dst.at[pl.ds(k * step, step)], sems.at[2 + k % 2]) for k in range(n)]
    ins[0].start()
    for k in range(n):
        ins[k].wait()
        outs[k].start()
        if k + 1 < n:
            if k >= 1:
                outs[k - 1].wait()
            ins[k + 1].start()
    if n >= 2:
        outs[n - 2].wait()
    outs[n - 1].wait()


def _stage_scratch(cols, dtype):
    return [pltpu.VMEM((2, STAGE_ROWS, cols), dtype), pltpu.SemaphoreType.DMA((4,))]


N_LINK_SEMS = (N_CHIPS - 1) * HALF_CHUNKS


def _link_sems():
    return [pltpu.SemaphoreType.DMA((N_LINK_SEMS,)), pltpu.SemaphoreType.DMA((N_LINK_SEMS,))]


def _gather_ici_start(p_ref, o_ref, send_sems, recv_sems):
    x, y, c, chips = _place()
    src = _chunks(_half(p_ref, c), HALF_CHUNKS)
    dst = _chunks(_half(o_ref.at[2 * x + y], c), HALF_CHUNKS)
    for j, chip in enumerate(chips):
        for k in range(HALF_CHUNKS):
            _remote(src[k], dst[k], send_sems, recv_sems, j * HALF_CHUNKS + k, (*chip, c)).start()


def _gather_ici_finish(p_ref, o_ref, send_sems, recv_sems):
    x, y, c, chips = _place()
    src = _chunks(_half(p_ref, c), HALF_CHUNKS)
    for j, (cx, cy) in enumerate(chips):
        for k, landed in enumerate(_chunks(_half(o_ref.at[2 * cx + cy], c), HALF_CHUNKS)):
            _remote(src[k], landed, send_sems, recv_sems, j * HALF_CHUNKS + k, (x, y, c)).wait_recv()
    for j in range(len(chips)):
        for k in range(HALF_CHUNKS):
            _remote(src[k], src[k], send_sems, recv_sems, j * HALF_CHUNKS + k, (x, y, c)).wait_send()


def _gather_ici_hosted(pack_l):
    rows, cols = pack_l.shape
    return _Hosted([pack_l], [jax.ShapeDtypeStruct((N_CHIPS, rows, cols), pack_l.dtype)], _link_sems(),
                   _gather_ici_start, _gather_ici_finish)


def _gather_ici(pack_l, *, name):
    rows, cols = pack_l.shape

    def body(p_ref, o_ref, send_sems, recv_sems):
        _gather_ici_start(p_ref, o_ref, send_sems, recv_sems)
        _gather_ici_finish(p_ref, o_ref, send_sems, recv_sems)

    return pl.pallas_call(
        body, name=name, in_specs=[ANY], out_specs=ANY,
        out_shape=jax.ShapeDtypeStruct((N_CHIPS, rows, cols), pack_l.dtype), scratch_shapes=_link_sems(),
    )(pack_l)


def _gather_finish(partial, pack_l, *, name):
    _, rows, cols = partial.shape

    def body(_, p_ref, o_ref, send_sems, recv_sems, buf, loc_sems):
        x, y, c, chips = _place()
        sib = (x, y, 1 - c)
        passed = []
        for j, (cx, cy) in enumerate(chips):
            for k, landed in enumerate(_chunks(_half(o_ref.at[2 * cx + cy], c), HALF_CHUNKS)):
                passed.append(_remote(landed, landed, send_sems, recv_sems, j * HALF_CHUNKS + k, sib))
        for cp in passed:
            cp.start()
        _staged_copy(p_ref, o_ref.at[2 * x + y], buf, loc_sems)
        for j, (cx, cy) in enumerate(chips):
            for k, landed in enumerate(_chunks(_half(o_ref.at[2 * cx + cy], 1 - c), HALF_CHUNKS)):
                _remote(landed, landed, send_sems, recv_sems, j * HALF_CHUNKS + k, sib).wait_recv()
        for cp in passed:
            cp.wait_send()

    return pl.pallas_call(
        body, name=name, in_specs=[ANY, ANY], out_specs=ANY, out_shape=jax.ShapeDtypeStruct(partial.shape, partial.dtype),
        input_output_aliases={0: 0}, scratch_shapes=_link_sems() + _stage_scratch(cols, partial.dtype),
    )(partial, pack_l)


def _grad_sibling_swap(g_pack_l, *, name):
    n_slabs, rows, cols = g_pack_l.shape
    n_sems = n_slabs * HALF_CHUNKS

    def body(g_ref, got_ref, send_sems, recv_sems):
        x, y, c, _ = _place()
        sib = (x, y, 1 - c)
        swaps = [_remote(src, dst, send_sems, recv_sems, s * HALF_CHUNKS + k, sib)
                 for s in range(n_slabs)
                 for k, (src, dst) in enumerate(zip(_chunks(_half(g_ref.at[s], 1 - c), HALF_CHUNKS),
                                                    _chunks(got_ref.at[s], HALF_CHUNKS)))]
        for cp in swaps:
            cp.start()
        for cp in swaps:
            cp.wait_recv()
        for cp in swaps:
            cp.wait_send()

    return pl.pallas_call(
        body, name=name, in_specs=[ANY], out_specs=ANY,
        out_shape=jax.ShapeDtypeStruct((n_slabs, rows // 2, cols), g_pack_l.dtype),
        scratch_shapes=[pltpu.SemaphoreType.DMA((n_sems,)), pltpu.SemaphoreType.DMA((n_sems,))],
    )(g_pack_l)


def _grad_ici_start(s_ref, o_ref, send_sems, recv_sems, buf, loc_sems):
    x, y, c, chips = _place()
    for j, (cx, cy) in enumerate(chips):
        pairs = zip(_chunks(s_ref.at[2 * cx + cy], HALF_CHUNKS), _chunks(o_ref.at[2 * x + y], HALF_CHUNKS))
        for k, (src, dst) in enumerate(pairs):
            _remote(src, dst, send_sems, recv_sems, j * HALF_CHUNKS + k, (cx, cy, c)).start()


def _grad_ici_finish(s_ref, o_ref, send_sems, recv_sems, buf, loc_sems):
    x, y, c, chips = _place()
    me = 2 * x + y
    _staged_copy(s_ref.at[me], o_ref.at[me], buf, loc_sems)
    for j, (cx, cy) in enumerate(chips):
        for k, landed in enumerate(_chunks(o_ref.at[2 * cx + cy], HALF_CHUNKS)):
            _remote(landed, landed, send_sems, recv_sems, j * HALF_CHUNKS + k, (x, y, c)).wait_recv()
    for j, (cx, cy) in enumerate(chips):
        for k, sent in enumerate(_chunks(s_ref.at[2 * cx + cy], HALF_CHUNKS)):
            _remote(sent, sent, send_sems, recv_sems, j * HALF_CHUNKS + k, (x, y, c)).wait_send()


def _grad_ici_hosted(s):
    return _Hosted([s], [jax.ShapeDtypeStruct(s.shape, s.dtype)], _link_sems() + _stage_scratch(s.shape[-1], s.dtype),
                   _grad_ici_start, _grad_ici_finish)


def _grad_ici(s, *, name):
    def body(*refs):
        _grad_ici_start(*refs)
        _grad_ici_finish(*refs)

    return pl.pallas_call(
        body, name=name, in_specs=[ANY], out_specs=ANY, out_shape=jax.ShapeDtypeStruct(s.shape, s.dtype),
        scratch_shapes=_link_sems() + _stage_scratch(s.shape[-1], s.dtype),
    )(s)


def _grad_sibling_share(t, *, name):
    half, cols = t.shape
    n_ch = 2 * HALF_CHUNKS

    def body(t_ref, o_ref, send_sems, recv_sems, buf, loc_sems):
        x, y, c, _ = _place()
        sib = (x, y, 1 - c)
        sends = [_remote(src, dst, send_sems, recv_sems, k, sib)
                 for k, (src, dst) in enumerate(zip(_chunks(t_ref, n_ch), _chunks(_half(o_ref, c), n_ch)))]
        for cp in sends:
            cp.start()
        _staged_copy(t_ref, _half(o_ref, c), buf, loc_sems)
        for k, landed in enumerate(_chunks(_half(o_ref, 1 - c), n_ch)):
            _remote(landed, landed, send_sems, recv_sems, k, sib).wait_recv()
        for cp in sends:
            cp.wait_send()

    return pl.pallas_call(
        body, name=name, in_specs=[ANY], out_specs=ANY, out_shape=jax.ShapeDtypeStruct((2 * half, cols), t.dtype),
        scratch_shapes=[pltpu.SemaphoreType.DMA((n_ch,)), pltpu.SemaphoreType.DMA((n_ch,))] + _stage_scratch(cols, t.dtype),
    )(t)


def _allgather8(v, *, name):
    rows, cols = v.shape

    def body(v_ref, o_ref, send_sems, recv_sems, loc_sem):
        x, y, c, chips = _place()
        sib = (x, y, 1 - c)

        def slot(px, py, pc):
            return o_ref.at[4 * px + 2 * py + pc]

        local = pltpu.make_async_copy(v_ref, slot(x, y, c), loc_sem.at[0])
        local.start()
        first = [_remote(v_ref, slot(x, y, c), send_sems, recv_sems, 0, sib)]
        first += [_remote(v_ref, slot(x, y, c), send_sems, recv_sems, 1 + j, (*chip, c)) for j, chip in enumerate(chips)]
        for cp in first:
            cp.start()
        passed = [_remote(slot(*chip, c), slot(*chip, c), send_sems, recv_sems, 4 + j, sib)
                  for j, chip in enumerate(chips)]
        for j, chip in enumerate(chips):
            _remote(v_ref, slot(*chip, c), send_sems, recv_sems, 1 + j, sib).wait_recv()
            passed[j].start()
        _remote(v_ref, slot(x, y, 1 - c), send_sems, recv_sems, 0, sib).wait_recv()
        for j, chip in enumerate(chips):
            _remote(v_ref, slot(*chip, 1 - c), send_sems, recv_sems, 4 + j, sib).wait_recv()
        for cp in first + passed:
            cp.wait_send()
        local.wait()

    return pl.pallas_call(
        body, name=name, in_specs=[ANY], out_specs=ANY, out_shape=jax.ShapeDtypeStruct((N_DEV, rows, cols), v.dtype),
        scratch_shapes=[pltpu.SemaphoreType.DMA((7,)), pltpu.SemaphoreType.DMA((7,)), pltpu.SemaphoreType.DMA((1,))],
    )(v)


_BIG = (("w_mod", (D, 6 * D), 1), ("w_in", (D, IN_W), 1), ("w_branch", (3 * D, D), None), ("w_out", (D, D), 0),
        ("w_up", (D, 2 * D_FF), 1), ("w_down", (D_FF, D), 0))


def _shard_rows(name):
    shape = dict((n, s) for n, s, _ in _BIG)[name]
    return shape[0] * shape[1] // N_CHIPS // D


PACK_ROWS = sum(_shard_rows(n) for n, _, _ in _BIG)


def _pack_shards(shards):
    return jnp.concatenate([shards[n].reshape(DEPTH, _shard_rows(n), D) for n, _, _ in _BIG], axis=1)


def _unpack_full(gathered):
    out, off = {}, 0
    for name, shape, axis in _BIG:
        r = _shard_rows(name)
        blk = gathered[:, off:off + r, :]
        off += r
        if name == "w_branch":
            out[name] = blk.reshape(N_CHIPS, 3, D // N_CHIPS, D).transpose(1, 0, 2, 3).reshape(3, D, D)
        elif axis == 0:
            out[name] = blk.reshape(shape)
        else:
            out[name] = blk.reshape(N_CHIPS, shape[0], shape[1] // N_CHIPS).transpose(1, 0, 2).reshape(shape)
    return out


def _pack_grads(grads):
    parts = []
    for name, shape, axis in _BIG:
        g = grads[name]
        r = _shard_rows(name)
        if name == "w_branch":
            g = g.reshape(3, N_CHIPS, D // N_CHIPS, D).transpose(1, 0, 2, 3)
        elif axis == 1:
            g = g.reshape(shape[0], N_CHIPS, shape[1] // N_CHIPS).transpose(1, 0, 2)
        parts.append(g.reshape(N_CHIPS, r, D))
    return jnp.concatenate(parts, axis=1)


def _unpack_shards(total, like):
    out, off = {}, 0
    for name, _, _ in _BIG:
        r = _shard_rows(name)
        out[name] = total[:, off:off + r, :].reshape(like[name].shape)
        off += r
    return out


def _pad_rows(v, rows):
    return jnp.concatenate([v, jnp.zeros((rows - v.shape[0],) + v.shape[1:], v.dtype)], axis=0)


def _local_step(x_tok, target, c_vec, c_ctx, wfull, small, n_lat, n_ctx):
    ctx = _step_context(c_vec, c_ctx, n_lat, n_ctx)
    saved = []
    xs = x_tok
    for l in range(DEPTH):
        xs, s, _ = _layer_fwd(l, xs, wfull[l], {k: v[l] for k, v in small.items() if k != "g_final"}, ctx)
        saved.append(s)
    dx, sq_err, d_g_final = _loss_bwd(xs, target, small["g_final"], n_lat)
    wgrads, lgrads, d_a128 = [None] * DEPTH, [None] * DEPTH, [None] * DEPTH
    for l in reversed(range(DEPTH)):
        dx, wgrads[l], lgrads[l], d_a128[l], _ = _layer_bwd(l, saved[l], wfull[l], dx, ctx)
    return sq_err, dx, wgrads, _small_grads(lgrads, d_a128, d_g_final, ctx)


def _step_context(c_vec, c_ctx, n_lat, n_ctx):
    cos_t, sin_t = _rope_tables(n_lat, n_ctx)
    a_in = _pad_rows(jnp.stack([c_vec, c_ctx]), LANES)
    a128 = _small(_silu, (LANES, D), a_in, name="cond_silu")
    return dict(cos_t=cos_t, sin_t=sin_t, a_in=a_in, a128=a128, n_lat=n_lat, n_ctx=n_ctx)


def _loss_bwd(xs, target, g_final, n_lat):
    dx, st = _loss_head(xs, target, g_final[None, :], n_lat, name="loss_head")
    return dx, st[1], st[0]


def _small_grads(lgrads, d_a128, d_g_final, ctx):
    d_cond = _small(lambda a, b, cin: (a + b) * _dsilu(cin), (LANES, D), d_a128[0], d_a128[1], ctx["a_in"],
                    name="cond_bwd")
    out = {k: jnp.stack([lgrads[l][k] for l in range(DEPTH)]) for k in lgrads[0]}
    out["c_ctx"] = d_cond[1]
    out["g_final"] = d_g_final
    return out


def _layer_fwd(l, xs, w, sm, ctx, hosted=_NO_EXCHANGE):
    n_lat, n_ctx, cos_t, sin_t, a128 = ctx["n_lat"], ctx["n_ctx"], ctx["cos_t"], ctx["sin_t"], ctx["a128"]
    mod128 = _mm(a128, w["w_mod"], name=f"mod{l}")
    mod8 = _small(lambda m, b: m + b, (SUBLANES, 6 * D), mod128[:SUBLANES], sm["b_mod"][None, :], name=f"mod_bias{l}")
    g_mix = sm["g_mix"][None, :]
    g_ffn = sm["g_ffn"][None, :]
    g_v = sm["g_v"][None, :]
    b_gate = sm["b_gate"][None, :]
    sink_b = jnp.broadcast_to(sm["sink"][:, None], (N_HEADS, LANES))
    b_sb = jnp.broadcast_to(sm["b_spatial"][:, :, None], (A_GROUPS, BLK, LANES))
    w_sconv8 = _pad_rows(sm["w_sconv"], SUBLANES)
    w_fconv8 = _pad_rows(sm["w_fconv"], SUBLANES)
    w_in = w["w_in"]
    w_seg = [w_in[:, SEG[k]:SEG[k + 1]] for k in range(4)]

    h = _norm_mod_fwd(xs, g_mix, mod8, 0, 1, n_lat, name=f"norm1_{l}")
    z_qkv, z_a, z_b, z_g = [_mm(h, w_seg[k], name=f"in_proj{k}_{l}") for k in range(4)]
    q, kd, vd = _qkv_prep(z_qkv, cos_t, sin_t, name=f"qkv_prep{l}")
    y_attn, carried = _attention_fwd(q, kd, vd, sink_b, n_lat, n_ctx, name=f"attn{l}", hosted=hosted)
    y_a = _gating_fwd(z_a, sm["w_spatial"], b_sb, g_v, name=f"gating{l}")
    y_b = _sconv_fwd(z_b, w_sconv8, n_lat, name=f"sconv{l}")
    ys = (y_attn, y_a, y_b)
    ts = [_mm(ys[k], w["w_branch"][k], name=f"branch{k}_{l}") for k in range(3)]
    merged = _merge_fwd(*ts, z_g, b_gate, name=f"merge{l}")
    mix_out = _mm(merged, w["w_out"], name=f"out_proj{l}")
    x1 = _residual_fwd(xs, mix_out, mod8, 2, n_lat, name=f"res1_{l}")
    h2 = _norm_mod_fwd(x1, g_ffn, mod8, 3, 4, n_lat, name=f"norm2_{l}")
    up = _mm(h2, w["w_up"], name=f"up_proj{l}")
    cv, f = _ffn_mid_fwd(up, w_fconv8, n_lat, name=f"ffn_mid{l}")
    ffn_out = _mm(f, w["w_down"], name=f"down_proj{l}")
    x2 = _residual_fwd(x1, ffn_out, mod8, 5, n_lat, name=f"res2_{l}")
    saved = dict(x0=xs, mod8=mod8, h=h, z_qkv=z_qkv, z_a=z_a, z_b=z_b, z_g=z_g, q=q, kd=kd, vd=vd, ys=ys, ts=ts,
                 merged=merged, mix_out=mix_out, x1=x1, h2=h2, up=up, cv=cv, f=f, ffn_out=ffn_out, w_seg=w_seg,
                 g_mix=g_mix, g_ffn=g_ffn, g_v=g_v, b_gate=b_gate, sink_b=sink_b, b_sb=b_sb,
                 w_sconv8=w_sconv8, w_fconv8=w_fconv8, w_spatial=sm["w_spatial"])
    return x2, saved, carried


def _layer_bwd(l, s, w, dx, ctx, hosted=_NO_EXCHANGE):
    n_lat, n_ctx, cos_t, sin_t, a128 = ctx["n_lat"], ctx["n_ctx"], ctx["cos_t"], ctx["sin_t"], ctx["a128"]
    mod8 = s["mod8"]
    d_ffn, st_gt2 = _residual_bwd(dx, s["ffn_out"], mod8, 5, n_lat, name=f"res2_bwd{l}")
    df = _mm(d_ffn, w["w_down"], tb=True, name=f"down_bwd_x{l}")
    g_down = _mm(s["f"], d_ffn, ta=True, out_dtype=BF16, name=f"down_bwd_w{l}")
    d_up, st_fc = _ffn_mid_bwd(s["up"], s["cv"], df, s["w_fconv8"], n_lat, name=f"ffn_mid_bwd{l}")
    dh2 = _mm(d_up, w["w_up"], tb=True, name=f"up_bwd_x{l}")
    g_up = _mm(s["h2"], d_up, ta=True, out_dtype=BF16, name=f"up_bwd_w{l}")
    dx1, st_n2 = _norm_mod_bwd(s["x1"], [dh2], dx, s["g_ffn"], mod8, 4, n_lat, name=f"norm2_bwd{l}")
    d_out, st_gt1 = _residual_bwd(dx1, s["mix_out"], mod8, 2, n_lat, name=f"res1_bwd{l}")
    d_merged = _mm(d_out, w["w_out"], tb=True, name=f"out_bwd_x{l}")
    g_out = _mm(s["merged"], d_out, ta=True, out_dtype=BF16, name=f"out_bwd_w{l}")
    dt0, dt1, dt2, dz_g, st_bg = _merge_bwd(d_merged, *s["ts"], s["z_g"], s["b_gate"], name=f"merge_bwd{l}")
    dts = (dt0, dt1, dt2)
    dys = [_mm(dts[k], w["w_branch"][k], tb=True, name=f"branch{k}_bwd_x{l}") for k in range(3)]
    g_branch = jnp.stack([_mm(s["ys"][k], dts[k], ta=True, out_dtype=BF16, name=f"branch{k}_bwd_w{l}")
                          for k in range(3)])
    dq, dk, dv, d_sink, carried = _attention_bwd(s["q"], s["kd"], s["vd"], s["sink_b"], dys[0], n_lat, n_ctx,
                                                 name=f"attn_bwd{l}", hosted=hosted)
    dz_qkv = _qkv_unprep(dq, dk, dv, cos_t, sin_t, name=f"qkv_unprep{l}")
    dz_a, d_ws, d_bs, st_gv = _gating_bwd(s["z_a"], dys[1], s["w_spatial"], s["b_sb"], s["g_v"], name=f"gating_bwd{l}")
    dz_b, st_sc = _sconv_bwd(s["z_b"], dys[2], s["w_sconv8"], n_lat, name=f"sconv_bwd{l}")
    dzs = (dz_qkv, dz_a, dz_b, dz_g)
    dh_parts = [_mm(dzs[k], s["w_seg"][k], tb=True, name=f"in_bwd_x{k}_{l}") for k in range(4)]
    g_in = jnp.concatenate([_mm(s["h"], dzs[k], ta=True, out_dtype=BF16, name=f"in_bwd_w{k}_{l}")
                            for k in range(4)], axis=1)
    dx0, st_n1 = _norm_mod_bwd(s["x0"], dh_parts, dx1, s["g_mix"], mod8, 1, n_lat, name=f"norm1_bwd{l}")
    dmod = jnp.concatenate([st_n1[0:2], st_n1[2:4], st_gt1[0:2], st_n2[0:2], st_n2[2:4], st_gt2[0:2]], axis=1)
    dmod128 = _pad_rows(dmod, LANES)
    g_mod = _mm(a128, dmod128, ta=True, out_dtype=BF16, name=f"mod_bwd_w{l}")
    d_a128 = _mm(dmod128, w["w_mod"], tb=True, name=f"mod_bwd_x{l}")
    wgrads = dict(w_mod=g_mod, w_in=g_in, w_branch=g_branch.reshape(3 * D, D), w_out=g_out, w_up=g_up, w_down=g_down)
    lgrads = dict(b_mod=dmod[0] + dmod[1], g_mix=st_n1[4], g_ffn=st_n2[4], b_gate=st_bg[0], sink=d_sink[:, 0],
                  w_spatial=d_ws, b_spatial=d_bs[:, :, 0], g_v=st_gv[0], w_sconv=st_sc[0:3], w_fconv=st_fc[0:3])
    return dx0, wgrads, lgrads, d_a128, carried


_SMALL_ORDER = ("c_ctx", "b_mod", "g_mix", "b_gate", "sink", "w_spatial", "b_spatial", "g_v", "w_sconv", "g_ffn",
                "w_fconv", "g_final")


def _flat_pack(parts, width):
    flat = jnp.concatenate([p.reshape(-1).astype(F32) for p in parts])
    rows = -(-flat.shape[0] // (width * SUBLANES)) * SUBLANES
    flat = jnp.concatenate([flat, jnp.zeros((rows * width - flat.shape[0],), F32)])
    return flat.reshape(rows, width)


def _flat_unpack(packed, likes):
    flat = packed.reshape(-1)
    out, off = [], 0
    for like in likes:
        n = math.prod(like.shape)
        out.append(flat[off:off + n].reshape(like.shape))
        off += n
    return out


def kernel(x, c, ctx, c_ctx, w_mod, b_mod, g_mix, w_in, b_gate, sink, w_spatial, b_spatial, g_v, w_sconv, w_branch, w_out, g_ffn, w_up, w_fconv, w_down, g_final, loss_target, m_c_ctx, m_w_mod, m_b_mod, m_g_mix, m_w_in, m_b_gate, m_sink, m_w_spatial, m_b_spatial, m_g_v, m_w_sconv, m_w_branch, m_w_out, m_g_ffn, m_w_up, m_w_fconv, m_w_down, m_g_final, v_c_ctx, v_w_mod, v_b_mod, v_g_mix, v_w_in, v_b_gate, v_sink, v_w_spatial, v_b_spatial, v_g_v, v_w_sconv, v_w_branch, v_w_out, v_g_ffn, v_w_up, v_w_fconv, v_w_down, v_g_final):
    n_lat, n_ctx = x.shape[1], ctx.shape[1]
    chip = 2 * lax.axis_index("x") + lax.axis_index("y")
    weights = dict(c_ctx=c_ctx, w_mod=w_mod, b_mod=b_mod, g_mix=g_mix, w_in=w_in, b_gate=b_gate, sink=sink,
                   w_spatial=w_spatial, b_spatial=b_spatial, g_v=g_v, w_sconv=w_sconv, w_branch=w_branch, w_out=w_out,
                   g_ffn=g_ffn, w_up=w_up, w_fconv=w_fconv, w_down=w_down, g_final=g_final)
    m_in = dict(c_ctx=m_c_ctx, w_mod=m_w_mod, b_mod=m_b_mod, g_mix=m_g_mix, w_in=m_w_in, b_gate=m_b_gate, sink=m_sink,
                w_spatial=m_w_spatial, b_spatial=m_b_spatial, g_v=m_g_v, w_sconv=m_w_sconv, w_branch=m_w_branch,
                w_out=m_w_out, g_ffn=m_g_ffn, w_up=m_w_up, w_fconv=m_w_fconv, w_down=m_w_down, g_final=m_g_final)
    v_in = dict(c_ctx=v_c_ctx, w_mod=v_w_mod, b_mod=v_b_mod, g_mix=v_g_mix, w_in=v_w_in, b_gate=v_b_gate, sink=v_sink,
                w_spatial=v_w_spatial, b_spatial=v_b_spatial, g_v=v_g_v, w_sconv=v_w_sconv, w_branch=v_w_branch,
                w_out=v_w_out, g_ffn=v_g_ffn, w_up=v_w_up, w_fconv=v_w_fconv, w_down=v_w_down, g_final=v_g_final)
    big_names = [n for n, _, _ in _BIG]

    conv_pack = _flat_pack([w_sconv, w_fconv], LANES)
    conv_all = _allgather8(conv_pack, name="gather_conv_weights")
    conv_parts = [_flat_unpack(conv_all[2 * p], [w_sconv, w_fconv]) for p in range(N_CHIPS)]
    w_sconv_full = jnp.concatenate([cp[0] for cp in conv_parts], axis=-1)
    w_fconv_full = jnp.concatenate([cp[1] for cp in conv_parts], axis=-1)

    small = dict(b_mod=b_mod, g_mix=g_mix, b_gate=b_gate, sink=sink, w_spatial=w_spatial, b_spatial=b_spatial, g_v=g_v,
                 w_sconv=w_sconv_full, g_ffn=g_ffn, w_fconv=w_fconv_full, g_final=g_final)
    x_tok = jnp.concatenate([x[0], ctx[0]], axis=0)
    step = _step_context(c[0], c_ctx, n_lat, n_ctx)
    layer_small = [{k: v[l] for k, v in small.items() if k != "g_final"} for l in range(DEPTH)]
    my_half = lax.axis_index("c").astype(jnp.int32).reshape(1)

    pack = _pack_shards({n: weights[n].astype(BF16) for n in big_names})
    w0 = _unpack_full(_gather_finish(_gather_ici(pack[0], name="gather_ici0"), pack[0], name="gather_finish0"))
    xs, saved0, (partial1,) = _layer_fwd(0, x_tok, w0, layer_small[0], step, hosted=_gather_ici_hosted(pack[1]))
    w1 = _unpack_full(_gather_finish(partial1, pack[1], name="gather_finish1"))
    xs, saved1, _ = _layer_fwd(1, xs, w1, layer_small[1], step)
    dx, sq_err, d_g_final = _loss_bwd(xs, loss_target[0], g_final, n_lat)
    loss = lax.psum(0.5 * jnp.sum(sq_err) / D, ("x", "y", "c"))

    def reduce_start(wgrads_l, l):
        g_pack = _pack_grads(wgrads_l)
        got = _grad_sibling_swap(g_pack, name=f"grad_sibling_swap{l}")
        return _add_half(my_half, g_pack, got, name=f"grad_pair_sum{l}")

    def reduce_finish(exchanged, l):
        return _grad_sibling_share(_sum_slabs(exchanged, F32, name=f"grad_chip_sum{l}"), name=f"grad_sibling_share{l}")

    dx, wgrads1, lgrads1, d_a1, _ = _layer_bwd(1, saved1, w1, dx, step)
    pair_sum1 = reduce_start(wgrads1, 1)
    dx, wgrads0, lgrads0, d_a0, (exchanged1,) = _layer_bwd(0, saved0, w0, dx, step, hosted=_grad_ici_hosted(pair_sum1))
    total1 = reduce_finish(exchanged1, 1)
    total0 = reduce_finish(_grad_ici(reduce_start(wgrads0, 0), name="grad_chip_exchange0"), 0)
    big_grads = _unpack_shards(jnp.stack([total0, total1]), {n: weights[n] for n in big_names})
    sgrads = _small_grads([lgrads0, lgrads1], [d_a0, d_a1], d_g_final, step)
    grad_x = dx[:n_lat][None]

    s_likes = [sgrads[n] for n in _SMALL_ORDER]
    s_all = _allgather8(_flat_pack(s_likes, D), name="gather_small_grads")
    s_tot = _flat_unpack(_sum_slabs(s_all, F32, name="small_grad_sum"), s_likes)
    grads = dict(big_grads)
    for n, g in zip(_SMALL_ORDER, s_tot):
        grads[n] = g
    grads["w_sconv"] = lax.dynamic_slice_in_dim(grads["w_sconv"], chip * w_sconv.shape[-1], w_sconv.shape[-1], axis=2)
    grads["w_fconv"] = lax.dynamic_slice_in_dim(grads["w_fconv"], chip * w_fconv.shape[-1], w_fconv.shape[-1], axis=2)

    delta, new_m, new_v = {}, {}, {}
    for n in big_names:
        cols = weights[n].shape[-1]
        view = lambda a: a.reshape(-1, cols)
        d_, m_, v_ = _adamw(view(weights[n]), view(grads[n]), view(m_in[n]), view(v_in[n]), name=f"adamw_{n}")
        delta[n], new_m[n], new_v[n] = (t.reshape(weights[n].shape) for t in (d_, m_, v_))
    likes = [weights[n] for n in _SMALL_ORDER]
    packs = [_flat_pack([src[n] for n in _SMALL_ORDER], LANES) for src in (weights, grads, m_in, v_in)]
    outs = _adamw(*packs, name="adamw_small")
    for dst, packed in zip((delta, new_m, new_v), outs):
        for n, val in zip(_SMALL_ORDER, _flat_unpack(packed, likes)):
            dst[n] = val

    order = ("c_ctx", "w_mod", "b_mod", "g_mix", "w_in", "b_gate", "sink", "w_spatial", "b_spatial", "g_v", "w_sconv",
             "w_branch", "w_out", "g_ffn", "w_up", "w_fconv", "w_down", "g_final")
    return (loss, grad_x, *[grads[n] for n in order], *[delta[n] for n in order], *[new_m[n] for n in order],
            *[new_v[n] for n in order])
```

```python
import functools
import math

import jax
import jax.numpy as jnp
from jax import lax
from jax.experimental import pallas as pl
from jax.experimental.pallas import tpu as pltpu

F32 = jnp.float32
BF16 = jnp.bfloat16

D = 1024
DEPTH = 2
GRID_W = 64
N_HEADS = 16
N_KV = 4
GRP = N_HEADS // N_KV
HEAD_DIM = 64
KV_W = N_KV * HEAD_DIM
WINDOW = 128
BLK = 128
ROPE_THETA = 10000.0
A_GROUPS = 8
D_FF = 2816
EPS = 1e-6
NEG = -1e30
QKV_W = D + 2 * KV_W
A_COLS = 2 * D
B_COLS = 3 * D
G_COLS = 3 * D
IN_W = QKV_W + A_COLS + B_COLS + G_COLS
SEG = (0, QKV_W, QKV_W + A_COLS, QKV_W + A_COLS + B_COLS, IN_W)
N_CHIPS = 4
N_DEV = 8
LANES = 128
SUBLANES = 8
VMEM_LIMIT = 48 * 1024 * 1024
ADAM_LR = 0.001
ADAM_B1 = 0.9
ADAM_B2 = 0.999
ADAM_EPS = 1e-08
ADAM_WD = 0.01
ADAM_STEP = 10
MESH = pl.DeviceIdType.MESH
ANY = pl.BlockSpec(memory_space=pl.ANY)


def _params(sem=None):
    return pltpu.CompilerParams(dimension_semantics=sem, vmem_limit_bytes=VMEM_LIMIT)


def _pick(n, cands):
    for c in cands:
        if n % c == 0:
            return c
    return n


def _rows8(rows, width):
    r = lax.broadcasted_iota(jnp.int32, (SUBLANES, width), 0)
    out = jnp.zeros((SUBLANES, width), F32)
    for idx, v in rows:
        out = out + jnp.where(r == idx, v, 0.0)
    return out


def _sel(mod_ref, k, is_ctx):
    return jnp.where(is_ctx, mod_ref[1:2, k * D:(k + 1) * D], mod_ref[0:1, k * D:(k + 1) * D])


def _colsum(v):
    return jnp.sum(v, axis=0, keepdims=True)


def _mm(a, b, *, name, ta=False, tb=False, out_dtype=F32):
    if ta:
        k_dim, m = a.shape
    else:
        m, k_dim = a.shape
    if tb:
        n, kb = b.shape
    else:
        kb, n = b.shape
    assert k_dim == kb, (a.shape, b.shape, ta, tb)
    tm = _pick(m, (1056, 1024, 1408, 768, 512, 256, 128))
    tn = _pick(n, (1536, 1408, 1024, 768, 512, 256, 128))
    tk = _pick(k_dim, (2048, 1536, 1408, 1024, 768, 512, 256, 128))
    nk = k_dim // tk
    dims = (((0 if ta else 1,), (1 if tb else 0,)), ((), ()))

    def product(a_ref, b_ref):
        return lax.dot_general(a_ref[...].astype(BF16), b_ref[...].astype(BF16), dims, preferred_element_type=F32)

    def body_single(a_ref, b_ref, o_ref):
        o_ref[...] = product(a_ref, b_ref).astype(o_ref.dtype)

    def body_acc(a_ref, b_ref, o_ref, acc_ref):
        k = pl.program_id(2)

        @pl.when(k == 0)
        def _():
            acc_ref[...] = product(a_ref, b_ref)

        @pl.when(k > 0)
        def _():
            acc_ref[...] += product(a_ref, b_ref)

        @pl.when(k == nk - 1)
        def _():
            o_ref[...] = acc_ref[...].astype(o_ref.dtype)

    a_spec = pl.BlockSpec((tk, tm), lambda i, j, k: (k, i)) if ta else pl.BlockSpec((tm, tk), lambda i, j, k: (i, k))
    b_spec = pl.BlockSpec((tn, tk), lambda i, j, k: (j, k)) if tb else pl.BlockSpec((tk, tn), lambda i, j, k: (k, j))
    return pl.pallas_call(
        body_single if nk == 1 else body_acc, name=name, grid=(m // tm, n // tn, nk),
        in_specs=[a_spec, b_spec], out_specs=pl.BlockSpec((tm, tn), lambda i, j, k: (i, j)),
        out_shape=jax.ShapeDtypeStruct((m, n), out_dtype),
        scratch_shapes=[] if nk == 1 else [pltpu.VMEM((tm, tn), F32)],
        compiler_params=_params(("parallel", "parallel", "arbitrary")),
    )(a, b)


def _small(fn, out_shape, *arrays, name):
    def body(*refs):
        refs[-1][...] = fn(*[r[...] for r in refs[:-1]]).astype(refs[-1].dtype)

    return pl.pallas_call(body, name=name, out_shape=jax.ShapeDtypeStruct(out_shape, F32))(*arrays)


def _silu(v):
    return v * jax.nn.sigmoid(v)


def _dsilu(v):
    s = jax.nn.sigmoid(v)
    return s * (1.0 + v * (1.0 - s))


def _row_spec(tm, width, col=0):
    return pl.BlockSpec((tm, width), lambda i: (i, col))


def _full_spec(shape):
    nd = len(shape)
    return pl.BlockSpec(shape, lambda i: (0,) * nd)


def _halo_specs(tm, width, t_rows, col=0):
    per = tm // SUBLANES
    last = t_rows // SUBLANES - 1
    prev = pl.BlockSpec((SUBLANES, width), lambda i: (jnp.maximum(i * per - 1, 0), col))
    nxt = pl.BlockSpec((SUBLANES, width), lambda i: (jnp.minimum((i + 1) * per, last), col))
    return prev, nxt


def _shift_rows(cur, prev8, next8, n_lat, t_rows, tm):
    i = pl.program_id(0)
    row = lax.broadcasted_iota(jnp.int32, (tm, 1), 0)
    g = row + i * tm
    up = pltpu.roll(cur, 1, 0)
    up = jnp.where(row == 0, prev8[SUBLANES - 1:SUBLANES, :], up)
    up = jnp.where((g == 0) | (g == n_lat), 0.0, up)
    dn = pltpu.roll(cur, tm - 1, 0)
    dn = jnp.where(row == tm - 1, next8[0:1, :], dn)
    dn = jnp.where((g == n_lat - 1) | (g == t_rows - 1), 0.0, dn)
    return up, dn


def _norm_mod_fwd(x, g, mod8, sh_idx, sc_idx, n_lat, *, name):
    t_rows = x.shape[0]
    tm = 256

    def body(x_ref, g_ref, mod_ref, o_ref):
        is_ctx = pl.program_id(0) * tm >= n_lat
        xv = x_ref[...]
        rstd = lax.rsqrt(jnp.mean(xv * xv, axis=-1, keepdims=True) + EPS)
        y = xv * rstd * g_ref[...]
        o_ref[...] = (y * (1.0 + _sel(mod_ref, sc_idx, is_ctx)) + _sel(mod_ref, sh_idx, is_ctx)).astype(BF16)

    return pl.pallas_call(
        body, name=name, grid=(t_rows // tm,),
        in_specs=[_row_spec(tm, D), _full_spec((1, D)), _full_spec((SUBLANES, 6 * D))],
        out_specs=_row_spec(tm, D), out_shape=jax.ShapeDtypeStruct((t_rows, D), BF16),
        compiler_params=_params(("parallel",)),
    )(x, g, mod8)


def _norm_mod_bwd(x, dh_parts, dres, g, mod8, sc_idx, n_lat, *, name):
    t_rows = x.shape[0]
    tm = 256
    n_parts = len(dh_parts)

    def body(*refs):
        x_ref, dres_ref, g_ref, mod_ref = refs[:4]
        part_refs = refs[4:4 + n_parts]
        dx_ref, st_ref = refs[4 + n_parts:]
        i = pl.program_id(0)
        is_ctx = i * tm >= n_lat
        dh = part_refs[0][...]
        for p in part_refs[1:]:
            dh = dh + p[...]
        xv = x_ref[...]
        gv = g_ref[...]
        rstd = lax.rsqrt(jnp.mean(xv * xv, axis=-1, keepdims=True) + EPS)
        rn = xv * rstd
        dy = dh * (1.0 + _sel(mod_ref, sc_idx, is_ctx))
        e = dy * gv
        dx_ref[...] = dres_ref[...] + rstd * (e - rn * jnp.mean(e * rn, axis=-1, keepdims=True))
        dsh = _colsum(dh)
        dsc = _colsum(dh * (rn * gv))
        dg = _colsum(dy * rn)
        zero = jnp.zeros_like(dsh)
        upd = _rows8([(0, jnp.where(is_ctx, zero, dsh)), (1, jnp.where(is_ctx, dsh, zero)),
                      (2, jnp.where(is_ctx, zero, dsc)), (3, jnp.where(is_ctx, dsc, zero)), (4, dg)], D)

        @pl.when(i == 0)
        def _():
            st_ref[...] = upd

        @pl.when(i > 0)
        def _():
            st_ref[...] += upd

    return pl.pallas_call(
        body, name=name, grid=(t_rows // tm,),
        in_specs=[_row_spec(tm, D), _row_spec(tm, D), _full_spec((1, D)), _full_spec((SUBLANES, 6 * D))]
        + [_row_spec(tm, D)] * n_parts,
        out_specs=[_row_spec(tm, D), _full_spec((SUBLANES, D))],
        out_shape=[jax.ShapeDtypeStruct((t_rows, D), F32), jax.ShapeDtypeStruct((SUBLANES, D), F32)],
        compiler_params=_params(("arbitrary",)),
    )(x, dres, g, mod8, *dh_parts)


def _residual_fwd(x, branch, mod8, gt_idx, n_lat, *, name):
    t_rows = x.shape[0]
    tm = 256

    def body(x_ref, b_ref, mod_ref, o_ref):
        is_ctx = pl.program_id(0) * tm >= n_lat
        o_ref[...] = x_ref[...] + _sel(mod_ref, gt_idx, is_ctx) * b_ref[...]

    return pl.pallas_call(
        body, name=name, grid=(t_rows // tm,),
        in_specs=[_row_spec(tm, D), _row_spec(tm, D), _full_spec((SUBLANES, 6 * D))],
        out_specs=_row_spec(tm, D), out_shape=jax.ShapeDtypeStruct((t_rows, D), F32),
        compiler_params=_params(("parallel",)),
    )(x, branch, mod8)


def _residual_bwd(dx, branch, mod8, gt_idx, n_lat, *, name):
    t_rows = dx.shape[0]
    tm = 256

    def body(dx_ref, b_ref, mod_ref, o_ref, st_ref):
        i = pl.program_id(0)
        is_ctx = i * tm >= n_lat
        dxv = dx_ref[...]
        o_ref[...] = (dxv * _sel(mod_ref, gt_idx, is_ctx)).astype(BF16)
        dgt = _colsum(dxv * b_ref[...])
        zero = jnp.zeros_like(dgt)
        upd = _rows8([(0, jnp.where(is_ctx, zero, dgt)), (1, jnp.where(is_ctx, dgt, zero))], D)

        @pl.when(i == 0)
        def _():
            st_ref[...] = upd

        @pl.when(i > 0)
        def _():
            st_ref[...] += upd

    return pl.pallas_call(
        body, name=name, grid=(t_rows // tm,),
        in_specs=[_row_spec(tm, D), _row_spec(tm, D), _full_spec((SUBLANES, 6 * D))],
        out_specs=[_row_spec(tm, D), _full_spec((SUBLANES, D))],
        out_shape=[jax.ShapeDtypeStruct((t_rows, D), BF16), jax.ShapeDtypeStruct((SUBLANES, D), F32)],
        compiler_params=_params(("arbitrary",)),
    )(dx, branch, mod8)


def _rope_tables(n_lat, n_ctx):
    rows = n_lat // GRID_W
    row = jnp.broadcast_to(jnp.arange(rows, dtype=F32)[:, None], (rows, GRID_W)).reshape(n_lat)
    col = jnp.broadcast_to(jnp.arange(GRID_W, dtype=F32)[None, :], (rows, GRID_W)).reshape(n_lat)
    half = HEAD_DIM // 2
    inv = ROPE_THETA ** (-jnp.arange(0, half, 2, dtype=F32) / half)
    ang = jnp.concatenate([row[:, None] * inv, col[:, None] * inv], axis=-1)
    cos, sin = jnp.cos(ang), jnp.sin(ang)
    c64 = jnp.concatenate([cos, cos], axis=-1)
    s64 = jnp.concatenate([-sin, sin], axis=-1)
    c64 = jnp.concatenate([c64, jnp.ones((n_ctx, HEAD_DIM), F32)], axis=0)
    s64 = jnp.concatenate([s64, jnp.zeros((n_ctx, HEAD_DIM), F32)], axis=0)
    return jnp.tile(c64, (1, 2)), jnp.tile(s64, (1, 2))


def _swap_halves(v):
    lane = lax.broadcasted_iota(jnp.int32, v.shape, 1)
    return jnp.where(lane % HEAD_DIM < HEAD_DIM // 2, pltpu.roll(v, LANES - HEAD_DIM // 2, 1),
                     pltpu.roll(v, HEAD_DIM // 2, 1))


def _low_half(shape):
    return lax.broadcasted_iota(jnp.int32, shape, 1) < HEAD_DIM


def _qkv_prep(z_qkv, cos_t, sin_t, *, name):
    t_rows = z_qkv.shape[0]
    tm = 256

    def body(z_ref, c_ref, s_ref, q_ref, k_ref, v_ref):
        cv, sv = c_ref[...], s_ref[...]

        def rope(chunk):
            return chunk * cv + _swap_halves(chunk) * sv

        for ch in range(D // LANES):
            roped = rope(z_ref[:, ch * LANES:(ch + 1) * LANES])
            q_ref[:, ch * LANES:(ch + 1) * LANES] = (roped * (HEAD_DIM ** -0.5)).astype(BF16)
        low = _low_half((tm, LANES))
        for pair in range(N_KV // 2):
            for which, ref, roped in ((0, k_ref, True), (1, v_ref, False)):
                off = D + which * KV_W + pair * LANES
                chunk = z_ref[:, off:off + LANES]
                if roped:
                    chunk = rope(chunk)
                other = pltpu.roll(chunk, HEAD_DIM, 1)
                even = jnp.where(low, chunk, other)
                odd = jnp.where(low, other, chunk)
                ref[:, (2 * pair) * LANES:(2 * pair + 1) * LANES] = even.astype(BF16)
                ref[:, (2 * pair + 1) * LANES:(2 * pair + 2) * LANES] = odd.astype(BF16)

    dup_w = N_KV * LANES
    return pl.pallas_call(
        body, name=name, grid=(t_rows // tm,),
        in_specs=[_row_spec(tm, QKV_W), _row_spec(tm, LANES), _row_spec(tm, LANES)],
        out_specs=[_row_spec(tm, D), _row_spec(tm, dup_w), _row_spec(tm, dup_w)],
        out_shape=[jax.ShapeDtypeStruct((t_rows, D), BF16), jax.ShapeDtypeStruct((t_rows, dup_w), BF16),
                   jax.ShapeDtypeStruct((t_rows, dup_w), BF16)],
        compiler_params=_params(("parallel",)),
    )(z_qkv, cos_t, sin_t)


def _qkv_unprep(dq, dk, dv, cos_t, sin_t, *, name):
    t_rows = dq.shape[0]
    tm = 256

    def body(dq_ref, dk_ref, dv_ref, c_ref, s_ref, o_ref):
        cv, sv = c_ref[...], s_ref[...]

        def unrope(chunk):
            return chunk * cv + _swap_halves(chunk * sv)

        for ch in range(D // LANES):
            o_ref[:, ch * LANES:(ch + 1) * LANES] = unrope(dq_ref[:, ch * LANES:(ch + 1) * LANES]).astype(BF16)
        for pair in range(N_KV // 2):
            for which, ref, roped in ((0, dk_ref, True), (1, dv_ref, False)):
                chunk = ref[:, pair * LANES:(pair + 1) * LANES]
                if roped:
                    chunk = unrope(chunk)
                off = D + which * KV_W + pair * LANES
                o_ref[:, off:off + LANES] = chunk.astype(BF16)

    return pl.pallas_call(
        body, name=name, grid=(t_rows // tm,),
        in_specs=[_row_spec(tm, D), _row_spec(tm, KV_W), _row_spec(tm, KV_W), _row_spec(tm, LANES),
                  _row_spec(tm, LANES)],
        out_specs=_row_spec(tm, QKV_W), out_shape=jax.ShapeDtypeStruct((t_rows, QKV_W), BF16),
        compiler_params=_params(("parallel",)),
    )(dq, dk, dv, cos_t, sin_t)


def _attn_specs(n_lat, n_ctx):
    nb = n_lat // BLK
    dup_w = N_KV * LANES

    def ws(j):
        return jnp.clip(j - 1, 0, nb - 3)

    win = [pl.BlockSpec((BLK, dup_w), functools.partial(lambda j, o: (ws(j) + o, 0), o=o)) for o in range(3)]
    ctx = pl.BlockSpec((n_ctx, dup_w), lambda j: (n_lat // n_ctx, 0))
    return nb, ws, win, ctx


def _attn_bias(j, ws_j, nb, n_ctx):
    n_keys = 3 * BLK + n_ctx
    row = lax.broadcasted_iota(jnp.int32, (BLK, n_keys), 0)
    col = lax.broadcasted_iota(jnp.int32, (BLK, n_keys), 1)
    rel = (ws_j - j) * BLK + col - row
    valid = (col >= 3 * BLK) | ((jnp.abs(rel) <= WINDOW) & (j < nb))
    bias = jnp.where(valid, 0.0, NEG)
    return jnp.concatenate([bias] * GRP, axis=0)


def _attn_probs(q_ref, kk, kh, bias, sink_ref):
    low = _low_half((BLK, LANES))
    qs = []
    for g in range(GRP):
        h = GRP * kh + g
        chunk = q_ref[:, (h // 2) * LANES:(h // 2 + 1) * LANES]
        qs.append(jnp.where(low if h % 2 == 0 else ~low, chunk, jnp.zeros_like(chunk)))
    qs = jnp.concatenate(qs, axis=0)
    s = lax.dot_general(qs, kk, (((1,), (1,)), ((), ())), preferred_element_type=F32) + bias
    snk = jnp.concatenate(
        [jnp.broadcast_to(jnp.max(sink_ref[GRP * kh + g:GRP * kh + g + 1, :], axis=1, keepdims=True), (BLK, 1))
         for g in range(GRP)], axis=0)
    m = jnp.maximum(jnp.max(s, axis=-1, keepdims=True), snk)
    p = jnp.exp(s - m)
    p_snk = jnp.exp(snk - m)
    inv = 1.0 / (jnp.sum(p, axis=-1, keepdims=True) + p_snk)
    return qs, p, p_snk, inv


class _Hosted:
    def __init__(self, arrays, out_shapes, scratch, start, finish):
        self.arrays, self.out_shapes, self.scratch, self.start, self.finish = arrays, out_shapes, scratch, start, finish


_NO_EXCHANGE = _Hosted([], [], [], None, None)


def _split_refs(refs, n_in, n_out, n_scratch, hosted):
    hi, ho, hs = len(hosted.arrays), len(hosted.out_shapes), len(hosted.scratch)
    a = n_in + hi
    b = a + n_out + ho
    ins, h_ins = refs[:n_in], refs[n_in:a]
    outs, h_outs = refs[a:a + n_out], refs[a + n_out:b]
    scr, h_scr = refs[b:b + n_scratch], refs[b + n_scratch:b + n_scratch + hs]
    return ins, outs, scr, (h_ins, h_outs, h_scr)


def _run_hosted(hosted, h_refs, step, n_steps):
    if hosted.start is None:
        return

    flat = [r for group in h_refs for r in group]

    @pl.when(step == 0)
    def _():
        hosted.start(*flat)

    @pl.when(step == n_steps - 1)
    def _():
        hosted.finish(*flat)


def _attention_fwd(q, kd, vd, sink_b, n_lat, n_ctx, *, name, hosted=_NO_EXCHANGE):
    t_rows = q.shape[0]
    nb, ws, win, ctx = _attn_specs(n_lat, n_ctx)
    n_steps = t_rows // BLK

    def body(*refs):
        ins, outs, _, h_refs = _split_refs(refs, 10, 1, 0, hosted)
        q_ref, k0, k1, k2, kc, v0, v1, v2, vc, sink_ref = ins
        o_ref, = outs
        j = pl.program_id(0)
        _run_hosted(hosted, h_refs, j, n_steps)
        ws_j = ws(j)
        low = _low_half((BLK, LANES))
        bias = _attn_bias(j, ws_j, nb, n_ctx)
        for kh in range(N_KV):
            sl = slice(kh * LANES, (kh + 1) * LANES)
            kk = jnp.concatenate([k0[:, sl], k1[:, sl], k2[:, sl], kc[:, sl]], axis=0)
            vv = jnp.concatenate([v0[:, sl], v1[:, sl], v2[:, sl], vc[:, sl]], axis=0)
            _, p, _, inv = _attn_probs(q_ref, kk, kh, bias, sink_ref)
            o = jnp.dot(p.astype(BF16), vv, preferred_element_type=F32) * inv
            for half in range(2):
                even = o[(2 * half) * BLK:(2 * half + 1) * BLK]
                odd = o[(2 * half + 1) * BLK:(2 * half + 2) * BLK]
                ch = 2 * kh + half
                o_ref[:, ch * LANES:(ch + 1) * LANES] = jnp.where(low, even, odd).astype(BF16)

    outs = pl.pallas_call(
        body, name=name, grid=(n_steps,),
        in_specs=[_row_spec(BLK, D)] + win + [ctx] + win + [ctx] + [_full_spec((N_HEADS, LANES))]
        + [ANY] * len(hosted.arrays),
        out_specs=[_row_spec(BLK, D)] + [ANY] * len(hosted.out_shapes),
        out_shape=[jax.ShapeDtypeStruct((t_rows, D), BF16)] + list(hosted.out_shapes),
        scratch_shapes=list(hosted.scratch),
        compiler_params=_params(("arbitrary",)),
    )(q, kd, kd, kd, kd, vd, vd, vd, vd, sink_b, *hosted.arrays)
    return outs[0], outs[1:]


def _attention_bwd(q, kd, vd, sink_b, dy, n_lat, n_ctx, *, name, hosted=_NO_EXCHANGE):
    t_rows = q.shape[0]
    nb, ws, win, ctx = _attn_specs(n_lat, n_ctx)
    n_steps = t_rows // BLK

    def body(*refs):
        ins, outs, scr, h_refs = _split_refs(refs, 11, 4, 3, hosted)
        q_ref, k0, k1, k2, kc, v0, v1, v2, vc, sink_ref, dy_ref = ins
        dq_ref, dk_hbm, dv_hbm, ds_ref = outs
        dk_acc, dv_acc, sem = scr
        j = pl.program_id(0)
        _run_hosted(hosted, h_refs, j, n_steps)
        ws_j = ws(j)

        @pl.when(j == 0)
        def _():
            dk_acc[...] = jnp.zeros_like(dk_acc)
            dv_acc[...] = jnp.zeros_like(dv_acc)
            ds_ref[...] = jnp.zeros_like(ds_ref)

        low = _low_half((BLK, LANES))
        low_keys = _low_half((3 * BLK + n_ctx, LANES))
        win_start = pl.multiple_of(ws_j * BLK, BLK)
        scale = HEAD_DIM ** -0.5
        dk_heads, dv_heads = [], []
        bias = _attn_bias(j, ws_j, nb, n_ctx)
        for kh in range(N_KV):
            sl = slice(kh * LANES, (kh + 1) * LANES)
            kk = jnp.concatenate([k0[:, sl], k1[:, sl], k2[:, sl], kc[:, sl]], axis=0)
            vv = jnp.concatenate([v0[:, sl], v1[:, sl], v2[:, sl], vc[:, sl]], axis=0)
            qs, p, p_snk, inv = _attn_probs(q_ref, kk, kh, bias, sink_ref)
            p = p * inv
            p_snk = p_snk * inv
            dos = []
            for g in range(GRP):
                h = GRP * kh + g
                chunk = dy_ref[:, (h // 2) * LANES:(h // 2 + 1) * LANES]
                dos.append(jnp.where(low if h % 2 == 0 else ~low, chunk, jnp.zeros_like(chunk)).astype(BF16))
            dos = jnp.concatenate(dos, axis=0)
            dp = lax.dot_general(dos, vv, (((1,), (1,)), ((), ())), preferred_element_type=F32)
            dsum = jnp.sum(p * dp, axis=-1, keepdims=True)
            ds = (p * (dp - dsum)).astype(BF16)
            for g in range(GRP):
                contrib = -jnp.sum(p_snk[g * BLK:(g + 1) * BLK] * dsum[g * BLK:(g + 1) * BLK], axis=0, keepdims=True)
                ds_ref[GRP * kh + g:GRP * kh + g + 1, :] += jnp.broadcast_to(contrib, (1, LANES))
            dqs = jnp.dot(ds, kk, preferred_element_type=F32) * scale
            for half in range(2):
                even = dqs[(2 * half) * BLK:(2 * half + 1) * BLK]
                odd = dqs[(2 * half + 1) * BLK:(2 * half + 2) * BLK]
                ch = 2 * kh + half
                dq_ref[:, ch * LANES:(ch + 1) * LANES] = jnp.where(low, even, odd)
            dkk = lax.dot_general(ds, qs, (((0,), (0,)), ((), ())), preferred_element_type=F32)
            dvv = lax.dot_general(p.astype(BF16), dos, (((0,), (0,)), ((), ())), preferred_element_type=F32)
            dk_heads.append(dkk + pltpu.roll(dkk, HEAD_DIM, 1))
            dv_heads.append(dvv + pltpu.roll(dvv, HEAD_DIM, 1))
        for pair in range(N_KV // 2):
            sl = slice(pair * LANES, (pair + 1) * LANES)
            for acc, heads in ((dk_acc, dk_heads), (dv_acc, dv_heads)):
                both = jnp.where(low_keys, heads[2 * pair], heads[2 * pair + 1])
                acc[pl.ds(win_start, 3 * BLK), sl] += both[:3 * BLK]
                acc[n_lat:n_lat + n_ctx, sl] += both[3 * BLK:]

        @pl.when(j == n_steps - 1)
        def _():
            ck = pltpu.make_async_copy(dk_acc, dk_hbm, sem.at[0])
            cv = pltpu.make_async_copy(dv_acc, dv_hbm, sem.at[1])
            ck.start()
            cv.start()
            ck.wait()
            cv.wait()

    outs = pl.pallas_call(
        body, name=name, grid=(n_steps,),
        in_specs=[_row_spec(BLK, D)] + win + [ctx] + win + [ctx] + [_full_spec((N_HEADS, LANES)), _row_spec(BLK, D)]
        + [ANY] * len(hosted.arrays),
        out_specs=[_row_spec(BLK, D), ANY, ANY, _full_spec((N_HEADS, LANES))] + [ANY] * len(hosted.out_shapes),
        out_shape=[jax.ShapeDtypeStruct((t_rows, D), F32), jax.ShapeDtypeStruct((t_rows, KV_W), F32),
                   jax.ShapeDtypeStruct((t_rows, KV_W), F32), jax.ShapeDtypeStruct((N_HEADS, LANES), F32)]
        + list(hosted.out_shapes),
        scratch_shapes=[pltpu.VMEM((t_rows, KV_W), F32), pltpu.VMEM((t_rows, KV_W), F32),
                        pltpu.SemaphoreType.DMA((2,))] + list(hosted.scratch),
        compiler_params=_params(("arbitrary",)),
    )(q, kd, kd, kd, kd, vd, vd, vd, vd, sink_b, dy, *hosted.arrays)
    return outs[0], outs[1], outs[2], outs[3], outs[4:]


_GELU_K = math.sqrt(2.0 / math.pi)


def _gelu(v):
    return jax.nn.gelu(v)


def _dgelu(v):
    t = jnp.tanh(_GELU_K * (v + 0.044715 * v * v * v))
    return 0.5 * (1.0 + t) + 0.5 * v * (1.0 - t * t) * _GELU_K * (1.0 + 3.0 * 0.044715 * v * v)


def _gating_fwd(z_a, w_s, b_sb, g_v, *, name):
    t_rows = z_a.shape[0]

    def body(z_ref, w_ref, b_ref, g_ref, o_ref):
        u = _gelu(z_ref[:, :D])
        v = _gelu(z_ref[:, D:])
        vn = v * lax.rsqrt(jnp.mean(v * v, axis=-1, keepdims=True) + EPS) * g_ref[...]
        for g in range(A_GROUPS):
            sl = slice(g * LANES, (g + 1) * LANES)
            mixed = jnp.dot(w_ref[g].astype(BF16), vn[:, sl].astype(BF16), preferred_element_type=F32) + b_ref[g]
            o_ref[:, sl] = (u[:, sl] * mixed).astype(BF16)

    return pl.pallas_call(
        body, name=name, grid=(t_rows // BLK,),
        in_specs=[_row_spec(BLK, A_COLS), _full_spec((A_GROUPS, BLK, BLK)), _full_spec((A_GROUPS, BLK, LANES)),
                  _full_spec((1, D))],
        out_specs=_row_spec(BLK, D), out_shape=jax.ShapeDtypeStruct((t_rows, D), BF16),
        compiler_params=_params(("parallel",)),
    )(z_a, w_s, b_sb, g_v)


def _gating_bwd(z_a, dy, w_s, b_sb, g_v, *, name):
    t_rows = z_a.shape[0]

    def body(z_ref, dy_ref, w_ref, b_ref, g_ref, dz_ref, dw_ref, db_ref, st_ref):
        i = pl.program_id(0)

        @pl.when(i == 0)
        def _():
            dw_ref[...] = jnp.zeros_like(dw_ref)
            db_ref[...] = jnp.zeros_like(db_ref)
            st_ref[...] = jnp.zeros_like(st_ref)

        zu = z_ref[:, :D]
        zv = z_ref[:, D:]
        u = _gelu(zu)
        v = _gelu(zv)
        gv = g_ref[...]
        rstd = lax.rsqrt(jnp.mean(v * v, axis=-1, keepdims=True) + EPS)
        vh = v * rstd
        vn = vh * gv
        dyv = dy_ref[...]
        dvn = []
        for g in range(A_GROUPS):
            sl = slice(g * LANES, (g + 1) * LANES)
            wg = w_ref[g].astype(BF16)
            vg = vn[:, sl].astype(BF16)
            mixed = jnp.dot(wg, vg, preferred_element_type=F32) + b_ref[g]
            dz_ref[:, sl] = (dyv[:, sl] * mixed * _dgelu(zu[:, sl])).astype(BF16)
            dmixed = dyv[:, sl] * u[:, sl]
            dmb = dmixed.astype(BF16)
            dvn.append(lax.dot_general(wg, dmb, (((0,), (0,)), ((), ())), preferred_element_type=F32))
            dw_ref[g] += lax.dot_general(dmb, vg, (((1,), (1,)), ((), ())), preferred_element_type=F32)
            db_ref[g] += jnp.broadcast_to(jnp.sum(dmixed, axis=-1, keepdims=True), (BLK, LANES))
        dvn = jnp.concatenate(dvn, axis=1)
        st_ref[...] += _rows8([(0, _colsum(dvn * vh))], D)
        e = dvn * gv
        dv = rstd * (e - vh * jnp.mean(e * vh, axis=-1, keepdims=True))
        dz_ref[:, D:] = (dv * _dgelu(zv)).astype(BF16)

    return pl.pallas_call(
        body, name=name, grid=(t_rows // BLK,),
        in_specs=[_row_spec(BLK, A_COLS), _row_spec(BLK, D), _full_spec((A_GROUPS, BLK, BLK)),
                  _full_spec((A_GROUPS, BLK, LANES)), _full_spec((1, D))],
        out_specs=[_row_spec(BLK, A_COLS), _full_spec((A_GROUPS, BLK, BLK)), _full_spec((A_GROUPS, BLK, LANES)),
                   _full_spec((SUBLANES, D))],
        out_shape=[jax.ShapeDtypeStruct((t_rows, A_COLS), BF16), jax.ShapeDtypeStruct((A_GROUPS, BLK, BLK), F32),
                   jax.ShapeDtypeStruct((A_GROUPS, BLK, LANES), F32), jax.ShapeDtypeStruct((SUBLANES, D), F32)],
        compiler_params=_params(("arbitrary",)),
    )(z_a, dy, w_s, b_sb, g_v)


def _sconv_fwd(z_b, w8, n_lat, *, name):
    t_rows = z_b.shape[0]
    tm = 256
    prev, nxt = _halo_specs(tm, B_COLS, t_rows)

    def body(z_ref, zp_ref, zn_ref, w_ref, o_ref):
        p = z_ref[:, D:2 * D] * z_ref[:, 2 * D:]
        pp = zp_ref[:, D:2 * D] * zp_ref[:, 2 * D:]
        pn = zn_ref[:, D:2 * D] * zn_ref[:, 2 * D:]
        up, dn = _shift_rows(p, pp, pn, n_lat, t_rows, tm)
        conv = w_ref[0:1, :] * up + w_ref[1:2, :] * p + w_ref[2:3, :] * dn
        o_ref[...] = (z_ref[:, :D] * conv).astype(BF16)

    return pl.pallas_call(
        body, name=name, grid=(t_rows // tm,),
        in_specs=[_row_spec(tm, B_COLS), prev, nxt, _full_spec((SUBLANES, D))],
        out_specs=_row_spec(tm, D), out_shape=jax.ShapeDtypeStruct((t_rows, D), BF16),
        compiler_params=_params(("parallel",)),
    )(z_b, z_b, z_b, w8)


def _sconv_bwd(z_b, dy, w8, n_lat, *, name):
    t_rows = z_b.shape[0]
    tm = 256
    prev, nxt = _halo_specs(tm, B_COLS, t_rows)
    dprev, dnxt = _halo_specs(tm, D, t_rows)

    def body(z_ref, zp_ref, zn_ref, dy_ref, dyp_ref, dyn_ref, w_ref, dz_ref, st_ref):
        i = pl.program_id(0)
        bg, cg, hb = z_ref[:, :D], z_ref[:, D:2 * D], z_ref[:, 2 * D:]
        p = cg * hb
        pp = zp_ref[:, D:2 * D] * zp_ref[:, 2 * D:]
        pn = zn_ref[:, D:2 * D] * zn_ref[:, 2 * D:]
        up, dn = _shift_rows(p, pp, pn, n_lat, t_rows, tm)
        w0, w1, w2 = w_ref[0:1, :], w_ref[1:2, :], w_ref[2:3, :]
        conv = w0 * up + w1 * p + w2 * dn
        dyv = dy_ref[...]
        dz_ref[:, :D] = (dyv * conv).astype(BF16)
        dcv = dyv * bg
        dcv_up, dcv_dn = _shift_rows(dcv, dyp_ref[...] * zp_ref[:, :D], dyn_ref[...] * zn_ref[:, :D], n_lat, t_rows, tm)
        dp = w0 * dcv_dn + w1 * dcv + w2 * dcv_up
        dz_ref[:, D:2 * D] = (dp * hb).astype(BF16)
        dz_ref[:, 2 * D:] = (dp * cg).astype(BF16)
        upd = _rows8([(0, _colsum(dcv * up)), (1, _colsum(dcv * p)), (2, _colsum(dcv * dn))], D)

        @pl.when(i == 0)
        def _():
            st_ref[...] = upd

        @pl.when(i > 0)
        def _():
            st_ref[...] += upd

    return pl.pallas_call(
        body, name=name, grid=(t_rows // tm,),
        in_specs=[_row_spec(tm, B_COLS), prev, nxt, _row_spec(tm, D), dprev, dnxt, _full_spec((SUBLANES, D))],
        out_specs=[_row_spec(tm, B_COLS), _full_spec((SUBLANES, D))],
        out_shape=[jax.ShapeDtypeStruct((t_rows, B_COLS), BF16), jax.ShapeDtypeStruct((SUBLANES, D), F32)],
        compiler_params=_params(("arbitrary",)),
    )(z_b, z_b, z_b, dy, dy, dy, w8)


def _merge_fwd(t0, t1, t2, z_g, b_gate, *, name):
    t_rows = t0.shape[0]
    tm = 256

    def body(t0_ref, t1_ref, t2_ref, z_ref, b_ref, o_ref):
        acc = None
        for k, t_ref in enumerate((t0_ref, t1_ref, t2_ref)):
            gate = jax.nn.sigmoid(z_ref[:, k * D:(k + 1) * D] + b_ref[:, k * D:(k + 1) * D])
            term = gate * t_ref[...]
            acc = term if acc is None else acc + term
        o_ref[...] = acc.astype(BF16)

    return pl.pallas_call(
        body, name=name, grid=(t_rows // tm,),
        in_specs=[_row_spec(tm, D)] * 3 + [_row_spec(tm, G_COLS), _full_spec((1, G_COLS))],
        out_specs=_row_spec(tm, D), out_shape=jax.ShapeDtypeStruct((t_rows, D), BF16),
        compiler_params=_params(("parallel",)),
    )(t0, t1, t2, z_g, b_gate)


def _merge_bwd(dmerged, t0, t1, t2, z_g, b_gate, *, name):
    t_rows = t0.shape[0]
    tm = 256

    def body(dm_ref, t0_ref, t1_ref, t2_ref, z_ref, b_ref, d0_ref, d1_ref, d2_ref, dz_ref, st_ref):
        i = pl.program_id(0)
        dm = dm_ref[...]
        sums = []
        for k, (t_ref, d_ref) in enumerate(((t0_ref, d0_ref), (t1_ref, d1_ref), (t2_ref, d2_ref))):
            gate = jax.nn.sigmoid(z_ref[:, k * D:(k + 1) * D] + b_ref[:, k * D:(k + 1) * D])
            d_ref[...] = (dm * gate).astype(BF16)
            dzg = dm * t_ref[...] * gate * (1.0 - gate)
            dz_ref[:, k * D:(k + 1) * D] = dzg.astype(BF16)
            sums.append(_colsum(dzg))
        upd = _rows8([(0, jnp.concatenate(sums, axis=1))], G_COLS)

        @pl.when(i == 0)
        def _():
            st_ref[...] = upd

        @pl.when(i > 0)
        def _():
            st_ref[...] += upd

    return pl.pallas_call(
        body, name=name, grid=(t_rows // tm,),
        in_specs=[_row_spec(tm, D)] * 4 + [_row_spec(tm, G_COLS), _full_spec((1, G_COLS))],
        out_specs=[_row_spec(tm, D)] * 3 + [_row_spec(tm, G_COLS), _full_spec((SUBLANES, G_COLS))],
        out_shape=[jax.ShapeDtypeStruct((t_rows, D), BF16)] * 3
        + [jax.ShapeDtypeStruct((t_rows, G_COLS), BF16), jax.ShapeDtypeStruct((SUBLANES, G_COLS), F32)],
        compiler_params=_params(("arbitrary",)),
    )(dmerged, t0, t1, t2, z_g, b_gate)


def _ffn_mid_fwd(up, w8, n_lat, *, name):
    t_rows = up.shape[0]
    tm = 128
    prev, nxt = _halo_specs(tm, D_FF, t_rows)

    def body(a_ref, ap_ref, an_ref, g_ref, w_ref, cv_ref, f_ref):
        a = a_ref[...]
        au, ad = _shift_rows(a, ap_ref[...], an_ref[...], n_lat, t_rows, tm)
        cv = w_ref[0:1, :] * au + w_ref[1:2, :] * a + w_ref[2:3, :] * ad
        cv_ref[...] = cv
        f_ref[...] = (_silu(cv) * g_ref[...]).astype(BF16)

    return pl.pallas_call(
        body, name=name, grid=(t_rows // tm,),
        in_specs=[_row_spec(tm, D_FF), prev, nxt, _row_spec(tm, D_FF, 1), _full_spec((SUBLANES, D_FF))],
        out_specs=[_row_spec(tm, D_FF), _row_spec(tm, D_FF)],
        out_shape=[jax.ShapeDtypeStruct((t_rows, D_FF), F32), jax.ShapeDtypeStruct((t_rows, D_FF), BF16)],
        compiler_params=_params(("parallel",)),
    )(up, up, up, up, w8)


def _ffn_mid_bwd(up, cv, df, w8, n_lat, *, name):
    t_rows = up.shape[0]
    tm = 128
    prev, nxt = _halo_specs(tm, D_FF, t_rows)
    gprev, gnxt = _halo_specs(tm, D_FF, t_rows, 1)

    def body(a_ref, ap_ref, an_ref, g_ref, gp_ref, gn_ref, cv_ref, cp_ref, cn_ref, df_ref, dfp_ref, dfn_ref,
             w_ref, o_ref, st_ref):
        i = pl.program_id(0)
        a = a_ref[...]
        au, ad = _shift_rows(a, ap_ref[...], an_ref[...], n_lat, t_rows, tm)
        cvv = cv_ref[...]
        dfv = df_ref[...]
        o_ref[:, D_FF:] = (dfv * _silu(cvv)).astype(BF16)
        dcv = dfv * g_ref[...] * _dsilu(cvv)
        dcv_p = dfp_ref[...] * gp_ref[...] * _dsilu(cp_ref[...])
        dcv_n = dfn_ref[...] * gn_ref[...] * _dsilu(cn_ref[...])
        du, dd = _shift_rows(dcv, dcv_p, dcv_n, n_lat, t_rows, tm)
        o_ref[:, :D_FF] = (w_ref[0:1, :] * dd + w_ref[1:2, :] * dcv + w_ref[2:3, :] * du).astype(BF16)
        upd = _rows8([(0, _colsum(dcv * au)), (1, _colsum(dcv * a)), (2, _colsum(dcv * ad))], D_FF)

        @pl.when(i == 0)
        def _():
            st_ref[...] = upd

        @pl.when(i > 0)
        def _():
            st_ref[...] += upd

    row = _row_spec(tm, D_FF)
    return pl.pallas_call(
        body, name=name, grid=(t_rows // tm,),
        in_specs=[row, prev, nxt, _row_spec(tm, D_FF, 1), gprev, gnxt, row, prev, nxt, row, prev, nxt,
                  _full_spec((SUBLANES, D_FF))],
        out_specs=[_row_spec(tm, 2 * D_FF), _full_spec((SUBLANES, D_FF))],
        out_shape=[jax.ShapeDtypeStruct((t_rows, 2 * D_FF), BF16), jax.ShapeDtypeStruct((SUBLANES, D_FF), F32)],
        compiler_params=_params(("arbitrary",)),
    )(up, up, up, up, up, up, cv, cv, cv, df, df, df, w8)


def _loss_head(x, target, g_final, n_lat, *, name):
    t_rows = x.shape[0]
    tm = 256
    last = n_lat // tm - 1

    def body(x_ref, t_ref, g_ref, dx_ref, st_ref):
        i = pl.program_id(0)
        is_ctx = i * tm >= n_lat
        xv = x_ref[...]
        gv = g_ref[...]
        rstd = lax.rsqrt(jnp.mean(xv * xv, axis=-1, keepdims=True) + EPS)
        rn = xv * rstd
        err = rn * gv - t_ref[...]
        dy = err / D
        e = dy * gv
        dx = rstd * (e - rn * jnp.mean(e * rn, axis=-1, keepdims=True))
        dx_ref[...] = jnp.where(is_ctx, 0.0, dx)
        keep = jnp.where(is_ctx, 0.0, 1.0)
        upd = _rows8([(0, keep * _colsum(dy * rn)), (1, keep * _colsum(err * err))], D)

        @pl.when(i == 0)
        def _():
            st_ref[...] = upd

        @pl.when(i > 0)
        def _():
            st_ref[...] += upd

    return pl.pallas_call(
        body, name=name, grid=(t_rows // tm,),
        in_specs=[_row_spec(tm, D), pl.BlockSpec((tm, D), lambda i: (jnp.minimum(i, last), 0)), _full_spec((1, D))],
        out_specs=[_row_spec(tm, D), _full_spec((SUBLANES, D))],
        out_shape=[jax.ShapeDtypeStruct((t_rows, D), F32), jax.ShapeDtypeStruct((SUBLANES, D), F32)],
        compiler_params=_params(("arbitrary",)),
    )(x, target, g_final)


def _sum_slabs(x, out_dtype, *, name):
    n_slabs, rows, cols = x.shape
    tm = _pick(rows, (432, 256, 192, 128, 64, 32, 24, 16, 8))

    def body(x_ref, o_ref):
        acc = x_ref[0].astype(F32)
        for s in range(1, n_slabs):
            acc = acc + x_ref[s].astype(F32)
        o_ref[...] = acc.astype(o_ref.dtype)

    return pl.pallas_call(
        body, name=name, grid=(rows // tm,),
        in_specs=[pl.BlockSpec((n_slabs, tm, cols), lambda i: (0, i, 0))],
        out_specs=pl.BlockSpec((tm, cols), lambda i: (i, 0)),
        out_shape=jax.ShapeDtypeStruct((rows, cols), out_dtype),
        compiler_params=_params(("parallel",)),
    )(x)


def _add_half(half_idx, a, b, *, name):
    n_slabs, rows, cols = b.shape
    tm = _pick(rows, (432, 256, 192, 128, 96, 64, 32, 16))
    per_half = rows // tm

    def body(half_ref, a_ref, b_ref, o_ref):
        o_ref[...] = (a_ref[...].astype(F32) + b_ref[...].astype(F32)).astype(BF16)

    spec = pl.BlockSpec((1, tm, cols), lambda s, i, half_ref: (s, i, 0))
    a_spec = pl.BlockSpec((1, tm, cols), lambda s, i, half_ref: (s, half_ref[0] * per_half + i, 0))
    return pl.pallas_call(
        body, name=name,
        grid_spec=pltpu.PrefetchScalarGridSpec(num_scalar_prefetch=1, grid=(n_slabs, per_half),
                                               in_specs=[a_spec, spec], out_specs=spec),
        out_shape=jax.ShapeDtypeStruct(b.shape, BF16), compiler_params=_params(("parallel", "parallel")),
    )(half_idx, a, b)


def _adamw(w, g, m, v, *, name):
    rows, cols = w.shape
    tm = _pick(rows, (128, 64, 32, 16, 8))

    def body(w_ref, g_ref, m_ref, v_ref, d_ref, nm_ref, nv_ref):
        gv = g_ref[...]
        nm = ADAM_B1 * m_ref[...] + (1.0 - ADAM_B1) * gv
        nv = ADAM_B2 * v_ref[...] + (1.0 - ADAM_B2) * jnp.square(gv)
        m_hat = nm / (1.0 - ADAM_B1 ** ADAM_STEP)
        v_hat = nv / (1.0 - ADAM_B2 ** ADAM_STEP)
        d_ref[...] = -ADAM_LR * (m_hat / (jnp.sqrt(v_hat) + ADAM_EPS) + ADAM_WD * w_ref[...])
        nm_ref[...] = nm
        nv_ref[...] = nv

    spec = pl.BlockSpec((tm, cols), lambda i: (i, 0))
    shape = jax.ShapeDtypeStruct((rows, cols), F32)
    return pl.pallas_call(
        body, name=name, grid=(rows // tm,), in_specs=[spec] * 4, out_specs=[spec] * 3, out_shape=[shape] * 3,
        compiler_params=_params(("parallel",)),
    )(w, g, m, v)


def _place():
    x, y, c = lax.axis_index("x"), lax.axis_index("y"), lax.axis_index("c")
    chips = [(1 - x, y), (x, 1 - y), (1 - x, 1 - y)]
    return x, y, c, chips


def _remote(src, dst, send_sems, recv_sems, k, to):
    return pltpu.make_async_remote_copy(src_ref=src, dst_ref=dst, send_sem=send_sems.at[k], recv_sem=recv_sems.at[k],
                                        device_id=to, device_id_type=MESH)


HALF_CHUNKS = 2


def _chunks(ref, n):
    step = ref.shape[0] // n
    tile_rows = SUBLANES if ref.dtype == F32 else 2 * SUBLANES
    assert step * n == ref.shape[0] and step % tile_rows == 0, (ref.shape, n)
    return [ref.at[pl.ds(k * step, step)] for k in range(n)]


def _half(ref, which):
    half = ref.shape[0] // 2
    return ref.at[pl.ds(pl.multiple_of(which * half, 2 * SUBLANES), half)]


def _staged_copy(src, dst, buf, sems):
    step = buf.shape[1]
    n = src.shape[0] // step
    assert n * step == src.shape[0], (src.shape, step)
    ins = [pltpu.make_async_copy(src.at[pl.ds(k * step, step)], buf.at[k % 2], sems.at[k % 2]) for k in range(n)]
    outs = [pltpu.make_async_copy(buf.at[k % 2], dst.at[pl.ds(k * step, step)], sems.at[2 + k % 2]) for k in range(n)]
    ins[0].start()
    for k in range(n):
        ins[k].wait()
        outs[k].start()
        if k + 1 < n:
            if k >= 1:
                outs[k - 1].wait()
            ins[k + 1].start()
    if n >= 2:
        outs[n - 2].wait()
    outs[n - 1].wait()


def _stage_rows(rows):
    return _pick(rows, (256, 432))


def _stage_scratch(slabs):
    return [pltpu.VMEM((2, _stage_rows(s.shape[-2]), s.shape[-1]), s.dtype) for s in slabs] + [pltpu.SemaphoreType.DMA((4,))]


N_LINK_SEMS = (N_CHIPS - 1) * HALF_CHUNKS


def _link_sems(n_groups):
    return [pltpu.SemaphoreType.DMA((n_groups * N_LINK_SEMS,)), pltpu.SemaphoreType.DMA((n_groups * N_LINK_SEMS,))]


def _sem_index(g, j, k):
    return g * N_LINK_SEMS + j * HALF_CHUNKS + k


def _gather_ici_start(*refs):
    n = (len(refs) - 2) // 2
    p_refs, o_refs, (send_sems, recv_sems) = refs[:n], refs[n:2 * n], refs[2 * n:]
    x, y, c, chips = _place()
    for g, (p_ref, o_ref) in enumerate(zip(p_refs, o_refs)):
        src = _chunks(_half(p_ref, c), HALF_CHUNKS)
        dst = _chunks(_half(o_ref.at[2 * x + y], c), HALF_CHUNKS)
        for j, chip in enumerate(chips):
            for k in range(HALF_CHUNKS):
                _remote(src[k], dst[k], send_sems, recv_sems, _sem_index(g, j, k), (*chip, c)).start()


def _gather_ici_finish(*refs):
    n = (len(refs) - 2) // 2
    p_refs, o_refs, (send_sems, recv_sems) = refs[:n], refs[n:2 * n], refs[2 * n:]
    x, y, c, chips = _place()
    for g, (p_ref, o_ref) in enumerate(zip(p_refs, o_refs)):
        src = _chunks(_half(p_ref, c), HALF_CHUNKS)
        for j, (cx, cy) in enumerate(chips):
            for k, landed in enumerate(_chunks(_half(o_ref.at[2 * cx + cy], c), HALF_CHUNKS)):
                _remote(src[k], landed, send_sems, recv_sems, _sem_index(g, j, k), (x, y, c)).wait_recv()
        for j in range(len(chips)):
            for k in range(HALF_CHUNKS):
                _remote(src[k], src[k], send_sems, recv_sems, _sem_index(g, j, k), (x, y, c)).wait_send()


def _gathered_shapes(slabs):
    return [jax.ShapeDtypeStruct((N_CHIPS,) + s.shape, s.dtype) for s in slabs]


def _gather_ici_hosted(slabs):
    return _Hosted(list(slabs), _gathered_shapes(slabs), _link_sems(len(slabs)), _gather_ici_start, _gather_ici_finish)


def _gather_ici(slabs, *, name):
    def body(*refs):
        _gather_ici_start(*refs)
        _gather_ici_finish(*refs)

    return pl.pallas_call(
        body, name=name, in_specs=[ANY] * len(slabs), out_specs=[ANY] * len(slabs),
        out_shape=_gathered_shapes(slabs), scratch_shapes=_link_sems(len(slabs)),
    )(*slabs)


def _gather_finish(partials, slabs, *, name):
    n = len(slabs)

    def body(*refs):
        p_refs, o_refs = refs[n:2 * n], refs[2 * n:3 * n]
        send_sems, recv_sems = refs[3 * n:3 * n + 2]
        bufs, loc_sems = refs[3 * n + 2:4 * n + 2], refs[4 * n + 2]
        x, y, c, chips = _place()
        sib = (x, y, 1 - c)
        passed = []
        for g, o_ref in enumerate(o_refs):
            for j, (cx, cy) in enumerate(chips):
                for k, landed in enumerate(_chunks(_half(o_ref.at[2 * cx + cy], c), HALF_CHUNKS)):
                    passed.append(_remote(landed, landed, send_sems, recv_sems, _sem_index(g, j, k), sib))
        for cp in passed:
            cp.start()
        for p_ref, o_ref, buf in zip(p_refs, o_refs, bufs):
            _staged_copy(p_ref, o_ref.at[2 * x + y], buf, loc_sems)
        for g, o_ref in enumerate(o_refs):
            for j, (cx, cy) in enumerate(chips):
                for k, landed in enumerate(_chunks(_half(o_ref.at[2 * cx + cy], 1 - c), HALF_CHUNKS)):
                    _remote(landed, landed, send_sems, recv_sems, _sem_index(g, j, k), sib).wait_recv()
        for cp in passed:
            cp.wait_send()

    return pl.pallas_call(
        body, name=name, in_specs=[ANY] * (2 * n), out_specs=[ANY] * n,
        out_shape=[jax.ShapeDtypeStruct(p.shape, p.dtype) for p in partials],
        input_output_aliases={g: g for g in range(n)}, scratch_shapes=_link_sems(n) + _stage_scratch(slabs),
        compiler_params=_params(),
    )(*partials, *slabs)


def _grad_sibling_swap(g_packs, *, name):
    n = len(g_packs)
    per_group = N_CHIPS * HALF_CHUNKS

    def body(*refs):
        g_refs, got_refs, (send_sems, recv_sems) = refs[:n], refs[n:2 * n], refs[2 * n:]
        x, y, c, _ = _place()
        sib = (x, y, 1 - c)
        swaps = [_remote(src, dst, send_sems, recv_sems, g * per_group + s * HALF_CHUNKS + k, sib)
                 for g, (g_ref, got_ref) in enumerate(zip(g_refs, got_refs))
                 for s in range(N_CHIPS)
                 for k, (src, dst) in enumerate(zip(_chunks(_half(g_ref.at[s], 1 - c), HALF_CHUNKS),
                                                    _chunks(got_ref.at[s], HALF_CHUNKS)))]
        for cp in swaps:
            cp.start()
        for cp in swaps:
            cp.wait_recv()
        for cp in swaps:
            cp.wait_send()

    return pl.pallas_call(
        body, name=name, in_specs=[ANY] * n, out_specs=[ANY] * n,
        out_shape=[jax.ShapeDtypeStruct((N_CHIPS, g.shape[1] // 2, g.shape[2]), g.dtype) for g in g_packs],
        scratch_shapes=[pltpu.SemaphoreType.DMA((n * per_group,)), pltpu.SemaphoreType.DMA((n * per_group,))],
    )(*g_packs)


def _grad_ici_refs(refs):
    n = (len(refs) - 3) // 3
    return refs[:n], refs[n:2 * n], refs[2 * n], refs[2 * n + 1], refs[2 * n + 2:3 * n + 2], refs[3 * n + 2]


def _grad_ici_start(*refs):
    s_refs, o_refs, send_sems, recv_sems, _, _ = _grad_ici_refs(refs)
    x, y, c, chips = _place()
    for g, (s_ref, o_ref) in enumerate(zip(s_refs, o_refs)):
        for j, (cx, cy) in enumerate(chips):
            pairs = zip(_chunks(s_ref.at[2 * cx + cy], HALF_CHUNKS), _chunks(o_ref.at[2 * x + y], HALF_CHUNKS))
            for k, (src, dst) in enumerate(pairs):
                _remote(src, dst, send_sems, recv_sems, _sem_index(g, j, k), (cx, cy, c)).start()


def _grad_ici_finish(*refs):
    s_refs, o_refs, send_sems, recv_sems, bufs, loc_sems = _grad_ici_refs(refs)
    x, y, c, chips = _place()
    me = 2 * x + y
    for s_ref, o_ref, buf in zip(s_refs, o_refs, bufs):
        _staged_copy(s_ref.at[me], o_ref.at[me], buf, loc_sems)
    for g, (s_ref, o_ref) in enumerate(zip(s_refs, o_refs)):
        for j, (cx, cy) in enumerate(chips):
            for k, landed in enumerate(_chunks(o_ref.at[2 * cx + cy], HALF_CHUNKS)):
                _remote(landed, landed, send_sems, recv_sems, _sem_index(g, j, k), (x, y, c)).wait_recv()
        for j, (cx, cy) in enumerate(chips):
            for k, sent in enumerate(_chunks(s_ref.at[2 * cx + cy], HALF_CHUNKS)):
                _remote(sent, sent, send_sems, recv_sems, _sem_index(g, j, k), (x, y, c)).wait_send()


def _grad_ici_hosted(sums):
    return _Hosted(list(sums), [jax.ShapeDtypeStruct(s.shape, s.dtype) for s in sums],
                   _link_sems(len(sums)) + _stage_scratch(sums), _grad_ici_start, _grad_ici_finish)


def _grad_ici(sums, *, name):
    def body(*refs):
        _grad_ici_start(*refs)
        _grad_ici_finish(*refs)

    n = len(sums)
    return pl.pallas_call(
        body, name=name, in_specs=[ANY] * n, out_specs=[ANY] * n,
        out_shape=[jax.ShapeDtypeStruct(s.shape, s.dtype) for s in sums],
        scratch_shapes=_link_sems(n) + _stage_scratch(sums), compiler_params=_params(),
    )(*sums)


def _grad_sibling_share(totals, *, name):
    n = len(totals)
    n_ch = HALF_CHUNKS

    def body(*refs):
        t_refs, o_refs = refs[:n], refs[n:2 * n]
        send_sems, recv_sems = refs[2 * n:2 * n + 2]
        bufs, loc_sems = refs[2 * n + 2:3 * n + 2], refs[3 * n + 2]
        x, y, c, _ = _place()
        sib = (x, y, 1 - c)
        sends = [_remote(src, dst, send_sems, recv_sems, g * n_ch + k, sib)
                 for g, (t_ref, o_ref) in enumerate(zip(t_refs, o_refs))
                 for k, (src, dst) in enumerate(zip(_chunks(t_ref, n_ch), _chunks(_half(o_ref, c), n_ch)))]
        for cp in sends:
            cp.start()
        for t_ref, o_ref, buf in zip(t_refs, o_refs, bufs):
            _staged_copy(t_ref, _half(o_ref, c), buf, loc_sems)
        for g, o_ref in enumerate(o_refs):
            for k, landed in enumerate(_chunks(_half(o_ref, 1 - c), n_ch)):
                _remote(landed, landed, send_sems, recv_sems, g * n_ch + k, sib).wait_recv()
        for cp in sends:
            cp.wait_send()

    return pl.pallas_call(
        body, name=name, in_specs=[ANY] * n, out_specs=[ANY] * n,
        out_shape=[jax.ShapeDtypeStruct((2 * t.shape[0], t.shape[1]), t.dtype) for t in totals],
        scratch_shapes=[pltpu.SemaphoreType.DMA((n * n_ch,)), pltpu.SemaphoreType.DMA((n * n_ch,))] + _stage_scratch(totals),
        compiler_params=_params(),
    )(*totals)


def _allgather8(v, *, name):
    rows, cols = v.shape

    def body(v_ref, o_ref, send_sems, recv_sems, loc_sem):
        x, y, c, chips = _place()
        sib = (x, y, 1 - c)

        def slot(px, py, pc):
            return o_ref.at[4 * px + 2 * py + pc]

        local = pltpu.make_async_copy(v_ref, slot(x, y, c), loc_sem.at[0])
        local.start()
        first = [_remote(v_ref, slot(x, y, c), send_sems, recv_sems, 0, sib)]
        first += [_remote(v_ref, slot(x, y, c), send_sems, recv_sems, 1 + j, (*chip, c)) for j, chip in enumerate(chips)]
        for cp in first:
            cp.start()
        passed = [_remote(slot(*chip, c), slot(*chip, c), send_sems, recv_sems, 4 + j, sib)
                  for j, chip in enumerate(chips)]
        for j, chip in enumerate(chips):
            _remote(v_ref, slot(*chip, c), send_sems, recv_sems, 1 + j, sib).wait_recv()
            passed[j].start()
        _remote(v_ref, slot(x, y, 1 - c), send_sems, recv_sems, 0, sib).wait_recv()
        for j, chip in enumerate(chips):
            _remote(v_ref, slot(*chip, 1 - c), send_sems, recv_sems, 4 + j, sib).wait_recv()
        for cp in first + passed:
            cp.wait_send()
        local.wait()

    return pl.pallas_call(
        body, name=name, in_specs=[ANY], out_specs=ANY, out_shape=jax.ShapeDtypeStruct((N_DEV, rows, cols), v.dtype),
        scratch_shapes=[pltpu.SemaphoreType.DMA((7,)), pltpu.SemaphoreType.DMA((7,)), pltpu.SemaphoreType.DMA((1,))],
    )(v)


_BIG = (("w_mod", (D, 6 * D), 1), ("w_in", (D, IN_W), 1), ("w_branch", (3 * D, D), None), ("w_out", (D, D), 0),
        ("w_up", (D, 2 * D_FF), 1), ("w_down", (D_FF, D), 0))
_COL_SHARDED = ("w_mod", "w_in", "w_up")
_ROW_SHARDED = (("w_branch", 3 * D // N_CHIPS), ("w_out", D // N_CHIPS), ("w_down", D_FF // N_CHIPS))


def _pack_shards(shards, layer):
    rows = jnp.concatenate([shards[n][layer].reshape(r, D) for n, r in _ROW_SHARDED], axis=0)
    return [shards[n][layer] for n in _COL_SHARDED] + [rows]


def _unpack_full(gathered):
    out = {}
    for name, blk in zip(_COL_SHARDED, gathered):
        out[name] = blk.transpose(1, 0, 2).reshape(blk.shape[1], N_CHIPS * blk.shape[2])
    off = 0
    for name, r in _ROW_SHARDED:
        blk = gathered[-1][:, off:off + r, :]
        off += r
        if name == "w_branch":
            out[name] = blk.reshape(N_CHIPS, 3, D // N_CHIPS, D).transpose(1, 0, 2, 3).reshape(3, D, D)
        else:
            out[name] = blk.reshape(N_CHIPS * r, D)
    return out


def _pack_grads(grads):
    cols = [grads[n].reshape(grads[n].shape[0], N_CHIPS, grads[n].shape[1] // N_CHIPS).transpose(1, 0, 2)
            for n in _COL_SHARDED]
    parts = []
    for name, r in _ROW_SHARDED:
        g = grads[name]
        if name == "w_branch":
            g = g.reshape(3, N_CHIPS, D // N_CHIPS, D).transpose(1, 0, 2, 3)
        parts.append(g.reshape(N_CHIPS, r, D))
    return cols + [jnp.concatenate(parts, axis=1)]


def _unpack_shards(totals, like):
    out = {n: jnp.stack([totals[l][g] for l in range(DEPTH)]) for g, n in enumerate(_COL_SHARDED)}
    off = 0
    for name, r in _ROW_SHARDED:
        out[name] = jnp.stack([totals[l][-1][off:off + r] for l in range(DEPTH)]).reshape(like[name].shape)
        off += r
    return out


def _pad_rows(v, rows):
    return jnp.concatenate([v, jnp.zeros((rows - v.shape[0],) + v.shape[1:], v.dtype)], axis=0)


def _local_step(x_tok, target, c_vec, c_ctx, wfull, small, n_lat, n_ctx):
    ctx = _step_context(c_vec, c_ctx, n_lat, n_ctx)
    saved = []
    xs = x_tok
    for l in range(DEPTH):
        xs, s, _ = _layer_fwd(l, xs, wfull[l], {k: v[l] for k, v in small.items() if k != "g_final"}, ctx)
        saved.append(s)
    dx, sq_err, d_g_final = _loss_bwd(xs, target, small["g_final"], n_lat)
    wgrads, lgrads, d_a128 = [None] * DEPTH, [None] * DEPTH, [None] * DEPTH
    for l in reversed(range(DEPTH)):
        dx, wgrads[l], lgrads[l], d_a128[l], _ = _layer_bwd(l, saved[l], wfull[l], dx, ctx)
    return sq_err, dx, wgrads, _small_grads(lgrads, d_a128, d_g_final, ctx)


def _step_context(c_vec, c_ctx, n_lat, n_ctx):
    cos_t, sin_t = _rope_tables(n_lat, n_ctx)
    a_in = _pad_rows(jnp.stack([c_vec, c_ctx]), LANES)
    a128 = _small(_silu, (LANES, D), a_in, name="cond_silu")
    return dict(cos_t=cos_t, sin_t=sin_t, a_in=a_in, a128=a128, n_lat=n_lat, n_ctx=n_ctx)


def _loss_bwd(xs, target, g_final, n_lat):
    dx, st = _loss_head(xs, target, g_final[None, :], n_lat, name="loss_head")
    return dx, st[1], st[0]


def _small_grads(lgrads, d_a128, d_g_final, ctx):
    d_cond = _small(lambda a, b, cin: (a + b) * _dsilu(cin), (LANES, D), d_a128[0], d_a128[1], ctx["a_in"],
                    name="cond_bwd")
    out = {k: jnp.stack([lgrads[l][k] for l in range(DEPTH)]) for k in lgrads[0]}
    out["c_ctx"] = d_cond[1]
    out["g_final"] = d_g_final
    return out


def _layer_fwd(l, xs, w, sm, ctx, hosted=_NO_EXCHANGE):
    n_lat, n_ctx, cos_t, sin_t, a128 = ctx["n_lat"], ctx["n_ctx"], ctx["cos_t"], ctx["sin_t"], ctx["a128"]
    mod128 = _mm(a128, w["w_mod"], name=f"mod{l}")
    mod8 = _small(lambda m, b: m + b, (SUBLANES, 6 * D), mod128[:SUBLANES], sm["b_mod"][None, :], name=f"mod_bias{l}")
    g_mix = sm["g_mix"][None, :]
    g_ffn = sm["g_ffn"][None, :]
    g_v = sm["g_v"][None, :]
    b_gate = sm["b_gate"][None, :]
    sink_b = jnp.broadcast_to(sm["sink"][:, None], (N_HEADS, LANES))
    b_sb = jnp.broadcast_to(sm["b_spatial"][:, :, None], (A_GROUPS, BLK, LANES))
    w_sconv8 = _pad_rows(sm["w_sconv"], SUBLANES)
    w_fconv8 = _pad_rows(sm["w_fconv"], SUBLANES)
    w_in = w["w_in"]
    w_seg = [w_in[:, SEG[k]:SEG[k + 1]] for k in range(4)]

    h = _norm_mod_fwd(xs, g_mix, mod8, 0, 1, n_lat, name=f"norm1_{l}")
    z_qkv, z_a, z_b, z_g = [_mm(h, w_seg[k], name=f"in_proj{k}_{l}") for k in range(4)]
    q, kd, vd = _qkv_prep(z_qkv, cos_t, sin_t, name=f"qkv_prep{l}")
    y_attn, carried = _attention_fwd(q, kd, vd, sink_b, n_lat, n_ctx, name=f"attn{l}", hosted=hosted)
    y_a = _gating_fwd(z_a, sm["w_spatial"], b_sb, g_v, name=f"gating{l}")
    y_b = _sconv_fwd(z_b, w_sconv8, n_lat, name=f"sconv{l}")
    ys = (y_attn, y_a, y_b)
    ts = [_mm(ys[k], w["w_branch"][k], name=f"branch{k}_{l}") for k in range(3)]
    merged = _merge_fwd(*ts, z_g, b_gate, name=f"merge{l}")
    mix_out = _mm(merged, w["w_out"], name=f"out_proj{l}")
    x1 = _residual_fwd(xs, mix_out, mod8, 2, n_lat, name=f"res1_{l}")
    h2 = _norm_mod_fwd(x1, g_ffn, mod8, 3, 4, n_lat, name=f"norm2_{l}")
    up = _mm(h2, w["w_up"], name=f"up_proj{l}")
    cv, f = _ffn_mid_fwd(up, w_fconv8, n_lat, name=f"ffn_mid{l}")
    ffn_out = _mm(f, w["w_down"], name=f"down_proj{l}")
    x2 = _residual_fwd(x1, ffn_out, mod8, 5, n_lat, name=f"res2_{l}")
    saved = dict(x0=xs, mod8=mod8, h=h, z_qkv=z_qkv, z_a=z_a, z_b=z_b, z_g=z_g, q=q, kd=kd, vd=vd, ys=ys, ts=ts,
                 merged=merged, mix_out=mix_out, x1=x1, h2=h2, up=up, cv=cv, f=f, ffn_out=ffn_out, w_seg=w_seg,
                 g_mix=g_mix, g_ffn=g_ffn, g_v=g_v, b_gate=b_gate, sink_b=sink_b, b_sb=b_sb,
                 w_sconv8=w_sconv8, w_fconv8=w_fconv8, w_spatial=sm["w_spatial"])
    return x2, saved, carried


def _layer_bwd(l, s, w, dx, ctx, hosted=_NO_EXCHANGE):
    n_lat, n_ctx, cos_t, sin_t, a128 = ctx["n_lat"], ctx["n_ctx"], ctx["cos_t"], ctx["sin_t"], ctx["a128"]
    mod8 = s["mod8"]
    d_ffn, st_gt2 = _residual_bwd(dx, s["ffn_out"], mod8, 5, n_lat, name=f"res2_bwd{l}")
    df = _mm(d_ffn, w["w_down"], tb=True, name=f"down_bwd_x{l}")
    g_down = _mm(s["f"], d_ffn, ta=True, out_dtype=BF16, name=f"down_bwd_w{l}")
    d_up, st_fc = _ffn_mid_bwd(s["up"], s["cv"], df, s["w_fconv8"], n_lat, name=f"ffn_mid_bwd{l}")
    dh2 = _mm(d_up, w["w_up"], tb=True, name=f"up_bwd_x{l}")
    g_up = _mm(s["h2"], d_up, ta=True, out_dtype=BF16, name=f"up_bwd_w{l}")
    dx1, st_n2 = _norm_mod_bwd(s["x1"], [dh2], dx, s["g_ffn"], mod8, 4, n_lat, name=f"norm2_bwd{l}")
    d_out, st_gt1 = _residual_bwd(dx1, s["mix_out"], mod8, 2, n_lat, name=f"res1_bwd{l}")
    d_merged = _mm(d_out, w["w_out"], tb=True, name=f"out_bwd_x{l}")
    g_out = _mm(s["merged"], d_out, ta=True, out_dtype=BF16, name=f"out_bwd_w{l}")
    dt0, dt1, dt2, dz_g, st_bg = _merge_bwd(d_merged, *s["ts"], s["z_g"], s["b_gate"], name=f"merge_bwd{l}")
    dts = (dt0, dt1, dt2)
    dys = [_mm(dts[k], w["w_branch"][k], tb=True, name=f"branch{k}_bwd_x{l}") for k in range(3)]
    g_branch = jnp.stack([_mm(s["ys"][k], dts[k], ta=True, out_dtype=BF16, name=f"branch{k}_bwd_w{l}")
                          for k in range(3)])
    dq, dk, dv, d_sink, carried = _attention_bwd(s["q"], s["kd"], s["vd"], s["sink_b"], dys[0], n_lat, n_ctx,
                                                 name=f"attn_bwd{l}", hosted=hosted)
    dz_qkv = _qkv_unprep(dq, dk, dv, cos_t, sin_t, name=f"qkv_unprep{l}")
    dz_a, d_ws, d_bs, st_gv = _gating_bwd(s["z_a"], dys[1], s["w_spatial"], s["b_sb"], s["g_v"], name=f"gating_bwd{l}")
    dz_b, st_sc = _sconv_bwd(s["z_b"], dys[2], s["w_sconv8"], n_lat, name=f"sconv_bwd{l}")
    dzs = (dz_qkv, dz_a, dz_b, dz_g)
    dh_parts = [_mm(dzs[k], s["w_seg"][k], tb=True, name=f"in_bwd_x{k}_{l}") for k in range(4)]
    g_in = jnp.concatenate([_mm(s["h"], dzs[k], ta=True, out_dtype=BF16, name=f"in_bwd_w{k}_{l}")
                            for k in range(4)], axis=1)
    dx0, st_n1 = _norm_mod_bwd(s["x0"], dh_parts, dx1, s["g_mix"], mod8, 1, n_lat, name=f"norm1_bwd{l}")
    dmod = jnp.concatenate([st_n1[0:2], st_n1[2:4], st_gt1[0:2], st_n2[0:2], st_n2[2:4], st_gt2[0:2]], axis=1)
    dmod128 = _pad_rows(dmod, LANES)
    g_mod = _mm(a128, dmod128, ta=True, out_dtype=BF16, name=f"mod_bwd_w{l}")
    d_a128 = _mm(dmod128, w["w_mod"], tb=True, name=f"mod_bwd_x{l}")
    wgrads = dict(w_mod=g_mod, w_in=g_in, w_branch=g_branch.reshape(3 * D, D), w_out=g_out, w_up=g_up, w_down=g_down)
    lgrads = dict(b_mod=dmod[0] + dmod[1], g_mix=st_n1[4], g_ffn=st_n2[4], b_gate=st_bg[0], sink=d_sink[:, 0],
                  w_spatial=d_ws, b_spatial=d_bs[:, :, 0], g_v=st_gv[0], w_sconv=st_sc[0:3], w_fconv=st_fc[0:3])
    return dx0, wgrads, lgrads, d_a128, carried


_SMALL_ORDER = ("c_ctx", "b_mod", "g_mix", "b_gate", "sink", "w_spatial", "b_spatial", "g_v", "w_sconv", "g_ffn",
                "w_fconv", "g_final")


def _flat_pack(parts, width):
    flat = jnp.concatenate([p.reshape(-1).astype(F32) for p in parts])
    rows = -(-flat.shape[0] // (width * SUBLANES)) * SUBLANES
    flat = jnp.concatenate([flat, jnp.zeros((rows * width - flat.shape[0],), F32)])
    return flat.reshape(rows, width)


def _flat_unpack(packed, likes):
    flat = packed.reshape(-1)
    out, off = [], 0
    for like in likes:
        n = math.prod(like.shape)
        out.append(flat[off:off + n].reshape(like.shape))
        off += n
    return out


def kernel(x, c, ctx, c_ctx, w_mod, b_mod, g_mix, w_in, b_gate, sink, w_spatial, b_spatial, g_v, w_sconv, w_branch, w_out, g_ffn, w_up, w_fconv, w_down, g_final, loss_target, m_c_ctx, m_w_mod, m_b_mod, m_g_mix, m_w_in, m_b_gate, m_sink, m_w_spatial, m_b_spatial, m_g_v, m_w_sconv, m_w_branch, m_w_out, m_g_ffn, m_w_up, m_w_fconv, m_w_down, m_g_final, v_c_ctx, v_w_mod, v_b_mod, v_g_mix, v_w_in, v_b_gate, v_sink, v_w_spatial, v_b_spatial, v_g_v, v_w_sconv, v_w_branch, v_w_out, v_g_ffn, v_w_up, v_w_fconv, v_w_down, v_g_final):
    n_lat, n_ctx = x.shape[1], ctx.shape[1]
    chip = 2 * lax.axis_index("x") + lax.axis_index("y")
    weights = dict(c_ctx=c_ctx, w_mod=w_mod, b_mod=b_mod, g_mix=g_mix, w_in=w_in, b_gate=b_gate, sink=sink,
                   w_spatial=w_spatial, b_spatial=b_spatial, g_v=g_v, w_sconv=w_sconv, w_branch=w_branch, w_out=w_out,
                   g_ffn=g_ffn, w_up=w_up, w_fconv=w_fconv, w_down=w_down, g_final=g_final)
    m_in = dict(c_ctx=m_c_ctx, w_mod=m_w_mod, b_mod=m_b_mod, g_mix=m_g_mix, w_in=m_w_in, b_gate=m_b_gate, sink=m_sink,
                w_spatial=m_w_spatial, b_spatial=m_b_spatial, g_v=m_g_v, w_sconv=m_w_sconv, w_branch=m_w_branch,
                w_out=m_w_out, g_ffn=m_g_ffn, w_up=m_w_up, w_fconv=m_w_fconv, w_down=m_w_down, g_final=m_g_final)
    v_in = dict(c_ctx=v_c_ctx, w_mod=v_w_mod, b_mod=v_b_mod, g_mix=v_g_mix, w_in=v_w_in, b_gate=v_b_gate, sink=v_sink,
                w_spatial=v_w_spatial, b_spatial=v_b_spatial, g_v=v_g_v, w_sconv=v_w_sconv, w_branch=v_w_branch,
                w_out=v_w_out, g_ffn=v_g_ffn, w_up=v_w_up, w_fconv=v_w_fconv, w_down=v_w_down, g_final=v_g_final)
    big_names = [n for n, _, _ in _BIG]

    conv_pack = _flat_pack([w_sconv, w_fconv], LANES)
    conv_all = _allgather8(conv_pack, name="gather_conv_weights")
    conv_parts = [_flat_unpack(conv_all[2 * p], [w_sconv, w_fconv]) for p in range(N_CHIPS)]
    w_sconv_full = jnp.concatenate([cp[0] for cp in conv_parts], axis=-1)
    w_fconv_full = jnp.concatenate([cp[1] for cp in conv_parts], axis=-1)

    small = dict(b_mod=b_mod, g_mix=g_mix, b_gate=b_gate, sink=sink, w_spatial=w_spatial, b_spatial=b_spatial, g_v=g_v,
                 w_sconv=w_sconv_full, g_ffn=g_ffn, w_fconv=w_fconv_full, g_final=g_final)
    x_tok = jnp.concatenate([x[0], ctx[0]], axis=0)
    step = _step_context(c[0], c_ctx, n_lat, n_ctx)
    layer_small = [{k: v[l] for k, v in small.items() if k != "g_final"} for l in range(DEPTH)]
    my_half = lax.axis_index("c").astype(jnp.int32).reshape(1)

    shards = {n: weights[n].astype(BF16) for n in big_names}
    pack = [_pack_shards(shards, l) for l in range(DEPTH)]
    w0 = _unpack_full(_gather_finish(_gather_ici(pack[0], name="gather_ici0"), pack[0], name="gather_finish0"))
    xs, saved0, partial1 = _layer_fwd(0, x_tok, w0, layer_small[0], step, hosted=_gather_ici_hosted(pack[1]))
    w1 = _unpack_full(_gather_finish(partial1, pack[1], name="gather_finish1"))
    xs, saved1, _ = _layer_fwd(1, xs, w1, layer_small[1], step)
    dx, sq_err, d_g_final = _loss_bwd(xs, loss_target[0], g_final, n_lat)
    loss = lax.psum(0.5 * jnp.sum(sq_err) / D, ("x", "y", "c"))

    def reduce_start(wgrads_l, l):
        g_packs = _pack_grads(wgrads_l)
        got = _grad_sibling_swap(g_packs, name=f"grad_sibling_swap{l}")
        return [_add_half(my_half, a, b, name=f"grad_pair_sum{l}_{g}") for g, (a, b) in enumerate(zip(g_packs, got))]

    def reduce_finish(exchanged, l):
        sums = [_sum_slabs(e, F32, name=f"grad_chip_sum{l}_{g}") for g, e in enumerate(exchanged)]
        return _grad_sibling_share(sums, name=f"grad_sibling_share{l}")

    dx, wgrads1, lgrads1, d_a1, _ = _layer_bwd(1, saved1, w1, dx, step)
    pair_sum1 = reduce_start(wgrads1, 1)
    dx, wgrads0, lgrads0, d_a0, exchanged1 = _layer_bwd(0, saved0, w0, dx, step, hosted=_grad_ici_hosted(pair_sum1))
    total1 = reduce_finish(exchanged1, 1)
    total0 = reduce_finish(_grad_ici(reduce_start(wgrads0, 0), name="grad_chip_exchange0"), 0)
    big_grads = _unpack_shards([total0, total1], {n: weights[n] for n in big_names})
    sgrads = _small_grads([lgrads0, lgrads1], [d_a0, d_a1], d_g_final, step)
    grad_x = dx[:n_lat][None]

    s_likes = [sgrads[n] for n in _SMALL_ORDER]
    s_all = _allgather8(_flat_pack(s_likes, D), name="gather_small_grads")
    s_tot = _flat_unpack(_sum_slabs(s_all, F32, name="small_grad_sum"), s_likes)
    grads = dict(big_grads)
    for n, g in zip(_SMALL_ORDER, s_tot):
        grads[n] = g
    grads["w_sconv"] = lax.dynamic_slice_in_dim(grads["w_sconv"], chip * w_sconv.shape[-1], w_sconv.shape[-1], axis=2)
    grads["w_fconv"] = lax.dynamic_slice_in_dim(grads["w_fconv"], chip * w_fconv.shape[-1], w_fconv.shape[-1], axis=2)

    delta, new_m, new_v = {}, {}, {}
    for n in big_names:
        cols = weights[n].shape[-1]
        view = lambda a: a.reshape(-1, cols)
        d_, m_, v_ = _adamw(view(weights[n]), view(grads[n]), view(m_in[n]), view(v_in[n]), name=f"adamw_{n}")
        delta[n], new_m[n], new_v[n] = (t.reshape(weights[n].shape) for t in (d_, m_, v_))
    likes = [weights[n] for n in _SMALL_ORDER]
    packs = [_flat_pack([src[n] for n in _SMALL_ORDER], LANES) for src in (weights, grads, m_in, v_in)]
    outs = _adamw(*packs, name="adamw_small")
    for dst, packed in zip((delta, new_m, new_v), outs):
        for n, val in zip(_SMALL_ORDER, _flat_unpack(packed, likes)):
            dst[n] = val

    order = ("c_ctx", "w_mod", "b_mod", "g_mix", "w_in", "b_gate", "sink", "w_spatial", "b_spatial", "g_v", "w_sconv",
             "w_branch", "w_out", "g_ffn", "w_up", "w_fconv", "w_down", "g_final")
    return (loss, grad_x, *[grads[n] for n in order], *[delta[n] for n in order], *[new_m[n] for n in order],
            *[new_v[n] for n in order])
```

```python
import functools
import math

import jax
import jax.numpy as jnp
from jax import lax
from jax.experimental import pallas as pl
from jax.experimental.pallas import tpu as pltpu

F32 = jnp.float32
BF16 = jnp.bfloat16

D = 1024
DEPTH = 2
GRID_W = 64
N_HEADS = 16
N_KV = 4
GRP = N_HEADS // N_KV
HEAD_DIM = 64
KV_W = N_KV * HEAD_DIM
WINDOW = 128
BLK = 128
ROPE_THETA = 10000.0
A_GROUPS = 8
D_FF = 2816
EPS = 1e-6
NEG = -1e30
QKV_W = D + 2 * KV_W
A_COLS = 2 * D
B_COLS = 3 * D
G_COLS = 3 * D
IN_W = QKV_W + A_COLS + B_COLS + G_COLS
SEG = (0, QKV_W, QKV_W + A_COLS, QKV_W + A_COLS + B_COLS, IN_W)
N_CHIPS = 4
N_DEV = 8
LANES = 128
SUBLANES = 8
VMEM_LIMIT = 48 * 1024 * 1024
ADAM_LR = 0.001
ADAM_B1 = 0.9
ADAM_B2 = 0.999
ADAM_EPS = 1e-08
ADAM_WD = 0.01
ADAM_STEP = 10
MESH = pl.DeviceIdType.MESH
ANY = pl.BlockSpec(memory_space=pl.ANY)


def _params(sem=None):
    return pltpu.CompilerParams(dimension_semantics=sem, vmem_limit_bytes=VMEM_LIMIT)


def _pick(n, cands):
    for c in cands:
        if n % c == 0:
            return c
    return n


def _rows8(rows, width):
    r = lax.broadcasted_iota(jnp.int32, (SUBLANES, width), 0)
    out = jnp.zeros((SUBLANES, width), F32)
    for idx, v in rows:
        out = out + jnp.where(r == idx, v, 0.0)
    return out


def _sel(mod_ref, k, is_ctx):
    return jnp.where(is_ctx, mod_ref[1:2, k * D:(k + 1) * D], mod_ref[0:1, k * D:(k + 1) * D])


def _colsum(v):
    return jnp.sum(v, axis=0, keepdims=True)


def _mm(a, b, *, name, ta=False, tb=False, out_dtype=F32):
    if ta:
        k_dim, m = a.shape
    else:
        m, k_dim = a.shape
    if tb:
        n, kb = b.shape
    else:
        kb, n = b.shape
    assert k_dim == kb, (a.shape, b.shape, ta, tb)
    tm = _pick(m, (1056, 1024, 1408, 768, 512, 256, 128))
    tn = _pick(n, (1536, 1408, 1024, 768, 512, 256, 128))
    tk = _pick(k_dim, (2048, 1536, 1408, 1024, 768, 512, 256, 128))
    nk = k_dim // tk
    dims = (((0 if ta else 1,), (1 if tb else 0,)), ((), ()))

    def product(a_ref, b_ref):
        return lax.dot_general(a_ref[...].astype(BF16), b_ref[...].astype(BF16), dims, preferred_element_type=F32)

    def body_single(a_ref, b_ref, o_ref):
        o_ref[...] = product(a_ref, b_ref).astype(o_ref.dtype)

    def body_acc(a_ref, b_ref, o_ref, acc_ref):
        k = pl.program_id(2)

        @pl.when(k == 0)
        def _():
            acc_ref[...] = product(a_ref, b_ref)

        @pl.when(k > 0)
        def _():
            acc_ref[...] += product(a_ref, b_ref)

        @pl.when(k == nk - 1)
        def _():
            o_ref[...] = acc_ref[...].astype(o_ref.dtype)

    a_spec = pl.BlockSpec((tk, tm), lambda i, j, k: (k, i)) if ta else pl.BlockSpec((tm, tk), lambda i, j, k: (i, k))
    b_spec = pl.BlockSpec((tn, tk), lambda i, j, k: (j, k)) if tb else pl.BlockSpec((tk, tn), lambda i, j, k: (k, j))
    return pl.pallas_call(
        body_single if nk == 1 else body_acc, name=name, grid=(m // tm, n // tn, nk),
        in_specs=[a_spec, b_spec], out_specs=pl.BlockSpec((tm, tn), lambda i, j, k: (i, j)),
        out_shape=jax.ShapeDtypeStruct((m, n), out_dtype),
        scratch_shapes=[] if nk == 1 else [pltpu.VMEM((tm, tn), F32)],
        compiler_params=_params(("parallel", "parallel", "arbitrary")),
    )(a, b)


def _small(fn, out_shape, *arrays, name):
    def body(*refs):
        refs[-1][...] = fn(*[r[...] for r in refs[:-1]]).astype(refs[-1].dtype)

    return pl.pallas_call(body, name=name, out_shape=jax.ShapeDtypeStruct(out_shape, F32))(*arrays)


def _silu(v):
    return v * jax.nn.sigmoid(v)


def _dsilu(v):
    s = jax.nn.sigmoid(v)
    return s * (1.0 + v * (1.0 - s))


def _row_spec(tm, width, col=0):
    return pl.BlockSpec((tm, width), lambda i: (i, col))


def _full_spec(shape):
    nd = len(shape)
    return pl.BlockSpec(shape, lambda i: (0,) * nd)


def _halo_specs(tm, width, t_rows, col=0):
    per = tm // SUBLANES
    last = t_rows // SUBLANES - 1
    prev = pl.BlockSpec((SUBLANES, width), lambda i: (jnp.maximum(i * per - 1, 0), col))
    nxt = pl.BlockSpec((SUBLANES, width), lambda i: (jnp.minimum((i + 1) * per, last), col))
    return prev, nxt


def _shift_rows(cur, prev8, next8, n_lat, t_rows, tm):
    i = pl.program_id(0)
    row = lax.broadcasted_iota(jnp.int32, (tm, 1), 0)
    g = row + i * tm
    up = pltpu.roll(cur, 1, 0)
    up = jnp.where(row == 0, prev8[SUBLANES - 1:SUBLANES, :], up)
    up = jnp.where((g == 0) | (g == n_lat), 0.0, up)
    dn = pltpu.roll(cur, tm - 1, 0)
    dn = jnp.where(row == tm - 1, next8[0:1, :], dn)
    dn = jnp.where((g == n_lat - 1) | (g == t_rows - 1), 0.0, dn)
    return up, dn


def _norm_mod_fwd(x, g, mod8, sh_idx, sc_idx, n_lat, *, name):
    t_rows = x.shape[0]
    tm = 256

    def body(x_ref, g_ref, mod_ref, o_ref):
        is_ctx = pl.program_id(0) * tm >= n_lat
        xv = x_ref[...]
        rstd = lax.rsqrt(jnp.mean(xv * xv, axis=-1, keepdims=True) + EPS)
        y = xv * rstd * g_ref[...]
        o_ref[...] = (y * (1.0 + _sel(mod_ref, sc_idx, is_ctx)) + _sel(mod_ref, sh_idx, is_ctx)).astype(BF16)

    return pl.pallas_call(
        body, name=name, grid=(t_rows // tm,),
        in_specs=[_row_spec(tm, D), _full_spec((1, D)), _full_spec((SUBLANES, 6 * D))],
        out_specs=_row_spec(tm, D), out_shape=jax.ShapeDtypeStruct((t_rows, D), BF16),
        compiler_params=_params(("parallel",)),
    )(x, g, mod8)


def _norm_mod_bwd(x, dh_parts, dres, g, mod8, sc_idx, n_lat, *, name):
    t_rows = x.shape[0]
    tm = 256
    n_parts = len(dh_parts)

    def body(*refs):
        x_ref, dres_ref, g_ref, mod_ref = refs[:4]
        part_refs = refs[4:4 + n_parts]
        dx_ref, st_ref = refs[4 + n_parts:]
        i = pl.program_id(0)
        is_ctx = i * tm >= n_lat
        dh = part_refs[0][...]
        for p in part_refs[1:]:
            dh = dh + p[...]
        xv = x_ref[...]
        gv = g_ref[...]
        rstd = lax.rsqrt(jnp.mean(xv * xv, axis=-1, keepdims=True) + EPS)
        rn = xv * rstd
        dy = dh * (1.0 + _sel(mod_ref, sc_idx, is_ctx))
        e = dy * gv
        dx_ref[...] = dres_ref[...] + rstd * (e - rn * jnp.mean(e * rn, axis=-1, keepdims=True))
        dsh = _colsum(dh)
        dsc = _colsum(dh * (rn * gv))
        dg = _colsum(dy * rn)
        zero = jnp.zeros_like(dsh)
        upd = _rows8([(0, jnp.where(is_ctx, zero, dsh)), (1, jnp.where(is_ctx, dsh, zero)),
                      (2, jnp.where(is_ctx, zero, dsc)), (3, jnp.where(is_ctx, dsc, zero)), (4, dg)], D)

        @pl.when(i == 0)
        def _():
            st_ref[...] = upd

        @pl.when(i > 0)
        def _():
            st_ref[...] += upd

    return pl.pallas_call(
        body, name=name, grid=(t_rows // tm,),
        in_specs=[_row_spec(tm, D), _row_spec(tm, D), _full_spec((1, D)), _full_spec((SUBLANES, 6 * D))]
        + [_row_spec(tm, D)] * n_parts,
        out_specs=[_row_spec(tm, D), _full_spec((SUBLANES, D))],
        out_shape=[jax.ShapeDtypeStruct((t_rows, D), F32), jax.ShapeDtypeStruct((SUBLANES, D), F32)],
        compiler_params=_params(("arbitrary",)),
    )(x, dres, g, mod8, *dh_parts)


def _residual_fwd(x, branch, mod8, gt_idx, n_lat, *, name):
    t_rows = x.shape[0]
    tm = 256

    def body(x_ref, b_ref, mod_ref, o_ref):
        is_ctx = pl.program_id(0) * tm >= n_lat
        o_ref[...] = x_ref[...] + _sel(mod_ref, gt_idx, is_ctx) * b_ref[...]

    return pl.pallas_call(
        body, name=name, grid=(t_rows // tm,),
        in_specs=[_row_spec(tm, D), _row_spec(tm, D), _full_spec((SUBLANES, 6 * D))],
        out_specs=_row_spec(tm, D), out_shape=jax.ShapeDtypeStruct((t_rows, D), F32),
        compiler_params=_params(("parallel",)),
    )(x, branch, mod8)


def _residual_bwd(dx, branch, mod8, gt_idx, n_lat, *, name):
    t_rows = dx.shape[0]
    tm = 256

    def body(dx_ref, b_ref, mod_ref, o_ref, st_ref):
        i = pl.program_id(0)
        is_ctx = i * tm >= n_lat
        dxv = dx_ref[...]
        o_ref[...] = (dxv * _sel(mod_ref, gt_idx, is_ctx)).astype(BF16)
        dgt = _colsum(dxv * b_ref[...])
        zero = jnp.zeros_like(dgt)
        upd = _rows8([(0, jnp.where(is_ctx, zero, dgt)), (1, jnp.where(is_ctx, dgt, zero))], D)

        @pl.when(i == 0)
        def _():
            st_ref[...] = upd

        @pl.when(i > 0)
        def _():
            st_ref[...] += upd

    return pl.pallas_call(
        body, name=name, grid=(t_rows // tm,),
        in_specs=[_row_spec(tm, D), _row_spec(tm, D), _full_spec((SUBLANES, 6 * D))],
        out_specs=[_row_spec(tm, D), _full_spec((SUBLANES, D))],
        out_shape=[jax.ShapeDtypeStruct((t_rows, D), BF16), jax.ShapeDtypeStruct((SUBLANES, D), F32)],
        compiler_params=_params(("arbitrary",)),
    )(dx, branch, mod8)


def _rope_tables(n_lat, n_ctx):
    rows = n_lat // GRID_W
    row = jnp.broadcast_to(jnp.arange(rows, dtype=F32)[:, None], (rows, GRID_W)).reshape(n_lat)
    col = jnp.broadcast_to(jnp.arange(GRID_W, dtype=F32)[None, :], (rows, GRID_W)).reshape(n_lat)
    half = HEAD_DIM // 2
    inv = ROPE_THETA ** (-jnp.arange(0, half, 2, dtype=F32) / half)
    ang = jnp.concatenate([row[:, None] * inv, col[:, None] * inv], axis=-1)
    cos, sin = jnp.cos(ang), jnp.sin(ang)
    c64 = jnp.concatenate([cos, cos], axis=-1)
    s64 = jnp.concatenate([-sin, sin], axis=-1)
    c64 = jnp.concatenate([c64, jnp.ones((n_ctx, HEAD_DIM), F32)], axis=0)
    s64 = jnp.concatenate([s64, jnp.zeros((n_ctx, HEAD_DIM), F32)], axis=0)
    return jnp.tile(c64, (1, 2)), jnp.tile(s64, (1, 2))


def _swap_halves(v):
    lane = lax.broadcasted_iota(jnp.int32, v.shape, 1)
    return jnp.where(lane % HEAD_DIM < HEAD_DIM // 2, pltpu.roll(v, LANES - HEAD_DIM // 2, 1),
                     pltpu.roll(v, HEAD_DIM // 2, 1))


def _low_half(shape):
    return lax.broadcasted_iota(jnp.int32, shape, 1) < HEAD_DIM


def _qkv_prep(z_qkv, cos_t, sin_t, *, name):
    t_rows = z_qkv.shape[0]
    tm = 256

    def body(z_ref, c_ref, s_ref, q_ref, k_ref, v_ref):
        cv, sv = c_ref[...], s_ref[...]

        def rope(chunk):
            return chunk * cv + _swap_halves(chunk) * sv

        for ch in range(D // LANES):
            roped = rope(z_ref[:, ch * LANES:(ch + 1) * LANES])
            q_ref[:, ch * LANES:(ch + 1) * LANES] = (roped * (HEAD_DIM ** -0.5)).astype(BF16)
        low = _low_half((tm, LANES))
        for pair in range(N_KV // 2):
            for which, ref, roped in ((0, k_ref, True), (1, v_ref, False)):
                off = D + which * KV_W + pair * LANES
                chunk = z_ref[:, off:off + LANES]
                if roped:
                    chunk = rope(chunk)
                other = pltpu.roll(chunk, HEAD_DIM, 1)
                even = jnp.where(low, chunk, other)
                odd = jnp.where(low, other, chunk)
                ref[:, (2 * pair) * LANES:(2 * pair + 1) * LANES] = even.astype(BF16)
                ref[:, (2 * pair + 1) * LANES:(2 * pair + 2) * LANES] = odd.astype(BF16)

    dup_w = N_KV * LANES
    return pl.pallas_call(
        body, name=name, grid=(t_rows // tm,),
        in_specs=[_row_spec(tm, QKV_W), _row_spec(tm, LANES), _row_spec(tm, LANES)],
        out_specs=[_row_spec(tm, D), _row_spec(tm, dup_w), _row_spec(tm, dup_w)],
        out_shape=[jax.ShapeDtypeStruct((t_rows, D), BF16), jax.ShapeDtypeStruct((t_rows, dup_w), BF16),
                   jax.ShapeDtypeStruct((t_rows, dup_w), BF16)],
        compiler_params=_params(("parallel",)),
    )(z_qkv, cos_t, sin_t)


def _qkv_unprep(dq, dk, dv, cos_t, sin_t, *, name):
    t_rows = dq.shape[0]
    tm = 256

    def body(dq_ref, dk_ref, dv_ref, c_ref, s_ref, o_ref):
        cv, sv = c_ref[...], s_ref[...]

        def unrope(chunk):
            return chunk * cv + _swap_halves(chunk * sv)

        for ch in range(D // LANES):
            o_ref[:, ch * LANES:(ch + 1) * LANES] = unrope(dq_ref[:, ch * LANES:(ch + 1) * LANES]).astype(BF16)
        for pair in range(N_KV // 2):
            for which, ref, roped in ((0, dk_ref, True), (1, dv_ref, False)):
                chunk = ref[:, pair * LANES:(pair + 1) * LANES]
                if roped:
                    chunk = unrope(chunk)
                off = D + which * KV_W + pair * LANES
                o_ref[:, off:off + LANES] = chunk.astype(BF16)

    return pl.pallas_call(
        body, name=name, grid=(t_rows // tm,),
        in_specs=[_row_spec(tm, D), _row_spec(tm, KV_W), _row_spec(tm, KV_W), _row_spec(tm, LANES),
                  _row_spec(tm, LANES)],
        out_specs=_row_spec(tm, QKV_W), out_shape=jax.ShapeDtypeStruct((t_rows, QKV_W), BF16),
        compiler_params=_params(("parallel",)),
    )(dq, dk, dv, cos_t, sin_t)


def _attn_specs(n_lat, n_ctx):
    nb = n_lat // BLK
    dup_w = N_KV * LANES

    def ws(j):
        return jnp.clip(j - 1, 0, nb - 3)

    win = [pl.BlockSpec((BLK, dup_w), functools.partial(lambda j, o: (ws(j) + o, 0), o=o)) for o in range(3)]
    ctx = pl.BlockSpec((n_ctx, dup_w), lambda j: (n_lat // n_ctx, 0))
    return nb, ws, win, ctx


def _attn_bias(j, ws_j, nb, n_ctx):
    n_keys = 3 * BLK + n_ctx
    row = lax.broadcasted_iota(jnp.int32, (BLK, n_keys), 0)
    col = lax.broadcasted_iota(jnp.int32, (BLK, n_keys), 1)
    rel = (ws_j - j) * BLK + col - row
    valid = (col >= 3 * BLK) | ((jnp.abs(rel) <= WINDOW) & (j < nb))
    bias = jnp.where(valid, 0.0, NEG)
    return jnp.concatenate([bias] * GRP, axis=0)


def _attn_probs(q_ref, kk, kh, bias, sink_ref):
    low = _low_half((BLK, LANES))
    qs = []
    for g in range(GRP):
        h = GRP * kh + g
        chunk = q_ref[:, (h // 2) * LANES:(h // 2 + 1) * LANES]
        qs.append(jnp.where(low if h % 2 == 0 else ~low, chunk, jnp.zeros_like(chunk)))
    qs = jnp.concatenate(qs, axis=0)
    s = lax.dot_general(qs, kk, (((1,), (1,)), ((), ())), preferred_element_type=F32) + bias
    snk = jnp.concatenate(
        [jnp.broadcast_to(jnp.max(sink_ref[GRP * kh + g:GRP * kh + g + 1, :], axis=1, keepdims=True), (BLK, 1))
         for g in range(GRP)], axis=0)
    m = jnp.maximum(jnp.max(s, axis=-1, keepdims=True), snk)
    p = jnp.exp(s - m)
    p_snk = jnp.exp(snk - m)
    inv = 1.0 / (jnp.sum(p, axis=-1, keepdims=True) + p_snk)
    return qs, p, p_snk, inv


class _Hosted:
    def __init__(self, arrays, out_shapes, scratch, start, finish):
        self.arrays, self.out_shapes, self.scratch, self.start, self.finish = arrays, out_shapes, scratch, start, finish


_NO_EXCHANGE = _Hosted([], [], [], None, None)


def _split_refs(refs, n_in, n_out, n_scratch, hosted):
    hi, ho, hs = len(hosted.arrays), len(hosted.out_shapes), len(hosted.scratch)
    a = n_in + hi
    b = a + n_out + ho
    ins, h_ins = refs[:n_in], refs[n_in:a]
    outs, h_outs = refs[a:a + n_out], refs[a + n_out:b]
    scr, h_scr = refs[b:b + n_scratch], refs[b + n_scratch:b + n_scratch + hs]
    return ins, outs, scr, (h_ins, h_outs, h_scr)


def _run_hosted(hosted, h_refs, step, n_steps):
    if hosted.start is None:
        return

    flat = [r for group in h_refs for r in group]

    @pl.when(step == 0)
    def _():
        hosted.start(*flat)

    @pl.when(step == n_steps - 1)
    def _():
        hosted.finish(*flat)


def _attention_fwd(q, kd, vd, sink_b, n_lat, n_ctx, *, name, hosted=_NO_EXCHANGE):
    t_rows = q.shape[0]
    nb, ws, win, ctx = _attn_specs(n_lat, n_ctx)
    n_steps = t_rows // BLK

    def body(*refs):
        ins, outs, _, h_refs = _split_refs(refs, 10, 1, 0, hosted)
        q_ref, k0, k1, k2, kc, v0, v1, v2, vc, sink_ref = ins
        o_ref, = outs
        j = pl.program_id(0)
        _run_hosted(hosted, h_refs, j, n_steps)
        ws_j = ws(j)
        low = _low_half((BLK, LANES))
        bias = _attn_bias(j, ws_j, nb, n_ctx)
        for kh in range(N_KV):
            sl = slice(kh * LANES, (kh + 1) * LANES)
            kk = jnp.concatenate([k0[:, sl], k1[:, sl], k2[:, sl], kc[:, sl]], axis=0)
            vv = jnp.concatenate([v0[:, sl], v1[:, sl], v2[:, sl], vc[:, sl]], axis=0)
            _, p, _, inv = _attn_probs(q_ref, kk, kh, bias, sink_ref)
            o = jnp.dot(p.astype(BF16), vv, preferred_element_type=F32) * inv
            for half in range(2):
                even = o[(2 * half) * BLK:(2 * half + 1) * BLK]
                odd = o[(2 * half + 1) * BLK:(2 * half + 2) * BLK]
                ch = 2 * kh + half
                o_ref[:, ch * LANES:(ch + 1) * LANES] = jnp.where(low, even, odd).astype(BF16)

    outs = pl.pallas_call(
        body, name=name, grid=(n_steps,),
        in_specs=[_row_spec(BLK, D)] + win + [ctx] + win + [ctx] + [_full_spec((N_HEADS, LANES))]
        + [ANY] * len(hosted.arrays),
        out_specs=[_row_spec(BLK, D)] + [ANY] * len(hosted.out_shapes),
        out_shape=[jax.ShapeDtypeStruct((t_rows, D), BF16)] + list(hosted.out_shapes),
        scratch_shapes=list(hosted.scratch),
        compiler_params=_params(("arbitrary",)),
    )(q, kd, kd, kd, kd, vd, vd, vd, vd, sink_b, *hosted.arrays)
    return outs[0], outs[1:]


def _attention_bwd(q, kd, vd, sink_b, dy, n_lat, n_ctx, *, name, hosted=_NO_EXCHANGE):
    t_rows = q.shape[0]
    nb, ws, win, ctx = _attn_specs(n_lat, n_ctx)
    n_steps = t_rows // BLK

    def body(*refs):
        ins, outs, scr, h_refs = _split_refs(refs, 11, 4, 3, hosted)
        q_ref, k0, k1, k2, kc, v0, v1, v2, vc, sink_ref, dy_ref = ins
        dq_ref, dk_hbm, dv_hbm, ds_ref = outs
        dk_acc, dv_acc, sem = scr
        j = pl.program_id(0)
        _run_hosted(hosted, h_refs, j, n_steps)
        ws_j = ws(j)

        @pl.when(j == 0)
        def _():
            dk_acc[...] = jnp.zeros_like(dk_acc)
            dv_acc[...] = jnp.zeros_like(dv_acc)
            ds_ref[...] = jnp.zeros_like(ds_ref)

        low = _low_half((BLK, LANES))
        low_keys = _low_half((3 * BLK + n_ctx, LANES))
        win_start = pl.multiple_of(ws_j * BLK, BLK)
        scale = HEAD_DIM ** -0.5
        dk_heads, dv_heads = [], []
        bias = _attn_bias(j, ws_j, nb, n_ctx)
        for kh in range(N_KV):
            sl = slice(kh * LANES, (kh + 1) * LANES)
            kk = jnp.concatenate([k0[:, sl], k1[:, sl], k2[:, sl], kc[:, sl]], axis=0)
            vv = jnp.concatenate([v0[:, sl], v1[:, sl], v2[:, sl], vc[:, sl]], axis=0)
            qs, p, p_snk, inv = _attn_probs(q_ref, kk, kh, bias, sink_ref)
            dos = []
            for g in range(GRP):
                h = GRP * kh + g
                chunk = dy_ref[:, (h // 2) * LANES:(h // 2 + 1) * LANES]
                dos.append(jnp.where(low if h % 2 == 0 else ~low, chunk, jnp.zeros_like(chunk)).astype(BF16))
            dos = jnp.concatenate(dos, axis=0)
            dp = lax.dot_general(dos, vv, (((1,), (1,)), ((), ())), preferred_element_type=F32)
            dsum = jnp.sum(p * dp, axis=-1, keepdims=True) * inv
            ds = (p * ((dp - dsum) * inv)).astype(BF16)
            snk_term = p_snk * inv * dsum
            for g in range(GRP):
                contrib = -jnp.sum(snk_term[g * BLK:(g + 1) * BLK], axis=0, keepdims=True)
                ds_ref[GRP * kh + g:GRP * kh + g + 1, :] += jnp.broadcast_to(contrib, (1, LANES))
            dqs = jnp.dot(ds, kk, preferred_element_type=F32) * scale
            for half in range(2):
                even = dqs[(2 * half) * BLK:(2 * half + 1) * BLK]
                odd = dqs[(2 * half + 1) * BLK:(2 * half + 2) * BLK]
                ch = 2 * kh + half
                dq_ref[:, ch * LANES:(ch + 1) * LANES] = jnp.where(low, even, odd)
            dkk = lax.dot_general(ds, qs, (((0,), (0,)), ((), ())), preferred_element_type=F32)
            dvv = lax.dot_general((p * inv).astype(BF16), dos, (((0,), (0,)), ((), ())), preferred_element_type=F32)
            dk_heads.append(dkk + pltpu.roll(dkk, HEAD_DIM, 1))
            dv_heads.append(dvv + pltpu.roll(dvv, HEAD_DIM, 1))
        for pair in range(N_KV // 2):
            sl = slice(pair * LANES, (pair + 1) * LANES)
            for acc, heads in ((dk_acc, dk_heads), (dv_acc, dv_heads)):
                both = jnp.where(low_keys, heads[2 * pair], heads[2 * pair + 1])
                acc[pl.ds(win_start, 3 * BLK), sl] += both[:3 * BLK]
                acc[n_lat:n_lat + n_ctx, sl] += both[3 * BLK:]

        @pl.when(j == n_steps - 1)
        def _():
            ck = pltpu.make_async_copy(dk_acc, dk_hbm, sem.at[0])
            cv = pltpu.make_async_copy(dv_acc, dv_hbm, sem.at[1])
            ck.start()
            cv.start()
            ck.wait()
            cv.wait()

    outs = pl.pallas_call(
        body, name=name, grid=(n_steps,),
        in_specs=[_row_spec(BLK, D)] + win + [ctx] + win + [ctx] + [_full_spec((N_HEADS, LANES)), _row_spec(BLK, D)]
        + [ANY] * len(hosted.arrays),
        out_specs=[_row_spec(BLK, D), ANY, ANY, _full_spec((N_HEADS, LANES))] + [ANY] * len(hosted.out_shapes),
        out_shape=[jax.ShapeDtypeStruct((t_rows, D), F32), jax.ShapeDtypeStruct((t_rows, KV_W), F32),
                   jax.ShapeDtypeStruct((t_rows, KV_W), F32), jax.ShapeDtypeStruct((N_HEADS, LANES), F32)]
        + list(hosted.out_shapes),
        scratch_shapes=[pltpu.VMEM((t_rows, KV_W), F32), pltpu.VMEM((t_rows, KV_W), F32),
                        pltpu.SemaphoreType.DMA((2,))] + list(hosted.scratch),
        compiler_params=_params(("arbitrary",)),
    )(q, kd, kd, kd, kd, vd, vd, vd, vd, sink_b, dy, *hosted.arrays)
    return outs[0], outs[1], outs[2], outs[3], outs[4:]


_GELU_K = math.sqrt(2.0 / math.pi)


def _gelu(v):
    return jax.nn.gelu(v)


def _dgelu(v):
    t = jnp.tanh(_GELU_K * (v + 0.044715 * v * v * v))
    return 0.5 * (1.0 + t) + 0.5 * v * (1.0 - t * t) * _GELU_K * (1.0 + 3.0 * 0.044715 * v * v)


def _gating_fwd(z_a, w_s, b_sb, g_v, *, name):
    t_rows = z_a.shape[0]

    def body(z_ref, w_ref, b_ref, g_ref, o_ref):
        u = _gelu(z_ref[:, :D])
        v = _gelu(z_ref[:, D:])
        vn = v * lax.rsqrt(jnp.mean(v * v, axis=-1, keepdims=True) + EPS) * g_ref[...]
        for g in range(A_GROUPS):
            sl = slice(g * LANES, (g + 1) * LANES)
            mixed = jnp.dot(w_ref[g].astype(BF16), vn[:, sl].astype(BF16), preferred_element_type=F32) + b_ref[g]
            o_ref[:, sl] = (u[:, sl] * mixed).astype(BF16)

    return pl.pallas_call(
        body, name=name, grid=(t_rows // BLK,),
        in_specs=[_row_spec(BLK, A_COLS), _full_spec((A_GROUPS, BLK, BLK)), _full_spec((A_GROUPS, BLK, LANES)),
                  _full_spec((1, D))],
        out_specs=_row_spec(BLK, D), out_shape=jax.ShapeDtypeStruct((t_rows, D), BF16),
        compiler_params=_params(("parallel",)),
    )(z_a, w_s, b_sb, g_v)


def _gating_bwd(z_a, dy, w_s, b_sb, g_v, *, name):
    t_rows = z_a.shape[0]

    def body(z_ref, dy_ref, w_ref, b_ref, g_ref, dz_ref, dw_ref, db_ref, st_ref):
        i = pl.program_id(0)

        @pl.when(i == 0)
        def _():
            dw_ref[...] = jnp.zeros_like(dw_ref)
            db_ref[...] = jnp.zeros_like(db_ref)
            st_ref[...] = jnp.zeros_like(st_ref)

        zu = z_ref[:, :D]
        zv = z_ref[:, D:]
        u = _gelu(zu)
        v = _gelu(zv)
        gv = g_ref[...]
        rstd = lax.rsqrt(jnp.mean(v * v, axis=-1, keepdims=True) + EPS)
        vh = v * rstd
        vn = vh * gv
        dyv = dy_ref[...]
        dvn = []
        for g in range(A_GROUPS):
            sl = slice(g * LANES, (g + 1) * LANES)
            wg = w_ref[g].astype(BF16)
            vg = vn[:, sl].astype(BF16)
            mixed = jnp.dot(wg, vg, preferred_element_type=F32) + b_ref[g]
            dz_ref[:, sl] = (dyv[:, sl] * mixed * _dgelu(zu[:, sl])).astype(BF16)
            dmixed = dyv[:, sl] * u[:, sl]
            dmb = dmixed.astype(BF16)
            dvn.append(lax.dot_general(wg, dmb, (((0,), (0,)), ((), ())), preferred_element_type=F32))
            dw_ref[g] += lax.dot_general(dmb, vg, (((1,), (1,)), ((), ())), preferred_element_type=F32)
            db_ref[g] += jnp.broadcast_to(jnp.sum(dmixed, axis=-1, keepdims=True), (BLK, LANES))
        dvn = jnp.concatenate(dvn, axis=1)
        st_ref[...] += _rows8([(0, _colsum(dvn * vh))], D)
        e = dvn * gv
        dv = rstd * (e - vh * jnp.mean(e * vh, axis=-1, keepdims=True))
        dz_ref[:, D:] = (dv * _dgelu(zv)).astype(BF16)

    return pl.pallas_call(
        body, name=name, grid=(t_rows // BLK,),
        in_specs=[_row_spec(BLK, A_COLS), _row_spec(BLK, D), _full_spec((A_GROUPS, BLK, BLK)),
                  _full_spec((A_GROUPS, BLK, LANES)), _full_spec((1, D))],
        out_specs=[_row_spec(BLK, A_COLS), _full_spec((A_GROUPS, BLK, BLK)), _full_spec((A_GROUPS, BLK, LANES)),
                   _full_spec((SUBLANES, D))],
        out_shape=[jax.ShapeDtypeStruct((t_rows, A_COLS), BF16), jax.ShapeDtypeStruct((A_GROUPS, BLK, BLK), F32),
                   jax.ShapeDtypeStruct((A_GROUPS, BLK, LANES), F32), jax.ShapeDtypeStruct((SUBLANES, D), F32)],
        compiler_params=_params(("arbitrary",)),
    )(z_a, dy, w_s, b_sb, g_v)


def _sconv_fwd(z_b, w8, n_lat, *, name):
    t_rows = z_b.shape[0]
    tm = 256
    prev, nxt = _halo_specs(tm, B_COLS, t_rows)

    def body(z_ref, zp_ref, zn_ref, w_ref, o_ref):
        p = z_ref[:, D:2 * D] * z_ref[:, 2 * D:]
        pp = zp_ref[:, D:2 * D] * zp_ref[:, 2 * D:]
        pn = zn_ref[:, D:2 * D] * zn_ref[:, 2 * D:]
        up, dn = _shift_rows(p, pp, pn, n_lat, t_rows, tm)
        conv = w_ref[0:1, :] * up + w_ref[1:2, :] * p + w_ref[2:3, :] * dn
        o_ref[...] = (z_ref[:, :D] * conv).astype(BF16)

    return pl.pallas_call(
        body, name=name, grid=(t_rows // tm,),
        in_specs=[_row_spec(tm, B_COLS), prev, nxt, _full_spec((SUBLANES, D))],
        out_specs=_row_spec(tm, D), out_shape=jax.ShapeDtypeStruct((t_rows, D), BF16),
        compiler_params=_params(("parallel",)),
    )(z_b, z_b, z_b, w8)


def _sconv_bwd(z_b, dy, w8, n_lat, *, name):
    t_rows = z_b.shape[0]
    tm = 256
    prev, nxt = _halo_specs(tm, B_COLS, t_rows)
    dprev, dnxt = _halo_specs(tm, D, t_rows)

    def body(z_ref, zp_ref, zn_ref, dy_ref, dyp_ref, dyn_ref, w_ref, dz_ref, st_ref):
        i = pl.program_id(0)
        bg, cg, hb = z_ref[:, :D], z_ref[:, D:2 * D], z_ref[:, 2 * D:]
        p = cg * hb
        pp = zp_ref[:, D:2 * D] * zp_ref[:, 2 * D:]
        pn = zn_ref[:, D:2 * D] * zn_ref[:, 2 * D:]
        up, dn = _shift_rows(p, pp, pn, n_lat, t_rows, tm)
        w0, w1, w2 = w_ref[0:1, :], w_ref[1:2, :], w_ref[2:3, :]
        conv = w0 * up + w1 * p + w2 * dn
        dyv = dy_ref[...]
        dz_ref[:, :D] = (dyv * conv).astype(BF16)
        dcv = dyv * bg
        dcv_up, dcv_dn = _shift_rows(dcv, dyp_ref[...] * zp_ref[:, :D], dyn_ref[...] * zn_ref[:, :D], n_lat, t_rows, tm)
        dp = w0 * dcv_dn + w1 * dcv + w2 * dcv_up
        dz_ref[:, D:2 * D] = (dp * hb).astype(BF16)
        dz_ref[:, 2 * D:] = (dp * cg).astype(BF16)
        upd = _rows8([(0, _colsum(dcv * up)), (1, _colsum(dcv * p)), (2, _colsum(dcv * dn))], D)

        @pl.when(i == 0)
        def _():
            st_ref[...] = upd

        @pl.when(i > 0)
        def _():
            st_ref[...] += upd

    return pl.pallas_call(
        body, name=name, grid=(t_rows // tm,),
        in_specs=[_row_spec(tm, B_COLS), prev, nxt, _row_spec(tm, D), dprev, dnxt, _full_spec((SUBLANES, D))],
        out_specs=[_row_spec(tm, B_COLS), _full_spec((SUBLANES, D))],
        out_shape=[jax.ShapeDtypeStruct((t_rows, B_COLS), BF16), jax.ShapeDtypeStruct((SUBLANES, D), F32)],
        compiler_params=_params(("arbitrary",)),
    )(z_b, z_b, z_b, dy, dy, dy, w8)


def _merge_fwd(t0, t1, t2, z_g, b_gate, *, name):
    t_rows = t0.shape[0]
    tm = 256

    def body(t0_ref, t1_ref, t2_ref, z_ref, b_ref, o_ref):
        acc = None
        for k, t_ref in enumerate((t0_ref, t1_ref, t2_ref)):
            gate = jax.nn.sigmoid(z_ref[:, k * D:(k + 1) * D] + b_ref[:, k * D:(k + 1) * D])
            term = gate * t_ref[...]
            acc = term if acc is None else acc + term
        o_ref[...] = acc.astype(BF16)

    return pl.pallas_call(
        body, name=name, grid=(t_rows // tm,),
        in_specs=[_row_spec(tm, D)] * 3 + [_row_spec(tm, G_COLS), _full_spec((1, G_COLS))],
        out_specs=_row_spec(tm, D), out_shape=jax.ShapeDtypeStruct((t_rows, D), BF16),
        compiler_params=_params(("parallel",)),
    )(t0, t1, t2, z_g, b_gate)


def _merge_bwd(dmerged, t0, t1, t2, z_g, b_gate, *, name):
    t_rows = t0.shape[0]
    tm = 256

    def body(dm_ref, t0_ref, t1_ref, t2_ref, z_ref, b_ref, d0_ref, d1_ref, d2_ref, dz_ref, st_ref):
        i = pl.program_id(0)
        dm = dm_ref[...]
        sums = []
        for k, (t_ref, d_ref) in enumerate(((t0_ref, d0_ref), (t1_ref, d1_ref), (t2_ref, d2_ref))):
            gate = jax.nn.sigmoid(z_ref[:, k * D:(k + 1) * D] + b_ref[:, k * D:(k + 1) * D])
            d_ref[...] = (dm * gate).astype(BF16)
            dzg = dm * t_ref[...] * gate * (1.0 - gate)
            dz_ref[:, k * D:(k + 1) * D] = dzg.astype(BF16)
            sums.append(_colsum(dzg))
        upd = _rows8([(0, jnp.concatenate(sums, axis=1))], G_COLS)

        @pl.when(i == 0)
        def _():
            st_ref[...] = upd

        @pl.when(i > 0)
        def _():
            st_ref[...] += upd

    return pl.pallas_call(
        body, name=name, grid=(t_rows // tm,),
        in_specs=[_row_spec(tm, D)] * 4 + [_row_spec(tm, G_COLS), _full_spec((1, G_COLS))],
        out_specs=[_row_spec(tm, D)] * 3 + [_row_spec(tm, G_COLS), _full_spec((SUBLANES, G_COLS))],
        out_shape=[jax.ShapeDtypeStruct((t_rows, D), BF16)] * 3
        + [jax.ShapeDtypeStruct((t_rows, G_COLS), BF16), jax.ShapeDtypeStruct((SUBLANES, G_COLS), F32)],
        compiler_params=_params(("arbitrary",)),
    )(dmerged, t0, t1, t2, z_g, b_gate)


def _ffn_mid_fwd(up, w8, n_lat, *, name, hosted=None):
    hosted = hosted or _NO_EXCHANGE
    t_rows = up.shape[0]
    tm = 128
    n_steps = t_rows // tm
    prev, nxt = _halo_specs(tm, D_FF, t_rows)

    def body(*refs):
        ins, outs, _, h_refs = _split_refs(refs, 5, 2, 0, hosted)
        a_ref, ap_ref, an_ref, g_ref, w_ref = ins
        cv_ref, f_ref = outs
        _run_hosted(hosted, h_refs, pl.program_id(0), n_steps)
        a = a_ref[...]
        au, ad = _shift_rows(a, ap_ref[...], an_ref[...], n_lat, t_rows, tm)
        cv = w_ref[0:1, :] * au + w_ref[1:2, :] * a + w_ref[2:3, :] * ad
        cv_ref[...] = cv
        f_ref[...] = (_silu(cv) * g_ref[...]).astype(BF16)

    outs = pl.pallas_call(
        body, name=name, grid=(n_steps,),
        in_specs=[_row_spec(tm, D_FF), prev, nxt, _row_spec(tm, D_FF, 1), _full_spec((SUBLANES, D_FF))]
        + [ANY] * len(hosted.arrays),
        out_specs=[_row_spec(tm, D_FF), _row_spec(tm, D_FF)] + [ANY] * len(hosted.out_shapes),
        out_shape=[jax.ShapeDtypeStruct((t_rows, D_FF), F32), jax.ShapeDtypeStruct((t_rows, D_FF), BF16)]
        + list(hosted.out_shapes),
        scratch_shapes=list(hosted.scratch),
        compiler_params=_params(("arbitrary",)),
    )(up, up, up, up, w8, *hosted.arrays)
    return outs[0], outs[1], outs[2:]


def _ffn_mid_bwd(up, cv, df, w8, n_lat, *, name):
    t_rows = up.shape[0]
    tm = 128
    prev, nxt = _halo_specs(tm, D_FF, t_rows)
    gprev, gnxt = _halo_specs(tm, D_FF, t_rows, 1)

    def body(a_ref, ap_ref, an_ref, g_ref, gp_ref, gn_ref, cv_ref, cp_ref, cn_ref, df_ref, dfp_ref, dfn_ref,
             w_ref, o_ref, st_ref):
        i = pl.program_id(0)
        a = a_ref[...]
        au, ad = _shift_rows(a, ap_ref[...], an_ref[...], n_lat, t_rows, tm)
        cvv = cv_ref[...]
        dfv = df_ref[...]
        o_ref[:, D_FF:] = (dfv * _silu(cvv)).astype(BF16)
        dcv = dfv * g_ref[...] * _dsilu(cvv)
        dcv_p = dfp_ref[...] * gp_ref[...] * _dsilu(cp_ref[...])
        dcv_n = dfn_ref[...] * gn_ref[...] * _dsilu(cn_ref[...])
        du, dd = _shift_rows(dcv, dcv_p, dcv_n, n_lat, t_rows, tm)
        o_ref[:, :D_FF] = (w_ref[0:1, :] * dd + w_ref[1:2, :] * dcv + w_ref[2:3, :] * du).astype(BF16)
        upd = _rows8([(0, _colsum(dcv * au)), (1, _colsum(dcv * a)), (2, _colsum(dcv * ad))], D_FF)

        @pl.when(i == 0)
        def _():
            st_ref[...] = upd

        @pl.when(i > 0)
        def _():
            st_ref[...] += upd

    row = _row_spec(tm, D_FF)
    return pl.pallas_call(
        body, name=name, grid=(t_rows // tm,),
        in_specs=[row, prev, nxt, _row_spec(tm, D_FF, 1), gprev, gnxt, row, prev, nxt, row, prev, nxt,
                  _full_spec((SUBLANES, D_FF))],
        out_specs=[_row_spec(tm, 2 * D_FF), _full_spec((SUBLANES, D_FF))],
        out_shape=[jax.ShapeDtypeStruct((t_rows, 2 * D_FF), BF16), jax.ShapeDtypeStruct((SUBLANES, D_FF), F32)],
        compiler_params=_params(("arbitrary",)),
    )(up, up, up, up, up, up, cv, cv, cv, df, df, df, w8)


def _loss_head(x, target, g_final, n_lat, *, name):
    t_rows = x.shape[0]
    tm = 256
    last = n_lat // tm - 1

    def body(x_ref, t_ref, g_ref, dx_ref, st_ref):
        i = pl.program_id(0)
        is_ctx = i * tm >= n_lat
        xv = x_ref[...]
        gv = g_ref[...]
        rstd = lax.rsqrt(jnp.mean(xv * xv, axis=-1, keepdims=True) + EPS)
        rn = xv * rstd
        err = rn * gv - t_ref[...]
        dy = err / D
        e = dy * gv
        dx = rstd * (e - rn * jnp.mean(e * rn, axis=-1, keepdims=True))
        dx_ref[...] = jnp.where(is_ctx, 0.0, dx)
        keep = jnp.where(is_ctx, 0.0, 1.0)
        upd = _rows8([(0, keep * _colsum(dy * rn)), (1, keep * _colsum(err * err))], D)

        @pl.when(i == 0)
        def _():
            st_ref[...] = upd

        @pl.when(i > 0)
        def _():
            st_ref[...] += upd

    return pl.pallas_call(
        body, name=name, grid=(t_rows // tm,),
        in_specs=[_row_spec(tm, D), pl.BlockSpec((tm, D), lambda i: (jnp.minimum(i, last), 0)), _full_spec((1, D))],
        out_specs=[_row_spec(tm, D), _full_spec((SUBLANES, D))],
        out_shape=[jax.ShapeDtypeStruct((t_rows, D), F32), jax.ShapeDtypeStruct((SUBLANES, D), F32)],
        compiler_params=_params(("arbitrary",)),
    )(x, target, g_final)


def _sum_slabs(x, out_dtype, *, name):
    n_slabs, rows, cols = x.shape
    tm = _pick(rows, (432, 256, 192, 128, 64, 32, 24, 16, 8))

    def body(x_ref, o_ref):
        acc = x_ref[0].astype(F32)
        for s in range(1, n_slabs):
            acc = acc + x_ref[s].astype(F32)
        o_ref[...] = acc.astype(o_ref.dtype)

    return pl.pallas_call(
        body, name=name, grid=(rows // tm,),
        in_specs=[pl.BlockSpec((n_slabs, tm, cols), lambda i: (0, i, 0))],
        out_specs=pl.BlockSpec((tm, cols), lambda i: (i, 0)),
        out_shape=jax.ShapeDtypeStruct((rows, cols), out_dtype),
        compiler_params=_params(("parallel",)),
    )(x)


def _add_half(half_idx, a, b, *, name):
    n_slabs, rows, cols = b.shape
    tm = _pick(rows, (432, 256, 192, 128, 96, 64, 32, 16))
    per_half = rows // tm

    def body(half_ref, a_ref, b_ref, o_ref):
        o_ref[...] = (a_ref[...].astype(F32) + b_ref[...].astype(F32)).astype(BF16)

    spec = pl.BlockSpec((1, tm, cols), lambda s, i, half_ref: (s, i, 0))
    a_spec = pl.BlockSpec((1, tm, cols), lambda s, i, half_ref: (s, half_ref[0] * per_half + i, 0))
    return pl.pallas_call(
        body, name=name,
        grid_spec=pltpu.PrefetchScalarGridSpec(num_scalar_prefetch=1, grid=(n_slabs, per_half),
                                               in_specs=[a_spec, spec], out_specs=spec),
        out_shape=jax.ShapeDtypeStruct(b.shape, BF16), compiler_params=_params(("parallel", "parallel")),
    )(half_idx, a, b)


def _adamw(w, g, m, v, *, name):
    rows, cols = w.shape
    tm = _pick(rows, (128, 64, 32, 16, 8))

    def body(w_ref, g_ref, m_ref, v_ref, d_ref, nm_ref, nv_ref):
        gv = g_ref[...]
        nm = ADAM_B1 * m_ref[...] + (1.0 - ADAM_B1) * gv
        nv = ADAM_B2 * v_ref[...] + (1.0 - ADAM_B2) * jnp.square(gv)
        m_hat = nm / (1.0 - ADAM_B1 ** ADAM_STEP)
        v_hat = nv / (1.0 - ADAM_B2 ** ADAM_STEP)
        d_ref[...] = -ADAM_LR * (m_hat / (jnp.sqrt(v_hat) + ADAM_EPS) + ADAM_WD * w_ref[...])
        nm_ref[...] = nm
        nv_ref[...] = nv

    spec = pl.BlockSpec((tm, cols), lambda i: (i, 0))
    shape = jax.ShapeDtypeStruct((rows, cols), F32)
    return pl.pallas_call(
        body, name=name, grid=(rows // tm,), in_specs=[spec] * 4, out_specs=[spec] * 3, out_shape=[shape] * 3,
        compiler_params=_params(("parallel",)),
    )(w, g, m, v)


def _place():
    x, y, c = lax.axis_index("x"), lax.axis_index("y"), lax.axis_index("c")
    chips = [(1 - x, y), (x, 1 - y), (1 - x, 1 - y)]
    return x, y, c, chips


def _remote(src, dst, send_sems, recv_sems, k, to):
    return pltpu.make_async_remote_copy(src_ref=src, dst_ref=dst, send_sem=send_sems.at[k], recv_sem=recv_sems.at[k],
                                        device_id=to, device_id_type=MESH)


HALF_CHUNKS = 2


def _chunks(ref, n):
    step = ref.shape[0] // n
    tile_rows = SUBLANES if ref.dtype == F32 else 2 * SUBLANES
    assert step * n == ref.shape[0] and step % tile_rows == 0, (ref.shape, n)
    return [ref.at[pl.ds(k * step, step)] for k in range(n)]


def _half(ref, which):
    half = ref.shape[0] // 2
    return ref.at[pl.ds(pl.multiple_of(which * half, 2 * SUBLANES), half)]


def _staged_copy(src, dst, buf, sems):
    step = buf.shape[1]
    n = src.shape[0] // step
    assert n * step == src.shape[0], (src.shape, step)
    ins = [pltpu.make_async_copy(src.at[pl.ds(k * step, step)], buf.at[k % 2], sems.at[k % 2]) for k in range(n)]
    outs = [pltpu.make_async_copy(buf.at[k % 2], dst.at[pl.ds(k * step, step)], sems.at[2 + k % 2]) for k in range(n)]
    ins[0].start()
    for k in range(n):
        ins[k].wait()
        outs[k].start()
        if k + 1 < n:
            if k >= 1:
                outs[k - 1].wait()
            ins[k + 1].start()
    if n >= 2:
        outs[n - 2].wait()
    outs[n - 1].wait()


def _stage_rows(rows):
    return _pick(rows, (256, 432))


def _stage_scratch(slabs):
    return [pltpu.VMEM((2, _stage_rows(s.shape[-2]), s.shape[-1]), s.dtype) for s in slabs] + [pltpu.SemaphoreType.DMA((4,))]


N_LINK_SEMS = (N_CHIPS - 1) * HALF_CHUNKS


def _link_sems(n_groups):
    return [pltpu.SemaphoreType.DMA((n_groups * N_LINK_SEMS,)), pltpu.SemaphoreType.DMA((n_groups * N_LINK_SEMS,))]


def _sem_index(g, j, k):
    return g * N_LINK_SEMS + j * HALF_CHUNKS + k


def _gather_ici_start(*refs):
    n = (len(refs) - 2) // 2
    p_refs, o_refs, (send_sems, recv_sems) = refs[:n], refs[n:2 * n], refs[2 * n:]
    x, y, c, chips = _place()
    for g, (p_ref, o_ref) in enumerate(zip(p_refs, o_refs)):
        src = _chunks(_half(p_ref, c), HALF_CHUNKS)
        dst = _chunks(_half(o_ref.at[2 * x + y], c), HALF_CHUNKS)
        for j, chip in enumerate(chips):
            for k in range(HALF_CHUNKS):
                _remote(src[k], dst[k], send_sems, recv_sems, _sem_index(g, j, k), (*chip, c)).start()


def _gather_ici_finish(*refs):
    n = (len(refs) - 2) // 2
    p_refs, o_refs, (send_sems, recv_sems) = refs[:n], refs[n:2 * n], refs[2 * n:]
    x, y, c, chips = _place()
    for g, (p_ref, o_ref) in enumerate(zip(p_refs, o_refs)):
        src = _chunks(_half(p_ref, c), HALF_CHUNKS)
        for j, (cx, cy) in enumerate(chips):
            for k, landed in enumerate(_chunks(_half(o_ref.at[2 * cx + cy], c), HALF_CHUNKS)):
                _remote(src[k], landed, send_sems, recv_sems, _sem_index(g, j, k), (x, y, c)).wait_recv()
        for j in range(len(chips)):
            for k in range(HALF_CHUNKS):
                _remote(src[k], src[k], send_sems, recv_sems, _sem_index(g, j, k), (x, y, c)).wait_send()


def _gathered_shapes(slabs):
    return [jax.ShapeDtypeStruct((N_CHIPS,) + s.shape, s.dtype) for s in slabs]


def _gather_ici_hosted(slabs):
    return _Hosted(list(slabs), _gathered_shapes(slabs), _link_sems(len(slabs)), _gather_ici_start, _gather_ici_finish)


def _gather_ici(slabs, *, name):
    def body(*refs):
        _gather_ici_start(*refs)
        _gather_ici_finish(*refs)

    return pl.pallas_call(
        body, name=name, in_specs=[ANY] * len(slabs), out_specs=[ANY] * len(slabs),
        out_shape=_gathered_shapes(slabs), scratch_shapes=_link_sems(len(slabs)),
    )(*slabs)


def _gather_finish(partials, slabs, *, name):
    n = len(slabs)

    def body(*refs):
        p_refs, o_refs = refs[n:2 * n], refs[2 * n:3 * n]
        send_sems, recv_sems = refs[3 * n:3 * n + 2]
        bufs, loc_sems = refs[3 * n + 2:4 * n + 2], refs[4 * n + 2]
        x, y, c, chips = _place()
        sib = (x, y, 1 - c)
        passed = []
        for g, o_ref in enumerate(o_refs):
            for j, (cx, cy) in enumerate(chips):
                for k, landed in enumerate(_chunks(_half(o_ref.at[2 * cx + cy], c), HALF_CHUNKS)):
                    passed.append(_remote(landed, landed, send_sems, recv_sems, _sem_index(g, j, k), sib))
        for cp in passed:
            cp.start()
        for p_ref, o_ref, buf in zip(p_refs, o_refs, bufs):
            _staged_copy(p_ref, o_ref.at[2 * x + y], buf, loc_sems)
        for g, o_ref in enumerate(o_refs):
            for j, (cx, cy) in enumerate(chips):
                for k, landed in enumerate(_chunks(_half(o_ref.at[2 * cx + cy], 1 - c), HALF_CHUNKS)):
                    _remote(landed, landed, send_sems, recv_sems, _sem_index(g, j, k), sib).wait_recv()
        for cp in passed:
            cp.wait_send()

    return pl.pallas_call(
        body, name=name, in_specs=[ANY] * (2 * n), out_specs=[ANY] * n,
        out_shape=[jax.ShapeDtypeStruct(p.shape, p.dtype) for p in partials],
        input_output_aliases={g: g for g in range(n)}, scratch_shapes=_link_sems(n) + _stage_scratch(slabs),
        compiler_params=_params(),
    )(*partials, *slabs)


def _grad_sibling_swap(g_packs, *, name):
    n = len(g_packs)
    per_group = N_CHIPS * HALF_CHUNKS

    def body(*refs):
        g_refs, got_refs, (send_sems, recv_sems) = refs[:n], refs[n:2 * n], refs[2 * n:]
        x, y, c, _ = _place()
        sib = (x, y, 1 - c)
        swaps = [_remote(src, dst, send_sems, recv_sems, g * per_group + s * HALF_CHUNKS + k, sib)
                 for g, (g_ref, got_ref) in enumerate(zip(g_refs, got_refs))
                 for s in range(N_CHIPS)
                 for k, (src, dst) in enumerate(zip(_chunks(_half(g_ref.at[s], 1 - c), HALF_CHUNKS),
                                                    _chunks(got_ref.at[s], HALF_CHUNKS)))]
        for cp in swaps:
            cp.start()
        for cp in swaps:
            cp.wait_recv()
        for cp in swaps:
            cp.wait_send()

    return pl.pallas_call(
        body, name=name, in_specs=[ANY] * n, out_specs=[ANY] * n,
        out_shape=[jax.ShapeDtypeStruct((N_CHIPS, g.shape[1] // 2, g.shape[2]), g.dtype) for g in g_packs],
        scratch_shapes=[pltpu.SemaphoreType.DMA((n * per_group,)), pltpu.SemaphoreType.DMA((n * per_group,))],
    )(*g_packs)


def _grad_ici_refs(refs):
    n = (len(refs) - 3) // 3
    return refs[:n], refs[n:2 * n], refs[2 * n], refs[2 * n + 1], refs[2 * n + 2:3 * n + 2], refs[3 * n + 2]


def _grad_ici_start(*refs):
    s_refs, o_refs, send_sems, recv_sems, _, _ = _grad_ici_refs(refs)
    x, y, c, chips = _place()
    for g, (s_ref, o_ref) in enumerate(zip(s_refs, o_refs)):
        for j, (cx, cy) in enumerate(chips):
            pairs = zip(_chunks(s_ref.at[2 * cx + cy], HALF_CHUNKS), _chunks(o_ref.at[2 * x + y], HALF_CHUNKS))
            for k, (src, dst) in enumerate(pairs):
                _remote(src, dst, send_sems, recv_sems, _sem_index(g, j, k), (cx, cy, c)).start()


def _grad_ici_finish(*refs):
    s_refs, o_refs, send_sems, recv_sems, bufs, loc_sems = _grad_ici_refs(refs)
    x, y, c, chips = _place()
    me = 2 * x + y
    for s_ref, o_ref, buf in zip(s_refs, o_refs, bufs):
        _staged_copy(s_ref.at[me], o_ref.at[me], buf, loc_sems)
    for g, (s_ref, o_ref) in enumerate(zip(s_refs, o_refs)):
        for j, (cx, cy) in enumerate(chips):
            for k, landed in enumerate(_chunks(o_ref.at[2 * cx + cy], HALF_CHUNKS)):
                _remote(landed, landed, send_sems, recv_sems, _sem_index(g, j, k), (x, y, c)).wait_recv()
        for j, (cx, cy) in enumerate(chips):
            for k, sent in enumerate(_chunks(s_ref.at[2 * cx + cy], HALF_CHUNKS)):
                _remote(sent, sent, send_sems, recv_sems, _sem_index(g, j, k), (x, y, c)).wait_send()


def _grad_ici_hosted(sums):
    return _Hosted(list(sums), [jax.ShapeDtypeStruct(s.shape, s.dtype) for s in sums],
                   _link_sems(len(sums)) + _stage_scratch(sums), _grad_ici_start, _grad_ici_finish)


def _grad_ici(sums, *, name):
    def body(*refs):
        _grad_ici_start(*refs)
        _grad_ici_finish(*refs)

    n = len(sums)
    return pl.pallas_call(
        body, name=name, in_specs=[ANY] * n, out_specs=[ANY] * n,
        out_shape=[jax.ShapeDtypeStruct(s.shape, s.dtype) for s in sums],
        scratch_shapes=_link_sems(n) + _stage_scratch(sums), compiler_params=_params(),
    )(*sums)


def _grad_sibling_share(totals, *, name):
    n = len(totals)
    n_ch = HALF_CHUNKS

    def body(*refs):
        t_refs, o_refs = refs[:n], refs[n:2 * n]
        send_sems, recv_sems = refs[2 * n:2 * n + 2]
        bufs, loc_sems = refs[2 * n + 2:3 * n + 2], refs[3 * n + 2]
        x, y, c, _ = _place()
        sib = (x, y, 1 - c)
        sends = [_remote(src, dst, send_sems, recv_sems, g * n_ch + k, sib)
                 for g, (t_ref, o_ref) in enumerate(zip(t_refs, o_refs))
                 for k, (src, dst) in enumerate(zip(_chunks(t_ref, n_ch), _chunks(_half(o_ref, c), n_ch)))]
        for cp in sends:
            cp.start()
        for t_ref, o_ref, buf in zip(t_refs, o_refs, bufs):
            _staged_copy(t_ref, _half(o_ref, c), buf, loc_sems)
        for g, o_ref in enumerate(o_refs):
            for k, landed in enumerate(_chunks(_half(o_ref, 1 - c), n_ch)):
                _remote(landed, landed, send_sems, recv_sems, g * n_ch + k, sib).wait_recv()
        for cp in sends:
            cp.wait_send()

    return pl.pallas_call(
        body, name=name, in_specs=[ANY] * n, out_specs=[ANY] * n,
        out_shape=[jax.ShapeDtypeStruct((2 * t.shape[0], t.shape[1]), t.dtype) for t in totals],
        scratch_shapes=[pltpu.SemaphoreType.DMA((n * n_ch,)), pltpu.SemaphoreType.DMA((n * n_ch,))] + _stage_scratch(totals),
        compiler_params=_params(),
    )(*totals)


def _allgather8(v, *, name):
    rows, cols = v.shape

    def body(v_ref, o_ref, send_sems, recv_sems, loc_sem):
        x, y, c, chips = _place()
        sib = (x, y, 1 - c)

        def slot(px, py, pc):
            return o_ref.at[4 * px + 2 * py + pc]

        local = pltpu.make_async_copy(v_ref, slot(x, y, c), loc_sem.at[0])
        local.start()
        first = [_remote(v_ref, slot(x, y, c), send_sems, recv_sems, 0, sib)]
        first += [_remote(v_ref, slot(x, y, c), send_sems, recv_sems, 1 + j, (*chip, c)) for j, chip in enumerate(chips)]
        for cp in first:
            cp.start()
        passed = [_remote(slot(*chip, c), slot(*chip, c), send_sems, recv_sems, 4 + j, sib)
                  for j, chip in enumerate(chips)]
        for j, chip in enumerate(chips):
            _remote(v_ref, slot(*chip, c), send_sems, recv_sems, 1 + j, sib).wait_recv()
            passed[j].start()
        _remote(v_ref, slot(x, y, 1 - c), send_sems, recv_sems, 0, sib).wait_recv()
        for j, chip in enumerate(chips):
            _remote(v_ref, slot(*chip, 1 - c), send_sems, recv_sems, 4 + j, sib).wait_recv()
        for cp in first + passed:
            cp.wait_send()
        local.wait()

    return pl.pallas_call(
        body, name=name, in_specs=[ANY], out_specs=ANY, out_shape=jax.ShapeDtypeStruct((N_DEV, rows, cols), v.dtype),
        scratch_shapes=[pltpu.SemaphoreType.DMA((7,)), pltpu.SemaphoreType.DMA((7,)), pltpu.SemaphoreType.DMA((1,))],
    )(v)


_BIG = (("w_mod", (D, 6 * D), 1), ("w_in", (D, IN_W), 1), ("w_branch", (3 * D, D), None), ("w_out", (D, D), 0),
        ("w_up", (D, 2 * D_FF), 1), ("w_down", (D_FF, D), 0))
_COL_SHARDED = ("w_mod", "w_in", "w_up")
_ROW_SHARDED = (("w_branch", 3 * D // N_CHIPS), ("w_out", D // N_CHIPS), ("w_down", D_FF // N_CHIPS))


def _pack_shards(shards, layer):
    rows = jnp.concatenate([shards[n][layer].reshape(r, D) for n, r in _ROW_SHARDED], axis=0)
    return [shards[n][layer] for n in _COL_SHARDED] + [rows]


def _unpack_full(gathered):
    out = {}
    for name, blk in zip(_COL_SHARDED, gathered):
        out[name] = blk.transpose(1, 0, 2).reshape(blk.shape[1], N_CHIPS * blk.shape[2])
    off = 0
    for name, r in _ROW_SHARDED:
        blk = gathered[-1][:, off:off + r, :]
        off += r
        if name == "w_branch":
            out[name] = blk.reshape(N_CHIPS, 3, D // N_CHIPS, D).transpose(1, 0, 2, 3).reshape(3, D, D)
        else:
            out[name] = blk.reshape(N_CHIPS * r, D)
    return out


def _pack_grad_cols(g):
    return g.reshape(g.shape[0], N_CHIPS, g.shape[1] // N_CHIPS).transpose(1, 0, 2)


def _pack_grad_rows(grads):
    parts = []
    for name, r in _ROW_SHARDED:
        g = grads[name]
        if name == "w_branch":
            g = g.reshape(3, N_CHIPS, D // N_CHIPS, D).transpose(1, 0, 2, 3)
        parts.append(g.reshape(N_CHIPS, r, D))
    return jnp.concatenate(parts, axis=1)


def _pack_grads(grads):
    return [_pack_grad_cols(grads[n]) for n in _COL_SHARDED] + [_pack_grad_rows(grads)]


def _unpack_shards(totals, like):
    out = {n: jnp.stack([totals[l][g] for l in range(DEPTH)]) for g, n in enumerate(_COL_SHARDED)}
    off = 0
    for name, r in _ROW_SHARDED:
        out[name] = jnp.stack([totals[l][-1][off:off + r] for l in range(DEPTH)]).reshape(like[name].shape)
        off += r
    return out


def _pad_rows(v, rows):
    return jnp.concatenate([v, jnp.zeros((rows - v.shape[0],) + v.shape[1:], v.dtype)], axis=0)


def _local_step(x_tok, target, c_vec, c_ctx, wfull, small, n_lat, n_ctx):
    ctx = _step_context(c_vec, c_ctx, n_lat, n_ctx)
    saved = []
    xs = x_tok
    for l in range(DEPTH):
        xs, s, _ = _layer_fwd(l, xs, wfull[l], {k: v[l] for k, v in small.items() if k != "g_final"}, ctx)
        saved.append(s)
    dx, sq_err, d_g_final = _loss_bwd(xs, target, small["g_final"], n_lat)
    wgrads, lgrads, d_a128 = [None] * DEPTH, [None] * DEPTH, [None] * DEPTH
    for l in reversed(range(DEPTH)):
        dx, wgrads[l], lgrads[l], d_a128[l], _ = _layer_bwd(l, saved[l], wfull[l], dx, ctx)
    return sq_err, dx, wgrads, _small_grads(lgrads, d_a128, d_g_final, ctx)


def _step_context(c_vec, c_ctx, n_lat, n_ctx):
    cos_t, sin_t = _rope_tables(n_lat, n_ctx)
    a_in = _pad_rows(jnp.stack([c_vec, c_ctx]), LANES)
    a128 = _small(_silu, (LANES, D), a_in, name="cond_silu")
    return dict(cos_t=cos_t, sin_t=sin_t, a_in=a_in, a128=a128, n_lat=n_lat, n_ctx=n_ctx)


def _loss_bwd(xs, target, g_final, n_lat):
    dx, st = _loss_head(xs, target, g_final[None, :], n_lat, name="loss_head")
    return dx, st[1], st[0]


def _small_grads(lgrads, d_a128, d_g_final, ctx):
    d_cond = _small(lambda a, b, cin: (a + b) * _dsilu(cin), (LANES, D), d_a128[0], d_a128[1], ctx["a_in"],
                    name="cond_bwd")
    out = {k: jnp.stack([lgrads[l][k] for l in range(DEPTH)]) for k in lgrads[0]}
    out["c_ctx"] = d_cond[1]
    out["g_final"] = d_g_final
    return out


def _layer_fwd(l, xs, w, sm, ctx, hosted=_NO_EXCHANGE, hosted_ffn=_NO_EXCHANGE):
    n_lat, n_ctx, cos_t, sin_t, a128 = ctx["n_lat"], ctx["n_ctx"], ctx["cos_t"], ctx["sin_t"], ctx["a128"]
    mod128 = _mm(a128, w["w_mod"], name=f"mod{l}")
    mod8 = _small(lambda m, b: m + b, (SUBLANES, 6 * D), mod128[:SUBLANES], sm["b_mod"][None, :], name=f"mod_bias{l}")
    g_mix = sm["g_mix"][None, :]
    g_ffn = sm["g_ffn"][None, :]
    g_v = sm["g_v"][None, :]
    b_gate = sm["b_gate"][None, :]
    sink_b = jnp.broadcast_to(sm["sink"][:, None], (N_HEADS, LANES))
    b_sb = jnp.broadcast_to(sm["b_spatial"][:, :, None], (A_GROUPS, BLK, LANES))
    w_sconv8 = _pad_rows(sm["w_sconv"], SUBLANES)
    w_fconv8 = _pad_rows(sm["w_fconv"], SUBLANES)
    w_in = w["w_in"]
    w_seg = [w_in[:, SEG[k]:SEG[k + 1]] for k in range(4)]

    h = _norm_mod_fwd(xs, g_mix, mod8, 0, 1, n_lat, name=f"norm1_{l}")
    z_qkv, z_a, z_b, z_g = [_mm(h, w_seg[k], name=f"in_proj{k}_{l}") for k in range(4)]
    q, kd, vd = _qkv_prep(z_qkv, cos_t, sin_t, name=f"qkv_prep{l}")
    y_attn, carried = _attention_fwd(q, kd, vd, sink_b, n_lat, n_ctx, name=f"attn{l}", hosted=hosted)
    y_a = _gating_fwd(z_a, sm["w_spatial"], b_sb, g_v, name=f"gating{l}")
    y_b = _sconv_fwd(z_b, w_sconv8, n_lat, name=f"sconv{l}")
    ys = (y_attn, y_a, y_b)
    ts = [_mm(ys[k], w["w_branch"][k], name=f"branch{k}_{l}") for k in range(3)]
    merged = _merge_fwd(*ts, z_g, b_gate, name=f"merge{l}")
    mix_out = _mm(merged, w["w_out"], name=f"out_proj{l}")
    x1 = _residual_fwd(xs, mix_out, mod8, 2, n_lat, name=f"res1_{l}")
    h2 = _norm_mod_fwd(x1, g_ffn, mod8, 3, 4, n_lat, name=f"norm2_{l}")
    up = _mm(h2, w["w_up"], name=f"up_proj{l}")
    cv, f, carried_ffn = _ffn_mid_fwd(up, w_fconv8, n_lat, name=f"ffn_mid{l}", hosted=hosted_ffn)
    ffn_out = _mm(f, w["w_down"], name=f"down_proj{l}")
    x2 = _residual_fwd(x1, ffn_out, mod8, 5, n_lat, name=f"res2_{l}")
    saved = dict(x0=xs, mod8=mod8, h=h, z_qkv=z_qkv, z_a=z_a, z_b=z_b, z_g=z_g, q=q, kd=kd, vd=vd, ys=ys, ts=ts,
                 merged=merged, mix_out=mix_out, x1=x1, h2=h2, up=up, cv=cv, f=f, ffn_out=ffn_out, w_seg=w_seg,
                 g_mix=g_mix, g_ffn=g_ffn, g_v=g_v, b_gate=b_gate, sink_b=sink_b, b_sb=b_sb,
                 w_sconv8=w_sconv8, w_fconv8=w_fconv8, w_spatial=sm["w_spatial"])
    return x2, saved, (carried, carried_ffn)


def _layer_bwd(l, s, w, dx, ctx, make_hosted=None):
    n_lat, n_ctx, cos_t, sin_t, a128 = ctx["n_lat"], ctx["n_ctx"], ctx["cos_t"], ctx["sin_t"], ctx["a128"]
    mod8 = s["mod8"]
    d_ffn, st_gt2 = _residual_bwd(dx, s["ffn_out"], mod8, 5, n_lat, name=f"res2_bwd{l}")
    df = _mm(d_ffn, w["w_down"], tb=True, name=f"down_bwd_x{l}")
    g_down = _mm(s["f"], d_ffn, ta=True, out_dtype=BF16, name=f"down_bwd_w{l}")
    d_up, st_fc = _ffn_mid_bwd(s["up"], s["cv"], df, s["w_fconv8"], n_lat, name=f"ffn_mid_bwd{l}")
    dh2 = _mm(d_up, w["w_up"], tb=True, name=f"up_bwd_x{l}")
    g_up = _mm(s["h2"], d_up, ta=True, out_dtype=BF16, name=f"up_bwd_w{l}")
    dx1, st_n2 = _norm_mod_bwd(s["x1"], [dh2], dx, s["g_ffn"], mod8, 4, n_lat, name=f"norm2_bwd{l}")
    d_out, st_gt1 = _residual_bwd(dx1, s["mix_out"], mod8, 2, n_lat, name=f"res1_bwd{l}")
    d_merged = _mm(d_out, w["w_out"], tb=True, name=f"out_bwd_x{l}")
    g_out = _mm(s["merged"], d_out, ta=True, out_dtype=BF16, name=f"out_bwd_w{l}")
    dt0, dt1, dt2, dz_g, st_bg = _merge_bwd(d_merged, *s["ts"], s["z_g"], s["b_gate"], name=f"merge_bwd{l}")
    dts = (dt0, dt1, dt2)
    dys = [_mm(dts[k], w["w_branch"][k], tb=True, name=f"branch{k}_bwd_x{l}") for k in range(3)]
    g_branch = jnp.stack([_mm(s["ys"][k], dts[k], ta=True, out_dtype=BF16, name=f"branch{k}_bwd_w{l}")
                          for k in range(3)])
    early = dict(w_branch=g_branch.reshape(3 * D, D), w_out=g_out, w_up=g_up, w_down=g_down)
    hosted = make_hosted(early) if make_hosted else _NO_EXCHANGE
    dq, dk, dv, d_sink, carried = _attention_bwd(s["q"], s["kd"], s["vd"], s["sink_b"], dys[0], n_lat, n_ctx,
                                                 name=f"attn_bwd{l}", hosted=hosted)
    dz_qkv = _qkv_unprep(dq, dk, dv, cos_t, sin_t, name=f"qkv_unprep{l}")
    dz_a, d_ws, d_bs, st_gv = _gating_bwd(s["z_a"], dys[1], s["w_spatial"], s["b_sb"], s["g_v"], name=f"gating_bwd{l}")
    dz_b, st_sc = _sconv_bwd(s["z_b"], dys[2], s["w_sconv8"], n_lat, name=f"sconv_bwd{l}")
    dzs = (dz_qkv, dz_a, dz_b, dz_g)
    dh_parts = [_mm(dzs[k], s["w_seg"][k], tb=True, name=f"in_bwd_x{k}_{l}") for k in range(4)]
    g_in = jnp.concatenate([_mm(s["h"], dzs[k], ta=True, out_dtype=BF16, name=f"in_bwd_w{k}_{l}")
                            for k in range(4)], axis=1)
    dx0, st_n1 = _norm_mod_bwd(s["x0"], dh_parts, dx1, s["g_mix"], mod8, 1, n_lat, name=f"norm1_bwd{l}")
    dmod = jnp.concatenate([st_n1[0:2], st_n1[2:4], st_gt1[0:2], st_n2[0:2], st_n2[2:4], st_gt2[0:2]], axis=1)
    dmod128 = _pad_rows(dmod, LANES)
    g_mod = _mm(a128, dmod128, ta=True, out_dtype=BF16, name=f"mod_bwd_w{l}")
    d_a128 = _mm(dmod128, w["w_mod"], tb=True, name=f"mod_bwd_x{l}")
    wgrads = dict(early, w_mod=g_mod, w_in=g_in)
    lgrads = dict(b_mod=dmod[0] + dmod[1], g_mix=st_n1[4], g_ffn=st_n2[4], b_gate=st_bg[0], sink=d_sink[:, 0],
                  w_spatial=d_ws, b_spatial=d_bs[:, :, 0], g_v=st_gv[0], w_sconv=st_sc[0:3], w_fconv=st_fc[0:3])
    return dx0, wgrads, lgrads, d_a128, carried


_SMALL_ORDER = ("c_ctx", "b_mod", "g_mix", "b_gate", "sink", "w_spatial", "b_spatial", "g_v", "w_sconv", "g_ffn",
                "w_fconv", "g_final")


def _flat_pack(parts, width):
    flat = jnp.concatenate([p.reshape(-1).astype(F32) for p in parts])
    rows = -(-flat.shape[0] // (width * SUBLANES)) * SUBLANES
    flat = jnp.concatenate([flat, jnp.zeros((rows * width - flat.shape[0],), F32)])
    return flat.reshape(rows, width)


def _flat_unpack(packed, likes):
    flat = packed.reshape(-1)
    out, off = [], 0
    for like in likes:
        n = math.prod(like.shape)
        out.append(flat[off:off + n].reshape(like.shape))
        off += n
    return out


def kernel(x, c, ctx, c_ctx, w_mod, b_mod, g_mix, w_in, b_gate, sink, w_spatial, b_spatial, g_v, w_sconv, w_branch, w_out, g_ffn, w_up, w_fconv, w_down, g_final, loss_target, m_c_ctx, m_w_mod, m_b_mod, m_g_mix, m_w_in, m_b_gate, m_sink, m_w_spatial, m_b_spatial, m_g_v, m_w_sconv, m_w_branch, m_w_out, m_g_ffn, m_w_up, m_w_fconv, m_w_down, m_g_final, v_c_ctx, v_w_mod, v_b_mod, v_g_mix, v_w_in, v_b_gate, v_sink, v_w_spatial, v_b_spatial, v_g_v, v_w_sconv, v_w_branch, v_w_out, v_g_ffn, v_w_up, v_w_fconv, v_w_down, v_g_final):
    n_lat, n_ctx = x.shape[1], ctx.shape[1]
    chip = 2 * lax.axis_index("x") + lax.axis_index("y")
    weights = dict(c_ctx=c_ctx, w_mod=w_mod, b_mod=b_mod, g_mix=g_mix, w_in=w_in, b_gate=b_gate, sink=sink,
                   w_spatial=w_spatial, b_spatial=b_spatial, g_v=g_v, w_sconv=w_sconv, w_branch=w_branch, w_out=w_out,
                   g_ffn=g_ffn, w_up=w_up, w_fconv=w_fconv, w_down=w_down, g_final=g_final)
    m_in = dict(c_ctx=m_c_ctx, w_mod=m_w_mod, b_mod=m_b_mod, g_mix=m_g_mix, w_in=m_w_in, b_gate=m_b_gate, sink=m_sink,
                w_spatial=m_w_spatial, b_spatial=m_b_spatial, g_v=m_g_v, w_sconv=m_w_sconv, w_branch=m_w_branch,
                w_out=m_w_out, g_ffn=m_g_ffn, w_up=m_w_up, w_fconv=m_w_fconv, w_down=m_w_down, g_final=m_g_final)
    v_in = dict(c_ctx=v_c_ctx, w_mod=v_w_mod, b_mod=v_b_mod, g_mix=v_g_mix, w_in=v_w_in, b_gate=v_b_gate, sink=v_sink,
                w_spatial=v_w_spatial, b_spatial=v_b_spatial, g_v=v_g_v, w_sconv=v_w_sconv, w_branch=v_w_branch,
                w_out=v_w_out, g_ffn=v_g_ffn, w_up=v_w_up, w_fconv=v_w_fconv, w_down=v_w_down, g_final=v_g_final)
    big_names = [n for n, _, _ in _BIG]

    conv_pack = _flat_pack([w_sconv, w_fconv], LANES)
    conv_all = _allgather8(conv_pack, name="gather_conv_weights")
    conv_parts = [_flat_unpack(conv_all[2 * p], [w_sconv, w_fconv]) for p in range(N_CHIPS)]
    w_sconv_full = jnp.concatenate([cp[0] for cp in conv_parts], axis=-1)
    w_fconv_full = jnp.concatenate([cp[1] for cp in conv_parts], axis=-1)

    small = dict(b_mod=b_mod, g_mix=g_mix, b_gate=b_gate, sink=sink, w_spatial=w_spatial, b_spatial=b_spatial, g_v=g_v,
                 w_sconv=w_sconv_full, g_ffn=g_ffn, w_fconv=w_fconv_full, g_final=g_final)
    x_tok = jnp.concatenate([x[0], ctx[0]], axis=0)
    step = _step_context(c[0], c_ctx, n_lat, n_ctx)
    layer_small = [{k: v[l] for k, v in small.items() if k != "g_final"} for l in range(DEPTH)]
    my_half = lax.axis_index("c").astype(jnp.int32).reshape(1)

    shards = {n: weights[n].astype(BF16) for n in big_names}
    pack = [_pack_shards(shards, l) for l in range(DEPTH)]
    w0 = _unpack_full(_gather_finish(_gather_ici(pack[0], name="gather_ici0"), pack[0], name="gather_finish0"))
    in_attn, in_ffn = (0, 1, 3), (2,)
    xs, saved0, (part_attn, part_ffn) = _layer_fwd(
        0, x_tok, w0, layer_small[0], step, hosted=_gather_ici_hosted([pack[1][g] for g in in_attn]),
        hosted_ffn=_gather_ici_hosted([pack[1][g] for g in in_ffn]))
    partial1 = [None] * len(pack[1])
    for g, part in zip(in_attn + in_ffn, list(part_attn) + list(part_ffn)):
        partial1[g] = part
    w1 = _unpack_full(_gather_finish(partial1, pack[1], name="gather_finish1"))
    xs, saved1, _ = _layer_fwd(1, xs, w1, layer_small[1], step)
    dx, sq_err, d_g_final = _loss_bwd(xs, loss_target[0], g_final, n_lat)
    loss = lax.psum(0.5 * jnp.sum(sq_err) / D, ("x", "y", "c"))

    def reduce_start(g_packs, tag):
        got = _grad_sibling_swap(g_packs, name=f"grad_sibling_swap{tag}")
        return [_add_half(my_half, a, b, name=f"grad_pair_sum{tag}_{g}") for g, (a, b) in enumerate(zip(g_packs, got))]

    def reduce_finish(exchanged, tag):
        sums = [_sum_slabs(e, F32, name=f"grad_chip_sum{tag}_{g}") for g, e in enumerate(exchanged)]
        return _grad_sibling_share(sums, name=f"grad_sibling_share{tag}")

    dx, wgrads1, lgrads1, d_a1, _ = _layer_bwd(1, saved1, w1, dx, step)
    pair_sum1 = reduce_start(_pack_grads(wgrads1), "1")

    def carried_by_attn_bwd0(early):
        pair_sum0_early = reduce_start([_pack_grad_cols(early["w_up"]), _pack_grad_rows(early)], "0_early")
        return _grad_ici_hosted(pair_sum1 + pair_sum0_early)

    dx, wgrads0, lgrads0, d_a0, exchanged = _layer_bwd(0, saved0, w0, dx, step, make_hosted=carried_by_attn_bwd0)
    total1 = reduce_finish(exchanged[:4], "1")
    total0_early = reduce_finish(exchanged[4:], "0_early")
    late = reduce_start([_pack_grad_cols(wgrads0["w_mod"]), _pack_grad_cols(wgrads0["w_in"])], "0_late")
    total0_late = reduce_finish(_grad_ici(late, name="grad_chip_exchange0_late"), "0_late")
    big_grads = _unpack_shards([list(total0_late) + list(total0_early), total1], {n: weights[n] for n in big_names})
    sgrads = _small_grads([lgrads0, lgrads1], [d_a0, d_a1], d_g_final, step)
    grad_x = dx[:n_lat][None]

    s_likes = [sgrads[n] for n in _SMALL_ORDER]
    s_all = _allgather8(_flat_pack(s_likes, D), name="gather_small_grads")
    s_tot = _flat_unpack(_sum_slabs(s_all, F32, name="small_grad_sum"), s_likes)
    grads = dict(big_grads)
    for n, g in zip(_SMALL_ORDER, s_tot):
        grads[n] = g
    grads["w_sconv"] = lax.dynamic_slice_in_dim(grads["w_sconv"], chip * w_sconv.shape[-1], w_sconv.shape[-1], axis=2)
    grads["w_fconv"] = lax.dynamic_slice_in_dim(grads["w_fconv"], chip * w_fconv.shape[-1], w_fconv.shape[-1], axis=2)

    delta, new_m, new_v = {}, {}, {}
    for n in big_names:
        cols = weights[n].shape[-1]
        view = lambda a: a.reshape(-1, cols)
        d_, m_, v_ = _adamw(view(weights[n]), view(grads[n]), view(m_in[n]), view(v_in[n]), name=f"adamw_{n}")
        delta[n], new_m[n], new_v[n] = (t.reshape(weights[n].shape) for t in (d_, m_, v_))
    likes = [weights[n] for n in _SMALL_ORDER]
    packs = [_flat_pack([src[n] for n in _SMALL_ORDER], D) for src in (weights, grads, m_in, v_in)]
    outs = _adamw(*packs, name="adamw_small")
    for dst, packed in zip((delta, new_m, new_v), outs):
        for n, val in zip(_SMALL_ORDER, _flat_unpack(packed, likes)):
            dst[n] = val

    order = ("c_ctx", "w_mod", "b_mod", "g_mix", "w_in", "b_gate", "sink", "w_spatial", "b_spatial", "g_v", "w_sconv",
             "w_branch", "w_out", "g_ffn", "w_up", "w_fconv", "w_down", "g_final")
    return (loss, grad_x, *[grads[n] for n in order], *[delta[n] for n in order], *[new_m[n] for n in order],
            *[new_v[n] for n in order])
```

```python
import functools
import math

import jax
import jax.numpy as jnp
from jax import lax
from jax.experimental import pallas as pl
from jax.experimental.pallas import tpu as pltpu

F32 = jnp.float32
BF16 = jnp.bfloat16

D = 1024
DEPTH = 2
GRID_W = 64
N_HEADS = 16
N_KV = 4
GRP = N_HEADS // N_KV
HEAD_DIM = 64
KV_W = N_KV * HEAD_DIM
WINDOW = 128
BLK = 128
ROPE_THETA = 10000.0
A_GROUPS = 8
D_FF = 2816
EPS = 1e-6
NEG = -1e30
QKV_W = D + 2 * KV_W
A_COLS = 2 * D
B_COLS = 3 * D
G_COLS = 3 * D
IN_W = QKV_W + A_COLS + B_COLS + G_COLS
SEG = (0, QKV_W, QKV_W + A_COLS, QKV_W + A_COLS + B_COLS, IN_W)
N_CHIPS = 4
N_DEV = 8
LANES = 128
SUBLANES = 8
VMEM_LIMIT = 48 * 1024 * 1024
ADAM_LR = 0.001
ADAM_B1 = 0.9
ADAM_B2 = 0.999
ADAM_EPS = 1e-08
ADAM_WD = 0.01
ADAM_STEP = 10
MESH = pl.DeviceIdType.MESH
ANY = pl.BlockSpec(memory_space=pl.ANY)


def _params(sem=None):
    return pltpu.CompilerParams(dimension_semantics=sem, vmem_limit_bytes=VMEM_LIMIT)


def _pick(n, cands):
    for c in cands:
        if n % c == 0:
            return c
    return n


def _rows8(rows, width):
    r = lax.broadcasted_iota(jnp.int32, (SUBLANES, width), 0)
    out = jnp.zeros((SUBLANES, width), F32)
    for idx, v in rows:
        out = out + jnp.where(r == idx, v, 0.0)
    return out


def _sel(mod_ref, k, is_ctx):
    return jnp.where(is_ctx, mod_ref[1:2, k * D:(k + 1) * D], mod_ref[0:1, k * D:(k + 1) * D])


def _colsum(v):
    return jnp.sum(v, axis=0, keepdims=True)


def _mm(a, b, *, name, ta=False, tb=False, out_dtype=F32):
    if ta:
        k_dim, m = a.shape
    else:
        m, k_dim = a.shape
    if tb:
        n, kb = b.shape
    else:
        kb, n = b.shape
    assert k_dim == kb, (a.shape, b.shape, ta, tb)
    tm = _pick(m, (1056, 1024, 1408, 768, 512, 256, 128))
    tn = _pick(n, (1536, 1408, 1024, 768, 512, 256, 128))
    tk = _pick(k_dim, (2048, 1536, 1408, 1024, 768, 512, 256, 128))
    nk = k_dim // tk
    dims = (((0 if ta else 1,), (1 if tb else 0,)), ((), ()))

    def product(a_ref, b_ref):
        return lax.dot_general(a_ref[...].astype(BF16), b_ref[...].astype(BF16), dims, preferred_element_type=F32)

    def body_single(a_ref, b_ref, o_ref):
        o_ref[...] = product(a_ref, b_ref).astype(o_ref.dtype)

    def body_acc(a_ref, b_ref, o_ref, acc_ref):
        k = pl.program_id(2)

        @pl.when(k == 0)
        def _():
            acc_ref[...] = product(a_ref, b_ref)

        @pl.when(k > 0)
        def _():
            acc_ref[...] += product(a_ref, b_ref)

        @pl.when(k == nk - 1)
        def _():
            o_ref[...] = acc_ref[...].astype(o_ref.dtype)

    a_spec = pl.BlockSpec((tk, tm), lambda i, j, k: (k, i)) if ta else pl.BlockSpec((tm, tk), lambda i, j, k: (i, k))
    b_spec = pl.BlockSpec((tn, tk), lambda i, j, k: (j, k)) if tb else pl.BlockSpec((tk, tn), lambda i, j, k: (k, j))
    return pl.pallas_call(
        body_single if nk == 1 else body_acc, name=name, grid=(m // tm, n // tn, nk),
        in_specs=[a_spec, b_spec], out_specs=pl.BlockSpec((tm, tn), lambda i, j, k: (i, j)),
        out_shape=jax.ShapeDtypeStruct((m, n), out_dtype),
        scratch_shapes=[] if nk == 1 else [pltpu.VMEM((tm, tn), F32)],
        compiler_params=_params(("parallel", "parallel", "arbitrary")),
    )(a, b)


def _small(fn, out_shape, *arrays, name):
    def body(*refs):
        refs[-1][...] = fn(*[r[...] for r in refs[:-1]]).astype(refs[-1].dtype)

    return pl.pallas_call(body, name=name, out_shape=jax.ShapeDtypeStruct(out_shape, F32))(*arrays)


def _silu(v):
    return v * jax.nn.sigmoid(v)


def _dsilu(v):
    s = jax.nn.sigmoid(v)
    return s * (1.0 + v * (1.0 - s))


def _row_spec(tm, width, col=0):
    return pl.BlockSpec((tm, width), lambda i: (i, col))


def _full_spec(shape):
    nd = len(shape)
    return pl.BlockSpec(shape, lambda i: (0,) * nd)


def _halo_specs(tm, width, t_rows, col=0):
    per = tm // SUBLANES
    last = t_rows // SUBLANES - 1
    prev = pl.BlockSpec((SUBLANES, width), lambda i: (jnp.maximum(i * per - 1, 0), col))
    nxt = pl.BlockSpec((SUBLANES, width), lambda i: (jnp.minimum((i + 1) * per, last), col))
    return prev, nxt


def _shift_rows(cur, prev8, next8, n_lat, t_rows, tm):
    i = pl.program_id(0)
    row = lax.broadcasted_iota(jnp.int32, (tm, 1), 0)
    g = row + i * tm
    up = pltpu.roll(cur, 1, 0)
    up = jnp.where(row == 0, prev8[SUBLANES - 1:SUBLANES, :], up)
    up = jnp.where((g == 0) | (g == n_lat), 0.0, up)
    dn = pltpu.roll(cur, tm - 1, 0)
    dn = jnp.where(row == tm - 1, next8[0:1, :], dn)
    dn = jnp.where((g == n_lat - 1) | (g == t_rows - 1), 0.0, dn)
    return up, dn


def _norm_mod_fwd(x, g, mod8, sh_idx, sc_idx, n_lat, *, name):
    t_rows = x.shape[0]
    tm = 256

    def body(x_ref, g_ref, mod_ref, o_ref):
        is_ctx = pl.program_id(0) * tm >= n_lat
        xv = x_ref[...]
        rstd = lax.rsqrt(jnp.mean(xv * xv, axis=-1, keepdims=True) + EPS)
        y = xv * rstd * g_ref[...]
        o_ref[...] = (y * (1.0 + _sel(mod_ref, sc_idx, is_ctx)) + _sel(mod_ref, sh_idx, is_ctx)).astype(BF16)

    return pl.pallas_call(
        body, name=name, grid=(t_rows // tm,),
        in_specs=[_row_spec(tm, D), _full_spec((1, D)), _full_spec((SUBLANES, 6 * D))],
        out_specs=_row_spec(tm, D), out_shape=jax.ShapeDtypeStruct((t_rows, D), BF16),
        compiler_params=_params(("parallel",)),
    )(x, g, mod8)


def _norm_mod_bwd(x, dh_parts, dres, g, mod8, sc_idx, n_lat, *, name):
    t_rows = x.shape[0]
    tm = 256
    n_parts = len(dh_parts)

    def body(*refs):
        x_ref, dres_ref, g_ref, mod_ref = refs[:4]
        part_refs = refs[4:4 + n_parts]
        dx_ref, st_ref = refs[4 + n_parts:]
        i = pl.program_id(0)
        is_ctx = i * tm >= n_lat
        dh = part_refs[0][...]
        for p in part_refs[1:]:
            dh = dh + p[...]
        xv = x_ref[...]
        gv = g_ref[...]
        rstd = lax.rsqrt(jnp.mean(xv * xv, axis=-1, keepdims=True) + EPS)
        rn = xv * rstd
        dy = dh * (1.0 + _sel(mod_ref, sc_idx, is_ctx))
        e = dy * gv
        dx_ref[...] = dres_ref[...] + rstd * (e - rn * jnp.mean(e * rn, axis=-1, keepdims=True))
        dsh = _colsum(dh)
        dsc = _colsum(dh * (rn * gv))
        dg = _colsum(dy * rn)
        zero = jnp.zeros_like(dsh)
        upd = _rows8([(0, jnp.where(is_ctx, zero, dsh)), (1, jnp.where(is_ctx, dsh, zero)),
                      (2, jnp.where(is_ctx, zero, dsc)), (3, jnp.where(is_ctx, dsc, zero)), (4, dg)], D)

        @pl.when(i == 0)
        def _():
            st_ref[...] = upd

        @pl.when(i > 0)
        def _():
            st_ref[...] += upd

    return pl.pallas_call(
        body, name=name, grid=(t_rows // tm,),
        in_specs=[_row_spec(tm, D), _row_spec(tm, D), _full_spec((1, D)), _full_spec((SUBLANES, 6 * D))]
        + [_row_spec(tm, D)] * n_parts,
        out_specs=[_row_spec(tm, D), _full_spec((SUBLANES, D))],
        out_shape=[jax.ShapeDtypeStruct((t_rows, D), F32), jax.ShapeDtypeStruct((SUBLANES, D), F32)],
        compiler_params=_params(("arbitrary",)),
    )(x, dres, g, mod8, *dh_parts)


def _residual_fwd(x, branch, mod8, gt_idx, n_lat, *, name):
    t_rows = x.shape[0]
    tm = 256

    def body(x_ref, b_ref, mod_ref, o_ref):
        is_ctx = pl.program_id(0) * tm >= n_lat
        o_ref[...] = x_ref[...] + _sel(mod_ref, gt_idx, is_ctx) * b_ref[...]

    return pl.pallas_call(
        body, name=name, grid=(t_rows // tm,),
        in_specs=[_row_spec(tm, D), _row_spec(tm, D), _full_spec((SUBLANES, 6 * D))],
        out_specs=_row_spec(tm, D), out_shape=jax.ShapeDtypeStruct((t_rows, D), F32),
        compiler_params=_params(("parallel",)),
    )(x, branch, mod8)


def _residual_bwd(dx, branch, mod8, gt_idx, n_lat, *, name):
    t_rows = dx.shape[0]
    tm = 256

    def body(dx_ref, b_ref, mod_ref, o_ref, st_ref):
        i = pl.program_id(0)
        is_ctx = i * tm >= n_lat
        dxv = dx_ref[...]
        o_ref[...] = (dxv * _sel(mod_ref, gt_idx, is_ctx)).astype(BF16)
        dgt = _colsum(dxv * b_ref[...])
        zero = jnp.zeros_like(dgt)
        upd = _rows8([(0, jnp.where(is_ctx, zero, dgt)), (1, jnp.where(is_ctx, dgt, zero))], D)

        @pl.when(i == 0)
        def _():
            st_ref[...] = upd

        @pl.when(i > 0)
        def _():
            st_ref[...] += upd

    return pl.pallas_call(
        body, name=name, grid=(t_rows // tm,),
        in_specs=[_row_spec(tm, D), _row_spec(tm, D), _full_spec((SUBLANES, 6 * D))],
        out_specs=[_row_spec(tm, D), _full_spec((SUBLANES, D))],
        out_shape=[jax.ShapeDtypeStruct((t_rows, D), BF16), jax.ShapeDtypeStruct((SUBLANES, D), F32)],
        compiler_params=_params(("arbitrary",)),
    )(dx, branch, mod8)


def _rope_tables(n_lat, n_ctx):
    rows = n_lat // GRID_W
    row = jnp.broadcast_to(jnp.arange(rows, dtype=F32)[:, None], (rows, GRID_W)).reshape(n_lat)
    col = jnp.broadcast_to(jnp.arange(GRID_W, dtype=F32)[None, :], (rows, GRID_W)).reshape(n_lat)
    half = HEAD_DIM // 2
    inv = ROPE_THETA ** (-jnp.arange(0, half, 2, dtype=F32) / half)
    ang = jnp.concatenate([row[:, None] * inv, col[:, None] * inv], axis=-1)
    cos, sin = jnp.cos(ang), jnp.sin(ang)
    c64 = jnp.concatenate([cos, cos], axis=-1)
    s64 = jnp.concatenate([-sin, sin], axis=-1)
    c64 = jnp.concatenate([c64, jnp.ones((n_ctx, HEAD_DIM), F32)], axis=0)
    s64 = jnp.concatenate([s64, jnp.zeros((n_ctx, HEAD_DIM), F32)], axis=0)
    return jnp.tile(c64, (1, 2)), jnp.tile(s64, (1, 2))


def _swap_halves(v):
    lane = lax.broadcasted_iota(jnp.int32, v.shape, 1)
    return jnp.where(lane % HEAD_DIM < HEAD_DIM // 2, pltpu.roll(v, LANES - HEAD_DIM // 2, 1),
                     pltpu.roll(v, HEAD_DIM // 2, 1))


def _low_half(shape):
    return lax.broadcasted_iota(jnp.int32, shape, 1) < HEAD_DIM


def _qkv_prep(z_qkv, cos_t, sin_t, *, name):
    t_rows = z_qkv.shape[0]
    tm = 256

    def body(z_ref, c_ref, s_ref, q_ref, k_ref, v_ref):
        cv, sv = c_ref[...], s_ref[...]

        def rope(chunk):
            return chunk * cv + _swap_halves(chunk) * sv

        for ch in range(D // LANES):
            roped = rope(z_ref[:, ch * LANES:(ch + 1) * LANES])
            q_ref[:, ch * LANES:(ch + 1) * LANES] = (roped * (HEAD_DIM ** -0.5)).astype(BF16)
        low = _low_half((tm, LANES))
        for pair in range(N_KV // 2):
            for which, ref, roped in ((0, k_ref, True), (1, v_ref, False)):
                off = D + which * KV_W + pair * LANES
                chunk = z_ref[:, off:off + LANES]
                if roped:
                    chunk = rope(chunk)
                other = pltpu.roll(chunk, HEAD_DIM, 1)
                even = jnp.where(low, chunk, other)
                odd = jnp.where(low, other, chunk)
                ref[:, (2 * pair) * LANES:(2 * pair + 1) * LANES] = even.astype(BF16)
                ref[:, (2 * pair + 1) * LANES:(2 * pair + 2) * LANES] = odd.astype(BF16)

    dup_w = N_KV * LANES
    return pl.pallas_call(
        body, name=name, grid=(t_rows // tm,),
        in_specs=[_row_spec(tm, QKV_W), _row_spec(tm, LANES), _row_spec(tm, LANES)],
        out_specs=[_row_spec(tm, D), _row_spec(tm, dup_w), _row_spec(tm, dup_w)],
        out_shape=[jax.ShapeDtypeStruct((t_rows, D), BF16), jax.ShapeDtypeStruct((t_rows, dup_w), BF16),
                   jax.ShapeDtypeStruct((t_rows, dup_w), BF16)],
        compiler_params=_params(("parallel",)),
    )(z_qkv, cos_t, sin_t)


def _qkv_unprep(dq, dk, dv, cos_t, sin_t, *, name):
    t_rows = dq.shape[0]
    tm = 256

    def body(dq_ref, dk_ref, dv_ref, c_ref, s_ref, o_ref):
        cv, sv = c_ref[...], s_ref[...]

        def unrope(chunk):
            return chunk * cv + _swap_halves(chunk * sv)

        for ch in range(D // LANES):
            o_ref[:, ch * LANES:(ch + 1) * LANES] = unrope(dq_ref[:, ch * LANES:(ch + 1) * LANES]).astype(BF16)
        for pair in range(N_KV // 2):
            for which, ref, roped in ((0, dk_ref, True), (1, dv_ref, False)):
                chunk = ref[:, pair * LANES:(pair + 1) * LANES]
                if roped:
                    chunk = unrope(chunk)
                off = D + which * KV_W + pair * LANES
                o_ref[:, off:off + LANES] = chunk.astype(BF16)

    return pl.pallas_call(
        body, name=name, grid=(t_rows // tm,),
        in_specs=[_row_spec(tm, D), _row_spec(tm, KV_W), _row_spec(tm, KV_W), _row_spec(tm, LANES),
                  _row_spec(tm, LANES)],
        out_specs=_row_spec(tm, QKV_W), out_shape=jax.ShapeDtypeStruct((t_rows, QKV_W), BF16),
        compiler_params=_params(("parallel",)),
    )(dq, dk, dv, cos_t, sin_t)


def _attn_specs(n_lat, n_ctx):
    nb = n_lat // BLK
    dup_w = N_KV * LANES

    def ws(j):
        return jnp.clip(j - 1, 0, nb - 3)

    win = [pl.BlockSpec((BLK, dup_w), functools.partial(lambda j, o: (ws(j) + o, 0), o=o)) for o in range(3)]
    ctx = pl.BlockSpec((n_ctx, dup_w), lambda j: (n_lat // n_ctx, 0))
    return nb, ws, win, ctx


def _attn_bias(j, ws_j, nb, n_ctx):
    n_keys = 3 * BLK + n_ctx
    row = lax.broadcasted_iota(jnp.int32, (BLK, n_keys), 0)
    col = lax.broadcasted_iota(jnp.int32, (BLK, n_keys), 1)
    rel = (ws_j - j) * BLK + col - row
    valid = (col >= 3 * BLK) | ((jnp.abs(rel) <= WINDOW) & (j < nb))
    bias = jnp.where(valid, 0.0, NEG)
    return jnp.concatenate([bias] * GRP, axis=0)


def _attn_probs(q_ref, kk, kh, bias, sink_ref):
    low = _low_half((BLK, LANES))
    qs = []
    for g in range(GRP):
        h = GRP * kh + g
        chunk = q_ref[:, (h // 2) * LANES:(h // 2 + 1) * LANES]
        qs.append(jnp.where(low if h % 2 == 0 else ~low, chunk, jnp.zeros_like(chunk)))
    qs = jnp.concatenate(qs, axis=0)
    s = lax.dot_general(qs, kk, (((1,), (1,)), ((), ())), preferred_element_type=F32) + bias
    snk = jnp.concatenate(
        [jnp.broadcast_to(jnp.max(sink_ref[GRP * kh + g:GRP * kh + g + 1, :], axis=1, keepdims=True), (BLK, 1))
         for g in range(GRP)], axis=0)
    m = jnp.maximum(jnp.max(s, axis=-1, keepdims=True), snk)
    p = jnp.exp(s - m)
    p_snk = jnp.exp(snk - m)
    inv = 1.0 / (jnp.sum(p, axis=-1, keepdims=True) + p_snk)
    return qs, p, p_snk, inv


class _Hosted:
    def __init__(self, arrays, out_shapes, scratch, start, finish):
        self.arrays, self.out_shapes, self.scratch, self.start, self.finish = arrays, out_shapes, scratch, start, finish


_NO_EXCHANGE = _Hosted([], [], [], None, None)


def _split_refs(refs, n_in, n_out, n_scratch, hosted):
    hi, ho, hs = len(hosted.arrays), len(hosted.out_shapes), len(hosted.scratch)
    a = n_in + hi
    b = a + n_out + ho
    ins, h_ins = refs[:n_in], refs[n_in:a]
    outs, h_outs = refs[a:a + n_out], refs[a + n_out:b]
    scr, h_scr = refs[b:b + n_scratch], refs[b + n_scratch:b + n_scratch + hs]
    return ins, outs, scr, (h_ins, h_outs, h_scr)


def _run_hosted(hosted, h_refs, step, n_steps):
    if hosted.start is None:
        return

    flat = [r for group in h_refs for r in group]

    @pl.when(step == 0)
    def _():
        hosted.start(*flat)

    @pl.when(step == n_steps - 1)
    def _():
        hosted.finish(*flat)


def _attention_fwd(q, kd, vd, sink_b, n_lat, n_ctx, *, name, hosted=_NO_EXCHANGE):
    t_rows = q.shape[0]
    nb, ws, win, ctx = _attn_specs(n_lat, n_ctx)
    n_steps = t_rows // BLK

    def body(*refs):
        ins, outs, _, h_refs = _split_refs(refs, 10, 1, 0, hosted)
        q_ref, k0, k1, k2, kc, v0, v1, v2, vc, sink_ref = ins
        o_ref, = outs
        j = pl.program_id(0)
        _run_hosted(hosted, h_refs, j, n_steps)
        ws_j = ws(j)
        low = _low_half((BLK, LANES))
        bias = _attn_bias(j, ws_j, nb, n_ctx)
        for kh in range(N_KV):
            sl = slice(kh * LANES, (kh + 1) * LANES)
            kk = jnp.concatenate([k0[:, sl], k1[:, sl], k2[:, sl], kc[:, sl]], axis=0)
            vv = jnp.concatenate([v0[:, sl], v1[:, sl], v2[:, sl], vc[:, sl]], axis=0)
            _, p, _, inv = _attn_probs(q_ref, kk, kh, bias, sink_ref)
            o = jnp.dot(p.astype(BF16), vv, preferred_element_type=F32) * inv
            for half in range(2):
                even = o[(2 * half) * BLK:(2 * half + 1) * BLK]
                odd = o[(2 * half + 1) * BLK:(2 * half + 2) * BLK]
                ch = 2 * kh + half
                o_ref[:, ch * LANES:(ch + 1) * LANES] = jnp.where(low, even, odd).astype(BF16)

    outs = pl.pallas_call(
        body, name=name, grid=(n_steps,),
        in_specs=[_row_spec(BLK, D)] + win + [ctx] + win + [ctx] + [_full_spec((N_HEADS, LANES))]
        + [ANY] * len(hosted.arrays),
        out_specs=[_row_spec(BLK, D)] + [ANY] * len(hosted.out_shapes),
        out_shape=[jax.ShapeDtypeStruct((t_rows, D), BF16)] + list(hosted.out_shapes),
        scratch_shapes=list(hosted.scratch),
        compiler_params=_params(("arbitrary",)),
    )(q, kd, kd, kd, kd, vd, vd, vd, vd, sink_b, *hosted.arrays)
    return outs[0], outs[1:]


def _attention_bwd(q, kd, vd, sink_b, dy, n_lat, n_ctx, *, name, hosted=_NO_EXCHANGE):
    t_rows = q.shape[0]
    nb, ws, win, ctx = _attn_specs(n_lat, n_ctx)
    n_steps = t_rows // BLK

    def body(*refs):
        ins, outs, scr, h_refs = _split_refs(refs, 11, 4, 3, hosted)
        q_ref, k0, k1, k2, kc, v0, v1, v2, vc, sink_ref, dy_ref = ins
        dq_ref, dk_hbm, dv_hbm, ds_ref = outs
        dk_acc, dv_acc, sem = scr
        j = pl.program_id(0)
        _run_hosted(hosted, h_refs, j, n_steps)
        ws_j = ws(j)

        @pl.when(j == 0)
        def _():
            dk_acc[...] = jnp.zeros_like(dk_acc)
            dv_acc[...] = jnp.zeros_like(dv_acc)
            ds_ref[...] = jnp.zeros_like(ds_ref)

        low = _low_half((BLK, LANES))
        low_keys = _low_half((3 * BLK + n_ctx, LANES))
        win_start = pl.multiple_of(ws_j * BLK, BLK)
        scale = HEAD_DIM ** -0.5
        dk_heads, dv_heads = [], []
        bias = _attn_bias(j, ws_j, nb, n_ctx)
        for kh in range(N_KV):
            sl = slice(kh * LANES, (kh + 1) * LANES)
            kk = jnp.concatenate([k0[:, sl], k1[:, sl], k2[:, sl], kc[:, sl]], axis=0)
            vv = jnp.concatenate([v0[:, sl], v1[:, sl], v2[:, sl], vc[:, sl]], axis=0)
            qs, p, p_snk, inv = _attn_probs(q_ref, kk, kh, bias, sink_ref)
            dos = []
            for g in range(GRP):
                h = GRP * kh + g
                chunk = dy_ref[:, (h // 2) * LANES:(h // 2 + 1) * LANES]
                dos.append(jnp.where(low if h % 2 == 0 else ~low, chunk, jnp.zeros_like(chunk)).astype(BF16))
            dos = jnp.concatenate(dos, axis=0)
            dp = lax.dot_general(dos, vv, (((1,), (1,)), ((), ())), preferred_element_type=F32)
            dsum = jnp.sum(p * dp, axis=-1, keepdims=True) * inv
            ds = (p * ((dp - dsum) * inv)).astype(BF16)
            snk_term = p_snk * inv * dsum
            for g in range(GRP):
                contrib = -jnp.sum(snk_term[g * BLK:(g + 1) * BLK], axis=0, keepdims=True)
                ds_ref[GRP * kh + g:GRP * kh + g + 1, :] += jnp.broadcast_to(contrib, (1, LANES))
            dqs = jnp.dot(ds, kk, preferred_element_type=F32) * scale
            for half in range(2):
                even = dqs[(2 * half) * BLK:(2 * half + 1) * BLK]
                odd = dqs[(2 * half + 1) * BLK:(2 * half + 2) * BLK]
                ch = 2 * kh + half
                dq_ref[:, ch * LANES:(ch + 1) * LANES] = jnp.where(low, even, odd)
            dkk = lax.dot_general(ds, qs, (((0,), (0,)), ((), ())), preferred_element_type=F32)
            dvv = lax.dot_general((p * inv).astype(BF16), dos, (((0,), (0,)), ((), ())), preferred_element_type=F32)
            dk_heads.append(dkk + pltpu.roll(dkk, HEAD_DIM, 1))
            dv_heads.append(dvv + pltpu.roll(dvv, HEAD_DIM, 1))
        for pair in range(N_KV // 2):
            sl = slice(pair * LANES, (pair + 1) * LANES)
            for acc, heads in ((dk_acc, dk_heads), (dv_acc, dv_heads)):
                both = jnp.where(low_keys, heads[2 * pair], heads[2 * pair + 1])
                acc[pl.ds(win_start, 3 * BLK), sl] += both[:3 * BLK]
                acc[n_lat:n_lat + n_ctx, sl] += both[3 * BLK:]

        @pl.when(j == n_steps - 1)
        def _():
            ck = pltpu.make_async_copy(dk_acc, dk_hbm, sem.at[0])
            cv = pltpu.make_async_copy(dv_acc, dv_hbm, sem.at[1])
            ck.start()
            cv.start()
            ck.wait()
            cv.wait()

    outs = pl.pallas_call(
        body, name=name, grid=(n_steps,),
        in_specs=[_row_spec(BLK, D)] + win + [ctx] + win + [ctx] + [_full_spec((N_HEADS, LANES)), _row_spec(BLK, D)]
        + [ANY] * len(hosted.arrays),
        out_specs=[_row_spec(BLK, D), ANY, ANY, _full_spec((N_HEADS, LANES))] + [ANY] * len(hosted.out_shapes),
        out_shape=[jax.ShapeDtypeStruct((t_rows, D), F32), jax.ShapeDtypeStruct((t_rows, KV_W), F32),
                   jax.ShapeDtypeStruct((t_rows, KV_W), F32), jax.ShapeDtypeStruct((N_HEADS, LANES), F32)]
        + list(hosted.out_shapes),
        scratch_shapes=[pltpu.VMEM((t_rows, KV_W), F32), pltpu.VMEM((t_rows, KV_W), F32),
                        pltpu.SemaphoreType.DMA((2,))] + list(hosted.scratch),
        compiler_params=_params(("arbitrary",)),
    )(q, kd, kd, kd, kd, vd, vd, vd, vd, sink_b, dy, *hosted.arrays)
    return outs[0], outs[1], outs[2], outs[3], outs[4:]


_GELU_K = math.sqrt(2.0 / math.pi)


def _gelu(v):
    return jax.nn.gelu(v)


def _gelu_and_grad(v):
    t = jnp.tanh(_GELU_K * (v + 0.044715 * (v * v * v)))
    cdf = 0.5 * (1.0 + t)
    return v * cdf, cdf + 0.5 * v * (1.0 - t * t) * _GELU_K * (1.0 + 3.0 * 0.044715 * v * v)


def _gating_fwd(z_a, w_s, b_sb, g_v, *, name):
    t_rows = z_a.shape[0]

    def body(z_ref, w_ref, b_ref, g_ref, o_ref):
        u = _gelu(z_ref[:, :D])
        v = _gelu(z_ref[:, D:])
        vn = v * lax.rsqrt(jnp.mean(v * v, axis=-1, keepdims=True) + EPS) * g_ref[...]
        for g in range(A_GROUPS):
            sl = slice(g * LANES, (g + 1) * LANES)
            mixed = jnp.dot(w_ref[g].astype(BF16), vn[:, sl].astype(BF16), preferred_element_type=F32) + b_ref[g]
            o_ref[:, sl] = (u[:, sl] * mixed).astype(BF16)

    return pl.pallas_call(
        body, name=name, grid=(t_rows // BLK,),
        in_specs=[_row_spec(BLK, A_COLS), _full_spec((A_GROUPS, BLK, BLK)), _full_spec((A_GROUPS, BLK, LANES)),
                  _full_spec((1, D))],
        out_specs=_row_spec(BLK, D), out_shape=jax.ShapeDtypeStruct((t_rows, D), BF16),
        compiler_params=_params(("parallel",)),
    )(z_a, w_s, b_sb, g_v)


def _gating_bwd(z_a, dy, w_s, b_sb, g_v, *, name):
    t_rows = z_a.shape[0]

    def body(z_ref, dy_ref, w_ref, b_ref, g_ref, dz_ref, dw_ref, db_ref, st_ref):
        i = pl.program_id(0)

        @pl.when(i == 0)
        def _():
            dw_ref[...] = jnp.zeros_like(dw_ref)
            db_ref[...] = jnp.zeros_like(db_ref)
            st_ref[...] = jnp.zeros_like(st_ref)

        u, du_dz = _gelu_and_grad(z_ref[:, :D])
        v, dv_dz = _gelu_and_grad(z_ref[:, D:])
        gv = g_ref[...]
        rstd = lax.rsqrt(jnp.mean(v * v, axis=-1, keepdims=True) + EPS)
        vh = v * rstd
        vn = vh * gv
        dyv = dy_ref[...]
        dvn = []
        for g in range(A_GROUPS):
            sl = slice(g * LANES, (g + 1) * LANES)
            wg = w_ref[g].astype(BF16)
            vg = vn[:, sl].astype(BF16)
            mixed = jnp.dot(wg, vg, preferred_element_type=F32) + b_ref[g]
            dz_ref[:, sl] = (dyv[:, sl] * mixed * du_dz[:, sl]).astype(BF16)
            dmixed = dyv[:, sl] * u[:, sl]
            dmb = dmixed.astype(BF16)
            dvn.append(lax.dot_general(wg, dmb, (((0,), (0,)), ((), ())), preferred_element_type=F32))
            dw_ref[g] += lax.dot_general(dmb, vg, (((1,), (1,)), ((), ())), preferred_element_type=F32)
            db_ref[g] += jnp.broadcast_to(jnp.sum(dmixed, axis=-1, keepdims=True), (BLK, LANES))
        dvn = jnp.concatenate(dvn, axis=1)
        st_ref[...] += _rows8([(0, _colsum(dvn * vh))], D)
        e = dvn * gv
        dv = rstd * (e - vh * jnp.mean(e * vh, axis=-1, keepdims=True))
        dz_ref[:, D:] = (dv * dv_dz).astype(BF16)

    return pl.pallas_call(
        body, name=name, grid=(t_rows // BLK,),
        in_specs=[_row_spec(BLK, A_COLS), _row_spec(BLK, D), _full_spec((A_GROUPS, BLK, BLK)),
                  _full_spec((A_GROUPS, BLK, LANES)), _full_spec((1, D))],
        out_specs=[_row_spec(BLK, A_COLS), _full_spec((A_GROUPS, BLK, BLK)), _full_spec((A_GROUPS, BLK, LANES)),
                   _full_spec((SUBLANES, D))],
        out_shape=[jax.ShapeDtypeStruct((t_rows, A_COLS), BF16), jax.ShapeDtypeStruct((A_GROUPS, BLK, BLK), F32),
                   jax.ShapeDtypeStruct((A_GROUPS, BLK, LANES), F32), jax.ShapeDtypeStruct((SUBLANES, D), F32)],
        compiler_params=_params(("arbitrary",)),
    )(z_a, dy, w_s, b_sb, g_v)


def _sconv_fwd(z_b, w8, n_lat, *, name):
    t_rows = z_b.shape[0]
    tm = 256
    prev, nxt = _halo_specs(tm, B_COLS, t_rows)

    def body(z_ref, zp_ref, zn_ref, w_ref, o_ref):
        p = z_ref[:, D:2 * D] * z_ref[:, 2 * D:]
        pp = zp_ref[:, D:2 * D] * zp_ref[:, 2 * D:]
        pn = zn_ref[:, D:2 * D] * zn_ref[:, 2 * D:]
        up, dn = _shift_rows(p, pp, pn, n_lat, t_rows, tm)
        conv = w_ref[0:1, :] * up + w_ref[1:2, :] * p + w_ref[2:3, :] * dn
        o_ref[...] = (z_ref[:, :D] * conv).astype(BF16)

    return pl.pallas_call(
        body, name=name, grid=(t_rows // tm,),
        in_specs=[_row_spec(tm, B_COLS), prev, nxt, _full_spec((SUBLANES, D))],
        out_specs=_row_spec(tm, D), out_shape=jax.ShapeDtypeStruct((t_rows, D), BF16),
        compiler_params=_params(("parallel",)),
    )(z_b, z_b, z_b, w8)


def _sconv_bwd(z_b, dy, w8, n_lat, *, name):
    t_rows = z_b.shape[0]
    tm = 256
    prev, nxt = _halo_specs(tm, B_COLS, t_rows)
    dprev, dnxt = _halo_specs(tm, D, t_rows)

    def body(z_ref, zp_ref, zn_ref, dy_ref, dyp_ref, dyn_ref, w_ref, dz_ref, st_ref):
        i = pl.program_id(0)
        bg, cg, hb = z_ref[:, :D], z_ref[:, D:2 * D], z_ref[:, 2 * D:]
        p = cg * hb
        pp = zp_ref[:, D:2 * D] * zp_ref[:, 2 * D:]
        pn = zn_ref[:, D:2 * D] * zn_ref[:, 2 * D:]
        up, dn = _shift_rows(p, pp, pn, n_lat, t_rows, tm)
        w0, w1, w2 = w_ref[0:1, :], w_ref[1:2, :], w_ref[2:3, :]
        conv = w0 * up + w1 * p + w2 * dn
        dyv = dy_ref[...]
        dz_ref[:, :D] = (dyv * conv).astype(BF16)
        dcv = dyv * bg
        dcv_up, dcv_dn = _shift_rows(dcv, dyp_ref[...] * zp_ref[:, :D], dyn_ref[...] * zn_ref[:, :D], n_lat, t_rows, tm)
        dp = w0 * dcv_dn + w1 * dcv + w2 * dcv_up
        dz_ref[:, D:2 * D] = (dp * hb).astype(BF16)
        dz_ref[:, 2 * D:] = (dp * cg).astype(BF16)
        upd = _rows8([(0, _colsum(dcv * up)), (1, _colsum(dcv * p)), (2, _colsum(dcv * dn))], D)

        @pl.when(i == 0)
        def _():
            st_ref[...] = upd

        @pl.when(i > 0)
        def _():
            st_ref[...] += upd

    return pl.pallas_call(
        body, name=name, grid=(t_rows // tm,),
        in_specs=[_row_spec(tm, B_COLS), prev, nxt, _row_spec(tm, D), dprev, dnxt, _full_spec((SUBLANES, D))],
        out_specs=[_row_spec(tm, B_COLS), _full_spec((SUBLANES, D))],
        out_shape=[jax.ShapeDtypeStruct((t_rows, B_COLS), BF16), jax.ShapeDtypeStruct((SUBLANES, D), F32)],
        compiler_params=_params(("arbitrary",)),
    )(z_b, z_b, z_b, dy, dy, dy, w8)


def _merge_fwd(t0, t1, t2, z_g, b_gate, *, name):
    t_rows = t0.shape[0]
    tm = 256

    def body(t0_ref, t1_ref, t2_ref, z_ref, b_ref, o_ref):
        acc = None
        for k, t_ref in enumerate((t0_ref, t1_ref, t2_ref)):
            gate = jax.nn.sigmoid(z_ref[:, k * D:(k + 1) * D] + b_ref[:, k * D:(k + 1) * D])
            term = gate * t_ref[...]
            acc = term if acc is None else acc + term
        o_ref[...] = acc.astype(BF16)

    return pl.pallas_call(
        body, name=name, grid=(t_rows // tm,),
        in_specs=[_row_spec(tm, D)] * 3 + [_row_spec(tm, G_COLS), _full_spec((1, G_COLS))],
        out_specs=_row_spec(tm, D), out_shape=jax.ShapeDtypeStruct((t_rows, D), BF16),
        compiler_params=_params(("parallel",)),
    )(t0, t1, t2, z_g, b_gate)


def _merge_bwd(dmerged, t0, t1, t2, z_g, b_gate, *, name):
    t_rows = t0.shape[0]
    tm = 256

    def body(dm_ref, t0_ref, t1_ref, t2_ref, z_ref, b_ref, d0_ref, d1_ref, d2_ref, dz_ref, st_ref):
        i = pl.program_id(0)
        dm = dm_ref[...]
        sums = []
        for k, (t_ref, d_ref) in enumerate(((t0_ref, d0_ref), (t1_ref, d1_ref), (t2_ref, d2_ref))):
            gate = jax.nn.sigmoid(z_ref[:, k * D:(k + 1) * D] + b_ref[:, k * D:(k + 1) * D])
            d_ref[...] = (dm * gate).astype(BF16)
            dzg = dm * t_ref[...] * gate * (1.0 - gate)
            dz_ref[:, k * D:(k + 1) * D] = dzg.astype(BF16)
            sums.append(_colsum(dzg))
        upd = _rows8([(0, jnp.concatenate(sums, axis=1))], G_COLS)

        @pl.when(i == 0)
        def _():
            st_ref[...] = upd

        @pl.when(i > 0)
        def _():
            st_ref[...] += upd

    return pl.pallas_call(
        body, name=name, grid=(t_rows // tm,),
        in_specs=[_row_spec(tm, D)] * 4 + [_row_spec(tm, G_COLS), _full_spec((1, G_COLS))],
        out_specs=[_row_spec(tm, D)] * 3 + [_row_spec(tm, G_COLS), _full_spec((SUBLANES, G_COLS))],
        out_shape=[jax.ShapeDtypeStruct((t_rows, D), BF16)] * 3
        + [jax.ShapeDtypeStruct((t_rows, G_COLS), BF16), jax.ShapeDtypeStruct((SUBLANES, G_COLS), F32)],
        compiler_params=_params(("arbitrary",)),
    )(dmerged, t0, t1, t2, z_g, b_gate)


def _ffn_mid_fwd(up, w8, n_lat, *, name, hosted=None):
    hosted = hosted or _NO_EXCHANGE
    t_rows = up.shape[0]
    tm = 128
    n_steps = t_rows // tm
    prev, nxt = _halo_specs(tm, D_FF, t_rows)

    def body(*refs):
        ins, outs, _, h_refs = _split_refs(refs, 5, 2, 0, hosted)
        a_ref, ap_ref, an_ref, g_ref, w_ref = ins
        cv_ref, f_ref = outs
        _run_hosted(hosted, h_refs, pl.program_id(0), n_steps)
        a = a_ref[...]
        au, ad = _shift_rows(a, ap_ref[...], an_ref[...], n_lat, t_rows, tm)
        cv = w_ref[0:1, :] * au + w_ref[1:2, :] * a + w_ref[2:3, :] * ad
        cv_ref[...] = cv
        f_ref[...] = (_silu(cv) * g_ref[...]).astype(BF16)

    outs = pl.pallas_call(
        body, name=name, grid=(n_steps,),
        in_specs=[_row_spec(tm, D_FF), prev, nxt, _row_spec(tm, D_FF, 1), _full_spec((SUBLANES, D_FF))]
        + [ANY] * len(hosted.arrays),
        out_specs=[_row_spec(tm, D_FF), _row_spec(tm, D_FF)] + [ANY] * len(hosted.out_shapes),
        out_shape=[jax.ShapeDtypeStruct((t_rows, D_FF), F32), jax.ShapeDtypeStruct((t_rows, D_FF), BF16)]
        + list(hosted.out_shapes),
        scratch_shapes=list(hosted.scratch),
        compiler_params=_params(("arbitrary",)),
    )(up, up, up, up, w8, *hosted.arrays)
    return outs[0], outs[1], outs[2:]


def _ffn_mid_bwd(up, cv, df, w8, n_lat, *, name):
    t_rows = up.shape[0]
    tm = 128
    prev, nxt = _halo_specs(tm, D_FF, t_rows)
    gprev, gnxt = _halo_specs(tm, D_FF, t_rows, 1)

    def body(a_ref, ap_ref, an_ref, g_ref, gp_ref, gn_ref, cv_ref, cp_ref, cn_ref, df_ref, dfp_ref, dfn_ref,
             w_ref, o_ref, st_ref):
        i = pl.program_id(0)
        a = a_ref[...]
        au, ad = _shift_rows(a, ap_ref[...], an_ref[...], n_lat, t_rows, tm)
        cvv = cv_ref[...]
        dfv = df_ref[...]
        o_ref[:, D_FF:] = (dfv * _silu(cvv)).astype(BF16)
        dcv = dfv * g_ref[...] * _dsilu(cvv)
        dcv_p = dfp_ref[...] * gp_ref[...] * _dsilu(cp_ref[...])
        dcv_n = dfn_ref[...] * gn_ref[...] * _dsilu(cn_ref[...])
        du, dd = _shift_rows(dcv, dcv_p, dcv_n, n_lat, t_rows, tm)
        o_ref[:, :D_FF] = (w_ref[0:1, :] * dd + w_ref[1:2, :] * dcv + w_ref[2:3, :] * du).astype(BF16)
        upd = _rows8([(0, _colsum(dcv * au)), (1, _colsum(dcv * a)), (2, _colsum(dcv * ad))], D_FF)

        @pl.when(i == 0)
        def _():
            st_ref[...] = upd

        @pl.when(i > 0)
        def _():
            st_ref[...] += upd

    row = _row_spec(tm, D_FF)
    return pl.pallas_call(
        body, name=name, grid=(t_rows // tm,),
        in_specs=[row, prev, nxt, _row_spec(tm, D_FF, 1), gprev, gnxt, row, prev, nxt, row, prev, nxt,
                  _full_spec((SUBLANES, D_FF))],
        out_specs=[_row_spec(tm, 2 * D_FF), _full_spec((SUBLANES, D_FF))],
        out_shape=[jax.ShapeDtypeStruct((t_rows, 2 * D_FF), BF16), jax.ShapeDtypeStruct((SUBLANES, D_FF), F32)],
        compiler_params=_params(("arbitrary",)),
    )(up, up, up, up, up, up, cv, cv, cv, df, df, df, w8)


def _loss_head(x, target, g_final, n_lat, *, name):
    t_rows = x.shape[0]
    tm = 256
    last = n_lat // tm - 1

    def body(x_ref, t_ref, g_ref, dx_ref, st_ref):
        i = pl.program_id(0)
        is_ctx = i * tm >= n_lat
        xv = x_ref[...]
        gv = g_ref[...]
        rstd = lax.rsqrt(jnp.mean(xv * xv, axis=-1, keepdims=True) + EPS)
        rn = xv * rstd
        err = rn * gv - t_ref[...]
        dy = err / D
        e = dy * gv
        dx = rstd * (e - rn * jnp.mean(e * rn, axis=-1, keepdims=True))
        dx_ref[...] = jnp.where(is_ctx, 0.0, dx)
        keep = jnp.where(is_ctx, 0.0, 1.0)
        upd = _rows8([(0, keep * _colsum(dy * rn)), (1, keep * _colsum(err * err))], D)

        @pl.when(i == 0)
        def _():
            st_ref[...] = upd

        @pl.when(i > 0)
        def _():
            st_ref[...] += upd

    return pl.pallas_call(
        body, name=name, grid=(t_rows // tm,),
        in_specs=[_row_spec(tm, D), pl.BlockSpec((tm, D), lambda i: (jnp.minimum(i, last), 0)), _full_spec((1, D))],
        out_specs=[_row_spec(tm, D), _full_spec((SUBLANES, D))],
        out_shape=[jax.ShapeDtypeStruct((t_rows, D), F32), jax.ShapeDtypeStruct((SUBLANES, D), F32)],
        compiler_params=_params(("arbitrary",)),
    )(x, target, g_final)


def _sum_slabs(x, out_dtype, *, name):
    n_slabs, rows, cols = x.shape
    tm = _pick(rows, (432, 256, 192, 128, 64, 32, 24, 16, 8))

    def body(x_ref, o_ref):
        acc = x_ref[0].astype(F32)
        for s in range(1, n_slabs):
            acc = acc + x_ref[s].astype(F32)
        o_ref[...] = acc.astype(o_ref.dtype)

    return pl.pallas_call(
        body, name=name, grid=(rows // tm,),
        in_specs=[pl.BlockSpec((n_slabs, tm, cols), lambda i: (0, i, 0))],
        out_specs=pl.BlockSpec((tm, cols), lambda i: (i, 0)),
        out_shape=jax.ShapeDtypeStruct((rows, cols), out_dtype),
        compiler_params=_params(("parallel",)),
    )(x)


def _add_half(half_idx, a, b, *, name):
    n_slabs, rows, cols = b.shape
    tm = _pick(rows, (432, 256, 192, 128, 96, 64, 32, 16))
    per_half = rows // tm

    def body(half_ref, a_ref, b_ref, o_ref):
        o_ref[...] = (a_ref[...].astype(F32) + b_ref[...].astype(F32)).astype(BF16)

    spec = pl.BlockSpec((1, tm, cols), lambda s, i, half_ref: (s, i, 0))
    a_spec = pl.BlockSpec((1, tm, cols), lambda s, i, half_ref: (s, half_ref[0] * per_half + i, 0))
    return pl.pallas_call(
        body, name=name,
        grid_spec=pltpu.PrefetchScalarGridSpec(num_scalar_prefetch=1, grid=(n_slabs, per_half),
                                               in_specs=[a_spec, spec], out_specs=spec),
        out_shape=jax.ShapeDtypeStruct(b.shape, BF16), compiler_params=_params(("parallel", "parallel")),
    )(half_idx, a, b)


def _adamw(w, g, m, v, *, name):
    rows, cols = w.shape
    tm = _pick(rows, (128, 64, 32, 16, 8))

    def body(w_ref, g_ref, m_ref, v_ref, d_ref, nm_ref, nv_ref):
        gv = g_ref[...]
        nm = ADAM_B1 * m_ref[...] + (1.0 - ADAM_B1) * gv
        nv = ADAM_B2 * v_ref[...] + (1.0 - ADAM_B2) * jnp.square(gv)
        m_hat = nm / (1.0 - ADAM_B1 ** ADAM_STEP)
        v_hat = nv / (1.0 - ADAM_B2 ** ADAM_STEP)
        d_ref[...] = -ADAM_LR * (m_hat / (jnp.sqrt(v_hat) + ADAM_EPS) + ADAM_WD * w_ref[...])
        nm_ref[...] = nm
        nv_ref[...] = nv

    spec = pl.BlockSpec((tm, cols), lambda i: (i, 0))
    shape = jax.ShapeDtypeStruct((rows, cols), F32)
    return pl.pallas_call(
        body, name=name, grid=(rows // tm,), in_specs=[spec] * 4, out_specs=[spec] * 3, out_shape=[shape] * 3,
        compiler_params=_params(("parallel",)),
    )(w, g, m, v)


def _place():
    x, y, c = lax.axis_index("x"), lax.axis_index("y"), lax.axis_index("c")
    chips = [(1 - x, y), (x, 1 - y), (1 - x, 1 - y)]
    return x, y, c, chips


def _remote(src, dst, send_sems, recv_sems, k, to):
    return pltpu.make_async_remote_copy(src_ref=src, dst_ref=dst, send_sem=send_sems.at[k], recv_sem=recv_sems.at[k],
                                        device_id=to, device_id_type=MESH)


HALF_CHUNKS = 2


def _chunks(ref, n):
    step = ref.shape[0] // n
    tile_rows = SUBLANES if ref.dtype == F32 else 2 * SUBLANES
    assert step * n == ref.shape[0] and step % tile_rows == 0, (ref.shape, n)
    return [ref.at[pl.ds(k * step, step)] for k in range(n)]


def _half(ref, which):
    half = ref.shape[0] // 2
    return ref.at[pl.ds(pl.multiple_of(which * half, 2 * SUBLANES), half)]


def _staged_copy(src, dst, buf, sems):
    step = buf.shape[1]
    n = src.shape[0] // step
    assert n * step == src.shape[0], (src.shape, step)
    ins = [pltpu.make_async_copy(src.at[pl.ds(k * step, step)], buf.at[k % 2], sems.at[k % 2]) for k in range(n)]
    outs = [pltpu.make_async_copy(buf.at[k % 2], dst.at[pl.ds(k * step, step)], sems.at[2 + k % 2]) for k in range(n)]
    ins[0].start()
    for k in range(n):
        ins[k].wait()
        outs[k].start()
        if k + 1 < n:
            if k >= 1:
                outs[k - 1].wait()
            ins[k + 1].start()
    if n >= 2:
        outs[n - 2].wait()
    outs[n - 1].wait()


def _stage_rows(rows):
    return _pick(rows, (256, 432))


def _stage_scratch(slabs):
    return [pltpu.VMEM((2, _stage_rows(s.shape[-2]), s.shape[-1]), s.dtype) for s in slabs] + [pltpu.SemaphoreType.DMA((4,))]


N_LINK_SEMS = (N_CHIPS - 1) * HALF_CHUNKS


def _link_sems(n_groups):
    return [pltpu.SemaphoreType.DMA((n_groups * N_LINK_SEMS,)), pltpu.SemaphoreType.DMA((n_groups * N_LINK_SEMS,))]


def _sem_index(g, j, k):
    return g * N_LINK_SEMS + j * HALF_CHUNKS + k


def _gather_ici_start(*refs):
    n = (len(refs) - 2) // 2
    p_refs, o_refs, (send_sems, recv_sems) = refs[:n], refs[n:2 * n], refs[2 * n:]
    x, y, c, chips = _place()
    for g, (p_ref, o_ref) in enumerate(zip(p_refs, o_refs)):
        src = _chunks(_half(p_ref, c), HALF_CHUNKS)
        dst = _chunks(_half(o_ref.at[2 * x + y], c), HALF_CHUNKS)
        for j, chip in enumerate(chips):
            for k in range(HALF_CHUNKS):
                _remote(src[k], dst[k], send_sems, recv_sems, _sem_index(g, j, k), (*chip, c)).start()


def _gather_ici_finish(*refs):
    n = (len(refs) - 2) // 2
    p_refs, o_refs, (send_sems, recv_sems) = refs[:n], refs[n:2 * n], refs[2 * n:]
    x, y, c, chips = _place()
    for g, (p_ref, o_ref) in enumerate(zip(p_refs, o_refs)):
        src = _chunks(_half(p_ref, c), HALF_CHUNKS)
        for j, (cx, cy) in enumerate(chips):
            for k, landed in enumerate(_chunks(_half(o_ref.at[2 * cx + cy], c), HALF_CHUNKS)):
                _remote(src[k], landed, send_sems, recv_sems, _sem_index(g, j, k), (x, y, c)).wait_recv()
        for j in range(len(chips)):
            for k in range(HALF_CHUNKS):
                _remote(src[k], src[k], send_sems, recv_sems, _sem_index(g, j, k), (x, y, c)).wait_send()


def _gathered_shapes(slabs):
    return [jax.ShapeDtypeStruct((N_CHIPS,) + s.shape, s.dtype) for s in slabs]


def _gather_ici_hosted(slabs):
    return _Hosted(list(slabs), _gathered_shapes(slabs), _link_sems(len(slabs)), _gather_ici_start, _gather_ici_finish)


def _gather_ici(slabs, *, name):
    def body(*refs):
        _gather_ici_start(*refs)
        _gather_ici_finish(*refs)

    return pl.pallas_call(
        body, name=name, in_specs=[ANY] * len(slabs), out_specs=[ANY] * len(slabs),
        out_shape=_gathered_shapes(slabs), scratch_shapes=_link_sems(len(slabs)),
    )(*slabs)


def _gather_finish(partials, slabs, *, name):
    n = len(slabs)

    def body(*refs):
        p_refs, o_refs = refs[n:2 * n], refs[2 * n:3 * n]
        send_sems, recv_sems = refs[3 * n:3 * n + 2]
        bufs, loc_sems = refs[3 * n + 2:4 * n + 2], refs[4 * n + 2]
        x, y, c, chips = _place()
        sib = (x, y, 1 - c)
        passed = []
        for g, o_ref in enumerate(o_refs):
            for j, (cx, cy) in enumerate(chips):
                for k, landed in enumerate(_chunks(_half(o_ref.at[2 * cx + cy], c), HALF_CHUNKS)):
                    passed.append(_remote(landed, landed, send_sems, recv_sems, _sem_index(g, j, k), sib))
        for cp in passed:
            cp.start()
        for p_ref, o_ref, buf in zip(p_refs, o_refs, bufs):
            _staged_copy(p_ref, o_ref.at[2 * x + y], buf, loc_sems)
        for g, o_ref in enumerate(o_refs):
            for j, (cx, cy) in enumerate(chips):
                for k, landed in enumerate(_chunks(_half(o_ref.at[2 * cx + cy], 1 - c), HALF_CHUNKS)):
                    _remote(landed, landed, send_sems, recv_sems, _sem_index(g, j, k), sib).wait_recv()
        for cp in passed:
            cp.wait_send()

    return pl.pallas_call(
        body, name=name, in_specs=[ANY] * (2 * n), out_specs=[ANY] * n,
        out_shape=[jax.ShapeDtypeStruct(p.shape, p.dtype) for p in partials],
        input_output_aliases={g: g for g in range(n)}, scratch_shapes=_link_sems(n) + _stage_scratch(slabs),
        compiler_params=_params(),
    )(*partials, *slabs)


def _grad_sibling_swap(g_packs, *, name):
    n = len(g_packs)
    per_group = N_CHIPS * HALF_CHUNKS

    def body(*refs):
        g_refs, got_refs, (send_sems, recv_sems) = refs[:n], refs[n:2 * n], refs[2 * n:]
        x, y, c, _ = _place()
        sib = (x, y, 1 - c)
        swaps = [_remote(src, dst, send_sems, recv_sems, g * per_group + s * HALF_CHUNKS + k, sib)
                 for g, (g_ref, got_ref) in enumerate(zip(g_refs, got_refs))
                 for s in range(N_CHIPS)
                 for k, (src, dst) in enumerate(zip(_chunks(_half(g_ref.at[s], 1 - c), HALF_CHUNKS),
                                                    _chunks(got_ref.at[s], HALF_CHUNKS)))]
        for cp in swaps:
            cp.start()
        for cp in swaps:
            cp.wait_recv()
        for cp in swaps:
            cp.wait_send()

    return pl.pallas_call(
        body, name=name, in_specs=[ANY] * n, out_specs=[ANY] * n,
        out_shape=[jax.ShapeDtypeStruct((N_CHIPS, g.shape[1] // 2, g.shape[2]), g.dtype) for g in g_packs],
        scratch_shapes=[pltpu.SemaphoreType.DMA((n * per_group,)), pltpu.SemaphoreType.DMA((n * per_group,))],
    )(*g_packs)


def _grad_ici_refs(refs):
    n = (len(refs) - 3) // 3
    return refs[:n], refs[n:2 * n], refs[2 * n], refs[2 * n + 1], refs[2 * n + 2:3 * n + 2], refs[3 * n + 2]


def _grad_ici_start(*refs):
    s_refs, o_refs, send_sems, recv_sems, _, _ = _grad_ici_refs(refs)
    x, y, c, chips = _place()
    for g, (s_ref, o_ref) in enumerate(zip(s_refs, o_refs)):
        for j, (cx, cy) in enumerate(chips):
            pairs = zip(_chunks(s_ref.at[2 * cx + cy], HALF_CHUNKS), _chunks(o_ref.at[2 * x + y], HALF_CHUNKS))
            for k, (src, dst) in enumerate(pairs):
                _remote(src, dst, send_sems, recv_sems, _sem_index(g, j, k), (cx, cy, c)).start()


def _grad_ici_finish(*refs):
    s_refs, o_refs, send_sems, recv_sems, bufs, loc_sems = _grad_ici_refs(refs)
    x, y, c, chips = _place()
    me = 2 * x + y
    for s_ref, o_ref, buf in zip(s_refs, o_refs, bufs):
        _staged_copy(s_ref.at[me], o_ref.at[me], buf, loc_sems)
    for g, (s_ref, o_ref) in enumerate(zip(s_refs, o_refs)):
        for j, (cx, cy) in enumerate(chips):
            for k, landed in enumerate(_chunks(o_ref.at[2 * cx + cy], HALF_CHUNKS)):
                _remote(landed, landed, send_sems, recv_sems, _sem_index(g, j, k), (x, y, c)).wait_recv()
        for j, (cx, cy) in enumerate(chips):
            for k, sent in enumerate(_chunks(s_ref.at[2 * cx + cy], HALF_CHUNKS)):
                _remote(sent, sent, send_sems, recv_sems, _sem_index(g, j, k), (x, y, c)).wait_send()


def _grad_ici_hosted(sums):
    return _Hosted(list(sums), [jax.ShapeDtypeStruct(s.shape, s.dtype) for s in sums],
                   _link_sems(len(sums)) + _stage_scratch(sums), _grad_ici_start, _grad_ici_finish)


def _grad_ici(sums, *, name):
    def body(*refs):
        _grad_ici_start(*refs)
        _grad_ici_finish(*refs)

    n = len(sums)
    return pl.pallas_call(
        body, name=name, in_specs=[ANY] * n, out_specs=[ANY] * n,
        out_shape=[jax.ShapeDtypeStruct(s.shape, s.dtype) for s in sums],
        scratch_shapes=_link_sems(n) + _stage_scratch(sums), compiler_params=_params(),
    )(*sums)


def _grad_sibling_share(totals, *, name):
    n = len(totals)
    n_ch = HALF_CHUNKS

    def body(*refs):
        t_refs, o_refs = refs[:n], refs[n:2 * n]
        send_sems, recv_sems = refs[2 * n:2 * n + 2]
        bufs, loc_sems = refs[2 * n + 2:3 * n + 2], refs[3 * n + 2]
        x, y, c, _ = _place()
        sib = (x, y, 1 - c)
        sends = [_remote(src, dst, send_sems, recv_sems, g * n_ch + k, sib)
                 for g, (t_ref, o_ref) in enumerate(zip(t_refs, o_refs))
                 for k, (src, dst) in enumerate(zip(_chunks(t_ref, n_ch), _chunks(_half(o_ref, c), n_ch)))]
        for cp in sends:
            cp.start()
        for t_ref, o_ref, buf in zip(t_refs, o_refs, bufs):
            _staged_copy(t_ref, _half(o_ref, c), buf, loc_sems)
        for g, o_ref in enumerate(o_refs):
            for k, landed in enumerate(_chunks(_half(o_ref, 1 - c), n_ch)):
                _remote(landed, landed, send_sems, recv_sems, g * n_ch + k, sib).wait_recv()
        for cp in sends:
            cp.wait_send()

    return pl.pallas_call(
        body, name=name, in_specs=[ANY] * n, out_specs=[ANY] * n,
        out_shape=[jax.ShapeDtypeStruct((2 * t.shape[0], t.shape[1]), t.dtype) for t in totals],
        scratch_shapes=[pltpu.SemaphoreType.DMA((n * n_ch,)), pltpu.SemaphoreType.DMA((n * n_ch,))] + _stage_scratch(totals),
        compiler_params=_params(),
    )(*totals)


def _allgather8(v, *, name):
    rows, cols = v.shape

    def body(v_ref, o_ref, send_sems, recv_sems, loc_sem):
        x, y, c, chips = _place()
        sib = (x, y, 1 - c)

        def slot(px, py, pc):
            return o_ref.at[4 * px + 2 * py + pc]

        local = pltpu.make_async_copy(v_ref, slot(x, y, c), loc_sem.at[0])
        local.start()
        first = [_remote(v_ref, slot(x, y, c), send_sems, recv_sems, 0, sib)]
        first += [_remote(v_ref, slot(x, y, c), send_sems, recv_sems, 1 + j, (*chip, c)) for j, chip in enumerate(chips)]
        for cp in first:
            cp.start()
        passed = [_remote(slot(*chip, c), slot(*chip, c), send_sems, recv_sems, 4 + j, sib)
                  for j, chip in enumerate(chips)]
        for j, chip in enumerate(chips):
            _remote(v_ref, slot(*chip, c), send_sems, recv_sems, 1 + j, sib).wait_recv()
            passed[j].start()
        _remote(v_ref, slot(x, y, 1 - c), send_sems, recv_sems, 0, sib).wait_recv()
        for j, chip in enumerate(chips):
            _remote(v_ref, slot(*chip, 1 - c), send_sems, recv_sems, 4 + j, sib).wait_recv()
        for cp in first + passed:
            cp.wait_send()
        local.wait()

    return pl.pallas_call(
        body, name=name, in_specs=[ANY], out_specs=ANY, out_shape=jax.ShapeDtypeStruct((N_DEV, rows, cols), v.dtype),
        scratch_shapes=[pltpu.SemaphoreType.DMA((7,)), pltpu.SemaphoreType.DMA((7,)), pltpu.SemaphoreType.DMA((1,))],
    )(v)


_BIG = (("w_mod", (D, 6 * D), 1), ("w_in", (D, IN_W), 1), ("w_branch", (3 * D, D), None), ("w_out", (D, D), 0),
        ("w_up", (D, 2 * D_FF), 1), ("w_down", (D_FF, D), 0))
_COL_SHARDED = ("w_mod", "w_in", "w_up")
_ROW_SHARDED = (("w_branch", 3 * D // N_CHIPS), ("w_out", D // N_CHIPS), ("w_down", D_FF // N_CHIPS))


def _pack_shards(shards, layer):
    rows = jnp.concatenate([shards[n][layer].reshape(r, D) for n, r in _ROW_SHARDED], axis=0)
    return [shards[n][layer] for n in _COL_SHARDED] + [rows]


def _unpack_cols(blk):
    return blk.transpose(1, 0, 2).reshape(blk.shape[1], N_CHIPS * blk.shape[2])


def _unpack_rows(stack):
    out, off = {}, 0
    for name, r in _ROW_SHARDED:
        blk = stack[:, off:off + r, :]
        off += r
        if name == "w_branch":
            out[name] = blk.reshape(N_CHIPS, 3, D // N_CHIPS, D).transpose(1, 0, 2, 3).reshape(3, D, D)
        else:
            out[name] = blk.reshape(N_CHIPS * r, D)
    return out


def _unpack_full(gathered):
    out = {name: _unpack_cols(blk) for name, blk in zip(_COL_SHARDED, gathered)}
    out.update(_unpack_rows(gathered[-1]))
    return out


def _pack_grad_cols(g):
    return g.reshape(g.shape[0], N_CHIPS, g.shape[1] // N_CHIPS).transpose(1, 0, 2)


def _pack_grad_rows(grads):
    parts = []
    for name, r in _ROW_SHARDED:
        g = grads[name]
        if name == "w_branch":
            g = g.reshape(3, N_CHIPS, D // N_CHIPS, D).transpose(1, 0, 2, 3)
        parts.append(g.reshape(N_CHIPS, r, D))
    return jnp.concatenate(parts, axis=1)


def _pack_grads(grads):
    return [_pack_grad_cols(grads[n]) for n in _COL_SHARDED] + [_pack_grad_rows(grads)]


def _unpack_shards(totals, like):
    out = {n: jnp.stack([totals[l][g] for l in range(DEPTH)]) for g, n in enumerate(_COL_SHARDED)}
    off = 0
    for name, r in _ROW_SHARDED:
        out[name] = jnp.stack([totals[l][-1][off:off + r] for l in range(DEPTH)]).reshape(like[name].shape)
        off += r
    return out


def _pad_rows(v, rows):
    return jnp.concatenate([v, jnp.zeros((rows - v.shape[0],) + v.shape[1:], v.dtype)], axis=0)


def _local_step(x_tok, target, c_vec, c_ctx, wfull, small, n_lat, n_ctx):
    ctx = _step_context(c_vec, c_ctx, n_lat, n_ctx)
    saved = []
    xs = x_tok
    for l in range(DEPTH):
        xs, s, _ = _layer_fwd(l, xs, wfull[l], {k: v[l] for k, v in small.items() if k != "g_final"}, ctx)
        saved.append(s)
    dx, sq_err, d_g_final = _loss_bwd(xs, target, small["g_final"], n_lat)
    wgrads, lgrads, d_a128 = [None] * DEPTH, [None] * DEPTH, [None] * DEPTH
    for l in reversed(range(DEPTH)):
        dx, wgrads[l], lgrads[l], d_a128[l], _ = _layer_bwd(l, saved[l], wfull[l], dx, ctx)
    return sq_err, dx, wgrads, _small_grads(lgrads, d_a128, d_g_final, ctx)


def _step_context(c_vec, c_ctx, n_lat, n_ctx):
    cos_t, sin_t = _rope_tables(n_lat, n_ctx)
    a_in = _pad_rows(jnp.stack([c_vec, c_ctx]), LANES)
    a128 = _small(_silu, (LANES, D), a_in, name="cond_silu")
    return dict(cos_t=cos_t, sin_t=sin_t, a_in=a_in, a128=a128, n_lat=n_lat, n_ctx=n_ctx)


def _loss_bwd(xs, target, g_final, n_lat):
    dx, st = _loss_head(xs, target, g_final[None, :], n_lat, name="loss_head")
    return dx, st[1], st[0]


def _small_grads(lgrads, d_a128, d_g_final, ctx):
    d_cond = _small(lambda a, b, cin: (a + b) * _dsilu(cin), (LANES, D), d_a128[0], d_a128[1], ctx["a_in"],
                    name="cond_bwd")
    out = {k: jnp.stack([lgrads[l][k] for l in range(DEPTH)]) for k in lgrads[0]}
    out["c_ctx"] = d_cond[1]
    out["g_final"] = d_g_final
    return out


def _layer_fwd(l, xs, w, sm, ctx, hosted=_NO_EXCHANGE, hosted_ffn=_NO_EXCHANGE, late_weights=None):
    n_lat, n_ctx, cos_t, sin_t, a128 = ctx["n_lat"], ctx["n_ctx"], ctx["cos_t"], ctx["sin_t"], ctx["a128"]
    mod128 = _mm(a128, w["w_mod"], name=f"mod{l}")
    mod8 = _small(lambda m, b: m + b, (SUBLANES, 6 * D), mod128[:SUBLANES], sm["b_mod"][None, :], name=f"mod_bias{l}")
    g_mix = sm["g_mix"][None, :]
    g_ffn = sm["g_ffn"][None, :]
    g_v = sm["g_v"][None, :]
    b_gate = sm["b_gate"][None, :]
    sink_b = jnp.broadcast_to(sm["sink"][:, None], (N_HEADS, LANES))
    b_sb = jnp.broadcast_to(sm["b_spatial"][:, :, None], (A_GROUPS, BLK, LANES))
    w_sconv8 = _pad_rows(sm["w_sconv"], SUBLANES)
    w_fconv8 = _pad_rows(sm["w_fconv"], SUBLANES)
    w_in = w["w_in"]
    w_seg = [w_in[:, SEG[k]:SEG[k + 1]] for k in range(4)]

    h = _norm_mod_fwd(xs, g_mix, mod8, 0, 1, n_lat, name=f"norm1_{l}")
    z_qkv, z_a, z_b, z_g = [_mm(h, w_seg[k], name=f"in_proj{k}_{l}") for k in range(4)]
    q, kd, vd = _qkv_prep(z_qkv, cos_t, sin_t, name=f"qkv_prep{l}")
    y_attn, carried = _attention_fwd(q, kd, vd, sink_b, n_lat, n_ctx, name=f"attn{l}", hosted=hosted)
    if late_weights is not None:
        w = dict(w, **late_weights(carried))
    y_a = _gating_fwd(z_a, sm["w_spatial"], b_sb, g_v, name=f"gating{l}")
    y_b = _sconv_fwd(z_b, w_sconv8, n_lat, name=f"sconv{l}")
    ys = (y_attn, y_a, y_b)
    ts = [_mm(ys[k], w["w_branch"][k], name=f"branch{k}_{l}") for k in range(3)]
    merged = _merge_fwd(*ts, z_g, b_gate, name=f"merge{l}")
    mix_out = _mm(merged, w["w_out"], name=f"out_proj{l}")
    x1 = _residual_fwd(xs, mix_out, mod8, 2, n_lat, name=f"res1_{l}")
    h2 = _norm_mod_fwd(x1, g_ffn, mod8, 3, 4, n_lat, name=f"norm2_{l}")
    up = _mm(h2, w["w_up"], name=f"up_proj{l}")
    cv, f, carried_ffn = _ffn_mid_fwd(up, w_fconv8, n_lat, name=f"ffn_mid{l}", hosted=hosted_ffn)
    ffn_out = _mm(f, w["w_down"], name=f"down_proj{l}")
    x2 = _residual_fwd(x1, ffn_out, mod8, 5, n_lat, name=f"res2_{l}")
    saved = dict(x0=xs, mod8=mod8, h=h, z_qkv=z_qkv, z_a=z_a, z_b=z_b, z_g=z_g, q=q, kd=kd, vd=vd, ys=ys, ts=ts,
                 merged=merged, mix_out=mix_out, x1=x1, h2=h2, up=up, cv=cv, f=f, ffn_out=ffn_out, w_seg=w_seg,
                 g_mix=g_mix, g_ffn=g_ffn, g_v=g_v, b_gate=b_gate, sink_b=sink_b, b_sb=b_sb,
                 w_sconv8=w_sconv8, w_fconv8=w_fconv8, w_spatial=sm["w_spatial"])
    return x2, saved, (carried, carried_ffn)


def _layer_bwd(l, s, w, dx, ctx, make_hosted=None):
    n_lat, n_ctx, cos_t, sin_t, a128 = ctx["n_lat"], ctx["n_ctx"], ctx["cos_t"], ctx["sin_t"], ctx["a128"]
    mod8 = s["mod8"]
    d_ffn, st_gt2 = _residual_bwd(dx, s["ffn_out"], mod8, 5, n_lat, name=f"res2_bwd{l}")
    df = _mm(d_ffn, w["w_down"], tb=True, name=f"down_bwd_x{l}")
    g_down = _mm(s["f"], d_ffn, ta=True, out_dtype=BF16, name=f"down_bwd_w{l}")
    d_up, st_fc = _ffn_mid_bwd(s["up"], s["cv"], df, s["w_fconv8"], n_lat, name=f"ffn_mid_bwd{l}")
    dh2 = _mm(d_up, w["w_up"], tb=True, name=f"up_bwd_x{l}")
    g_up = _mm(s["h2"], d_up, ta=True, out_dtype=BF16, name=f"up_bwd_w{l}")
    dx1, st_n2 = _norm_mod_bwd(s["x1"], [dh2], dx, s["g_ffn"], mod8, 4, n_lat, name=f"norm2_bwd{l}")
    d_out, st_gt1 = _residual_bwd(dx1, s["mix_out"], mod8, 2, n_lat, name=f"res1_bwd{l}")
    d_merged = _mm(d_out, w["w_out"], tb=True, name=f"out_bwd_x{l}")
    g_out = _mm(s["merged"], d_out, ta=True, out_dtype=BF16, name=f"out_bwd_w{l}")
    dt0, dt1, dt2, dz_g, st_bg = _merge_bwd(d_merged, *s["ts"], s["z_g"], s["b_gate"], name=f"merge_bwd{l}")
    dts = (dt0, dt1, dt2)
    dys = [_mm(dts[k], w["w_branch"][k], tb=True, name=f"branch{k}_bwd_x{l}") for k in range(3)]
    g_branch = jnp.stack([_mm(s["ys"][k], dts[k], ta=True, out_dtype=BF16, name=f"branch{k}_bwd_w{l}")
                          for k in range(3)])
    early = dict(w_branch=g_branch.reshape(3 * D, D), w_out=g_out, w_up=g_up, w_down=g_down)
    hosted = make_hosted(early) if make_hosted else _NO_EXCHANGE
    dq, dk, dv, d_sink, carried = _attention_bwd(s["q"], s["kd"], s["vd"], s["sink_b"], dys[0], n_lat, n_ctx,
                                                 name=f"attn_bwd{l}", hosted=hosted)
    dz_qkv = _qkv_unprep(dq, dk, dv, cos_t, sin_t, name=f"qkv_unprep{l}")
    dz_a, d_ws, d_bs, st_gv = _gating_bwd(s["z_a"], dys[1], s["w_spatial"], s["b_sb"], s["g_v"], name=f"gating_bwd{l}")
    dz_b, st_sc = _sconv_bwd(s["z_b"], dys[2], s["w_sconv8"], n_lat, name=f"sconv_bwd{l}")
    dzs = (dz_qkv, dz_a, dz_b, dz_g)
    dh_parts = [_mm(dzs[k], s["w_seg"][k], tb=True, name=f"in_bwd_x{k}_{l}") for k in range(4)]
    g_in = jnp.concatenate([_mm(s["h"], dzs[k], ta=True, out_dtype=BF16, name=f"in_bwd_w{k}_{l}")
                            for k in range(4)], axis=1)
    dx0, st_n1 = _norm_mod_bwd(s["x0"], dh_parts, dx1, s["g_mix"], mod8, 1, n_lat, name=f"norm1_bwd{l}")
    dmod = jnp.concatenate([st_n1[0:2], st_n1[2:4], st_gt1[0:2], st_n2[0:2], st_n2[2:4], st_gt2[0:2]], axis=1)
    dmod128 = _pad_rows(dmod, LANES)
    g_mod = _mm(a128, dmod128, ta=True, out_dtype=BF16, name=f"mod_bwd_w{l}")
    d_a128 = _mm(dmod128, w["w_mod"], tb=True, name=f"mod_bwd_x{l}")
    wgrads = dict(early, w_mod=g_mod, w_in=g_in)
    lgrads = dict(b_mod=dmod[0] + dmod[1], g_mix=st_n1[4], g_ffn=st_n2[4], b_gate=st_bg[0], sink=d_sink[:, 0],
                  w_spatial=d_ws, b_spatial=d_bs[:, :, 0], g_v=st_gv[0], w_sconv=st_sc[0:3], w_fconv=st_fc[0:3])
    return dx0, wgrads, lgrads, d_a128, carried


_SMALL_ORDER = ("c_ctx", "b_mod", "g_mix", "b_gate", "sink", "w_spatial", "b_spatial", "g_v", "w_sconv", "g_ffn",
                "w_fconv", "g_final")


def _flat_pack(parts, width):
    flat = jnp.concatenate([p.reshape(-1).astype(F32) for p in parts])
    rows = -(-flat.shape[0] // (width * SUBLANES)) * SUBLANES
    flat = jnp.concatenate([flat, jnp.zeros((rows * width - flat.shape[0],), F32)])
    return flat.reshape(rows, width)


def _flat_unpack(packed, likes):
    flat = packed.reshape(-1)
    out, off = [], 0
    for like in likes:
        n = math.prod(like.shape)
        out.append(flat[off:off + n].reshape(like.shape))
        off += n
    return out


def kernel(x, c, ctx, c_ctx, w_mod, b_mod, g_mix, w_in, b_gate, sink, w_spatial, b_spatial, g_v, w_sconv, w_branch, w_out, g_ffn, w_up, w_fconv, w_down, g_final, loss_target, m_c_ctx, m_w_mod, m_b_mod, m_g_mix, m_w_in, m_b_gate, m_sink, m_w_spatial, m_b_spatial, m_g_v, m_w_sconv, m_w_branch, m_w_out, m_g_ffn, m_w_up, m_w_fconv, m_w_down, m_g_final, v_c_ctx, v_w_mod, v_b_mod, v_g_mix, v_w_in, v_b_gate, v_sink, v_w_spatial, v_b_spatial, v_g_v, v_w_sconv, v_w_branch, v_w_out, v_g_ffn, v_w_up, v_w_fconv, v_w_down, v_g_final):
    n_lat, n_ctx = x.shape[1], ctx.shape[1]
    chip = 2 * lax.axis_index("x") + lax.axis_index("y")
    weights = dict(c_ctx=c_ctx, w_mod=w_mod, b_mod=b_mod, g_mix=g_mix, w_in=w_in, b_gate=b_gate, sink=sink,
                   w_spatial=w_spatial, b_spatial=b_spatial, g_v=g_v, w_sconv=w_sconv, w_branch=w_branch, w_out=w_out,
                   g_ffn=g_ffn, w_up=w_up, w_fconv=w_fconv, w_down=w_down, g_final=g_final)
    m_in = dict(c_ctx=m_c_ctx, w_mod=m_w_mod, b_mod=m_b_mod, g_mix=m_g_mix, w_in=m_w_in, b_gate=m_b_gate, sink=m_sink,
                w_spatial=m_w_spatial, b_spatial=m_b_spatial, g_v=m_g_v, w_sconv=m_w_sconv, w_branch=m_w_branch,
                w_out=m_w_out, g_ffn=m_g_ffn, w_up=m_w_up, w_fconv=m_w_fconv, w_down=m_w_down, g_final=m_g_final)
    v_in = dict(c_ctx=v_c_ctx, w_mod=v_w_mod, b_mod=v_b_mod, g_mix=v_g_mix, w_in=v_w_in, b_gate=v_b_gate, sink=v_sink,
                w_spatial=v_w_spatial, b_spatial=v_b_spatial, g_v=v_g_v, w_sconv=v_w_sconv, w_branch=v_w_branch,
                w_out=v_w_out, g_ffn=v_g_ffn, w_up=v_w_up, w_fconv=v_w_fconv, w_down=v_w_down, g_final=v_g_final)
    big_names = [n for n, _, _ in _BIG]

    conv_pack = _flat_pack([w_sconv, w_fconv], LANES)
    conv_all = _allgather8(conv_pack, name="gather_conv_weights")
    conv_parts = [_flat_unpack(conv_all[2 * p], [w_sconv, w_fconv]) for p in range(N_CHIPS)]
    w_sconv_full = jnp.concatenate([cp[0] for cp in conv_parts], axis=-1)
    w_fconv_full = jnp.concatenate([cp[1] for cp in conv_parts], axis=-1)

    small = dict(b_mod=b_mod, g_mix=g_mix, b_gate=b_gate, sink=sink, w_spatial=w_spatial, b_spatial=b_spatial, g_v=g_v,
                 w_sconv=w_sconv_full, g_ffn=g_ffn, w_fconv=w_fconv_full, g_final=g_final)
    x_tok = jnp.concatenate([x[0], ctx[0]], axis=0)
    step = _step_context(c[0], c_ctx, n_lat, n_ctx)
    layer_small = [{k: v[l] for k, v in small.items() if k != "g_final"} for l in range(DEPTH)]
    my_half = lax.axis_index("c").astype(jnp.int32).reshape(1)

    shards = {n: weights[n].astype(BF16) for n in big_names}
    pack = [_pack_shards(shards, l) for l in range(DEPTH)]
    first = _gather_finish(_gather_ici(pack[0][:2], name="gather_ici0"), pack[0][:2], name="gather_finish0")
    w0 = dict(w_mod=_unpack_cols(first[0]), w_in=_unpack_cols(first[1]))

    def layer0_late_weights(carried):
        rest = _gather_finish(list(carried[2:]), pack[0][2:], name="gather_finish0_late")
        w0.update(w_up=_unpack_cols(rest[0]), **_unpack_rows(rest[1]))
        return w0

    xs, saved0, (part_attn, part_ffn) = _layer_fwd(
        0, x_tok, w0, layer_small[0], step, hosted=_gather_ici_hosted(pack[1][:2] + pack[0][2:]),
        hosted_ffn=_gather_ici_hosted(pack[1][2:]), late_weights=layer0_late_weights)
    w1 = _unpack_full(_gather_finish(list(part_attn[:2]) + list(part_ffn), pack[1], name="gather_finish1"))
    xs, saved1, _ = _layer_fwd(1, xs, w1, layer_small[1], step)
    dx, sq_err, d_g_final = _loss_bwd(xs, loss_target[0], g_final, n_lat)
    loss = lax.psum(0.5 * jnp.sum(sq_err) / D, ("x", "y", "c"))

    def reduce_start(g_packs, tag):
        got = _grad_sibling_swap(g_packs, name=f"grad_sibling_swap{tag}")
        return [_add_half(my_half, a, b, name=f"grad_pair_sum{tag}_{g}") for g, (a, b) in enumerate(zip(g_packs, got))]

    def reduce_finish(exchanged, tag):
        sums = [_sum_slabs(e, F32, name=f"grad_chip_sum{tag}_{g}") for g, e in enumerate(exchanged)]
        return _grad_sibling_share(sums, name=f"grad_sibling_share{tag}")

    dx, wgrads1, lgrads1, d_a1, _ = _layer_bwd(1, saved1, w1, dx, step)
    pair_sum1 = reduce_start(_pack_grads(wgrads1), "1")

    def carried_by_attn_bwd0(early):
        pair_sum0_early = reduce_start([_pack_grad_cols(early["w_up"]), _pack_grad_rows(early)], "0_early")
        return _grad_ici_hosted(pair_sum1 + pair_sum0_early)

    dx, wgrads0, lgrads0, d_a0, exchanged = _layer_bwd(0, saved0, w0, dx, step, make_hosted=carried_by_attn_bwd0)
    total1 = reduce_finish(exchanged[:4], "1")
    total0_early = reduce_finish(exchanged[4:], "0_early")
    late = reduce_start([_pack_grad_cols(wgrads0["w_mod"]), _pack_grad_cols(wgrads0["w_in"])], "0_late")
    total0_late = reduce_finish(_grad_ici(late, name="grad_chip_exchange0_late"), "0_late")
    big_grads = _unpack_shards([list(total0_late) + list(total0_early), total1], {n: weights[n] for n in big_names})
    sgrads = _small_grads([lgrads0, lgrads1], [d_a0, d_a1], d_g_final, step)
    grad_x = dx[:n_lat][None]

    s_likes = [sgrads[n] for n in _SMALL_ORDER]
    s_all = _allgather8(_flat_pack(s_likes, D), name="gather_small_grads")
    s_tot = _flat_unpack(_sum_slabs(s_all, F32, name="small_grad_sum"), s_likes)
    grads = dict(big_grads)
    for n, g in zip(_SMALL_ORDER, s_tot):
        grads[n] = g
    grads["w_sconv"] = lax.dynamic_slice_in_dim(grads["w_sconv"], chip * w_sconv.shape[-1], w_sconv.shape[-1], axis=2)
    grads["w_fconv"] = lax.dynamic_slice_in_dim(grads["w_fconv"], chip * w_fconv.shape[-1], w_fconv.shape[-1], axis=2)

    delta, new_m, new_v = {}, {}, {}
    for n in big_names:
        cols = weights[n].shape[-1]
        view = lambda a: a.reshape(-1, cols)
        d_, m_, v_ = _adamw(view(weights[n]), view(grads[n]), view(m_in[n]), view(v_in[n]), name=f"adamw_{n}")
        delta[n], new_m[n], new_v[n] = (t.reshape(weights[n].shape) for t in (d_, m_, v_))
    likes = [weights[n] for n in _SMALL_ORDER]
    packs = [_flat_pack([src[n] for n in _SMALL_ORDER], D) for src in (weights, grads, m_in, v_in)]
    outs = _adamw(*packs, name="adamw_small")
    for dst, packed in zip((delta, new_m, new_v), outs):
        for n, val in zip(_SMALL_ORDER, _flat_unpack(packed, likes)):
            dst[n] = val

    order = ("c_ctx", "w_mod", "b_mod", "g_mix", "w_in", "b_gate", "sink", "w_spatial", "b_spatial", "g_v", "w_sconv",
             "w_branch", "w_out", "g_ffn", "w_up", "w_fconv", "w_down", "g_final")
    return (loss, grad_x, *[grads[n] for n in order], *[delta[n] for n in order], *[new_m[n] for n in order],
            *[new_v[n] for n in order])
```

```python
import functools
import math

import jax
import jax.numpy as jnp
from jax import lax
from jax.experimental import pallas as pl
from jax.experimental.pallas import tpu as pltpu

F32 = jnp.float32
BF16 = jnp.bfloat16

D = 1024
DEPTH = 2
GRID_W = 64
N_HEADS = 16
N_KV = 4
GRP = N_HEADS // N_KV
HEAD_DIM = 64
KV_W = N_KV * HEAD_DIM
WINDOW = 128
BLK = 128
ROPE_THETA = 10000.0
A_GROUPS = 8
D_FF = 2816
EPS = 1e-6
NEG = -1e30
QKV_W = D + 2 * KV_W
A_COLS = 2 * D
B_COLS = 3 * D
G_COLS = 3 * D
IN_W = QKV_W + A_COLS + B_COLS + G_COLS
SEG = (0, QKV_W, QKV_W + A_COLS, QKV_W + A_COLS + B_COLS, IN_W)
N_CHIPS = 4
N_DEV = 8
LANES = 128
SUBLANES = 8
VMEM_LIMIT = 48 * 1024 * 1024
ADAM_LR = 0.001
ADAM_B1 = 0.9
ADAM_B2 = 0.999
ADAM_EPS = 1e-08
ADAM_WD = 0.01
ADAM_STEP = 10
MESH = pl.DeviceIdType.MESH
ANY = pl.BlockSpec(memory_space=pl.ANY)


def _params(sem=None):
    return pltpu.CompilerParams(dimension_semantics=sem, vmem_limit_bytes=VMEM_LIMIT)


def _pick(n, cands):
    for c in cands:
        if n % c == 0:
            return c
    return n


def _rows8(rows, width):
    r = lax.broadcasted_iota(jnp.int32, (SUBLANES, width), 0)
    out = jnp.zeros((SUBLANES, width), F32)
    for idx, v in rows:
        out = out + jnp.where(r == idx, v, 0.0)
    return out


def _sel(mod_ref, k, is_ctx):
    return jnp.where(is_ctx, mod_ref[1:2, k * D:(k + 1) * D], mod_ref[0:1, k * D:(k + 1) * D])


def _colsum(v):
    return jnp.sum(v, axis=0, keepdims=True)


def _mm(a, b, *, name, ta=False, tb=False, out_dtype=F32):
    if ta:
        k_dim, m = a.shape
    else:
        m, k_dim = a.shape
    if tb:
        n, kb = b.shape
    else:
        kb, n = b.shape
    assert k_dim == kb, (a.shape, b.shape, ta, tb)
    tm = _pick(m, (1056, 1024, 1408, 768, 512, 256, 128))
    tn = _pick(n, (1536, 1408, 1024, 768, 512, 256, 128))
    tk = _pick(k_dim, (2048, 1536, 1408, 1024, 768, 512, 256, 128))
    nk = k_dim // tk
    dims = (((0 if ta else 1,), (1 if tb else 0,)), ((), ()))

    def product(a_ref, b_ref):
        return lax.dot_general(a_ref[...].astype(BF16), b_ref[...].astype(BF16), dims, preferred_element_type=F32)

    def body_single(a_ref, b_ref, o_ref):
        o_ref[...] = product(a_ref, b_ref).astype(o_ref.dtype)

    def body_acc(a_ref, b_ref, o_ref, acc_ref):
        k = pl.program_id(2)

        @pl.when(k == 0)
        def _():
            acc_ref[...] = product(a_ref, b_ref)

        @pl.when(k > 0)
        def _():
            acc_ref[...] += product(a_ref, b_ref)

        @pl.when(k == nk - 1)
        def _():
            o_ref[...] = acc_ref[...].astype(o_ref.dtype)

    a_spec = pl.BlockSpec((tk, tm), lambda i, j, k: (k, i)) if ta else pl.BlockSpec((tm, tk), lambda i, j, k: (i, k))
    b_spec = pl.BlockSpec((tn, tk), lambda i, j, k: (j, k)) if tb else pl.BlockSpec((tk, tn), lambda i, j, k: (k, j))
    return pl.pallas_call(
        body_single if nk == 1 else body_acc, name=name, grid=(m // tm, n // tn, nk),
        in_specs=[a_spec, b_spec], out_specs=pl.BlockSpec((tm, tn), lambda i, j, k: (i, j)),
        out_shape=jax.ShapeDtypeStruct((m, n), out_dtype),
        scratch_shapes=[] if nk == 1 else [pltpu.VMEM((tm, tn), F32)],
        compiler_params=_params(("parallel", "parallel", "arbitrary")),
    )(a, b)


def _small(fn, out_shape, *arrays, name):
    def body(*refs):
        refs[-1][...] = fn(*[r[...] for r in refs[:-1]]).astype(refs[-1].dtype)

    return pl.pallas_call(body, name=name, out_shape=jax.ShapeDtypeStruct(out_shape, F32))(*arrays)


def _silu(v):
    return v * jax.nn.sigmoid(v)


def _dsilu(v):
    s = jax.nn.sigmoid(v)
    return s * (1.0 + v * (1.0 - s))


def _row_spec(tm, width, col=0):
    return pl.BlockSpec((tm, width), lambda i: (i, col))


def _full_spec(shape):
    nd = len(shape)
    return pl.BlockSpec(shape, lambda i: (0,) * nd)


def _halo_specs(tm, width, t_rows, col=0):
    per = tm // SUBLANES
    last = t_rows // SUBLANES - 1
    prev = pl.BlockSpec((SUBLANES, width), lambda i: (jnp.maximum(i * per - 1, 0), col))
    nxt = pl.BlockSpec((SUBLANES, width), lambda i: (jnp.minimum((i + 1) * per, last), col))
    return prev, nxt


def _shift_rows(cur, prev8, next8, n_lat, t_rows, tm):
    i = pl.program_id(0)
    row = lax.broadcasted_iota(jnp.int32, (tm, 1), 0)
    g = row + i * tm
    up = pltpu.roll(cur, 1, 0)
    up = jnp.where(row == 0, prev8[SUBLANES - 1:SUBLANES, :], up)
    up = jnp.where((g == 0) | (g == n_lat), 0.0, up)
    dn = pltpu.roll(cur, tm - 1, 0)
    dn = jnp.where(row == tm - 1, next8[0:1, :], dn)
    dn = jnp.where((g == n_lat - 1) | (g == t_rows - 1), 0.0, dn)
    return up, dn


def _norm_mod_fwd(x, g, mod8, sh_idx, sc_idx, n_lat, *, name):
    t_rows = x.shape[0]
    tm = 256

    def body(x_ref, g_ref, mod_ref, o_ref):
        is_ctx = pl.program_id(0) * tm >= n_lat
        xv = x_ref[...]
        rstd = lax.rsqrt(jnp.mean(xv * xv, axis=-1, keepdims=True) + EPS)
        y = xv * rstd * g_ref[...]
        o_ref[...] = (y * (1.0 + _sel(mod_ref, sc_idx, is_ctx)) + _sel(mod_ref, sh_idx, is_ctx)).astype(BF16)

    return pl.pallas_call(
        body, name=name, grid=(t_rows // tm,),
        in_specs=[_row_spec(tm, D), _full_spec((1, D)), _full_spec((SUBLANES, 6 * D))],
        out_specs=_row_spec(tm, D), out_shape=jax.ShapeDtypeStruct((t_rows, D), BF16),
        compiler_params=_params(("parallel",)),
    )(x, g, mod8)


def _norm_mod_bwd(x, dh_parts, dres, g, mod8, sc_idx, n_lat, *, name, hosted=None):
    hosted = hosted or _NO_EXCHANGE
    t_rows = x.shape[0]
    tm = 256
    n_parts = len(dh_parts)
    n_steps = t_rows // tm

    def body(*refs):
        ins, outs, _, h_refs = _split_refs(refs, 4 + n_parts, 2, 0, hosted)
        x_ref, dres_ref, g_ref, mod_ref = ins[:4]
        part_refs = ins[4:]
        dx_ref, st_ref = outs
        i = pl.program_id(0)
        _run_hosted(hosted, h_refs, i, n_steps)
        is_ctx = i * tm >= n_lat
        dh = part_refs[0][...]
        for p in part_refs[1:]:
            dh = dh + p[...]
        xv = x_ref[...]
        gv = g_ref[...]
        rstd = lax.rsqrt(jnp.mean(xv * xv, axis=-1, keepdims=True) + EPS)
        rn = xv * rstd
        dy = dh * (1.0 + _sel(mod_ref, sc_idx, is_ctx))
        e = dy * gv
        dx_ref[...] = dres_ref[...] + rstd * (e - rn * jnp.mean(e * rn, axis=-1, keepdims=True))
        dsh = _colsum(dh)
        dsc = _colsum(dh * (rn * gv))
        dg = _colsum(dy * rn)
        zero = jnp.zeros_like(dsh)
        upd = _rows8([(0, jnp.where(is_ctx, zero, dsh)), (1, jnp.where(is_ctx, dsh, zero)),
                      (2, jnp.where(is_ctx, zero, dsc)), (3, jnp.where(is_ctx, dsc, zero)), (4, dg)], D)

        @pl.when(i == 0)
        def _():
            st_ref[...] = upd

        @pl.when(i > 0)
        def _():
            st_ref[...] += upd

    outs = pl.pallas_call(
        body, name=name, grid=(n_steps,),
        in_specs=[_row_spec(tm, D), _row_spec(tm, D), _full_spec((1, D)), _full_spec((SUBLANES, 6 * D))]
        + [_row_spec(tm, D)] * n_parts + [ANY] * len(hosted.arrays),
        out_specs=[_row_spec(tm, D), _full_spec((SUBLANES, D))] + [ANY] * len(hosted.out_shapes),
        out_shape=[jax.ShapeDtypeStruct((t_rows, D), F32), jax.ShapeDtypeStruct((SUBLANES, D), F32)]
        + list(hosted.out_shapes),
        scratch_shapes=list(hosted.scratch),
        compiler_params=_params(("arbitrary",)),
    )(x, dres, g, mod8, *dh_parts, *hosted.arrays)
    return outs[0], outs[1], outs[2:]


def _residual_fwd(x, branch, mod8, gt_idx, n_lat, *, name):
    t_rows = x.shape[0]
    tm = 256

    def body(x_ref, b_ref, mod_ref, o_ref):
        is_ctx = pl.program_id(0) * tm >= n_lat
        o_ref[...] = x_ref[...] + _sel(mod_ref, gt_idx, is_ctx) * b_ref[...]

    return pl.pallas_call(
        body, name=name, grid=(t_rows // tm,),
        in_specs=[_row_spec(tm, D), _row_spec(tm, D), _full_spec((SUBLANES, 6 * D))],
        out_specs=_row_spec(tm, D), out_shape=jax.ShapeDtypeStruct((t_rows, D), F32),
        compiler_params=_params(("parallel",)),
    )(x, branch, mod8)


def _residual_bwd(dx, branch, mod8, gt_idx, n_lat, *, name):
    t_rows = dx.shape[0]
    tm = 256

    def body(dx_ref, b_ref, mod_ref, o_ref, st_ref):
        i = pl.program_id(0)
        is_ctx = i * tm >= n_lat
        dxv = dx_ref[...]
        o_ref[...] = (dxv * _sel(mod_ref, gt_idx, is_ctx)).astype(BF16)
        dgt = _colsum(dxv * b_ref[...])
        zero = jnp.zeros_like(dgt)
        upd = _rows8([(0, jnp.where(is_ctx, zero, dgt)), (1, jnp.where(is_ctx, dgt, zero))], D)

        @pl.when(i == 0)
        def _():
            st_ref[...] = upd

        @pl.when(i > 0)
        def _():
            st_ref[...] += upd

    return pl.pallas_call(
        body, name=name, grid=(t_rows // tm,),
        in_specs=[_row_spec(tm, D), _row_spec(tm, D), _full_spec((SUBLANES, 6 * D))],
        out_specs=[_row_spec(tm, D), _full_spec((SUBLANES, D))],
        out_shape=[jax.ShapeDtypeStruct((t_rows, D), BF16), jax.ShapeDtypeStruct((SUBLANES, D), F32)],
        compiler_params=_params(("arbitrary",)),
    )(dx, branch, mod8)


def _rope_tables(n_lat, n_ctx):
    rows = n_lat // GRID_W
    row = jnp.broadcast_to(jnp.arange(rows, dtype=F32)[:, None], (rows, GRID_W)).reshape(n_lat)
    col = jnp.broadcast_to(jnp.arange(GRID_W, dtype=F32)[None, :], (rows, GRID_W)).reshape(n_lat)
    half = HEAD_DIM // 2
    inv = ROPE_THETA ** (-jnp.arange(0, half, 2, dtype=F32) / half)
    ang = jnp.concatenate([row[:, None] * inv, col[:, None] * inv], axis=-1)
    cos, sin = jnp.cos(ang), jnp.sin(ang)
    c64 = jnp.concatenate([cos, cos], axis=-1)
    s64 = jnp.concatenate([-sin, sin], axis=-1)
    c64 = jnp.concatenate([c64, jnp.ones((n_ctx, HEAD_DIM), F32)], axis=0)
    s64 = jnp.concatenate([s64, jnp.zeros((n_ctx, HEAD_DIM), F32)], axis=0)
    return jnp.tile(c64, (1, 2)), jnp.tile(s64, (1, 2))


def _swap_halves(v):
    lane = lax.broadcasted_iota(jnp.int32, v.shape, 1)
    return jnp.where(lane % HEAD_DIM < HEAD_DIM // 2, pltpu.roll(v, LANES - HEAD_DIM // 2, 1),
                     pltpu.roll(v, HEAD_DIM // 2, 1))


def _low_half(shape):
    return lax.broadcasted_iota(jnp.int32, shape, 1) < HEAD_DIM


def _qkv_prep(z_qkv, cos_t, sin_t, *, name):
    t_rows = z_qkv.shape[0]
    tm = 256

    def body(z_ref, c_ref, s_ref, q_ref, k_ref, v_ref):
        cv, sv = c_ref[...], s_ref[...]

        def rope(chunk):
            return chunk * cv + _swap_halves(chunk) * sv

        for ch in range(D // LANES):
            roped = rope(z_ref[:, ch * LANES:(ch + 1) * LANES])
            q_ref[:, ch * LANES:(ch + 1) * LANES] = (roped * (HEAD_DIM ** -0.5)).astype(BF16)
        low = _low_half((tm, LANES))
        for pair in range(N_KV // 2):
            for which, ref, roped in ((0, k_ref, True), (1, v_ref, False)):
                off = D + which * KV_W + pair * LANES
                chunk = z_ref[:, off:off + LANES]
                if roped:
                    chunk = rope(chunk)
                other = pltpu.roll(chunk, HEAD_DIM, 1)
                even = jnp.where(low, chunk, other)
                odd = jnp.where(low, other, chunk)
                ref[:, (2 * pair) * LANES:(2 * pair + 1) * LANES] = even.astype(BF16)
                ref[:, (2 * pair + 1) * LANES:(2 * pair + 2) * LANES] = odd.astype(BF16)

    dup_w = N_KV * LANES
    return pl.pallas_call(
        body, name=name, grid=(t_rows // tm,),
        in_specs=[_row_spec(tm, QKV_W), _row_spec(tm, LANES), _row_spec(tm, LANES)],
        out_specs=[_row_spec(tm, D), _row_spec(tm, dup_w), _row_spec(tm, dup_w)],
        out_shape=[jax.ShapeDtypeStruct((t_rows, D), BF16), jax.ShapeDtypeStruct((t_rows, dup_w), BF16),
                   jax.ShapeDtypeStruct((t_rows, dup_w), BF16)],
        compiler_params=_params(("parallel",)),
    )(z_qkv, cos_t, sin_t)


def _qkv_unprep(dq, dk, dv, cos_t, sin_t, *, name):
    t_rows = dq.shape[0]
    tm = 256

    def body(dq_ref, dk_ref, dv_ref, c_ref, s_ref, o_ref):
        cv, sv = c_ref[...], s_ref[...]

        def unrope(chunk):
            return chunk * cv + _swap_halves(chunk * sv)

        for ch in range(D // LANES):
            o_ref[:, ch * LANES:(ch + 1) * LANES] = unrope(dq_ref[:, ch * LANES:(ch + 1) * LANES]).astype(BF16)
        for pair in range(N_KV // 2):
            for which, ref, roped in ((0, dk_ref, True), (1, dv_ref, False)):
                chunk = ref[:, pair * LANES:(pair + 1) * LANES]
                if roped:
                    chunk = unrope(chunk)
                off = D + which * KV_W + pair * LANES
                o_ref[:, off:off + LANES] = chunk.astype(BF16)

    return pl.pallas_call(
        body, name=name, grid=(t_rows // tm,),
        in_specs=[_row_spec(tm, D), _row_spec(tm, KV_W), _row_spec(tm, KV_W), _row_spec(tm, LANES),
                  _row_spec(tm, LANES)],
        out_specs=_row_spec(tm, QKV_W), out_shape=jax.ShapeDtypeStruct((t_rows, QKV_W), BF16),
        compiler_params=_params(("parallel",)),
    )(dq, dk, dv, cos_t, sin_t)


def _attn_specs(n_lat, n_ctx):
    nb = n_lat // BLK
    dup_w = N_KV * LANES

    def ws(j):
        return jnp.clip(j - 1, 0, nb - 3)

    win = [pl.BlockSpec((BLK, dup_w), functools.partial(lambda j, o: (ws(j) + o, 0), o=o)) for o in range(3)]
    ctx = pl.BlockSpec((n_ctx, dup_w), lambda j: (n_lat // n_ctx, 0))
    return nb, ws, win, ctx


def _attn_bias(j, ws_j, nb, n_ctx):
    n_keys = 3 * BLK + n_ctx
    row = lax.broadcasted_iota(jnp.int32, (BLK, n_keys), 0)
    col = lax.broadcasted_iota(jnp.int32, (BLK, n_keys), 1)
    rel = (ws_j - j) * BLK + col - row
    valid = (col >= 3 * BLK) | ((jnp.abs(rel) <= WINDOW) & (j < nb))
    bias = jnp.where(valid, 0.0, NEG)
    return jnp.concatenate([bias] * GRP, axis=0)


def _attn_probs(q_ref, kk, kh, bias, sink_ref):
    low = _low_half((BLK, LANES))
    qs = []
    for g in range(GRP):
        h = GRP * kh + g
        chunk = q_ref[:, (h // 2) * LANES:(h // 2 + 1) * LANES]
        qs.append(jnp.where(low if h % 2 == 0 else ~low, chunk, jnp.zeros_like(chunk)))
    qs = jnp.concatenate(qs, axis=0)
    s = lax.dot_general(qs, kk, (((1,), (1,)), ((), ())), preferred_element_type=F32) + bias
    snk = jnp.concatenate(
        [jnp.broadcast_to(jnp.max(sink_ref[GRP * kh + g:GRP * kh + g + 1, :], axis=1, keepdims=True), (BLK, 1))
         for g in range(GRP)], axis=0)
    m = jnp.maximum(jnp.max(s, axis=-1, keepdims=True), snk)
    p = jnp.exp(s - m)
    p_snk = jnp.exp(snk - m)
    inv = 1.0 / (jnp.sum(p, axis=-1, keepdims=True) + p_snk)
    return qs, p, p_snk, inv


class _Hosted:
    def __init__(self, arrays, out_shapes, scratch, start, finish):
        self.arrays, self.out_shapes, self.scratch, self.start, self.finish = arrays, out_shapes, scratch, start, finish


_NO_EXCHANGE = _Hosted([], [], [], None, None)


def _split_refs(refs, n_in, n_out, n_scratch, hosted):
    hi, ho, hs = len(hosted.arrays), len(hosted.out_shapes), len(hosted.scratch)
    a = n_in + hi
    b = a + n_out + ho
    ins, h_ins = refs[:n_in], refs[n_in:a]
    outs, h_outs = refs[a:a + n_out], refs[a + n_out:b]
    scr, h_scr = refs[b:b + n_scratch], refs[b + n_scratch:b + n_scratch + hs]
    return ins, outs, scr, (h_ins, h_outs, h_scr)


def _run_hosted(hosted, h_refs, step, n_steps):
    if hosted.start is None:
        return

    flat = [r for group in h_refs for r in group]

    @pl.when(step == 0)
    def _():
        hosted.start(*flat)

    @pl.when(step == n_steps - 1)
    def _():
        hosted.finish(*flat)


def _attention_fwd(q, kd, vd, sink_b, n_lat, n_ctx, *, name, hosted=_NO_EXCHANGE):
    t_rows = q.shape[0]
    nb, ws, win, ctx = _attn_specs(n_lat, n_ctx)
    n_steps = t_rows // BLK

    def body(*refs):
        ins, outs, _, h_refs = _split_refs(refs, 10, 1, 0, hosted)
        q_ref, k0, k1, k2, kc, v0, v1, v2, vc, sink_ref = ins
        o_ref, = outs
        j = pl.program_id(0)
        _run_hosted(hosted, h_refs, j, n_steps)
        ws_j = ws(j)
        low = _low_half((BLK, LANES))
        bias = _attn_bias(j, ws_j, nb, n_ctx)
        for kh in range(N_KV):
            sl = slice(kh * LANES, (kh + 1) * LANES)
            kk = jnp.concatenate([k0[:, sl], k1[:, sl], k2[:, sl], kc[:, sl]], axis=0)
            vv = jnp.concatenate([v0[:, sl], v1[:, sl], v2[:, sl], vc[:, sl]], axis=0)
            _, p, _, inv = _attn_probs(q_ref, kk, kh, bias, sink_ref)
            o = jnp.dot(p.astype(BF16), vv, preferred_element_type=F32) * inv
            for half in range(2):
                even = o[(2 * half) * BLK:(2 * half + 1) * BLK]
                odd = o[(2 * half + 1) * BLK:(2 * half + 2) * BLK]
                ch = 2 * kh + half
                o_ref[:, ch * LANES:(ch + 1) * LANES] = jnp.where(low, even, odd).astype(BF16)

    outs = pl.pallas_call(
        body, name=name, grid=(n_steps,),
        in_specs=[_row_spec(BLK, D)] + win + [ctx] + win + [ctx] + [_full_spec((N_HEADS, LANES))]
        + [ANY] * len(hosted.arrays),
        out_specs=[_row_spec(BLK, D)] + [ANY] * len(hosted.out_shapes),
        out_shape=[jax.ShapeDtypeStruct((t_rows, D), BF16)] + list(hosted.out_shapes),
        scratch_shapes=list(hosted.scratch),
        compiler_params=_params(("arbitrary",)),
    )(q, kd, kd, kd, kd, vd, vd, vd, vd, sink_b, *hosted.arrays)
    return outs[0], outs[1:]


def _attention_bwd(q, kd, vd, sink_b, dy, n_lat, n_ctx, *, name, hosted=_NO_EXCHANGE):
    t_rows = q.shape[0]
    nb, ws, win, ctx = _attn_specs(n_lat, n_ctx)
    n_steps = t_rows // BLK

    def body(*refs):
        ins, outs, scr, h_refs = _split_refs(refs, 11, 4, 3, hosted)
        q_ref, k0, k1, k2, kc, v0, v1, v2, vc, sink_ref, dy_ref = ins
        dq_ref, dk_hbm, dv_hbm, ds_ref = outs
        dk_acc, dv_acc, sem = scr
        j = pl.program_id(0)
        _run_hosted(hosted, h_refs, j, n_steps)
        ws_j = ws(j)

        @pl.when(j == 0)
        def _():
            dk_acc[...] = jnp.zeros_like(dk_acc)
            dv_acc[...] = jnp.zeros_like(dv_acc)
            ds_ref[...] = jnp.zeros_like(ds_ref)

        low = _low_half((BLK, LANES))
        low_keys = _low_half((3 * BLK + n_ctx, LANES))
        win_start = pl.multiple_of(ws_j * BLK, BLK)
        scale = HEAD_DIM ** -0.5
        dk_heads, dv_heads = [], []
        bias = _attn_bias(j, ws_j, nb, n_ctx)
        for kh in range(N_KV):
            sl = slice(kh * LANES, (kh + 1) * LANES)
            kk = jnp.concatenate([k0[:, sl], k1[:, sl], k2[:, sl], kc[:, sl]], axis=0)
            vv = jnp.concatenate([v0[:, sl], v1[:, sl], v2[:, sl], vc[:, sl]], axis=0)
            qs, p, p_snk, inv = _attn_probs(q_ref, kk, kh, bias, sink_ref)
            dos = []
            for g in range(GRP):
                h = GRP * kh + g
                chunk = dy_ref[:, (h // 2) * LANES:(h // 2 + 1) * LANES]
                dos.append(jnp.where(low if h % 2 == 0 else ~low, chunk, jnp.zeros_like(chunk)).astype(BF16))
            dos = jnp.concatenate(dos, axis=0)
            dp = lax.dot_general(dos, vv, (((1,), (1,)), ((), ())), preferred_element_type=F32)
            dsum = jnp.sum(p * dp, axis=-1, keepdims=True) * inv
            ds = (p * ((dp - dsum) * inv)).astype(BF16)
            snk_term = p_snk * inv * dsum
            for g in range(GRP):
                contrib = -jnp.sum(snk_term[g * BLK:(g + 1) * BLK], axis=0, keepdims=True)
                ds_ref[GRP * kh + g:GRP * kh + g + 1, :] += jnp.broadcast_to(contrib, (1, LANES))
            dqs = jnp.dot(ds, kk, preferred_element_type=F32) * scale
            for half in range(2):
                even = dqs[(2 * half) * BLK:(2 * half + 1) * BLK]
                odd = dqs[(2 * half + 1) * BLK:(2 * half + 2) * BLK]
                ch = 2 * kh + half
                dq_ref[:, ch * LANES:(ch + 1) * LANES] = jnp.where(low, even, odd)
            dkk = lax.dot_general(ds, qs, (((0,), (0,)), ((), ())), preferred_element_type=F32)
            dvv = lax.dot_general((p * inv).astype(BF16), dos, (((0,), (0,)), ((), ())), preferred_element_type=F32)
            dk_heads.append(dkk + pltpu.roll(dkk, HEAD_DIM, 1))
            dv_heads.append(dvv + pltpu.roll(dvv, HEAD_DIM, 1))
        for pair in range(N_KV // 2):
            sl = slice(pair * LANES, (pair + 1) * LANES)
            for acc, heads in ((dk_acc, dk_heads), (dv_acc, dv_heads)):
                both = jnp.where(low_keys, heads[2 * pair], heads[2 * pair + 1])
                acc[pl.ds(win_start, 3 * BLK), sl] += both[:3 * BLK]
                acc[n_lat:n_lat + n_ctx, sl] += both[3 * BLK:]

        @pl.when(j == n_steps - 1)
        def _():
            ck = pltpu.make_async_copy(dk_acc, dk_hbm, sem.at[0])
            cv = pltpu.make_async_copy(dv_acc, dv_hbm, sem.at[1])
            ck.start()
            cv.start()
            ck.wait()
            cv.wait()

    outs = pl.pallas_call(
        body, name=name, grid=(n_steps,),
        in_specs=[_row_spec(BLK, D)] + win + [ctx] + win + [ctx] + [_full_spec((N_HEADS, LANES)), _row_spec(BLK, D)]
        + [ANY] * len(hosted.arrays),
        out_specs=[_row_spec(BLK, D), ANY, ANY, _full_spec((N_HEADS, LANES))] + [ANY] * len(hosted.out_shapes),
        out_shape=[jax.ShapeDtypeStruct((t_rows, D), F32), jax.ShapeDtypeStruct((t_rows, KV_W), F32),
                   jax.ShapeDtypeStruct((t_rows, KV_W), F32), jax.ShapeDtypeStruct((N_HEADS, LANES), F32)]
        + list(hosted.out_shapes),
        scratch_shapes=[pltpu.VMEM((t_rows, KV_W), F32), pltpu.VMEM((t_rows, KV_W), F32),
                        pltpu.SemaphoreType.DMA((2,))] + list(hosted.scratch),
        compiler_params=_params(("arbitrary",)),
    )(q, kd, kd, kd, kd, vd, vd, vd, vd, sink_b, dy, *hosted.arrays)
    return outs[0], outs[1], outs[2], outs[3], outs[4:]


_GELU_K = math.sqrt(2.0 / math.pi)


def _gelu(v):
    return jax.nn.gelu(v)


def _gelu_and_grad(v):
    t = jnp.tanh(_GELU_K * (v + 0.044715 * (v * v * v)))
    cdf = 0.5 * (1.0 + t)
    return v * cdf, cdf + 0.5 * v * (1.0 - t * t) * _GELU_K * (1.0 + 3.0 * 0.044715 * v * v)


def _gating_fwd(z_a, w_s, b_sb, g_v, *, name, hosted=None):
    hosted = hosted or _NO_EXCHANGE
    t_rows = z_a.shape[0]
    n_steps = t_rows // BLK

    def body(*refs):
        ins, outs, _, h_refs = _split_refs(refs, 4, 1, 0, hosted)
        z_ref, w_ref, b_ref, g_ref = ins
        o_ref, = outs
        _run_hosted(hosted, h_refs, pl.program_id(0), n_steps)
        u = _gelu(z_ref[:, :D])
        v = _gelu(z_ref[:, D:])
        vn = v * lax.rsqrt(jnp.mean(v * v, axis=-1, keepdims=True) + EPS) * g_ref[...]
        for g in range(A_GROUPS):
            sl = slice(g * LANES, (g + 1) * LANES)
            mixed = jnp.dot(w_ref[g].astype(BF16), vn[:, sl].astype(BF16), preferred_element_type=F32) + b_ref[g]
            o_ref[:, sl] = (u[:, sl] * mixed).astype(BF16)

    outs = pl.pallas_call(
        body, name=name, grid=(n_steps,),
        in_specs=[_row_spec(BLK, A_COLS), _full_spec((A_GROUPS, BLK, BLK)), _full_spec((A_GROUPS, BLK, LANES)),
                  _full_spec((1, D))] + [ANY] * len(hosted.arrays),
        out_specs=[_row_spec(BLK, D)] + [ANY] * len(hosted.out_shapes),
        out_shape=[jax.ShapeDtypeStruct((t_rows, D), BF16)] + list(hosted.out_shapes),
        scratch_shapes=list(hosted.scratch),
        compiler_params=_params(("arbitrary",)),
    )(z_a, w_s, b_sb, g_v, *hosted.arrays)
    return outs[0], outs[1:]


def _gating_bwd(z_a, dy, w_s, b_sb, g_v, *, name, hosted=None):
    hosted = hosted or _NO_EXCHANGE
    t_rows = z_a.shape[0]
    n_steps = t_rows // BLK

    def body(*refs):
        ins, outs, _, h_refs = _split_refs(refs, 5, 4, 0, hosted)
        z_ref, dy_ref, w_ref, b_ref, g_ref = ins
        dz_ref, dw_ref, db_ref, st_ref = outs
        i = pl.program_id(0)
        _run_hosted(hosted, h_refs, i, n_steps)

        @pl.when(i == 0)
        def _():
            dw_ref[...] = jnp.zeros_like(dw_ref)
            db_ref[...] = jnp.zeros_like(db_ref)
            st_ref[...] = jnp.zeros_like(st_ref)

        u, du_dz = _gelu_and_grad(z_ref[:, :D])
        v, dv_dz = _gelu_and_grad(z_ref[:, D:])
        gv = g_ref[...]
        rstd = lax.rsqrt(jnp.mean(v * v, axis=-1, keepdims=True) + EPS)
        vh = v * rstd
        vn = vh * gv
        dyv = dy_ref[...]
        dvn = []
        for g in range(A_GROUPS):
            sl = slice(g * LANES, (g + 1) * LANES)
            wg = w_ref[g].astype(BF16)
            vg = vn[:, sl].astype(BF16)
            mixed = jnp.dot(wg, vg, preferred_element_type=F32) + b_ref[g]
            dz_ref[:, sl] = (dyv[:, sl] * mixed * du_dz[:, sl]).astype(BF16)
            dmixed = dyv[:, sl] * u[:, sl]
            dmb = dmixed.astype(BF16)
            dvn.append(lax.dot_general(wg, dmb, (((0,), (0,)), ((), ())), preferred_element_type=F32))
            dw_ref[g] += lax.dot_general(dmb, vg, (((1,), (1,)), ((), ())), preferred_element_type=F32)
            db_ref[g] += jnp.broadcast_to(jnp.sum(dmixed, axis=-1, keepdims=True), (BLK, LANES))
        dvn = jnp.concatenate(dvn, axis=1)
        st_ref[...] += _rows8([(0, _colsum(dvn * vh))], D)
        e = dvn * gv
        dv = rstd * (e - vh * jnp.mean(e * vh, axis=-1, keepdims=True))
        dz_ref[:, D:] = (dv * dv_dz).astype(BF16)

    outs = pl.pallas_call(
        body, name=name, grid=(n_steps,),
        in_specs=[_row_spec(BLK, A_COLS), _row_spec(BLK, D), _full_spec((A_GROUPS, BLK, BLK)),
                  _full_spec((A_GROUPS, BLK, LANES)), _full_spec((1, D))] + [ANY] * len(hosted.arrays),
        out_specs=[_row_spec(BLK, A_COLS), _full_spec((A_GROUPS, BLK, BLK)), _full_spec((A_GROUPS, BLK, LANES)),
                   _full_spec((SUBLANES, D))] + [ANY] * len(hosted.out_shapes),
        out_shape=[jax.ShapeDtypeStruct((t_rows, A_COLS), BF16), jax.ShapeDtypeStruct((A_GROUPS, BLK, BLK), F32),
                   jax.ShapeDtypeStruct((A_GROUPS, BLK, LANES), F32), jax.ShapeDtypeStruct((SUBLANES, D), F32)]
        + list(hosted.out_shapes),
        scratch_shapes=list(hosted.scratch),
        compiler_params=_params(("arbitrary",)),
    )(z_a, dy, w_s, b_sb, g_v, *hosted.arrays)
    return outs[0], outs[1], outs[2], outs[3], outs[4:]


def _sconv_fwd(z_b, w8, n_lat, *, name):
    t_rows = z_b.shape[0]
    tm = 256
    prev, nxt = _halo_specs(tm, B_COLS, t_rows)

    def body(z_ref, zp_ref, zn_ref, w_ref, o_ref):
        p = z_ref[:, D:2 * D] * z_ref[:, 2 * D:]
        pp = zp_ref[:, D:2 * D] * zp_ref[:, 2 * D:]
        pn = zn_ref[:, D:2 * D] * zn_ref[:, 2 * D:]
        up, dn = _shift_rows(p, pp, pn, n_lat, t_rows, tm)
        conv = w_ref[0:1, :] * up + w_ref[1:2, :] * p + w_ref[2:3, :] * dn
        o_ref[...] = (z_ref[:, :D] * conv).astype(BF16)

    return pl.pallas_call(
        body, name=name, grid=(t_rows // tm,),
        in_specs=[_row_spec(tm, B_COLS), prev, nxt, _full_spec((SUBLANES, D))],
        out_specs=_row_spec(tm, D), out_shape=jax.ShapeDtypeStruct((t_rows, D), BF16),
        compiler_params=_params(("parallel",)),
    )(z_b, z_b, z_b, w8)


def _sconv_bwd(z_b, dy, w8, n_lat, *, name):
    t_rows = z_b.shape[0]
    tm = 256
    prev, nxt = _halo_specs(tm, B_COLS, t_rows)
    dprev, dnxt = _halo_specs(tm, D, t_rows)

    def body(z_ref, zp_ref, zn_ref, dy_ref, dyp_ref, dyn_ref, w_ref, dz_ref, st_ref):
        i = pl.program_id(0)
        bg, cg, hb = z_ref[:, :D], z_ref[:, D:2 * D], z_ref[:, 2 * D:]
        p = cg * hb
        pp = zp_ref[:, D:2 * D] * zp_ref[:, 2 * D:]
        pn = zn_ref[:, D:2 * D] * zn_ref[:, 2 * D:]
        up, dn = _shift_rows(p, pp, pn, n_lat, t_rows, tm)
        w0, w1, w2 = w_ref[0:1, :], w_ref[1:2, :], w_ref[2:3, :]
        conv = w0 * up + w1 * p + w2 * dn
        dyv = dy_ref[...]
        dz_ref[:, :D] = (dyv * conv).astype(BF16)
        dcv = dyv * bg
        dcv_up, dcv_dn = _shift_rows(dcv, dyp_ref[...] * zp_ref[:, :D], dyn_ref[...] * zn_ref[:, :D], n_lat, t_rows, tm)
        dp = w0 * dcv_dn + w1 * dcv + w2 * dcv_up
        dz_ref[:, D:2 * D] = (dp * hb).astype(BF16)
        dz_ref[:, 2 * D:] = (dp * cg).astype(BF16)
        upd = _rows8([(0, _colsum(dcv * up)), (1, _colsum(dcv * p)), (2, _colsum(dcv * dn))], D)

        @pl.when(i == 0)
        def _():
            st_ref[...] = upd

        @pl.when(i > 0)
        def _():
            st_ref[...] += upd

    return pl.pallas_call(
        body, name=name, grid=(t_rows // tm,),
        in_specs=[_row_spec(tm, B_COLS), prev, nxt, _row_spec(tm, D), dprev, dnxt, _full_spec((SUBLANES, D))],
        out_specs=[_row_spec(tm, B_COLS), _full_spec((SUBLANES, D))],
        out_shape=[jax.ShapeDtypeStruct((t_rows, B_COLS), BF16), jax.ShapeDtypeStruct((SUBLANES, D), F32)],
        compiler_params=_params(("arbitrary",)),
    )(z_b, z_b, z_b, dy, dy, dy, w8)


def _merge_fwd(t0, t1, t2, z_g, b_gate, *, name):
    t_rows = t0.shape[0]
    tm = 256

    def body(t0_ref, t1_ref, t2_ref, z_ref, b_ref, o_ref):
        acc = None
        for k, t_ref in enumerate((t0_ref, t1_ref, t2_ref)):
            gate = jax.nn.sigmoid(z_ref[:, k * D:(k + 1) * D] + b_ref[:, k * D:(k + 1) * D])
            term = gate * t_ref[...]
            acc = term if acc is None else acc + term
        o_ref[...] = acc.astype(BF16)

    return pl.pallas_call(
        body, name=name, grid=(t_rows // tm,),
        in_specs=[_row_spec(tm, D)] * 3 + [_row_spec(tm, G_COLS), _full_spec((1, G_COLS))],
        out_specs=_row_spec(tm, D), out_shape=jax.ShapeDtypeStruct((t_rows, D), BF16),
        compiler_params=_params(("parallel",)),
    )(t0, t1, t2, z_g, b_gate)


def _merge_bwd(dmerged, t0, t1, t2, z_g, b_gate, *, name):
    t_rows = t0.shape[0]
    tm = 256

    def body(dm_ref, t0_ref, t1_ref, t2_ref, z_ref, b_ref, d0_ref, d1_ref, d2_ref, dz_ref, st_ref):
        i = pl.program_id(0)
        dm = dm_ref[...]
        sums = []
        for k, (t_ref, d_ref) in enumerate(((t0_ref, d0_ref), (t1_ref, d1_ref), (t2_ref, d2_ref))):
            gate = jax.nn.sigmoid(z_ref[:, k * D:(k + 1) * D] + b_ref[:, k * D:(k + 1) * D])
            d_ref[...] = (dm * gate).astype(BF16)
            dzg = dm * t_ref[...] * gate * (1.0 - gate)
            dz_ref[:, k * D:(k + 1) * D] = dzg.astype(BF16)
            sums.append(_colsum(dzg))
        upd = _rows8([(0, jnp.concatenate(sums, axis=1))], G_COLS)

        @pl.when(i == 0)
        def _():
            st_ref[...] = upd

        @pl.when(i > 0)
        def _():
            st_ref[...] += upd

    return pl.pallas_call(
        body, name=name, grid=(t_rows // tm,),
        in_specs=[_row_spec(tm, D)] * 4 + [_row_spec(tm, G_COLS), _full_spec((1, G_COLS))],
        out_specs=[_row_spec(tm, D)] * 3 + [_row_spec(tm, G_COLS), _full_spec((SUBLANES, G_COLS))],
        out_shape=[jax.ShapeDtypeStruct((t_rows, D), BF16)] * 3
        + [jax.ShapeDtypeStruct((t_rows, G_COLS), BF16), jax.ShapeDtypeStruct((SUBLANES, G_COLS), F32)],
        compiler_params=_params(("arbitrary",)),
    )(dmerged, t0, t1, t2, z_g, b_gate)


def _ffn_mid_fwd(up, w8, n_lat, *, name, hosted=None):
    hosted = hosted or _NO_EXCHANGE
    t_rows = up.shape[0]
    tm = 256
    n_steps = t_rows // tm
    prev, nxt = _halo_specs(tm, D_FF, t_rows)

    def body(*refs):
        ins, outs, _, h_refs = _split_refs(refs, 5, 2, 0, hosted)
        a_ref, ap_ref, an_ref, g_ref, w_ref = ins
        cv_ref, f_ref = outs
        _run_hosted(hosted, h_refs, pl.program_id(0), n_steps)
        a = a_ref[...]
        au, ad = _shift_rows(a, ap_ref[...], an_ref[...], n_lat, t_rows, tm)
        cv = w_ref[0:1, :] * au + w_ref[1:2, :] * a + w_ref[2:3, :] * ad
        cv_ref[...] = cv
        f_ref[...] = (_silu(cv) * g_ref[...]).astype(BF16)

    outs = pl.pallas_call(
        body, name=name, grid=(n_steps,),
        in_specs=[_row_spec(tm, D_FF), prev, nxt, _row_spec(tm, D_FF, 1), _full_spec((SUBLANES, D_FF))]
        + [ANY] * len(hosted.arrays),
        out_specs=[_row_spec(tm, D_FF), _row_spec(tm, D_FF)] + [ANY] * len(hosted.out_shapes),
        out_shape=[jax.ShapeDtypeStruct((t_rows, D_FF), F32), jax.ShapeDtypeStruct((t_rows, D_FF), BF16)]
        + list(hosted.out_shapes),
        scratch_shapes=list(hosted.scratch),
        compiler_params=_params(("arbitrary",)),
    )(up, up, up, up, w8, *hosted.arrays)
    return outs[0], outs[1], outs[2:]


def _ffn_mid_bwd(up, cv, df, w8, n_lat, *, name):
    t_rows = up.shape[0]
    tm = 128
    prev, nxt = _halo_specs(tm, D_FF, t_rows)
    gprev, gnxt = _halo_specs(tm, D_FF, t_rows, 1)

    def body(a_ref, ap_ref, an_ref, g_ref, gp_ref, gn_ref, cv_ref, cp_ref, cn_ref, df_ref, dfp_ref, dfn_ref,
             w_ref, o_ref, st_ref):
        i = pl.program_id(0)
        a = a_ref[...]
        au, ad = _shift_rows(a, ap_ref[...], an_ref[...], n_lat, t_rows, tm)
        cvv = cv_ref[...]
        dfv = df_ref[...]
        o_ref[:, D_FF:] = (dfv * _silu(cvv)).astype(BF16)
        dcv = dfv * g_ref[...] * _dsilu(cvv)
        dcv_p = dfp_ref[...] * gp_ref[...] * _dsilu(cp_ref[...])
        dcv_n = dfn_ref[...] * gn_ref[...] * _dsilu(cn_ref[...])
        du, dd = _shift_rows(dcv, dcv_p, dcv_n, n_lat, t_rows, tm)
        o_ref[:, :D_FF] = (w_ref[0:1, :] * dd + w_ref[1:2, :] * dcv + w_ref[2:3, :] * du).astype(BF16)
        upd = _rows8([(0, _colsum(dcv * au)), (1, _colsum(dcv * a)), (2, _colsum(dcv * ad))], D_FF)

        @pl.when(i == 0)
        def _():
            st_ref[...] = upd

        @pl.when(i > 0)
        def _():
            st_ref[...] += upd

    row = _row_spec(tm, D_FF)
    return pl.pallas_call(
        body, name=name, grid=(t_rows // tm,),
        in_specs=[row, prev, nxt, _row_spec(tm, D_FF, 1), gprev, gnxt, row, prev, nxt, row, prev, nxt,
                  _full_spec((SUBLANES, D_FF))],
        out_specs=[_row_spec(tm, 2 * D_FF), _full_spec((SUBLANES, D_FF))],
        out_shape=[jax.ShapeDtypeStruct((t_rows, 2 * D_FF), BF16), jax.ShapeDtypeStruct((SUBLANES, D_FF), F32)],
        compiler_params=_params(("arbitrary",)),
    )(up, up, up, up, up, up, cv, cv, cv, df, df, df, w8)


def _loss_head(x, target, g_final, n_lat, *, name):
    t_rows = x.shape[0]
    tm = 256
    last = n_lat // tm - 1

    def body(x_ref, t_ref, g_ref, dx_ref, st_ref):
        i = pl.program_id(0)
        is_ctx = i * tm >= n_lat
        xv = x_ref[...]
        gv = g_ref[...]
        rstd = lax.rsqrt(jnp.mean(xv * xv, axis=-1, keepdims=True) + EPS)
        rn = xv * rstd
        err = rn * gv - t_ref[...]
        dy = err / D
        e = dy * gv
        dx = rstd * (e - rn * jnp.mean(e * rn, axis=-1, keepdims=True))
        dx_ref[...] = jnp.where(is_ctx, 0.0, dx)
        keep = jnp.where(is_ctx, 0.0, 1.0)
        upd = _rows8([(0, keep * _colsum(dy * rn)), (1, keep * _colsum(err * err))], D)

        @pl.when(i == 0)
        def _():
            st_ref[...] = upd

        @pl.when(i > 0)
        def _():
            st_ref[...] += upd

    return pl.pallas_call(
        body, name=name, grid=(t_rows // tm,),
        in_specs=[_row_spec(tm, D), pl.BlockSpec((tm, D), lambda i: (jnp.minimum(i, last), 0)), _full_spec((1, D))],
        out_specs=[_row_spec(tm, D), _full_spec((SUBLANES, D))],
        out_shape=[jax.ShapeDtypeStruct((t_rows, D), F32), jax.ShapeDtypeStruct((SUBLANES, D), F32)],
        compiler_params=_params(("arbitrary",)),
    )(x, target, g_final)


def _sum_slabs(x, out_dtype, *, name):
    n_slabs, rows, cols = x.shape
    tm = _pick(rows, (432, 256, 192, 128, 64, 32, 24, 16, 8))

    def body(x_ref, o_ref):
        acc = x_ref[0].astype(F32)
        for s in range(1, n_slabs):
            acc = acc + x_ref[s].astype(F32)
        o_ref[...] = acc.astype(o_ref.dtype)

    return pl.pallas_call(
        body, name=name, grid=(rows // tm,),
        in_specs=[pl.BlockSpec((n_slabs, tm, cols), lambda i: (0, i, 0))],
        out_specs=pl.BlockSpec((tm, cols), lambda i: (i, 0)),
        out_shape=jax.ShapeDtypeStruct((rows, cols), out_dtype),
        compiler_params=_params(("parallel",)),
    )(x)


def _add_half(half_idx, a, b, *, name):
    n_slabs, rows, cols = b.shape
    tm = _pick(rows, (432, 256, 192, 128, 96, 64, 32, 16))
    per_half = rows // tm

    def body(half_ref, a_ref, b_ref, o_ref):
        o_ref[...] = (a_ref[...].astype(F32) + b_ref[...].astype(F32)).astype(BF16)

    spec = pl.BlockSpec((1, tm, cols), lambda s, i, half_ref: (s, i, 0))
    a_spec = pl.BlockSpec((1, tm, cols), lambda s, i, half_ref: (s, half_ref[0] * per_half + i, 0))
    return pl.pallas_call(
        body, name=name,
        grid_spec=pltpu.PrefetchScalarGridSpec(num_scalar_prefetch=1, grid=(n_slabs, per_half),
                                               in_specs=[a_spec, spec], out_specs=spec),
        out_shape=jax.ShapeDtypeStruct(b.shape, BF16), compiler_params=_params(("parallel", "parallel")),
    )(half_idx, a, b)


def _adamw(w, g, m, v, *, name):
    rows, cols = w.shape
    tm = _pick(rows, (128, 64, 32, 16, 8))

    def body(w_ref, g_ref, m_ref, v_ref, d_ref, nm_ref, nv_ref):
        gv = g_ref[...]
        nm = ADAM_B1 * m_ref[...] + (1.0 - ADAM_B1) * gv
        nv = ADAM_B2 * v_ref[...] + (1.0 - ADAM_B2) * jnp.square(gv)
        m_hat = nm / (1.0 - ADAM_B1 ** ADAM_STEP)
        v_hat = nv / (1.0 - ADAM_B2 ** ADAM_STEP)
        d_ref[...] = -ADAM_LR * (m_hat / (jnp.sqrt(v_hat) + ADAM_EPS) + ADAM_WD * w_ref[...])
        nm_ref[...] = nm
        nv_ref[...] = nv

    spec = pl.BlockSpec((tm, cols), lambda i: (i, 0))
    shape = jax.ShapeDtypeStruct((rows, cols), F32)
    return pl.pallas_call(
        body, name=name, grid=(rows // tm,), in_specs=[spec] * 4, out_specs=[spec] * 3, out_shape=[shape] * 3,
        compiler_params=_params(("parallel",)),
    )(w, g, m, v)


def _place():
    x, y, c = lax.axis_index("x"), lax.axis_index("y"), lax.axis_index("c")
    chips = [(1 - x, y), (x, 1 - y), (1 - x, 1 - y)]
    return x, y, c, chips


def _remote(src, dst, send_sems, recv_sems, k, to):
    return pltpu.make_async_remote_copy(src_ref=src, dst_ref=dst, send_sem=send_sems.at[k], recv_sem=recv_sems.at[k],
                                        device_id=to, device_id_type=MESH)


HALF_CHUNKS = 2


def _chunks(ref, n):
    step = ref.shape[0] // n
    tile_rows = SUBLANES if ref.dtype == F32 else 2 * SUBLANES
    assert step * n == ref.shape[0] and step % tile_rows == 0, (ref.shape, n)
    return [ref.at[pl.ds(k * step, step)] for k in range(n)]


def _half(ref, which):
    half = ref.shape[0] // 2
    return ref.at[pl.ds(pl.multiple_of(which * half, 2 * SUBLANES), half)]


def _staged_copy(src, dst, buf, sems):
    step = buf.shape[1]
    n = src.shape[0] // step
    assert n * step == src.shape[0], (src.shape, step)
    ins = [pltpu.make_async_copy(src.at[pl.ds(k * step, step)], buf.at[k % 2], sems.at[k % 2]) for k in range(n)]
    outs = [pltpu.make_async_copy(buf.at[k % 2], dst.at[pl.ds(k * step, step)], sems.at[2 + k % 2]) for k in range(n)]
    ins[0].start()
    for k in range(n):
        ins[k].wait()
        outs[k].start()
        if k + 1 < n:
            if k >= 1:
                outs[k - 1].wait()
            ins[k + 1].start()
    if n >= 2:
        outs[n - 2].wait()
    outs[n - 1].wait()


def _stage_rows(rows):
    return _pick(rows, (256, 432))


def _stage_scratch(slabs):
    return [pltpu.VMEM((2, _stage_rows(s.shape[-2]), s.shape[-1]), s.dtype) for s in slabs] + [pltpu.SemaphoreType.DMA((4,))]


N_LINK_SEMS = (N_CHIPS - 1) * HALF_CHUNKS


def _link_sems(n_groups):
    return [pltpu.SemaphoreType.DMA((n_groups * N_LINK_SEMS,)), pltpu.SemaphoreType.DMA((n_groups * N_LINK_SEMS,))]


def _sem_index(g, j, k):
    return g * N_LINK_SEMS + j * HALF_CHUNKS + k


def _gather_ici_start(*refs):
    n = (len(refs) - 2) // 2
    p_refs, o_refs, (send_sems, recv_sems) = refs[:n], refs[n:2 * n], refs[2 * n:]
    x, y, c, chips = _place()
    for g, (p_ref, o_ref) in enumerate(zip(p_refs, o_refs)):
        src = _chunks(_half(p_ref, c), HALF_CHUNKS)
        dst = _chunks(_half(o_ref.at[2 * x + y], c), HALF_CHUNKS)
        for j, chip in enumerate(chips):
            for k in range(HALF_CHUNKS):
                _remote(src[k], dst[k], send_sems, recv_sems, _sem_index(g, j, k), (*chip, c)).start()


def _gather_ici_finish(*refs):
    n = (len(refs) - 2) // 2
    p_refs, o_refs, (send_sems, recv_sems) = refs[:n], refs[n:2 * n], refs[2 * n:]
    x, y, c, chips = _place()
    for g, (p_ref, o_ref) in enumerate(zip(p_refs, o_refs)):
        src = _chunks(_half(p_ref, c), HALF_CHUNKS)
        for j, (cx, cy) in enumerate(chips):
            for k, landed in enumerate(_chunks(_half(o_ref.at[2 * cx + cy], c), HALF_CHUNKS)):
                _remote(src[k], landed, send_sems, recv_sems, _sem_index(g, j, k), (x, y, c)).wait_recv()
        for j in range(len(chips)):
            for k in range(HALF_CHUNKS):
                _remote(src[k], src[k], send_sems, recv_sems, _sem_index(g, j, k), (x, y, c)).wait_send()


def _gathered_shapes(slabs):
    return [jax.ShapeDtypeStruct((N_CHIPS,) + s.shape, s.dtype) for s in slabs]


def _gather_ici_hosted(slabs):
    return _Hosted(list(slabs), _gathered_shapes(slabs), _link_sems(len(slabs)), _gather_ici_start, _gather_ici_finish)


def _gather_ici(slabs, *, name):
    def body(*refs):
        _gather_ici_start(*refs)
        _gather_ici_finish(*refs)

    return pl.pallas_call(
        body, name=name, in_specs=[ANY] * len(slabs), out_specs=[ANY] * len(slabs),
        out_shape=_gathered_shapes(slabs), scratch_shapes=_link_sems(len(slabs)),
    )(*slabs)


def _gather_finish(partials, slabs, *, name):
    n = len(slabs)

    def body(*refs):
        p_refs, o_refs = refs[n:2 * n], refs[2 * n:3 * n]
        send_sems, recv_sems = refs[3 * n:3 * n + 2]
        bufs, loc_sems = refs[3 * n + 2:4 * n + 2], refs[4 * n + 2]
        x, y, c, chips = _place()
        sib = (x, y, 1 - c)
        passed = []
        for g, o_ref in enumerate(o_refs):
            for j, (cx, cy) in enumerate(chips):
                for k, landed in enumerate(_chunks(_half(o_ref.at[2 * cx + cy], c), HALF_CHUNKS)):
                    passed.append(_remote(landed, landed, send_sems, recv_sems, _sem_index(g, j, k), sib))
        for cp in passed:
            cp.start()
        for p_ref, o_ref, buf in zip(p_refs, o_refs, bufs):
            _staged_copy(p_ref, o_ref.at[2 * x + y], buf, loc_sems)
        for g, o_ref in enumerate(o_refs):
            for j, (cx, cy) in enumerate(chips):
                for k, landed in enumerate(_chunks(_half(o_ref.at[2 * cx + cy], 1 - c), HALF_CHUNKS)):
                    _remote(landed, landed, send_sems, recv_sems, _sem_index(g, j, k), sib).wait_recv()
        for cp in passed:
            cp.wait_send()

    return pl.pallas_call(
        body, name=name, in_specs=[ANY] * (2 * n), out_specs=[ANY] * n,
        out_shape=[jax.ShapeDtypeStruct(p.shape, p.dtype) for p in partials],
        input_output_aliases={g: g for g in range(n)}, scratch_shapes=_link_sems(n) + _stage_scratch(slabs),
        compiler_params=_params(),
    )(*partials, *slabs)


def _grad_sibling_swap(g_packs, *, name):
    n = len(g_packs)
    per_group = N_CHIPS * HALF_CHUNKS

    def body(*refs):
        g_refs, got_refs, (send_sems, recv_sems) = refs[:n], refs[n:2 * n], refs[2 * n:]
        x, y, c, _ = _place()
        sib = (x, y, 1 - c)
        swaps = [_remote(src, dst, send_sems, recv_sems, g * per_group + s * HALF_CHUNKS + k, sib)
                 for g, (g_ref, got_ref) in enumerate(zip(g_refs, got_refs))
                 for s in range(N_CHIPS)
                 for k, (src, dst) in enumerate(zip(_chunks(_half(g_ref.at[s], 1 - c), HALF_CHUNKS),
                                                    _chunks(got_ref.at[s], HALF_CHUNKS)))]
        for cp in swaps:
            cp.start()
        for cp in swaps:
            cp.wait_recv()
        for cp in swaps:
            cp.wait_send()

    return pl.pallas_call(
        body, name=name, in_specs=[ANY] * n, out_specs=[ANY] * n,
        out_shape=[jax.ShapeDtypeStruct((N_CHIPS, g.shape[1] // 2, g.shape[2]), g.dtype) for g in g_packs],
        scratch_shapes=[pltpu.SemaphoreType.DMA((n * per_group,)), pltpu.SemaphoreType.DMA((n * per_group,))],
    )(*g_packs)


def _grad_ici_refs(refs):
    n = (len(refs) - 3) // 3
    return refs[:n], refs[n:2 * n], refs[2 * n], refs[2 * n + 1], refs[2 * n + 2:3 * n + 2], refs[3 * n + 2]


def _grad_ici_start(*refs):
    s_refs, o_refs, send_sems, recv_sems, _, _ = _grad_ici_refs(refs)
    x, y, c, chips = _place()
    for g, (s_ref, o_ref) in enumerate(zip(s_refs, o_refs)):
        for j, (cx, cy) in enumerate(chips):
            pairs = zip(_chunks(s_ref.at[2 * cx + cy], HALF_CHUNKS), _chunks(o_ref.at[2 * x + y], HALF_CHUNKS))
            for k, (src, dst) in enumerate(pairs):
                _remote(src, dst, send_sems, recv_sems, _sem_index(g, j, k), (cx, cy, c)).start()


def _grad_ici_finish(*refs):
    s_refs, o_refs, send_sems, recv_sems, bufs, loc_sems = _grad_ici_refs(refs)
    x, y, c, chips = _place()
    me = 2 * x + y
    for s_ref, o_ref, buf in zip(s_refs, o_refs, bufs):
        _staged_copy(s_ref.at[me], o_ref.at[me], buf, loc_sems)
    for g, (s_ref, o_ref) in enumerate(zip(s_refs, o_refs)):
        for j, (cx, cy) in enumerate(chips):
            for k, landed in enumerate(_chunks(o_ref.at[2 * cx + cy], HALF_CHUNKS)):
                _remote(landed, landed, send_sems, recv_sems, _sem_index(g, j, k), (x, y, c)).wait_recv()
        for j, (cx, cy) in enumerate(chips):
            for k, sent in enumerate(_chunks(s_ref.at[2 * cx + cy], HALF_CHUNKS)):
                _remote(sent, sent, send_sems, recv_sems, _sem_index(g, j, k), (x, y, c)).wait_send()


def _grad_ici_hosted(sums):
    return _Hosted(list(sums), [jax.ShapeDtypeStruct(s.shape, s.dtype) for s in sums],
                   _link_sems(len(sums)) + _stage_scratch(sums), _grad_ici_start, _grad_ici_finish)


def _grad_ici(sums, *, name):
    def body(*refs):
        _grad_ici_start(*refs)
        _grad_ici_finish(*refs)

    n = len(sums)
    return pl.pallas_call(
        body, name=name, in_specs=[ANY] * n, out_specs=[ANY] * n,
        out_shape=[jax.ShapeDtypeStruct(s.shape, s.dtype) for s in sums],
        scratch_shapes=_link_sems(n) + _stage_scratch(sums), compiler_params=_params(),
    )(*sums)


def _grad_sibling_share(totals, *, name):
    n = len(totals)
    n_ch = HALF_CHUNKS

    def body(*refs):
        t_refs, o_refs = refs[:n], refs[n:2 * n]
        send_sems, recv_sems = refs[2 * n:2 * n + 2]
        bufs, loc_sems = refs[2 * n + 2:3 * n + 2], refs[3 * n + 2]
        x, y, c, _ = _place()
        sib = (x, y, 1 - c)
        sends = [_remote(src, dst, send_sems, recv_sems, g * n_ch + k, sib)
                 for g, (t_ref, o_ref) in enumerate(zip(t_refs, o_refs))
                 for k, (src, dst) in enumerate(zip(_chunks(t_ref, n_ch), _chunks(_half(o_ref, c), n_ch)))]
        for cp in sends:
            cp.start()
        for t_ref, o_ref, buf in zip(t_refs, o_refs, bufs):
            _staged_copy(t_ref, _half(o_ref, c), buf, loc_sems)
        for g, o_ref in enumerate(o_refs):
            for k, landed in enumerate(_chunks(_half(o_ref, 1 - c), n_ch)):
                _remote(landed, landed, send_sems, recv_sems, g * n_ch + k, sib).wait_recv()
        for cp in sends:
            cp.wait_send()

    return pl.pallas_call(
        body, name=name, in_specs=[ANY] * n, out_specs=[ANY] * n,
        out_shape=[jax.ShapeDtypeStruct((2 * t.shape[0], t.shape[1]), t.dtype) for t in totals],
        scratch_shapes=[pltpu.SemaphoreType.DMA((n * n_ch,)), pltpu.SemaphoreType.DMA((n * n_ch,))] + _stage_scratch(totals),
        compiler_params=_params(),
    )(*totals)


def _allgather8(v, *, name):
    rows, cols = v.shape

    def body(v_ref, o_ref, send_sems, recv_sems, loc_sem):
        x, y, c, chips = _place()
        sib = (x, y, 1 - c)

        def slot(px, py, pc):
            return o_ref.at[4 * px + 2 * py + pc]

        local = pltpu.make_async_copy(v_ref, slot(x, y, c), loc_sem.at[0])
        local.start()
        first = [_remote(v_ref, slot(x, y, c), send_sems, recv_sems, 0, sib)]
        first += [_remote(v_ref, slot(x, y, c), send_sems, recv_sems, 1 + j, (*chip, c)) for j, chip in enumerate(chips)]
        for cp in first:
            cp.start()
        passed = [_remote(slot(*chip, c), slot(*chip, c), send_sems, recv_sems, 4 + j, sib)
                  for j, chip in enumerate(chips)]
        for j, chip in enumerate(chips):
            _remote(v_ref, slot(*chip, c), send_sems, recv_sems, 1 + j, sib).wait_recv()
            passed[j].start()
        _remote(v_ref, slot(x, y, 1 - c), send_sems, recv_sems, 0, sib).wait_recv()
        for j, chip in enumerate(chips):
            _remote(v_ref, slot(*chip, 1 - c), send_sems, recv_sems, 4 + j, sib).wait_recv()
        for cp in first + passed:
            cp.wait_send()
        local.wait()

    return pl.pallas_call(
        body, name=name, in_specs=[ANY], out_specs=ANY, out_shape=jax.ShapeDtypeStruct((N_DEV, rows, cols), v.dtype),
        scratch_shapes=[pltpu.SemaphoreType.DMA((7,)), pltpu.SemaphoreType.DMA((7,)), pltpu.SemaphoreType.DMA((1,))],
    )(v)


_BIG = (("w_mod", (D, 6 * D), 1), ("w_in", (D, IN_W), 1), ("w_branch", (3 * D, D), None), ("w_out", (D, D), 0),
        ("w_up", (D, 2 * D_FF), 1), ("w_down", (D_FF, D), 0))
_COL_SHARDED = ("w_mod", "w_in", "w_up")
_ROW_SHARDED = (("w_branch", 3 * D // N_CHIPS), ("w_out", D // N_CHIPS), ("w_down", D_FF // N_CHIPS))


def _pack_shards(shards, layer):
    rows = jnp.concatenate([shards[n][layer].reshape(r, D) for n, r in _ROW_SHARDED], axis=0)
    return [shards[n][layer] for n in _COL_SHARDED] + [rows]


def _unpack_cols(blk):
    return blk.transpose(1, 0, 2).reshape(blk.shape[1], N_CHIPS * blk.shape[2])


def _unpack_rows(stack):
    out, off = {}, 0
    for name, r in _ROW_SHARDED:
        blk = stack[:, off:off + r, :]
        off += r
        if name == "w_branch":
            out[name] = blk.reshape(N_CHIPS, 3, D // N_CHIPS, D).transpose(1, 0, 2, 3).reshape(3, D, D)
        else:
            out[name] = blk.reshape(N_CHIPS * r, D)
    return out


def _unpack_full(gathered):
    out = {name: _unpack_cols(blk) for name, blk in zip(_COL_SHARDED, gathered)}
    out.update(_unpack_rows(gathered[-1]))
    return out


def _pack_grad_cols(g):
    return g.reshape(g.shape[0], N_CHIPS, g.shape[1] // N_CHIPS).transpose(1, 0, 2)


def _pack_grad_rows(grads):
    parts = []
    for name, r in _ROW_SHARDED:
        g = grads[name]
        if name == "w_branch":
            g = g.reshape(3, N_CHIPS, D // N_CHIPS, D).transpose(1, 0, 2, 3)
        parts.append(g.reshape(N_CHIPS, r, D))
    return jnp.concatenate(parts, axis=1)


def _pack_grads(grads):
    return [_pack_grad_cols(grads[n]) for n in _COL_SHARDED] + [_pack_grad_rows(grads)]


def _unpack_shards(totals, like):
    out = {n: jnp.stack([totals[l][g] for l in range(DEPTH)]) for g, n in enumerate(_COL_SHARDED)}
    off = 0
    for name, r in _ROW_SHARDED:
        out[name] = jnp.stack([totals[l][-1][off:off + r] for l in range(DEPTH)]).reshape(like[name].shape)
        off += r
    return out


def _pad_rows(v, rows):
    return jnp.concatenate([v, jnp.zeros((rows - v.shape[0],) + v.shape[1:], v.dtype)], axis=0)


def _local_step(x_tok, target, c_vec, c_ctx, wfull, small, n_lat, n_ctx):
    ctx = _step_context(c_vec, c_ctx, n_lat, n_ctx)
    saved = []
    xs = x_tok
    for l in range(DEPTH):
        xs, s, _ = _layer_fwd(l, xs, wfull[l], {k: v[l] for k, v in small.items() if k != "g_final"}, ctx)
        saved.append(s)
    dx, sq_err, d_g_final = _loss_bwd(xs, target, small["g_final"], n_lat)
    wgrads, lgrads, d_a128 = [None] * DEPTH, [None] * DEPTH, [None] * DEPTH
    for l in reversed(range(DEPTH)):
        dx, wgrads[l], lgrads[l], d_a128[l], _ = _layer_bwd(l, saved[l], wfull[l], dx, ctx)
    return sq_err, dx, wgrads, _small_grads(lgrads, d_a128, d_g_final, ctx)


def _step_context(c_vec, c_ctx, n_lat, n_ctx):
    cos_t, sin_t = _rope_tables(n_lat, n_ctx)
    a_in = _pad_rows(jnp.stack([c_vec, c_ctx]), LANES)
    a128 = _small(_silu, (LANES, D), a_in, name="cond_silu")
    return dict(cos_t=cos_t, sin_t=sin_t, a_in=a_in, a128=a128, n_lat=n_lat, n_ctx=n_ctx)


def _loss_bwd(xs, target, g_final, n_lat):
    dx, st = _loss_head(xs, target, g_final[None, :], n_lat, name="loss_head")
    return dx, st[1], st[0]


def _small_grads(lgrads, d_a128, d_g_final, ctx):
    d_cond = _small(lambda a, b, cin: (a + b) * _dsilu(cin), (LANES, D), d_a128[0], d_a128[1], ctx["a_in"],
                    name="cond_bwd")
    out = {k: jnp.stack([lgrads[l][k] for l in range(DEPTH)]) for k in lgrads[0]}
    out["c_ctx"] = d_cond[1]
    out["g_final"] = d_g_final
    return out


def _layer_fwd(l, xs, w, sm, ctx, hosted=_NO_EXCHANGE, hosted_gating=_NO_EXCHANGE, hosted_ffn=_NO_EXCHANGE,
               late_weights=None):
    n_lat, n_ctx, cos_t, sin_t, a128 = ctx["n_lat"], ctx["n_ctx"], ctx["cos_t"], ctx["sin_t"], ctx["a128"]
    mod128 = _mm(a128, w["w_mod"], name=f"mod{l}")
    mod8 = _small(lambda m, b: m + b, (SUBLANES, 6 * D), mod128[:SUBLANES], sm["b_mod"][None, :], name=f"mod_bias{l}")
    g_mix = sm["g_mix"][None, :]
    g_ffn = sm["g_ffn"][None, :]
    g_v = sm["g_v"][None, :]
    b_gate = sm["b_gate"][None, :]
    sink_b = jnp.broadcast_to(sm["sink"][:, None], (N_HEADS, LANES))
    b_sb = jnp.broadcast_to(sm["b_spatial"][:, :, None], (A_GROUPS, BLK, LANES))
    w_sconv8 = _pad_rows(sm["w_sconv"], SUBLANES)
    w_fconv8 = _pad_rows(sm["w_fconv"], SUBLANES)
    w_in = w["w_in"]
    w_seg = [w_in[:, SEG[k]:SEG[k + 1]] for k in range(4)]

    h = _norm_mod_fwd(xs, g_mix, mod8, 0, 1, n_lat, name=f"norm1_{l}")
    z_qkv, z_a, z_b, z_g = [_mm(h, w_seg[k], name=f"in_proj{k}_{l}") for k in range(4)]
    q, kd, vd = _qkv_prep(z_qkv, cos_t, sin_t, name=f"qkv_prep{l}")
    y_attn, carried = _attention_fwd(q, kd, vd, sink_b, n_lat, n_ctx, name=f"attn{l}", hosted=hosted)
    if late_weights is not None:
        w = dict(w, **late_weights(carried))
    y_a, carried_gating = _gating_fwd(z_a, sm["w_spatial"], b_sb, g_v, name=f"gating{l}", hosted=hosted_gating)
    y_b = _sconv_fwd(z_b, w_sconv8, n_lat, name=f"sconv{l}")
    ys = (y_attn, y_a, y_b)
    ts = [_mm(ys[k], w["w_branch"][k], name=f"branch{k}_{l}") for k in range(3)]
    merged = _merge_fwd(*ts, z_g, b_gate, name=f"merge{l}")
    mix_out = _mm(merged, w["w_out"], name=f"out_proj{l}")
    x1 = _residual_fwd(xs, mix_out, mod8, 2, n_lat, name=f"res1_{l}")
    h2 = _norm_mod_fwd(x1, g_ffn, mod8, 3, 4, n_lat, name=f"norm2_{l}")
    up = _mm(h2, w["w_up"], name=f"up_proj{l}")
    cv, f, carried_ffn = _ffn_mid_fwd(up, w_fconv8, n_lat, name=f"ffn_mid{l}", hosted=hosted_ffn)
    ffn_out = _mm(f, w["w_down"], name=f"down_proj{l}")
    x2 = _residual_fwd(x1, ffn_out, mod8, 5, n_lat, name=f"res2_{l}")
    saved = dict(x0=xs, mod8=mod8, h=h, z_qkv=z_qkv, z_a=z_a, z_b=z_b, z_g=z_g, q=q, kd=kd, vd=vd, ys=ys, ts=ts,
                 merged=merged, mix_out=mix_out, x1=x1, h2=h2, up=up, cv=cv, f=f, ffn_out=ffn_out, w_seg=w_seg,
                 g_mix=g_mix, g_ffn=g_ffn, g_v=g_v, b_gate=b_gate, sink_b=sink_b, b_sb=b_sb,
                 w_sconv8=w_sconv8, w_fconv8=w_fconv8, w_spatial=sm["w_spatial"])
    return x2, saved, (carried, carried_gating, carried_ffn)


def _layer_bwd(l, s, w, dx, ctx, hosts=None):
    n_lat, n_ctx, cos_t, sin_t, a128 = ctx["n_lat"], ctx["n_ctx"], ctx["cos_t"], ctx["sin_t"], ctx["a128"]
    mod8 = s["mod8"]
    d_ffn, st_gt2 = _residual_bwd(dx, s["ffn_out"], mod8, 5, n_lat, name=f"res2_bwd{l}")
    df = _mm(d_ffn, w["w_down"], tb=True, name=f"down_bwd_x{l}")
    g_down = _mm(s["f"], d_ffn, ta=True, out_dtype=BF16, name=f"down_bwd_w{l}")
    d_up, st_fc = _ffn_mid_bwd(s["up"], s["cv"], df, s["w_fconv8"], n_lat, name=f"ffn_mid_bwd{l}")
    dh2 = _mm(d_up, w["w_up"], tb=True, name=f"up_bwd_x{l}")
    g_up = _mm(s["h2"], d_up, ta=True, out_dtype=BF16, name=f"up_bwd_w{l}")
    dx1, st_n2, _ = _norm_mod_bwd(s["x1"], [dh2], dx, s["g_ffn"], mod8, 4, n_lat, name=f"norm2_bwd{l}")
    d_out, st_gt1 = _residual_bwd(dx1, s["mix_out"], mod8, 2, n_lat, name=f"res1_bwd{l}")
    d_merged = _mm(d_out, w["w_out"], tb=True, name=f"out_bwd_x{l}")
    g_out = _mm(s["merged"], d_out, ta=True, out_dtype=BF16, name=f"out_bwd_w{l}")
    dt0, dt1, dt2, dz_g, st_bg = _merge_bwd(d_merged, *s["ts"], s["z_g"], s["b_gate"], name=f"merge_bwd{l}")
    dts = (dt0, dt1, dt2)
    dys = [_mm(dts[k], w["w_branch"][k], tb=True, name=f"branch{k}_bwd_x{l}") for k in range(3)]
    g_branch = jnp.stack([_mm(s["ys"][k], dts[k], ta=True, out_dtype=BF16, name=f"branch{k}_bwd_w{l}")
                          for k in range(3)])
    early = dict(w_branch=g_branch.reshape(3 * D, D), w_out=g_out, w_up=g_up, w_down=g_down)
    hosts = hosts or {}
    in_attn, in_gating = hosts["early"](early) if "early" in hosts else (_NO_EXCHANGE, _NO_EXCHANGE)
    carried = {}
    dq, dk, dv, d_sink, carried["attn"] = _attention_bwd(s["q"], s["kd"], s["vd"], s["sink_b"], dys[0], n_lat, n_ctx,
                                                         name=f"attn_bwd{l}", hosted=in_attn)
    dz_qkv = _qkv_unprep(dq, dk, dv, cos_t, sin_t, name=f"qkv_unprep{l}")
    dz_a, d_ws, d_bs, st_gv, carried["gating"] = _gating_bwd(s["z_a"], dys[1], s["w_spatial"], s["b_sb"], s["g_v"],
                                                             name=f"gating_bwd{l}", hosted=in_gating)
    dz_b, st_sc = _sconv_bwd(s["z_b"], dys[2], s["w_sconv8"], n_lat, name=f"sconv_bwd{l}")
    dzs = (dz_qkv, dz_a, dz_b, dz_g)
    g_in = jnp.concatenate([_mm(s["h"], dzs[k], ta=True, out_dtype=BF16, name=f"in_bwd_w{k}_{l}")
                            for k in range(4)], axis=1)
    dh_parts = [_mm(dzs[k], s["w_seg"][k], tb=True, name=f"in_bwd_x{k}_{l}") for k in range(4)]
    in_norm1 = hosts["w_in"](g_in) if "w_in" in hosts else _NO_EXCHANGE
    dx0, st_n1, carried["norm1"] = _norm_mod_bwd(s["x0"], dh_parts, dx1, s["g_mix"], mod8, 1, n_lat,
                                                 name=f"norm1_bwd{l}", hosted=in_norm1)
    dmod = jnp.concatenate([st_n1[0:2], st_n1[2:4], st_gt1[0:2], st_n2[0:2], st_n2[2:4], st_gt2[0:2]], axis=1)
    dmod128 = _pad_rows(dmod, LANES)
    g_mod = _mm(a128, dmod128, ta=True, out_dtype=BF16, name=f"mod_bwd_w{l}")
    d_a128 = _mm(dmod128, w["w_mod"], tb=True, name=f"mod_bwd_x{l}")
    wgrads = dict(early, w_mod=g_mod, w_in=g_in)
    lgrads = dict(b_mod=dmod[0] + dmod[1], g_mix=st_n1[4], g_ffn=st_n2[4], b_gate=st_bg[0], sink=d_sink[:, 0],
                  w_spatial=d_ws, b_spatial=d_bs[:, :, 0], g_v=st_gv[0], w_sconv=st_sc[0:3], w_fconv=st_fc[0:3])
    return dx0, wgrads, lgrads, d_a128, carried


_SMALL_ORDER = ("c_ctx", "b_mod", "g_mix", "b_gate", "sink", "w_spatial", "b_spatial", "g_v", "w_sconv", "g_ffn",
                "w_fconv", "g_final")


def _flat_pack(parts, width):
    flat = jnp.concatenate([p.reshape(-1).astype(F32) for p in parts])
    rows = -(-flat.shape[0] // (width * SUBLANES)) * SUBLANES
    flat = jnp.concatenate([flat, jnp.zeros((rows * width - flat.shape[0],), F32)])
    return flat.reshape(rows, width)


def _flat_unpack(packed, likes):
    flat = packed.reshape(-1)
    out, off = [], 0
    for like in likes:
        n = math.prod(like.shape)
        out.append(flat[off:off + n].reshape(like.shape))
        off += n
    return out


def kernel(x, c, ctx, c_ctx, w_mod, b_mod, g_mix, w_in, b_gate, sink, w_spatial, b_spatial, g_v, w_sconv, w_branch, w_out, g_ffn, w_up, w_fconv, w_down, g_final, loss_target, m_c_ctx, m_w_mod, m_b_mod, m_g_mix, m_w_in, m_b_gate, m_sink, m_w_spatial, m_b_spatial, m_g_v, m_w_sconv, m_w_branch, m_w_out, m_g_ffn, m_w_up, m_w_fconv, m_w_down, m_g_final, v_c_ctx, v_w_mod, v_b_mod, v_g_mix, v_w_in, v_b_gate, v_sink, v_w_spatial, v_b_spatial, v_g_v, v_w_sconv, v_w_branch, v_w_out, v_g_ffn, v_w_up, v_w_fconv, v_w_down, v_g_final):
    n_lat, n_ctx = x.shape[1], ctx.shape[1]
    chip = 2 * lax.axis_index("x") + lax.axis_index("y")
    weights = dict(c_ctx=c_ctx, w_mod=w_mod, b_mod=b_mod, g_mix=g_mix, w_in=w_in, b_gate=b_gate, sink=sink,
                   w_spatial=w_spatial, b_spatial=b_spatial, g_v=g_v, w_sconv=w_sconv, w_branch=w_branch, w_out=w_out,
                   g_ffn=g_ffn, w_up=w_up, w_fconv=w_fconv, w_down=w_down, g_final=g_final)
    m_in = dict(c_ctx=m_c_ctx, w_mod=m_w_mod, b_mod=m_b_mod, g_mix=m_g_mix, w_in=m_w_in, b_gate=m_b_gate, sink=m_sink,
                w_spatial=m_w_spatial, b_spatial=m_b_spatial, g_v=m_g_v, w_sconv=m_w_sconv, w_branch=m_w_branch,
                w_out=m_w_out, g_ffn=m_g_ffn, w_up=m_w_up, w_fconv=m_w_fconv, w_down=m_w_down, g_final=m_g_final)
    v_in = dict(c_ctx=v_c_ctx, w_mod=v_w_mod, b_mod=v_b_mod, g_mix=v_g_mix, w_in=v_w_in, b_gate=v_b_gate, sink=v_sink,
                w_spatial=v_w_spatial, b_spatial=v_b_spatial, g_v=v_g_v, w_sconv=v_w_sconv, w_branch=v_w_branch,
                w_out=v_w_out, g_ffn=v_g_ffn, w_up=v_w_up, w_fconv=v_w_fconv, w_down=v_w_down, g_final=v_g_final)
    big_names = [n for n, _, _ in _BIG]

    conv_pack = _flat_pack([w_sconv, w_fconv], LANES)
    conv_all = _allgather8(conv_pack, name="gather_conv_weights")
    conv_parts = [_flat_unpack(conv_all[2 * p], [w_sconv, w_fconv]) for p in range(N_CHIPS)]
    w_sconv_full = jnp.concatenate([cp[0] for cp in conv_parts], axis=-1)
    w_fconv_full = jnp.concatenate([cp[1] for cp in conv_parts], axis=-1)

    small = dict(b_mod=b_mod, g_mix=g_mix, b_gate=b_gate, sink=sink, w_spatial=w_spatial, b_spatial=b_spatial, g_v=g_v,
                 w_sconv=w_sconv_full, g_ffn=g_ffn, w_fconv=w_fconv_full, g_final=g_final)
    x_tok = jnp.concatenate([x[0], ctx[0]], axis=0)
    step = _step_context(c[0], c_ctx, n_lat, n_ctx)
    layer_small = [{k: v[l] for k, v in small.items() if k != "g_final"} for l in range(DEPTH)]
    my_half = lax.axis_index("c").astype(jnp.int32).reshape(1)

    shards = {n: weights[n].astype(BF16) for n in big_names}
    pack = [_pack_shards(shards, l) for l in range(DEPTH)]
    first = _gather_finish(_gather_ici(pack[0][:2], name="gather_ici0"), pack[0][:2], name="gather_finish0")
    w0 = dict(w_mod=_unpack_cols(first[0]), w_in=_unpack_cols(first[1]))

    def layer0_late_weights(carried):
        rest = _gather_finish(list(carried[1:]), pack[0][2:], name="gather_finish0_late")
        w0.update(w_up=_unpack_cols(rest[0]), **_unpack_rows(rest[1]))
        return w0

    xs, saved0, (part_attn, part_gating, part_ffn) = _layer_fwd(
        0, x_tok, w0, layer_small[0], step, hosted=_gather_ici_hosted(pack[1][1:2] + pack[0][2:]),
        hosted_gating=_gather_ici_hosted(pack[1][:1]), hosted_ffn=_gather_ici_hosted(pack[1][2:]),
        late_weights=layer0_late_weights)
    partial1 = list(part_gating) + list(part_attn[:1]) + list(part_ffn)
    w1 = _unpack_full(_gather_finish(partial1, pack[1], name="gather_finish1"))
    xs, saved1, _ = _layer_fwd(1, xs, w1, layer_small[1], step)
    dx, sq_err, d_g_final = _loss_bwd(xs, loss_target[0], g_final, n_lat)
    loss = lax.psum(0.5 * jnp.sum(sq_err) / D, ("x", "y", "c"))

    def reduce_start(g_packs, tag):
        got = _grad_sibling_swap(g_packs, name=f"grad_sibling_swap{tag}")
        return [_add_half(my_half, a, b, name=f"grad_pair_sum{tag}_{g}") for g, (a, b) in enumerate(zip(g_packs, got))]

    def reduce_finish(exchanged, tag):
        sums = [_sum_slabs(e, F32, name=f"grad_chip_sum{tag}_{g}") for g, e in enumerate(exchanged)]
        return _grad_sibling_share(sums, name=f"grad_sibling_share{tag}")

    dx, wgrads1, lgrads1, d_a1, _ = _layer_bwd(1, saved1, w1, dx, step)
    pair_sum1 = reduce_start(_pack_grads(wgrads1), "1")

    def carried_early(early):
        pair_sum0_early = reduce_start([_pack_grad_cols(early["w_up"]), _pack_grad_rows(early)], "0_early")
        return _grad_ici_hosted(pair_sum1 + pair_sum0_early[1:]), _grad_ici_hosted(pair_sum0_early[:1])

    def carried_w_in(g_in):
        return _grad_ici_hosted(reduce_start([_pack_grad_cols(g_in)], "0_in"))

    dx, wgrads0, lgrads0, d_a0, exchanged = _layer_bwd(0, saved0, w0, dx, step,
                                                       hosts=dict(early=carried_early, w_in=carried_w_in))
    total1 = reduce_finish(exchanged["attn"][:4], "1")
    total0_early = reduce_finish(list(exchanged["gating"]) + list(exchanged["attn"][4:]), "0_early")
    total0_in = reduce_finish(exchanged["norm1"], "0_in")
    mod_sum = reduce_start([_pack_grad_cols(wgrads0["w_mod"])], "0_mod")
    total0_mod = reduce_finish(_grad_ici(mod_sum, name="grad_chip_exchange0_mod"), "0_mod")
    total0 = list(total0_mod) + list(total0_in) + list(total0_early)
    big_grads = _unpack_shards([total0, total1], {n: weights[n] for n in big_names})
    sgrads = _small_grads([lgrads0, lgrads1], [d_a0, d_a1], d_g_final, step)
    grad_x = dx[:n_lat][None]

    s_likes = [sgrads[n] for n in _SMALL_ORDER]
    s_all = _allgather8(_flat_pack(s_likes, D), name="gather_small_grads")
    s_tot = _flat_unpack(_sum_slabs(s_all, F32, name="small_grad_sum"), s_likes)
    grads = dict(big_grads)
    for n, g in zip(_SMALL_ORDER, s_tot):
        grads[n] = g
    grads["w_sconv"] = lax.dynamic_slice_in_dim(grads["w_sconv"], chip * w_sconv.shape[-1], w_sconv.shape[-1], axis=2)
    grads["w_fconv"] = lax.dynamic_slice_in_dim(grads["w_fconv"], chip * w_fconv.shape[-1], w_fconv.shape[-1], axis=2)

    delta, new_m, new_v = {}, {}, {}
    for n in big_names:
        cols = weights[n].shape[-1]
        view = lambda a: a.reshape(-1, cols)
        d_, m_, v_ = _adamw(view(weights[n]), view(grads[n]), view(m_in[n]), view(v_in[n]), name=f"adamw_{n}")
        delta[n], new_m[n], new_v[n] = (t.reshape(weights[n].shape) for t in (d_, m_, v_))
    likes = [weights[n] for n in _SMALL_ORDER]
    packs = [_flat_pack([src[n] for n in _SMALL_ORDER], D) for src in (weights, grads, m_in, v_in)]
    outs = _adamw(*packs, name="adamw_small")
    for dst, packed in zip((delta, new_m, new_v), outs):
        for n, val in zip(_SMALL_ORDER, _flat_unpack(packed, likes)):
            dst[n] = val

    order = ("c_ctx", "w_mod", "b_mod", "g_mix", "w_in", "b_gate", "sink", "w_spatial", "b_spatial", "g_v", "w_sconv",
             "w_branch", "w_out", "g_ffn", "w_up", "w_fconv", "w_down", "g_final")
    return (loss, grad_x, *[grads[n] for n in order], *[delta[n] for n in order], *[new_m[n] for n in order],
            *[new_v[n] for n in order])
```

```python
import functools
import math

import jax
import jax.numpy as jnp
from jax import lax
from jax.experimental import pallas as pl
from jax.experimental.pallas import tpu as pltpu

F32 = jnp.float32
BF16 = jnp.bfloat16

D = 1024
DEPTH = 2
GRID_W = 64
N_HEADS = 16
N_KV = 4
GRP = N_HEADS // N_KV
HEAD_DIM = 64
KV_W = N_KV * HEAD_DIM
WINDOW = 128
BLK = 128
ROPE_THETA = 10000.0
A_GROUPS = 8
D_FF = 2816
EPS = 1e-6
NEG = -1e30
QKV_W = D + 2 * KV_W
A_COLS = 2 * D
B_COLS = 3 * D
G_COLS = 3 * D
IN_W = QKV_W + A_COLS + B_COLS + G_COLS
SEG = (0, QKV_W, QKV_W + A_COLS, QKV_W + A_COLS + B_COLS, IN_W)
N_CHIPS = 4
N_DEV = 8
LANES = 128
SUBLANES = 8
VMEM_LIMIT = 56 * 1024 * 1024
ADAM_LR = 0.001
ADAM_B1 = 0.9
ADAM_B2 = 0.999
ADAM_EPS = 1e-08
ADAM_WD = 0.01
ADAM_STEP = 10
MESH = pl.DeviceIdType.MESH
ANY = pl.BlockSpec(memory_space=pl.ANY)


def _params(sem=None):
    return pltpu.CompilerParams(dimension_semantics=sem, vmem_limit_bytes=VMEM_LIMIT)


def _pick(n, cands):
    for c in cands:
        if n % c == 0:
            return c
    return n


def _rows8(rows, width):
    r = lax.broadcasted_iota(jnp.int32, (SUBLANES, width), 0)
    out = jnp.zeros((SUBLANES, width), F32)
    for idx, v in rows:
        out = out + jnp.where(r == idx, v, 0.0)
    return out


def _sel(mod_ref, k, is_ctx):
    return jnp.where(is_ctx, mod_ref[1:2, k * D:(k + 1) * D], mod_ref[0:1, k * D:(k + 1) * D])


def _colsum(v):
    return jnp.sum(v, axis=0, keepdims=True)


def _mm(a, b, *, name, ta=False, tb=False, out_dtype=F32):
    if ta:
        k_dim, m = a.shape
    else:
        m, k_dim = a.shape
    if tb:
        n, kb = b.shape
    else:
        kb, n = b.shape
    assert k_dim == kb, (a.shape, b.shape, ta, tb)
    tm = _pick(m, (1056, 1024, 1408, 768, 512, 256, 128))
    tn = _pick(n, (1536, 1408, 1024, 768, 512, 256, 128))
    tk = _pick(k_dim, (2048, 1536, 1408, 1024, 768, 512, 256, 128))
    nk = k_dim // tk
    dims = (((0 if ta else 1,), (1 if tb else 0,)), ((), ()))

    def product(a_ref, b_ref):
        return lax.dot_general(a_ref[...].astype(BF16), b_ref[...].astype(BF16), dims, preferred_element_type=F32)

    def body_single(a_ref, b_ref, o_ref):
        o_ref[...] = product(a_ref, b_ref).astype(o_ref.dtype)

    def body_acc(a_ref, b_ref, o_ref, acc_ref):
        k = pl.program_id(2)

        @pl.when(k == 0)
        def _():
            acc_ref[...] = product(a_ref, b_ref)

        @pl.when(k > 0)
        def _():
            acc_ref[...] += product(a_ref, b_ref)

        @pl.when(k == nk - 1)
        def _():
            o_ref[...] = acc_ref[...].astype(o_ref.dtype)

    a_spec = pl.BlockSpec((tk, tm), lambda i, j, k: (k, i)) if ta else pl.BlockSpec((tm, tk), lambda i, j, k: (i, k))
    b_spec = pl.BlockSpec((tn, tk), lambda i, j, k: (j, k)) if tb else pl.BlockSpec((tk, tn), lambda i, j, k: (k, j))
    return pl.pallas_call(
        body_single if nk == 1 else body_acc, name=name, grid=(m // tm, n // tn, nk),
        in_specs=[a_spec, b_spec], out_specs=pl.BlockSpec((tm, tn), lambda i, j, k: (i, j)),
        out_shape=jax.ShapeDtypeStruct((m, n), out_dtype),
        scratch_shapes=[] if nk == 1 else [pltpu.VMEM((tm, tn), F32)],
        compiler_params=_params(("parallel", "parallel", "arbitrary")),
    )(a, b)


def _small(fn, out_shape, *arrays, name):
    def body(*refs):
        refs[-1][...] = fn(*[r[...] for r in refs[:-1]]).astype(refs[-1].dtype)

    return pl.pallas_call(body, name=name, out_shape=jax.ShapeDtypeStruct(out_shape, F32))(*arrays)


def _silu(v):
    return v * jax.nn.sigmoid(v)


def _dsilu(v):
    s = jax.nn.sigmoid(v)
    return s * (1.0 + v * (1.0 - s))


def _row_spec(tm, width, col=0):
    return pl.BlockSpec((tm, width), lambda i: (i, col))


def _full_spec(shape):
    nd = len(shape)
    return pl.BlockSpec(shape, lambda i: (0,) * nd)


def _halo_specs(tm, width, t_rows, col=0):
    per = tm // SUBLANES
    last = t_rows // SUBLANES - 1
    prev = pl.BlockSpec((SUBLANES, width), lambda i: (jnp.maximum(i * per - 1, 0), col))
    nxt = pl.BlockSpec((SUBLANES, width), lambda i: (jnp.minimum((i + 1) * per, last), col))
    return prev, nxt


def _shift_rows(cur, prev8, next8, n_lat, t_rows, tm):
    i = pl.program_id(0)
    row = lax.broadcasted_iota(jnp.int32, (tm, 1), 0)
    g = row + i * tm
    up = pltpu.roll(cur, 1, 0)
    up = jnp.where(row == 0, prev8[SUBLANES - 1:SUBLANES, :], up)
    up = jnp.where((g == 0) | (g == n_lat), 0.0, up)
    dn = pltpu.roll(cur, tm - 1, 0)
    dn = jnp.where(row == tm - 1, next8[0:1, :], dn)
    dn = jnp.where((g == n_lat - 1) | (g == t_rows - 1), 0.0, dn)
    return up, dn


def _norm_mod_fwd(x, g, mod8, sh_idx, sc_idx, n_lat, *, name):
    t_rows = x.shape[0]
    tm = 256

    def body(x_ref, g_ref, mod_ref, o_ref):
        is_ctx = pl.program_id(0) * tm >= n_lat
        xv = x_ref[...]
        rstd = lax.rsqrt(jnp.mean(xv * xv, axis=-1, keepdims=True) + EPS)
        y = xv * rstd * g_ref[...]
        o_ref[...] = (y * (1.0 + _sel(mod_ref, sc_idx, is_ctx)) + _sel(mod_ref, sh_idx, is_ctx)).astype(BF16)

    return pl.pallas_call(
        body, name=name, grid=(t_rows // tm,),
        in_specs=[_row_spec(tm, D), _full_spec((1, D)), _full_spec((SUBLANES, 6 * D))],
        out_specs=_row_spec(tm, D), out_shape=jax.ShapeDtypeStruct((t_rows, D), BF16),
        compiler_params=_params(("parallel",)),
    )(x, g, mod8)


def _norm_mod_bwd(x, dh_parts, dres, g, mod8, sc_idx, n_lat, *, name, hosted=None):
    hosted = hosted or _NO_EXCHANGE
    t_rows = x.shape[0]
    tm = 256
    n_parts = len(dh_parts)
    n_steps = t_rows // tm

    def body(*refs):
        ins, outs, _, h_refs = _split_refs(refs, 4 + n_parts, 2, 0, hosted)
        x_ref, dres_ref, g_ref, mod_ref = ins[:4]
        part_refs = ins[4:]
        dx_ref, st_ref = outs
        i = pl.program_id(0)
        _run_hosted(hosted, h_refs, i, n_steps)
        is_ctx = i * tm >= n_lat
        dh = part_refs[0][...]
        for p in part_refs[1:]:
            dh = dh + p[...]
        xv = x_ref[...]
        gv = g_ref[...]
        rstd = lax.rsqrt(jnp.mean(xv * xv, axis=-1, keepdims=True) + EPS)
        rn = xv * rstd
        dy = dh * (1.0 + _sel(mod_ref, sc_idx, is_ctx))
        e = dy * gv
        dx_ref[...] = dres_ref[...] + rstd * (e - rn * jnp.mean(e * rn, axis=-1, keepdims=True))
        dsh = _colsum(dh)
        dsc = _colsum(dh * (rn * gv))
        dg = _colsum(dy * rn)
        zero = jnp.zeros_like(dsh)
        upd = _rows8([(0, jnp.where(is_ctx, zero, dsh)), (1, jnp.where(is_ctx, dsh, zero)),
                      (2, jnp.where(is_ctx, zero, dsc)), (3, jnp.where(is_ctx, dsc, zero)), (4, dg)], D)

        @pl.when(i == 0)
        def _():
            st_ref[...] = upd

        @pl.when(i > 0)
        def _():
            st_ref[...] += upd

    outs = pl.pallas_call(
        body, name=name, grid=(n_steps,),
        in_specs=[_row_spec(tm, D), _row_spec(tm, D), _full_spec((1, D)), _full_spec((SUBLANES, 6 * D))]
        + [_row_spec(tm, D)] * n_parts + [ANY] * len(hosted.arrays),
        out_specs=[_row_spec(tm, D), _full_spec((SUBLANES, D))] + [ANY] * len(hosted.out_shapes),
        out_shape=[jax.ShapeDtypeStruct((t_rows, D), F32), jax.ShapeDtypeStruct((SUBLANES, D), F32)]
        + list(hosted.out_shapes),
        scratch_shapes=list(hosted.scratch),
        compiler_params=_params(("arbitrary",)),
    )(x, dres, g, mod8, *dh_parts, *hosted.arrays)
    return outs[0], outs[1], outs[2:]


def _residual_fwd(x, branch, mod8, gt_idx, n_lat, *, name):
    t_rows = x.shape[0]
    tm = 256

    def body(x_ref, b_ref, mod_ref, o_ref):
        is_ctx = pl.program_id(0) * tm >= n_lat
        o_ref[...] = x_ref[...] + _sel(mod_ref, gt_idx, is_ctx) * b_ref[...]

    return pl.pallas_call(
        body, name=name, grid=(t_rows // tm,),
        in_specs=[_row_spec(tm, D), _row_spec(tm, D), _full_spec((SUBLANES, 6 * D))],
        out_specs=_row_spec(tm, D), out_shape=jax.ShapeDtypeStruct((t_rows, D), F32),
        compiler_params=_params(("parallel",)),
    )(x, branch, mod8)


def _residual_bwd(dx, branch, mod8, gt_idx, n_lat, *, name):
    t_rows = dx.shape[0]
    tm = 256

    def body(dx_ref, b_ref, mod_ref, o_ref, st_ref):
        i = pl.program_id(0)
        is_ctx = i * tm >= n_lat
        dxv = dx_ref[...]
        o_ref[...] = (dxv * _sel(mod_ref, gt_idx, is_ctx)).astype(BF16)
        dgt = _colsum(dxv * b_ref[...])
        zero = jnp.zeros_like(dgt)
        upd = _rows8([(0, jnp.where(is_ctx, zero, dgt)), (1, jnp.where(is_ctx, dgt, zero))], D)

        @pl.when(i == 0)
        def _():
            st_ref[...] = upd

        @pl.when(i > 0)
        def _():
            st_ref[...] += upd

    return pl.pallas_call(
        body, name=name, grid=(t_rows // tm,),
        in_specs=[_row_spec(tm, D), _row_spec(tm, D), _full_spec((SUBLANES, 6 * D))],
        out_specs=[_row_spec(tm, D), _full_spec((SUBLANES, D))],
        out_shape=[jax.ShapeDtypeStruct((t_rows, D), BF16), jax.ShapeDtypeStruct((SUBLANES, D), F32)],
        compiler_params=_params(("arbitrary",)),
    )(dx, branch, mod8)


def _rope_tables(n_lat, n_ctx):
    rows = n_lat // GRID_W
    row = jnp.broadcast_to(jnp.arange(rows, dtype=F32)[:, None], (rows, GRID_W)).reshape(n_lat)
    col = jnp.broadcast_to(jnp.arange(GRID_W, dtype=F32)[None, :], (rows, GRID_W)).reshape(n_lat)
    half = HEAD_DIM // 2
    inv = ROPE_THETA ** (-jnp.arange(0, half, 2, dtype=F32) / half)
    ang = jnp.concatenate([row[:, None] * inv, col[:, None] * inv], axis=-1)
    cos, sin = jnp.cos(ang), jnp.sin(ang)
    c64 = jnp.concatenate([cos, cos], axis=-1)
    s64 = jnp.concatenate([-sin, sin], axis=-1)
    c64 = jnp.concatenate([c64, jnp.ones((n_ctx, HEAD_DIM), F32)], axis=0)
    s64 = jnp.concatenate([s64, jnp.zeros((n_ctx, HEAD_DIM), F32)], axis=0)
    return jnp.tile(c64, (1, 2)), jnp.tile(s64, (1, 2))


def _swap_halves(v):
    lane = lax.broadcasted_iota(jnp.int32, v.shape, 1)
    return jnp.where(lane % HEAD_DIM < HEAD_DIM // 2, pltpu.roll(v, LANES - HEAD_DIM // 2, 1),
                     pltpu.roll(v, HEAD_DIM // 2, 1))


def _low_half(shape):
    return lax.broadcasted_iota(jnp.int32, shape, 1) < HEAD_DIM


def _qkv_prep(z_qkv, cos_t, sin_t, *, name):
    t_rows = z_qkv.shape[0]
    tm = 256

    def body(z_ref, c_ref, s_ref, q_ref, k_ref, v_ref):
        cv, sv = c_ref[...], s_ref[...]

        def rope(chunk):
            return chunk * cv + _swap_halves(chunk) * sv

        for ch in range(D // LANES):
            roped = rope(z_ref[:, ch * LANES:(ch + 1) * LANES])
            q_ref[:, ch * LANES:(ch + 1) * LANES] = (roped * (HEAD_DIM ** -0.5)).astype(BF16)
        low = _low_half((tm, LANES))
        for pair in range(N_KV // 2):
            for which, ref, roped in ((0, k_ref, True), (1, v_ref, False)):
                off = D + which * KV_W + pair * LANES
                chunk = z_ref[:, off:off + LANES]
                if roped:
                    chunk = rope(chunk)
                other = pltpu.roll(chunk, HEAD_DIM, 1)
                even = jnp.where(low, chunk, other)
                odd = jnp.where(low, other, chunk)
                ref[:, (2 * pair) * LANES:(2 * pair + 1) * LANES] = even.astype(BF16)
                ref[:, (2 * pair + 1) * LANES:(2 * pair + 2) * LANES] = odd.astype(BF16)

    dup_w = N_KV * LANES
    return pl.pallas_call(
        body, name=name, grid=(t_rows // tm,),
        in_specs=[_row_spec(tm, QKV_W), _row_spec(tm, LANES), _row_spec(tm, LANES)],
        out_specs=[_row_spec(tm, D), _row_spec(tm, dup_w), _row_spec(tm, dup_w)],
        out_shape=[jax.ShapeDtypeStruct((t_rows, D), BF16), jax.ShapeDtypeStruct((t_rows, dup_w), BF16),
                   jax.ShapeDtypeStruct((t_rows, dup_w), BF16)],
        compiler_params=_params(("parallel",)),
    )(z_qkv, cos_t, sin_t)


def _qkv_unprep(dq, dk, dv, cos_t, sin_t, *, name):
    t_rows = dq.shape[0]
    tm = 256

    def body(dq_ref, dk_ref, dv_ref, c_ref, s_ref, o_ref):
        cv, sv = c_ref[...], s_ref[...]

        def unrope(chunk):
            return chunk * cv + _swap_halves(chunk * sv)

        for ch in range(D // LANES):
            o_ref[:, ch * LANES:(ch + 1) * LANES] = unrope(dq_ref[:, ch * LANES:(ch + 1) * LANES]).astype(BF16)
        for pair in range(N_KV // 2):
            for which, ref, roped in ((0, dk_ref, True), (1, dv_ref, False)):
                chunk = ref[:, pair * LANES:(pair + 1) * LANES]
                if roped:
                    chunk = unrope(chunk)
                off = D + which * KV_W + pair * LANES
                o_ref[:, off:off + LANES] = chunk.astype(BF16)

    return pl.pallas_call(
        body, name=name, grid=(t_rows // tm,),
        in_specs=[_row_spec(tm, D), _row_spec(tm, KV_W), _row_spec(tm, KV_W), _row_spec(tm, LANES),
                  _row_spec(tm, LANES)],
        out_specs=_row_spec(tm, QKV_W), out_shape=jax.ShapeDtypeStruct((t_rows, QKV_W), BF16),
        compiler_params=_params(("parallel",)),
    )(dq, dk, dv, cos_t, sin_t)


def _attn_specs(n_lat, n_ctx):
    nb = n_lat // BLK
    dup_w = N_KV * LANES

    def ws(j):
        return jnp.clip(j - 1, 0, nb - 3)

    win = [pl.BlockSpec((BLK, dup_w), functools.partial(lambda j, o: (ws(j) + o, 0), o=o)) for o in range(3)]
    ctx = pl.BlockSpec((n_ctx, dup_w), lambda j: (n_lat // n_ctx, 0))
    return nb, ws, win, ctx


def _attn_bias(j, ws_j, nb, n_ctx):
    n_keys = 3 * BLK + n_ctx
    row = lax.broadcasted_iota(jnp.int32, (BLK, n_keys), 0)
    col = lax.broadcasted_iota(jnp.int32, (BLK, n_keys), 1)
    rel = (ws_j - j) * BLK + col - row
    valid = (col >= 3 * BLK) | ((jnp.abs(rel) <= WINDOW) & (j < nb))
    bias = jnp.where(valid, 0.0, NEG)
    return jnp.concatenate([bias] * GRP, axis=0)


def _attn_probs(q_ref, kk, kh, bias, sink_ref):
    low = _low_half((BLK, LANES))
    qs = []
    for g in range(GRP):
        h = GRP * kh + g
        chunk = q_ref[:, (h // 2) * LANES:(h // 2 + 1) * LANES]
        qs.append(jnp.where(low if h % 2 == 0 else ~low, chunk, jnp.zeros_like(chunk)))
    qs = jnp.concatenate(qs, axis=0)
    s = lax.dot_general(qs, kk, (((1,), (1,)), ((), ())), preferred_element_type=F32) + bias
    snk = jnp.concatenate(
        [jnp.broadcast_to(jnp.max(sink_ref[GRP * kh + g:GRP * kh + g + 1, :], axis=1, keepdims=True), (BLK, 1))
         for g in range(GRP)], axis=0)
    m = jnp.maximum(jnp.max(s, axis=-1, keepdims=True), snk)
    p = jnp.exp(s - m)
    p_snk = jnp.exp(snk - m)
    inv = 1.0 / (jnp.sum(p, axis=-1, keepdims=True) + p_snk)
    return qs, p, p_snk, inv


class _Hosted:
    def __init__(self, arrays, out_shapes, scratch, start, finish):
        self.arrays, self.out_shapes, self.scratch, self.start, self.finish = arrays, out_shapes, scratch, start, finish


_NO_EXCHANGE = _Hosted([], [], [], None, None)


def _split_refs(refs, n_in, n_out, n_scratch, hosted):
    hi, ho, hs = len(hosted.arrays), len(hosted.out_shapes), len(hosted.scratch)
    a = n_in + hi
    b = a + n_out + ho
    ins, h_ins = refs[:n_in], refs[n_in:a]
    outs, h_outs = refs[a:a + n_out], refs[a + n_out:b]
    scr, h_scr = refs[b:b + n_scratch], refs[b + n_scratch:b + n_scratch + hs]
    return ins, outs, scr, (h_ins, h_outs, h_scr)


def _run_hosted(hosted, h_refs, step, n_steps):
    if hosted.start is None:
        return

    flat = [r for group in h_refs for r in group]

    @pl.when(step == 0)
    def _():
        hosted.start(*flat)

    @pl.when(step == n_steps - 1)
    def _():
        hosted.finish(*flat)


def _attention_fwd(q, kd, vd, sink_b, n_lat, n_ctx, *, name, hosted=_NO_EXCHANGE):
    t_rows = q.shape[0]
    nb, ws, win, ctx = _attn_specs(n_lat, n_ctx)
    n_steps = t_rows // BLK

    def body(*refs):
        ins, outs, _, h_refs = _split_refs(refs, 10, 1, 0, hosted)
        q_ref, k0, k1, k2, kc, v0, v1, v2, vc, sink_ref = ins
        o_ref, = outs
        j = pl.program_id(0)
        _run_hosted(hosted, h_refs, j, n_steps)
        ws_j = ws(j)
        low = _low_half((BLK, LANES))
        bias = _attn_bias(j, ws_j, nb, n_ctx)
        for kh in range(N_KV):
            sl = slice(kh * LANES, (kh + 1) * LANES)
            kk = jnp.concatenate([k0[:, sl], k1[:, sl], k2[:, sl], kc[:, sl]], axis=0)
            vv = jnp.concatenate([v0[:, sl], v1[:, sl], v2[:, sl], vc[:, sl]], axis=0)
            _, p, _, inv = _attn_probs(q_ref, kk, kh, bias, sink_ref)
            o = jnp.dot(p.astype(BF16), vv, preferred_element_type=F32) * inv
            for half in range(2):
                even = o[(2 * half) * BLK:(2 * half + 1) * BLK]
                odd = o[(2 * half + 1) * BLK:(2 * half + 2) * BLK]
                ch = 2 * kh + half
                o_ref[:, ch * LANES:(ch + 1) * LANES] = jnp.where(low, even, odd).astype(BF16)

    outs = pl.pallas_call(
        body, name=name, grid=(n_steps,),
        in_specs=[_row_spec(BLK, D)] + win + [ctx] + win + [ctx] + [_full_spec((N_HEADS, LANES))]
        + [ANY] * len(hosted.arrays),
        out_specs=[_row_spec(BLK, D)] + [ANY] * len(hosted.out_shapes),
        out_shape=[jax.ShapeDtypeStruct((t_rows, D), BF16)] + list(hosted.out_shapes),
        scratch_shapes=list(hosted.scratch),
        compiler_params=_params(("arbitrary",)),
    )(q, kd, kd, kd, kd, vd, vd, vd, vd, sink_b, *hosted.arrays)
    return outs[0], outs[1:]


def _attention_bwd(q, kd, vd, sink_b, dy, n_lat, n_ctx, *, name, hosted=_NO_EXCHANGE):
    t_rows = q.shape[0]
    nb, ws, win, ctx = _attn_specs(n_lat, n_ctx)
    n_steps = t_rows // BLK

    def body(*refs):
        ins, outs, scr, h_refs = _split_refs(refs, 11, 4, 3, hosted)
        q_ref, k0, k1, k2, kc, v0, v1, v2, vc, sink_ref, dy_ref = ins
        dq_ref, dk_hbm, dv_hbm, ds_ref = outs
        dk_acc, dv_acc, sem = scr
        j = pl.program_id(0)
        _run_hosted(hosted, h_refs, j, n_steps)
        ws_j = ws(j)

        @pl.when(j == 0)
        def _():
            dk_acc[...] = jnp.zeros_like(dk_acc)
            dv_acc[...] = jnp.zeros_like(dv_acc)
            ds_ref[...] = jnp.zeros_like(ds_ref)

        low = _low_half((BLK, LANES))
        low_keys = _low_half((3 * BLK + n_ctx, LANES))
        win_start = pl.multiple_of(ws_j * BLK, BLK)
        scale = HEAD_DIM ** -0.5
        dk_heads, dv_heads = [], []
        bias = _attn_bias(j, ws_j, nb, n_ctx)
        for kh in range(N_KV):
            sl = slice(kh * LANES, (kh + 1) * LANES)
            kk = jnp.concatenate([k0[:, sl], k1[:, sl], k2[:, sl], kc[:, sl]], axis=0)
            vv = jnp.concatenate([v0[:, sl], v1[:, sl], v2[:, sl], vc[:, sl]], axis=0)
            qs, p, p_snk, inv = _attn_probs(q_ref, kk, kh, bias, sink_ref)
            dos = []
            for g in range(GRP):
                h = GRP * kh + g
                chunk = dy_ref[:, (h // 2) * LANES:(h // 2 + 1) * LANES]
                dos.append(jnp.where(low if h % 2 == 0 else ~low, chunk, jnp.zeros_like(chunk)).astype(BF16))
            dos = jnp.concatenate(dos, axis=0)
            dp = lax.dot_general(dos, vv, (((1,), (1,)), ((), ())), preferred_element_type=F32)
            dsum = jnp.sum(p * dp, axis=-1, keepdims=True) * inv
            ds = (p * ((dp - dsum) * inv)).astype(BF16)
            snk_term = p_snk * inv * dsum
            for g in range(GRP):
                contrib = -jnp.sum(snk_term[g * BLK:(g + 1) * BLK], axis=0, keepdims=True)
                ds_ref[GRP * kh + g:GRP * kh + g + 1, :] += jnp.broadcast_to(contrib, (1, LANES))
            dqs = jnp.dot(ds, kk, preferred_element_type=F32) * scale
            for half in range(2):
                even = dqs[(2 * half) * BLK:(2 * half + 1) * BLK]
                odd = dqs[(2 * half + 1) * BLK:(2 * half + 2) * BLK]
                ch = 2 * kh + half
                dq_ref[:, ch * LANES:(ch + 1) * LANES] = jnp.where(low, even, odd)
            dkk = lax.dot_general(ds, qs, (((0,), (0,)), ((), ())), preferred_element_type=F32)
            dvv = lax.dot_general((p * inv).astype(BF16), dos, (((0,), (0,)), ((), ())), preferred_element_type=F32)
            dk_heads.append(dkk + pltpu.roll(dkk, HEAD_DIM, 1))
            dv_heads.append(dvv + pltpu.roll(dvv, HEAD_DIM, 1))
        for pair in range(N_KV // 2):
            sl = slice(pair * LANES, (pair + 1) * LANES)
            for acc, heads in ((dk_acc, dk_heads), (dv_acc, dv_heads)):
                both = jnp.where(low_keys, heads[2 * pair], heads[2 * pair + 1])
                acc[pl.ds(win_start, 3 * BLK), sl] += both[:3 * BLK]
                acc[n_lat:n_lat + n_ctx, sl] += both[3 * BLK:]

        @pl.when(j == n_steps - 1)
        def _():
            ck = pltpu.make_async_copy(dk_acc, dk_hbm, sem.at[0])
            cv = pltpu.make_async_copy(dv_acc, dv_hbm, sem.at[1])
            ck.start()
            cv.start()
            ck.wait()
            cv.wait()

    outs = pl.pallas_call(
        body, name=name, grid=(n_steps,),
        in_specs=[_row_spec(BLK, D)] + win + [ctx] + win + [ctx] + [_full_spec((N_HEADS, LANES)), _row_spec(BLK, D)]
        + [ANY] * len(hosted.arrays),
        out_specs=[_row_spec(BLK, D), ANY, ANY, _full_spec((N_HEADS, LANES))] + [ANY] * len(hosted.out_shapes),
        out_shape=[jax.ShapeDtypeStruct((t_rows, D), F32), jax.ShapeDtypeStruct((t_rows, KV_W), F32),
                   jax.ShapeDtypeStruct((t_rows, KV_W), F32), jax.ShapeDtypeStruct((N_HEADS, LANES), F32)]
        + list(hosted.out_shapes),
        scratch_shapes=[pltpu.VMEM((t_rows, KV_W), F32), pltpu.VMEM((t_rows, KV_W), F32),
                        pltpu.SemaphoreType.DMA((2,))] + list(hosted.scratch),
        compiler_params=_params(("arbitrary",)),
    )(q, kd, kd, kd, kd, vd, vd, vd, vd, sink_b, dy, *hosted.arrays)
    return outs[0], outs[1], outs[2], outs[3], outs[4:]


_GELU_K = math.sqrt(2.0 / math.pi)


def _gelu(v):
    return jax.nn.gelu(v)


def _gelu_and_grad(v):
    t = jnp.tanh(_GELU_K * (v + 0.044715 * (v * v * v)))
    cdf = 0.5 * (1.0 + t)
    return v * cdf, cdf + 0.5 * v * (1.0 - t * t) * _GELU_K * (1.0 + 3.0 * 0.044715 * v * v)


def _gating_fwd(z_a, w_s, b_sb, g_v, *, name, hosted=None):
    hosted = hosted or _NO_EXCHANGE
    t_rows = z_a.shape[0]
    n_steps = t_rows // BLK

    def body(*refs):
        ins, outs, _, h_refs = _split_refs(refs, 4, 1, 0, hosted)
        z_ref, w_ref, b_ref, g_ref = ins
        o_ref, = outs
        _run_hosted(hosted, h_refs, pl.program_id(0), n_steps)
        u = _gelu(z_ref[:, :D])
        v = _gelu(z_ref[:, D:])
        vn = v * lax.rsqrt(jnp.mean(v * v, axis=-1, keepdims=True) + EPS) * g_ref[...]
        for g in range(A_GROUPS):
            sl = slice(g * LANES, (g + 1) * LANES)
            mixed = jnp.dot(w_ref[g].astype(BF16), vn[:, sl].astype(BF16), preferred_element_type=F32) + b_ref[g]
            o_ref[:, sl] = (u[:, sl] * mixed).astype(BF16)

    outs = pl.pallas_call(
        body, name=name, grid=(n_steps,),
        in_specs=[_row_spec(BLK, A_COLS), _full_spec((A_GROUPS, BLK, BLK)), _full_spec((A_GROUPS, BLK, LANES)),
                  _full_spec((1, D))] + [ANY] * len(hosted.arrays),
        out_specs=[_row_spec(BLK, D)] + [ANY] * len(hosted.out_shapes),
        out_shape=[jax.ShapeDtypeStruct((t_rows, D), BF16)] + list(hosted.out_shapes),
        scratch_shapes=list(hosted.scratch),
        compiler_params=_params(("arbitrary",)),
    )(z_a, w_s, b_sb, g_v, *hosted.arrays)
    return outs[0], outs[1:]


def _gating_bwd(z_a, dy, w_s, b_sb, g_v, *, name, hosted=None):
    hosted = hosted or _NO_EXCHANGE
    t_rows = z_a.shape[0]
    n_steps = t_rows // BLK

    def body(*refs):
        ins, outs, _, h_refs = _split_refs(refs, 5, 4, 0, hosted)
        z_ref, dy_ref, w_ref, b_ref, g_ref = ins
        dz_ref, dw_ref, db_ref, st_ref = outs
        i = pl.program_id(0)
        _run_hosted(hosted, h_refs, i, n_steps)

        @pl.when(i == 0)
        def _():
            dw_ref[...] = jnp.zeros_like(dw_ref)
            db_ref[...] = jnp.zeros_like(db_ref)
            st_ref[...] = jnp.zeros_like(st_ref)

        u, du_dz = _gelu_and_grad(z_ref[:, :D])
        v, dv_dz = _gelu_and_grad(z_ref[:, D:])
        gv = g_ref[...]
        rstd = lax.rsqrt(jnp.mean(v * v, axis=-1, keepdims=True) + EPS)
        vh = v * rstd
        vn = vh * gv
        dyv = dy_ref[...]
        dvn = []
        for g in range(A_GROUPS):
            sl = slice(g * LANES, (g + 1) * LANES)
            wg = w_ref[g].astype(BF16)
            vg = vn[:, sl].astype(BF16)
            mixed = jnp.dot(wg, vg, preferred_element_type=F32) + b_ref[g]
            dz_ref[:, sl] = (dyv[:, sl] * mixed * du_dz[:, sl]).astype(BF16)
            dmixed = dyv[:, sl] * u[:, sl]
            dmb = dmixed.astype(BF16)
            dvn.append(lax.dot_general(wg, dmb, (((0,), (0,)), ((), ())), preferred_element_type=F32))
            dw_ref[g] += lax.dot_general(dmb, vg, (((1,), (1,)), ((), ())), preferred_element_type=F32)
            db_ref[g] += jnp.broadcast_to(jnp.sum(dmixed, axis=-1, keepdims=True), (BLK, LANES))
        dvn = jnp.concatenate(dvn, axis=1)
        st_ref[...] += _rows8([(0, _colsum(dvn * vh))], D)
        e = dvn * gv
        dv = rstd * (e - vh * jnp.mean(e * vh, axis=-1, keepdims=True))
        dz_ref[:, D:] = (dv * dv_dz).astype(BF16)

    outs = pl.pallas_call(
        body, name=name, grid=(n_steps,),
        in_specs=[_row_spec(BLK, A_COLS), _row_spec(BLK, D), _full_spec((A_GROUPS, BLK, BLK)),
                  _full_spec((A_GROUPS, BLK, LANES)), _full_spec((1, D))] + [ANY] * len(hosted.arrays),
        out_specs=[_row_spec(BLK, A_COLS), _full_spec((A_GROUPS, BLK, BLK)), _full_spec((A_GROUPS, BLK, LANES)),
                   _full_spec((SUBLANES, D))] + [ANY] * len(hosted.out_shapes),
        out_shape=[jax.ShapeDtypeStruct((t_rows, A_COLS), BF16), jax.ShapeDtypeStruct((A_GROUPS, BLK, BLK), F32),
                   jax.ShapeDtypeStruct((A_GROUPS, BLK, LANES), F32), jax.ShapeDtypeStruct((SUBLANES, D), F32)]
        + list(hosted.out_shapes),
        scratch_shapes=list(hosted.scratch),
        compiler_params=_params(("arbitrary",)),
    )(z_a, dy, w_s, b_sb, g_v, *hosted.arrays)
    return outs[0], outs[1], outs[2], outs[3], outs[4:]


def _sconv_fwd(z_b, w8, n_lat, *, name):
    t_rows = z_b.shape[0]
    tm = 256
    prev, nxt = _halo_specs(tm, B_COLS, t_rows)

    def body(z_ref, zp_ref, zn_ref, w_ref, o_ref):
        p = z_ref[:, D:2 * D] * z_ref[:, 2 * D:]
        pp = zp_ref[:, D:2 * D] * zp_ref[:, 2 * D:]
        pn = zn_ref[:, D:2 * D] * zn_ref[:, 2 * D:]
        up, dn = _shift_rows(p, pp, pn, n_lat, t_rows, tm)
        conv = w_ref[0:1, :] * up + w_ref[1:2, :] * p + w_ref[2:3, :] * dn
        o_ref[...] = (z_ref[:, :D] * conv).astype(BF16)

    return pl.pallas_call(
        body, name=name, grid=(t_rows // tm,),
        in_specs=[_row_spec(tm, B_COLS), prev, nxt, _full_spec((SUBLANES, D))],
        out_specs=_row_spec(tm, D), out_shape=jax.ShapeDtypeStruct((t_rows, D), BF16),
        compiler_params=_params(("parallel",)),
    )(z_b, z_b, z_b, w8)


def _sconv_bwd(z_b, dy, w8, n_lat, *, name):
    t_rows = z_b.shape[0]
    tm = 256
    prev, nxt = _halo_specs(tm, B_COLS, t_rows)
    dprev, dnxt = _halo_specs(tm, D, t_rows)

    def body(z_ref, zp_ref, zn_ref, dy_ref, dyp_ref, dyn_ref, w_ref, dz_ref, st_ref):
        i = pl.program_id(0)
        bg, cg, hb = z_ref[:, :D], z_ref[:, D:2 * D], z_ref[:, 2 * D:]
        p = cg * hb
        pp = zp_ref[:, D:2 * D] * zp_ref[:, 2 * D:]
        pn = zn_ref[:, D:2 * D] * zn_ref[:, 2 * D:]
        up, dn = _shift_rows(p, pp, pn, n_lat, t_rows, tm)
        w0, w1, w2 = w_ref[0:1, :], w_ref[1:2, :], w_ref[2:3, :]
        conv = w0 * up + w1 * p + w2 * dn
        dyv = dy_ref[...]
        dz_ref[:, :D] = (dyv * conv).astype(BF16)
        dcv = dyv * bg
        dcv_up, dcv_dn = _shift_rows(dcv, dyp_ref[...] * zp_ref[:, :D], dyn_ref[...] * zn_ref[:, :D], n_lat, t_rows, tm)
        dp = w0 * dcv_dn + w1 * dcv + w2 * dcv_up
        dz_ref[:, D:2 * D] = (dp * hb).astype(BF16)
        dz_ref[:, 2 * D:] = (dp * cg).astype(BF16)
        upd = _rows8([(0, _colsum(dcv * up)), (1, _colsum(dcv * p)), (2, _colsum(dcv * dn))], D)

        @pl.when(i == 0)
        def _():
            st_ref[...] = upd

        @pl.when(i > 0)
        def _():
            st_ref[...] += upd

    return pl.pallas_call(
        body, name=name, grid=(t_rows // tm,),
        in_specs=[_row_spec(tm, B_COLS), prev, nxt, _row_spec(tm, D), dprev, dnxt, _full_spec((SUBLANES, D))],
        out_specs=[_row_spec(tm, B_COLS), _full_spec((SUBLANES, D))],
        out_shape=[jax.ShapeDtypeStruct((t_rows, B_COLS), BF16), jax.ShapeDtypeStruct((SUBLANES, D), F32)],
        compiler_params=_params(("arbitrary",)),
    )(z_b, z_b, z_b, dy, dy, dy, w8)


def _merge_fwd(t0, t1, t2, z_g, b_gate, *, name):
    t_rows = t0.shape[0]
    tm = 256

    def body(t0_ref, t1_ref, t2_ref, z_ref, b_ref, o_ref):
        acc = None
        for k, t_ref in enumerate((t0_ref, t1_ref, t2_ref)):
            gate = jax.nn.sigmoid(z_ref[:, k * D:(k + 1) * D] + b_ref[:, k * D:(k + 1) * D])
            term = gate * t_ref[...]
            acc = term if acc is None else acc + term
        o_ref[...] = acc.astype(BF16)

    return pl.pallas_call(
        body, name=name, grid=(t_rows // tm,),
        in_specs=[_row_spec(tm, D)] * 3 + [_row_spec(tm, G_COLS), _full_spec((1, G_COLS))],
        out_specs=_row_spec(tm, D), out_shape=jax.ShapeDtypeStruct((t_rows, D), BF16),
        compiler_params=_params(("parallel",)),
    )(t0, t1, t2, z_g, b_gate)


def _merge_bwd(dmerged, t0, t1, t2, z_g, b_gate, *, name):
    t_rows = t0.shape[0]
    tm = 256

    def body(dm_ref, t0_ref, t1_ref, t2_ref, z_ref, b_ref, d0_ref, d1_ref, d2_ref, dz_ref, st_ref):
        i = pl.program_id(0)
        dm = dm_ref[...]
        sums = []
        for k, (t_ref, d_ref) in enumerate(((t0_ref, d0_ref), (t1_ref, d1_ref), (t2_ref, d2_ref))):
            gate = jax.nn.sigmoid(z_ref[:, k * D:(k + 1) * D] + b_ref[:, k * D:(k + 1) * D])
            d_ref[...] = (dm * gate).astype(BF16)
            dzg = dm * t_ref[...] * gate * (1.0 - gate)
            dz_ref[:, k * D:(k + 1) * D] = dzg.astype(BF16)
            sums.append(_colsum(dzg))
        upd = _rows8([(0, jnp.concatenate(sums, axis=1))], G_COLS)

        @pl.when(i == 0)
        def _():
            st_ref[...] = upd

        @pl.when(i > 0)
        def _():
            st_ref[...] += upd

    return pl.pallas_call(
        body, name=name, grid=(t_rows // tm,),
        in_specs=[_row_spec(tm, D)] * 4 + [_row_spec(tm, G_COLS), _full_spec((1, G_COLS))],
        out_specs=[_row_spec(tm, D)] * 3 + [_row_spec(tm, G_COLS), _full_spec((SUBLANES, G_COLS))],
        out_shape=[jax.ShapeDtypeStruct((t_rows, D), BF16)] * 3
        + [jax.ShapeDtypeStruct((t_rows, G_COLS), BF16), jax.ShapeDtypeStruct((SUBLANES, G_COLS), F32)],
        compiler_params=_params(("arbitrary",)),
    )(dmerged, t0, t1, t2, z_g, b_gate)


def _ffn_mid_fwd(up, w8, n_lat, *, name, hosted=None):
    hosted = hosted or _NO_EXCHANGE
    t_rows = up.shape[0]
    tm = 256
    n_steps = t_rows // tm
    prev, nxt = _halo_specs(tm, D_FF, t_rows)

    def body(*refs):
        ins, outs, _, h_refs = _split_refs(refs, 5, 2, 0, hosted)
        a_ref, ap_ref, an_ref, g_ref, w_ref = ins
        cv_ref, f_ref = outs
        _run_hosted(hosted, h_refs, pl.program_id(0), n_steps)
        a = a_ref[...]
        au, ad = _shift_rows(a, ap_ref[...], an_ref[...], n_lat, t_rows, tm)
        cv = w_ref[0:1, :] * au + w_ref[1:2, :] * a + w_ref[2:3, :] * ad
        cv_ref[...] = cv
        f_ref[...] = (_silu(cv) * g_ref[...]).astype(BF16)

    outs = pl.pallas_call(
        body, name=name, grid=(n_steps,),
        in_specs=[_row_spec(tm, D_FF), prev, nxt, _row_spec(tm, D_FF, 1), _full_spec((SUBLANES, D_FF))]
        + [ANY] * len(hosted.arrays),
        out_specs=[_row_spec(tm, D_FF), _row_spec(tm, D_FF)] + [ANY] * len(hosted.out_shapes),
        out_shape=[jax.ShapeDtypeStruct((t_rows, D_FF), F32), jax.ShapeDtypeStruct((t_rows, D_FF), BF16)]
        + list(hosted.out_shapes),
        scratch_shapes=list(hosted.scratch),
        compiler_params=_params(("arbitrary",)),
    )(up, up, up, up, w8, *hosted.arrays)
    return outs[0], outs[1], outs[2:]


def _ffn_mid_bwd(up, cv, df, w8, n_lat, *, name):
    t_rows = up.shape[0]
    tm = 256
    prev, nxt = _halo_specs(tm, D_FF, t_rows)
    gprev, gnxt = _halo_specs(tm, D_FF, t_rows, 1)

    def body(a_ref, ap_ref, an_ref, g_ref, gp_ref, gn_ref, cv_ref, cp_ref, cn_ref, df_ref, dfp_ref, dfn_ref,
             w_ref, o_ref, st_ref):
        i = pl.program_id(0)
        a = a_ref[...]
        au, ad = _shift_rows(a, ap_ref[...], an_ref[...], n_lat, t_rows, tm)
        cvv = cv_ref[...]
        dfv = df_ref[...]
        sig = jax.nn.sigmoid(cvv)
        o_ref[:, D_FF:] = (dfv * (cvv * sig)).astype(BF16)
        dcv = dfv * g_ref[...] * (sig * (1.0 + cvv * (1.0 - sig)))
        dcv_p = dfp_ref[...] * gp_ref[...] * _dsilu(cp_ref[...])
        dcv_n = dfn_ref[...] * gn_ref[...] * _dsilu(cn_ref[...])
        du, dd = _shift_rows(dcv, dcv_p, dcv_n, n_lat, t_rows, tm)
        o_ref[:, :D_FF] = (w_ref[0:1, :] * dd + w_ref[1:2, :] * dcv + w_ref[2:3, :] * du).astype(BF16)
        upd = _rows8([(0, _colsum(dcv * au)), (1, _colsum(dcv * a)), (2, _colsum(dcv * ad))], D_FF)

        @pl.when(i == 0)
        def _():
            st_ref[...] = upd

        @pl.when(i > 0)
        def _():
            st_ref[...] += upd

    row = _row_spec(tm, D_FF)
    return pl.pallas_call(
        body, name=name, grid=(t_rows // tm,),
        in_specs=[row, prev, nxt, _row_spec(tm, D_FF, 1), gprev, gnxt, row, prev, nxt, row, prev, nxt,
                  _full_spec((SUBLANES, D_FF))],
        out_specs=[_row_spec(tm, 2 * D_FF), _full_spec((SUBLANES, D_FF))],
        out_shape=[jax.ShapeDtypeStruct((t_rows, 2 * D_FF), BF16), jax.ShapeDtypeStruct((SUBLANES, D_FF), F32)],
        compiler_params=_params(("arbitrary",)),
    )(up, up, up, up, up, up, cv, cv, cv, df, df, df, w8)


def _loss_head(x, target, g_final, n_lat, *, name):
    t_rows = x.shape[0]
    tm = 256
    last = n_lat // tm - 1

    def body(x_ref, t_ref, g_ref, dx_ref, st_ref):
        i = pl.program_id(0)
        is_ctx = i * tm >= n_lat
        xv = x_ref[...]
        gv = g_ref[...]
        rstd = lax.rsqrt(jnp.mean(xv * xv, axis=-1, keepdims=True) + EPS)
        rn = xv * rstd
        err = rn * gv - t_ref[...]
        dy = err / D
        e = dy * gv
        dx = rstd * (e - rn * jnp.mean(e * rn, axis=-1, keepdims=True))
        dx_ref[...] = jnp.where(is_ctx, 0.0, dx)
        keep = jnp.where(is_ctx, 0.0, 1.0)
        upd = _rows8([(0, keep * _colsum(dy * rn)), (1, keep * _colsum(err * err))], D)

        @pl.when(i == 0)
        def _():
            st_ref[...] = upd

        @pl.when(i > 0)
        def _():
            st_ref[...] += upd

    return pl.pallas_call(
        body, name=name, grid=(t_rows // tm,),
        in_specs=[_row_spec(tm, D), pl.BlockSpec((tm, D), lambda i: (jnp.minimum(i, last), 0)), _full_spec((1, D))],
        out_specs=[_row_spec(tm, D), _full_spec((SUBLANES, D))],
        out_shape=[jax.ShapeDtypeStruct((t_rows, D), F32), jax.ShapeDtypeStruct((SUBLANES, D), F32)],
        compiler_params=_params(("arbitrary",)),
    )(x, target, g_final)


def _sum_slabs(x, out_dtype, *, name):
    n_slabs, rows, cols = x.shape
    tm = _pick(rows, (432, 256, 192, 128, 64, 32, 24, 16, 8))

    def body(x_ref, o_ref):
        acc = x_ref[0].astype(F32)
        for s in range(1, n_slabs):
            acc = acc + x_ref[s].astype(F32)
        o_ref[...] = acc.astype(o_ref.dtype)

    return pl.pallas_call(
        body, name=name, grid=(rows // tm,),
        in_specs=[pl.BlockSpec((n_slabs, tm, cols), lambda i: (0, i, 0))],
        out_specs=pl.BlockSpec((tm, cols), lambda i: (i, 0)),
        out_shape=jax.ShapeDtypeStruct((rows, cols), out_dtype),
        compiler_params=_params(("parallel",)),
    )(x)


def _add_half(half_idx, a, b, *, name):
    n_slabs, rows, cols = b.shape
    tm = _pick(rows, (432, 256, 192, 128, 96, 64, 32, 16))
    per_half = rows // tm

    def body(half_ref, a_ref, b_ref, o_ref):
        o_ref[...] = (a_ref[...].astype(F32) + b_ref[...].astype(F32)).astype(BF16)

    spec = pl.BlockSpec((1, tm, cols), lambda s, i, half_ref: (s, i, 0))
    a_spec = pl.BlockSpec((1, tm, cols), lambda s, i, half_ref: (s, half_ref[0] * per_half + i, 0))
    return pl.pallas_call(
        body, name=name,
        grid_spec=pltpu.PrefetchScalarGridSpec(num_scalar_prefetch=1, grid=(n_slabs, per_half),
                                               in_specs=[a_spec, spec], out_specs=spec),
        out_shape=jax.ShapeDtypeStruct(b.shape, BF16), compiler_params=_params(("parallel", "parallel")),
    )(half_idx, a, b)


def _adamw(w, g, m, v, *, name, hosted=None):
    hosted = hosted or _NO_EXCHANGE
    rows, cols = w.shape
    tm = _pick(rows, (256, 128, 64, 32, 16, 8))
    n_steps = rows // tm

    def body(*refs):
        ins, outs, _, h_refs = _split_refs(refs, 4, 3, 0, hosted)
        w_ref, g_ref, m_ref, v_ref = ins
        d_ref, nm_ref, nv_ref = outs
        _run_hosted(hosted, h_refs, pl.program_id(0), n_steps)
        gv = g_ref[...]
        nm = ADAM_B1 * m_ref[...] + (1.0 - ADAM_B1) * gv
        nv = ADAM_B2 * v_ref[...] + (1.0 - ADAM_B2) * jnp.square(gv)
        m_hat = nm / (1.0 - ADAM_B1 ** ADAM_STEP)
        v_hat = nv / (1.0 - ADAM_B2 ** ADAM_STEP)
        d_ref[...] = -ADAM_LR * (m_hat / (jnp.sqrt(v_hat) + ADAM_EPS) + ADAM_WD * w_ref[...])
        nm_ref[...] = nm
        nv_ref[...] = nv

    spec = pl.BlockSpec((tm, cols), lambda i: (i, 0))
    shape = jax.ShapeDtypeStruct((rows, cols), F32)
    outs = pl.pallas_call(
        body, name=name, grid=(n_steps,), in_specs=[spec] * 4 + [ANY] * len(hosted.arrays),
        out_specs=[spec] * 3 + [ANY] * len(hosted.out_shapes), out_shape=[shape] * 3 + list(hosted.out_shapes),
        scratch_shapes=list(hosted.scratch), compiler_params=_params(("arbitrary",)),
    )(w, g, m, v, *hosted.arrays)
    return outs[0], outs[1], outs[2], outs[3:]


def _place():
    x, y, c = lax.axis_index("x"), lax.axis_index("y"), lax.axis_index("c")
    chips = [(1 - x, y), (x, 1 - y), (1 - x, 1 - y)]
    return x, y, c, chips


def _remote(src, dst, send_sems, recv_sems, k, to):
    return pltpu.make_async_remote_copy(src_ref=src, dst_ref=dst, send_sem=send_sems.at[k], recv_sem=recv_sems.at[k],
                                        device_id=to, device_id_type=MESH)


HALF_CHUNKS = 2


def _chunks(ref, n):
    step = ref.shape[0] // n
    tile_rows = SUBLANES if ref.dtype == F32 else 2 * SUBLANES
    assert step * n == ref.shape[0] and step % tile_rows == 0, (ref.shape, n)
    return [ref.at[pl.ds(k * step, step)] for k in range(n)]


def _half(ref, which):
    half = ref.shape[0] // 2
    return ref.at[pl.ds(pl.multiple_of(which * half, 2 * SUBLANES), half)]


def _staged_copy(src, dst, buf, sems):
    step = buf.shape[1]
    n = src.shape[0] // step
    assert n * step == src.shape[0], (src.shape, step)
    ins = [pltpu.make_async_copy(src.at[pl.ds(k * step, step)], buf.at[k % 2], sems.at[k % 2]) for k in range(n)]
    outs = [pltpu.make_async_copy(buf.at[k % 2], dst.at[pl.ds(k * step, step)], sems.at[2 + k % 2]) for k in range(n)]
    ins[0].start()
    for k in range(n):
        ins[k].wait()
        outs[k].start()
        if k + 1 < n:
            if k >= 1:
                outs[k - 1].wait()
            ins[k + 1].start()
    if n >= 2:
        outs[n - 2].wait()
    outs[n - 1].wait()


def _stage_rows(rows):
    return _pick(rows, (256, 432))


def _stage_scratch(slabs):
    return [pltpu.VMEM((2, _stage_rows(s.shape[-2]), s.shape[-1]), s.dtype) for s in slabs] + [pltpu.SemaphoreType.DMA((4,))]


N_LINK_SEMS = (N_CHIPS - 1) * HALF_CHUNKS


def _link_sems(n_groups):
    return [pltpu.SemaphoreType.DMA((n_groups * N_LINK_SEMS,)), pltpu.SemaphoreType.DMA((n_groups * N_LINK_SEMS,))]


def _sem_index(g, j, k):
    return g * N_LINK_SEMS + j * HALF_CHUNKS + k


def _gather_ici_start(*refs):
    n = (len(refs) - 2) // 2
    p_refs, o_refs, (send_sems, recv_sems) = refs[:n], refs[n:2 * n], refs[2 * n:]
    x, y, c, chips = _place()
    for g, (p_ref, o_ref) in enumerate(zip(p_refs, o_refs)):
        src = _chunks(_half(p_ref, c), HALF_CHUNKS)
        dst = _chunks(_half(o_ref.at[2 * x + y], c), HALF_CHUNKS)
        for j, chip in enumerate(chips):
            for k in range(HALF_CHUNKS):
                _remote(src[k], dst[k], send_sems, recv_sems, _sem_index(g, j, k), (*chip, c)).start()


def _gather_ici_finish(*refs):
    n = (len(refs) - 2) // 2
    p_refs, o_refs, (send_sems, recv_sems) = refs[:n], refs[n:2 * n], refs[2 * n:]
    x, y, c, chips = _place()
    for g, (p_ref, o_ref) in enumerate(zip(p_refs, o_refs)):
        src = _chunks(_half(p_ref, c), HALF_CHUNKS)
        for j, (cx, cy) in enumerate(chips):
            for k, landed in enumerate(_chunks(_half(o_ref.at[2 * cx + cy], c), HALF_CHUNKS)):
                _remote(src[k], landed, send_sems, recv_sems, _sem_index(g, j, k), (x, y, c)).wait_recv()
        for j in range(len(chips)):
            for k in range(HALF_CHUNKS):
                _remote(src[k], src[k], send_sems, recv_sems, _sem_index(g, j, k), (x, y, c)).wait_send()


def _gathered_shapes(slabs):
    return [jax.ShapeDtypeStruct((N_CHIPS,) + s.shape, s.dtype) for s in slabs]


def _gather_ici_hosted(slabs):
    return _Hosted(list(slabs), _gathered_shapes(slabs), _link_sems(len(slabs)), _gather_ici_start, _gather_ici_finish)


def _gather_ici(slabs, *, name):
    def body(*refs):
        _gather_ici_start(*refs)
        _gather_ici_finish(*refs)

    return pl.pallas_call(
        body, name=name, in_specs=[ANY] * len(slabs), out_specs=[ANY] * len(slabs),
        out_shape=_gathered_shapes(slabs), scratch_shapes=_link_sems(len(slabs)),
    )(*slabs)


def _gather_finish(partials, slabs, *, name):
    n = len(slabs)

    def body(*refs):
        p_refs, o_refs = refs[n:2 * n], refs[2 * n:3 * n]
        send_sems, recv_sems = refs[3 * n:3 * n + 2]
        bufs, loc_sems = refs[3 * n + 2:4 * n + 2], refs[4 * n + 2]
        x, y, c, chips = _place()
        sib = (x, y, 1 - c)
        passed = []
        for g, o_ref in enumerate(o_refs):
            for j, (cx, cy) in enumerate(chips):
                for k, landed in enumerate(_chunks(_half(o_ref.at[2 * cx + cy], c), HALF_CHUNKS)):
                    passed.append(_remote(landed, landed, send_sems, recv_sems, _sem_index(g, j, k), sib))
        for cp in passed:
            cp.start()
        for p_ref, o_ref, buf in zip(p_refs, o_refs, bufs):
            _staged_copy(p_ref, o_ref.at[2 * x + y], buf, loc_sems)
        for g, o_ref in enumerate(o_refs):
            for j, (cx, cy) in enumerate(chips):
                for k, landed in enumerate(_chunks(_half(o_ref.at[2 * cx + cy], 1 - c), HALF_CHUNKS)):
                    _remote(landed, landed, send_sems, recv_sems, _sem_index(g, j, k), sib).wait_recv()
        for cp in passed:
            cp.wait_send()

    return pl.pallas_call(
        body, name=name, in_specs=[ANY] * (2 * n), out_specs=[ANY] * n,
        out_shape=[jax.ShapeDtypeStruct(p.shape, p.dtype) for p in partials],
        input_output_aliases={g: g for g in range(n)}, scratch_shapes=_link_sems(n) + _stage_scratch(slabs),
        compiler_params=_params(),
    )(*partials, *slabs)


def _grad_sibling_swap(g_packs, *, name):
    n = len(g_packs)
    per_group = N_CHIPS * HALF_CHUNKS

    def body(*refs):
        g_refs, got_refs, (send_sems, recv_sems) = refs[:n], refs[n:2 * n], refs[2 * n:]
        x, y, c, _ = _place()
        sib = (x, y, 1 - c)
        swaps = [_remote(src, dst, send_sems, recv_sems, g * per_group + s * HALF_CHUNKS + k, sib)
                 for g, (g_ref, got_ref) in enumerate(zip(g_refs, got_refs))
                 for s in range(N_CHIPS)
                 for k, (src, dst) in enumerate(zip(_chunks(_half(g_ref.at[s], 1 - c), HALF_CHUNKS),
                                                    _chunks(got_ref.at[s], HALF_CHUNKS)))]
        for cp in swaps:
            cp.start()
        for cp in swaps:
            cp.wait_recv()
        for cp in swaps:
            cp.wait_send()

    return pl.pallas_call(
        body, name=name, in_specs=[ANY] * n, out_specs=[ANY] * n,
        out_shape=[jax.ShapeDtypeStruct((N_CHIPS, g.shape[1] // 2, g.shape[2]), g.dtype) for g in g_packs],
        scratch_shapes=[pltpu.SemaphoreType.DMA((n * per_group,)), pltpu.SemaphoreType.DMA((n * per_group,))],
    )(*g_packs)


def _grad_ici_refs(refs):
    n = (len(refs) - 3) // 3
    return refs[:n], refs[n:2 * n], refs[2 * n], refs[2 * n + 1], refs[2 * n + 2:3 * n + 2], refs[3 * n + 2]


def _grad_ici_start(*refs):
    s_refs, o_refs, send_sems, recv_sems, _, _ = _grad_ici_refs(refs)
    x, y, c, chips = _place()
    for g, (s_ref, o_ref) in enumerate(zip(s_refs, o_refs)):
        for j, (cx, cy) in enumerate(chips):
            pairs = zip(_chunks(s_ref.at[2 * cx + cy], HALF_CHUNKS), _chunks(o_ref.at[2 * x + y], HALF_CHUNKS))
            for k, (src, dst) in enumerate(pairs):
                _remote(src, dst, send_sems, recv_sems, _sem_index(g, j, k), (cx, cy, c)).start()


def _grad_ici_finish(*refs):
    s_refs, o_refs, send_sems, recv_sems, bufs, loc_sems = _grad_ici_refs(refs)
    x, y, c, chips = _place()
    me = 2 * x + y
    for s_ref, o_ref, buf in zip(s_refs, o_refs, bufs):
        _staged_copy(s_ref.at[me], o_ref.at[me], buf, loc_sems)
    for g, (s_ref, o_ref) in enumerate(zip(s_refs, o_refs)):
        for j, (cx, cy) in enumerate(chips):
            for k, landed in enumerate(_chunks(o_ref.at[2 * cx + cy], HALF_CHUNKS)):
                _remote(landed, landed, send_sems, recv_sems, _sem_index(g, j, k), (x, y, c)).wait_recv()
        for j, (cx, cy) in enumerate(chips):
            for k, sent in enumerate(_chunks(s_ref.at[2 * cx + cy], HALF_CHUNKS)):
                _remote(sent, sent, send_sems, recv_sems, _sem_index(g, j, k), (x, y, c)).wait_send()


def _grad_ici_hosted(sums):
    return _Hosted(list(sums), [jax.ShapeDtypeStruct(s.shape, s.dtype) for s in sums],
                   _link_sems(len(sums)) + _stage_scratch(sums), _grad_ici_start, _grad_ici_finish)


def _grad_ici(sums, *, name):
    def body(*refs):
        _grad_ici_start(*refs)
        _grad_ici_finish(*refs)

    n = len(sums)
    return pl.pallas_call(
        body, name=name, in_specs=[ANY] * n, out_specs=[ANY] * n,
        out_shape=[jax.ShapeDtypeStruct(s.shape, s.dtype) for s in sums],
        scratch_shapes=_link_sems(n) + _stage_scratch(sums), compiler_params=_params(),
    )(*sums)


def _grad_sibling_share(totals, *, name):
    n = len(totals)
    n_ch = HALF_CHUNKS

    def body(*refs):
        t_refs, o_refs = refs[:n], refs[n:2 * n]
        send_sems, recv_sems = refs[2 * n:2 * n + 2]
        bufs, loc_sems = refs[2 * n + 2:3 * n + 2], refs[3 * n + 2]
        x, y, c, _ = _place()
        sib = (x, y, 1 - c)
        sends = [_remote(src, dst, send_sems, recv_sems, g * n_ch + k, sib)
                 for g, (t_ref, o_ref) in enumerate(zip(t_refs, o_refs))
                 for k, (src, dst) in enumerate(zip(_chunks(t_ref, n_ch), _chunks(_half(o_ref, c), n_ch)))]
        for cp in sends:
            cp.start()
        for t_ref, o_ref, buf in zip(t_refs, o_refs, bufs):
            _staged_copy(t_ref, _half(o_ref, c), buf, loc_sems)
        for g, o_ref in enumerate(o_refs):
            for k, landed in enumerate(_chunks(_half(o_ref, 1 - c), n_ch)):
                _remote(landed, landed, send_sems, recv_sems, g * n_ch + k, sib).wait_recv()
        for cp in sends:
            cp.wait_send()

    return pl.pallas_call(
        body, name=name, in_specs=[ANY] * n, out_specs=[ANY] * n,
        out_shape=[jax.ShapeDtypeStruct((2 * t.shape[0], t.shape[1]), t.dtype) for t in totals],
        scratch_shapes=[pltpu.SemaphoreType.DMA((n * n_ch,)), pltpu.SemaphoreType.DMA((n * n_ch,))] + _stage_scratch(totals),
        compiler_params=_params(),
    )(*totals)


def _allgather8(v, *, name):
    rows, cols = v.shape

    def body(v_ref, o_ref, send_sems, recv_sems, loc_sem):
        x, y, c, chips = _place()
        sib = (x, y, 1 - c)

        def slot(px, py, pc):
            return o_ref.at[4 * px + 2 * py + pc]

        local = pltpu.make_async_copy(v_ref, slot(x, y, c), loc_sem.at[0])
        local.start()
        first = [_remote(v_ref, slot(x, y, c), send_sems, recv_sems, 0, sib)]
        first += [_remote(v_ref, slot(x, y, c), send_sems, recv_sems, 1 + j, (*chip, c)) for j, chip in enumerate(chips)]
        for cp in first:
            cp.start()
        passed = [_remote(slot(*chip, c), slot(*chip, c), send_sems, recv_sems, 4 + j, sib)
                  for j, chip in enumerate(chips)]
        for j, chip in enumerate(chips):
            _remote(v_ref, slot(*chip, c), send_sems, recv_sems, 1 + j, sib).wait_recv()
            passed[j].start()
        _remote(v_ref, slot(x, y, 1 - c), send_sems, recv_sems, 0, sib).wait_recv()
        for j, chip in enumerate(chips):
            _remote(v_ref, slot(*chip, 1 - c), send_sems, recv_sems, 4 + j, sib).wait_recv()
        for cp in first + passed:
            cp.wait_send()
        local.wait()

    return pl.pallas_call(
        body, name=name, in_specs=[ANY], out_specs=ANY, out_shape=jax.ShapeDtypeStruct((N_DEV, rows, cols), v.dtype),
        scratch_shapes=[pltpu.SemaphoreType.DMA((7,)), pltpu.SemaphoreType.DMA((7,)), pltpu.SemaphoreType.DMA((1,))],
    )(v)


_BIG = (("w_mod", (D, 6 * D), 1), ("w_in", (D, IN_W), 1), ("w_branch", (3 * D, D), None), ("w_out", (D, D), 0),
        ("w_up", (D, 2 * D_FF), 1), ("w_down", (D_FF, D), 0))
_COL_SHARDED = ("w_mod", "w_in", "w_up")
_ROW_SHARDED = (("w_branch", 3 * D // N_CHIPS), ("w_out", D // N_CHIPS), ("w_down", D_FF // N_CHIPS))


def _pack_shards(shards, layer):
    rows = jnp.concatenate([shards[n][layer].reshape(r, D) for n, r in _ROW_SHARDED], axis=0)
    return [shards[n][layer] for n in _COL_SHARDED] + [rows]


def _unpack_cols(blk):
    return blk.transpose(1, 0, 2).reshape(blk.shape[1], N_CHIPS * blk.shape[2])


def _unpack_rows(stack):
    out, off = {}, 0
    for name, r in _ROW_SHARDED:
        blk = stack[:, off:off + r, :]
        off += r
        if name == "w_branch":
            out[name] = blk.reshape(N_CHIPS, 3, D // N_CHIPS, D).transpose(1, 0, 2, 3).reshape(3, D, D)
        else:
            out[name] = blk.reshape(N_CHIPS * r, D)
    return out


def _unpack_full(gathered):
    out = {name: _unpack_cols(blk) for name, blk in zip(_COL_SHARDED, gathered)}
    out.update(_unpack_rows(gathered[-1]))
    return out


def _pack_grad_cols(g):
    return g.reshape(g.shape[0], N_CHIPS, g.shape[1] // N_CHIPS).transpose(1, 0, 2)


def _pack_grad_rows(grads):
    parts = []
    for name, r in _ROW_SHARDED:
        g = grads[name]
        if name == "w_branch":
            g = g.reshape(3, N_CHIPS, D // N_CHIPS, D).transpose(1, 0, 2, 3)
        parts.append(g.reshape(N_CHIPS, r, D))
    return jnp.concatenate(parts, axis=1)


def _pack_grads(grads):
    return [_pack_grad_cols(grads[n]) for n in _COL_SHARDED] + [_pack_grad_rows(grads)]


def _unpack_shards(totals, like):
    out = {n: jnp.stack([totals[l][g] for l in range(DEPTH)]) for g, n in enumerate(_COL_SHARDED)}
    off = 0
    for name, r in _ROW_SHARDED:
        out[name] = jnp.stack([totals[l][-1][off:off + r] for l in range(DEPTH)]).reshape(like[name].shape)
        off += r
    return out


def _pad_rows(v, rows):
    return jnp.concatenate([v, jnp.zeros((rows - v.shape[0],) + v.shape[1:], v.dtype)], axis=0)


def _local_step(x_tok, target, c_vec, c_ctx, wfull, small, n_lat, n_ctx):
    ctx = _step_context(c_vec, c_ctx, n_lat, n_ctx)
    saved = []
    xs = x_tok
    for l in range(DEPTH):
        xs, s, _ = _layer_fwd(l, xs, wfull[l], {k: v[l] for k, v in small.items() if k != "g_final"}, ctx)
        saved.append(s)
    dx, sq_err, d_g_final = _loss_bwd(xs, target, small["g_final"], n_lat)
    wgrads, lgrads, d_a128 = [None] * DEPTH, [None] * DEPTH, [None] * DEPTH
    for l in reversed(range(DEPTH)):
        dx, wgrads[l], lgrads[l], d_a128[l], _ = _layer_bwd(l, saved[l], wfull[l], dx, ctx)
    return sq_err, dx, wgrads, _small_grads(lgrads, d_a128, d_g_final, ctx)


def _step_context(c_vec, c_ctx, n_lat, n_ctx):
    cos_t, sin_t = _rope_tables(n_lat, n_ctx)
    a_in = _pad_rows(jnp.stack([c_vec, c_ctx]), LANES)
    a128 = _small(_silu, (LANES, D), a_in, name="cond_silu")
    return dict(cos_t=cos_t, sin_t=sin_t, a_in=a_in, a128=a128, n_lat=n_lat, n_ctx=n_ctx)


def _loss_bwd(xs, target, g_final, n_lat):
    dx, st = _loss_head(xs, target, g_final[None, :], n_lat, name="loss_head")
    return dx, st[1], st[0]


def _small_grads(lgrads, d_a128, d_g_final, ctx):
    d_cond = _small(lambda a, b, cin: (a + b) * _dsilu(cin), (LANES, D), d_a128[0], d_a128[1], ctx["a_in"],
                    name="cond_bwd")
    out = {k: jnp.stack([lgrads[l][k] for l in range(DEPTH)]) for k in lgrads[0]}
    out["c_ctx"] = d_cond[1]
    out["g_final"] = d_g_final
    return out


def _layer_fwd(l, xs, w, sm, ctx, hosted=_NO_EXCHANGE, hosted_gating=_NO_EXCHANGE, hosted_ffn=_NO_EXCHANGE,
               late_weights=None):
    n_lat, n_ctx, cos_t, sin_t, a128 = ctx["n_lat"], ctx["n_ctx"], ctx["cos_t"], ctx["sin_t"], ctx["a128"]
    mod128 = _mm(a128, w["w_mod"], name=f"mod{l}")
    mod8 = _small(lambda m, b: m + b, (SUBLANES, 6 * D), mod128[:SUBLANES], sm["b_mod"][None, :], name=f"mod_bias{l}")
    g_mix = sm["g_mix"][None, :]
    g_ffn = sm["g_ffn"][None, :]
    g_v = sm["g_v"][None, :]
    b_gate = sm["b_gate"][None, :]
    sink_b = jnp.broadcast_to(sm["sink"][:, None], (N_HEADS, LANES))
    b_sb = jnp.broadcast_to(sm["b_spatial"][:, :, None], (A_GROUPS, BLK, LANES))
    w_sconv8 = _pad_rows(sm["w_sconv"], SUBLANES)
    w_fconv8 = _pad_rows(sm["w_fconv"], SUBLANES)
    w_in = w["w_in"]
    w_seg = [w_in[:, SEG[k]:SEG[k + 1]] for k in range(4)]

    h = _norm_mod_fwd(xs, g_mix, mod8, 0, 1, n_lat, name=f"norm1_{l}")
    z_qkv, z_a, z_b, z_g = [_mm(h, w_seg[k], name=f"in_proj{k}_{l}") for k in range(4)]
    q, kd, vd = _qkv_prep(z_qkv, cos_t, sin_t, name=f"qkv_prep{l}")
    y_attn, carried = _attention_fwd(q, kd, vd, sink_b, n_lat, n_ctx, name=f"attn{l}", hosted=hosted)
    if late_weights is not None:
        w = dict(w, **late_weights(carried))
    y_a, carried_gating = _gating_fwd(z_a, sm["w_spatial"], b_sb, g_v, name=f"gating{l}", hosted=hosted_gating)
    y_b = _sconv_fwd(z_b, w_sconv8, n_lat, name=f"sconv{l}")
    ys = (y_attn, y_a, y_b)
    ts = [_mm(ys[k], w["w_branch"][k], name=f"branch{k}_{l}") for k in range(3)]
    merged = _merge_fwd(*ts, z_g, b_gate, name=f"merge{l}")
    mix_out = _mm(merged, w["w_out"], name=f"out_proj{l}")
    x1 = _residual_fwd(xs, mix_out, mod8, 2, n_lat, name=f"res1_{l}")
    h2 = _norm_mod_fwd(x1, g_ffn, mod8, 3, 4, n_lat, name=f"norm2_{l}")
    up = _mm(h2, w["w_up"], name=f"up_proj{l}")
    cv, f, carried_ffn = _ffn_mid_fwd(up, w_fconv8, n_lat, name=f"ffn_mid{l}", hosted=hosted_ffn)
    ffn_out = _mm(f, w["w_down"], name=f"down_proj{l}")
    x2 = _residual_fwd(x1, ffn_out, mod8, 5, n_lat, name=f"res2_{l}")
    saved = dict(x0=xs, mod8=mod8, h=h, z_qkv=z_qkv, z_a=z_a, z_b=z_b, z_g=z_g, q=q, kd=kd, vd=vd, ys=ys, ts=ts,
                 merged=merged, mix_out=mix_out, x1=x1, h2=h2, up=up, cv=cv, f=f, ffn_out=ffn_out, w_seg=w_seg,
                 g_mix=g_mix, g_ffn=g_ffn, g_v=g_v, b_gate=b_gate, sink_b=sink_b, b_sb=b_sb,
                 w_sconv8=w_sconv8, w_fconv8=w_fconv8, w_spatial=sm["w_spatial"])
    return x2, saved, (carried, carried_gating, carried_ffn)


def _layer_bwd(l, s, w, dx, ctx, hosts=None):
    n_lat, n_ctx, cos_t, sin_t, a128 = ctx["n_lat"], ctx["n_ctx"], ctx["cos_t"], ctx["sin_t"], ctx["a128"]
    mod8 = s["mod8"]
    d_ffn, st_gt2 = _residual_bwd(dx, s["ffn_out"], mod8, 5, n_lat, name=f"res2_bwd{l}")
    df = _mm(d_ffn, w["w_down"], tb=True, name=f"down_bwd_x{l}")
    g_down = _mm(s["f"], d_ffn, ta=True, out_dtype=BF16, name=f"down_bwd_w{l}")
    d_up, st_fc = _ffn_mid_bwd(s["up"], s["cv"], df, s["w_fconv8"], n_lat, name=f"ffn_mid_bwd{l}")
    dh2 = _mm(d_up, w["w_up"], tb=True, name=f"up_bwd_x{l}")
    g_up = _mm(s["h2"], d_up, ta=True, out_dtype=BF16, name=f"up_bwd_w{l}")
    dx1, st_n2, _ = _norm_mod_bwd(s["x1"], [dh2], dx, s["g_ffn"], mod8, 4, n_lat, name=f"norm2_bwd{l}")
    d_out, st_gt1 = _residual_bwd(dx1, s["mix_out"], mod8, 2, n_lat, name=f"res1_bwd{l}")
    d_merged = _mm(d_out, w["w_out"], tb=True, name=f"out_bwd_x{l}")
    g_out = _mm(s["merged"], d_out, ta=True, out_dtype=BF16, name=f"out_bwd_w{l}")
    dt0, dt1, dt2, dz_g, st_bg = _merge_bwd(d_merged, *s["ts"], s["z_g"], s["b_gate"], name=f"merge_bwd{l}")
    dts = (dt0, dt1, dt2)
    dys = [_mm(dts[k], w["w_branch"][k], tb=True, name=f"branch{k}_bwd_x{l}") for k in range(3)]
    g_branch = jnp.stack([_mm(s["ys"][k], dts[k], ta=True, out_dtype=BF16, name=f"branch{k}_bwd_w{l}")
                          for k in range(3)])
    early = dict(w_branch=g_branch.reshape(3 * D, D), w_out=g_out, w_up=g_up, w_down=g_down)
    hosts = hosts or {}
    in_attn, in_gating = hosts["early"](early) if "early" in hosts else (_NO_EXCHANGE, _NO_EXCHANGE)
    carried = {}
    dq, dk, dv, d_sink, carried["attn"] = _attention_bwd(s["q"], s["kd"], s["vd"], s["sink_b"], dys[0], n_lat, n_ctx,
                                                         name=f"attn_bwd{l}", hosted=in_attn)
    dz_qkv = _qkv_unprep(dq, dk, dv, cos_t, sin_t, name=f"qkv_unprep{l}")
    dz_a, d_ws, d_bs, st_gv, carried["gating"] = _gating_bwd(s["z_a"], dys[1], s["w_spatial"], s["b_sb"], s["g_v"],
                                                             name=f"gating_bwd{l}", hosted=in_gating)
    dz_b, st_sc = _sconv_bwd(s["z_b"], dys[2], s["w_sconv8"], n_lat, name=f"sconv_bwd{l}")
    dzs = (dz_qkv, dz_a, dz_b, dz_g)
    g_in = jnp.concatenate([_mm(s["h"], dzs[k], ta=True, out_dtype=BF16, name=f"in_bwd_w{k}_{l}")
                            for k in range(4)], axis=1)
    dh_parts = [_mm(dzs[k], s["w_seg"][k], tb=True, name=f"in_bwd_x{k}_{l}") for k in range(4)]
    in_norm1 = hosts["w_in"](g_in) if "w_in" in hosts else _NO_EXCHANGE
    dx0, st_n1, carried["norm1"] = _norm_mod_bwd(s["x0"], dh_parts, dx1, s["g_mix"], mod8, 1, n_lat,
                                                 name=f"norm1_bwd{l}", hosted=in_norm1)
    dmod = jnp.concatenate([st_n1[0:2], st_n1[2:4], st_gt1[0:2], st_n2[0:2], st_n2[2:4], st_gt2[0:2]], axis=1)
    dmod128 = _pad_rows(dmod, LANES)
    g_mod = _mm(a128, dmod128, ta=True, out_dtype=BF16, name=f"mod_bwd_w{l}")
    d_a128 = _mm(dmod128, w["w_mod"], tb=True, name=f"mod_bwd_x{l}")
    wgrads = dict(early, w_mod=g_mod, w_in=g_in)
    lgrads = dict(b_mod=dmod[0] + dmod[1], g_mix=st_n1[4], g_ffn=st_n2[4], b_gate=st_bg[0], sink=d_sink[:, 0],
                  w_spatial=d_ws, b_spatial=d_bs[:, :, 0], g_v=st_gv[0], w_sconv=st_sc[0:3], w_fconv=st_fc[0:3])
    return dx0, wgrads, lgrads, d_a128, carried


_SMALL_ORDER = ("c_ctx", "b_mod", "g_mix", "b_gate", "sink", "w_spatial", "b_spatial", "g_v", "w_sconv", "g_ffn",
                "w_fconv", "g_final")


def _flat_pack(parts, width):
    flat = jnp.concatenate([p.reshape(-1).astype(F32) for p in parts])
    rows = -(-flat.shape[0] // (width * SUBLANES)) * SUBLANES
    flat = jnp.concatenate([flat, jnp.zeros((rows * width - flat.shape[0],), F32)])
    return flat.reshape(rows, width)


def _flat_unpack(packed, likes):
    flat = packed.reshape(-1)
    out, off = [], 0
    for like in likes:
        n = math.prod(like.shape)
        out.append(flat[off:off + n].reshape(like.shape))
        off += n
    return out


def kernel(x, c, ctx, c_ctx, w_mod, b_mod, g_mix, w_in, b_gate, sink, w_spatial, b_spatial, g_v, w_sconv, w_branch, w_out, g_ffn, w_up, w_fconv, w_down, g_final, loss_target, m_c_ctx, m_w_mod, m_b_mod, m_g_mix, m_w_in, m_b_gate, m_sink, m_w_spatial, m_b_spatial, m_g_v, m_w_sconv, m_w_branch, m_w_out, m_g_ffn, m_w_up, m_w_fconv, m_w_down, m_g_final, v_c_ctx, v_w_mod, v_b_mod, v_g_mix, v_w_in, v_b_gate, v_sink, v_w_spatial, v_b_spatial, v_g_v, v_w_sconv, v_w_branch, v_w_out, v_g_ffn, v_w_up, v_w_fconv, v_w_down, v_g_final):
    n_lat, n_ctx = x.shape[1], ctx.shape[1]
    chip = 2 * lax.axis_index("x") + lax.axis_index("y")
    weights = dict(c_ctx=c_ctx, w_mod=w_mod, b_mod=b_mod, g_mix=g_mix, w_in=w_in, b_gate=b_gate, sink=sink,
                   w_spatial=w_spatial, b_spatial=b_spatial, g_v=g_v, w_sconv=w_sconv, w_branch=w_branch, w_out=w_out,
                   g_ffn=g_ffn, w_up=w_up, w_fconv=w_fconv, w_down=w_down, g_final=g_final)
    m_in = dict(c_ctx=m_c_ctx, w_mod=m_w_mod, b_mod=m_b_mod, g_mix=m_g_mix, w_in=m_w_in, b_gate=m_b_gate, sink=m_sink,
                w_spatial=m_w_spatial, b_spatial=m_b_spatial, g_v=m_g_v, w_sconv=m_w_sconv, w_branch=m_w_branch,
                w_out=m_w_out, g_ffn=m_g_ffn, w_up=m_w_up, w_fconv=m_w_fconv, w_down=m_w_down, g_final=m_g_final)
    v_in = dict(c_ctx=v_c_ctx, w_mod=v_w_mod, b_mod=v_b_mod, g_mix=v_g_mix, w_in=v_w_in, b_gate=v_b_gate, sink=v_sink,
                w_spatial=v_w_spatial, b_spatial=v_b_spatial, g_v=v_g_v, w_sconv=v_w_sconv, w_branch=v_w_branch,
                w_out=v_w_out, g_ffn=v_g_ffn, w_up=v_w_up, w_fconv=v_w_fconv, w_down=v_w_down, g_final=v_g_final)
    big_names = [n for n, _, _ in _BIG]

    conv_pack = _flat_pack([w_sconv, w_fconv], LANES)
    conv_all = _allgather8(conv_pack, name="gather_conv_weights")
    conv_parts = [_flat_unpack(conv_all[2 * p], [w_sconv, w_fconv]) for p in range(N_CHIPS)]
    w_sconv_full = jnp.concatenate([cp[0] for cp in conv_parts], axis=-1)
    w_fconv_full = jnp.concatenate([cp[1] for cp in conv_parts], axis=-1)

    small = dict(b_mod=b_mod, g_mix=g_mix, b_gate=b_gate, sink=sink, w_spatial=w_spatial, b_spatial=b_spatial, g_v=g_v,
                 w_sconv=w_sconv_full, g_ffn=g_ffn, w_fconv=w_fconv_full, g_final=g_final)
    x_tok = jnp.concatenate([x[0], ctx[0]], axis=0)
    step = _step_context(c[0], c_ctx, n_lat, n_ctx)
    layer_small = [{k: v[l] for k, v in small.items() if k != "g_final"} for l in range(DEPTH)]
    my_half = lax.axis_index("c").astype(jnp.int32).reshape(1)

    shards = {n: weights[n].astype(BF16) for n in big_names}
    pack = [_pack_shards(shards, l) for l in range(DEPTH)]
    first = _gather_finish(_gather_ici(pack[0][:2], name="gather_ici0"), pack[0][:2], name="gather_finish0")
    w0 = dict(w_mod=_unpack_cols(first[0]), w_in=_unpack_cols(first[1]))

    def layer0_late_weights(carried):
        rest = _gather_finish(list(carried[1:]), pack[0][2:], name="gather_finish0_late")
        w0.update(w_up=_unpack_cols(rest[0]), **_unpack_rows(rest[1]))
        return w0

    xs, saved0, (part_attn, part_gating, part_ffn) = _layer_fwd(
        0, x_tok, w0, layer_small[0], step, hosted=_gather_ici_hosted(pack[1][1:2] + pack[0][2:]),
        hosted_gating=_gather_ici_hosted(pack[1][:1]), hosted_ffn=_gather_ici_hosted(pack[1][2:]),
        late_weights=layer0_late_weights)
    partial1 = list(part_gating) + list(part_attn[:1]) + list(part_ffn)
    w1 = _unpack_full(_gather_finish(partial1, pack[1], name="gather_finish1"))
    xs, saved1, _ = _layer_fwd(1, xs, w1, layer_small[1], step)
    dx, sq_err, d_g_final = _loss_bwd(xs, loss_target[0], g_final, n_lat)
    loss = lax.psum(0.5 * jnp.sum(sq_err) / D, ("x", "y", "c"))

    def reduce_start(g_packs, tag):
        got = _grad_sibling_swap(g_packs, name=f"grad_sibling_swap{tag}")
        return [_add_half(my_half, a, b, name=f"grad_pair_sum{tag}_{g}") for g, (a, b) in enumerate(zip(g_packs, got))]

    def reduce_finish(exchanged, tag):
        sums = [_sum_slabs(e, F32, name=f"grad_chip_sum{tag}_{g}") for g, e in enumerate(exchanged)]
        return _grad_sibling_share(sums, name=f"grad_sibling_share{tag}")

    dx, wgrads1, lgrads1, d_a1, _ = _layer_bwd(1, saved1, w1, dx, step)
    pair_sum1 = reduce_start(_pack_grads(wgrads1), "1")

    def carried_early(early):
        pair_sum0_early = reduce_start([_pack_grad_cols(early["w_up"]), _pack_grad_rows(early)], "0_early")
        return _grad_ici_hosted(pair_sum1 + pair_sum0_early[1:]), _grad_ici_hosted(pair_sum0_early[:1])

    def carried_w_in(g_in):
        return _grad_ici_hosted(reduce_start([_pack_grad_cols(g_in)], "0_in"))

    dx, wgrads0, lgrads0, d_a0, exchanged = _layer_bwd(0, saved0, w0, dx, step,
                                                       hosts=dict(early=carried_early, w_in=carried_w_in))
    total1 = reduce_finish(exchanged["attn"][:4], "1")
    total0_early = reduce_finish(list(exchanged["gating"]) + list(exchanged["attn"][4:]), "0_early")
    total0_in = reduce_finish(exchanged["norm1"], "0_in")
    mod_sum = reduce_start([_pack_grad_cols(wgrads0["w_mod"])], "0_mod")
    sgrads = _small_grads([lgrads0, lgrads1], [d_a0, d_a1], d_g_final, step)
    grad_x = dx[:n_lat][None]

    s_likes = [sgrads[n] for n in _SMALL_ORDER]
    s_all = _allgather8(_flat_pack(s_likes, D), name="gather_small_grads")
    s_tot = _flat_unpack(_sum_slabs(s_all, F32, name="small_grad_sum"), s_likes)
    grads = dict(zip(_SMALL_ORDER, s_tot))
    grads["w_sconv"] = lax.dynamic_slice_in_dim(grads["w_sconv"], chip * w_sconv.shape[-1], w_sconv.shape[-1], axis=2)
    grads["w_fconv"] = lax.dynamic_slice_in_dim(grads["w_fconv"], chip * w_fconv.shape[-1], w_fconv.shape[-1], axis=2)

    delta, new_m, new_v = {}, {}, {}

    def adamw(n, hosted=None):
        cols = weights[n].shape[-1]
        view = lambda a: a.reshape(-1, cols)
        d_, m_, v_, carried = _adamw(view(weights[n]), view(grads[n]), view(m_in[n]), view(v_in[n]), name=f"adamw_{n}",
                                     hosted=hosted)
        delta[n], new_m[n], new_v[n] = (t.reshape(weights[n].shape) for t in (d_, m_, v_))
        return carried

    grads["w_in"] = jnp.stack([total0_in[0], total1[1]])
    total0_mod = reduce_finish(adamw("w_in", _grad_ici_hosted(mod_sum)), "0_mod")
    total0 = list(total0_mod) + list(total0_in) + list(total0_early)
    grads.update(_unpack_shards([total0, total1], {n: weights[n] for n in big_names}))
    for n in big_names:
        if n != "w_in":
            adamw(n)
    likes = [weights[n] for n in _SMALL_ORDER]
    packs = [_flat_pack([src[n] for n in _SMALL_ORDER], D) for src in (weights, grads, m_in, v_in)]
    outs = _adamw(*packs, name="adamw_small")[:3]
    for dst, packed in zip((delta, new_m, new_v), outs):
        for n, val in zip(_SMALL_ORDER, _flat_unpack(packed, likes)):
            dst[n] = val

    order = ("c_ctx", "w_mod", "b_mod", "g_mix", "w_in", "b_gate", "sink", "w_spatial", "b_spatial", "g_v", "w_sconv",
             "w_branch", "w_out", "g_ffn", "w_up", "w_fconv", "w_down", "g_final")
    return (loss, grad_x, *[grads[n] for n in order], *[delta[n] for n in order], *[new_m[n] for n in order],
            *[new_v[n] for n in order])
```

```python
import functools
import math

import jax
import jax.numpy as jnp
from jax import lax
from jax.experimental import pallas as pl
from jax.experimental.pallas import tpu as pltpu

F32 = jnp.float32
BF16 = jnp.bfloat16

D = 1024
DEPTH = 2
GRID_W = 64
N_HEADS = 16
N_KV = 4
GRP = N_HEADS // N_KV
HEAD_DIM = 64
KV_W = N_KV * HEAD_DIM
WINDOW = 128
BLK = 128
ROPE_THETA = 10000.0
A_GROUPS = 8
D_FF = 2816
EPS = 1e-6
NEG = -1e30
QKV_W = D + 2 * KV_W
A_COLS = 2 * D
B_COLS = 3 * D
G_COLS = 3 * D
IN_W = QKV_W + A_COLS + B_COLS + G_COLS
SEG = (0, QKV_W, QKV_W + A_COLS, QKV_W + A_COLS + B_COLS, IN_W)
N_CHIPS = 4
N_DEV = 8
LANES = 128
SUBLANES = 8
VMEM_LIMIT = 48 * 1024 * 1024
VMEM_LIMIT_WIDE = 56 * 1024 * 1024
ADAM_LR = 0.001
ADAM_B1 = 0.9
ADAM_B2 = 0.999
ADAM_EPS = 1e-08
ADAM_WD = 0.01
ADAM_STEP = 10
MESH = pl.DeviceIdType.MESH
ANY = pl.BlockSpec(memory_space=pl.ANY)


def _params(sem=None, vmem=VMEM_LIMIT):
    return pltpu.CompilerParams(dimension_semantics=sem, vmem_limit_bytes=vmem)


def _pick(n, cands):
    for c in cands:
        if n % c == 0:
            return c
    return n


def _rows8(rows, width):
    r = lax.broadcasted_iota(jnp.int32, (SUBLANES, width), 0)
    out = jnp.zeros((SUBLANES, width), F32)
    for idx, v in rows:
        out = out + jnp.where(r == idx, v, 0.0)
    return out


def _sel(mod_ref, k, is_ctx):
    return jnp.where(is_ctx, mod_ref[1:2, k * D:(k + 1) * D], mod_ref[0:1, k * D:(k + 1) * D])


def _colsum(v):
    return jnp.sum(v, axis=0, keepdims=True)


def _mm(a, b, *, name, ta=False, tb=False, out_dtype=F32):
    if ta:
        k_dim, m = a.shape
    else:
        m, k_dim = a.shape
    if tb:
        n, kb = b.shape
    else:
        kb, n = b.shape
    assert k_dim == kb, (a.shape, b.shape, ta, tb)
    tm = _pick(m, (1056, 1024, 1408, 768, 512, 256, 128))
    tn = _pick(n, (1536, 1408, 1024, 768, 512, 256, 128))
    tk = _pick(k_dim, (2048, 1536, 1408, 1024, 768, 512, 256, 128))
    nk = k_dim // tk
    dims = (((0 if ta else 1,), (1 if tb else 0,)), ((), ()))

    def product(a_ref, b_ref):
        return lax.dot_general(a_ref[...].astype(BF16), b_ref[...].astype(BF16), dims, preferred_element_type=F32)

    def body_single(a_ref, b_ref, o_ref):
        o_ref[...] = product(a_ref, b_ref).astype(o_ref.dtype)

    def body_acc(a_ref, b_ref, o_ref, acc_ref):
        k = pl.program_id(2)

        @pl.when(k == 0)
        def _():
            acc_ref[...] = product(a_ref, b_ref)

        @pl.when(k > 0)
        def _():
            acc_ref[...] += product(a_ref, b_ref)

        @pl.when(k == nk - 1)
        def _():
            o_ref[...] = acc_ref[...].astype(o_ref.dtype)

    a_spec = pl.BlockSpec((tk, tm), lambda i, j, k: (k, i)) if ta else pl.BlockSpec((tm, tk), lambda i, j, k: (i, k))
    b_spec = pl.BlockSpec((tn, tk), lambda i, j, k: (j, k)) if tb else pl.BlockSpec((tk, tn), lambda i, j, k: (k, j))
    return pl.pallas_call(
        body_single if nk == 1 else body_acc, name=name, grid=(m // tm, n // tn, nk),
        in_specs=[a_spec, b_spec], out_specs=pl.BlockSpec((tm, tn), lambda i, j, k: (i, j)),
        out_shape=jax.ShapeDtypeStruct((m, n), out_dtype),
        scratch_shapes=[] if nk == 1 else [pltpu.VMEM((tm, tn), F32)],
        compiler_params=_params(("parallel", "parallel", "arbitrary")),
    )(a, b)


def _small(fn, out_shape, *arrays, name):
    def body(*refs):
        refs[-1][...] = fn(*[r[...] for r in refs[:-1]]).astype(refs[-1].dtype)

    return pl.pallas_call(body, name=name, out_shape=jax.ShapeDtypeStruct(out_shape, F32))(*arrays)


def _silu(v):
    return v * jax.nn.sigmoid(v)


def _dsilu(v):
    s = jax.nn.sigmoid(v)
    return s * (1.0 + v * (1.0 - s))


def _row_spec(tm, width, col=0):
    return pl.BlockSpec((tm, width), lambda i: (i, col))


def _full_spec(shape):
    nd = len(shape)
    return pl.BlockSpec(shape, lambda i: (0,) * nd)


def _halo_specs(tm, width, t_rows, col=0):
    per = tm // SUBLANES
    last = t_rows // SUBLANES - 1
    prev = pl.BlockSpec((SUBLANES, width), lambda i: (jnp.maximum(i * per - 1, 0), col))
    nxt = pl.BlockSpec((SUBLANES, width), lambda i: (jnp.minimum((i + 1) * per, last), col))
    return prev, nxt


def _shift_rows(cur, prev8, next8, n_lat, t_rows, tm):
    i = pl.program_id(0)
    row = lax.broadcasted_iota(jnp.int32, (tm, 1), 0)
    g = row + i * tm
    up = pltpu.roll(cur, 1, 0)
    up = jnp.where(row == 0, prev8[SUBLANES - 1:SUBLANES, :], up)
    up = jnp.where((g == 0) | (g == n_lat), 0.0, up)
    dn = pltpu.roll(cur, tm - 1, 0)
    dn = jnp.where(row == tm - 1, next8[0:1, :], dn)
    dn = jnp.where((g == n_lat - 1) | (g == t_rows - 1), 0.0, dn)
    return up, dn


def _norm_mod_fwd(x, g, mod8, sh_idx, sc_idx, n_lat, *, name):
    t_rows = x.shape[0]
    tm = 256

    def body(x_ref, g_ref, mod_ref, o_ref):
        is_ctx = pl.program_id(0) * tm >= n_lat
        xv = x_ref[...]
        rstd = lax.rsqrt(jnp.mean(xv * xv, axis=-1, keepdims=True) + EPS)
        y = xv * rstd * g_ref[...]
        o_ref[...] = (y * (1.0 + _sel(mod_ref, sc_idx, is_ctx)) + _sel(mod_ref, sh_idx, is_ctx)).astype(BF16)

    return pl.pallas_call(
        body, name=name, grid=(t_rows // tm,),
        in_specs=[_row_spec(tm, D), _full_spec((1, D)), _full_spec((SUBLANES, 6 * D))],
        out_specs=_row_spec(tm, D), out_shape=jax.ShapeDtypeStruct((t_rows, D), BF16),
        compiler_params=_params(("parallel",)),
    )(x, g, mod8)


def _norm_mod_bwd(x, dh_parts, dres, g, mod8, sc_idx, n_lat, *, name, hosted=None):
    hosted = hosted or _NO_EXCHANGE
    t_rows = x.shape[0]
    tm = 256
    n_parts = len(dh_parts)
    n_steps = t_rows // tm

    def body(*refs):
        ins, outs, _, h_refs = _split_refs(refs, 4 + n_parts, 2, 0, hosted)
        x_ref, dres_ref, g_ref, mod_ref = ins[:4]
        part_refs = ins[4:]
        dx_ref, st_ref = outs
        i = pl.program_id(0)
        _run_hosted(hosted, h_refs, i, n_steps)
        is_ctx = i * tm >= n_lat
        dh = part_refs[0][...]
        for p in part_refs[1:]:
            dh = dh + p[...]
        xv = x_ref[...]
        gv = g_ref[...]
        rstd = lax.rsqrt(jnp.mean(xv * xv, axis=-1, keepdims=True) + EPS)
        rn = xv * rstd
        dy = dh * (1.0 + _sel(mod_ref, sc_idx, is_ctx))
        e = dy * gv
        dx_ref[...] = dres_ref[...] + rstd * (e - rn * jnp.mean(e * rn, axis=-1, keepdims=True))
        dsh = _colsum(dh)
        dsc = _colsum(dh * (rn * gv))
        dg = _colsum(dy * rn)
        zero = jnp.zeros_like(dsh)
        upd = _rows8([(0, jnp.where(is_ctx, zero, dsh)), (1, jnp.where(is_ctx, dsh, zero)),
                      (2, jnp.where(is_ctx, zero, dsc)), (3, jnp.where(is_ctx, dsc, zero)), (4, dg)], D)

        @pl.when(i == 0)
        def _():
            st_ref[...] = upd

        @pl.when(i > 0)
        def _():
            st_ref[...] += upd

    outs = pl.pallas_call(
        body, name=name, grid=(n_steps,),
        in_specs=[_row_spec(tm, D), _row_spec(tm, D), _full_spec((1, D)), _full_spec((SUBLANES, 6 * D))]
        + [_row_spec(tm, D)] * n_parts + [ANY] * len(hosted.arrays),
        out_specs=[_row_spec(tm, D), _full_spec((SUBLANES, D))] + [ANY] * len(hosted.out_shapes),
        out_shape=[jax.ShapeDtypeStruct((t_rows, D), F32), jax.ShapeDtypeStruct((SUBLANES, D), F32)]
        + list(hosted.out_shapes),
        scratch_shapes=list(hosted.scratch),
        compiler_params=_params(("arbitrary",)),
    )(x, dres, g, mod8, *dh_parts, *hosted.arrays)
    return outs[0], outs[1], outs[2:]


def _residual_fwd(x, branch, mod8, gt_idx, n_lat, *, name):
    t_rows = x.shape[0]
    tm = 256

    def body(x_ref, b_ref, mod_ref, o_ref):
        is_ctx = pl.program_id(0) * tm >= n_lat
        o_ref[...] = x_ref[...] + _sel(mod_ref, gt_idx, is_ctx) * b_ref[...]

    return pl.pallas_call(
        body, name=name, grid=(t_rows // tm,),
        in_specs=[_row_spec(tm, D), _row_spec(tm, D), _full_spec((SUBLANES, 6 * D))],
        out_specs=_row_spec(tm, D), out_shape=jax.ShapeDtypeStruct((t_rows, D), F32),
        compiler_params=_params(("parallel",)),
    )(x, branch, mod8)


def _residual_bwd(dx, branch, mod8, gt_idx, n_lat, *, name):
    t_rows = dx.shape[0]
    tm = 256

    def body(dx_ref, b_ref, mod_ref, o_ref, st_ref):
        i = pl.program_id(0)
        is_ctx = i * tm >= n_lat
        dxv = dx_ref[...]
        o_ref[...] = (dxv * _sel(mod_ref, gt_idx, is_ctx)).astype(BF16)
        dgt = _colsum(dxv * b_ref[...])
        zero = jnp.zeros_like(dgt)
        upd = _rows8([(0, jnp.where(is_ctx, zero, dgt)), (1, jnp.where(is_ctx, dgt, zero))], D)

        @pl.when(i == 0)
        def _():
            st_ref[...] = upd

        @pl.when(i > 0)
        def _():
            st_ref[...] += upd

    return pl.pallas_call(
        body, name=name, grid=(t_rows // tm,),
        in_specs=[_row_spec(tm, D), _row_spec(tm, D), _full_spec((SUBLANES, 6 * D))],
        out_specs=[_row_spec(tm, D), _full_spec((SUBLANES, D))],
        out_shape=[jax.ShapeDtypeStruct((t_rows, D), BF16), jax.ShapeDtypeStruct((SUBLANES, D), F32)],
        compiler_params=_params(("arbitrary",)),
    )(dx, branch, mod8)


def _rope_tables(n_lat, n_ctx):
    rows = n_lat // GRID_W
    row = jnp.broadcast_to(jnp.arange(rows, dtype=F32)[:, None], (rows, GRID_W)).reshape(n_lat)
    col = jnp.broadcast_to(jnp.arange(GRID_W, dtype=F32)[None, :], (rows, GRID_W)).reshape(n_lat)
    half = HEAD_DIM // 2
    inv = ROPE_THETA ** (-jnp.arange(0, half, 2, dtype=F32) / half)
    ang = jnp.concatenate([row[:, None] * inv, col[:, None] * inv], axis=-1)
    cos, sin = jnp.cos(ang), jnp.sin(ang)
    c64 = jnp.concatenate([cos, cos], axis=-1)
    s64 = jnp.concatenate([-sin, sin], axis=-1)
    c64 = jnp.concatenate([c64, jnp.ones((n_ctx, HEAD_DIM), F32)], axis=0)
    s64 = jnp.concatenate([s64, jnp.zeros((n_ctx, HEAD_DIM), F32)], axis=0)
    return jnp.tile(c64, (1, 2)), jnp.tile(s64, (1, 2))


def _swap_halves(v):
    lane = lax.broadcasted_iota(jnp.int32, v.shape, 1)
    return jnp.where(lane % HEAD_DIM < HEAD_DIM // 2, pltpu.roll(v, LANES - HEAD_DIM // 2, 1),
                     pltpu.roll(v, HEAD_DIM // 2, 1))


def _low_half(shape):
    return lax.broadcasted_iota(jnp.int32, shape, 1) < HEAD_DIM


def _qkv_prep(z_qkv, cos_t, sin_t, *, name):
    t_rows = z_qkv.shape[0]
    tm = 256

    def body(z_ref, c_ref, s_ref, q_ref, k_ref, v_ref):
        cv, sv = c_ref[...], s_ref[...]

        def rope(chunk):
            return chunk * cv + _swap_halves(chunk) * sv

        for ch in range(D // LANES):
            roped = rope(z_ref[:, ch * LANES:(ch + 1) * LANES])
            q_ref[:, ch * LANES:(ch + 1) * LANES] = (roped * (HEAD_DIM ** -0.5)).astype(BF16)
        low = _low_half((tm, LANES))
        for pair in range(N_KV // 2):
            for which, ref, roped in ((0, k_ref, True), (1, v_ref, False)):
                off = D + which * KV_W + pair * LANES
                chunk = z_ref[:, off:off + LANES]
                if roped:
                    chunk = rope(chunk)
                other = pltpu.roll(chunk, HEAD_DIM, 1)
                even = jnp.where(low, chunk, other)
                odd = jnp.where(low, other, chunk)
                ref[:, (2 * pair) * LANES:(2 * pair + 1) * LANES] = even.astype(BF16)
                ref[:, (2 * pair + 1) * LANES:(2 * pair + 2) * LANES] = odd.astype(BF16)

    dup_w = N_KV * LANES
    return pl.pallas_call(
        body, name=name, grid=(t_rows // tm,),
        in_specs=[_row_spec(tm, QKV_W), _row_spec(tm, LANES), _row_spec(tm, LANES)],
        out_specs=[_row_spec(tm, D), _row_spec(tm, dup_w), _row_spec(tm, dup_w)],
        out_shape=[jax.ShapeDtypeStruct((t_rows, D), BF16), jax.ShapeDtypeStruct((t_rows, dup_w), BF16),
                   jax.ShapeDtypeStruct((t_rows, dup_w), BF16)],
        compiler_params=_params(("parallel",)),
    )(z_qkv, cos_t, sin_t)


def _qkv_unprep(dq, dk, dv, cos_t, sin_t, *, name):
    t_rows = dq.shape[0]
    tm = 256

    def body(dq_ref, dk_ref, dv_ref, c_ref, s_ref, o_ref):
        cv, sv = c_ref[...], s_ref[...]

        def unrope(chunk):
            return chunk * cv + _swap_halves(chunk * sv)

        for ch in range(D // LANES):
            o_ref[:, ch * LANES:(ch + 1) * LANES] = unrope(dq_ref[:, ch * LANES:(ch + 1) * LANES]).astype(BF16)
        for pair in range(N_KV // 2):
            for which, ref, roped in ((0, dk_ref, True), (1, dv_ref, False)):
                chunk = ref[:, pair * LANES:(pair + 1) * LANES]
                if roped:
                    chunk = unrope(chunk)
                off = D + which * KV_W + pair * LANES
                o_ref[:, off:off + LANES] = chunk.astype(BF16)

    return pl.pallas_call(
        body, name=name, grid=(t_rows // tm,),
        in_specs=[_row_spec(tm, D), _row_spec(tm, KV_W), _row_spec(tm, KV_W), _row_spec(tm, LANES),
                  _row_spec(tm, LANES)],
        out_specs=_row_spec(tm, QKV_W), out_shape=jax.ShapeDtypeStruct((t_rows, QKV_W), BF16),
        compiler_params=_params(("parallel",)),
    )(dq, dk, dv, cos_t, sin_t)


def _attn_specs(n_lat, n_ctx):
    nb = n_lat // BLK
    dup_w = N_KV * LANES

    def ws(j):
        return jnp.clip(j - 1, 0, nb - 3)

    win = [pl.BlockSpec((BLK, dup_w), functools.partial(lambda j, o: (ws(j) + o, 0), o=o)) for o in range(3)]
    ctx = pl.BlockSpec((n_ctx, dup_w), lambda j: (n_lat // n_ctx, 0))
    return nb, ws, win, ctx


def _attn_bias(j, ws_j, nb, n_ctx):
    n_keys = 3 * BLK + n_ctx
    row = lax.broadcasted_iota(jnp.int32, (BLK, n_keys), 0)
    col = lax.broadcasted_iota(jnp.int32, (BLK, n_keys), 1)
    rel = (ws_j - j) * BLK + col - row
    valid = (col >= 3 * BLK) | ((jnp.abs(rel) <= WINDOW) & (j < nb))
    bias = jnp.where(valid, 0.0, NEG)
    return jnp.concatenate([bias] * GRP, axis=0)


def _attn_probs(q_ref, kk, kh, bias, sink_ref):
    low = _low_half((BLK, LANES))
    qs = []
    for g in range(GRP):
        h = GRP * kh + g
        chunk = q_ref[:, (h // 2) * LANES:(h // 2 + 1) * LANES]
        qs.append(jnp.where(low if h % 2 == 0 else ~low, chunk, jnp.zeros_like(chunk)))
    qs = jnp.concatenate(qs, axis=0)
    s = lax.dot_general(qs, kk, (((1,), (1,)), ((), ())), preferred_element_type=F32) + bias
    snk = jnp.concatenate(
        [jnp.broadcast_to(jnp.max(sink_ref[GRP * kh + g:GRP * kh + g + 1, :], axis=1, keepdims=True), (BLK, 1))
         for g in range(GRP)], axis=0)
    m = jnp.maximum(jnp.max(s, axis=-1, keepdims=True), snk)
    p = jnp.exp(s - m)
    p_snk = jnp.exp(snk - m)
    inv = 1.0 / (jnp.sum(p, axis=-1, keepdims=True) + p_snk)
    return qs, p, p_snk, inv


class _Hosted:
    def __init__(self, arrays, out_shapes, scratch, start, finish):
        self.arrays, self.out_shapes, self.scratch, self.start, self.finish = arrays, out_shapes, scratch, start, finish


_NO_EXCHANGE = _Hosted([], [], [], None, None)


def _split_refs(refs, n_in, n_out, n_scratch, hosted):
    hi, ho, hs = len(hosted.arrays), len(hosted.out_shapes), len(hosted.scratch)
    a = n_in + hi
    b = a + n_out + ho
    ins, h_ins = refs[:n_in], refs[n_in:a]
    outs, h_outs = refs[a:a + n_out], refs[a + n_out:b]
    scr, h_scr = refs[b:b + n_scratch], refs[b + n_scratch:b + n_scratch + hs]
    return ins, outs, scr, (h_ins, h_outs, h_scr)


def _run_hosted(hosted, h_refs, step, n_steps):
    if hosted.start is None:
        return

    flat = [r for group in h_refs for r in group]

    @pl.when(step == 0)
    def _():
        hosted.start(*flat)

    @pl.when(step == n_steps - 1)
    def _():
        hosted.finish(*flat)


def _attention_fwd(q, kd, vd, sink_b, n_lat, n_ctx, *, name, hosted=_NO_EXCHANGE):
    t_rows = q.shape[0]
    nb, ws, win, ctx = _attn_specs(n_lat, n_ctx)
    n_steps = t_rows // BLK

    def body(*refs):
        ins, outs, _, h_refs = _split_refs(refs, 10, 1, 0, hosted)
        q_ref, k0, k1, k2, kc, v0, v1, v2, vc, sink_ref = ins
        o_ref, = outs
        j = pl.program_id(0)
        _run_hosted(hosted, h_refs, j, n_steps)
        ws_j = ws(j)
        low = _low_half((BLK, LANES))
        bias = _attn_bias(j, ws_j, nb, n_ctx)
        for kh in range(N_KV):
            sl = slice(kh * LANES, (kh + 1) * LANES)
            kk = jnp.concatenate([k0[:, sl], k1[:, sl], k2[:, sl], kc[:, sl]], axis=0)
            vv = jnp.concatenate([v0[:, sl], v1[:, sl], v2[:, sl], vc[:, sl]], axis=0)
            _, p, _, inv = _attn_probs(q_ref, kk, kh, bias, sink_ref)
            o = jnp.dot(p.astype(BF16), vv, preferred_element_type=F32) * inv
            for half in range(2):
                even = o[(2 * half) * BLK:(2 * half + 1) * BLK]
                odd = o[(2 * half + 1) * BLK:(2 * half + 2) * BLK]
                ch = 2 * kh + half
                o_ref[:, ch * LANES:(ch + 1) * LANES] = jnp.where(low, even, odd).astype(BF16)

    outs = pl.pallas_call(
        body, name=name, grid=(n_steps,),
        in_specs=[_row_spec(BLK, D)] + win + [ctx] + win + [ctx] + [_full_spec((N_HEADS, LANES))]
        + [ANY] * len(hosted.arrays),
        out_specs=[_row_spec(BLK, D)] + [ANY] * len(hosted.out_shapes),
        out_shape=[jax.ShapeDtypeStruct((t_rows, D), BF16)] + list(hosted.out_shapes),
        scratch_shapes=list(hosted.scratch),
        compiler_params=_params(("arbitrary",)),
    )(q, kd, kd, kd, kd, vd, vd, vd, vd, sink_b, *hosted.arrays)
    return outs[0], outs[1:]


def _attention_bwd(q, kd, vd, sink_b, dy, n_lat, n_ctx, *, name, hosted=_NO_EXCHANGE):
    t_rows = q.shape[0]
    nb, ws, win, ctx = _attn_specs(n_lat, n_ctx)
    n_steps = t_rows // BLK

    def body(*refs):
        ins, outs, scr, h_refs = _split_refs(refs, 11, 4, 3, hosted)
        q_ref, k0, k1, k2, kc, v0, v1, v2, vc, sink_ref, dy_ref = ins
        dq_ref, dk_hbm, dv_hbm, ds_ref = outs
        dk_acc, dv_acc, sem = scr
        j = pl.program_id(0)
        _run_hosted(hosted, h_refs, j, n_steps)
        ws_j = ws(j)

        @pl.when(j == 0)
        def _():
            dk_acc[...] = jnp.zeros_like(dk_acc)
            dv_acc[...] = jnp.zeros_like(dv_acc)
            ds_ref[...] = jnp.zeros_like(ds_ref)

        low = _low_half((BLK, LANES))
        low_keys = _low_half((3 * BLK + n_ctx, LANES))
        win_start = pl.multiple_of(ws_j * BLK, BLK)
        scale = HEAD_DIM ** -0.5
        dk_heads, dv_heads = [], []
        bias = _attn_bias(j, ws_j, nb, n_ctx)
        for kh in range(N_KV):
            sl = slice(kh * LANES, (kh + 1) * LANES)
            kk = jnp.concatenate([k0[:, sl], k1[:, sl], k2[:, sl], kc[:, sl]], axis=0)
            vv = jnp.concatenate([v0[:, sl], v1[:, sl], v2[:, sl], vc[:, sl]], axis=0)
            qs, p, p_snk, inv = _attn_probs(q_ref, kk, kh, bias, sink_ref)
            dos = []
            for g in range(GRP):
                h = GRP * kh + g
                chunk = dy_ref[:, (h // 2) * LANES:(h // 2 + 1) * LANES]
                dos.append(jnp.where(low if h % 2 == 0 else ~low, chunk, jnp.zeros_like(chunk)).astype(BF16))
            dos = jnp.concatenate(dos, axis=0)
            dp = lax.dot_general(dos, vv, (((1,), (1,)), ((), ())), preferred_element_type=F32)
            dsum = jnp.sum(p * dp, axis=-1, keepdims=True) * inv
            ds = (p * ((dp - dsum) * inv)).astype(BF16)
            snk_term = p_snk * inv * dsum
            for g in range(GRP):
                contrib = -jnp.sum(snk_term[g * BLK:(g + 1) * BLK], axis=0, keepdims=True)
                ds_ref[GRP * kh + g:GRP * kh + g + 1, :] += jnp.broadcast_to(contrib, (1, LANES))
            dqs = jnp.dot(ds, kk, preferred_element_type=F32) * scale
            for half in range(2):
                even = dqs[(2 * half) * BLK:(2 * half + 1) * BLK]
                odd = dqs[(2 * half + 1) * BLK:(2 * half + 2) * BLK]
                ch = 2 * kh + half
                dq_ref[:, ch * LANES:(ch + 1) * LANES] = jnp.where(low, even, odd)
            dkk = lax.dot_general(ds, qs, (((0,), (0,)), ((), ())), preferred_element_type=F32)
            dvv = lax.dot_general((p * inv).astype(BF16), dos, (((0,), (0,)), ((), ())), preferred_element_type=F32)
            dk_heads.append(dkk + pltpu.roll(dkk, HEAD_DIM, 1))
            dv_heads.append(dvv + pltpu.roll(dvv, HEAD_DIM, 1))
        for pair in range(N_KV // 2):
            sl = slice(pair * LANES, (pair + 1) * LANES)
            for acc, heads in ((dk_acc, dk_heads), (dv_acc, dv_heads)):
                both = jnp.where(low_keys, heads[2 * pair], heads[2 * pair + 1])
                acc[pl.ds(win_start, 3 * BLK), sl] += both[:3 * BLK]
                acc[n_lat:n_lat + n_ctx, sl] += both[3 * BLK:]

        @pl.when(j == n_steps - 1)
        def _():
            ck = pltpu.make_async_copy(dk_acc, dk_hbm, sem.at[0])
            cv = pltpu.make_async_copy(dv_acc, dv_hbm, sem.at[1])
            ck.start()
            cv.start()
            ck.wait()
            cv.wait()

    outs = pl.pallas_call(
        body, name=name, grid=(n_steps,),
        in_specs=[_row_spec(BLK, D)] + win + [ctx] + win + [ctx] + [_full_spec((N_HEADS, LANES)), _row_spec(BLK, D)]
        + [ANY] * len(hosted.arrays),
        out_specs=[_row_spec(BLK, D), ANY, ANY, _full_spec((N_HEADS, LANES))] + [ANY] * len(hosted.out_shapes),
        out_shape=[jax.ShapeDtypeStruct((t_rows, D), F32), jax.ShapeDtypeStruct((t_rows, KV_W), F32),
                   jax.ShapeDtypeStruct((t_rows, KV_W), F32), jax.ShapeDtypeStruct((N_HEADS, LANES), F32)]
        + list(hosted.out_shapes),
        scratch_shapes=[pltpu.VMEM((t_rows, KV_W), F32), pltpu.VMEM((t_rows, KV_W), F32),
                        pltpu.SemaphoreType.DMA((2,))] + list(hosted.scratch),
        compiler_params=_params(("arbitrary",)),
    )(q, kd, kd, kd, kd, vd, vd, vd, vd, sink_b, dy, *hosted.arrays)
    return outs[0], outs[1], outs[2], outs[3], outs[4:]


_GELU_K = math.sqrt(2.0 / math.pi)


def _gelu(v):
    return jax.nn.gelu(v)


def _gelu_and_grad(v):
    t = jnp.tanh(_GELU_K * (v + 0.044715 * (v * v * v)))
    cdf = 0.5 * (1.0 + t)
    return v * cdf, cdf + 0.5 * v * (1.0 - t * t) * _GELU_K * (1.0 + 3.0 * 0.044715 * v * v)


def _gating_fwd(z_a, w_s, b_sb, g_v, *, name, hosted=None):
    hosted = hosted or _NO_EXCHANGE
    t_rows = z_a.shape[0]
    n_steps = t_rows // BLK

    def body(*refs):
        ins, outs, _, h_refs = _split_refs(refs, 4, 1, 0, hosted)
        z_ref, w_ref, b_ref, g_ref = ins
        o_ref, = outs
        _run_hosted(hosted, h_refs, pl.program_id(0), n_steps)
        u = _gelu(z_ref[:, :D])
        v = _gelu(z_ref[:, D:])
        vn = v * lax.rsqrt(jnp.mean(v * v, axis=-1, keepdims=True) + EPS) * g_ref[...]
        for g in range(A_GROUPS):
            sl = slice(g * LANES, (g + 1) * LANES)
            mixed = jnp.dot(w_ref[g].astype(BF16), vn[:, sl].astype(BF16), preferred_element_type=F32) + b_ref[g]
            o_ref[:, sl] = (u[:, sl] * mixed).astype(BF16)

    outs = pl.pallas_call(
        body, name=name, grid=(n_steps,),
        in_specs=[_row_spec(BLK, A_COLS), _full_spec((A_GROUPS, BLK, BLK)), _full_spec((A_GROUPS, BLK, LANES)),
                  _full_spec((1, D))] + [ANY] * len(hosted.arrays),
        out_specs=[_row_spec(BLK, D)] + [ANY] * len(hosted.out_shapes),
        out_shape=[jax.ShapeDtypeStruct((t_rows, D), BF16)] + list(hosted.out_shapes),
        scratch_shapes=list(hosted.scratch),
        compiler_params=_params(("arbitrary",)),
    )(z_a, w_s, b_sb, g_v, *hosted.arrays)
    return outs[0], outs[1:]


def _gating_bwd(z_a, dy, w_s, b_sb, g_v, *, name, hosted=None):
    hosted = hosted or _NO_EXCHANGE
    t_rows = z_a.shape[0]
    n_steps = t_rows // BLK

    def body(*refs):
        ins, outs, _, h_refs = _split_refs(refs, 5, 4, 0, hosted)
        z_ref, dy_ref, w_ref, b_ref, g_ref = ins
        dz_ref, dw_ref, db_ref, st_ref = outs
        i = pl.program_id(0)
        _run_hosted(hosted, h_refs, i, n_steps)

        @pl.when(i == 0)
        def _():
            dw_ref[...] = jnp.zeros_like(dw_ref)
            db_ref[...] = jnp.zeros_like(db_ref)
            st_ref[...] = jnp.zeros_like(st_ref)

        u, du_dz = _gelu_and_grad(z_ref[:, :D])
        v, dv_dz = _gelu_and_grad(z_ref[:, D:])
        gv = g_ref[...]
        rstd = lax.rsqrt(jnp.mean(v * v, axis=-1, keepdims=True) + EPS)
        vh = v * rstd
        vn = vh * gv
        dyv = dy_ref[...]
        dvn = []
        for g in range(A_GROUPS):
            sl = slice(g * LANES, (g + 1) * LANES)
            wg = w_ref[g].astype(BF16)
            vg = vn[:, sl].astype(BF16)
            mixed = jnp.dot(wg, vg, preferred_element_type=F32) + b_ref[g]
            dz_ref[:, sl] = (dyv[:, sl] * mixed * du_dz[:, sl]).astype(BF16)
            dmixed = dyv[:, sl] * u[:, sl]
            dmb = dmixed.astype(BF16)
            dvn.append(lax.dot_general(wg, dmb, (((0,), (0,)), ((), ())), preferred_element_type=F32))
            dw_ref[g] += lax.dot_general(dmb, vg, (((1,), (1,)), ((), ())), preferred_element_type=F32)
            db_ref[g] += jnp.broadcast_to(jnp.sum(dmixed, axis=-1, keepdims=True), (BLK, LANES))
        dvn = jnp.concatenate(dvn, axis=1)
        st_ref[...] += _rows8([(0, _colsum(dvn * vh))], D)
        e = dvn * gv
        dv = rstd * (e - vh * jnp.mean(e * vh, axis=-1, keepdims=True))
        dz_ref[:, D:] = (dv * dv_dz).astype(BF16)

    outs = pl.pallas_call(
        body, name=name, grid=(n_steps,),
        in_specs=[_row_spec(BLK, A_COLS), _row_spec(BLK, D), _full_spec((A_GROUPS, BLK, BLK)),
                  _full_spec((A_GROUPS, BLK, LANES)), _full_spec((1, D))] + [ANY] * len(hosted.arrays),
        out_specs=[_row_spec(BLK, A_COLS), _full_spec((A_GROUPS, BLK, BLK)), _full_spec((A_GROUPS, BLK, LANES)),
                   _full_spec((SUBLANES, D))] + [ANY] * len(hosted.out_shapes),
        out_shape=[jax.ShapeDtypeStruct((t_rows, A_COLS), BF16), jax.ShapeDtypeStruct((A_GROUPS, BLK, BLK), F32),
                   jax.ShapeDtypeStruct((A_GROUPS, BLK, LANES), F32), jax.ShapeDtypeStruct((SUBLANES, D), F32)]
        + list(hosted.out_shapes),
        scratch_shapes=list(hosted.scratch),
        compiler_params=_params(("arbitrary",)),
    )(z_a, dy, w_s, b_sb, g_v, *hosted.arrays)
    return outs[0], outs[1], outs[2], outs[3], outs[4:]


def _sconv_fwd(z_b, w8, n_lat, *, name):
    t_rows = z_b.shape[0]
    tm = 256
    prev, nxt = _halo_specs(tm, B_COLS, t_rows)

    def body(z_ref, zp_ref, zn_ref, w_ref, o_ref):
        p = z_ref[:, D:2 * D] * z_ref[:, 2 * D:]
        pp = zp_ref[:, D:2 * D] * zp_ref[:, 2 * D:]
        pn = zn_ref[:, D:2 * D] * zn_ref[:, 2 * D:]
        up, dn = _shift_rows(p, pp, pn, n_lat, t_rows, tm)
        conv = w_ref[0:1, :] * up + w_ref[1:2, :] * p + w_ref[2:3, :] * dn
        o_ref[...] = (z_ref[:, :D] * conv).astype(BF16)

    return pl.pallas_call(
        body, name=name, grid=(t_rows // tm,),
        in_specs=[_row_spec(tm, B_COLS), prev, nxt, _full_spec((SUBLANES, D))],
        out_specs=_row_spec(tm, D), out_shape=jax.ShapeDtypeStruct((t_rows, D), BF16),
        compiler_params=_params(("parallel",)),
    )(z_b, z_b, z_b, w8)


def _sconv_bwd(z_b, dy, w8, n_lat, *, name):
    t_rows = z_b.shape[0]
    tm = 256
    prev, nxt = _halo_specs(tm, B_COLS, t_rows)
    dprev, dnxt = _halo_specs(tm, D, t_rows)

    def body(z_ref, zp_ref, zn_ref, dy_ref, dyp_ref, dyn_ref, w_ref, dz_ref, st_ref):
        i = pl.program_id(0)
        bg, cg, hb = z_ref[:, :D], z_ref[:, D:2 * D], z_ref[:, 2 * D:]
        p = cg * hb
        pp = zp_ref[:, D:2 * D] * zp_ref[:, 2 * D:]
        pn = zn_ref[:, D:2 * D] * zn_ref[:, 2 * D:]
        up, dn = _shift_rows(p, pp, pn, n_lat, t_rows, tm)
        w0, w1, w2 = w_ref[0:1, :], w_ref[1:2, :], w_ref[2:3, :]
        conv = w0 * up + w1 * p + w2 * dn
        dyv = dy_ref[...]
        dz_ref[:, :D] = (dyv * conv).astype(BF16)
        dcv = dyv * bg
        dcv_up, dcv_dn = _shift_rows(dcv, dyp_ref[...] * zp_ref[:, :D], dyn_ref[...] * zn_ref[:, :D], n_lat, t_rows, tm)
        dp = w0 * dcv_dn + w1 * dcv + w2 * dcv_up
        dz_ref[:, D:2 * D] = (dp * hb).astype(BF16)
        dz_ref[:, 2 * D:] = (dp * cg).astype(BF16)
        upd = _rows8([(0, _colsum(dcv * up)), (1, _colsum(dcv * p)), (2, _colsum(dcv * dn))], D)

        @pl.when(i == 0)
        def _():
            st_ref[...] = upd

        @pl.when(i > 0)
        def _():
            st_ref[...] += upd

    return pl.pallas_call(
        body, name=name, grid=(t_rows // tm,),
        in_specs=[_row_spec(tm, B_COLS), prev, nxt, _row_spec(tm, D), dprev, dnxt, _full_spec((SUBLANES, D))],
        out_specs=[_row_spec(tm, B_COLS), _full_spec((SUBLANES, D))],
        out_shape=[jax.ShapeDtypeStruct((t_rows, B_COLS), BF16), jax.ShapeDtypeStruct((SUBLANES, D), F32)],
        compiler_params=_params(("arbitrary",)),
    )(z_b, z_b, z_b, dy, dy, dy, w8)


def _merge_fwd(t0, t1, t2, z_g, b_gate, *, name):
    t_rows = t0.shape[0]
    tm = 256

    def body(t0_ref, t1_ref, t2_ref, z_ref, b_ref, o_ref):
        acc = None
        for k, t_ref in enumerate((t0_ref, t1_ref, t2_ref)):
            gate = jax.nn.sigmoid(z_ref[:, k * D:(k + 1) * D] + b_ref[:, k * D:(k + 1) * D])
            term = gate * t_ref[...]
            acc = term if acc is None else acc + term
        o_ref[...] = acc.astype(BF16)

    return pl.pallas_call(
        body, name=name, grid=(t_rows // tm,),
        in_specs=[_row_spec(tm, D)] * 3 + [_row_spec(tm, G_COLS), _full_spec((1, G_COLS))],
        out_specs=_row_spec(tm, D), out_shape=jax.ShapeDtypeStruct((t_rows, D), BF16),
        compiler_params=_params(("parallel",)),
    )(t0, t1, t2, z_g, b_gate)


def _merge_bwd(dmerged, t0, t1, t2, z_g, b_gate, *, name):
    t_rows = t0.shape[0]
    tm = 256

    def body(dm_ref, t0_ref, t1_ref, t2_ref, z_ref, b_ref, d0_ref, d1_ref, d2_ref, dz_ref, st_ref):
        i = pl.program_id(0)
        dm = dm_ref[...]
        sums = []
        for k, (t_ref, d_ref) in enumerate(((t0_ref, d0_ref), (t1_ref, d1_ref), (t2_ref, d2_ref))):
            gate = jax.nn.sigmoid(z_ref[:, k * D:(k + 1) * D] + b_ref[:, k * D:(k + 1) * D])
            d_ref[...] = (dm * gate).astype(BF16)
            dzg = dm * t_ref[...] * gate * (1.0 - gate)
            dz_ref[:, k * D:(k + 1) * D] = dzg.astype(BF16)
            sums.append(_colsum(dzg))
        upd = _rows8([(0, jnp.concatenate(sums, axis=1))], G_COLS)

        @pl.when(i == 0)
        def _():
            st_ref[...] = upd

        @pl.when(i > 0)
        def _():
            st_ref[...] += upd

    return pl.pallas_call(
        body, name=name, grid=(t_rows // tm,),
        in_specs=[_row_spec(tm, D)] * 4 + [_row_spec(tm, G_COLS), _full_spec((1, G_COLS))],
        out_specs=[_row_spec(tm, D)] * 3 + [_row_spec(tm, G_COLS), _full_spec((SUBLANES, G_COLS))],
        out_shape=[jax.ShapeDtypeStruct((t_rows, D), BF16)] * 3
        + [jax.ShapeDtypeStruct((t_rows, G_COLS), BF16), jax.ShapeDtypeStruct((SUBLANES, G_COLS), F32)],
        compiler_params=_params(("arbitrary",)),
    )(dmerged, t0, t1, t2, z_g, b_gate)


def _ffn_mid_fwd(up, w8, n_lat, *, name, hosted=None):
    hosted = hosted or _NO_EXCHANGE
    t_rows = up.shape[0]
    tm = 256
    n_steps = t_rows // tm
    prev, nxt = _halo_specs(tm, D_FF, t_rows)

    def body(*refs):
        ins, outs, _, h_refs = _split_refs(refs, 5, 2, 0, hosted)
        a_ref, ap_ref, an_ref, g_ref, w_ref = ins
        cv_ref, f_ref = outs
        _run_hosted(hosted, h_refs, pl.program_id(0), n_steps)
        a = a_ref[...]
        au, ad = _shift_rows(a, ap_ref[...], an_ref[...], n_lat, t_rows, tm)
        cv = w_ref[0:1, :] * au + w_ref[1:2, :] * a + w_ref[2:3, :] * ad
        cv_ref[...] = cv
        f_ref[...] = (_silu(cv) * g_ref[...]).astype(BF16)

    outs = pl.pallas_call(
        body, name=name, grid=(n_steps,),
        in_specs=[_row_spec(tm, D_FF), prev, nxt, _row_spec(tm, D_FF, 1), _full_spec((SUBLANES, D_FF))]
        + [ANY] * len(hosted.arrays),
        out_specs=[_row_spec(tm, D_FF), _row_spec(tm, D_FF)] + [ANY] * len(hosted.out_shapes),
        out_shape=[jax.ShapeDtypeStruct((t_rows, D_FF), F32), jax.ShapeDtypeStruct((t_rows, D_FF), BF16)]
        + list(hosted.out_shapes),
        scratch_shapes=list(hosted.scratch),
        compiler_params=_params(("arbitrary",)),
    )(up, up, up, up, w8, *hosted.arrays)
    return outs[0], outs[1], outs[2:]


def _ffn_mid_bwd(up, cv, df, w8, n_lat, *, name):
    t_rows = up.shape[0]
    tm = 256
    prev, nxt = _halo_specs(tm, D_FF, t_rows)
    gprev, gnxt = _halo_specs(tm, D_FF, t_rows, 1)

    def body(a_ref, ap_ref, an_ref, g_ref, gp_ref, gn_ref, cv_ref, cp_ref, cn_ref, df_ref, dfp_ref, dfn_ref,
             w_ref, o_ref, st_ref):
        i = pl.program_id(0)
        a = a_ref[...]
        au, ad = _shift_rows(a, ap_ref[...], an_ref[...], n_lat, t_rows, tm)
        cvv = cv_ref[...]
        dfv = df_ref[...]
        sig = jax.nn.sigmoid(cvv)
        o_ref[:, D_FF:] = (dfv * (cvv * sig)).astype(BF16)
        dcv = dfv * g_ref[...] * (sig * (1.0 + cvv * (1.0 - sig)))
        dcv_p = dfp_ref[...] * gp_ref[...] * _dsilu(cp_ref[...])
        dcv_n = dfn_ref[...] * gn_ref[...] * _dsilu(cn_ref[...])
        du, dd = _shift_rows(dcv, dcv_p, dcv_n, n_lat, t_rows, tm)
        o_ref[:, :D_FF] = (w_ref[0:1, :] * dd + w_ref[1:2, :] * dcv + w_ref[2:3, :] * du).astype(BF16)
        upd = _rows8([(0, _colsum(dcv * au)), (1, _colsum(dcv * a)), (2, _colsum(dcv * ad))], D_FF)

        @pl.when(i == 0)
        def _():
            st_ref[...] = upd

        @pl.when(i > 0)
        def _():
            st_ref[...] += upd

    row = _row_spec(tm, D_FF)
    return pl.pallas_call(
        body, name=name, grid=(t_rows // tm,),
        in_specs=[row, prev, nxt, _row_spec(tm, D_FF, 1), gprev, gnxt, row, prev, nxt, row, prev, nxt,
                  _full_spec((SUBLANES, D_FF))],
        out_specs=[_row_spec(tm, 2 * D_FF), _full_spec((SUBLANES, D_FF))],
        out_shape=[jax.ShapeDtypeStruct((t_rows, 2 * D_FF), BF16), jax.ShapeDtypeStruct((SUBLANES, D_FF), F32)],
        compiler_params=_params(("arbitrary",), VMEM_LIMIT_WIDE),
    )(up, up, up, up, up, up, cv, cv, cv, df, df, df, w8)


def _loss_head(x, target, g_final, n_lat, *, name):
    t_rows = x.shape[0]
    tm = 256
    last = n_lat // tm - 1

    def body(x_ref, t_ref, g_ref, dx_ref, st_ref):
        i = pl.program_id(0)
        is_ctx = i * tm >= n_lat
        xv = x_ref[...]
        gv = g_ref[...]
        rstd = lax.rsqrt(jnp.mean(xv * xv, axis=-1, keepdims=True) + EPS)
        rn = xv * rstd
        err = rn * gv - t_ref[...]
        dy = err / D
        e = dy * gv
        dx = rstd * (e - rn * jnp.mean(e * rn, axis=-1, keepdims=True))
        dx_ref[...] = jnp.where(is_ctx, 0.0, dx)
        keep = jnp.where(is_ctx, 0.0, 1.0)
        upd = _rows8([(0, keep * _colsum(dy * rn)), (1, keep * _colsum(err * err))], D)

        @pl.when(i == 0)
        def _():
            st_ref[...] = upd

        @pl.when(i > 0)
        def _():
            st_ref[...] += upd

    return pl.pallas_call(
        body, name=name, grid=(t_rows // tm,),
        in_specs=[_row_spec(tm, D), pl.BlockSpec((tm, D), lambda i: (jnp.minimum(i, last), 0)), _full_spec((1, D))],
        out_specs=[_row_spec(tm, D), _full_spec((SUBLANES, D))],
        out_shape=[jax.ShapeDtypeStruct((t_rows, D), F32), jax.ShapeDtypeStruct((SUBLANES, D), F32)],
        compiler_params=_params(("arbitrary",)),
    )(x, target, g_final)


def _sum_slabs(x, out_dtype, *, name):
    n_slabs, rows, cols = x.shape
    tm = _pick(rows, (432, 256, 192, 128, 64, 32, 24, 16, 8))

    def body(x_ref, o_ref):
        acc = x_ref[0].astype(F32)
        for s in range(1, n_slabs):
            acc = acc + x_ref[s].astype(F32)
        o_ref[...] = acc.astype(o_ref.dtype)

    return pl.pallas_call(
        body, name=name, grid=(rows // tm,),
        in_specs=[pl.BlockSpec((n_slabs, tm, cols), lambda i: (0, i, 0))],
        out_specs=pl.BlockSpec((tm, cols), lambda i: (i, 0)),
        out_shape=jax.ShapeDtypeStruct((rows, cols), out_dtype),
        compiler_params=_params(("parallel",)),
    )(x)


def _add_half(half_idx, a, b, *, name):
    n_slabs, rows, cols = b.shape
    tm = _pick(rows, (432, 256, 192, 128, 96, 64, 32, 16))
    per_half = rows // tm

    def body(half_ref, a_ref, b_ref, o_ref):
        o_ref[...] = (a_ref[...].astype(F32) + b_ref[...].astype(F32)).astype(BF16)

    spec = pl.BlockSpec((1, tm, cols), lambda s, i, half_ref: (s, i, 0))
    a_spec = pl.BlockSpec((1, tm, cols), lambda s, i, half_ref: (s, half_ref[0] * per_half + i, 0))
    return pl.pallas_call(
        body, name=name,
        grid_spec=pltpu.PrefetchScalarGridSpec(num_scalar_prefetch=1, grid=(n_slabs, per_half),
                                               in_specs=[a_spec, spec], out_specs=spec),
        out_shape=jax.ShapeDtypeStruct(b.shape, BF16), compiler_params=_params(("parallel", "parallel")),
    )(half_idx, a, b)


def _adamw(w, g, m, v, *, name, hosted=None):
    hosted = hosted or _NO_EXCHANGE
    rows, cols = w.shape
    tm = _pick(rows, (256, 128, 64, 32, 16, 8))
    n_steps = rows // tm

    def body(*refs):
        ins, outs, _, h_refs = _split_refs(refs, 4, 3, 0, hosted)
        w_ref, g_ref, m_ref, v_ref = ins
        d_ref, nm_ref, nv_ref = outs
        _run_hosted(hosted, h_refs, pl.program_id(0), n_steps)
        gv = g_ref[...]
        nm = ADAM_B1 * m_ref[...] + (1.0 - ADAM_B1) * gv
        nv = ADAM_B2 * v_ref[...] + (1.0 - ADAM_B2) * jnp.square(gv)
        m_hat = nm / (1.0 - ADAM_B1 ** ADAM_STEP)
        v_hat = nv / (1.0 - ADAM_B2 ** ADAM_STEP)
        d_ref[...] = -ADAM_LR * (m_hat / (jnp.sqrt(v_hat) + ADAM_EPS) + ADAM_WD * w_ref[...])
        nm_ref[...] = nm
        nv_ref[...] = nv

    spec = pl.BlockSpec((tm, cols), lambda i: (i, 0))
    shape = jax.ShapeDtypeStruct((rows, cols), F32)
    outs = pl.pallas_call(
        body, name=name, grid=(n_steps,), in_specs=[spec] * 4 + [ANY] * len(hosted.arrays),
        out_specs=[spec] * 3 + [ANY] * len(hosted.out_shapes), out_shape=[shape] * 3 + list(hosted.out_shapes),
        scratch_shapes=list(hosted.scratch), compiler_params=_params(("arbitrary",)),
    )(w, g, m, v, *hosted.arrays)
    return outs[0], outs[1], outs[2], outs[3:]


def _place():
    x, y, c = lax.axis_index("x"), lax.axis_index("y"), lax.axis_index("c")
    chips = [(1 - x, y), (x, 1 - y), (1 - x, 1 - y)]
    return x, y, c, chips


def _remote(src, dst, send_sems, recv_sems, k, to):
    return pltpu.make_async_remote_copy(src_ref=src, dst_ref=dst, send_sem=send_sems.at[k], recv_sem=recv_sems.at[k],
                                        device_id=to, device_id_type=MESH)


HALF_CHUNKS = 2


def _chunks(ref, n):
    step = ref.shape[0] // n
    tile_rows = SUBLANES if ref.dtype == F32 else 2 * SUBLANES
    assert step * n == ref.shape[0] and step % tile_rows == 0, (ref.shape, n)
    return [ref.at[pl.ds(k * step, step)] for k in range(n)]


def _half(ref, which):
    half = ref.shape[0] // 2
    return ref.at[pl.ds(pl.multiple_of(which * half, 2 * SUBLANES), half)]


def _staged_copy(src, dst, buf, sems):
    step = buf.shape[1]
    n = src.shape[0] // step
    assert n * step == src.shape[0], (src.shape, step)
    ins = [pltpu.make_async_copy(src.at[pl.ds(k * step, step)], buf.at[k % 2], sems.at[k % 2]) for k in range(n)]
    outs = [pltpu.make_async_copy(buf.at[k % 2], dst.at[pl.ds(k * step, step)], sems.at[2 + k % 2]) for k in range(n)]
    ins[0].start()
    for k in range(n):
        ins[k].wait()
        outs[k].start()
        if k + 1 < n:
            if k >= 1:
                outs[k - 1].wait()
            ins[k + 1].start()
    if n >= 2:
        outs[n - 2].wait()
    outs[n - 1].wait()


def _stage_rows(rows):
    return _pick(rows, (256, 432))


def _stage_scratch(slabs):
    return [pltpu.VMEM((2, _stage_rows(s.shape[-2]), s.shape[-1]), s.dtype) for s in slabs] + [pltpu.SemaphoreType.DMA((4,))]


N_LINK_SEMS = (N_CHIPS - 1) * HALF_CHUNKS


def _link_sems(n_groups):
    return [pltpu.SemaphoreType.DMA((n_groups * N_LINK_SEMS,)), pltpu.SemaphoreType.DMA((n_groups * N_LINK_SEMS,))]


def _sem_index(g, j, k):
    return g * N_LINK_SEMS + j * HALF_CHUNKS + k


def _gather_ici_start(*refs):
    n = (len(refs) - 2) // 2
    p_refs, o_refs, (send_sems, recv_sems) = refs[:n], refs[n:2 * n], refs[2 * n:]
    x, y, c, chips = _place()
    for g, (p_ref, o_ref) in enumerate(zip(p_refs, o_refs)):
        src = _chunks(_half(p_ref, c), HALF_CHUNKS)
        dst = _chunks(_half(o_ref.at[2 * x + y], c), HALF_CHUNKS)
        for j, chip in enumerate(chips):
            for k in range(HALF_CHUNKS):
                _remote(src[k], dst[k], send_sems, recv_sems, _sem_index(g, j, k), (*chip, c)).start()


def _gather_ici_finish(*refs):
    n = (len(refs) - 2) // 2
    p_refs, o_refs, (send_sems, recv_sems) = refs[:n], refs[n:2 * n], refs[2 * n:]
    x, y, c, chips = _place()
    for g, (p_ref, o_ref) in enumerate(zip(p_refs, o_refs)):
        src = _chunks(_half(p_ref, c), HALF_CHUNKS)
        for j, (cx, cy) in enumerate(chips):
            for k, landed in enumerate(_chunks(_half(o_ref.at[2 * cx + cy], c), HALF_CHUNKS)):
                _remote(src[k], landed, send_sems, recv_sems, _sem_index(g, j, k), (x, y, c)).wait_recv()
        for j in range(len(chips)):
            for k in range(HALF_CHUNKS):
                _remote(src[k], src[k], send_sems, recv_sems, _sem_index(g, j, k), (x, y, c)).wait_send()


def _gathered_shapes(slabs):
    return [jax.ShapeDtypeStruct((N_CHIPS,) + s.shape, s.dtype) for s in slabs]


def _gather_ici_hosted(slabs):
    return _Hosted(list(slabs), _gathered_shapes(slabs), _link_sems(len(slabs)), _gather_ici_start, _gather_ici_finish)


def _gather_ici(slabs, *, name):
    def body(*refs):
        _gather_ici_start(*refs)
        _gather_ici_finish(*refs)

    return pl.pallas_call(
        body, name=name, in_specs=[ANY] * len(slabs), out_specs=[ANY] * len(slabs),
        out_shape=_gathered_shapes(slabs), scratch_shapes=_link_sems(len(slabs)),
    )(*slabs)


def _gather_finish(partials, slabs, *, name):
    n = len(slabs)

    def body(*refs):
        p_refs, o_refs = refs[n:2 * n], refs[2 * n:3 * n]
        send_sems, recv_sems = refs[3 * n:3 * n + 2]
        bufs, loc_sems = refs[3 * n + 2:4 * n + 2], refs[4 * n + 2]
        x, y, c, chips = _place()
        sib = (x, y, 1 - c)
        passed = []
        for g, o_ref in enumerate(o_refs):
            for j, (cx, cy) in enumerate(chips):
                for k, landed in enumerate(_chunks(_half(o_ref.at[2 * cx + cy], c), HALF_CHUNKS)):
                    passed.append(_remote(landed, landed, send_sems, recv_sems, _sem_index(g, j, k), sib))
        for cp in passed:
            cp.start()
        for p_ref, o_ref, buf in zip(p_refs, o_refs, bufs):
            _staged_copy(p_ref, o_ref.at[2 * x + y], buf, loc_sems)
        for g, o_ref in enumerate(o_refs):
            for j, (cx, cy) in enumerate(chips):
                for k, landed in enumerate(_chunks(_half(o_ref.at[2 * cx + cy], 1 - c), HALF_CHUNKS)):
                    _remote(landed, landed, send_sems, recv_sems, _sem_index(g, j, k), sib).wait_recv()
        for cp in passed:
            cp.wait_send()

    return pl.pallas_call(
        body, name=name, in_specs=[ANY] * (2 * n), out_specs=[ANY] * n,
        out_shape=[jax.ShapeDtypeStruct(p.shape, p.dtype) for p in partials],
        input_output_aliases={g: g for g in range(n)}, scratch_shapes=_link_sems(n) + _stage_scratch(slabs),
        compiler_params=_params(),
    )(*partials, *slabs)


def _grad_sibling_swap(g_packs, *, name):
    n = len(g_packs)
    per_group = N_CHIPS * HALF_CHUNKS

    def body(*refs):
        g_refs, got_refs, (send_sems, recv_sems) = refs[:n], refs[n:2 * n], refs[2 * n:]
        x, y, c, _ = _place()
        sib = (x, y, 1 - c)
        swaps = [_remote(src, dst, send_sems, recv_sems, g * per_group + s * HALF_CHUNKS + k, sib)
                 for g, (g_ref, got_ref) in enumerate(zip(g_refs, got_refs))
                 for s in range(N_CHIPS)
                 for k, (src, dst) in enumerate(zip(_chunks(_half(g_ref.at[s], 1 - c), HALF_CHUNKS),
                                                    _chunks(got_ref.at[s], HALF_CHUNKS)))]
        for cp in swaps:
            cp.start()
        for cp in swaps:
            cp.wait_recv()
        for cp in swaps:
            cp.wait_send()

    return pl.pallas_call(
        body, name=name, in_specs=[ANY] * n, out_specs=[ANY] * n,
        out_shape=[jax.ShapeDtypeStruct((N_CHIPS, g.shape[1] // 2, g.shape[2]), g.dtype) for g in g_packs],
        scratch_shapes=[pltpu.SemaphoreType.DMA((n * per_group,)), pltpu.SemaphoreType.DMA((n * per_group,))],
    )(*g_packs)


def _grad_ici_refs(refs):
    n = (len(refs) - 3) // 3
    return refs[:n], refs[n:2 * n], refs[2 * n], refs[2 * n + 1], refs[2 * n + 2:3 * n + 2], refs[3 * n + 2]


def _grad_ici_start(*refs):
    s_refs, o_refs, send_sems, recv_sems, _, _ = _grad_ici_refs(refs)
    x, y, c, chips = _place()
    for g, (s_ref, o_ref) in enumerate(zip(s_refs, o_refs)):
        for j, (cx, cy) in enumerate(chips):
            pairs = zip(_chunks(s_ref.at[2 * cx + cy], HALF_CHUNKS), _chunks(o_ref.at[2 * x + y], HALF_CHUNKS))
            for k, (src, dst) in enumerate(pairs):
                _remote(src, dst, send_sems, recv_sems, _sem_index(g, j, k), (cx, cy, c)).start()


def _grad_ici_finish(*refs):
    s_refs, o_refs, send_sems, recv_sems, bufs, loc_sems = _grad_ici_refs(refs)
    x, y, c, chips = _place()
    me = 2 * x + y
    for s_ref, o_ref, buf in zip(s_refs, o_refs, bufs):
        _staged_copy(s_ref.at[me], o_ref.at[me], buf, loc_sems)
    for g, (s_ref, o_ref) in enumerate(zip(s_refs, o_refs)):
        for j, (cx, cy) in enumerate(chips):
            for k, landed in enumerate(_chunks(o_ref.at[2 * cx + cy], HALF_CHUNKS)):
                _remote(landed, landed, send_sems, recv_sems, _sem_index(g, j, k), (x, y, c)).wait_recv()
        for j, (cx, cy) in enumerate(chips):
            for k, sent in enumerate(_chunks(s_ref.at[2 * cx + cy], HALF_CHUNKS)):
                _remote(sent, sent, send_sems, recv_sems, _sem_index(g, j, k), (x, y, c)).wait_send()


def _grad_ici_hosted(sums):
    return _Hosted(list(sums), [jax.ShapeDtypeStruct(s.shape, s.dtype) for s in sums],
                   _link_sems(len(sums)) + _stage_scratch(sums), _grad_ici_start, _grad_ici_finish)


def _grad_ici(sums, *, name):
    def body(*refs):
        _grad_ici_start(*refs)
        _grad_ici_finish(*refs)

    n = len(sums)
    return pl.pallas_call(
        body, name=name, in_specs=[ANY] * n, out_specs=[ANY] * n,
        out_shape=[jax.ShapeDtypeStruct(s.shape, s.dtype) for s in sums],
        scratch_shapes=_link_sems(n) + _stage_scratch(sums), compiler_params=_params(),
    )(*sums)


def _grad_sibling_share(totals, *, name):
    n = len(totals)
    n_ch = HALF_CHUNKS

    def body(*refs):
        t_refs, o_refs = refs[:n], refs[n:2 * n]
        send_sems, recv_sems = refs[2 * n:2 * n + 2]
        bufs, loc_sems = refs[2 * n + 2:3 * n + 2], refs[3 * n + 2]
        x, y, c, _ = _place()
        sib = (x, y, 1 - c)
        sends = [_remote(src, dst, send_sems, recv_sems, g * n_ch + k, sib)
                 for g, (t_ref, o_ref) in enumerate(zip(t_refs, o_refs))
                 for k, (src, dst) in enumerate(zip(_chunks(t_ref, n_ch), _chunks(_half(o_ref, c), n_ch)))]
        for cp in sends:
            cp.start()
        for t_ref, o_ref, buf in zip(t_refs, o_refs, bufs):
            _staged_copy(t_ref, _half(o_ref, c), buf, loc_sems)
        for g, o_ref in enumerate(o_refs):
            for k, landed in enumerate(_chunks(_half(o_ref, 1 - c), n_ch)):
                _remote(landed, landed, send_sems, recv_sems, g * n_ch + k, sib).wait_recv()
        for cp in sends:
            cp.wait_send()

    return pl.pallas_call(
        body, name=name, in_specs=[ANY] * n, out_specs=[ANY] * n,
        out_shape=[jax.ShapeDtypeStruct((2 * t.shape[0], t.shape[1]), t.dtype) for t in totals],
        scratch_shapes=[pltpu.SemaphoreType.DMA((n * n_ch,)), pltpu.SemaphoreType.DMA((n * n_ch,))] + _stage_scratch(totals),
        compiler_params=_params(),
    )(*totals)


def _allgather8(v, *, name):
    rows, cols = v.shape

    def body(v_ref, o_ref, send_sems, recv_sems, loc_sem):
        x, y, c, chips = _place()
        sib = (x, y, 1 - c)

        def slot(px, py, pc):
            return o_ref.at[4 * px + 2 * py + pc]

        local = pltpu.make_async_copy(v_ref, slot(x, y, c), loc_sem.at[0])
        local.start()
        first = [_remote(v_ref, slot(x, y, c), send_sems, recv_sems, 0, sib)]
        first += [_remote(v_ref, slot(x, y, c), send_sems, recv_sems, 1 + j, (*chip, c)) for j, chip in enumerate(chips)]
        for cp in first:
            cp.start()
        passed = [_remote(slot(*chip, c), slot(*chip, c), send_sems, recv_sems, 4 + j, sib)
                  for j, chip in enumerate(chips)]
        for j, chip in enumerate(chips):
            _remote(v_ref, slot(*chip, c), send_sems, recv_sems, 1 + j, sib).wait_recv()
            passed[j].start()
        _remote(v_ref, slot(x, y, 1 - c), send_sems, recv_sems, 0, sib).wait_recv()
        for j, chip in enumerate(chips):
            _remote(v_ref, slot(*chip, 1 - c), send_sems, recv_sems, 4 + j, sib).wait_recv()
        for cp in first + passed:
            cp.wait_send()
        local.wait()

    return pl.pallas_call(
        body, name=name, in_specs=[ANY], out_specs=ANY, out_shape=jax.ShapeDtypeStruct((N_DEV, rows, cols), v.dtype),
        scratch_shapes=[pltpu.SemaphoreType.DMA((7,)), pltpu.SemaphoreType.DMA((7,)), pltpu.SemaphoreType.DMA((1,))],
    )(v)


_BIG = (("w_mod", (D, 6 * D), 1), ("w_in", (D, IN_W), 1), ("w_branch", (3 * D, D), None), ("w_out", (D, D), 0),
        ("w_up", (D, 2 * D_FF), 1), ("w_down", (D_FF, D), 0))
_COL_SHARDED = ("w_mod", "w_in", "w_up")
_ROW_SHARDED = (("w_branch", 3 * D // N_CHIPS), ("w_out", D // N_CHIPS), ("w_down", D_FF // N_CHIPS))


def _pack_shards(shards, layer):
    rows = jnp.concatenate([shards[n][layer].reshape(r, D) for n, r in _ROW_SHARDED], axis=0)
    return [shards[n][layer] for n in _COL_SHARDED] + [rows]


def _unpack_cols(blk):
    return blk.transpose(1, 0, 2).reshape(blk.shape[1], N_CHIPS * blk.shape[2])


def _unpack_rows(stack):
    out, off = {}, 0
    for name, r in _ROW_SHARDED:
        blk = stack[:, off:off + r, :]
        off += r
        if name == "w_branch":
            out[name] = blk.reshape(N_CHIPS, 3, D // N_CHIPS, D).transpose(1, 0, 2, 3).reshape(3, D, D)
        else:
            out[name] = blk.reshape(N_CHIPS * r, D)
    return out


def _unpack_full(gathered):
    out = {name: _unpack_cols(blk) for name, blk in zip(_COL_SHARDED, gathered)}
    out.update(_unpack_rows(gathered[-1]))
    return out


def _pack_grad_cols(g):
    return g.reshape(g.shape[0], N_CHIPS, g.shape[1] // N_CHIPS).transpose(1, 0, 2)


def _pack_grad_rows(grads):
    parts = []
    for name, r in _ROW_SHARDED:
        g = grads[name]
        if name == "w_branch":
            g = g.reshape(3, N_CHIPS, D // N_CHIPS, D).transpose(1, 0, 2, 3)
        parts.append(g.reshape(N_CHIPS, r, D))
    return jnp.concatenate(parts, axis=1)


def _pack_grads(grads):
    return [_pack_grad_cols(grads[n]) for n in _COL_SHARDED] + [_pack_grad_rows(grads)]


def _unpack_shards(totals, like):
    out = {n: jnp.stack([totals[l][g] for l in range(DEPTH)]) for g, n in enumerate(_COL_SHARDED)}
    off = 0
    for name, r in _ROW_SHARDED:
        out[name] = jnp.stack([totals[l][-1][off:off + r] for l in range(DEPTH)]).reshape(like[name].shape)
        off += r
    return out


def _pad_rows(v, rows):
    return jnp.concatenate([v, jnp.zeros((rows - v.shape[0],) + v.shape[1:], v.dtype)], axis=0)


def _local_step(x_tok, target, c_vec, c_ctx, wfull, small, n_lat, n_ctx):
    ctx = _step_context(c_vec, c_ctx, n_lat, n_ctx)
    saved = []
    xs = x_tok
    for l in range(DEPTH):
        xs, s, _ = _layer_fwd(l, xs, wfull[l], {k: v[l] for k, v in small.items() if k != "g_final"}, ctx)
        saved.append(s)
    dx, sq_err, d_g_final = _loss_bwd(xs, target, small["g_final"], n_lat)
    wgrads, lgrads, d_a128 = [None] * DEPTH, [None] * DEPTH, [None] * DEPTH
    for l in reversed(range(DEPTH)):
        dx, wgrads[l], lgrads[l], d_a128[l], _ = _layer_bwd(l, saved[l], wfull[l], dx, ctx)
    return sq_err, dx, wgrads, _small_grads(lgrads, d_a128, d_g_final, ctx)


def _step_context(c_vec, c_ctx, n_lat, n_ctx):
    cos_t, sin_t = _rope_tables(n_lat, n_ctx)
    a_in = _pad_rows(jnp.stack([c_vec, c_ctx]), LANES)
    a128 = _small(_silu, (LANES, D), a_in, name="cond_silu")
    return dict(cos_t=cos_t, sin_t=sin_t, a_in=a_in, a128=a128, n_lat=n_lat, n_ctx=n_ctx)


def _loss_bwd(xs, target, g_final, n_lat):
    dx, st = _loss_head(xs, target, g_final[None, :], n_lat, name="loss_head")
    return dx, st[1], st[0]


def _small_grads(lgrads, d_a128, d_g_final, ctx):
    d_cond = _small(lambda a, b, cin: (a + b) * _dsilu(cin), (LANES, D), d_a128[0], d_a128[1], ctx["a_in"],
                    name="cond_bwd")
    out = {k: jnp.stack([lgrads[l][k] for l in range(DEPTH)]) for k in lgrads[0]}
    out["c_ctx"] = d_cond[1]
    out["g_final"] = d_g_final
    return out


def _layer_fwd(l, xs, w, sm, ctx, hosted=_NO_EXCHANGE, hosted_gating=_NO_EXCHANGE, hosted_ffn=_NO_EXCHANGE,
               late_weights=None):
    n_lat, n_ctx, cos_t, sin_t, a128 = ctx["n_lat"], ctx["n_ctx"], ctx["cos_t"], ctx["sin_t"], ctx["a128"]
    mod128 = _mm(a128, w["w_mod"], name=f"mod{l}")
    mod8 = _small(lambda m, b: m + b, (SUBLANES, 6 * D), mod128[:SUBLANES], sm["b_mod"][None, :], name=f"mod_bias{l}")
    g_mix = sm["g_mix"][None, :]
    g_ffn = sm["g_ffn"][None, :]
    g_v = sm["g_v"][None, :]
    b_gate = sm["b_gate"][None, :]
    sink_b = jnp.broadcast_to(sm["sink"][:, None], (N_HEADS, LANES))
    b_sb = jnp.broadcast_to(sm["b_spatial"][:, :, None], (A_GROUPS, BLK, LANES))
    w_sconv8 = _pad_rows(sm["w_sconv"], SUBLANES)
    w_fconv8 = _pad_rows(sm["w_fconv"], SUBLANES)
    w_in = w["w_in"]
    w_seg = [w_in[:, SEG[k]:SEG[k + 1]] for k in range(4)]

    h = _norm_mod_fwd(xs, g_mix, mod8, 0, 1, n_lat, name=f"norm1_{l}")
    z_qkv, z_a, z_b, z_g = [_mm(h, w_seg[k], name=f"in_proj{k}_{l}") for k in range(4)]
    q, kd, vd = _qkv_prep(z_qkv, cos_t, sin_t, name=f"qkv_prep{l}")
    y_attn, carried = _attention_fwd(q, kd, vd, sink_b, n_lat, n_ctx, name=f"attn{l}", hosted=hosted)
    if late_weights is not None:
        w = dict(w, **late_weights(carried))
    y_a, carried_gating = _gating_fwd(z_a, sm["w_spatial"], b_sb, g_v, name=f"gating{l}", hosted=hosted_gating)
    y_b = _sconv_fwd(z_b, w_sconv8, n_lat, name=f"sconv{l}")
    ys = (y_attn, y_a, y_b)
    ts = [_mm(ys[k], w["w_branch"][k], name=f"branch{k}_{l}") for k in range(3)]
    merged = _merge_fwd(*ts, z_g, b_gate, name=f"merge{l}")
    mix_out = _mm(merged, w["w_out"], name=f"out_proj{l}")
    x1 = _residual_fwd(xs, mix_out, mod8, 2, n_lat, name=f"res1_{l}")
    h2 = _norm_mod_fwd(x1, g_ffn, mod8, 3, 4, n_lat, name=f"norm2_{l}")
    up = _mm(h2, w["w_up"], name=f"up_proj{l}")
    cv, f, carried_ffn = _ffn_mid_fwd(up, w_fconv8, n_lat, name=f"ffn_mid{l}", hosted=hosted_ffn)
    ffn_out = _mm(f, w["w_down"], name=f"down_proj{l}")
    x2 = _residual_fwd(x1, ffn_out, mod8, 5, n_lat, name=f"res2_{l}")
    saved = dict(x0=xs, mod8=mod8, h=h, z_qkv=z_qkv, z_a=z_a, z_b=z_b, z_g=z_g, q=q, kd=kd, vd=vd, ys=ys, ts=ts,
                 merged=merged, mix_out=mix_out, x1=x1, h2=h2, up=up, cv=cv, f=f, ffn_out=ffn_out, w_seg=w_seg,
                 g_mix=g_mix, g_ffn=g_ffn, g_v=g_v, b_gate=b_gate, sink_b=sink_b, b_sb=b_sb,
                 w_sconv8=w_sconv8, w_fconv8=w_fconv8, w_spatial=sm["w_spatial"])
    return x2, saved, (carried, carried_gating, carried_ffn)


def _layer_bwd(l, s, w, dx, ctx, hosts=None):
    n_lat, n_ctx, cos_t, sin_t, a128 = ctx["n_lat"], ctx["n_ctx"], ctx["cos_t"], ctx["sin_t"], ctx["a128"]
    mod8 = s["mod8"]
    d_ffn, st_gt2 = _residual_bwd(dx, s["ffn_out"], mod8, 5, n_lat, name=f"res2_bwd{l}")
    df = _mm(d_ffn, w["w_down"], tb=True, name=f"down_bwd_x{l}")
    g_down = _mm(s["f"], d_ffn, ta=True, out_dtype=BF16, name=f"down_bwd_w{l}")
    d_up, st_fc = _ffn_mid_bwd(s["up"], s["cv"], df, s["w_fconv8"], n_lat, name=f"ffn_mid_bwd{l}")
    dh2 = _mm(d_up, w["w_up"], tb=True, name=f"up_bwd_x{l}")
    g_up = _mm(s["h2"], d_up, ta=True, out_dtype=BF16, name=f"up_bwd_w{l}")
    dx1, st_n2, _ = _norm_mod_bwd(s["x1"], [dh2], dx, s["g_ffn"], mod8, 4, n_lat, name=f"norm2_bwd{l}")
    d_out, st_gt1 = _residual_bwd(dx1, s["mix_out"], mod8, 2, n_lat, name=f"res1_bwd{l}")
    d_merged = _mm(d_out, w["w_out"], tb=True, name=f"out_bwd_x{l}")
    g_out = _mm(s["merged"], d_out, ta=True, out_dtype=BF16, name=f"out_bwd_w{l}")
    dt0, dt1, dt2, dz_g, st_bg = _merge_bwd(d_merged, *s["ts"], s["z_g"], s["b_gate"], name=f"merge_bwd{l}")
    dts = (dt0, dt1, dt2)
    dys = [_mm(dts[k], w["w_branch"][k], tb=True, name=f"branch{k}_bwd_x{l}") for k in range(3)]
    g_branch = jnp.stack([_mm(s["ys"][k], dts[k], ta=True, out_dtype=BF16, name=f"branch{k}_bwd_w{l}")
                          for k in range(3)])
    early = dict(w_branch=g_branch.reshape(3 * D, D), w_out=g_out, w_up=g_up, w_down=g_down)
    hosts = hosts or {}
    in_attn, in_gating = hosts["early"](early) if "early" in hosts else (_NO_EXCHANGE, _NO_EXCHANGE)
    carried = {}
    dq, dk, dv, d_sink, carried["attn"] = _attention_bwd(s["q"], s["kd"], s["vd"], s["sink_b"], dys[0], n_lat, n_ctx,
                                                         name=f"attn_bwd{l}", hosted=in_attn)
    dz_qkv = _qkv_unprep(dq, dk, dv, cos_t, sin_t, name=f"qkv_unprep{l}")
    dz_a, d_ws, d_bs, st_gv, carried["gating"] = _gating_bwd(s["z_a"], dys[1], s["w_spatial"], s["b_sb"], s["g_v"],
                                                             name=f"gating_bwd{l}", hosted=in_gating)
    dz_b, st_sc = _sconv_bwd(s["z_b"], dys[2], s["w_sconv8"], n_lat, name=f"sconv_bwd{l}")
    dzs = (dz_qkv, dz_a, dz_b, dz_g)
    g_in = jnp.concatenate([_mm(s["h"], dzs[k], ta=True, out_dtype=BF16, name=f"in_bwd_w{k}_{l}")
                            for k in range(4)], axis=1)
    dh_parts = [_mm(dzs[k], s["w_seg"][k], tb=True, name=f"in_bwd_x{k}_{l}") for k in range(4)]
    in_norm1 = hosts["w_in"](g_in) if "w_in" in hosts else _NO_EXCHANGE
    dx0, st_n1, carried["norm1"] = _norm_mod_bwd(s["x0"], dh_parts, dx1, s["g_mix"], mod8, 1, n_lat,
                                                 name=f"norm1_bwd{l}", hosted=in_norm1)
    dmod = jnp.concatenate([st_n1[0:2], st_n1[2:4], st_gt1[0:2], st_n2[0:2], st_n2[2:4], st_gt2[0:2]], axis=1)
    dmod128 = _pad_rows(dmod, LANES)
    g_mod = _mm(a128, dmod128, ta=True, out_dtype=BF16, name=f"mod_bwd_w{l}")
    d_a128 = _mm(dmod128, w["w_mod"], tb=True, name=f"mod_bwd_x{l}")
    wgrads = dict(early, w_mod=g_mod, w_in=g_in)
    lgrads = dict(b_mod=dmod[0] + dmod[1], g_mix=st_n1[4], g_ffn=st_n2[4], b_gate=st_bg[0], sink=d_sink[:, 0],
                  w_spatial=d_ws, b_spatial=d_bs[:, :, 0], g_v=st_gv[0], w_sconv=st_sc[0:3], w_fconv=st_fc[0:3])
    return dx0, wgrads, lgrads, d_a128, carried


_SMALL_ORDER = ("c_ctx", "b_mod", "g_mix", "b_gate", "sink", "w_spatial", "b_spatial", "g_v", "w_sconv", "g_ffn",
                "w_fconv", "g_final")


def _flat_pack(parts, width):
    flat = jnp.concatenate([p.reshape(-1).astype(F32) for p in parts])
    rows = -(-flat.shape[0] // (width * SUBLANES)) * SUBLANES
    flat = jnp.concatenate([flat, jnp.zeros((rows * width - flat.shape[0],), F32)])
    return flat.reshape(rows, width)


def _flat_unpack(packed, likes):
    flat = packed.reshape(-1)
    out, off = [], 0
    for like in likes:
        n = math.prod(like.shape)
        out.append(flat[off:off + n].reshape(like.shape))
        off += n
    return out


def kernel(x, c, ctx, c_ctx, w_mod, b_mod, g_mix, w_in, b_gate, sink, w_spatial, b_spatial, g_v, w_sconv, w_branch, w_out, g_ffn, w_up, w_fconv, w_down, g_final, loss_target, m_c_ctx, m_w_mod, m_b_mod, m_g_mix, m_w_in, m_b_gate, m_sink, m_w_spatial, m_b_spatial, m_g_v, m_w_sconv, m_w_branch, m_w_out, m_g_ffn, m_w_up, m_w_fconv, m_w_down, m_g_final, v_c_ctx, v_w_mod, v_b_mod, v_g_mix, v_w_in, v_b_gate, v_sink, v_w_spatial, v_b_spatial, v_g_v, v_w_sconv, v_w_branch, v_w_out, v_g_ffn, v_w_up, v_w_fconv, v_w_down, v_g_final):
    n_lat, n_ctx = x.shape[1], ctx.shape[1]
    chip = 2 * lax.axis_index("x") + lax.axis_index("y")
    weights = dict(c_ctx=c_ctx, w_mod=w_mod, b_mod=b_mod, g_mix=g_mix, w_in=w_in, b_gate=b_gate, sink=sink,
                   w_spatial=w_spatial, b_spatial=b_spatial, g_v=g_v, w_sconv=w_sconv, w_branch=w_branch, w_out=w_out,
                   g_ffn=g_ffn, w_up=w_up, w_fconv=w_fconv, w_down=w_down, g_final=g_final)
    m_in = dict(c_ctx=m_c_ctx, w_mod=m_w_mod, b_mod=m_b_mod, g_mix=m_g_mix, w_in=m_w_in, b_gate=m_b_gate, sink=m_sink,
                w_spatial=m_w_spatial, b_spatial=m_b_spatial, g_v=m_g_v, w_sconv=m_w_sconv, w_branch=m_w_branch,
                w_out=m_w_out, g_ffn=m_g_ffn, w_up=m_w_up, w_fconv=m_w_fconv, w_down=m_w_down, g_final=m_g_final)
    v_in = dict(c_ctx=v_c_ctx, w_mod=v_w_mod, b_mod=v_b_mod, g_mix=v_g_mix, w_in=v_w_in, b_gate=v_b_gate, sink=v_sink,
                w_spatial=v_w_spatial, b_spatial=v_b_spatial, g_v=v_g_v, w_sconv=v_w_sconv, w_branch=v_w_branch,
                w_out=v_w_out, g_ffn=v_g_ffn, w_up=v_w_up, w_fconv=v_w_fconv, w_down=v_w_down, g_final=v_g_final)
    big_names = [n for n, _, _ in _BIG]

    conv_pack = _flat_pack([w_sconv, w_fconv], LANES)
    conv_all = _allgather8(conv_pack, name="gather_conv_weights")
    conv_parts = [_flat_unpack(conv_all[2 * p], [w_sconv, w_fconv]) for p in range(N_CHIPS)]
    w_sconv_full = jnp.concatenate([cp[0] for cp in conv_parts], axis=-1)
    w_fconv_full = jnp.concatenate([cp[1] for cp in conv_parts], axis=-1)

    small = dict(b_mod=b_mod, g_mix=g_mix, b_gate=b_gate, sink=sink, w_spatial=w_spatial, b_spatial=b_spatial, g_v=g_v,
                 w_sconv=w_sconv_full, g_ffn=g_ffn, w_fconv=w_fconv_full, g_final=g_final)
    x_tok = jnp.concatenate([x[0], ctx[0]], axis=0)
    step = _step_context(c[0], c_ctx, n_lat, n_ctx)
    layer_small = [{k: v[l] for k, v in small.items() if k != "g_final"} for l in range(DEPTH)]
    my_half = lax.axis_index("c").astype(jnp.int32).reshape(1)

    shards = {n: weights[n].astype(BF16) for n in big_names}
    pack = [_pack_shards(shards, l) for l in range(DEPTH)]
    first = _gather_finish(_gather_ici(pack[0][:2], name="gather_ici0"), pack[0][:2], name="gather_finish0")
    w0 = dict(w_mod=_unpack_cols(first[0]), w_in=_unpack_cols(first[1]))

    def layer0_late_weights(carried):
        rest = _gather_finish(list(carried[1:]), pack[0][2:], name="gather_finish0_late")
        w0.update(w_up=_unpack_cols(rest[0]), **_unpack_rows(rest[1]))
        return w0

    xs, saved0, (part_attn, part_gating, part_ffn) = _layer_fwd(
        0, x_tok, w0, layer_small[0], step, hosted=_gather_ici_hosted(pack[1][1:2] + pack[0][2:]),
        hosted_gating=_gather_ici_hosted(pack[1][:1]), hosted_ffn=_gather_ici_hosted(pack[1][2:]),
        late_weights=layer0_late_weights)
    partial1 = list(part_gating) + list(part_attn[:1]) + list(part_ffn)
    w1 = _unpack_full(_gather_finish(partial1, pack[1], name="gather_finish1"))
    xs, saved1, _ = _layer_fwd(1, xs, w1, layer_small[1], step)
    dx, sq_err, d_g_final = _loss_bwd(xs, loss_target[0], g_final, n_lat)
    loss = lax.psum(0.5 * jnp.sum(sq_err) / D, ("x", "y", "c"))

    def reduce_start(g_packs, tag):
        got = _grad_sibling_swap(g_packs, name=f"grad_sibling_swap{tag}")
        return [_add_half(my_half, a, b, name=f"grad_pair_sum{tag}_{g}") for g, (a, b) in enumerate(zip(g_packs, got))]

    def reduce_finish(exchanged, tag):
        sums = [_sum_slabs(e, F32, name=f"grad_chip_sum{tag}_{g}") for g, e in enumerate(exchanged)]
        return _grad_sibling_share(sums, name=f"grad_sibling_share{tag}")

    dx, wgrads1, lgrads1, d_a1, _ = _layer_bwd(1, saved1, w1, dx, step)
    pair_sum1 = reduce_start(_pack_grads(wgrads1), "1")

    def carried_early(early):
        pair_sum0_early = reduce_start([_pack_grad_cols(early["w_up"]), _pack_grad_rows(early)], "0_early")
        return _grad_ici_hosted(pair_sum1 + pair_sum0_early[1:]), _grad_ici_hosted(pair_sum0_early[:1])

    def carried_w_in(g_in):
        return _grad_ici_hosted(reduce_start([_pack_grad_cols(g_in)], "0_in"))

    dx, wgrads0, lgrads0, d_a0, exchanged = _layer_bwd(0, saved0, w0, dx, step,
                                                       hosts=dict(early=carried_early, w_in=carried_w_in))
    total1 = reduce_finish(exchanged["attn"][:4], "1")
    total0_early = reduce_finish(list(exchanged["gating"]) + list(exchanged["attn"][4:]), "0_early")
    total0_in = reduce_finish(exchanged["norm1"], "0_in")
    mod_sum = reduce_start([_pack_grad_cols(wgrads0["w_mod"])], "0_mod")
    sgrads = _small_grads([lgrads0, lgrads1], [d_a0, d_a1], d_g_final, step)
    grad_x = dx[:n_lat][None]

    s_likes = [sgrads[n] for n in _SMALL_ORDER]
    s_all = _allgather8(_flat_pack(s_likes, D), name="gather_small_grads")
    s_tot = _flat_unpack(_sum_slabs(s_all, F32, name="small_grad_sum"), s_likes)
    grads = dict(zip(_SMALL_ORDER, s_tot))
    grads["w_sconv"] = lax.dynamic_slice_in_dim(grads["w_sconv"], chip * w_sconv.shape[-1], w_sconv.shape[-1], axis=2)
    grads["w_fconv"] = lax.dynamic_slice_in_dim(grads["w_fconv"], chip * w_fconv.shape[-1], w_fconv.shape[-1], axis=2)

    delta, new_m, new_v = {}, {}, {}

    def adamw(n, hosted=None):
        cols = weights[n].shape[-1]
        view = lambda a: a.reshape(-1, cols)
        d_, m_, v_, carried = _adamw(view(weights[n]), view(grads[n]), view(m_in[n]), view(v_in[n]), name=f"adamw_{n}",
                                     hosted=hosted)
        delta[n], new_m[n], new_v[n] = (t.reshape(weights[n].shape) for t in (d_, m_, v_))
        return carried

    grads["w_in"] = jnp.stack([total0_in[0], total1[1]])
    total0_mod = reduce_finish(adamw("w_in", _grad_ici_hosted(mod_sum)), "0_mod")
    total0 = list(total0_mod) + list(total0_in) + list(total0_early)
    rest = _unpack_shards([total0, total1], {n: weights[n] for n in big_names})
    grads.update({n: g for n, g in rest.items() if n != "w_in"})
    for n in big_names:
        if n != "w_in":
            adamw(n)
    likes = [weights[n] for n in _SMALL_ORDER]
    packs = [_flat_pack([src[n] for n in _SMALL_ORDER], D) for src in (weights, grads, m_in, v_in)]
    outs = _adamw(*packs, name="adamw_small")[:3]
    for dst, packed in zip((delta, new_m, new_v), outs):
        for n, val in zip(_SMALL_ORDER, _flat_unpack(packed, likes)):
            dst[n] = val

    order = ("c_ctx", "w_mod", "b_mod", "g_mix", "w_in", "b_gate", "sink", "w_spatial", "b_spatial", "g_v", "w_sconv",
             "w_branch", "w_out", "g_ffn", "w_up", "w_fconv", "w_down", "g_final")
    return (loss, grad_x, *[grads[n] for n in order], *[delta[n] for n in order], *[new_m[n] for n in order],
            *[new_v[n] for n in order])
```

```python
import functools
import math

import jax
import jax.numpy as jnp
from jax import lax
from jax.experimental import pallas as pl
from jax.experimental.pallas import tpu as pltpu

F32 = jnp.float32
BF16 = jnp.bfloat16

D = 1024
DEPTH = 2
GRID_W = 64
N_HEADS = 16
N_KV = 4
GRP = N_HEADS // N_KV
HEAD_DIM = 64
KV_W = N_KV * HEAD_DIM
WINDOW = 128
BLK = 128
ROPE_THETA = 10000.0
A_GROUPS = 8
D_FF = 2816
EPS = 1e-6
NEG = -1e30
QKV_W = D + 2 * KV_W
A_COLS = 2 * D
B_COLS = 3 * D
G_COLS = 3 * D
IN_W = QKV_W + A_COLS + B_COLS + G_COLS
SEG = (0, QKV_W, QKV_W + A_COLS, QKV_W + A_COLS + B_COLS, IN_W)
N_CHIPS = 4
N_DEV = 8
LANES = 128
SUBLANES = 8
VMEM_LIMIT = 48 * 1024 * 1024
VMEM_LIMIT_WIDE = 56 * 1024 * 1024
ADAM_LR = 0.001
ADAM_B1 = 0.9
ADAM_B2 = 0.999
ADAM_EPS = 1e-08
ADAM_WD = 0.01
ADAM_STEP = 10
MESH = pl.DeviceIdType.MESH
ANY = pl.BlockSpec(memory_space=pl.ANY)


def _params(sem=None, vmem=VMEM_LIMIT):
    return pltpu.CompilerParams(dimension_semantics=sem, vmem_limit_bytes=vmem)


def _pick(n, cands):
    for c in cands:
        if n % c == 0:
            return c
    return n


def _rows8(rows, width):
    r = lax.broadcasted_iota(jnp.int32, (SUBLANES, width), 0)
    out = jnp.zeros((SUBLANES, width), F32)
    for idx, v in rows:
        out = out + jnp.where(r == idx, v, 0.0)
    return out


def _sel(mod_ref, k, is_ctx):
    return jnp.where(is_ctx, mod_ref[1:2, k * D:(k + 1) * D], mod_ref[0:1, k * D:(k + 1) * D])


def _colsum(v):
    return jnp.sum(v, axis=0, keepdims=True)


def _mm(a, b, *, name, ta=False, tb=False, out_dtype=F32):
    if ta:
        k_dim, m = a.shape
    else:
        m, k_dim = a.shape
    if tb:
        n, kb = b.shape
    else:
        kb, n = b.shape
    assert k_dim == kb, (a.shape, b.shape, ta, tb)
    tm = _pick(m, (1056, 1024, 1408, 768, 512, 256, 128))
    tn = _pick(n, (1536, 1408, 1024, 768, 512, 256, 128))
    tk = _pick(k_dim, (2048, 1536, 1408, 1024, 768, 512, 256, 128))
    nk = k_dim // tk
    dims = (((0 if ta else 1,), (1 if tb else 0,)), ((), ()))

    def product(a_ref, b_ref):
        return lax.dot_general(a_ref[...].astype(BF16), b_ref[...].astype(BF16), dims, preferred_element_type=F32)

    def body_single(a_ref, b_ref, o_ref):
        o_ref[...] = product(a_ref, b_ref).astype(o_ref.dtype)

    def body_acc(a_ref, b_ref, o_ref, acc_ref):
        k = pl.program_id(2)

        @pl.when(k == 0)
        def _():
            acc_ref[...] = product(a_ref, b_ref)

        @pl.when(k > 0)
        def _():
            acc_ref[...] += product(a_ref, b_ref)

        @pl.when(k == nk - 1)
        def _():
            o_ref[...] = acc_ref[...].astype(o_ref.dtype)

    a_spec = pl.BlockSpec((tk, tm), lambda i, j, k: (k, i)) if ta else pl.BlockSpec((tm, tk), lambda i, j, k: (i, k))
    b_spec = pl.BlockSpec((tn, tk), lambda i, j, k: (j, k)) if tb else pl.BlockSpec((tk, tn), lambda i, j, k: (k, j))
    return pl.pallas_call(
        body_single if nk == 1 else body_acc, name=name, grid=(m // tm, n // tn, nk),
        in_specs=[a_spec, b_spec], out_specs=pl.BlockSpec((tm, tn), lambda i, j, k: (i, j)),
        out_shape=jax.ShapeDtypeStruct((m, n), out_dtype),
        scratch_shapes=[] if nk == 1 else [pltpu.VMEM((tm, tn), F32)],
        compiler_params=_params(("parallel", "parallel", "arbitrary")),
    )(a, b)


def _small(fn, out_shape, *arrays, name):
    def body(*refs):
        refs[-1][...] = fn(*[r[...] for r in refs[:-1]]).astype(refs[-1].dtype)

    return pl.pallas_call(body, name=name, out_shape=jax.ShapeDtypeStruct(out_shape, F32))(*arrays)


def _silu(v):
    return v * jax.nn.sigmoid(v)


def _dsilu(v):
    s = jax.nn.sigmoid(v)
    return s * (1.0 + v * (1.0 - s))


def _row_spec(tm, width, col=0):
    return pl.BlockSpec((tm, width), lambda i: (i, col))


def _full_spec(shape):
    nd = len(shape)
    return pl.BlockSpec(shape, lambda i: (0,) * nd)


def _halo_specs(tm, width, t_rows, col=0):
    per = tm // SUBLANES
    last = t_rows // SUBLANES - 1
    prev = pl.BlockSpec((SUBLANES, width), lambda i: (jnp.maximum(i * per - 1, 0), col))
    nxt = pl.BlockSpec((SUBLANES, width), lambda i: (jnp.minimum((i + 1) * per, last), col))
    return prev, nxt


def _shift_rows(cur, prev8, next8, n_lat, t_rows, tm):
    i = pl.program_id(0)
    row = lax.broadcasted_iota(jnp.int32, (tm, 1), 0)
    g = row + i * tm
    up = pltpu.roll(cur, 1, 0)
    up = jnp.where(row == 0, prev8[SUBLANES - 1:SUBLANES, :], up)
    up = jnp.where((g == 0) | (g == n_lat), 0.0, up)
    dn = pltpu.roll(cur, tm - 1, 0)
    dn = jnp.where(row == tm - 1, next8[0:1, :], dn)
    dn = jnp.where((g == n_lat - 1) | (g == t_rows - 1), 0.0, dn)
    return up, dn


def _norm_mod_fwd(x, g, mod8, sh_idx, sc_idx, n_lat, *, name):
    t_rows = x.shape[0]
    tm = 256

    def body(x_ref, g_ref, mod_ref, o_ref):
        is_ctx = pl.program_id(0) * tm >= n_lat
        xv = x_ref[...]
        rstd = lax.rsqrt(jnp.mean(xv * xv, axis=-1, keepdims=True) + EPS)
        y = xv * rstd * g_ref[...]
        o_ref[...] = (y * (1.0 + _sel(mod_ref, sc_idx, is_ctx)) + _sel(mod_ref, sh_idx, is_ctx)).astype(BF16)

    return pl.pallas_call(
        body, name=name, grid=(t_rows // tm,),
        in_specs=[_row_spec(tm, D), _full_spec((1, D)), _full_spec((SUBLANES, 6 * D))],
        out_specs=_row_spec(tm, D), out_shape=jax.ShapeDtypeStruct((t_rows, D), BF16),
        compiler_params=_params(("parallel",)),
    )(x, g, mod8)


def _norm_mod_bwd(x, dh_parts, dres, g, mod8, sc_idx, n_lat, *, name, hosted=None):
    hosted = hosted or _NO_EXCHANGE
    t_rows = x.shape[0]
    tm = 256
    n_parts = len(dh_parts)
    n_steps = t_rows // tm

    def body(*refs):
        ins, outs, _, h_refs = _split_refs(refs, 4 + n_parts, 2, 0, hosted)
        x_ref, dres_ref, g_ref, mod_ref = ins[:4]
        part_refs = ins[4:]
        dx_ref, st_ref = outs
        i = pl.program_id(0)
        _run_hosted(hosted, h_refs, i, n_steps)
        is_ctx = i * tm >= n_lat
        dh = part_refs[0][...]
        for p in part_refs[1:]:
            dh = dh + p[...]
        xv = x_ref[...]
        gv = g_ref[...]
        rstd = lax.rsqrt(jnp.mean(xv * xv, axis=-1, keepdims=True) + EPS)
        rn = xv * rstd
        dy = dh * (1.0 + _sel(mod_ref, sc_idx, is_ctx))
        e = dy * gv
        dx_ref[...] = dres_ref[...] + rstd * (e - rn * jnp.mean(e * rn, axis=-1, keepdims=True))
        dsh = _colsum(dh)
        dsc = _colsum(dh * (rn * gv))
        dg = _colsum(dy * rn)
        zero = jnp.zeros_like(dsh)
        upd = _rows8([(0, jnp.where(is_ctx, zero, dsh)), (1, jnp.where(is_ctx, dsh, zero)),
                      (2, jnp.where(is_ctx, zero, dsc)), (3, jnp.where(is_ctx, dsc, zero)), (4, dg)], D)

        @pl.when(i == 0)
        def _():
            st_ref[...] = upd

        @pl.when(i > 0)
        def _():
            st_ref[...] += upd

    outs = pl.pallas_call(
        body, name=name, grid=(n_steps,),
        in_specs=[_row_spec(tm, D), _row_spec(tm, D), _full_spec((1, D)), _full_spec((SUBLANES, 6 * D))]
        + [_row_spec(tm, D)] * n_parts + [ANY] * len(hosted.arrays),
        out_specs=[_row_spec(tm, D), _full_spec((SUBLANES, D))] + [ANY] * len(hosted.out_shapes),
        out_shape=[jax.ShapeDtypeStruct((t_rows, D), F32), jax.ShapeDtypeStruct((SUBLANES, D), F32)]
        + list(hosted.out_shapes),
        scratch_shapes=list(hosted.scratch),
        compiler_params=_params(("arbitrary",)),
    )(x, dres, g, mod8, *dh_parts, *hosted.arrays)
    return outs[0], outs[1], outs[2:]


def _residual_fwd(x, branch, mod8, gt_idx, n_lat, *, name):
    t_rows = x.shape[0]
    tm = 256

    def body(x_ref, b_ref, mod_ref, o_ref):
        is_ctx = pl.program_id(0) * tm >= n_lat
        o_ref[...] = x_ref[...] + _sel(mod_ref, gt_idx, is_ctx) * b_ref[...]

    return pl.pallas_call(
        body, name=name, grid=(t_rows // tm,),
        in_specs=[_row_spec(tm, D), _row_spec(tm, D), _full_spec((SUBLANES, 6 * D))],
        out_specs=_row_spec(tm, D), out_shape=jax.ShapeDtypeStruct((t_rows, D), F32),
        compiler_params=_params(("parallel",)),
    )(x, branch, mod8)


def _residual_bwd(dx, branch, mod8, gt_idx, n_lat, *, name):
    t_rows = dx.shape[0]
    tm = 256

    def body(dx_ref, b_ref, mod_ref, o_ref, st_ref):
        i = pl.program_id(0)
        is_ctx = i * tm >= n_lat
        dxv = dx_ref[...]
        o_ref[...] = (dxv * _sel(mod_ref, gt_idx, is_ctx)).astype(BF16)
        dgt = _colsum(dxv * b_ref[...])
        zero = jnp.zeros_like(dgt)
        upd = _rows8([(0, jnp.where(is_ctx, zero, dgt)), (1, jnp.where(is_ctx, dgt, zero))], D)

        @pl.when(i == 0)
        def _():
            st_ref[...] = upd

        @pl.when(i > 0)
        def _():
            st_ref[...] += upd

    return pl.pallas_call(
        body, name=name, grid=(t_rows // tm,),
        in_specs=[_row_spec(tm, D), _row_spec(tm, D), _full_spec((SUBLANES, 6 * D))],
        out_specs=[_row_spec(tm, D), _full_spec((SUBLANES, D))],
        out_shape=[jax.ShapeDtypeStruct((t_rows, D), BF16), jax.ShapeDtypeStruct((SUBLANES, D), F32)],
        compiler_params=_params(("arbitrary",)),
    )(dx, branch, mod8)


def _rope_tables(n_lat, n_ctx):
    rows = n_lat // GRID_W
    row = jnp.broadcast_to(jnp.arange(rows, dtype=F32)[:, None], (rows, GRID_W)).reshape(n_lat)
    col = jnp.broadcast_to(jnp.arange(GRID_W, dtype=F32)[None, :], (rows, GRID_W)).reshape(n_lat)
    half = HEAD_DIM // 2
    inv = ROPE_THETA ** (-jnp.arange(0, half, 2, dtype=F32) / half)
    ang = jnp.concatenate([row[:, None] * inv, col[:, None] * inv], axis=-1)
    cos, sin = jnp.cos(ang), jnp.sin(ang)
    c64 = jnp.concatenate([cos, cos], axis=-1)
    s64 = jnp.concatenate([-sin, sin], axis=-1)
    c64 = jnp.concatenate([c64, jnp.ones((n_ctx, HEAD_DIM), F32)], axis=0)
    s64 = jnp.concatenate([s64, jnp.zeros((n_ctx, HEAD_DIM), F32)], axis=0)
    return jnp.tile(c64, (1, 2)), jnp.tile(s64, (1, 2))


def _swap_halves(v):
    lane = lax.broadcasted_iota(jnp.int32, v.shape, 1)
    return jnp.where(lane % HEAD_DIM < HEAD_DIM // 2, pltpu.roll(v, LANES - HEAD_DIM // 2, 1),
                     pltpu.roll(v, HEAD_DIM // 2, 1))


def _low_half(shape):
    return lax.broadcasted_iota(jnp.int32, shape, 1) < HEAD_DIM


def _qkv_proj(h, w_qkv, cos_t, sin_t, *, name):
    t_rows = h.shape[0]
    tm = _pick(t_rows, (768, 512, 256))

    def body(h_ref, w_ref, c_ref, s_ref, q_ref, k_ref, v_ref):
        z = jnp.dot(h_ref[...], w_ref[...], preferred_element_type=F32)
        cv, sv = c_ref[...], s_ref[...]

        def rope(chunk):
            return chunk * cv + _swap_halves(chunk) * sv

        for ch in range(D // LANES):
            roped = rope(z[:, ch * LANES:(ch + 1) * LANES])
            q_ref[:, ch * LANES:(ch + 1) * LANES] = (roped * (HEAD_DIM ** -0.5)).astype(BF16)
        low = _low_half((tm, LANES))
        for pair in range(N_KV // 2):
            for which, ref, roped in ((0, k_ref, True), (1, v_ref, False)):
                off = D + which * KV_W + pair * LANES
                chunk = z[:, off:off + LANES]
                if roped:
                    chunk = rope(chunk)
                other = pltpu.roll(chunk, HEAD_DIM, 1)
                even = jnp.where(low, chunk, other)
                odd = jnp.where(low, other, chunk)
                ref[:, (2 * pair) * LANES:(2 * pair + 1) * LANES] = even.astype(BF16)
                ref[:, (2 * pair + 1) * LANES:(2 * pair + 2) * LANES] = odd.astype(BF16)

    dup_w = N_KV * LANES
    return pl.pallas_call(
        body, name=name, grid=(t_rows // tm,),
        in_specs=[_row_spec(tm, D), _full_spec((D, QKV_W)), _row_spec(tm, LANES), _row_spec(tm, LANES)],
        out_specs=[_row_spec(tm, D), _row_spec(tm, dup_w), _row_spec(tm, dup_w)],
        out_shape=[jax.ShapeDtypeStruct((t_rows, D), BF16), jax.ShapeDtypeStruct((t_rows, dup_w), BF16),
                   jax.ShapeDtypeStruct((t_rows, dup_w), BF16)],
        compiler_params=_params(("parallel",)),
    )(h, w_qkv, cos_t, sin_t)


def _qkv_unprep(dq, dk, dv, cos_t, sin_t, *, name):
    t_rows = dq.shape[0]
    tm = 256

    def body(dq_ref, dk_ref, dv_ref, c_ref, s_ref, o_ref):
        cv, sv = c_ref[...], s_ref[...]

        def unrope(chunk):
            return chunk * cv + _swap_halves(chunk * sv)

        for ch in range(D // LANES):
            o_ref[:, ch * LANES:(ch + 1) * LANES] = unrope(dq_ref[:, ch * LANES:(ch + 1) * LANES]).astype(BF16)
        for pair in range(N_KV // 2):
            for which, ref, roped in ((0, dk_ref, True), (1, dv_ref, False)):
                chunk = ref[:, pair * LANES:(pair + 1) * LANES]
                if roped:
                    chunk = unrope(chunk)
                off = D + which * KV_W + pair * LANES
                o_ref[:, off:off + LANES] = chunk.astype(BF16)

    return pl.pallas_call(
        body, name=name, grid=(t_rows // tm,),
        in_specs=[_row_spec(tm, D), _row_spec(tm, KV_W), _row_spec(tm, KV_W), _row_spec(tm, LANES),
                  _row_spec(tm, LANES)],
        out_specs=_row_spec(tm, QKV_W), out_shape=jax.ShapeDtypeStruct((t_rows, QKV_W), BF16),
        compiler_params=_params(("parallel",)),
    )(dq, dk, dv, cos_t, sin_t)


def _attn_specs(n_lat, n_ctx):
    nb = n_lat // BLK
    dup_w = N_KV * LANES

    def ws(j):
        return jnp.clip(j - 1, 0, nb - 3)

    win = [pl.BlockSpec((BLK, dup_w), functools.partial(lambda j, o: (ws(j) + o, 0), o=o)) for o in range(3)]
    ctx = pl.BlockSpec((n_ctx, dup_w), lambda j: (n_lat // n_ctx, 0))
    return nb, ws, win, ctx


def _attn_bias(j, ws_j, nb, n_ctx):
    n_keys = 3 * BLK + n_ctx
    row = lax.broadcasted_iota(jnp.int32, (BLK, n_keys), 0)
    col = lax.broadcasted_iota(jnp.int32, (BLK, n_keys), 1)
    rel = (ws_j - j) * BLK + col - row
    valid = (col >= 3 * BLK) | ((jnp.abs(rel) <= WINDOW) & (j < nb))
    bias = jnp.where(valid, 0.0, NEG)
    return jnp.concatenate([bias] * GRP, axis=0)


def _attn_probs(q_ref, kk, kh, bias, sink_ref):
    low = _low_half((BLK, LANES))
    qs = []
    for g in range(GRP):
        h = GRP * kh + g
        chunk = q_ref[:, (h // 2) * LANES:(h // 2 + 1) * LANES]
        qs.append(jnp.where(low if h % 2 == 0 else ~low, chunk, jnp.zeros_like(chunk)))
    qs = jnp.concatenate(qs, axis=0)
    s = lax.dot_general(qs, kk, (((1,), (1,)), ((), ())), preferred_element_type=F32) + bias
    snk = jnp.concatenate(
        [jnp.broadcast_to(jnp.max(sink_ref[GRP * kh + g:GRP * kh + g + 1, :], axis=1, keepdims=True), (BLK, 1))
         for g in range(GRP)], axis=0)
    m = jnp.maximum(jnp.max(s, axis=-1, keepdims=True), snk)
    p = jnp.exp(s - m)
    p_snk = jnp.exp(snk - m)
    inv = 1.0 / (jnp.sum(p, axis=-1, keepdims=True) + p_snk)
    return qs, p, p_snk, inv


class _Hosted:
    def __init__(self, arrays, out_shapes, scratch, start, finish):
        self.arrays, self.out_shapes, self.scratch, self.start, self.finish = arrays, out_shapes, scratch, start, finish


_NO_EXCHANGE = _Hosted([], [], [], None, None)


def _split_refs(refs, n_in, n_out, n_scratch, hosted):
    hi, ho, hs = len(hosted.arrays), len(hosted.out_shapes), len(hosted.scratch)
    a = n_in + hi
    b = a + n_out + ho
    ins, h_ins = refs[:n_in], refs[n_in:a]
    outs, h_outs = refs[a:a + n_out], refs[a + n_out:b]
    scr, h_scr = refs[b:b + n_scratch], refs[b + n_scratch:b + n_scratch + hs]
    return ins, outs, scr, (h_ins, h_outs, h_scr)


def _run_hosted(hosted, h_refs, step, n_steps):
    if hosted.start is None:
        return

    flat = [r for group in h_refs for r in group]

    @pl.when(step == 0)
    def _():
        hosted.start(*flat)

    @pl.when(step == n_steps - 1)
    def _():
        hosted.finish(*flat)


def _attention_fwd(q, kd, vd, sink_b, n_lat, n_ctx, *, name, hosted=_NO_EXCHANGE):
    t_rows = q.shape[0]
    nb, ws, win, ctx = _attn_specs(n_lat, n_ctx)
    n_steps = t_rows // BLK

    def body(*refs):
        ins, outs, _, h_refs = _split_refs(refs, 10, 1, 0, hosted)
        q_ref, k0, k1, k2, kc, v0, v1, v2, vc, sink_ref = ins
        o_ref, = outs
        j = pl.program_id(0)
        _run_hosted(hosted, h_refs, j, n_steps)
        ws_j = ws(j)
        low = _low_half((BLK, LANES))
        bias = _attn_bias(j, ws_j, nb, n_ctx)
        for kh in range(N_KV):
            sl = slice(kh * LANES, (kh + 1) * LANES)
            kk = jnp.concatenate([k0[:, sl], k1[:, sl], k2[:, sl], kc[:, sl]], axis=0)
            vv = jnp.concatenate([v0[:, sl], v1[:, sl], v2[:, sl], vc[:, sl]], axis=0)
            _, p, _, inv = _attn_probs(q_ref, kk, kh, bias, sink_ref)
            o = jnp.dot(p.astype(BF16), vv, preferred_element_type=F32) * inv
            for half in range(2):
                even = o[(2 * half) * BLK:(2 * half + 1) * BLK]
                odd = o[(2 * half + 1) * BLK:(2 * half + 2) * BLK]
                ch = 2 * kh + half
                o_ref[:, ch * LANES:(ch + 1) * LANES] = jnp.where(low, even, odd).astype(BF16)

    outs = pl.pallas_call(
        body, name=name, grid=(n_steps,),
        in_specs=[_row_spec(BLK, D)] + win + [ctx] + win + [ctx] + [_full_spec((N_HEADS, LANES))]
        + [ANY] * len(hosted.arrays),
        out_specs=[_row_spec(BLK, D)] + [ANY] * len(hosted.out_shapes),
        out_shape=[jax.ShapeDtypeStruct((t_rows, D), BF16)] + list(hosted.out_shapes),
        scratch_shapes=list(hosted.scratch),
        compiler_params=_params(("arbitrary",)),
    )(q, kd, kd, kd, kd, vd, vd, vd, vd, sink_b, *hosted.arrays)
    return outs[0], outs[1:]


def _attention_bwd(q, kd, vd, sink_b, dy, n_lat, n_ctx, *, name, hosted=_NO_EXCHANGE):
    t_rows = q.shape[0]
    nb, ws, win, ctx = _attn_specs(n_lat, n_ctx)
    n_steps = t_rows // BLK

    def body(*refs):
        ins, outs, scr, h_refs = _split_refs(refs, 11, 4, 3, hosted)
        q_ref, k0, k1, k2, kc, v0, v1, v2, vc, sink_ref, dy_ref = ins
        dq_ref, dk_hbm, dv_hbm, ds_ref = outs
        dk_acc, dv_acc, sem = scr
        j = pl.program_id(0)
        _run_hosted(hosted, h_refs, j, n_steps)
        ws_j = ws(j)

        @pl.when(j == 0)
        def _():
            dk_acc[...] = jnp.zeros_like(dk_acc)
            dv_acc[...] = jnp.zeros_like(dv_acc)
            ds_ref[...] = jnp.zeros_like(ds_ref)

        low = _low_half((BLK, LANES))
        low_keys = _low_half((3 * BLK + n_ctx, LANES))
        win_start = pl.multiple_of(ws_j * BLK, BLK)
        scale = HEAD_DIM ** -0.5
        dk_heads, dv_heads = [], []
        bias = _attn_bias(j, ws_j, nb, n_ctx)
        for kh in range(N_KV):
            sl = slice(kh * LANES, (kh + 1) * LANES)
            kk = jnp.concatenate([k0[:, sl], k1[:, sl], k2[:, sl], kc[:, sl]], axis=0)
            vv = jnp.concatenate([v0[:, sl], v1[:, sl], v2[:, sl], vc[:, sl]], axis=0)
            qs, p, p_snk, inv = _attn_probs(q_ref, kk, kh, bias, sink_ref)
            dos = []
            for g in range(GRP):
                h = GRP * kh + g
                chunk = dy_ref[:, (h // 2) * LANES:(h // 2 + 1) * LANES]
                dos.append(jnp.where(low if h % 2 == 0 else ~low, chunk, jnp.zeros_like(chunk)).astype(BF16))
            dos = jnp.concatenate(dos, axis=0)
            dp = lax.dot_general(dos, vv, (((1,), (1,)), ((), ())), preferred_element_type=F32)
            dsum = jnp.sum(p * dp, axis=-1, keepdims=True) * inv
            ds = (p * ((dp - dsum) * inv)).astype(BF16)
            snk_term = p_snk * inv * dsum
            for g in range(GRP):
                contrib = -jnp.sum(snk_term[g * BLK:(g + 1) * BLK], axis=0, keepdims=True)
                ds_ref[GRP * kh + g:GRP * kh + g + 1, :] += jnp.broadcast_to(contrib, (1, LANES))
            dqs = jnp.dot(ds, kk, preferred_element_type=F32) * scale
            for half in range(2):
                even = dqs[(2 * half) * BLK:(2 * half + 1) * BLK]
                odd = dqs[(2 * half + 1) * BLK:(2 * half + 2) * BLK]
                ch = 2 * kh + half
                dq_ref[:, ch * LANES:(ch + 1) * LANES] = jnp.where(low, even, odd)
            dkk = lax.dot_general(ds, qs, (((0,), (0,)), ((), ())), preferred_element_type=F32)
            dvv = lax.dot_general((p * inv).astype(BF16), dos, (((0,), (0,)), ((), ())), preferred_element_type=F32)
            dk_heads.append(dkk + pltpu.roll(dkk, HEAD_DIM, 1))
            dv_heads.append(dvv + pltpu.roll(dvv, HEAD_DIM, 1))
        for pair in range(N_KV // 2):
            sl = slice(pair * LANES, (pair + 1) * LANES)
            for acc, heads in ((dk_acc, dk_heads), (dv_acc, dv_heads)):
                both = jnp.where(low_keys, heads[2 * pair], heads[2 * pair + 1])
                acc[pl.ds(win_start, 3 * BLK), sl] += both[:3 * BLK]
                acc[n_lat:n_lat + n_ctx, sl] += both[3 * BLK:]

        @pl.when(j == n_steps - 1)
        def _():
            ck = pltpu.make_async_copy(dk_acc, dk_hbm, sem.at[0])
            cv = pltpu.make_async_copy(dv_acc, dv_hbm, sem.at[1])
            ck.start()
            cv.start()
            ck.wait()
            cv.wait()

    outs = pl.pallas_call(
        body, name=name, grid=(n_steps,),
        in_specs=[_row_spec(BLK, D)] + win + [ctx] + win + [ctx] + [_full_spec((N_HEADS, LANES)), _row_spec(BLK, D)]
        + [ANY] * len(hosted.arrays),
        out_specs=[_row_spec(BLK, D), ANY, ANY, _full_spec((N_HEADS, LANES))] + [ANY] * len(hosted.out_shapes),
        out_shape=[jax.ShapeDtypeStruct((t_rows, D), F32), jax.ShapeDtypeStruct((t_rows, KV_W), F32),
                   jax.ShapeDtypeStruct((t_rows, KV_W), F32), jax.ShapeDtypeStruct((N_HEADS, LANES), F32)]
        + list(hosted.out_shapes),
        scratch_shapes=[pltpu.VMEM((t_rows, KV_W), F32), pltpu.VMEM((t_rows, KV_W), F32),
                        pltpu.SemaphoreType.DMA((2,))] + list(hosted.scratch),
        compiler_params=_params(("arbitrary",)),
    )(q, kd, kd, kd, kd, vd, vd, vd, vd, sink_b, dy, *hosted.arrays)
    return outs[0], outs[1], outs[2], outs[3], outs[4:]


_GELU_K = math.sqrt(2.0 / math.pi)


def _gelu(v):
    return jax.nn.gelu(v)


def _gelu_and_grad(v):
    t = jnp.tanh(_GELU_K * (v + 0.044715 * (v * v * v)))
    cdf = 0.5 * (1.0 + t)
    return v * cdf, cdf + 0.5 * v * (1.0 - t * t) * _GELU_K * (1.0 + 3.0 * 0.044715 * v * v)


def _gating_fwd(z_a, w_s, b_sb, g_v, *, name, hosted=None):
    hosted = hosted or _NO_EXCHANGE
    t_rows = z_a.shape[0]
    n_steps = t_rows // BLK

    def body(*refs):
        ins, outs, _, h_refs = _split_refs(refs, 4, 1, 0, hosted)
        z_ref, w_ref, b_ref, g_ref = ins
        o_ref, = outs
        _run_hosted(hosted, h_refs, pl.program_id(0), n_steps)
        u = _gelu(z_ref[:, :D])
        v = _gelu(z_ref[:, D:])
        vn = v * lax.rsqrt(jnp.mean(v * v, axis=-1, keepdims=True) + EPS) * g_ref[...]
        for g in range(A_GROUPS):
            sl = slice(g * LANES, (g + 1) * LANES)
            mixed = jnp.dot(w_ref[g].astype(BF16), vn[:, sl].astype(BF16), preferred_element_type=F32) + b_ref[g]
            o_ref[:, sl] = (u[:, sl] * mixed).astype(BF16)

    outs = pl.pallas_call(
        body, name=name, grid=(n_steps,),
        in_specs=[_row_spec(BLK, A_COLS), _full_spec((A_GROUPS, BLK, BLK)), _full_spec((A_GROUPS, BLK, LANES)),
                  _full_spec((1, D))] + [ANY] * len(hosted.arrays),
        out_specs=[_row_spec(BLK, D)] + [ANY] * len(hosted.out_shapes),
        out_shape=[jax.ShapeDtypeStruct((t_rows, D), BF16)] + list(hosted.out_shapes),
        scratch_shapes=list(hosted.scratch),
        compiler_params=_params(("arbitrary",)),
    )(z_a, w_s, b_sb, g_v, *hosted.arrays)
    return outs[0], outs[1:]


def _gating_bwd(z_a, dy, w_s, b_sb, g_v, *, name, hosted=None):
    hosted = hosted or _NO_EXCHANGE
    t_rows = z_a.shape[0]
    n_steps = t_rows // BLK

    def body(*refs):
        ins, outs, _, h_refs = _split_refs(refs, 5, 4, 0, hosted)
        z_ref, dy_ref, w_ref, b_ref, g_ref = ins
        dz_ref, dw_ref, db_ref, st_ref = outs
        i = pl.program_id(0)
        _run_hosted(hosted, h_refs, i, n_steps)

        @pl.when(i == 0)
        def _():
            dw_ref[...] = jnp.zeros_like(dw_ref)
            db_ref[...] = jnp.zeros_like(db_ref)
            st_ref[...] = jnp.zeros_like(st_ref)

        u, du_dz = _gelu_and_grad(z_ref[:, :D])
        v, dv_dz = _gelu_and_grad(z_ref[:, D:])
        gv = g_ref[...]
        rstd = lax.rsqrt(jnp.mean(v * v, axis=-1, keepdims=True) + EPS)
        vh = v * rstd
        vn = vh * gv
        dyv = dy_ref[...]
        dvn = []
        for g in range(A_GROUPS):
            sl = slice(g * LANES, (g + 1) * LANES)
            wg = w_ref[g].astype(BF16)
            vg = vn[:, sl].astype(BF16)
            mixed = jnp.dot(wg, vg, preferred_element_type=F32) + b_ref[g]
            dz_ref[:, sl] = (dyv[:, sl] * mixed * du_dz[:, sl]).astype(BF16)
            dmixed = dyv[:, sl] * u[:, sl]
            dmb = dmixed.astype(BF16)
            dvn.append(lax.dot_general(wg, dmb, (((0,), (0,)), ((), ())), preferred_element_type=F32))
            dw_ref[g] += lax.dot_general(dmb, vg, (((1,), (1,)), ((), ())), preferred_element_type=F32)
            db_ref[g] += jnp.broadcast_to(jnp.sum(dmixed, axis=-1, keepdims=True), (BLK, LANES))
        dvn = jnp.concatenate(dvn, axis=1)
        st_ref[...] += _rows8([(0, _colsum(dvn * vh))], D)
        e = dvn * gv
        dv = rstd * (e - vh * jnp.mean(e * vh, axis=-1, keepdims=True))
        dz_ref[:, D:] = (dv * dv_dz).astype(BF16)

    outs = pl.pallas_call(
        body, name=name, grid=(n_steps,),
        in_specs=[_row_spec(BLK, A_COLS), _row_spec(BLK, D), _full_spec((A_GROUPS, BLK, BLK)),
                  _full_spec((A_GROUPS, BLK, LANES)), _full_spec((1, D))] + [ANY] * len(hosted.arrays),
        out_specs=[_row_spec(BLK, A_COLS), _full_spec((A_GROUPS, BLK, BLK)), _full_spec((A_GROUPS, BLK, LANES)),
                   _full_spec((SUBLANES, D))] + [ANY] * len(hosted.out_shapes),
        out_shape=[jax.ShapeDtypeStruct((t_rows, A_COLS), BF16), jax.ShapeDtypeStruct((A_GROUPS, BLK, BLK), F32),
                   jax.ShapeDtypeStruct((A_GROUPS, BLK, LANES), F32), jax.ShapeDtypeStruct((SUBLANES, D), F32)]
        + list(hosted.out_shapes),
        scratch_shapes=list(hosted.scratch),
        compiler_params=_params(("arbitrary",)),
    )(z_a, dy, w_s, b_sb, g_v, *hosted.arrays)
    return outs[0], outs[1], outs[2], outs[3], outs[4:]


def _sconv_fwd(z_b, w8, n_lat, *, name):
    t_rows = z_b.shape[0]
    tm = 256
    prev, nxt = _halo_specs(tm, B_COLS, t_rows)

    def body(z_ref, zp_ref, zn_ref, w_ref, o_ref):
        p = z_ref[:, D:2 * D] * z_ref[:, 2 * D:]
        pp = zp_ref[:, D:2 * D] * zp_ref[:, 2 * D:]
        pn = zn_ref[:, D:2 * D] * zn_ref[:, 2 * D:]
        up, dn = _shift_rows(p, pp, pn, n_lat, t_rows, tm)
        conv = w_ref[0:1, :] * up + w_ref[1:2, :] * p + w_ref[2:3, :] * dn
        o_ref[...] = (z_ref[:, :D] * conv).astype(BF16)

    return pl.pallas_call(
        body, name=name, grid=(t_rows // tm,),
        in_specs=[_row_spec(tm, B_COLS), prev, nxt, _full_spec((SUBLANES, D))],
        out_specs=_row_spec(tm, D), out_shape=jax.ShapeDtypeStruct((t_rows, D), BF16),
        compiler_params=_params(("parallel",)),
    )(z_b, z_b, z_b, w8)


def _sconv_bwd(z_b, dy, w8, n_lat, *, name):
    t_rows = z_b.shape[0]
    tm = 256
    prev, nxt = _halo_specs(tm, B_COLS, t_rows)
    dprev, dnxt = _halo_specs(tm, D, t_rows)

    def body(z_ref, zp_ref, zn_ref, dy_ref, dyp_ref, dyn_ref, w_ref, dz_ref, st_ref):
        i = pl.program_id(0)
        bg, cg, hb = z_ref[:, :D], z_ref[:, D:2 * D], z_ref[:, 2 * D:]
        p = cg * hb
        pp = zp_ref[:, D:2 * D] * zp_ref[:, 2 * D:]
        pn = zn_ref[:, D:2 * D] * zn_ref[:, 2 * D:]
        up, dn = _shift_rows(p, pp, pn, n_lat, t_rows, tm)
        w0, w1, w2 = w_ref[0:1, :], w_ref[1:2, :], w_ref[2:3, :]
        conv = w0 * up + w1 * p + w2 * dn
        dyv = dy_ref[...]
        dz_ref[:, :D] = (dyv * conv).astype(BF16)
        dcv = dyv * bg
        dcv_up, dcv_dn = _shift_rows(dcv, dyp_ref[...] * zp_ref[:, :D], dyn_ref[...] * zn_ref[:, :D], n_lat, t_rows, tm)
        dp = w0 * dcv_dn + w1 * dcv + w2 * dcv_up
        dz_ref[:, D:2 * D] = (dp * hb).astype(BF16)
        dz_ref[:, 2 * D:] = (dp * cg).astype(BF16)
        upd = _rows8([(0, _colsum(dcv * up)), (1, _colsum(dcv * p)), (2, _colsum(dcv * dn))], D)

        @pl.when(i == 0)
        def _():
            st_ref[...] = upd

        @pl.when(i > 0)
        def _():
            st_ref[...] += upd

    return pl.pallas_call(
        body, name=name, grid=(t_rows // tm,),
        in_specs=[_row_spec(tm, B_COLS), prev, nxt, _row_spec(tm, D), dprev, dnxt, _full_spec((SUBLANES, D))],
        out_specs=[_row_spec(tm, B_COLS), _full_spec((SUBLANES, D))],
        out_shape=[jax.ShapeDtypeStruct((t_rows, B_COLS), BF16), jax.ShapeDtypeStruct((SUBLANES, D), F32)],
        compiler_params=_params(("arbitrary",)),
    )(z_b, z_b, z_b, dy, dy, dy, w8)


def _branch_merge_fwd(ys, w_branch, z_g, b_gate, *, name):
    t_rows = z_g.shape[0]
    tm = _pick(t_rows, (768, 512, 256))

    def body(y0_ref, y1_ref, y2_ref, w_ref, z_ref, b_ref, t_ref, o_ref, acc_ref):
        k = pl.program_id(1)
        for which, y_ref in enumerate((y0_ref, y1_ref, y2_ref)):
            @pl.when(k == which)
            def _():
                t_ref[...] = jnp.dot(y_ref[...], w_ref[...], preferred_element_type=F32)

        term = jax.nn.sigmoid(z_ref[...] + b_ref[...]) * t_ref[...]

        @pl.when(k == 0)
        def _():
            acc_ref[...] = term

        @pl.when(k > 0)
        def _():
            acc_ref[...] += term

        @pl.when(k == 2)
        def _():
            o_ref[...] = acc_ref[...].astype(BF16)

    y_spec = pl.BlockSpec((tm, D), lambda i, k: (i, 0))
    return pl.pallas_call(
        body, name=name, grid=(t_rows // tm, 3),
        in_specs=[y_spec, y_spec, y_spec, pl.BlockSpec((None, D, D), lambda i, k: (k, 0, 0)),
                  pl.BlockSpec((tm, D), lambda i, k: (i, k)), pl.BlockSpec((None, 1, D), lambda i, k: (k, 0, 0))],
        out_specs=[pl.BlockSpec((None, tm, D), lambda i, k: (k, i, 0)), y_spec],
        out_shape=[jax.ShapeDtypeStruct((3, t_rows, D), F32), jax.ShapeDtypeStruct((t_rows, D), BF16)],
        scratch_shapes=[pltpu.VMEM((tm, D), F32)],
        compiler_params=_params(("parallel", "arbitrary")),
    )(*ys, w_branch, z_g, b_gate.reshape(3, 1, D))


def _merge_bwd(dmerged, t_all, z_g, b_gate, *, name):
    t_rows = dmerged.shape[0]
    tm = 256
    t_specs = [pl.BlockSpec((None, tm, D), functools.partial(lambda i, k: (k, i, 0), k=k)) for k in range(3)]

    def body(dm_ref, t0_ref, t1_ref, t2_ref, z_ref, b_ref, d0_ref, d1_ref, d2_ref, dz_ref, st_ref):
        i = pl.program_id(0)
        dm = dm_ref[...]
        sums = []
        for k, (t_ref, d_ref) in enumerate(((t0_ref, d0_ref), (t1_ref, d1_ref), (t2_ref, d2_ref))):
            gate = jax.nn.sigmoid(z_ref[:, k * D:(k + 1) * D] + b_ref[:, k * D:(k + 1) * D])
            d_ref[...] = (dm * gate).astype(BF16)
            dzg = dm * t_ref[...] * gate * (1.0 - gate)
            dz_ref[:, k * D:(k + 1) * D] = dzg.astype(BF16)
            sums.append(_colsum(dzg))
        upd = _rows8([(0, jnp.concatenate(sums, axis=1))], G_COLS)

        @pl.when(i == 0)
        def _():
            st_ref[...] = upd

        @pl.when(i > 0)
        def _():
            st_ref[...] += upd

    return pl.pallas_call(
        body, name=name, grid=(t_rows // tm,),
        in_specs=[_row_spec(tm, D)] + t_specs + [_row_spec(tm, G_COLS), _full_spec((1, G_COLS))],
        out_specs=[_row_spec(tm, D)] * 3 + [_row_spec(tm, G_COLS), _full_spec((SUBLANES, G_COLS))],
        out_shape=[jax.ShapeDtypeStruct((t_rows, D), BF16)] * 3
        + [jax.ShapeDtypeStruct((t_rows, G_COLS), BF16), jax.ShapeDtypeStruct((SUBLANES, G_COLS), F32)],
        compiler_params=_params(("arbitrary",)),
    )(dmerged, t_all, t_all, t_all, z_g, b_gate)


def _ffn_mid_fwd(up, w8, n_lat, *, name, hosted=None):
    hosted = hosted or _NO_EXCHANGE
    t_rows = up.shape[0]
    tm = 256
    n_steps = t_rows // tm
    prev, nxt = _halo_specs(tm, D_FF, t_rows)

    def body(*refs):
        ins, outs, _, h_refs = _split_refs(refs, 5, 2, 0, hosted)
        a_ref, ap_ref, an_ref, g_ref, w_ref = ins
        cv_ref, f_ref = outs
        _run_hosted(hosted, h_refs, pl.program_id(0), n_steps)
        a = a_ref[...]
        au, ad = _shift_rows(a, ap_ref[...], an_ref[...], n_lat, t_rows, tm)
        cv = w_ref[0:1, :] * au + w_ref[1:2, :] * a + w_ref[2:3, :] * ad
        cv_ref[...] = cv
        f_ref[...] = (_silu(cv) * g_ref[...]).astype(BF16)

    outs = pl.pallas_call(
        body, name=name, grid=(n_steps,),
        in_specs=[_row_spec(tm, D_FF), prev, nxt, _row_spec(tm, D_FF, 1), _full_spec((SUBLANES, D_FF))]
        + [ANY] * len(hosted.arrays),
        out_specs=[_row_spec(tm, D_FF), _row_spec(tm, D_FF)] + [ANY] * len(hosted.out_shapes),
        out_shape=[jax.ShapeDtypeStruct((t_rows, D_FF), F32), jax.ShapeDtypeStruct((t_rows, D_FF), BF16)]
        + list(hosted.out_shapes),
        scratch_shapes=list(hosted.scratch),
        compiler_params=_params(("arbitrary",)),
    )(up, up, up, up, w8, *hosted.arrays)
    return outs[0], outs[1], outs[2:]


def _ffn_mid_bwd(up, cv, df, w8, n_lat, *, name):
    t_rows = up.shape[0]
    tm = 256
    prev, nxt = _halo_specs(tm, D_FF, t_rows)
    gprev, gnxt = _halo_specs(tm, D_FF, t_rows, 1)

    def body(a_ref, ap_ref, an_ref, g_ref, gp_ref, gn_ref, cv_ref, cp_ref, cn_ref, df_ref, dfp_ref, dfn_ref,
             w_ref, o_ref, st_ref):
        i = pl.program_id(0)
        a = a_ref[...]
        au, ad = _shift_rows(a, ap_ref[...], an_ref[...], n_lat, t_rows, tm)
        cvv = cv_ref[...]
        dfv = df_ref[...]
        sig = jax.nn.sigmoid(cvv)
        o_ref[:, D_FF:] = (dfv * (cvv * sig)).astype(BF16)
        dcv = dfv * g_ref[...] * (sig * (1.0 + cvv * (1.0 - sig)))
        dcv_p = dfp_ref[...] * gp_ref[...] * _dsilu(cp_ref[...])
        dcv_n = dfn_ref[...] * gn_ref[...] * _dsilu(cn_ref[...])
        du, dd = _shift_rows(dcv, dcv_p, dcv_n, n_lat, t_rows, tm)
        o_ref[:, :D_FF] = (w_ref[0:1, :] * dd + w_ref[1:2, :] * dcv + w_ref[2:3, :] * du).astype(BF16)
        upd = _rows8([(0, _colsum(dcv * au)), (1, _colsum(dcv * a)), (2, _colsum(dcv * ad))], D_FF)

        @pl.when(i == 0)
        def _():
            st_ref[...] = upd

        @pl.when(i > 0)
        def _():
            st_ref[...] += upd

    row = _row_spec(tm, D_FF)
    return pl.pallas_call(
        body, name=name, grid=(t_rows // tm,),
        in_specs=[row, prev, nxt, _row_spec(tm, D_FF, 1), gprev, gnxt, row, prev, nxt, row, prev, nxt,
                  _full_spec((SUBLANES, D_FF))],
        out_specs=[_row_spec(tm, 2 * D_FF), _full_spec((SUBLANES, D_FF))],
        out_shape=[jax.ShapeDtypeStruct((t_rows, 2 * D_FF), BF16), jax.ShapeDtypeStruct((SUBLANES, D_FF), F32)],
        compiler_params=_params(("arbitrary",), VMEM_LIMIT_WIDE),
    )(up, up, up, up, up, up, cv, cv, cv, df, df, df, w8)


def _loss_head(x, target, g_final, n_lat, *, name):
    t_rows = x.shape[0]
    tm = 256
    last = n_lat // tm - 1

    def body(x_ref, t_ref, g_ref, dx_ref, st_ref):
        i = pl.program_id(0)
        is_ctx = i * tm >= n_lat
        xv = x_ref[...]
        gv = g_ref[...]
        rstd = lax.rsqrt(jnp.mean(xv * xv, axis=-1, keepdims=True) + EPS)
        rn = xv * rstd
        err = rn * gv - t_ref[...]
        dy = err / D
        e = dy * gv
        dx = rstd * (e - rn * jnp.mean(e * rn, axis=-1, keepdims=True))
        dx_ref[...] = jnp.where(is_ctx, 0.0, dx)
        keep = jnp.where(is_ctx, 0.0, 1.0)
        upd = _rows8([(0, keep * _colsum(dy * rn)), (1, keep * _colsum(err * err))], D)

        @pl.when(i == 0)
        def _():
            st_ref[...] = upd

        @pl.when(i > 0)
        def _():
            st_ref[...] += upd

    return pl.pallas_call(
        body, name=name, grid=(t_rows // tm,),
        in_specs=[_row_spec(tm, D), pl.BlockSpec((tm, D), lambda i: (jnp.minimum(i, last), 0)), _full_spec((1, D))],
        out_specs=[_row_spec(tm, D), _full_spec((SUBLANES, D))],
        out_shape=[jax.ShapeDtypeStruct((t_rows, D), F32), jax.ShapeDtypeStruct((SUBLANES, D), F32)],
        compiler_params=_params(("arbitrary",)),
    )(x, target, g_final)


def _sum_slabs(x, out_dtype, *, name):
    n_slabs, rows, cols = x.shape
    tm = _pick(rows, (432, 256, 192, 128, 64, 32, 24, 16, 8))

    def body(x_ref, o_ref):
        acc = x_ref[0].astype(F32)
        for s in range(1, n_slabs):
            acc = acc + x_ref[s].astype(F32)
        o_ref[...] = acc.astype(o_ref.dtype)

    return pl.pallas_call(
        body, name=name, grid=(rows // tm,),
        in_specs=[pl.BlockSpec((n_slabs, tm, cols), lambda i: (0, i, 0))],
        out_specs=pl.BlockSpec((tm, cols), lambda i: (i, 0)),
        out_shape=jax.ShapeDtypeStruct((rows, cols), out_dtype),
        compiler_params=_params(("parallel",)),
    )(x)


def _add_half(half_idx, a, b, *, name):
    n_slabs, rows, cols = b.shape
    tm = _pick(rows, (432, 256, 192, 128, 96, 64, 32, 16))
    per_half = rows // tm

    def body(half_ref, a_ref, b_ref, o_ref):
        o_ref[...] = (a_ref[...].astype(F32) + b_ref[...].astype(F32)).astype(BF16)

    spec = pl.BlockSpec((1, tm, cols), lambda s, i, half_ref: (s, i, 0))
    a_spec = pl.BlockSpec((1, tm, cols), lambda s, i, half_ref: (s, half_ref[0] * per_half + i, 0))
    return pl.pallas_call(
        body, name=name,
        grid_spec=pltpu.PrefetchScalarGridSpec(num_scalar_prefetch=1, grid=(n_slabs, per_half),
                                               in_specs=[a_spec, spec], out_specs=spec),
        out_shape=jax.ShapeDtypeStruct(b.shape, BF16), compiler_params=_params(("parallel", "parallel")),
    )(half_idx, a, b)


def _adamw(w, g, m, v, *, name, hosted=None):
    hosted = hosted or _NO_EXCHANGE
    rows, cols = w.shape
    tm = _pick(rows, (256, 128, 64, 32, 16, 8))
    n_steps = rows // tm

    def body(*refs):
        ins, outs, _, h_refs = _split_refs(refs, 4, 3, 0, hosted)
        w_ref, g_ref, m_ref, v_ref = ins
        d_ref, nm_ref, nv_ref = outs
        _run_hosted(hosted, h_refs, pl.program_id(0), n_steps)
        gv = g_ref[...]
        nm = ADAM_B1 * m_ref[...] + (1.0 - ADAM_B1) * gv
        nv = ADAM_B2 * v_ref[...] + (1.0 - ADAM_B2) * jnp.square(gv)
        m_hat = nm / (1.0 - ADAM_B1 ** ADAM_STEP)
        v_hat = nv / (1.0 - ADAM_B2 ** ADAM_STEP)
        d_ref[...] = -ADAM_LR * (m_hat / (jnp.sqrt(v_hat) + ADAM_EPS) + ADAM_WD * w_ref[...])
        nm_ref[...] = nm
        nv_ref[...] = nv

    spec = pl.BlockSpec((tm, cols), lambda i: (i, 0))
    shape = jax.ShapeDtypeStruct((rows, cols), F32)
    outs = pl.pallas_call(
        body, name=name, grid=(n_steps,), in_specs=[spec] * 4 + [ANY] * len(hosted.arrays),
        out_specs=[spec] * 3 + [ANY] * len(hosted.out_shapes), out_shape=[shape] * 3 + list(hosted.out_shapes),
        scratch_shapes=list(hosted.scratch), compiler_params=_params(("arbitrary",)),
    )(w, g, m, v, *hosted.arrays)
    return outs[0], outs[1], outs[2], outs[3:]


def _place():
    x, y, c = lax.axis_index("x"), lax.axis_index("y"), lax.axis_index("c")
    chips = [(1 - x, y), (x, 1 - y), (1 - x, 1 - y)]
    return x, y, c, chips


def _remote(src, dst, send_sems, recv_sems, k, to):
    return pltpu.make_async_remote_copy(src_ref=src, dst_ref=dst, send_sem=send_sems.at[k], recv_sem=recv_sems.at[k],
                                        device_id=to, device_id_type=MESH)


HALF_CHUNKS = 2


def _chunks(ref, n):
    step = ref.shape[0] // n
    tile_rows = SUBLANES if ref.dtype == F32 else 2 * SUBLANES
    assert step * n == ref.shape[0] and step % tile_rows == 0, (ref.shape, n)
    return [ref.at[pl.ds(k * step, step)] for k in range(n)]


def _half(ref, which):
    half = ref.shape[0] // 2
    return ref.at[pl.ds(pl.multiple_of(which * half, 2 * SUBLANES), half)]


def _staged_copy(src, dst, buf, sems):
    step = buf.shape[1]
    n = src.shape[0] // step
    assert n * step == src.shape[0], (src.shape, step)
    ins = [pltpu.make_async_copy(src.at[pl.ds(k * step, step)], buf.at[k % 2], sems.at[k % 2]) for k in range(n)]
    outs = [pltpu.make_async_copy(buf.at[k % 2], dst.at[pl.ds(k * step, step)], sems.at[2 + k % 2]) for k in range(n)]
    ins[0].start()
    for k in range(n):
        ins[k].wait()
        outs[k].start()
        if k + 1 < n:
            if k >= 1:
                outs[k - 1].wait()
            ins[k + 1].start()
    if n >= 2:
        outs[n - 2].wait()
    outs[n - 1].wait()


def _stage_rows(rows):
    return _pick(rows, (256, 432))


def _stage_scratch(slabs):
    return [pltpu.VMEM((2, _stage_rows(s.shape[-2]), s.shape[-1]), s.dtype) for s in slabs] + [pltpu.SemaphoreType.DMA((4,))]


N_LINK_SEMS = (N_CHIPS - 1) * HALF_CHUNKS


def _link_sems(n_groups):
    return [pltpu.SemaphoreType.DMA((n_groups * N_LINK_SEMS,)), pltpu.SemaphoreType.DMA((n_groups * N_LINK_SEMS,))]


def _sem_index(g, j, k):
    return g * N_LINK_SEMS + j * HALF_CHUNKS + k


def _gather_ici_start(*refs):
    n = (len(refs) - 2) // 2
    p_refs, o_refs, (send_sems, recv_sems) = refs[:n], refs[n:2 * n], refs[2 * n:]
    x, y, c, chips = _place()
    for g, (p_ref, o_ref) in enumerate(zip(p_refs, o_refs)):
        src = _chunks(_half(p_ref, c), HALF_CHUNKS)
        dst = _chunks(_half(o_ref.at[2 * x + y], c), HALF_CHUNKS)
        for j, chip in enumerate(chips):
            for k in range(HALF_CHUNKS):
                _remote(src[k], dst[k], send_sems, recv_sems, _sem_index(g, j, k), (*chip, c)).start()


def _gather_ici_finish(*refs):
    n = (len(refs) - 2) // 2
    p_refs, o_refs, (send_sems, recv_sems) = refs[:n], refs[n:2 * n], refs[2 * n:]
    x, y, c, chips = _place()
    for g, (p_ref, o_ref) in enumerate(zip(p_refs, o_refs)):
        src = _chunks(_half(p_ref, c), HALF_CHUNKS)
        for j, (cx, cy) in enumerate(chips):
            for k, landed in enumerate(_chunks(_half(o_ref.at[2 * cx + cy], c), HALF_CHUNKS)):
                _remote(src[k], landed, send_sems, recv_sems, _sem_index(g, j, k), (x, y, c)).wait_recv()
        for j in range(len(chips)):
            for k in range(HALF_CHUNKS):
                _remote(src[k], src[k], send_sems, recv_sems, _sem_index(g, j, k), (x, y, c)).wait_send()


def _gathered_shapes(slabs):
    return [jax.ShapeDtypeStruct((N_CHIPS,) + s.shape, s.dtype) for s in slabs]


def _gather_ici_hosted(slabs):
    return _Hosted(list(slabs), _gathered_shapes(slabs), _link_sems(len(slabs)), _gather_ici_start, _gather_ici_finish)


def _gather_ici(slabs, *, name):
    def body(*refs):
        _gather_ici_start(*refs)
        _gather_ici_finish(*refs)

    return pl.pallas_call(
        body, name=name, in_specs=[ANY] * len(slabs), out_specs=[ANY] * len(slabs),
        out_shape=_gathered_shapes(slabs), scratch_shapes=_link_sems(len(slabs)),
    )(*slabs)


def _gather_finish(partials, slabs, *, name):
    n = len(slabs)

    def body(*refs):
        p_refs, o_refs = refs[n:2 * n], refs[2 * n:3 * n]
        send_sems, recv_sems = refs[3 * n:3 * n + 2]
        bufs, loc_sems = refs[3 * n + 2:4 * n + 2], refs[4 * n + 2]
        x, y, c, chips = _place()
        sib = (x, y, 1 - c)
        passed = []
        for g, o_ref in enumerate(o_refs):
            for j, (cx, cy) in enumerate(chips):
                for k, landed in enumerate(_chunks(_half(o_ref.at[2 * cx + cy], c), HALF_CHUNKS)):
                    passed.append(_remote(landed, landed, send_sems, recv_sems, _sem_index(g, j, k), sib))
        for cp in passed:
            cp.start()
        for p_ref, o_ref, buf in zip(p_refs, o_refs, bufs):
            _staged_copy(p_ref, o_ref.at[2 * x + y], buf, loc_sems)
        for g, o_ref in enumerate(o_refs):
            for j, (cx, cy) in enumerate(chips):
                for k, landed in enumerate(_chunks(_half(o_ref.at[2 * cx + cy], 1 - c), HALF_CHUNKS)):
                    _remote(landed, landed, send_sems, recv_sems, _sem_index(g, j, k), sib).wait_recv()
        for cp in passed:
            cp.wait_send()

    return pl.pallas_call(
        body, name=name, in_specs=[ANY] * (2 * n), out_specs=[ANY] * n,
        out_shape=[jax.ShapeDtypeStruct(p.shape, p.dtype) for p in partials],
        input_output_aliases={g: g for g in range(n)}, scratch_shapes=_link_sems(n) + _stage_scratch(slabs),
        compiler_params=_params(),
    )(*partials, *slabs)


def _grad_sibling_swap(g_packs, *, name):
    n = len(g_packs)
    per_group = N_CHIPS * HALF_CHUNKS

    def body(*refs):
        g_refs, got_refs, (send_sems, recv_sems) = refs[:n], refs[n:2 * n], refs[2 * n:]
        x, y, c, _ = _place()
        sib = (x, y, 1 - c)
        swaps = [_remote(src, dst, send_sems, recv_sems, g * per_group + s * HALF_CHUNKS + k, sib)
                 for g, (g_ref, got_ref) in enumerate(zip(g_refs, got_refs))
                 for s in range(N_CHIPS)
                 for k, (src, dst) in enumerate(zip(_chunks(_half(g_ref.at[s], 1 - c), HALF_CHUNKS),
                                                    _chunks(got_ref.at[s], HALF_CHUNKS)))]
        for cp in swaps:
            cp.start()
        for cp in swaps:
            cp.wait_recv()
        for cp in swaps:
            cp.wait_send()

    return pl.pallas_call(
        body, name=name, in_specs=[ANY] * n, out_specs=[ANY] * n,
        out_shape=[jax.ShapeDtypeStruct((N_CHIPS, g.shape[1] // 2, g.shape[2]), g.dtype) for g in g_packs],
        scratch_shapes=[pltpu.SemaphoreType.DMA((n * per_group,)), pltpu.SemaphoreType.DMA((n * per_group,))],
    )(*g_packs)


def _grad_ici_refs(refs):
    n = (len(refs) - 3) // 3
    return refs[:n], refs[n:2 * n], refs[2 * n], refs[2 * n + 1], refs[2 * n + 2:3 * n + 2], refs[3 * n + 2]


def _grad_ici_start(*refs):
    s_refs, o_refs, send_sems, recv_sems, _, _ = _grad_ici_refs(refs)
    x, y, c, chips = _place()
    for g, (s_ref, o_ref) in enumerate(zip(s_refs, o_refs)):
        for j, (cx, cy) in enumerate(chips):
            pairs = zip(_chunks(s_ref.at[2 * cx + cy], HALF_CHUNKS), _chunks(o_ref.at[2 * x + y], HALF_CHUNKS))
            for k, (src, dst) in enumerate(pairs):
                _remote(src, dst, send_sems, recv_sems, _sem_index(g, j, k), (cx, cy, c)).start()


def _grad_ici_finish(*refs):
    s_refs, o_refs, send_sems, recv_sems, bufs, loc_sems = _grad_ici_refs(refs)
    x, y, c, chips = _place()
    me = 2 * x + y
    for s_ref, o_ref, buf in zip(s_refs, o_refs, bufs):
        _staged_copy(s_ref.at[me], o_ref.at[me], buf, loc_sems)
    for g, (s_ref, o_ref) in enumerate(zip(s_refs, o_refs)):
        for j, (cx, cy) in enumerate(chips):
            for k, landed in enumerate(_chunks(o_ref.at[2 * cx + cy], HALF_CHUNKS)):
                _remote(landed, landed, send_sems, recv_sems, _sem_index(g, j, k), (x, y, c)).wait_recv()
        for j, (cx, cy) in enumerate(chips):
            for k, sent in enumerate(_chunks(s_ref.at[2 * cx + cy], HALF_CHUNKS)):
                _remote(sent, sent, send_sems, recv_sems, _sem_index(g, j, k), (x, y, c)).wait_send()


def _grad_ici_hosted(sums):
    return _Hosted(list(sums), [jax.ShapeDtypeStruct(s.shape, s.dtype) for s in sums],
                   _link_sems(len(sums)) + _stage_scratch(sums), _grad_ici_start, _grad_ici_finish)


def _grad_ici(sums, *, name):
    def body(*refs):
        _grad_ici_start(*refs)
        _grad_ici_finish(*refs)

    n = len(sums)
    return pl.pallas_call(
        body, name=name, in_specs=[ANY] * n, out_specs=[ANY] * n,
        out_shape=[jax.ShapeDtypeStruct(s.shape, s.dtype) for s in sums],
        scratch_shapes=_link_sems(n) + _stage_scratch(sums), compiler_params=_params(),
    )(*sums)


def _grad_sibling_share(totals, *, name):
    n = len(totals)
    n_ch = HALF_CHUNKS

    def body(*refs):
        t_refs, o_refs = refs[:n], refs[n:2 * n]
        send_sems, recv_sems = refs[2 * n:2 * n + 2]
        bufs, loc_sems = refs[2 * n + 2:3 * n + 2], refs[3 * n + 2]
        x, y, c, _ = _place()
        sib = (x, y, 1 - c)
        sends = [_remote(src, dst, send_sems, recv_sems, g * n_ch + k, sib)
                 for g, (t_ref, o_ref) in enumerate(zip(t_refs, o_refs))
                 for k, (src, dst) in enumerate(zip(_chunks(t_ref, n_ch), _chunks(_half(o_ref, c), n_ch)))]
        for cp in sends:
            cp.start()
        for t_ref, o_ref, buf in zip(t_refs, o_refs, bufs):
            _staged_copy(t_ref, _half(o_ref, c), buf, loc_sems)
        for g, o_ref in enumerate(o_refs):
            for k, landed in enumerate(_chunks(_half(o_ref, 1 - c), n_ch)):
                _remote(landed, landed, send_sems, recv_sems, g * n_ch + k, sib).wait_recv()
        for cp in sends:
            cp.wait_send()

    return pl.pallas_call(
        body, name=name, in_specs=[ANY] * n, out_specs=[ANY] * n,
        out_shape=[jax.ShapeDtypeStruct((2 * t.shape[0], t.shape[1]), t.dtype) for t in totals],
        scratch_shapes=[pltpu.SemaphoreType.DMA((n * n_ch,)), pltpu.SemaphoreType.DMA((n * n_ch,))] + _stage_scratch(totals),
        compiler_params=_params(),
    )(*totals)


def _allgather8(v, *, name):
    rows, cols = v.shape

    def body(v_ref, o_ref, send_sems, recv_sems, loc_sem):
        x, y, c, chips = _place()
        sib = (x, y, 1 - c)

        def slot(px, py, pc):
            return o_ref.at[4 * px + 2 * py + pc]

        local = pltpu.make_async_copy(v_ref, slot(x, y, c), loc_sem.at[0])
        local.start()
        first = [_remote(v_ref, slot(x, y, c), send_sems, recv_sems, 0, sib)]
        first += [_remote(v_ref, slot(x, y, c), send_sems, recv_sems, 1 + j, (*chip, c)) for j, chip in enumerate(chips)]
        for cp in first:
            cp.start()
        passed = [_remote(slot(*chip, c), slot(*chip, c), send_sems, recv_sems, 4 + j, sib)
                  for j, chip in enumerate(chips)]
        for j, chip in enumerate(chips):
            _remote(v_ref, slot(*chip, c), send_sems, recv_sems, 1 + j, sib).wait_recv()
            passed[j].start()
        _remote(v_ref, slot(x, y, 1 - c), send_sems, recv_sems, 0, sib).wait_recv()
        for j, chip in enumerate(chips):
            _remote(v_ref, slot(*chip, 1 - c), send_sems, recv_sems, 4 + j, sib).wait_recv()
        for cp in first + passed:
            cp.wait_send()
        local.wait()

    return pl.pallas_call(
        body, name=name, in_specs=[ANY], out_specs=ANY, out_shape=jax.ShapeDtypeStruct((N_DEV, rows, cols), v.dtype),
        scratch_shapes=[pltpu.SemaphoreType.DMA((7,)), pltpu.SemaphoreType.DMA((7,)), pltpu.SemaphoreType.DMA((1,))],
    )(v)


_BIG = (("w_mod", (D, 6 * D), 1), ("w_in", (D, IN_W), 1), ("w_branch", (3 * D, D), None), ("w_out", (D, D), 0),
        ("w_up", (D, 2 * D_FF), 1), ("w_down", (D_FF, D), 0))
_COL_SHARDED = ("w_mod", "w_in", "w_up")
_ROW_SHARDED = (("w_branch", 3 * D // N_CHIPS), ("w_out", D // N_CHIPS), ("w_down", D_FF // N_CHIPS))


def _pack_shards(shards, layer):
    rows = jnp.concatenate([shards[n][layer].reshape(r, D) for n, r in _ROW_SHARDED], axis=0)
    return [shards[n][layer] for n in _COL_SHARDED] + [rows]


def _unpack_cols(blk):
    return blk.transpose(1, 0, 2).reshape(blk.shape[1], N_CHIPS * blk.shape[2])


def _unpack_rows(stack):
    out, off = {}, 0
    for name, r in _ROW_SHARDED:
        blk = stack[:, off:off + r, :]
        off += r
        if name == "w_branch":
            out[name] = blk.reshape(N_CHIPS, 3, D // N_CHIPS, D).transpose(1, 0, 2, 3).reshape(3, D, D)
        else:
            out[name] = blk.reshape(N_CHIPS * r, D)
    return out


def _unpack_full(gathered):
    out = {name: _unpack_cols(blk) for name, blk in zip(_COL_SHARDED, gathered)}
    out.update(_unpack_rows(gathered[-1]))
    return out


def _pack_grad_cols(g):
    return g.reshape(g.shape[0], N_CHIPS, g.shape[1] // N_CHIPS).transpose(1, 0, 2)


def _pack_grad_rows(grads):
    parts = []
    for name, r in _ROW_SHARDED:
        g = grads[name]
        if name == "w_branch":
            g = g.reshape(3, N_CHIPS, D // N_CHIPS, D).transpose(1, 0, 2, 3)
        parts.append(g.reshape(N_CHIPS, r, D))
    return jnp.concatenate(parts, axis=1)


def _pack_grads(grads):
    return [_pack_grad_cols(grads[n]) for n in _COL_SHARDED] + [_pack_grad_rows(grads)]


def _unpack_shards(totals, like):
    out = {n: jnp.stack([totals[l][g] for l in range(DEPTH)]) for g, n in enumerate(_COL_SHARDED)}
    off = 0
    for name, r in _ROW_SHARDED:
        out[name] = jnp.stack([totals[l][-1][off:off + r] for l in range(DEPTH)]).reshape(like[name].shape)
        off += r
    return out


def _pad_rows(v, rows):
    return jnp.concatenate([v, jnp.zeros((rows - v.shape[0],) + v.shape[1:], v.dtype)], axis=0)


def _local_step(x_tok, target, c_vec, c_ctx, wfull, small, n_lat, n_ctx):
    ctx = _step_context(c_vec, c_ctx, n_lat, n_ctx)
    saved = []
    xs = x_tok
    for l in range(DEPTH):
        xs, s, _ = _layer_fwd(l, xs, wfull[l], {k: v[l] for k, v in small.items() if k != "g_final"}, ctx)
        saved.append(s)
    dx, sq_err, d_g_final = _loss_bwd(xs, target, small["g_final"], n_lat)
    wgrads, lgrads, d_a128 = [None] * DEPTH, [None] * DEPTH, [None] * DEPTH
    for l in reversed(range(DEPTH)):
        dx, wgrads[l], lgrads[l], d_a128[l], _ = _layer_bwd(l, saved[l], wfull[l], dx, ctx)
    return sq_err, dx, wgrads, _small_grads(lgrads, d_a128, d_g_final, ctx)


def _step_context(c_vec, c_ctx, n_lat, n_ctx):
    cos_t, sin_t = _rope_tables(n_lat, n_ctx)
    a_in = _pad_rows(jnp.stack([c_vec, c_ctx]), LANES)
    a128 = _small(_silu, (LANES, D), a_in, name="cond_silu")
    return dict(cos_t=cos_t, sin_t=sin_t, a_in=a_in, a128=a128, n_lat=n_lat, n_ctx=n_ctx)


def _loss_bwd(xs, target, g_final, n_lat):
    dx, st = _loss_head(xs, target, g_final[None, :], n_lat, name="loss_head")
    return dx, st[1], st[0]


def _small_grads(lgrads, d_a128, d_g_final, ctx):
    d_cond = _small(lambda a, b, cin: (a + b) * _dsilu(cin), (LANES, D), d_a128[0], d_a128[1], ctx["a_in"],
                    name="cond_bwd")
    out = {k: jnp.stack([lgrads[l][k] for l in range(DEPTH)]) for k in lgrads[0]}
    out["c_ctx"] = d_cond[1]
    out["g_final"] = d_g_final
    return out


def _layer_fwd(l, xs, w, sm, ctx, hosted=_NO_EXCHANGE, hosted_gating=_NO_EXCHANGE, hosted_ffn=_NO_EXCHANGE,
               late_weights=None):
    n_lat, n_ctx, cos_t, sin_t, a128 = ctx["n_lat"], ctx["n_ctx"], ctx["cos_t"], ctx["sin_t"], ctx["a128"]
    mod128 = _mm(a128, w["w_mod"], name=f"mod{l}")
    mod8 = _small(lambda m, b: m + b, (SUBLANES, 6 * D), mod128[:SUBLANES], sm["b_mod"][None, :], name=f"mod_bias{l}")
    g_mix = sm["g_mix"][None, :]
    g_ffn = sm["g_ffn"][None, :]
    g_v = sm["g_v"][None, :]
    b_gate = sm["b_gate"][None, :]
    sink_b = jnp.broadcast_to(sm["sink"][:, None], (N_HEADS, LANES))
    b_sb = jnp.broadcast_to(sm["b_spatial"][:, :, None], (A_GROUPS, BLK, LANES))
    w_sconv8 = _pad_rows(sm["w_sconv"], SUBLANES)
    w_fconv8 = _pad_rows(sm["w_fconv"], SUBLANES)
    w_in = w["w_in"]
    w_seg = [w_in[:, SEG[k]:SEG[k + 1]] for k in range(4)]

    h = _norm_mod_fwd(xs, g_mix, mod8, 0, 1, n_lat, name=f"norm1_{l}")
    q, kd, vd = _qkv_proj(h, w_seg[0], cos_t, sin_t, name=f"in_proj_qkv{l}")
    z_a, z_b, z_g = [_mm(h, w_seg[k], name=f"in_proj{k}_{l}") for k in range(1, 4)]
    y_attn, carried = _attention_fwd(q, kd, vd, sink_b, n_lat, n_ctx, name=f"attn{l}", hosted=hosted)
    if late_weights is not None:
        w = dict(w, **late_weights(carried))
    y_a, carried_gating = _gating_fwd(z_a, sm["w_spatial"], b_sb, g_v, name=f"gating{l}", hosted=hosted_gating)
    y_b = _sconv_fwd(z_b, w_sconv8, n_lat, name=f"sconv{l}")
    ys = (y_attn, y_a, y_b)
    ts, merged = _branch_merge_fwd(ys, w["w_branch"], z_g, b_gate, name=f"branch_merge{l}")
    mix_out = _mm(merged, w["w_out"], name=f"out_proj{l}")
    x1 = _residual_fwd(xs, mix_out, mod8, 2, n_lat, name=f"res1_{l}")
    h2 = _norm_mod_fwd(x1, g_ffn, mod8, 3, 4, n_lat, name=f"norm2_{l}")
    up = _mm(h2, w["w_up"], name=f"up_proj{l}")
    cv, f, carried_ffn = _ffn_mid_fwd(up, w_fconv8, n_lat, name=f"ffn_mid{l}", hosted=hosted_ffn)
    ffn_out = _mm(f, w["w_down"], name=f"down_proj{l}")
    x2 = _residual_fwd(x1, ffn_out, mod8, 5, n_lat, name=f"res2_{l}")
    saved = dict(x0=xs, mod8=mod8, h=h, z_a=z_a, z_b=z_b, z_g=z_g, q=q, kd=kd, vd=vd, ys=ys, ts=ts,
                 merged=merged, mix_out=mix_out, x1=x1, h2=h2, up=up, cv=cv, f=f, ffn_out=ffn_out, w_seg=w_seg,
                 g_mix=g_mix, g_ffn=g_ffn, g_v=g_v, b_gate=b_gate, sink_b=sink_b, b_sb=b_sb,
                 w_sconv8=w_sconv8, w_fconv8=w_fconv8, w_spatial=sm["w_spatial"])
    return x2, saved, (carried, carried_gating, carried_ffn)


def _layer_bwd(l, s, w, dx, ctx, hosts=None):
    n_lat, n_ctx, cos_t, sin_t, a128 = ctx["n_lat"], ctx["n_ctx"], ctx["cos_t"], ctx["sin_t"], ctx["a128"]
    mod8 = s["mod8"]
    d_ffn, st_gt2 = _residual_bwd(dx, s["ffn_out"], mod8, 5, n_lat, name=f"res2_bwd{l}")
    df = _mm(d_ffn, w["w_down"], tb=True, name=f"down_bwd_x{l}")
    g_down = _mm(s["f"], d_ffn, ta=True, out_dtype=BF16, name=f"down_bwd_w{l}")
    d_up, st_fc = _ffn_mid_bwd(s["up"], s["cv"], df, s["w_fconv8"], n_lat, name=f"ffn_mid_bwd{l}")
    dh2 = _mm(d_up, w["w_up"], tb=True, name=f"up_bwd_x{l}")
    g_up = _mm(s["h2"], d_up, ta=True, out_dtype=BF16, name=f"up_bwd_w{l}")
    dx1, st_n2, _ = _norm_mod_bwd(s["x1"], [dh2], dx, s["g_ffn"], mod8, 4, n_lat, name=f"norm2_bwd{l}")
    d_out, st_gt1 = _residual_bwd(dx1, s["mix_out"], mod8, 2, n_lat, name=f"res1_bwd{l}")
    d_merged = _mm(d_out, w["w_out"], tb=True, name=f"out_bwd_x{l}")
    g_out = _mm(s["merged"], d_out, ta=True, out_dtype=BF16, name=f"out_bwd_w{l}")
    dt0, dt1, dt2, dz_g, st_bg = _merge_bwd(d_merged, s["ts"], s["z_g"], s["b_gate"], name=f"merge_bwd{l}")
    dts = (dt0, dt1, dt2)
    dys = [_mm(dts[k], w["w_branch"][k], tb=True, name=f"branch{k}_bwd_x{l}") for k in range(3)]
    g_branch = jnp.stack([_mm(s["ys"][k], dts[k], ta=True, out_dtype=BF16, name=f"branch{k}_bwd_w{l}")
                          for k in range(3)])
    early = dict(w_branch=g_branch.reshape(3 * D, D), w_out=g_out, w_up=g_up, w_down=g_down)
    hosts = hosts or {}
    in_attn, in_gating = hosts["early"](early) if "early" in hosts else (_NO_EXCHANGE, _NO_EXCHANGE)
    carried = {}
    dq, dk, dv, d_sink, carried["attn"] = _attention_bwd(s["q"], s["kd"], s["vd"], s["sink_b"], dys[0], n_lat, n_ctx,
                                                         name=f"attn_bwd{l}", hosted=in_attn)
    dz_qkv = _qkv_unprep(dq, dk, dv, cos_t, sin_t, name=f"qkv_unprep{l}")
    dz_a, d_ws, d_bs, st_gv, carried["gating"] = _gating_bwd(s["z_a"], dys[1], s["w_spatial"], s["b_sb"], s["g_v"],
                                                             name=f"gating_bwd{l}", hosted=in_gating)
    dz_b, st_sc = _sconv_bwd(s["z_b"], dys[2], s["w_sconv8"], n_lat, name=f"sconv_bwd{l}")
    dzs = (dz_qkv, dz_a, dz_b, dz_g)
    g_in = jnp.concatenate([_mm(s["h"], dzs[k], ta=True, out_dtype=BF16, name=f"in_bwd_w{k}_{l}")
                            for k in range(4)], axis=1)
    dh_parts = [_mm(dzs[k], s["w_seg"][k], tb=True, name=f"in_bwd_x{k}_{l}") for k in range(4)]
    in_norm1 = hosts["w_in"](g_in) if "w_in" in hosts else _NO_EXCHANGE
    dx0, st_n1, carried["norm1"] = _norm_mod_bwd(s["x0"], dh_parts, dx1, s["g_mix"], mod8, 1, n_lat,
                                                 name=f"norm1_bwd{l}", hosted=in_norm1)
    dmod = jnp.concatenate([st_n1[0:2], st_n1[2:4], st_gt1[0:2], st_n2[0:2], st_n2[2:4], st_gt2[0:2]], axis=1)
    dmod128 = _pad_rows(dmod, LANES)
    g_mod = _mm(a128, dmod128, ta=True, out_dtype=BF16, name=f"mod_bwd_w{l}")
    d_a128 = _mm(dmod128, w["w_mod"], tb=True, name=f"mod_bwd_x{l}")
    wgrads = dict(early, w_mod=g_mod, w_in=g_in)
    lgrads = dict(b_mod=dmod[0] + dmod[1], g_mix=st_n1[4], g_ffn=st_n2[4], b_gate=st_bg[0], sink=d_sink[:, 0],
                  w_spatial=d_ws, b_spatial=d_bs[:, :, 0], g_v=st_gv[0], w_sconv=st_sc[0:3], w_fconv=st_fc[0:3])
    return dx0, wgrads, lgrads, d_a128, carried


_SMALL_ORDER = ("c_ctx", "b_mod", "g_mix", "b_gate", "sink", "w_spatial", "b_spatial", "g_v", "w_sconv", "g_ffn",
                "w_fconv", "g_final")


def _flat_pack(parts, width):
    flat = jnp.concatenate([p.reshape(-1).astype(F32) for p in parts])
    rows = -(-flat.shape[0] // (width * SUBLANES)) * SUBLANES
    flat = jnp.concatenate([flat, jnp.zeros((rows * width - flat.shape[0],), F32)])
    return flat.reshape(rows, width)


def _flat_unpack(packed, likes):
    flat = packed.reshape(-1)
    out, off = [], 0
    for like in likes:
        n = math.prod(like.shape)
        out.append(flat[off:off + n].reshape(like.shape))
        off += n
    return out


def kernel(x, c, ctx, c_ctx, w_mod, b_mod, g_mix, w_in, b_gate, sink, w_spatial, b_spatial, g_v, w_sconv, w_branch, w_out, g_ffn, w_up, w_fconv, w_down, g_final, loss_target, m_c_ctx, m_w_mod, m_b_mod, m_g_mix, m_w_in, m_b_gate, m_sink, m_w_spatial, m_b_spatial, m_g_v, m_w_sconv, m_w_branch, m_w_out, m_g_ffn, m_w_up, m_w_fconv, m_w_down, m_g_final, v_c_ctx, v_w_mod, v_b_mod, v_g_mix, v_w_in, v_b_gate, v_sink, v_w_spatial, v_b_spatial, v_g_v, v_w_sconv, v_w_branch, v_w_out, v_g_ffn, v_w_up, v_w_fconv, v_w_down, v_g_final):
    n_lat, n_ctx = x.shape[1], ctx.shape[1]
    chip = 2 * lax.axis_index("x") + lax.axis_index("y")
    weights = dict(c_ctx=c_ctx, w_mod=w_mod, b_mod=b_mod, g_mix=g_mix, w_in=w_in, b_gate=b_gate, sink=sink,
                   w_spatial=w_spatial, b_spatial=b_spatial, g_v=g_v, w_sconv=w_sconv, w_branch=w_branch, w_out=w_out,
                   g_ffn=g_ffn, w_up=w_up, w_fconv=w_fconv, w_down=w_down, g_final=g_final)
    m_in = dict(c_ctx=m_c_ctx, w_mod=m_w_mod, b_mod=m_b_mod, g_mix=m_g_mix, w_in=m_w_in, b_gate=m_b_gate, sink=m_sink,
                w_spatial=m_w_spatial, b_spatial=m_b_spatial, g_v=m_g_v, w_sconv=m_w_sconv, w_branch=m_w_branch,
                w_out=m_w_out, g_ffn=m_g_ffn, w_up=m_w_up, w_fconv=m_w_fconv, w_down=m_w_down, g_final=m_g_final)
    v_in = dict(c_ctx=v_c_ctx, w_mod=v_w_mod, b_mod=v_b_mod, g_mix=v_g_mix, w_in=v_w_in, b_gate=v_b_gate, sink=v_sink,
                w_spatial=v_w_spatial, b_spatial=v_b_spatial, g_v=v_g_v, w_sconv=v_w_sconv, w_branch=v_w_branch,
                w_out=v_w_out, g_ffn=v_g_ffn, w_up=v_w_up, w_fconv=v_w_fconv, w_down=v_w_down, g_final=v_g_final)
    big_names = [n for n, _, _ in _BIG]

    conv_pack = _flat_pack([w_sconv, w_fconv], LANES)
    conv_all = _allgather8(conv_pack, name="gather_conv_weights")
    conv_parts = [_flat_unpack(conv_all[2 * p], [w_sconv, w_fconv]) for p in range(N_CHIPS)]
    w_sconv_full = jnp.concatenate([cp[0] for cp in conv_parts], axis=-1)
    w_fconv_full = jnp.concatenate([cp[1] for cp in conv_parts], axis=-1)

    small = dict(b_mod=b_mod, g_mix=g_mix, b_gate=b_gate, sink=sink, w_spatial=w_spatial, b_spatial=b_spatial, g_v=g_v,
                 w_sconv=w_sconv_full, g_ffn=g_ffn, w_fconv=w_fconv_full, g_final=g_final)
    x_tok = jnp.concatenate([x[0], ctx[0]], axis=0)
    step = _step_context(c[0], c_ctx, n_lat, n_ctx)
    layer_small = [{k: v[l] for k, v in small.items() if k != "g_final"} for l in range(DEPTH)]
    my_half = lax.axis_index("c").astype(jnp.int32).reshape(1)

    shards = {n: weights[n].astype(BF16) for n in big_names}
    pack = [_pack_shards(shards, l) for l in range(DEPTH)]
    first = _gather_finish(_gather_ici(pack[0][:2], name="gather_ici0"), pack[0][:2], name="gather_finish0")
    w0 = dict(w_mod=_unpack_cols(first[0]), w_in=_unpack_cols(first[1]))

    def layer0_late_weights(carried):
        rest = _gather_finish(list(carried[1:]), pack[0][2:], name="gather_finish0_late")
        w0.update(w_up=_unpack_cols(rest[0]), **_unpack_rows(rest[1]))
        return w0

    xs, saved0, (part_attn, part_gating, part_ffn) = _layer_fwd(
        0, x_tok, w0, layer_small[0], step, hosted=_gather_ici_hosted(pack[1][1:2] + pack[0][2:]),
        hosted_gating=_gather_ici_hosted(pack[1][:1]), hosted_ffn=_gather_ici_hosted(pack[1][2:]),
        late_weights=layer0_late_weights)
    partial1 = list(part_gating) + list(part_attn[:1]) + list(part_ffn)
    w1 = _unpack_full(_gather_finish(partial1, pack[1], name="gather_finish1"))
    xs, saved1, _ = _layer_fwd(1, xs, w1, layer_small[1], step)
    dx, sq_err, d_g_final = _loss_bwd(xs, loss_target[0], g_final, n_lat)
    loss = lax.psum(0.5 * jnp.sum(sq_err) / D, ("x", "y", "c"))

    def reduce_start(g_packs, tag):
        got = _grad_sibling_swap(g_packs, name=f"grad_sibling_swap{tag}")
        return [_add_half(my_half, a, b, name=f"grad_pair_sum{tag}_{g}") for g, (a, b) in enumerate(zip(g_packs, got))]

    def reduce_finish(exchanged, tag):
        sums = [_sum_slabs(e, F32, name=f"grad_chip_sum{tag}_{g}") for g, e in enumerate(exchanged)]
        return _grad_sibling_share(sums, name=f"grad_sibling_share{tag}")

    dx, wgrads1, lgrads1, d_a1, _ = _layer_bwd(1, saved1, w1, dx, step)
    pair_sum1 = reduce_start(_pack_grads(wgrads1), "1")

    def carried_early(early):
        pair_sum0_early = reduce_start([_pack_grad_cols(early["w_up"]), _pack_grad_rows(early)], "0_early")
        return _grad_ici_hosted(pair_sum1 + pair_sum0_early[1:]), _grad_ici_hosted(pair_sum0_early[:1])

    def carried_w_in(g_in):
        return _grad_ici_hosted(reduce_start([_pack_grad_cols(g_in)], "0_in"))

    dx, wgrads0, lgrads0, d_a0, exchanged = _layer_bwd(0, saved0, w0, dx, step,
                                                       hosts=dict(early=carried_early, w_in=carried_w_in))
    total1 = reduce_finish(exchanged["attn"][:4], "1")
    total0_early = reduce_finish(list(exchanged["gating"]) + list(exchanged["attn"][4:]), "0_early")
    total0_in = reduce_finish(exchanged["norm1"], "0_in")
    mod_sum = reduce_start([_pack_grad_cols(wgrads0["w_mod"])], "0_mod")
    sgrads = _small_grads([lgrads0, lgrads1], [d_a0, d_a1], d_g_final, step)
    grad_x = dx[:n_lat][None]

    s_likes = [sgrads[n] for n in _SMALL_ORDER]
    s_all = _allgather8(_flat_pack(s_likes, D), name="gather_small_grads")
    s_tot = _flat_unpack(_sum_slabs(s_all, F32, name="small_grad_sum"), s_likes)
    grads = dict(zip(_SMALL_ORDER, s_tot))
    grads["w_sconv"] = lax.dynamic_slice_in_dim(grads["w_sconv"], chip * w_sconv.shape[-1], w_sconv.shape[-1], axis=2)
    grads["w_fconv"] = lax.dynamic_slice_in_dim(grads["w_fconv"], chip * w_fconv.shape[-1], w_fconv.shape[-1], axis=2)

    delta, new_m, new_v = {}, {}, {}

    def adamw(n, hosted=None):
        cols = weights[n].shape[-1]
        view = lambda a: a.reshape(-1, cols)
        d_, m_, v_, carried = _adamw(view(weights[n]), view(grads[n]), view(m_in[n]), view(v_in[n]), name=f"adamw_{n}",
                                     hosted=hosted)
        delta[n], new_m[n], new_v[n] = (t.reshape(weights[n].shape) for t in (d_, m_, v_))
        return carried

    grads["w_in"] = jnp.stack([total0_in[0], total1[1]])
    total0_mod = reduce_finish(adamw("w_in", _grad_ici_hosted(mod_sum)), "0_mod")
    total0 = list(total0_mod) + list(total0_in) + list(total0_early)
    rest = _unpack_shards([total0, total1], {n: weights[n] for n in big_names})
    grads.update({n: g for n, g in rest.items() if n != "w_in"})
    for n in big_names:
        if n != "w_in":
            adamw(n)
    likes = [weights[n] for n in _SMALL_ORDER]
    packs = [_flat_pack([src[n] for n in _SMALL_ORDER], D) for src in (weights, grads, m_in, v_in)]
    outs = _adamw(*packs, name="adamw_small")[:3]
    for dst, packed in zip((delta, new_m, new_v), outs):
        for n, val in zip(_SMALL_ORDER, _flat_unpack(packed, likes)):
            dst[n] = val

    order = ("c_ctx", "w_mod", "b_mod", "g_mix", "w_in", "b_gate", "sink", "w_spatial", "b_spatial", "g_v", "w_sconv",
             "w_branch", "w_out", "g_ffn", "w_up", "w_fconv", "w_down", "g_final")
    return (loss, grad_x, *[grads[n] for n in order], *[delta[n] for n in order], *[new_m[n] for n in order],
            *[new_v[n] for n in order])
```

```python
import functools
import math

import jax
import jax.numpy as jnp
from jax import lax
from jax.experimental import pallas as pl
from jax.experimental.pallas import tpu as pltpu

F32 = jnp.float32
BF16 = jnp.bfloat16

D = 1024
DEPTH = 2
GRID_W = 64
N_HEADS = 16
N_KV = 4
GRP = N_HEADS // N_KV
HEAD_DIM = 64
KV_W = N_KV * HEAD_DIM
WINDOW = 128
BLK = 128
ROPE_THETA = 10000.0
A_GROUPS = 8
D_FF = 2816
EPS = 1e-6
NEG = -1e30
QKV_W = D + 2 * KV_W
A_COLS = 2 * D
B_COLS = 3 * D
G_COLS = 3 * D
IN_W = QKV_W + A_COLS + B_COLS + G_COLS
SEG = (0, QKV_W, QKV_W + A_COLS, QKV_W + A_COLS + B_COLS, IN_W)
N_CHIPS = 4
N_DEV = 8
LANES = 128
SUBLANES = 8
VMEM_LIMIT = 48 * 1024 * 1024
VMEM_LIMIT_WIDE = 56 * 1024 * 1024
ADAM_LR = 0.001
ADAM_B1 = 0.9
ADAM_B2 = 0.999
ADAM_EPS = 1e-08
ADAM_WD = 0.01
ADAM_STEP = 10
MESH = pl.DeviceIdType.MESH
ANY = pl.BlockSpec(memory_space=pl.ANY)


def _params(sem=None, vmem=VMEM_LIMIT):
    return pltpu.CompilerParams(dimension_semantics=sem, vmem_limit_bytes=vmem)


def _pick(n, cands):
    for c in cands:
        if n % c == 0:
            return c
    return n


def _rows8(rows, width):
    r = lax.broadcasted_iota(jnp.int32, (SUBLANES, width), 0)
    out = jnp.zeros((SUBLANES, width), F32)
    for idx, v in rows:
        out = out + jnp.where(r == idx, v, 0.0)
    return out


def _sel(mod_ref, k, is_ctx):
    return jnp.where(is_ctx, mod_ref[1:2, k * D:(k + 1) * D], mod_ref[0:1, k * D:(k + 1) * D])


def _colsum(v):
    return jnp.sum(v, axis=0, keepdims=True)


def _mm(a, b, *, name, ta=False, tb=False, out_dtype=F32):
    if ta:
        k_dim, m = a.shape
    else:
        m, k_dim = a.shape
    if tb:
        n, kb = b.shape
    else:
        kb, n = b.shape
    assert k_dim == kb, (a.shape, b.shape, ta, tb)
    tm = _pick(m, (1056, 1024, 1408, 768, 512, 256, 128))
    tn = _pick(n, (1536, 1408, 1024, 768, 512, 256, 128))
    tk = _pick(k_dim, (2048, 1536, 1408, 1024, 768, 512, 256, 128))
    nk = k_dim // tk
    dims = (((0 if ta else 1,), (1 if tb else 0,)), ((), ()))

    def product(a_ref, b_ref):
        return lax.dot_general(a_ref[...].astype(BF16), b_ref[...].astype(BF16), dims, preferred_element_type=F32)

    def body_single(a_ref, b_ref, o_ref):
        o_ref[...] = product(a_ref, b_ref).astype(o_ref.dtype)

    def body_acc(a_ref, b_ref, o_ref, acc_ref):
        k = pl.program_id(2)

        @pl.when(k == 0)
        def _():
            acc_ref[...] = product(a_ref, b_ref)

        @pl.when(k > 0)
        def _():
            acc_ref[...] += product(a_ref, b_ref)

        @pl.when(k == nk - 1)
        def _():
            o_ref[...] = acc_ref[...].astype(o_ref.dtype)

    a_spec = pl.BlockSpec((tk, tm), lambda i, j, k: (k, i)) if ta else pl.BlockSpec((tm, tk), lambda i, j, k: (i, k))
    b_spec = pl.BlockSpec((tn, tk), lambda i, j, k: (j, k)) if tb else pl.BlockSpec((tk, tn), lambda i, j, k: (k, j))
    return pl.pallas_call(
        body_single if nk == 1 else body_acc, name=name, grid=(m // tm, n // tn, nk),
        in_specs=[a_spec, b_spec], out_specs=pl.BlockSpec((tm, tn), lambda i, j, k: (i, j)),
        out_shape=jax.ShapeDtypeStruct((m, n), out_dtype),
        scratch_shapes=[] if nk == 1 else [pltpu.VMEM((tm, tn), F32)],
        compiler_params=_params(("parallel", "parallel", "arbitrary")),
    )(a, b)


def _small(fn, out_shape, *arrays, name):
    def body(*refs):
        refs[-1][...] = fn(*[r[...] for r in refs[:-1]]).astype(refs[-1].dtype)

    return pl.pallas_call(body, name=name, out_shape=jax.ShapeDtypeStruct(out_shape, F32))(*arrays)


def _silu(v):
    return v * jax.nn.sigmoid(v)


def _dsilu(v):
    s = jax.nn.sigmoid(v)
    return s * (1.0 + v * (1.0 - s))


def _row_spec(tm, width, col=0):
    return pl.BlockSpec((tm, width), lambda i: (i, col))


def _full_spec(shape):
    nd = len(shape)
    return pl.BlockSpec(shape, lambda i: (0,) * nd)


def _halo_specs(tm, width, t_rows, col=0):
    per = tm // SUBLANES
    last = t_rows // SUBLANES - 1
    prev = pl.BlockSpec((SUBLANES, width), lambda i: (jnp.maximum(i * per - 1, 0), col))
    nxt = pl.BlockSpec((SUBLANES, width), lambda i: (jnp.minimum((i + 1) * per, last), col))
    return prev, nxt


def _shift_rows(cur, prev8, next8, n_lat, t_rows, tm):
    i = pl.program_id(0)
    row = lax.broadcasted_iota(jnp.int32, (tm, 1), 0)
    g = row + i * tm
    up = pltpu.roll(cur, 1, 0)
    up = jnp.where(row == 0, prev8[SUBLANES - 1:SUBLANES, :], up)
    up = jnp.where((g == 0) | (g == n_lat), 0.0, up)
    dn = pltpu.roll(cur, tm - 1, 0)
    dn = jnp.where(row == tm - 1, next8[0:1, :], dn)
    dn = jnp.where((g == n_lat - 1) | (g == t_rows - 1), 0.0, dn)
    return up, dn


def _norm_mod_fwd(x, g, mod8, sh_idx, sc_idx, n_lat, *, name):
    t_rows = x.shape[0]
    tm = 256

    def body(x_ref, g_ref, mod_ref, o_ref):
        is_ctx = pl.program_id(0) * tm >= n_lat
        xv = x_ref[...]
        rstd = lax.rsqrt(jnp.mean(xv * xv, axis=-1, keepdims=True) + EPS)
        y = xv * rstd * g_ref[...]
        o_ref[...] = (y * (1.0 + _sel(mod_ref, sc_idx, is_ctx)) + _sel(mod_ref, sh_idx, is_ctx)).astype(BF16)

    return pl.pallas_call(
        body, name=name, grid=(t_rows // tm,),
        in_specs=[_row_spec(tm, D), _full_spec((1, D)), _full_spec((SUBLANES, 6 * D))],
        out_specs=_row_spec(tm, D), out_shape=jax.ShapeDtypeStruct((t_rows, D), BF16),
        compiler_params=_params(("parallel",)),
    )(x, g, mod8)


def _norm_mod_bwd(x, dh_parts, dres, g, mod8, sc_idx, n_lat, *, name, hosted=None, res=None):
    hosted = hosted or _NO_EXCHANGE
    t_rows = x.shape[0]
    tm = 256
    n_parts = len(dh_parts)
    n_steps = t_rows // tm
    fused = res is not None

    def body(*refs):
        ins, outs, _, h_refs = _split_refs(refs, 4 + n_parts + fused, 2 + fused, 0, hosted)
        x_ref, dres_ref, g_ref, mod_ref = ins[:4]
        part_refs = ins[4:4 + n_parts]
        dx_ref, st_ref = outs[:2]
        i = pl.program_id(0)
        _run_hosted(hosted, h_refs, i, n_steps)
        is_ctx = i * tm >= n_lat
        dh = part_refs[0][...]
        for p in part_refs[1:]:
            dh = dh + p[...]
        xv = x_ref[...]
        gv = g_ref[...]
        rstd = lax.rsqrt(jnp.mean(xv * xv, axis=-1, keepdims=True) + EPS)
        rn = xv * rstd
        dy = dh * (1.0 + _sel(mod_ref, sc_idx, is_ctx))
        e = dy * gv
        dxv = dres_ref[...] + rstd * (e - rn * jnp.mean(e * rn, axis=-1, keepdims=True))
        dx_ref[...] = dxv
        dsh = _colsum(dh)
        dsc = _colsum(dh * (rn * gv))
        dg = _colsum(dy * rn)
        zero = jnp.zeros_like(dsh)
        rows = [(0, jnp.where(is_ctx, zero, dsh)), (1, jnp.where(is_ctx, dsh, zero)),
                (2, jnp.where(is_ctx, zero, dsc)), (3, jnp.where(is_ctx, dsc, zero)), (4, dg)]
        if fused:
            outs[2][...] = (dxv * _sel(mod_ref, res[1], is_ctx)).astype(BF16)
            dgt = _colsum(dxv * ins[4 + n_parts][...])
            rows += [(5, jnp.where(is_ctx, zero, dgt)), (6, jnp.where(is_ctx, dgt, zero))]
        upd = _rows8(rows, D)

        @pl.when(i == 0)
        def _():
            st_ref[...] = upd

        @pl.when(i > 0)
        def _():
            st_ref[...] += upd

    outs = pl.pallas_call(
        body, name=name, grid=(n_steps,),
        in_specs=[_row_spec(tm, D), _row_spec(tm, D), _full_spec((1, D)), _full_spec((SUBLANES, 6 * D))]
        + [_row_spec(tm, D)] * (n_parts + fused) + [ANY] * len(hosted.arrays),
        out_specs=[_row_spec(tm, D), _full_spec((SUBLANES, D))] + [_row_spec(tm, D)] * fused
        + [ANY] * len(hosted.out_shapes),
        out_shape=[jax.ShapeDtypeStruct((t_rows, D), F32), jax.ShapeDtypeStruct((SUBLANES, D), F32)]
        + [jax.ShapeDtypeStruct((t_rows, D), BF16)] * fused + list(hosted.out_shapes),
        scratch_shapes=list(hosted.scratch),
        compiler_params=_params(("arbitrary",)),
    )(x, dres, g, mod8, *dh_parts, *([res[0]] if fused else []), *hosted.arrays)
    return outs[0], outs[1], outs[2 + fused:], (outs[2] if fused else None)


def _residual_fwd(x, branch, mod8, gt_idx, n_lat, *, name, norm=None):
    t_rows = x.shape[0]
    tm = 256

    def body(x_ref, b_ref, mod_ref, *rest):
        is_ctx = pl.program_id(0) * tm >= n_lat
        xv = x_ref[...] + _sel(mod_ref, gt_idx, is_ctx) * b_ref[...]
        if norm is None:
            rest[0][...] = xv
            return
        g_ref, o_ref, h_ref = rest
        o_ref[...] = xv
        rstd = lax.rsqrt(jnp.mean(xv * xv, axis=-1, keepdims=True) + EPS)
        y = xv * rstd * g_ref[...]
        h_ref[...] = (y * (1.0 + _sel(mod_ref, norm[2], is_ctx)) + _sel(mod_ref, norm[1], is_ctx)).astype(BF16)

    fused = norm is not None
    outs = pl.pallas_call(
        body, name=name, grid=(t_rows // tm,),
        in_specs=[_row_spec(tm, D), _row_spec(tm, D), _full_spec((SUBLANES, 6 * D))] + [_full_spec((1, D))] * fused,
        out_specs=[_row_spec(tm, D)] * (1 + fused),
        out_shape=[jax.ShapeDtypeStruct((t_rows, D), F32)] + [jax.ShapeDtypeStruct((t_rows, D), BF16)] * fused,
        compiler_params=_params(("parallel",)),
    )(x, branch, mod8, *([norm[0]] if fused else []))
    return outs[0], (outs[1] if fused else None)


def _residual_bwd(dx, branch, mod8, gt_idx, n_lat, *, name):
    t_rows = dx.shape[0]
    tm = 256

    def body(dx_ref, b_ref, mod_ref, o_ref, st_ref):
        i = pl.program_id(0)
        is_ctx = i * tm >= n_lat
        dxv = dx_ref[...]
        o_ref[...] = (dxv * _sel(mod_ref, gt_idx, is_ctx)).astype(BF16)
        dgt = _colsum(dxv * b_ref[...])
        zero = jnp.zeros_like(dgt)
        upd = _rows8([(0, jnp.where(is_ctx, zero, dgt)), (1, jnp.where(is_ctx, dgt, zero))], D)

        @pl.when(i == 0)
        def _():
            st_ref[...] = upd

        @pl.when(i > 0)
        def _():
            st_ref[...] += upd

    return pl.pallas_call(
        body, name=name, grid=(t_rows // tm,),
        in_specs=[_row_spec(tm, D), _row_spec(tm, D), _full_spec((SUBLANES, 6 * D))],
        out_specs=[_row_spec(tm, D), _full_spec((SUBLANES, D))],
        out_shape=[jax.ShapeDtypeStruct((t_rows, D), BF16), jax.ShapeDtypeStruct((SUBLANES, D), F32)],
        compiler_params=_params(("arbitrary",)),
    )(dx, branch, mod8)


def _rope_tables(n_lat, n_ctx):
    rows = n_lat // GRID_W
    row = jnp.broadcast_to(jnp.arange(rows, dtype=F32)[:, None], (rows, GRID_W)).reshape(n_lat)
    col = jnp.broadcast_to(jnp.arange(GRID_W, dtype=F32)[None, :], (rows, GRID_W)).reshape(n_lat)
    half = HEAD_DIM // 2
    inv = ROPE_THETA ** (-jnp.arange(0, half, 2, dtype=F32) / half)
    ang = jnp.concatenate([row[:, None] * inv, col[:, None] * inv], axis=-1)
    cos, sin = jnp.cos(ang), jnp.sin(ang)
    c64 = jnp.concatenate([cos, cos], axis=-1)
    s64 = jnp.concatenate([-sin, sin], axis=-1)
    c64 = jnp.concatenate([c64, jnp.ones((n_ctx, HEAD_DIM), F32)], axis=0)
    s64 = jnp.concatenate([s64, jnp.zeros((n_ctx, HEAD_DIM), F32)], axis=0)
    return jnp.tile(c64, (1, 2)), jnp.tile(s64, (1, 2))


def _swap_halves(v):
    lane = lax.broadcasted_iota(jnp.int32, v.shape, 1)
    return jnp.where(lane % HEAD_DIM < HEAD_DIM // 2, pltpu.roll(v, LANES - HEAD_DIM // 2, 1),
                     pltpu.roll(v, HEAD_DIM // 2, 1))


def _low_half(shape):
    return lax.broadcasted_iota(jnp.int32, shape, 1) < HEAD_DIM


def _qkv_proj(h, w_qkv, cos_t, sin_t, *, name):
    t_rows = h.shape[0]
    tm = _pick(t_rows, (768, 512, 256))

    def body(h_ref, w_ref, c_ref, s_ref, q_ref, k_ref, v_ref):
        z = jnp.dot(h_ref[...], w_ref[...], preferred_element_type=F32)
        cv, sv = c_ref[...], s_ref[...]

        def rope(chunk):
            return chunk * cv + _swap_halves(chunk) * sv

        for ch in range(D // LANES):
            roped = rope(z[:, ch * LANES:(ch + 1) * LANES])
            q_ref[:, ch * LANES:(ch + 1) * LANES] = (roped * (HEAD_DIM ** -0.5)).astype(BF16)
        low = _low_half((tm, LANES))
        for pair in range(N_KV // 2):
            for which, ref, roped in ((0, k_ref, True), (1, v_ref, False)):
                off = D + which * KV_W + pair * LANES
                chunk = z[:, off:off + LANES]
                if roped:
                    chunk = rope(chunk)
                other = pltpu.roll(chunk, HEAD_DIM, 1)
                even = jnp.where(low, chunk, other)
                odd = jnp.where(low, other, chunk)
                ref[:, (2 * pair) * LANES:(2 * pair + 1) * LANES] = even.astype(BF16)
                ref[:, (2 * pair + 1) * LANES:(2 * pair + 2) * LANES] = odd.astype(BF16)

    dup_w = N_KV * LANES
    return pl.pallas_call(
        body, name=name, grid=(t_rows // tm,),
        in_specs=[_row_spec(tm, D), _full_spec((D, QKV_W)), _row_spec(tm, LANES), _row_spec(tm, LANES)],
        out_specs=[_row_spec(tm, D), _row_spec(tm, dup_w), _row_spec(tm, dup_w)],
        out_shape=[jax.ShapeDtypeStruct((t_rows, D), BF16), jax.ShapeDtypeStruct((t_rows, dup_w), BF16),
                   jax.ShapeDtypeStruct((t_rows, dup_w), BF16)],
        compiler_params=_params(("parallel",)),
    )(h, w_qkv, cos_t, sin_t)


def _qkv_unprep(dq, dk, dv, cos_t, sin_t, *, name):
    t_rows = dq.shape[0]
    tm = 256

    def body(dq_ref, dk_ref, dv_ref, c_ref, s_ref, o_ref):
        cv, sv = c_ref[...], s_ref[...]

        def unrope(chunk):
            return chunk * cv + _swap_halves(chunk * sv)

        for ch in range(D // LANES):
            o_ref[:, ch * LANES:(ch + 1) * LANES] = unrope(dq_ref[:, ch * LANES:(ch + 1) * LANES]).astype(BF16)
        for pair in range(N_KV // 2):
            for which, ref, roped in ((0, dk_ref, True), (1, dv_ref, False)):
                chunk = ref[:, pair * LANES:(pair + 1) * LANES]
                if roped:
                    chunk = unrope(chunk)
                off = D + which * KV_W + pair * LANES
                o_ref[:, off:off + LANES] = chunk.astype(BF16)

    return pl.pallas_call(
        body, name=name, grid=(t_rows // tm,),
        in_specs=[_row_spec(tm, D), _row_spec(tm, KV_W), _row_spec(tm, KV_W), _row_spec(tm, LANES),
                  _row_spec(tm, LANES)],
        out_specs=_row_spec(tm, QKV_W), out_shape=jax.ShapeDtypeStruct((t_rows, QKV_W), BF16),
        compiler_params=_params(("parallel",)),
    )(dq, dk, dv, cos_t, sin_t)


def _attn_specs(n_lat, n_ctx):
    nb = n_lat // BLK
    dup_w = N_KV * LANES

    def ws(j):
        return jnp.clip(j - 1, 0, nb - 3)

    win = [pl.BlockSpec((BLK, dup_w), functools.partial(lambda j, o: (ws(j) + o, 0), o=o)) for o in range(3)]
    ctx = pl.BlockSpec((n_ctx, dup_w), lambda j: (n_lat // n_ctx, 0))
    return nb, ws, win, ctx


def _attn_bias(j, ws_j, nb, n_ctx):
    n_keys = 3 * BLK + n_ctx
    row = lax.broadcasted_iota(jnp.int32, (BLK, n_keys), 0)
    col = lax.broadcasted_iota(jnp.int32, (BLK, n_keys), 1)
    rel = (ws_j - j) * BLK + col - row
    valid = (col >= 3 * BLK) | ((jnp.abs(rel) <= WINDOW) & (j < nb))
    bias = jnp.where(valid, 0.0, NEG)
    return jnp.concatenate([bias] * GRP, axis=0)


def _attn_probs(q_ref, kk, kh, bias, sink_ref):
    low = _low_half((BLK, LANES))
    qs = []
    for g in range(GRP):
        h = GRP * kh + g
        chunk = q_ref[:, (h // 2) * LANES:(h // 2 + 1) * LANES]
        qs.append(jnp.where(low if h % 2 == 0 else ~low, chunk, jnp.zeros_like(chunk)))
    qs = jnp.concatenate(qs, axis=0)
    s = lax.dot_general(qs, kk, (((1,), (1,)), ((), ())), preferred_element_type=F32) + bias
    snk = jnp.concatenate(
        [jnp.broadcast_to(jnp.max(sink_ref[GRP * kh + g:GRP * kh + g + 1, :], axis=1, keepdims=True), (BLK, 1))
         for g in range(GRP)], axis=0)
    m = jnp.maximum(jnp.max(s, axis=-1, keepdims=True), snk)
    p = jnp.exp(s - m)
    p_snk = jnp.exp(snk - m)
    inv = 1.0 / (jnp.sum(p, axis=-1, keepdims=True) + p_snk)
    return qs, p, p_snk, inv


class _Hosted:
    def __init__(self, arrays, out_shapes, scratch, start, finish):
        self.arrays, self.out_shapes, self.scratch, self.start, self.finish = arrays, out_shapes, scratch, start, finish


_NO_EXCHANGE = _Hosted([], [], [], None, None)


def _split_refs(refs, n_in, n_out, n_scratch, hosted):
    hi, ho, hs = len(hosted.arrays), len(hosted.out_shapes), len(hosted.scratch)
    a = n_in + hi
    b = a + n_out + ho
    ins, h_ins = refs[:n_in], refs[n_in:a]
    outs, h_outs = refs[a:a + n_out], refs[a + n_out:b]
    scr, h_scr = refs[b:b + n_scratch], refs[b + n_scratch:b + n_scratch + hs]
    return ins, outs, scr, (h_ins, h_outs, h_scr)


def _run_hosted(hosted, h_refs, step, n_steps):
    if hosted.start is None:
        return

    flat = [r for group in h_refs for r in group]

    @pl.when(step == 0)
    def _():
        hosted.start(*flat)

    @pl.when(step == n_steps - 1)
    def _():
        hosted.finish(*flat)


def _attention_fwd(q, kd, vd, sink_b, n_lat, n_ctx, *, name, hosted=_NO_EXCHANGE):
    t_rows = q.shape[0]
    nb, ws, win, ctx = _attn_specs(n_lat, n_ctx)
    n_steps = t_rows // BLK

    def body(*refs):
        ins, outs, _, h_refs = _split_refs(refs, 10, 1, 0, hosted)
        q_ref, k0, k1, k2, kc, v0, v1, v2, vc, sink_ref = ins
        o_ref, = outs
        j = pl.program_id(0)
        _run_hosted(hosted, h_refs, j, n_steps)
        ws_j = ws(j)
        low = _low_half((BLK, LANES))
        bias = _attn_bias(j, ws_j, nb, n_ctx)
        for kh in range(N_KV):
            sl = slice(kh * LANES, (kh + 1) * LANES)
            kk = jnp.concatenate([k0[:, sl], k1[:, sl], k2[:, sl], kc[:, sl]], axis=0)
            vv = jnp.concatenate([v0[:, sl], v1[:, sl], v2[:, sl], vc[:, sl]], axis=0)
            _, p, _, inv = _attn_probs(q_ref, kk, kh, bias, sink_ref)
            o = jnp.dot(p.astype(BF16), vv, preferred_element_type=F32) * inv
            for half in range(2):
                even = o[(2 * half) * BLK:(2 * half + 1) * BLK]
                odd = o[(2 * half + 1) * BLK:(2 * half + 2) * BLK]
                ch = 2 * kh + half
                o_ref[:, ch * LANES:(ch + 1) * LANES] = jnp.where(low, even, odd).astype(BF16)

    outs = pl.pallas_call(
        body, name=name, grid=(n_steps,),
        in_specs=[_row_spec(BLK, D)] + win + [ctx] + win + [ctx] + [_full_spec((N_HEADS, LANES))]
        + [ANY] * len(hosted.arrays),
        out_specs=[_row_spec(BLK, D)] + [ANY] * len(hosted.out_shapes),
        out_shape=[jax.ShapeDtypeStruct((t_rows, D), BF16)] + list(hosted.out_shapes),
        scratch_shapes=list(hosted.scratch),
        compiler_params=_params(("arbitrary",)),
    )(q, kd, kd, kd, kd, vd, vd, vd, vd, sink_b, *hosted.arrays)
    return outs[0], outs[1:]


def _attention_bwd(q, kd, vd, sink_b, dy, n_lat, n_ctx, *, name, hosted=_NO_EXCHANGE):
    t_rows = q.shape[0]
    nb, ws, win, ctx = _attn_specs(n_lat, n_ctx)
    n_steps = t_rows // BLK

    def body(*refs):
        ins, outs, scr, h_refs = _split_refs(refs, 11, 4, 3, hosted)
        q_ref, k0, k1, k2, kc, v0, v1, v2, vc, sink_ref, dy_ref = ins
        dq_ref, dk_hbm, dv_hbm, ds_ref = outs
        dk_acc, dv_acc, sem = scr
        j = pl.program_id(0)
        _run_hosted(hosted, h_refs, j, n_steps)
        ws_j = ws(j)

        @pl.when(j == 0)
        def _():
            dk_acc[...] = jnp.zeros_like(dk_acc)
            dv_acc[...] = jnp.zeros_like(dv_acc)
            ds_ref[...] = jnp.zeros_like(ds_ref)

        low = _low_half((BLK, LANES))
        low_keys = _low_half((3 * BLK + n_ctx, LANES))
        win_start = pl.multiple_of(ws_j * BLK, BLK)
        scale = HEAD_DIM ** -0.5
        dk_heads, dv_heads = [], []
        bias = _attn_bias(j, ws_j, nb, n_ctx)
        for kh in range(N_KV):
            sl = slice(kh * LANES, (kh + 1) * LANES)
            kk = jnp.concatenate([k0[:, sl], k1[:, sl], k2[:, sl], kc[:, sl]], axis=0)
            vv = jnp.concatenate([v0[:, sl], v1[:, sl], v2[:, sl], vc[:, sl]], axis=0)
            qs, p, p_snk, inv = _attn_probs(q_ref, kk, kh, bias, sink_ref)
            dos = []
            for g in range(GRP):
                h = GRP * kh + g
                chunk = dy_ref[:, (h // 2) * LANES:(h // 2 + 1) * LANES]
                dos.append(jnp.where(low if h % 2 == 0 else ~low, chunk, jnp.zeros_like(chunk)).astype(BF16))
            dos = jnp.concatenate(dos, axis=0)
            dp = lax.dot_general(dos, vv, (((1,), (1,)), ((), ())), preferred_element_type=F32)
            dsum = jnp.sum(p * dp, axis=-1, keepdims=True) * inv
            ds = (p * ((dp - dsum) * inv)).astype(BF16)
            snk_term = p_snk * inv * dsum
            for g in range(GRP):
                contrib = -jnp.sum(snk_term[g * BLK:(g + 1) * BLK], axis=0, keepdims=True)
                ds_ref[GRP * kh + g:GRP * kh + g + 1, :] += jnp.broadcast_to(contrib, (1, LANES))
            dqs = jnp.dot(ds, kk, preferred_element_type=F32) * scale
            for half in range(2):
                even = dqs[(2 * half) * BLK:(2 * half + 1) * BLK]
                odd = dqs[(2 * half + 1) * BLK:(2 * half + 2) * BLK]
                ch = 2 * kh + half
                dq_ref[:, ch * LANES:(ch + 1) * LANES] = jnp.where(low, even, odd)
            dkk = lax.dot_general(ds, qs, (((0,), (0,)), ((), ())), preferred_element_type=F32)
            dvv = lax.dot_general((p * inv).astype(BF16), dos, (((0,), (0,)), ((), ())), preferred_element_type=F32)
            dk_heads.append(dkk + pltpu.roll(dkk, HEAD_DIM, 1))
            dv_heads.append(dvv + pltpu.roll(dvv, HEAD_DIM, 1))
        for pair in range(N_KV // 2):
            sl = slice(pair * LANES, (pair + 1) * LANES)
            for acc, heads in ((dk_acc, dk_heads), (dv_acc, dv_heads)):
                both = jnp.where(low_keys, heads[2 * pair], heads[2 * pair + 1])
                acc[pl.ds(win_start, 3 * BLK), sl] += both[:3 * BLK]
                acc[n_lat:n_lat + n_ctx, sl] += both[3 * BLK:]

        @pl.when(j == n_steps - 1)
        def _():
            ck = pltpu.make_async_copy(dk_acc, dk_hbm, sem.at[0])
            cv = pltpu.make_async_copy(dv_acc, dv_hbm, sem.at[1])
            ck.start()
            cv.start()
            ck.wait()
            cv.wait()

    outs = pl.pallas_call(
        body, name=name, grid=(n_steps,),
        in_specs=[_row_spec(BLK, D)] + win + [ctx] + win + [ctx] + [_full_spec((N_HEADS, LANES)), _row_spec(BLK, D)]
        + [ANY] * len(hosted.arrays),
        out_specs=[_row_spec(BLK, D), ANY, ANY, _full_spec((N_HEADS, LANES))] + [ANY] * len(hosted.out_shapes),
        out_shape=[jax.ShapeDtypeStruct((t_rows, D), F32), jax.ShapeDtypeStruct((t_rows, KV_W), F32),
                   jax.ShapeDtypeStruct((t_rows, KV_W), F32), jax.ShapeDtypeStruct((N_HEADS, LANES), F32)]
        + list(hosted.out_shapes),
        scratch_shapes=[pltpu.VMEM((t_rows, KV_W), F32), pltpu.VMEM((t_rows, KV_W), F32),
                        pltpu.SemaphoreType.DMA((2,))] + list(hosted.scratch),
        compiler_params=_params(("arbitrary",)),
    )(q, kd, kd, kd, kd, vd, vd, vd, vd, sink_b, dy, *hosted.arrays)
    return outs[0], outs[1], outs[2], outs[3], outs[4:]


_GELU_K = math.sqrt(2.0 / math.pi)


def _gelu(v):
    return jax.nn.gelu(v)


def _gelu_and_grad(v):
    t = jnp.tanh(_GELU_K * (v + 0.044715 * (v * v * v)))
    cdf = 0.5 * (1.0 + t)
    return v * cdf, cdf + 0.5 * v * (1.0 - t * t) * _GELU_K * (1.0 + 3.0 * 0.044715 * v * v)


def _gating_fwd(z_a, w_s, b_sb, g_v, *, name, hosted=None):
    hosted = hosted or _NO_EXCHANGE
    t_rows = z_a.shape[0]
    n_steps = t_rows // BLK

    def body(*refs):
        ins, outs, _, h_refs = _split_refs(refs, 4, 1, 0, hosted)
        z_ref, w_ref, b_ref, g_ref = ins
        o_ref, = outs
        _run_hosted(hosted, h_refs, pl.program_id(0), n_steps)
        u = _gelu(z_ref[:, :D])
        v = _gelu(z_ref[:, D:])
        vn = v * lax.rsqrt(jnp.mean(v * v, axis=-1, keepdims=True) + EPS) * g_ref[...]
        for g in range(A_GROUPS):
            sl = slice(g * LANES, (g + 1) * LANES)
            mixed = jnp.dot(w_ref[g].astype(BF16), vn[:, sl].astype(BF16), preferred_element_type=F32) + b_ref[g]
            o_ref[:, sl] = (u[:, sl] * mixed).astype(BF16)

    outs = pl.pallas_call(
        body, name=name, grid=(n_steps,),
        in_specs=[_row_spec(BLK, A_COLS), _full_spec((A_GROUPS, BLK, BLK)), _full_spec((A_GROUPS, BLK, LANES)),
                  _full_spec((1, D))] + [ANY] * len(hosted.arrays),
        out_specs=[_row_spec(BLK, D)] + [ANY] * len(hosted.out_shapes),
        out_shape=[jax.ShapeDtypeStruct((t_rows, D), BF16)] + list(hosted.out_shapes),
        scratch_shapes=list(hosted.scratch),
        compiler_params=_params(("arbitrary",)),
    )(z_a, w_s, b_sb, g_v, *hosted.arrays)
    return outs[0], outs[1:]


def _gating_bwd(z_a, dy, w_s, b_sb, g_v, *, name, hosted=None):
    hosted = hosted or _NO_EXCHANGE
    t_rows = z_a.shape[0]
    n_steps = t_rows // BLK

    def body(*refs):
        ins, outs, _, h_refs = _split_refs(refs, 5, 4, 0, hosted)
        z_ref, dy_ref, w_ref, b_ref, g_ref = ins
        dz_ref, dw_ref, db_ref, st_ref = outs
        i = pl.program_id(0)
        _run_hosted(hosted, h_refs, i, n_steps)

        @pl.when(i == 0)
        def _():
            dw_ref[...] = jnp.zeros_like(dw_ref)
            db_ref[...] = jnp.zeros_like(db_ref)
            st_ref[...] = jnp.zeros_like(st_ref)

        u, du_dz = _gelu_and_grad(z_ref[:, :D])
        v, dv_dz = _gelu_and_grad(z_ref[:, D:])
        gv = g_ref[...]
        rstd = lax.rsqrt(jnp.mean(v * v, axis=-1, keepdims=True) + EPS)
        vh = v * rstd
        vn = vh * gv
        dyv = dy_ref[...]
        dvn = []
        for g in range(A_GROUPS):
            sl = slice(g * LANES, (g + 1) * LANES)
            wg = w_ref[g].astype(BF16)
            vg = vn[:, sl].astype(BF16)
            mixed = jnp.dot(wg, vg, preferred_element_type=F32) + b_ref[g]
            dz_ref[:, sl] = (dyv[:, sl] * mixed * du_dz[:, sl]).astype(BF16)
            dmixed = dyv[:, sl] * u[:, sl]
            dmb = dmixed.astype(BF16)
            dvn.append(lax.dot_general(wg, dmb, (((0,), (0,)), ((), ())), preferred_element_type=F32))
            dw_ref[g] += lax.dot_general(dmb, vg, (((1,), (1,)), ((), ())), preferred_element_type=F32)
            db_ref[g] += jnp.broadcast_to(jnp.sum(dmixed, axis=-1, keepdims=True), (BLK, LANES))
        dvn = jnp.concatenate(dvn, axis=1)
        st_ref[...] += _rows8([(0, _colsum(dvn * vh))], D)
        e = dvn * gv
        dv = rstd * (e - vh * jnp.mean(e * vh, axis=-1, keepdims=True))
        dz_ref[:, D:] = (dv * dv_dz).astype(BF16)

    outs = pl.pallas_call(
        body, name=name, grid=(n_steps,),
        in_specs=[_row_spec(BLK, A_COLS), _row_spec(BLK, D), _full_spec((A_GROUPS, BLK, BLK)),
                  _full_spec((A_GROUPS, BLK, LANES)), _full_spec((1, D))] + [ANY] * len(hosted.arrays),
        out_specs=[_row_spec(BLK, A_COLS), _full_spec((A_GROUPS, BLK, BLK)), _full_spec((A_GROUPS, BLK, LANES)),
                   _full_spec((SUBLANES, D))] + [ANY] * len(hosted.out_shapes),
        out_shape=[jax.ShapeDtypeStruct((t_rows, A_COLS), BF16), jax.ShapeDtypeStruct((A_GROUPS, BLK, BLK), F32),
                   jax.ShapeDtypeStruct((A_GROUPS, BLK, LANES), F32), jax.ShapeDtypeStruct((SUBLANES, D), F32)]
        + list(hosted.out_shapes),
        scratch_shapes=list(hosted.scratch),
        compiler_params=_params(("arbitrary",)),
    )(z_a, dy, w_s, b_sb, g_v, *hosted.arrays)
    return outs[0], outs[1], outs[2], outs[3], outs[4:]


def _sconv_fwd(z_b, w8, n_lat, *, name):
    t_rows = z_b.shape[0]
    tm = 256
    prev, nxt = _halo_specs(tm, B_COLS, t_rows)

    def body(z_ref, zp_ref, zn_ref, w_ref, o_ref):
        p = z_ref[:, D:2 * D] * z_ref[:, 2 * D:]
        pp = zp_ref[:, D:2 * D] * zp_ref[:, 2 * D:]
        pn = zn_ref[:, D:2 * D] * zn_ref[:, 2 * D:]
        up, dn = _shift_rows(p, pp, pn, n_lat, t_rows, tm)
        conv = w_ref[0:1, :] * up + w_ref[1:2, :] * p + w_ref[2:3, :] * dn
        o_ref[...] = (z_ref[:, :D] * conv).astype(BF16)

    return pl.pallas_call(
        body, name=name, grid=(t_rows // tm,),
        in_specs=[_row_spec(tm, B_COLS), prev, nxt, _full_spec((SUBLANES, D))],
        out_specs=_row_spec(tm, D), out_shape=jax.ShapeDtypeStruct((t_rows, D), BF16),
        compiler_params=_params(("parallel",)),
    )(z_b, z_b, z_b, w8)


def _sconv_bwd(z_b, dy, w8, n_lat, *, name):
    t_rows = z_b.shape[0]
    tm = 256
    prev, nxt = _halo_specs(tm, B_COLS, t_rows)
    dprev, dnxt = _halo_specs(tm, D, t_rows)

    def body(z_ref, zp_ref, zn_ref, dy_ref, dyp_ref, dyn_ref, w_ref, dz_ref, st_ref):
        i = pl.program_id(0)
        bg, cg, hb = z_ref[:, :D], z_ref[:, D:2 * D], z_ref[:, 2 * D:]
        p = cg * hb
        pp = zp_ref[:, D:2 * D] * zp_ref[:, 2 * D:]
        pn = zn_ref[:, D:2 * D] * zn_ref[:, 2 * D:]
        up, dn = _shift_rows(p, pp, pn, n_lat, t_rows, tm)
        w0, w1, w2 = w_ref[0:1, :], w_ref[1:2, :], w_ref[2:3, :]
        conv = w0 * up + w1 * p + w2 * dn
        dyv = dy_ref[...]
        dz_ref[:, :D] = (dyv * conv).astype(BF16)
        dcv = dyv * bg
        dcv_up, dcv_dn = _shift_rows(dcv, dyp_ref[...] * zp_ref[:, :D], dyn_ref[...] * zn_ref[:, :D], n_lat, t_rows, tm)
        dp = w0 * dcv_dn + w1 * dcv + w2 * dcv_up
        dz_ref[:, D:2 * D] = (dp * hb).astype(BF16)
        dz_ref[:, 2 * D:] = (dp * cg).astype(BF16)
        upd = _rows8([(0, _colsum(dcv * up)), (1, _colsum(dcv * p)), (2, _colsum(dcv * dn))], D)

        @pl.when(i == 0)
        def _():
            st_ref[...] = upd

        @pl.when(i > 0)
        def _():
            st_ref[...] += upd

    return pl.pallas_call(
        body, name=name, grid=(t_rows // tm,),
        in_specs=[_row_spec(tm, B_COLS), prev, nxt, _row_spec(tm, D), dprev, dnxt, _full_spec((SUBLANES, D))],
        out_specs=[_row_spec(tm, B_COLS), _full_spec((SUBLANES, D))],
        out_shape=[jax.ShapeDtypeStruct((t_rows, B_COLS), BF16), jax.ShapeDtypeStruct((SUBLANES, D), F32)],
        compiler_params=_params(("arbitrary",)),
    )(z_b, z_b, z_b, dy, dy, dy, w8)


def _branch_merge_fwd(ys, w_branch, z_g, b_gate, *, name):
    t_rows = z_g.shape[0]
    tm = _pick(t_rows, (768, 512, 256))

    def body(y0_ref, y1_ref, y2_ref, w_ref, z_ref, b_ref, t_ref, o_ref, acc_ref):
        k = pl.program_id(1)
        for which, y_ref in enumerate((y0_ref, y1_ref, y2_ref)):
            @pl.when(k == which)
            def _():
                t_ref[...] = jnp.dot(y_ref[...], w_ref[...], preferred_element_type=F32)

        term = jax.nn.sigmoid(z_ref[...] + b_ref[...]) * t_ref[...]

        @pl.when(k == 0)
        def _():
            acc_ref[...] = term

        @pl.when(k > 0)
        def _():
            acc_ref[...] += term

        @pl.when(k == 2)
        def _():
            o_ref[...] = acc_ref[...].astype(BF16)

    y_spec = pl.BlockSpec((tm, D), lambda i, k: (i, 0))
    return pl.pallas_call(
        body, name=name, grid=(t_rows // tm, 3),
        in_specs=[y_spec, y_spec, y_spec, pl.BlockSpec((None, D, D), lambda i, k: (k, 0, 0)),
                  pl.BlockSpec((tm, D), lambda i, k: (i, k)), pl.BlockSpec((None, 1, D), lambda i, k: (k, 0, 0))],
        out_specs=[pl.BlockSpec((None, tm, D), lambda i, k: (k, i, 0)), y_spec],
        out_shape=[jax.ShapeDtypeStruct((3, t_rows, D), F32), jax.ShapeDtypeStruct((t_rows, D), BF16)],
        scratch_shapes=[pltpu.VMEM((tm, D), F32)],
        compiler_params=_params(("parallel", "arbitrary")),
    )(*ys, w_branch, z_g, b_gate.reshape(3, 1, D))


def _merge_bwd(dmerged, t_all, z_g, b_gate, *, name):
    t_rows = dmerged.shape[0]
    tm = 256
    t_specs = [pl.BlockSpec((None, tm, D), functools.partial(lambda i, k: (k, i, 0), k=k)) for k in range(3)]

    def body(dm_ref, t0_ref, t1_ref, t2_ref, z_ref, b_ref, d0_ref, d1_ref, d2_ref, dz_ref, st_ref):
        i = pl.program_id(0)
        dm = dm_ref[...]
        sums = []
        for k, (t_ref, d_ref) in enumerate(((t0_ref, d0_ref), (t1_ref, d1_ref), (t2_ref, d2_ref))):
            gate = jax.nn.sigmoid(z_ref[:, k * D:(k + 1) * D] + b_ref[:, k * D:(k + 1) * D])
            d_ref[...] = (dm * gate).astype(BF16)
            dzg = dm * t_ref[...] * gate * (1.0 - gate)
            dz_ref[:, k * D:(k + 1) * D] = dzg.astype(BF16)
            sums.append(_colsum(dzg))
        upd = _rows8([(0, jnp.concatenate(sums, axis=1))], G_COLS)

        @pl.when(i == 0)
        def _():
            st_ref[...] = upd

        @pl.when(i > 0)
        def _():
            st_ref[...] += upd

    return pl.pallas_call(
        body, name=name, grid=(t_rows // tm,),
        in_specs=[_row_spec(tm, D)] + t_specs + [_row_spec(tm, G_COLS), _full_spec((1, G_COLS))],
        out_specs=[_row_spec(tm, D)] * 3 + [_row_spec(tm, G_COLS), _full_spec((SUBLANES, G_COLS))],
        out_shape=[jax.ShapeDtypeStruct((t_rows, D), BF16)] * 3
        + [jax.ShapeDtypeStruct((t_rows, G_COLS), BF16), jax.ShapeDtypeStruct((SUBLANES, G_COLS), F32)],
        compiler_params=_params(("arbitrary",)),
    )(dmerged, t_all, t_all, t_all, z_g, b_gate)


def _ffn_mid_fwd(up, w8, n_lat, *, name, hosted=None):
    hosted = hosted or _NO_EXCHANGE
    t_rows = up.shape[0]
    tm = 256
    n_steps = t_rows // tm
    prev, nxt = _halo_specs(tm, D_FF, t_rows)

    def body(*refs):
        ins, outs, _, h_refs = _split_refs(refs, 5, 2, 0, hosted)
        a_ref, ap_ref, an_ref, g_ref, w_ref = ins
        cv_ref, f_ref = outs
        _run_hosted(hosted, h_refs, pl.program_id(0), n_steps)
        a = a_ref[...]
        au, ad = _shift_rows(a, ap_ref[...], an_ref[...], n_lat, t_rows, tm)
        cv = w_ref[0:1, :] * au + w_ref[1:2, :] * a + w_ref[2:3, :] * ad
        cv_ref[...] = cv
        f_ref[...] = (_silu(cv) * g_ref[...]).astype(BF16)

    outs = pl.pallas_call(
        body, name=name, grid=(n_steps,),
        in_specs=[_row_spec(tm, D_FF), prev, nxt, _row_spec(tm, D_FF, 1), _full_spec((SUBLANES, D_FF))]
        + [ANY] * len(hosted.arrays),
        out_specs=[_row_spec(tm, D_FF), _row_spec(tm, D_FF)] + [ANY] * len(hosted.out_shapes),
        out_shape=[jax.ShapeDtypeStruct((t_rows, D_FF), F32), jax.ShapeDtypeStruct((t_rows, D_FF), BF16)]
        + list(hosted.out_shapes),
        scratch_shapes=list(hosted.scratch),
        compiler_params=_params(("arbitrary",)),
    )(up, up, up, up, w8, *hosted.arrays)
    return outs[0], outs[1], outs[2:]


def _ffn_mid_bwd(up, cv, df, w8, n_lat, *, name):
    t_rows = up.shape[0]
    tm = 256
    prev, nxt = _halo_specs(tm, D_FF, t_rows)
    gprev, gnxt = _halo_specs(tm, D_FF, t_rows, 1)

    def body(a_ref, ap_ref, an_ref, g_ref, gp_ref, gn_ref, cv_ref, cp_ref, cn_ref, df_ref, dfp_ref, dfn_ref,
             w_ref, o_ref, st_ref):
        i = pl.program_id(0)
        a = a_ref[...]
        au, ad = _shift_rows(a, ap_ref[...], an_ref[...], n_lat, t_rows, tm)
        cvv = cv_ref[...]
        dfv = df_ref[...]
        sig = jax.nn.sigmoid(cvv)
        o_ref[:, D_FF:] = (dfv * (cvv * sig)).astype(BF16)
        dcv = dfv * g_ref[...] * (sig * (1.0 + cvv * (1.0 - sig)))
        dcv_p = dfp_ref[...] * gp_ref[...] * _dsilu(cp_ref[...])
        dcv_n = dfn_ref[...] * gn_ref[...] * _dsilu(cn_ref[...])
        du, dd = _shift_rows(dcv, dcv_p, dcv_n, n_lat, t_rows, tm)
        o_ref[:, :D_FF] = (w_ref[0:1, :] * dd + w_ref[1:2, :] * dcv + w_ref[2:3, :] * du).astype(BF16)
        upd = _rows8([(0, _colsum(dcv * au)), (1, _colsum(dcv * a)), (2, _colsum(dcv * ad))], D_FF)

        @pl.when(i == 0)
        def _():
            st_ref[...] = upd

        @pl.when(i > 0)
        def _():
            st_ref[...] += upd

    row = _row_spec(tm, D_FF)
    return pl.pallas_call(
        body, name=name, grid=(t_rows // tm,),
        in_specs=[row, prev, nxt, _row_spec(tm, D_FF, 1), gprev, gnxt, row, prev, nxt, row, prev, nxt,
                  _full_spec((SUBLANES, D_FF))],
        out_specs=[_row_spec(tm, 2 * D_FF), _full_spec((SUBLANES, D_FF))],
        out_shape=[jax.ShapeDtypeStruct((t_rows, 2 * D_FF), BF16), jax.ShapeDtypeStruct((SUBLANES, D_FF), F32)],
        compiler_params=_params(("arbitrary",), VMEM_LIMIT_WIDE),
    )(up, up, up, up, up, up, cv, cv, cv, df, df, df, w8)


def _loss_head(x, target, g_final, n_lat, *, name):
    t_rows = x.shape[0]
    tm = 256
    last = n_lat // tm - 1

    def body(x_ref, t_ref, g_ref, dx_ref, st_ref):
        i = pl.program_id(0)
        is_ctx = i * tm >= n_lat
        xv = x_ref[...]
        gv = g_ref[...]
        rstd = lax.rsqrt(jnp.mean(xv * xv, axis=-1, keepdims=True) + EPS)
        rn = xv * rstd
        err = rn * gv - t_ref[...]
        dy = err / D
        e = dy * gv
        dx = rstd * (e - rn * jnp.mean(e * rn, axis=-1, keepdims=True))
        dx_ref[...] = jnp.where(is_ctx, 0.0, dx)
        keep = jnp.where(is_ctx, 0.0, 1.0)
        upd = _rows8([(0, keep * _colsum(dy * rn)), (1, keep * _colsum(err * err))], D)

        @pl.when(i == 0)
        def _():
            st_ref[...] = upd

        @pl.when(i > 0)
        def _():
            st_ref[...] += upd

    return pl.pallas_call(
        body, name=name, grid=(t_rows // tm,),
        in_specs=[_row_spec(tm, D), pl.BlockSpec((tm, D), lambda i: (jnp.minimum(i, last), 0)), _full_spec((1, D))],
        out_specs=[_row_spec(tm, D), _full_spec((SUBLANES, D))],
        out_shape=[jax.ShapeDtypeStruct((t_rows, D), F32), jax.ShapeDtypeStruct((SUBLANES, D), F32)],
        compiler_params=_params(("arbitrary",)),
    )(x, target, g_final)


def _sum_slabs(x, out_dtype, *, name):
    n_slabs, rows, cols = x.shape
    tm = _pick(rows, (432, 256, 192, 128, 64, 32, 24, 16, 8))

    def body(x_ref, o_ref):
        acc = x_ref[0].astype(F32)
        for s in range(1, n_slabs):
            acc = acc + x_ref[s].astype(F32)
        o_ref[...] = acc.astype(o_ref.dtype)

    return pl.pallas_call(
        body, name=name, grid=(rows // tm,),
        in_specs=[pl.BlockSpec((n_slabs, tm, cols), lambda i: (0, i, 0))],
        out_specs=pl.BlockSpec((tm, cols), lambda i: (i, 0)),
        out_shape=jax.ShapeDtypeStruct((rows, cols), out_dtype),
        compiler_params=_params(("parallel",)),
    )(x)


def _add_half(half_idx, a, b, *, name):
    n_slabs, rows, cols = b.shape
    tm = _pick(rows, (432, 256, 192, 128, 96, 64, 32, 16))
    per_half = rows // tm

    def body(half_ref, a_ref, b_ref, o_ref):
        o_ref[...] = (a_ref[...].astype(F32) + b_ref[...].astype(F32)).astype(BF16)

    spec = pl.BlockSpec((1, tm, cols), lambda s, i, half_ref: (s, i, 0))
    a_spec = pl.BlockSpec((1, tm, cols), lambda s, i, half_ref: (s, half_ref[0] * per_half + i, 0))
    return pl.pallas_call(
        body, name=name,
        grid_spec=pltpu.PrefetchScalarGridSpec(num_scalar_prefetch=1, grid=(n_slabs, per_half),
                                               in_specs=[a_spec, spec], out_specs=spec),
        out_shape=jax.ShapeDtypeStruct(b.shape, BF16), compiler_params=_params(("parallel", "parallel")),
    )(half_idx, a, b)


def _adamw(w, g, m, v, *, name, hosted=None):
    hosted = hosted or _NO_EXCHANGE
    rows, cols = w.shape
    tm = _pick(rows, (256, 128, 64, 32, 16, 8))
    n_steps = rows // tm

    def body(*refs):
        ins, outs, _, h_refs = _split_refs(refs, 4, 3, 0, hosted)
        w_ref, g_ref, m_ref, v_ref = ins
        d_ref, nm_ref, nv_ref = outs
        _run_hosted(hosted, h_refs, pl.program_id(0), n_steps)
        gv = g_ref[...]
        nm = ADAM_B1 * m_ref[...] + (1.0 - ADAM_B1) * gv
        nv = ADAM_B2 * v_ref[...] + (1.0 - ADAM_B2) * jnp.square(gv)
        m_hat = nm / (1.0 - ADAM_B1 ** ADAM_STEP)
        v_hat = nv / (1.0 - ADAM_B2 ** ADAM_STEP)
        d_ref[...] = -ADAM_LR * (m_hat / (jnp.sqrt(v_hat) + ADAM_EPS) + ADAM_WD * w_ref[...])
        nm_ref[...] = nm
        nv_ref[...] = nv

    spec = pl.BlockSpec((tm, cols), lambda i: (i, 0))
    shape = jax.ShapeDtypeStruct((rows, cols), F32)
    outs = pl.pallas_call(
        body, name=name, grid=(n_steps,), in_specs=[spec] * 4 + [ANY] * len(hosted.arrays),
        out_specs=[spec] * 3 + [ANY] * len(hosted.out_shapes), out_shape=[shape] * 3 + list(hosted.out_shapes),
        scratch_shapes=list(hosted.scratch), compiler_params=_params(("arbitrary",)),
    )(w, g, m, v, *hosted.arrays)
    return outs[0], outs[1], outs[2], outs[3:]


def _place():
    x, y, c = lax.axis_index("x"), lax.axis_index("y"), lax.axis_index("c")
    chips = [(1 - x, y), (x, 1 - y), (1 - x, 1 - y)]
    return x, y, c, chips


def _remote(src, dst, send_sems, recv_sems, k, to):
    return pltpu.make_async_remote_copy(src_ref=src, dst_ref=dst, send_sem=send_sems.at[k], recv_sem=recv_sems.at[k],
                                        device_id=to, device_id_type=MESH)


HALF_CHUNKS = 2


def _chunks(ref, n):
    step = ref.shape[0] // n
    tile_rows = SUBLANES if ref.dtype == F32 else 2 * SUBLANES
    assert step * n == ref.shape[0] and step % tile_rows == 0, (ref.shape, n)
    return [ref.at[pl.ds(k * step, step)] for k in range(n)]


def _half(ref, which):
    half = ref.shape[0] // 2
    return ref.at[pl.ds(pl.multiple_of(which * half, 2 * SUBLANES), half)]


def _staged_copy(src, dst, buf, sems):
    step = buf.shape[1]
    n = src.shape[0] // step
    assert n * step == src.shape[0], (src.shape, step)
    ins = [pltpu.make_async_copy(src.at[pl.ds(k * step, step)], buf.at[k % 2], sems.at[k % 2]) for k in range(n)]
    outs = [pltpu.make_async_copy(buf.at[k % 2], dst.at[pl.ds(k * step, step)], sems.at[2 + k % 2]) for k in range(n)]
    ins[0].start()
    for k in range(n):
        ins[k].wait()
        outs[k].start()
        if k + 1 < n:
            if k >= 1:
                outs[k - 1].wait()
            ins[k + 1].start()
    if n >= 2:
        outs[n - 2].wait()
    outs[n - 1].wait()


def _stage_rows(rows):
    return _pick(rows, (256, 432))


def _stage_scratch(slabs):
    return [pltpu.VMEM((2, _stage_rows(s.shape[-2]), s.shape[-1]), s.dtype) for s in slabs] + [pltpu.SemaphoreType.DMA((4,))]


N_LINK_SEMS = (N_CHIPS - 1) * HALF_CHUNKS


def _link_sems(n_groups):
    return [pltpu.SemaphoreType.DMA((n_groups * N_LINK_SEMS,)), pltpu.SemaphoreType.DMA((n_groups * N_LINK_SEMS,))]


def _sem_index(g, j, k):
    return g * N_LINK_SEMS + j * HALF_CHUNKS + k


def _gather_ici_start(*refs):
    n = (len(refs) - 2) // 2
    p_refs, o_refs, (send_sems, recv_sems) = refs[:n], refs[n:2 * n], refs[2 * n:]
    x, y, c, chips = _place()
    for g, (p_ref, o_ref) in enumerate(zip(p_refs, o_refs)):
        src = _chunks(_half(p_ref, c), HALF_CHUNKS)
        dst = _chunks(_half(o_ref.at[2 * x + y], c), HALF_CHUNKS)
        for j, chip in enumerate(chips):
            for k in range(HALF_CHUNKS):
                _remote(src[k], dst[k], send_sems, recv_sems, _sem_index(g, j, k), (*chip, c)).start()


def _gather_ici_finish(*refs):
    n = (len(refs) - 2) // 2
    p_refs, o_refs, (send_sems, recv_sems) = refs[:n], refs[n:2 * n], refs[2 * n:]
    x, y, c, chips = _place()
    for g, (p_ref, o_ref) in enumerate(zip(p_refs, o_refs)):
        src = _chunks(_half(p_ref, c), HALF_CHUNKS)
        for j, (cx, cy) in enumerate(chips):
            for k, landed in enumerate(_chunks(_half(o_ref.at[2 * cx + cy], c), HALF_CHUNKS)):
                _remote(src[k], landed, send_sems, recv_sems, _sem_index(g, j, k), (x, y, c)).wait_recv()
        for j in range(len(chips)):
            for k in range(HALF_CHUNKS):
                _remote(src[k], src[k], send_sems, recv_sems, _sem_index(g, j, k), (x, y, c)).wait_send()


def _gathered_shapes(slabs):
    return [jax.ShapeDtypeStruct((N_CHIPS,) + s.shape, s.dtype) for s in slabs]


def _gather_ici_hosted(slabs):
    return _Hosted(list(slabs), _gathered_shapes(slabs), _link_sems(len(slabs)), _gather_ici_start, _gather_ici_finish)


def _gather_ici(slabs, *, name):
    def body(*refs):
        _gather_ici_start(*refs)
        _gather_ici_finish(*refs)

    return pl.pallas_call(
        body, name=name, in_specs=[ANY] * len(slabs), out_specs=[ANY] * len(slabs),
        out_shape=_gathered_shapes(slabs), scratch_shapes=_link_sems(len(slabs)),
    )(*slabs)


def _gather_finish(partials, slabs, *, name):
    n = len(slabs)

    def body(*refs):
        p_refs, o_refs = refs[n:2 * n], refs[2 * n:3 * n]
        send_sems, recv_sems = refs[3 * n:3 * n + 2]
        bufs, loc_sems = refs[3 * n + 2:4 * n + 2], refs[4 * n + 2]
        x, y, c, chips = _place()
        sib = (x, y, 1 - c)
        passed = []
        for g, o_ref in enumerate(o_refs):
            for j, (cx, cy) in enumerate(chips):
                for k, landed in enumerate(_chunks(_half(o_ref.at[2 * cx + cy], c), HALF_CHUNKS)):
                    passed.append(_remote(landed, landed, send_sems, recv_sems, _sem_index(g, j, k), sib))
        for cp in passed:
            cp.start()
        for p_ref, o_ref, buf in zip(p_refs, o_refs, bufs):
            _staged_copy(p_ref, o_ref.at[2 * x + y], buf, loc_sems)
        for g, o_ref in enumerate(o_refs):
            for j, (cx, cy) in enumerate(chips):
                for k, landed in enumerate(_chunks(_half(o_ref.at[2 * cx + cy], 1 - c), HALF_CHUNKS)):
                    _remote(landed, landed, send_sems, recv_sems, _sem_index(g, j, k), sib).wait_recv()
        for cp in passed:
            cp.wait_send()

    return pl.pallas_call(
        body, name=name, in_specs=[ANY] * (2 * n), out_specs=[ANY] * n,
        out_shape=[jax.ShapeDtypeStruct(p.shape, p.dtype) for p in partials],
        input_output_aliases={g: g for g in range(n)}, scratch_shapes=_link_sems(n) + _stage_scratch(slabs),
        compiler_params=_params(),
    )(*partials, *slabs)


def _grad_sibling_swap(g_packs, *, name):
    n = len(g_packs)
    per_group = N_CHIPS * HALF_CHUNKS

    def body(*refs):
        g_refs, got_refs, (send_sems, recv_sems) = refs[:n], refs[n:2 * n], refs[2 * n:]
        x, y, c, _ = _place()
        sib = (x, y, 1 - c)
        swaps = [_remote(src, dst, send_sems, recv_sems, g * per_group + s * HALF_CHUNKS + k, sib)
                 for g, (g_ref, got_ref) in enumerate(zip(g_refs, got_refs))
                 for s in range(N_CHIPS)
                 for k, (src, dst) in enumerate(zip(_chunks(_half(g_ref.at[s], 1 - c), HALF_CHUNKS),
                                                    _chunks(got_ref.at[s], HALF_CHUNKS)))]
        for cp in swaps:
            cp.start()
        for cp in swaps:
            cp.wait_recv()
        for cp in swaps:
            cp.wait_send()

    return pl.pallas_call(
        body, name=name, in_specs=[ANY] * n, out_specs=[ANY] * n,
        out_shape=[jax.ShapeDtypeStruct((N_CHIPS, g.shape[1] // 2, g.shape[2]), g.dtype) for g in g_packs],
        scratch_shapes=[pltpu.SemaphoreType.DMA((n * per_group,)), pltpu.SemaphoreType.DMA((n * per_group,))],
    )(*g_packs)


def _grad_ici_refs(refs):
    n = (len(refs) - 3) // 3
    return refs[:n], refs[n:2 * n], refs[2 * n], refs[2 * n + 1], refs[2 * n + 2:3 * n + 2], refs[3 * n + 2]


def _grad_ici_start(*refs):
    s_refs, o_refs, send_sems, recv_sems, _, _ = _grad_ici_refs(refs)
    x, y, c, chips = _place()
    for g, (s_ref, o_ref) in enumerate(zip(s_refs, o_refs)):
        for j, (cx, cy) in enumerate(chips):
            pairs = zip(_chunks(s_ref.at[2 * cx + cy], HALF_CHUNKS), _chunks(o_ref.at[2 * x + y], HALF_CHUNKS))
            for k, (src, dst) in enumerate(pairs):
                _remote(src, dst, send_sems, recv_sems, _sem_index(g, j, k), (cx, cy, c)).start()


def _grad_ici_finish(*refs):
    s_refs, o_refs, send_sems, recv_sems, bufs, loc_sems = _grad_ici_refs(refs)
    x, y, c, chips = _place()
    me = 2 * x + y
    for s_ref, o_ref, buf in zip(s_refs, o_refs, bufs):
        _staged_copy(s_ref.at[me], o_ref.at[me], buf, loc_sems)
    for g, (s_ref, o_ref) in enumerate(zip(s_refs, o_refs)):
        for j, (cx, cy) in enumerate(chips):
            for k, landed in enumerate(_chunks(o_ref.at[2 * cx + cy], HALF_CHUNKS)):
                _remote(landed, landed, send_sems, recv_sems, _sem_index(g, j, k), (x, y, c)).wait_recv()
        for j, (cx, cy) in enumerate(chips):
            for k, sent in enumerate(_chunks(s_ref.at[2 * cx + cy], HALF_CHUNKS)):
                _remote(sent, sent, send_sems, recv_sems, _sem_index(g, j, k), (x, y, c)).wait_send()


def _grad_ici_hosted(sums):
    return _Hosted(list(sums), [jax.ShapeDtypeStruct(s.shape, s.dtype) for s in sums],
                   _link_sems(len(sums)) + _stage_scratch(sums), _grad_ici_start, _grad_ici_finish)


def _grad_ici(sums, *, name):
    def body(*refs):
        _grad_ici_start(*refs)
        _grad_ici_finish(*refs)

    n = len(sums)
    return pl.pallas_call(
        body, name=name, in_specs=[ANY] * n, out_specs=[ANY] * n,
        out_shape=[jax.ShapeDtypeStruct(s.shape, s.dtype) for s in sums],
        scratch_shapes=_link_sems(n) + _stage_scratch(sums), compiler_params=_params(),
    )(*sums)


def _grad_sibling_share(totals, *, name):
    n = len(totals)
    n_ch = HALF_CHUNKS

    def body(*refs):
        t_refs, o_refs = refs[:n], refs[n:2 * n]
        send_sems, recv_sems = refs[2 * n:2 * n + 2]
        bufs, loc_sems = refs[2 * n + 2:3 * n + 2], refs[3 * n + 2]
        x, y, c, _ = _place()
        sib = (x, y, 1 - c)
        sends = [_remote(src, dst, send_sems, recv_sems, g * n_ch + k, sib)
                 for g, (t_ref, o_ref) in enumerate(zip(t_refs, o_refs))
                 for k, (src, dst) in enumerate(zip(_chunks(t_ref, n_ch), _chunks(_half(o_ref, c), n_ch)))]
        for cp in sends:
            cp.start()
        for t_ref, o_ref, buf in zip(t_refs, o_refs, bufs):
            _staged_copy(t_ref, _half(o_ref, c), buf, loc_sems)
        for g, o_ref in enumerate(o_refs):
            for k, landed in enumerate(_chunks(_half(o_ref, 1 - c), n_ch)):
                _remote(landed, landed, send_sems, recv_sems, g * n_ch + k, sib).wait_recv()
        for cp in sends:
            cp.wait_send()

    return pl.pallas_call(
        body, name=name, in_specs=[ANY] * n, out_specs=[ANY] * n,
        out_shape=[jax.ShapeDtypeStruct((2 * t.shape[0], t.shape[1]), t.dtype) for t in totals],
        scratch_shapes=[pltpu.SemaphoreType.DMA((n * n_ch,)), pltpu.SemaphoreType.DMA((n * n_ch,))] + _stage_scratch(totals),
        compiler_params=_params(),
    )(*totals)


def _allgather8(v, *, name):
    rows, cols = v.shape

    def body(v_ref, o_ref, send_sems, recv_sems, loc_sem):
        x, y, c, chips = _place()
        sib = (x, y, 1 - c)

        def slot(px, py, pc):
            return o_ref.at[4 * px + 2 * py + pc]

        local = pltpu.make_async_copy(v_ref, slot(x, y, c), loc_sem.at[0])
        local.start()
        first = [_remote(v_ref, slot(x, y, c), send_sems, recv_sems, 0, sib)]
        first += [_remote(v_ref, slot(x, y, c), send_sems, recv_sems, 1 + j, (*chip, c)) for j, chip in enumerate(chips)]
        for cp in first:
            cp.start()
        passed = [_remote(slot(*chip, c), slot(*chip, c), send_sems, recv_sems, 4 + j, sib)
                  for j, chip in enumerate(chips)]
        for j, chip in enumerate(chips):
            _remote(v_ref, slot(*chip, c), send_sems, recv_sems, 1 + j, sib).wait_recv()
            passed[j].start()
        _remote(v_ref, slot(x, y, 1 - c), send_sems, recv_sems, 0, sib).wait_recv()
        for j, chip in enumerate(chips):
            _remote(v_ref, slot(*chip, 1 - c), send_sems, recv_sems, 4 + j, sib).wait_recv()
        for cp in first + passed:
            cp.wait_send()
        local.wait()

    return pl.pallas_call(
        body, name=name, in_specs=[ANY], out_specs=ANY, out_shape=jax.ShapeDtypeStruct((N_DEV, rows, cols), v.dtype),
        scratch_shapes=[pltpu.SemaphoreType.DMA((7,)), pltpu.SemaphoreType.DMA((7,)), pltpu.SemaphoreType.DMA((1,))],
    )(v)


_BIG = (("w_mod", (D, 6 * D), 1), ("w_in", (D, IN_W), 1), ("w_branch", (3 * D, D), None), ("w_out", (D, D), 0),
        ("w_up", (D, 2 * D_FF), 1), ("w_down", (D_FF, D), 0))
_COL_SHARDED = ("w_mod", "w_in", "w_up")
_ROW_SHARDED = (("w_branch", 3 * D // N_CHIPS), ("w_out", D // N_CHIPS), ("w_down", D_FF // N_CHIPS))


def _pack_shards(shards, layer):
    rows = jnp.concatenate([shards[n][layer].reshape(r, D) for n, r in _ROW_SHARDED], axis=0)
    return [shards[n][layer] for n in _COL_SHARDED] + [rows]


def _unpack_cols(blk):
    return blk.transpose(1, 0, 2).reshape(blk.shape[1], N_CHIPS * blk.shape[2])


def _unpack_rows(stack):
    out, off = {}, 0
    for name, r in _ROW_SHARDED:
        blk = stack[:, off:off + r, :]
        off += r
        if name == "w_branch":
            out[name] = blk.reshape(N_CHIPS, 3, D // N_CHIPS, D).transpose(1, 0, 2, 3).reshape(3, D, D)
        else:
            out[name] = blk.reshape(N_CHIPS * r, D)
    return out


def _unpack_full(gathered):
    out = {name: _unpack_cols(blk) for name, blk in zip(_COL_SHARDED, gathered)}
    out.update(_unpack_rows(gathered[-1]))
    return out


def _pack_grad_cols(g):
    return g.reshape(g.shape[0], N_CHIPS, g.shape[1] // N_CHIPS).transpose(1, 0, 2)


def _pack_grad_rows(grads):
    parts = []
    for name, r in _ROW_SHARDED:
        g = grads[name]
        if name == "w_branch":
            g = g.reshape(3, N_CHIPS, D // N_CHIPS, D).transpose(1, 0, 2, 3)
        parts.append(g.reshape(N_CHIPS, r, D))
    return jnp.concatenate(parts, axis=1)


def _pack_grads(grads):
    return [_pack_grad_cols(grads[n]) for n in _COL_SHARDED] + [_pack_grad_rows(grads)]


def _unpack_shards(totals, like):
    out = {n: jnp.stack([totals[l][g] for l in range(DEPTH)]) for g, n in enumerate(_COL_SHARDED)}
    off = 0
    for name, r in _ROW_SHARDED:
        out[name] = jnp.stack([totals[l][-1][off:off + r] for l in range(DEPTH)]).reshape(like[name].shape)
        off += r
    return out


def _pad_rows(v, rows):
    return jnp.concatenate([v, jnp.zeros((rows - v.shape[0],) + v.shape[1:], v.dtype)], axis=0)


def _local_step(x_tok, target, c_vec, c_ctx, wfull, small, n_lat, n_ctx):
    ctx = _step_context(c_vec, c_ctx, n_lat, n_ctx)
    saved = []
    xs = x_tok
    for l in range(DEPTH):
        xs, s, _ = _layer_fwd(l, xs, wfull[l], {k: v[l] for k, v in small.items() if k != "g_final"}, ctx)
        saved.append(s)
    dx, sq_err, d_g_final = _loss_bwd(xs, target, small["g_final"], n_lat)
    wgrads, lgrads, d_a128 = [None] * DEPTH, [None] * DEPTH, [None] * DEPTH
    for l in reversed(range(DEPTH)):
        dx, wgrads[l], lgrads[l], d_a128[l], _ = _layer_bwd(l, saved[l], wfull[l], dx, ctx)
    return sq_err, dx, wgrads, _small_grads(lgrads, d_a128, d_g_final, ctx)


def _step_context(c_vec, c_ctx, n_lat, n_ctx):
    cos_t, sin_t = _rope_tables(n_lat, n_ctx)
    a_in = _pad_rows(jnp.stack([c_vec, c_ctx]), LANES)
    a128 = _small(_silu, (LANES, D), a_in, name="cond_silu")
    return dict(cos_t=cos_t, sin_t=sin_t, a_in=a_in, a128=a128, n_lat=n_lat, n_ctx=n_ctx)


def _loss_bwd(xs, target, g_final, n_lat):
    dx, st = _loss_head(xs, target, g_final[None, :], n_lat, name="loss_head")
    return dx, st[1], st[0]


def _small_grads(lgrads, d_a128, d_g_final, ctx):
    d_cond = _small(lambda a, b, cin: (a + b) * _dsilu(cin), (LANES, D), d_a128[0], d_a128[1], ctx["a_in"],
                    name="cond_bwd")
    out = {k: jnp.stack([lgrads[l][k] for l in range(DEPTH)]) for k in lgrads[0]}
    out["c_ctx"] = d_cond[1]
    out["g_final"] = d_g_final
    return out


def _layer_fwd(l, xs, w, sm, ctx, hosted=_NO_EXCHANGE, hosted_gating=_NO_EXCHANGE, hosted_ffn=_NO_EXCHANGE,
               late_weights=None):
    n_lat, n_ctx, cos_t, sin_t, a128 = ctx["n_lat"], ctx["n_ctx"], ctx["cos_t"], ctx["sin_t"], ctx["a128"]
    mod128 = _mm(a128, w["w_mod"], name=f"mod{l}")
    mod8 = _small(lambda m, b: m + b, (SUBLANES, 6 * D), mod128[:SUBLANES], sm["b_mod"][None, :], name=f"mod_bias{l}")
    g_mix = sm["g_mix"][None, :]
    g_ffn = sm["g_ffn"][None, :]
    g_v = sm["g_v"][None, :]
    b_gate = sm["b_gate"][None, :]
    sink_b = jnp.broadcast_to(sm["sink"][:, None], (N_HEADS, LANES))
    b_sb = jnp.broadcast_to(sm["b_spatial"][:, :, None], (A_GROUPS, BLK, LANES))
    w_sconv8 = _pad_rows(sm["w_sconv"], SUBLANES)
    w_fconv8 = _pad_rows(sm["w_fconv"], SUBLANES)
    w_in = w["w_in"]
    w_seg = [w_in[:, SEG[k]:SEG[k + 1]] for k in range(4)]

    h = _norm_mod_fwd(xs, g_mix, mod8, 0, 1, n_lat, name=f"norm1_{l}")
    q, kd, vd = _qkv_proj(h, w_seg[0], cos_t, sin_t, name=f"in_proj_qkv{l}")
    z_a, z_b, z_g = [_mm(h, w_seg[k], name=f"in_proj{k}_{l}") for k in range(1, 4)]
    y_attn, carried = _attention_fwd(q, kd, vd, sink_b, n_lat, n_ctx, name=f"attn{l}", hosted=hosted)
    if late_weights is not None:
        w = dict(w, **late_weights(carried))
    y_a, carried_gating = _gating_fwd(z_a, sm["w_spatial"], b_sb, g_v, name=f"gating{l}", hosted=hosted_gating)
    y_b = _sconv_fwd(z_b, w_sconv8, n_lat, name=f"sconv{l}")
    ys = (y_attn, y_a, y_b)
    ts, merged = _branch_merge_fwd(ys, w["w_branch"], z_g, b_gate, name=f"branch_merge{l}")
    mix_out = _mm(merged, w["w_out"], name=f"out_proj{l}")
    x1, h2 = _residual_fwd(xs, mix_out, mod8, 2, n_lat, name=f"res1_norm2_{l}", norm=(g_ffn, 3, 4))
    up = _mm(h2, w["w_up"], name=f"up_proj{l}")
    cv, f, carried_ffn = _ffn_mid_fwd(up, w_fconv8, n_lat, name=f"ffn_mid{l}", hosted=hosted_ffn)
    ffn_out = _mm(f, w["w_down"], name=f"down_proj{l}")
    x2, _ = _residual_fwd(x1, ffn_out, mod8, 5, n_lat, name=f"res2_{l}")
    saved = dict(x0=xs, mod8=mod8, h=h, z_a=z_a, z_b=z_b, z_g=z_g, q=q, kd=kd, vd=vd, ys=ys, ts=ts,
                 merged=merged, mix_out=mix_out, x1=x1, h2=h2, up=up, cv=cv, f=f, ffn_out=ffn_out, w_seg=w_seg,
                 g_mix=g_mix, g_ffn=g_ffn, g_v=g_v, b_gate=b_gate, sink_b=sink_b, b_sb=b_sb,
                 w_sconv8=w_sconv8, w_fconv8=w_fconv8, w_spatial=sm["w_spatial"])
    return x2, saved, (carried, carried_gating, carried_ffn)


def _layer_bwd(l, s, w, dx, ctx, hosts=None):
    n_lat, n_ctx, cos_t, sin_t, a128 = ctx["n_lat"], ctx["n_ctx"], ctx["cos_t"], ctx["sin_t"], ctx["a128"]
    mod8 = s["mod8"]
    d_ffn, st_gt2 = _residual_bwd(dx, s["ffn_out"], mod8, 5, n_lat, name=f"res2_bwd{l}")
    df = _mm(d_ffn, w["w_down"], tb=True, name=f"down_bwd_x{l}")
    g_down = _mm(s["f"], d_ffn, ta=True, out_dtype=BF16, name=f"down_bwd_w{l}")
    d_up, st_fc = _ffn_mid_bwd(s["up"], s["cv"], df, s["w_fconv8"], n_lat, name=f"ffn_mid_bwd{l}")
    dh2 = _mm(d_up, w["w_up"], tb=True, name=f"up_bwd_x{l}")
    g_up = _mm(s["h2"], d_up, ta=True, out_dtype=BF16, name=f"up_bwd_w{l}")
    dx1, st_n2, _, d_out = _norm_mod_bwd(s["x1"], [dh2], dx, s["g_ffn"], mod8, 4, n_lat, name=f"norm2_res1_bwd{l}",
                                         res=(s["mix_out"], 2))
    st_gt1 = st_n2[5:7]
    d_merged = _mm(d_out, w["w_out"], tb=True, name=f"out_bwd_x{l}")
    g_out = _mm(s["merged"], d_out, ta=True, out_dtype=BF16, name=f"out_bwd_w{l}")
    dt0, dt1, dt2, dz_g, st_bg = _merge_bwd(d_merged, s["ts"], s["z_g"], s["b_gate"], name=f"merge_bwd{l}")
    dts = (dt0, dt1, dt2)
    dys = [_mm(dts[k], w["w_branch"][k], tb=True, name=f"branch{k}_bwd_x{l}") for k in range(3)]
    g_branch = jnp.stack([_mm(s["ys"][k], dts[k], ta=True, out_dtype=BF16, name=f"branch{k}_bwd_w{l}")
                          for k in range(3)])
    early = dict(w_branch=g_branch.reshape(3 * D, D), w_out=g_out, w_up=g_up, w_down=g_down)
    hosts = hosts or {}
    in_attn, in_gating = hosts["early"](early) if "early" in hosts else (_NO_EXCHANGE, _NO_EXCHANGE)
    carried = {}
    dq, dk, dv, d_sink, carried["attn"] = _attention_bwd(s["q"], s["kd"], s["vd"], s["sink_b"], dys[0], n_lat, n_ctx,
                                                         name=f"attn_bwd{l}", hosted=in_attn)
    dz_qkv = _qkv_unprep(dq, dk, dv, cos_t, sin_t, name=f"qkv_unprep{l}")
    dz_a, d_ws, d_bs, st_gv, carried["gating"] = _gating_bwd(s["z_a"], dys[1], s["w_spatial"], s["b_sb"], s["g_v"],
                                                             name=f"gating_bwd{l}", hosted=in_gating)
    dz_b, st_sc = _sconv_bwd(s["z_b"], dys[2], s["w_sconv8"], n_lat, name=f"sconv_bwd{l}")
    dzs = (dz_qkv, dz_a, dz_b, dz_g)
    g_in = jnp.concatenate([_mm(s["h"], dzs[k], ta=True, out_dtype=BF16, name=f"in_bwd_w{k}_{l}")
                            for k in range(4)], axis=1)
    dh_parts = [_mm(dzs[k], s["w_seg"][k], tb=True, name=f"in_bwd_x{k}_{l}") for k in range(4)]
    in_norm1 = hosts["w_in"](g_in) if "w_in" in hosts else _NO_EXCHANGE
    dx0, st_n1, carried["norm1"], _ = _norm_mod_bwd(s["x0"], dh_parts, dx1, s["g_mix"], mod8, 1, n_lat,
                                                    name=f"norm1_bwd{l}", hosted=in_norm1)
    dmod = jnp.concatenate([st_n1[0:2], st_n1[2:4], st_gt1[0:2], st_n2[0:2], st_n2[2:4], st_gt2[0:2]], axis=1)
    dmod128 = _pad_rows(dmod, LANES)
    g_mod = _mm(a128, dmod128, ta=True, out_dtype=BF16, name=f"mod_bwd_w{l}")
    d_a128 = _mm(dmod128, w["w_mod"], tb=True, name=f"mod_bwd_x{l}")
    wgrads = dict(early, w_mod=g_mod, w_in=g_in)
    lgrads = dict(b_mod=dmod[0] + dmod[1], g_mix=st_n1[4], g_ffn=st_n2[4], b_gate=st_bg[0], sink=d_sink[:, 0],
                  w_spatial=d_ws, b_spatial=d_bs[:, :, 0], g_v=st_gv[0], w_sconv=st_sc[0:3], w_fconv=st_fc[0:3])
    return dx0, wgrads, lgrads, d_a128, carried


_SMALL_ORDER = ("c_ctx", "b_mod", "g_mix", "b_gate", "sink", "w_spatial", "b_spatial", "g_v", "w_sconv", "g_ffn",
                "w_fconv", "g_final")


def _flat_pack(parts, width):
    flat = jnp.concatenate([p.reshape(-1).astype(F32) for p in parts])
    rows = -(-flat.shape[0] // (width * SUBLANES)) * SUBLANES
    flat = jnp.concatenate([flat, jnp.zeros((rows * width - flat.shape[0],), F32)])
    return flat.reshape(rows, width)


def _flat_unpack(packed, likes):
    flat = packed.reshape(-1)
    out, off = [], 0
    for like in likes:
        n = math.prod(like.shape)
        out.append(flat[off:off + n].reshape(like.shape))
        off += n
    return out


def kernel(x, c, ctx, c_ctx, w_mod, b_mod, g_mix, w_in, b_gate, sink, w_spatial, b_spatial, g_v, w_sconv, w_branch, w_out, g_ffn, w_up, w_fconv, w_down, g_final, loss_target, m_c_ctx, m_w_mod, m_b_mod, m_g_mix, m_w_in, m_b_gate, m_sink, m_w_spatial, m_b_spatial, m_g_v, m_w_sconv, m_w_branch, m_w_out, m_g_ffn, m_w_up, m_w_fconv, m_w_down, m_g_final, v_c_ctx, v_w_mod, v_b_mod, v_g_mix, v_w_in, v_b_gate, v_sink, v_w_spatial, v_b_spatial, v_g_v, v_w_sconv, v_w_branch, v_w_out, v_g_ffn, v_w_up, v_w_fconv, v_w_down, v_g_final):
    n_lat, n_ctx = x.shape[1], ctx.shape[1]
    chip = 2 * lax.axis_index("x") + lax.axis_index("y")
    weights = dict(c_ctx=c_ctx, w_mod=w_mod, b_mod=b_mod, g_mix=g_mix, w_in=w_in, b_gate=b_gate, sink=sink,
                   w_spatial=w_spatial, b_spatial=b_spatial, g_v=g_v, w_sconv=w_sconv, w_branch=w_branch, w_out=w_out,
                   g_ffn=g_ffn, w_up=w_up, w_fconv=w_fconv, w_down=w_down, g_final=g_final)
    m_in = dict(c_ctx=m_c_ctx, w_mod=m_w_mod, b_mod=m_b_mod, g_mix=m_g_mix, w_in=m_w_in, b_gate=m_b_gate, sink=m_sink,
                w_spatial=m_w_spatial, b_spatial=m_b_spatial, g_v=m_g_v, w_sconv=m_w_sconv, w_branch=m_w_branch,
                w_out=m_w_out, g_ffn=m_g_ffn, w_up=m_w_up, w_fconv=m_w_fconv, w_down=m_w_down, g_final=m_g_final)
    v_in = dict(c_ctx=v_c_ctx, w_mod=v_w_mod, b_mod=v_b_mod, g_mix=v_g_mix, w_in=v_w_in, b_gate=v_b_gate, sink=v_sink,
                w_spatial=v_w_spatial, b_spatial=v_b_spatial, g_v=v_g_v, w_sconv=v_w_sconv, w_branch=v_w_branch,
                w_out=v_w_out, g_ffn=v_g_ffn, w_up=v_w_up, w_fconv=v_w_fconv, w_down=v_w_down, g_final=v_g_final)
    big_names = [n for n, _, _ in _BIG]

    conv_pack = _flat_pack([w_sconv, w_fconv], LANES)
    conv_all = _allgather8(conv_pack, name="gather_conv_weights")
    conv_parts = [_flat_unpack(conv_all[2 * p], [w_sconv, w_fconv]) for p in range(N_CHIPS)]
    w_sconv_full = jnp.concatenate([cp[0] for cp in conv_parts], axis=-1)
    w_fconv_full = jnp.concatenate([cp[1] for cp in conv_parts], axis=-1)

    small = dict(b_mod=b_mod, g_mix=g_mix, b_gate=b_gate, sink=sink, w_spatial=w_spatial, b_spatial=b_spatial, g_v=g_v,
                 w_sconv=w_sconv_full, g_ffn=g_ffn, w_fconv=w_fconv_full, g_final=g_final)
    x_tok = jnp.concatenate([x[0], ctx[0]], axis=0)
    step = _step_context(c[0], c_ctx, n_lat, n_ctx)
    layer_small = [{k: v[l] for k, v in small.items() if k != "g_final"} for l in range(DEPTH)]
    my_half = lax.axis_index("c").astype(jnp.int32).reshape(1)

    shards = {n: weights[n].astype(BF16) for n in big_names}
    pack = [_pack_shards(shards, l) for l in range(DEPTH)]
    first = _gather_finish(_gather_ici(pack[0][:2], name="gather_ici0"), pack[0][:2], name="gather_finish0")
    w0 = dict(w_mod=_unpack_cols(first[0]), w_in=_unpack_cols(first[1]))

    def layer0_late_weights(carried):
        rest = _gather_finish(list(carried[1:]), pack[0][2:], name="gather_finish0_late")
        w0.update(w_up=_unpack_cols(rest[0]), **_unpack_rows(rest[1]))
        return w0

    xs, saved0, (part_attn, part_gating, part_ffn) = _layer_fwd(
        0, x_tok, w0, layer_small[0], step, hosted=_gather_ici_hosted(pack[1][1:2] + pack[0][2:]),
        hosted_gating=_gather_ici_hosted(pack[1][:1]), hosted_ffn=_gather_ici_hosted(pack[1][2:]),
        late_weights=layer0_late_weights)
    partial1 = list(part_gating) + list(part_attn[:1]) + list(part_ffn)
    w1 = _unpack_full(_gather_finish(partial1, pack[1], name="gather_finish1"))
    xs, saved1, _ = _layer_fwd(1, xs, w1, layer_small[1], step)
    dx, sq_err, d_g_final = _loss_bwd(xs, loss_target[0], g_final, n_lat)
    loss = lax.psum(0.5 * jnp.sum(sq_err) / D, ("x", "y", "c"))

    def reduce_start(g_packs, tag):
        got = _grad_sibling_swap(g_packs, name=f"grad_sibling_swap{tag}")
        return [_add_half(my_half, a, b, name=f"grad_pair_sum{tag}_{g}") for g, (a, b) in enumerate(zip(g_packs, got))]

    def reduce_finish(exchanged, tag):
        sums = [_sum_slabs(e, F32, name=f"grad_chip_sum{tag}_{g}") for g, e in enumerate(exchanged)]
        return _grad_sibling_share(sums, name=f"grad_sibling_share{tag}")

    dx, wgrads1, lgrads1, d_a1, _ = _layer_bwd(1, saved1, w1, dx, step)
    pair_sum1 = reduce_start(_pack_grads(wgrads1), "1")

    def carried_early(early):
        pair_sum0_early = reduce_start([_pack_grad_cols(early["w_up"]), _pack_grad_rows(early)], "0_early")
        return _grad_ici_hosted(pair_sum1 + pair_sum0_early[1:]), _grad_ici_hosted(pair_sum0_early[:1])

    def carried_w_in(g_in):
        return _grad_ici_hosted(reduce_start([_pack_grad_cols(g_in)], "0_in"))

    dx, wgrads0, lgrads0, d_a0, exchanged = _layer_bwd(0, saved0, w0, dx, step,
                                                       hosts=dict(early=carried_early, w_in=carried_w_in))
    total1 = reduce_finish(exchanged["attn"][:4], "1")
    total0_early = reduce_finish(list(exchanged["gating"]) + list(exchanged["attn"][4:]), "0_early")
    total0_in = reduce_finish(exchanged["norm1"], "0_in")
    mod_sum = reduce_start([_pack_grad_cols(wgrads0["w_mod"])], "0_mod")
    sgrads = _small_grads([lgrads0, lgrads1], [d_a0, d_a1], d_g_final, step)
    grad_x = dx[:n_lat][None]

    s_likes = [sgrads[n] for n in _SMALL_ORDER]
    s_all = _allgather8(_flat_pack(s_likes, D), name="gather_small_grads")
    s_tot = _flat_unpack(_sum_slabs(s_all, F32, name="small_grad_sum"), s_likes)
    grads = dict(zip(_SMALL_ORDER, s_tot))
    grads["w_sconv"] = lax.dynamic_slice_in_dim(grads["w_sconv"], chip * w_sconv.shape[-1], w_sconv.shape[-1], axis=2)
    grads["w_fconv"] = lax.dynamic_slice_in_dim(grads["w_fconv"], chip * w_fconv.shape[-1], w_fconv.shape[-1], axis=2)

    delta, new_m, new_v = {}, {}, {}

    def adamw(n, hosted=None):
        cols = weights[n].shape[-1]
        view = lambda a: a.reshape(-1, cols)
        d_, m_, v_, carried = _adamw(view(weights[n]), view(grads[n]), view(m_in[n]), view(v_in[n]), name=f"adamw_{n}",
                                     hosted=hosted)
        delta[n], new_m[n], new_v[n] = (t.reshape(weights[n].shape) for t in (d_, m_, v_))
        return carried

    grads["w_in"] = jnp.stack([total0_in[0], total1[1]])
    total0_mod = reduce_finish(adamw("w_in", _grad_ici_hosted(mod_sum)), "0_mod")
    total0 = list(total0_mod) + list(total0_in) + list(total0_early)
    rest = _unpack_shards([total0, total1], {n: weights[n] for n in big_names})
    grads.update({n: g for n, g in rest.items() if n != "w_in"})
    for n in big_names:
        if n != "w_in":
            adamw(n)
    likes = [weights[n] for n in _SMALL_ORDER]
    packs = [_flat_pack([src[n] for n in _SMALL_ORDER], D) for src in (weights, grads, m_in, v_in)]
    outs = _adamw(*packs, name="adamw_small")[:3]
    for dst, packed in zip((delta, new_m, new_v), outs):
        for n, val in zip(_SMALL_ORDER, _flat_unpack(packed, likes)):
            dst[n] = val

    order = ("c_ctx", "w_mod", "b_mod", "g_mix", "w_in", "b_gate", "sink", "w_spatial", "b_spatial", "g_v", "w_sconv",
             "w_branch", "w_out", "g_ffn", "w_up", "w_fconv", "w_down", "g_final")
    return (loss, grad_x, *[grads[n] for n in order], *[delta[n] for n in order], *[new_m[n] for n in order],
            *[new_v[n] for n in order])
```

```python
import functools
import math

import jax
import jax.numpy as jnp
from jax import lax
from jax.experimental import pallas as pl
from jax.experimental.pallas import tpu as pltpu

F32 = jnp.float32
BF16 = jnp.bfloat16

D = 1024
DEPTH = 2
GRID_W = 64
N_HEADS = 16
N_KV = 4
GRP = N_HEADS // N_KV
HEAD_DIM = 64
KV_W = N_KV * HEAD_DIM
WINDOW = 128
BLK = 128
ROPE_THETA = 10000.0
A_GROUPS = 8
D_FF = 2816
EPS = 1e-6
NEG = -1e30
QKV_W = D + 2 * KV_W
A_COLS = 2 * D
B_COLS = 3 * D
G_COLS = 3 * D
IN_W = QKV_W + A_COLS + B_COLS + G_COLS
SEG = (0, QKV_W, QKV_W + A_COLS, QKV_W + A_COLS + B_COLS, IN_W)
N_CHIPS = 4
N_DEV = 8
LANES = 128
SUBLANES = 8
VMEM_LIMIT = 48 * 1024 * 1024
VMEM_LIMIT_WIDE = 56 * 1024 * 1024
ADAM_LR = 0.001
ADAM_B1 = 0.9
ADAM_B2 = 0.999
ADAM_EPS = 1e-08
ADAM_WD = 0.01
ADAM_STEP = 10
MESH = pl.DeviceIdType.MESH
ANY = pl.BlockSpec(memory_space=pl.ANY)


def _params(sem=None, vmem=VMEM_LIMIT):
    return pltpu.CompilerParams(dimension_semantics=sem, vmem_limit_bytes=vmem)


def _pick(n, cands):
    for c in cands:
        if n % c == 0:
            return c
    return n


def _rows8(rows, width):
    r = lax.broadcasted_iota(jnp.int32, (SUBLANES, width), 0)
    out = jnp.zeros((SUBLANES, width), F32)
    for idx, v in rows:
        out = out + jnp.where(r == idx, v, 0.0)
    return out


def _sel(mod_ref, k, is_ctx):
    return jnp.where(is_ctx, mod_ref[1:2, k * D:(k + 1) * D], mod_ref[0:1, k * D:(k + 1) * D])


def _colsum(v):
    return jnp.sum(v, axis=0, keepdims=True)


def _mm(a, b, *, name, ta=False, tb=False, out_dtype=F32):
    if ta:
        k_dim, m = a.shape
    else:
        m, k_dim = a.shape
    if tb:
        n, kb = b.shape
    else:
        kb, n = b.shape
    assert k_dim == kb, (a.shape, b.shape, ta, tb)
    tm = _pick(m, (1056, 1024, 1408, 768, 512, 256, 128))
    tn = _pick(n, (1536, 1408, 1024, 768, 512, 256, 128))
    tk = _pick(k_dim, (2048, 1536, 1408, 1024, 768, 512, 256, 128))
    nk = k_dim // tk
    dims = (((0 if ta else 1,), (1 if tb else 0,)), ((), ()))

    def product(a_ref, b_ref):
        return lax.dot_general(a_ref[...].astype(BF16), b_ref[...].astype(BF16), dims, preferred_element_type=F32)

    def body_single(a_ref, b_ref, o_ref):
        o_ref[...] = product(a_ref, b_ref).astype(o_ref.dtype)

    def body_acc(a_ref, b_ref, o_ref, acc_ref):
        k = pl.program_id(2)

        @pl.when(k == 0)
        def _():
            acc_ref[...] = product(a_ref, b_ref)

        @pl.when(k > 0)
        def _():
            acc_ref[...] += product(a_ref, b_ref)

        @pl.when(k == nk - 1)
        def _():
            o_ref[...] = acc_ref[...].astype(o_ref.dtype)

    a_spec = pl.BlockSpec((tk, tm), lambda i, j, k: (k, i)) if ta else pl.BlockSpec((tm, tk), lambda i, j, k: (i, k))
    b_spec = pl.BlockSpec((tn, tk), lambda i, j, k: (j, k)) if tb else pl.BlockSpec((tk, tn), lambda i, j, k: (k, j))
    return pl.pallas_call(
        body_single if nk == 1 else body_acc, name=name, grid=(m // tm, n // tn, nk),
        in_specs=[a_spec, b_spec], out_specs=pl.BlockSpec((tm, tn), lambda i, j, k: (i, j)),
        out_shape=jax.ShapeDtypeStruct((m, n), out_dtype),
        scratch_shapes=[] if nk == 1 else [pltpu.VMEM((tm, tn), F32)],
        compiler_params=_params(("parallel", "parallel", "arbitrary")),
    )(a, b)


def _mm_residual(a, b, x, mod8, gt_idx, n_lat, *, name, norm=None):
    t_rows, k_dim = a.shape
    tm = _pick(t_rows, (768, 512, 256))
    tk = _pick(k_dim, (1408, 1024, 512, 256, 128))
    nk = k_dim // tk
    fused = norm is not None

    def body(a_ref, b_ref, x_ref, mod_ref, *rest):
        g_ref = rest[0] if fused else None
        br_ref, xo_ref = rest[fused:fused + 2]
        acc_ref = rest[-1]
        i, k = pl.program_id(0), pl.program_id(1)
        part = jnp.dot(a_ref[...].astype(BF16), b_ref[...].astype(BF16), preferred_element_type=F32)

        @pl.when(k == 0)
        def _():
            acc_ref[...] = part

        @pl.when(k > 0)
        def _():
            acc_ref[...] += part

        @pl.when(k == nk - 1)
        def _():
            row_is_ctx = i * tm + lax.broadcasted_iota(jnp.int32, (tm, 1), 0) >= n_lat

            def sel(idx):
                return jnp.where(row_is_ctx, mod_ref[1:2, idx * D:(idx + 1) * D], mod_ref[0:1, idx * D:(idx + 1) * D])

            branch = acc_ref[...]
            br_ref[...] = branch
            xv = x_ref[...] + sel(gt_idx) * branch
            xo_ref[...] = xv
            if fused:
                rstd = lax.rsqrt(jnp.mean(xv * xv, axis=-1, keepdims=True) + EPS)
                y = xv * rstd * g_ref[...]
                rest[3][...] = (y * (1.0 + sel(norm[2])) + sel(norm[1])).astype(BF16)

    row = pl.BlockSpec((tm, D), lambda i, k: (i, 0))
    whole = lambda shape: pl.BlockSpec(shape, lambda i, k: (0, 0))
    outs = pl.pallas_call(
        body, name=name, grid=(t_rows // tm, nk),
        in_specs=[pl.BlockSpec((tm, tk), lambda i, k: (i, k)), pl.BlockSpec((tk, D), lambda i, k: (k, 0)), row,
                  whole((SUBLANES, 6 * D))] + [whole((1, D))] * fused,
        out_specs=[row, row] + [row] * fused,
        out_shape=[jax.ShapeDtypeStruct((t_rows, D), F32)] * 2 + [jax.ShapeDtypeStruct((t_rows, D), BF16)] * fused,
        scratch_shapes=[pltpu.VMEM((tm, D), F32)],
        compiler_params=_params(("parallel", "arbitrary")),
    )(a, b, x, mod8, *([norm[0]] if fused else []))
    return outs[0], outs[1], (outs[2] if fused else None)


def _small(fn, out_shape, *arrays, name):
    def body(*refs):
        refs[-1][...] = fn(*[r[...] for r in refs[:-1]]).astype(refs[-1].dtype)

    return pl.pallas_call(body, name=name, out_shape=jax.ShapeDtypeStruct(out_shape, F32))(*arrays)


def _silu(v):
    return v * jax.nn.sigmoid(v)


def _dsilu(v):
    s = jax.nn.sigmoid(v)
    return s * (1.0 + v * (1.0 - s))


def _row_spec(tm, width, col=0):
    return pl.BlockSpec((tm, width), lambda i: (i, col))


def _full_spec(shape):
    nd = len(shape)
    return pl.BlockSpec(shape, lambda i: (0,) * nd)


def _halo_specs(tm, width, t_rows, col=0):
    per = tm // SUBLANES
    last = t_rows // SUBLANES - 1
    prev = pl.BlockSpec((SUBLANES, width), lambda i: (jnp.maximum(i * per - 1, 0), col))
    nxt = pl.BlockSpec((SUBLANES, width), lambda i: (jnp.minimum((i + 1) * per, last), col))
    return prev, nxt


def _shift_rows(cur, prev8, next8, n_lat, t_rows, tm):
    i = pl.program_id(0)
    row = lax.broadcasted_iota(jnp.int32, (tm, 1), 0)
    g = row + i * tm
    up = pltpu.roll(cur, 1, 0)
    up = jnp.where(row == 0, prev8[SUBLANES - 1:SUBLANES, :], up)
    up = jnp.where((g == 0) | (g == n_lat), 0.0, up)
    dn = pltpu.roll(cur, tm - 1, 0)
    dn = jnp.where(row == tm - 1, next8[0:1, :], dn)
    dn = jnp.where((g == n_lat - 1) | (g == t_rows - 1), 0.0, dn)
    return up, dn


def _norm_mod_fwd(x, g, mod8, sh_idx, sc_idx, n_lat, *, name):
    t_rows = x.shape[0]
    tm = 256

    def body(x_ref, g_ref, mod_ref, o_ref):
        is_ctx = pl.program_id(0) * tm >= n_lat
        xv = x_ref[...]
        rstd = lax.rsqrt(jnp.mean(xv * xv, axis=-1, keepdims=True) + EPS)
        y = xv * rstd * g_ref[...]
        o_ref[...] = (y * (1.0 + _sel(mod_ref, sc_idx, is_ctx)) + _sel(mod_ref, sh_idx, is_ctx)).astype(BF16)

    return pl.pallas_call(
        body, name=name, grid=(t_rows // tm,),
        in_specs=[_row_spec(tm, D), _full_spec((1, D)), _full_spec((SUBLANES, 6 * D))],
        out_specs=_row_spec(tm, D), out_shape=jax.ShapeDtypeStruct((t_rows, D), BF16),
        compiler_params=_params(("parallel",)),
    )(x, g, mod8)


def _norm_mod_bwd(x, dh_parts, dres, g, mod8, sc_idx, n_lat, *, name, hosted=None, res=None):
    hosted = hosted or _NO_EXCHANGE
    t_rows = x.shape[0]
    tm = 256
    n_parts = len(dh_parts)
    n_steps = t_rows // tm
    fused = res is not None

    def body(*refs):
        ins, outs, _, h_refs = _split_refs(refs, 4 + n_parts + fused, 2 + fused, 0, hosted)
        x_ref, dres_ref, g_ref, mod_ref = ins[:4]
        part_refs = ins[4:4 + n_parts]
        dx_ref, st_ref = outs[:2]
        i = pl.program_id(0)
        _run_hosted(hosted, h_refs, i, n_steps)
        is_ctx = i * tm >= n_lat
        dh = part_refs[0][...]
        for p in part_refs[1:]:
            dh = dh + p[...]
        xv = x_ref[...]
        gv = g_ref[...]
        rstd = lax.rsqrt(jnp.mean(xv * xv, axis=-1, keepdims=True) + EPS)
        rn = xv * rstd
        dy = dh * (1.0 + _sel(mod_ref, sc_idx, is_ctx))
        e = dy * gv
        dxv = dres_ref[...] + rstd * (e - rn * jnp.mean(e * rn, axis=-1, keepdims=True))
        dx_ref[...] = dxv
        dsh = _colsum(dh)
        dsc = _colsum(dh * (rn * gv))
        dg = _colsum(dy * rn)
        zero = jnp.zeros_like(dsh)
        rows = [(0, jnp.where(is_ctx, zero, dsh)), (1, jnp.where(is_ctx, dsh, zero)),
                (2, jnp.where(is_ctx, zero, dsc)), (3, jnp.where(is_ctx, dsc, zero)), (4, dg)]
        if fused:
            outs[2][...] = (dxv * _sel(mod_ref, res[1], is_ctx)).astype(BF16)
            dgt = _colsum(dxv * ins[4 + n_parts][...])
            rows += [(5, jnp.where(is_ctx, zero, dgt)), (6, jnp.where(is_ctx, dgt, zero))]
        upd = _rows8(rows, D)

        @pl.when(i == 0)
        def _():
            st_ref[...] = upd

        @pl.when(i > 0)
        def _():
            st_ref[...] += upd

    outs = pl.pallas_call(
        body, name=name, grid=(n_steps,),
        in_specs=[_row_spec(tm, D), _row_spec(tm, D), _full_spec((1, D)), _full_spec((SUBLANES, 6 * D))]
        + [_row_spec(tm, D)] * (n_parts + fused) + [ANY] * len(hosted.arrays),
        out_specs=[_row_spec(tm, D), _full_spec((SUBLANES, D))] + [_row_spec(tm, D)] * fused
        + [ANY] * len(hosted.out_shapes),
        out_shape=[jax.ShapeDtypeStruct((t_rows, D), F32), jax.ShapeDtypeStruct((SUBLANES, D), F32)]
        + [jax.ShapeDtypeStruct((t_rows, D), BF16)] * fused + list(hosted.out_shapes),
        scratch_shapes=list(hosted.scratch),
        compiler_params=_params(("arbitrary",)),
    )(x, dres, g, mod8, *dh_parts, *([res[0]] if fused else []), *hosted.arrays)
    return outs[0], outs[1], outs[2 + fused:], (outs[2] if fused else None)


def _residual_bwd(dx, branch, mod8, gt_idx, n_lat, *, name):
    t_rows = dx.shape[0]
    tm = 256

    def body(dx_ref, b_ref, mod_ref, o_ref, st_ref):
        i = pl.program_id(0)
        is_ctx = i * tm >= n_lat
        dxv = dx_ref[...]
        o_ref[...] = (dxv * _sel(mod_ref, gt_idx, is_ctx)).astype(BF16)
        dgt = _colsum(dxv * b_ref[...])
        zero = jnp.zeros_like(dgt)
        upd = _rows8([(0, jnp.where(is_ctx, zero, dgt)), (1, jnp.where(is_ctx, dgt, zero))], D)

        @pl.when(i == 0)
        def _():
            st_ref[...] = upd

        @pl.when(i > 0)
        def _():
            st_ref[...] += upd

    return pl.pallas_call(
        body, name=name, grid=(t_rows // tm,),
        in_specs=[_row_spec(tm, D), _row_spec(tm, D), _full_spec((SUBLANES, 6 * D))],
        out_specs=[_row_spec(tm, D), _full_spec((SUBLANES, D))],
        out_shape=[jax.ShapeDtypeStruct((t_rows, D), BF16), jax.ShapeDtypeStruct((SUBLANES, D), F32)],
        compiler_params=_params(("arbitrary",)),
    )(dx, branch, mod8)


def _rope_tables(n_lat, n_ctx):
    rows = n_lat // GRID_W
    row = jnp.broadcast_to(jnp.arange(rows, dtype=F32)[:, None], (rows, GRID_W)).reshape(n_lat)
    col = jnp.broadcast_to(jnp.arange(GRID_W, dtype=F32)[None, :], (rows, GRID_W)).reshape(n_lat)
    half = HEAD_DIM // 2
    inv = ROPE_THETA ** (-jnp.arange(0, half, 2, dtype=F32) / half)
    ang = jnp.concatenate([row[:, None] * inv, col[:, None] * inv], axis=-1)
    cos, sin = jnp.cos(ang), jnp.sin(ang)
    c64 = jnp.concatenate([cos, cos], axis=-1)
    s64 = jnp.concatenate([-sin, sin], axis=-1)
    c64 = jnp.concatenate([c64, jnp.ones((n_ctx, HEAD_DIM), F32)], axis=0)
    s64 = jnp.concatenate([s64, jnp.zeros((n_ctx, HEAD_DIM), F32)], axis=0)
    return jnp.tile(c64, (1, 2)), jnp.tile(s64, (1, 2))


def _swap_halves(v):
    lane = lax.broadcasted_iota(jnp.int32, v.shape, 1)
    return jnp.where(lane % HEAD_DIM < HEAD_DIM // 2, pltpu.roll(v, LANES - HEAD_DIM // 2, 1),
                     pltpu.roll(v, HEAD_DIM // 2, 1))


def _low_half(shape):
    return lax.broadcasted_iota(jnp.int32, shape, 1) < HEAD_DIM


def _qkv_proj(h, w_qkv, cos_t, sin_t, *, name):
    t_rows = h.shape[0]
    tm = _pick(t_rows, (768, 512, 256))

    def body(h_ref, w_ref, c_ref, s_ref, q_ref, k_ref, v_ref):
        z = jnp.dot(h_ref[...], w_ref[...], preferred_element_type=F32)
        cv, sv = c_ref[...], s_ref[...]

        def rope(chunk):
            return chunk * cv + _swap_halves(chunk) * sv

        for ch in range(D // LANES):
            roped = rope(z[:, ch * LANES:(ch + 1) * LANES])
            q_ref[:, ch * LANES:(ch + 1) * LANES] = (roped * (HEAD_DIM ** -0.5)).astype(BF16)
        low = _low_half((tm, LANES))
        for pair in range(N_KV // 2):
            for which, ref, roped in ((0, k_ref, True), (1, v_ref, False)):
                off = D + which * KV_W + pair * LANES
                chunk = z[:, off:off + LANES]
                if roped:
                    chunk = rope(chunk)
                other = pltpu.roll(chunk, HEAD_DIM, 1)
                even = jnp.where(low, chunk, other)
                odd = jnp.where(low, other, chunk)
                ref[:, (2 * pair) * LANES:(2 * pair + 1) * LANES] = even.astype(BF16)
                ref[:, (2 * pair + 1) * LANES:(2 * pair + 2) * LANES] = odd.astype(BF16)

    dup_w = N_KV * LANES
    return pl.pallas_call(
        body, name=name, grid=(t_rows // tm,),
        in_specs=[_row_spec(tm, D), _full_spec((D, QKV_W)), _row_spec(tm, LANES), _row_spec(tm, LANES)],
        out_specs=[_row_spec(tm, D), _row_spec(tm, dup_w), _row_spec(tm, dup_w)],
        out_shape=[jax.ShapeDtypeStruct((t_rows, D), BF16), jax.ShapeDtypeStruct((t_rows, dup_w), BF16),
                   jax.ShapeDtypeStruct((t_rows, dup_w), BF16)],
        compiler_params=_params(("parallel",)),
    )(h, w_qkv, cos_t, sin_t)


def _qkv_unprep(dq, dk, dv, cos_t, sin_t, *, name):
    t_rows = dq.shape[0]
    tm = 256

    def body(dq_ref, dk_ref, dv_ref, c_ref, s_ref, o_ref):
        cv, sv = c_ref[...], s_ref[...]

        def unrope(chunk):
            return chunk * cv + _swap_halves(chunk * sv)

        for ch in range(D // LANES):
            o_ref[:, ch * LANES:(ch + 1) * LANES] = unrope(dq_ref[:, ch * LANES:(ch + 1) * LANES]).astype(BF16)
        for pair in range(N_KV // 2):
            for which, ref, roped in ((0, dk_ref, True), (1, dv_ref, False)):
                chunk = ref[:, pair * LANES:(pair + 1) * LANES]
                if roped:
                    chunk = unrope(chunk)
                off = D + which * KV_W + pair * LANES
                o_ref[:, off:off + LANES] = chunk.astype(BF16)

    return pl.pallas_call(
        body, name=name, grid=(t_rows // tm,),
        in_specs=[_row_spec(tm, D), _row_spec(tm, KV_W), _row_spec(tm, KV_W), _row_spec(tm, LANES),
                  _row_spec(tm, LANES)],
        out_specs=_row_spec(tm, QKV_W), out_shape=jax.ShapeDtypeStruct((t_rows, QKV_W), BF16),
        compiler_params=_params(("parallel",)),
    )(dq, dk, dv, cos_t, sin_t)


def _attn_specs(n_lat, n_ctx):
    nb = n_lat // BLK
    dup_w = N_KV * LANES

    def ws(j):
        return jnp.clip(j - 1, 0, nb - 3)

    win = [pl.BlockSpec((BLK, dup_w), functools.partial(lambda j, o: (ws(j) + o, 0), o=o)) for o in range(3)]
    ctx = pl.BlockSpec((n_ctx, dup_w), lambda j: (n_lat // n_ctx, 0))
    return nb, ws, win, ctx


def _attn_bias(j, ws_j, nb, n_ctx):
    n_keys = 3 * BLK + n_ctx
    row = lax.broadcasted_iota(jnp.int32, (BLK, n_keys), 0)
    col = lax.broadcasted_iota(jnp.int32, (BLK, n_keys), 1)
    rel = (ws_j - j) * BLK + col - row
    valid = (col >= 3 * BLK) | ((jnp.abs(rel) <= WINDOW) & (j < nb))
    bias = jnp.where(valid, 0.0, NEG)
    return jnp.concatenate([bias] * GRP, axis=0)


def _attn_probs(q_ref, kk, kh, bias, sink_ref):
    low = _low_half((BLK, LANES))
    qs = []
    for g in range(GRP):
        h = GRP * kh + g
        chunk = q_ref[:, (h // 2) * LANES:(h // 2 + 1) * LANES]
        qs.append(jnp.where(low if h % 2 == 0 else ~low, chunk, jnp.zeros_like(chunk)))
    qs = jnp.concatenate(qs, axis=0)
    s = lax.dot_general(qs, kk, (((1,), (1,)), ((), ())), preferred_element_type=F32) + bias
    snk = jnp.concatenate(
        [jnp.broadcast_to(jnp.max(sink_ref[GRP * kh + g:GRP * kh + g + 1, :], axis=1, keepdims=True), (BLK, 1))
         for g in range(GRP)], axis=0)
    m = jnp.maximum(jnp.max(s, axis=-1, keepdims=True), snk)
    p = jnp.exp(s - m)
    p_snk = jnp.exp(snk - m)
    inv = 1.0 / (jnp.sum(p, axis=-1, keepdims=True) + p_snk)
    return qs, p, p_snk, inv


class _Hosted:
    def __init__(self, arrays, out_shapes, scratch, start, finish):
        self.arrays, self.out_shapes, self.scratch, self.start, self.finish = arrays, out_shapes, scratch, start, finish


_NO_EXCHANGE = _Hosted([], [], [], None, None)


def _split_refs(refs, n_in, n_out, n_scratch, hosted):
    hi, ho, hs = len(hosted.arrays), len(hosted.out_shapes), len(hosted.scratch)
    a = n_in + hi
    b = a + n_out + ho
    ins, h_ins = refs[:n_in], refs[n_in:a]
    outs, h_outs = refs[a:a + n_out], refs[a + n_out:b]
    scr, h_scr = refs[b:b + n_scratch], refs[b + n_scratch:b + n_scratch + hs]
    return ins, outs, scr, (h_ins, h_outs, h_scr)


def _run_hosted(hosted, h_refs, step, n_steps):
    if hosted.start is None:
        return

    flat = [r for group in h_refs for r in group]

    @pl.when(step == 0)
    def _():
        hosted.start(*flat)

    @pl.when(step == n_steps - 1)
    def _():
        hosted.finish(*flat)


def _attention_fwd(q, kd, vd, sink_b, n_lat, n_ctx, *, name, hosted=_NO_EXCHANGE):
    t_rows = q.shape[0]
    nb, ws, win, ctx = _attn_specs(n_lat, n_ctx)
    n_steps = t_rows // BLK

    def body(*refs):
        ins, outs, _, h_refs = _split_refs(refs, 10, 1, 0, hosted)
        q_ref, k0, k1, k2, kc, v0, v1, v2, vc, sink_ref = ins
        o_ref, = outs
        j = pl.program_id(0)
        _run_hosted(hosted, h_refs, j, n_steps)
        ws_j = ws(j)
        low = _low_half((BLK, LANES))
        bias = _attn_bias(j, ws_j, nb, n_ctx)
        for kh in range(N_KV):
            sl = slice(kh * LANES, (kh + 1) * LANES)
            kk = jnp.concatenate([k0[:, sl], k1[:, sl], k2[:, sl], kc[:, sl]], axis=0)
            vv = jnp.concatenate([v0[:, sl], v1[:, sl], v2[:, sl], vc[:, sl]], axis=0)
            _, p, _, inv = _attn_probs(q_ref, kk, kh, bias, sink_ref)
            o = jnp.dot(p.astype(BF16), vv, preferred_element_type=F32) * inv
            for half in range(2):
                even = o[(2 * half) * BLK:(2 * half + 1) * BLK]
                odd = o[(2 * half + 1) * BLK:(2 * half + 2) * BLK]
                ch = 2 * kh + half
                o_ref[:, ch * LANES:(ch + 1) * LANES] = jnp.where(low, even, odd).astype(BF16)

    outs = pl.pallas_call(
        body, name=name, grid=(n_steps,),
        in_specs=[_row_spec(BLK, D)] + win + [ctx] + win + [ctx] + [_full_spec((N_HEADS, LANES))]
        + [ANY] * len(hosted.arrays),
        out_specs=[_row_spec(BLK, D)] + [ANY] * len(hosted.out_shapes),
        out_shape=[jax.ShapeDtypeStruct((t_rows, D), BF16)] + list(hosted.out_shapes),
        scratch_shapes=list(hosted.scratch),
        compiler_params=_params(("arbitrary",)),
    )(q, kd, kd, kd, kd, vd, vd, vd, vd, sink_b, *hosted.arrays)
    return outs[0], outs[1:]


def _attention_bwd(q, kd, vd, sink_b, dy, n_lat, n_ctx, *, name, hosted=_NO_EXCHANGE):
    t_rows = q.shape[0]
    nb, ws, win, ctx = _attn_specs(n_lat, n_ctx)
    n_steps = t_rows // BLK

    def body(*refs):
        ins, outs, scr, h_refs = _split_refs(refs, 11, 4, 3, hosted)
        q_ref, k0, k1, k2, kc, v0, v1, v2, vc, sink_ref, dy_ref = ins
        dq_ref, dk_hbm, dv_hbm, ds_ref = outs
        dk_acc, dv_acc, sem = scr
        j = pl.program_id(0)
        _run_hosted(hosted, h_refs, j, n_steps)
        ws_j = ws(j)

        @pl.when(j == 0)
        def _():
            dk_acc[...] = jnp.zeros_like(dk_acc)
            dv_acc[...] = jnp.zeros_like(dv_acc)
            ds_ref[...] = jnp.zeros_like(ds_ref)

        low = _low_half((BLK, LANES))
        low_keys = _low_half((3 * BLK + n_ctx, LANES))
        win_start = pl.multiple_of(ws_j * BLK, BLK)
        scale = HEAD_DIM ** -0.5
        dk_heads, dv_heads = [], []
        bias = _attn_bias(j, ws_j, nb, n_ctx)
        for kh in range(N_KV):
            sl = slice(kh * LANES, (kh + 1) * LANES)
            kk = jnp.concatenate([k0[:, sl], k1[:, sl], k2[:, sl], kc[:, sl]], axis=0)
            vv = jnp.concatenate([v0[:, sl], v1[:, sl], v2[:, sl], vc[:, sl]], axis=0)
            qs, p, p_snk, inv = _attn_probs(q_ref, kk, kh, bias, sink_ref)
            dos = []
            for g in range(GRP):
                h = GRP * kh + g
                chunk = dy_ref[:, (h // 2) * LANES:(h // 2 + 1) * LANES]
                dos.append(jnp.where(low if h % 2 == 0 else ~low, chunk, jnp.zeros_like(chunk)).astype(BF16))
            dos = jnp.concatenate(dos, axis=0)
            dp = lax.dot_general(dos, vv, (((1,), (1,)), ((), ())), preferred_element_type=F32)
            dsum = jnp.sum(p * dp, axis=-1, keepdims=True) * inv
            ds = (p * ((dp - dsum) * inv)).astype(BF16)
            snk_term = p_snk * inv * dsum
            for g in range(GRP):
                contrib = -jnp.sum(snk_term[g * BLK:(g + 1) * BLK], axis=0, keepdims=True)
                ds_ref[GRP * kh + g:GRP * kh + g + 1, :] += jnp.broadcast_to(contrib, (1, LANES))
            dqs = jnp.dot(ds, kk, preferred_element_type=F32) * scale
            for half in range(2):
                even = dqs[(2 * half) * BLK:(2 * half + 1) * BLK]
                odd = dqs[(2 * half + 1) * BLK:(2 * half + 2) * BLK]
                ch = 2 * kh + half
                dq_ref[:, ch * LANES:(ch + 1) * LANES] = jnp.where(low, even, odd)
            dkk = lax.dot_general(ds, qs, (((0,), (0,)), ((), ())), preferred_element_type=F32)
            dvv = lax.dot_general((p * inv).astype(BF16), dos, (((0,), (0,)), ((), ())), preferred_element_type=F32)
            dk_heads.append(dkk + pltpu.roll(dkk, HEAD_DIM, 1))
            dv_heads.append(dvv + pltpu.roll(dvv, HEAD_DIM, 1))
        for pair in range(N_KV // 2):
            sl = slice(pair * LANES, (pair + 1) * LANES)
            for acc, heads in ((dk_acc, dk_heads), (dv_acc, dv_heads)):
                both = jnp.where(low_keys, heads[2 * pair], heads[2 * pair + 1])
                acc[pl.ds(win_start, 3 * BLK), sl] += both[:3 * BLK]
                acc[n_lat:n_lat + n_ctx, sl] += both[3 * BLK:]

        @pl.when(j == n_steps - 1)
        def _():
            ck = pltpu.make_async_copy(dk_acc, dk_hbm, sem.at[0])
            cv = pltpu.make_async_copy(dv_acc, dv_hbm, sem.at[1])
            ck.start()
            cv.start()
            ck.wait()
            cv.wait()

    outs = pl.pallas_call(
        body, name=name, grid=(n_steps,),
        in_specs=[_row_spec(BLK, D)] + win + [ctx] + win + [ctx] + [_full_spec((N_HEADS, LANES)), _row_spec(BLK, D)]
        + [ANY] * len(hosted.arrays),
        out_specs=[_row_spec(BLK, D), ANY, ANY, _full_spec((N_HEADS, LANES))] + [ANY] * len(hosted.out_shapes),
        out_shape=[jax.ShapeDtypeStruct((t_rows, D), F32), jax.ShapeDtypeStruct((t_rows, KV_W), F32),
                   jax.ShapeDtypeStruct((t_rows, KV_W), F32), jax.ShapeDtypeStruct((N_HEADS, LANES), F32)]
        + list(hosted.out_shapes),
        scratch_shapes=[pltpu.VMEM((t_rows, KV_W), F32), pltpu.VMEM((t_rows, KV_W), F32),
                        pltpu.SemaphoreType.DMA((2,))] + list(hosted.scratch),
        compiler_params=_params(("arbitrary",)),
    )(q, kd, kd, kd, kd, vd, vd, vd, vd, sink_b, dy, *hosted.arrays)
    return outs[0], outs[1], outs[2], outs[3], outs[4:]


_GELU_K = math.sqrt(2.0 / math.pi)


def _gelu(v):
    return jax.nn.gelu(v)


def _gelu_and_grad(v):
    t = jnp.tanh(_GELU_K * (v + 0.044715 * (v * v * v)))
    cdf = 0.5 * (1.0 + t)
    return v * cdf, cdf + 0.5 * v * (1.0 - t * t) * _GELU_K * (1.0 + 3.0 * 0.044715 * v * v)


def _gating_fwd(z_a, w_s, b_sb, g_v, *, name, hosted=None):
    hosted = hosted or _NO_EXCHANGE
    t_rows = z_a.shape[0]
    n_steps = t_rows // BLK

    def body(*refs):
        ins, outs, _, h_refs = _split_refs(refs, 4, 1, 0, hosted)
        z_ref, w_ref, b_ref, g_ref = ins
        o_ref, = outs
        _run_hosted(hosted, h_refs, pl.program_id(0), n_steps)
        u = _gelu(z_ref[:, :D])
        v = _gelu(z_ref[:, D:])
        vn = v * lax.rsqrt(jnp.mean(v * v, axis=-1, keepdims=True) + EPS) * g_ref[...]
        for g in range(A_GROUPS):
            sl = slice(g * LANES, (g + 1) * LANES)
            mixed = jnp.dot(w_ref[g].astype(BF16), vn[:, sl].astype(BF16), preferred_element_type=F32) + b_ref[g]
            o_ref[:, sl] = (u[:, sl] * mixed).astype(BF16)

    outs = pl.pallas_call(
        body, name=name, grid=(n_steps,),
        in_specs=[_row_spec(BLK, A_COLS), _full_spec((A_GROUPS, BLK, BLK)), _full_spec((A_GROUPS, BLK, LANES)),
                  _full_spec((1, D))] + [ANY] * len(hosted.arrays),
        out_specs=[_row_spec(BLK, D)] + [ANY] * len(hosted.out_shapes),
        out_shape=[jax.ShapeDtypeStruct((t_rows, D), BF16)] + list(hosted.out_shapes),
        scratch_shapes=list(hosted.scratch),
        compiler_params=_params(("arbitrary",)),
    )(z_a, w_s, b_sb, g_v, *hosted.arrays)
    return outs[0], outs[1:]


def _gating_bwd(z_a, dy, w_s, b_sb, g_v, *, name, hosted=None):
    hosted = hosted or _NO_EXCHANGE
    t_rows = z_a.shape[0]
    n_steps = t_rows // BLK

    def body(*refs):
        ins, outs, _, h_refs = _split_refs(refs, 5, 4, 0, hosted)
        z_ref, dy_ref, w_ref, b_ref, g_ref = ins
        dz_ref, dw_ref, db_ref, st_ref = outs
        i = pl.program_id(0)
        _run_hosted(hosted, h_refs, i, n_steps)

        @pl.when(i == 0)
        def _():
            dw_ref[...] = jnp.zeros_like(dw_ref)
            db_ref[...] = jnp.zeros_like(db_ref)
            st_ref[...] = jnp.zeros_like(st_ref)

        u, du_dz = _gelu_and_grad(z_ref[:, :D])
        v, dv_dz = _gelu_and_grad(z_ref[:, D:])
        gv = g_ref[...]
        rstd = lax.rsqrt(jnp.mean(v * v, axis=-1, keepdims=True) + EPS)
        vh = v * rstd
        vn = vh * gv
        dyv = dy_ref[...]
        dvn = []
        for g in range(A_GROUPS):
            sl = slice(g * LANES, (g + 1) * LANES)
            wg = w_ref[g].astype(BF16)
            vg = vn[:, sl].astype(BF16)
            mixed = jnp.dot(wg, vg, preferred_element_type=F32) + b_ref[g]
            dz_ref[:, sl] = (dyv[:, sl] * mixed * du_dz[:, sl]).astype(BF16)
            dmixed = dyv[:, sl] * u[:, sl]
            dmb = dmixed.astype(BF16)
            dvn.append(lax.dot_general(wg, dmb, (((0,), (0,)), ((), ())), preferred_element_type=F32))
            dw_ref[g] += lax.dot_general(dmb, vg, (((1,), (1,)), ((), ())), preferred_element_type=F32)
            db_ref[g] += jnp.broadcast_to(jnp.sum(dmixed, axis=-1, keepdims=True), (BLK, LANES))
        dvn = jnp.concatenate(dvn, axis=1)
        st_ref[...] += _rows8([(0, _colsum(dvn * vh))], D)
        e = dvn * gv
        dv = rstd * (e - vh * jnp.mean(e * vh, axis=-1, keepdims=True))
        dz_ref[:, D:] = (dv * dv_dz).astype(BF16)

    outs = pl.pallas_call(
        body, name=name, grid=(n_steps,),
        in_specs=[_row_spec(BLK, A_COLS), _row_spec(BLK, D), _full_spec((A_GROUPS, BLK, BLK)),
                  _full_spec((A_GROUPS, BLK, LANES)), _full_spec((1, D))] + [ANY] * len(hosted.arrays),
        out_specs=[_row_spec(BLK, A_COLS), _full_spec((A_GROUPS, BLK, BLK)), _full_spec((A_GROUPS, BLK, LANES)),
                   _full_spec((SUBLANES, D))] + [ANY] * len(hosted.out_shapes),
        out_shape=[jax.ShapeDtypeStruct((t_rows, A_COLS), BF16), jax.ShapeDtypeStruct((A_GROUPS, BLK, BLK), F32),
                   jax.ShapeDtypeStruct((A_GROUPS, BLK, LANES), F32), jax.ShapeDtypeStruct((SUBLANES, D), F32)]
        + list(hosted.out_shapes),
        scratch_shapes=list(hosted.scratch),
        compiler_params=_params(("arbitrary",)),
    )(z_a, dy, w_s, b_sb, g_v, *hosted.arrays)
    return outs[0], outs[1], outs[2], outs[3], outs[4:]


def _sconv_fwd(z_b, w8, n_lat, *, name):
    t_rows = z_b.shape[0]
    tm = 256
    prev, nxt = _halo_specs(tm, B_COLS, t_rows)

    def body(z_ref, zp_ref, zn_ref, w_ref, o_ref):
        p = z_ref[:, D:2 * D] * z_ref[:, 2 * D:]
        pp = zp_ref[:, D:2 * D] * zp_ref[:, 2 * D:]
        pn = zn_ref[:, D:2 * D] * zn_ref[:, 2 * D:]
        up, dn = _shift_rows(p, pp, pn, n_lat, t_rows, tm)
        conv = w_ref[0:1, :] * up + w_ref[1:2, :] * p + w_ref[2:3, :] * dn
        o_ref[...] = (z_ref[:, :D] * conv).astype(BF16)

    return pl.pallas_call(
        body, name=name, grid=(t_rows // tm,),
        in_specs=[_row_spec(tm, B_COLS), prev, nxt, _full_spec((SUBLANES, D))],
        out_specs=_row_spec(tm, D), out_shape=jax.ShapeDtypeStruct((t_rows, D), BF16),
        compiler_params=_params(("parallel",)),
    )(z_b, z_b, z_b, w8)


def _sconv_bwd(z_b, dy, w8, n_lat, *, name):
    t_rows = z_b.shape[0]
    tm = 256
    prev, nxt = _halo_specs(tm, B_COLS, t_rows)
    dprev, dnxt = _halo_specs(tm, D, t_rows)

    def body(z_ref, zp_ref, zn_ref, dy_ref, dyp_ref, dyn_ref, w_ref, dz_ref, st_ref):
        i = pl.program_id(0)
        bg, cg, hb = z_ref[:, :D], z_ref[:, D:2 * D], z_ref[:, 2 * D:]
        p = cg * hb
        pp = zp_ref[:, D:2 * D] * zp_ref[:, 2 * D:]
        pn = zn_ref[:, D:2 * D] * zn_ref[:, 2 * D:]
        up, dn = _shift_rows(p, pp, pn, n_lat, t_rows, tm)
        w0, w1, w2 = w_ref[0:1, :], w_ref[1:2, :], w_ref[2:3, :]
        conv = w0 * up + w1 * p + w2 * dn
        dyv = dy_ref[...]
        dz_ref[:, :D] = (dyv * conv).astype(BF16)
        dcv = dyv * bg
        dcv_up, dcv_dn = _shift_rows(dcv, dyp_ref[...] * zp_ref[:, :D], dyn_ref[...] * zn_ref[:, :D], n_lat, t_rows, tm)
        dp = w0 * dcv_dn + w1 * dcv + w2 * dcv_up
        dz_ref[:, D:2 * D] = (dp * hb).astype(BF16)
        dz_ref[:, 2 * D:] = (dp * cg).astype(BF16)
        upd = _rows8([(0, _colsum(dcv * up)), (1, _colsum(dcv * p)), (2, _colsum(dcv * dn))], D)

        @pl.when(i == 0)
        def _():
            st_ref[...] = upd

        @pl.when(i > 0)
        def _():
            st_ref[...] += upd

    return pl.pallas_call(
        body, name=name, grid=(t_rows // tm,),
        in_specs=[_row_spec(tm, B_COLS), prev, nxt, _row_spec(tm, D), dprev, dnxt, _full_spec((SUBLANES, D))],
        out_specs=[_row_spec(tm, B_COLS), _full_spec((SUBLANES, D))],
        out_shape=[jax.ShapeDtypeStruct((t_rows, B_COLS), BF16), jax.ShapeDtypeStruct((SUBLANES, D), F32)],
        compiler_params=_params(("arbitrary",)),
    )(z_b, z_b, z_b, dy, dy, dy, w8)


def _branch_merge_fwd(ys, w_branch, z_g, b_gate, *, name):
    t_rows = z_g.shape[0]
    tm = _pick(t_rows, (768, 512, 256))

    def body(y0_ref, y1_ref, y2_ref, w_ref, z_ref, b_ref, t_ref, o_ref, acc_ref):
        k = pl.program_id(1)
        for which, y_ref in enumerate((y0_ref, y1_ref, y2_ref)):
            @pl.when(k == which)
            def _():
                t_ref[...] = jnp.dot(y_ref[...], w_ref[...], preferred_element_type=F32)

        term = jax.nn.sigmoid(z_ref[...] + b_ref[...]) * t_ref[...]

        @pl.when(k == 0)
        def _():
            acc_ref[...] = term

        @pl.when(k > 0)
        def _():
            acc_ref[...] += term

        @pl.when(k == 2)
        def _():
            o_ref[...] = acc_ref[...].astype(BF16)

    y_spec = pl.BlockSpec((tm, D), lambda i, k: (i, 0))
    return pl.pallas_call(
        body, name=name, grid=(t_rows // tm, 3),
        in_specs=[y_spec, y_spec, y_spec, pl.BlockSpec((None, D, D), lambda i, k: (k, 0, 0)),
                  pl.BlockSpec((tm, D), lambda i, k: (i, k)), pl.BlockSpec((None, 1, D), lambda i, k: (k, 0, 0))],
        out_specs=[pl.BlockSpec((None, tm, D), lambda i, k: (k, i, 0)), y_spec],
        out_shape=[jax.ShapeDtypeStruct((3, t_rows, D), F32), jax.ShapeDtypeStruct((t_rows, D), BF16)],
        scratch_shapes=[pltpu.VMEM((tm, D), F32)],
        compiler_params=_params(("parallel", "arbitrary")),
    )(*ys, w_branch, z_g, b_gate.reshape(3, 1, D))


def _merge_bwd(dmerged, t_all, z_g, b_gate, *, name):
    t_rows = dmerged.shape[0]
    tm = 256
    t_specs = [pl.BlockSpec((None, tm, D), functools.partial(lambda i, k: (k, i, 0), k=k)) for k in range(3)]

    def body(dm_ref, t0_ref, t1_ref, t2_ref, z_ref, b_ref, d0_ref, d1_ref, d2_ref, dz_ref, st_ref):
        i = pl.program_id(0)
        dm = dm_ref[...]
        sums = []
        for k, (t_ref, d_ref) in enumerate(((t0_ref, d0_ref), (t1_ref, d1_ref), (t2_ref, d2_ref))):
            gate = jax.nn.sigmoid(z_ref[:, k * D:(k + 1) * D] + b_ref[:, k * D:(k + 1) * D])
            d_ref[...] = (dm * gate).astype(BF16)
            dzg = dm * t_ref[...] * gate * (1.0 - gate)
            dz_ref[:, k * D:(k + 1) * D] = dzg.astype(BF16)
            sums.append(_colsum(dzg))
        upd = _rows8([(0, jnp.concatenate(sums, axis=1))], G_COLS)

        @pl.when(i == 0)
        def _():
            st_ref[...] = upd

        @pl.when(i > 0)
        def _():
            st_ref[...] += upd

    return pl.pallas_call(
        body, name=name, grid=(t_rows // tm,),
        in_specs=[_row_spec(tm, D)] + t_specs + [_row_spec(tm, G_COLS), _full_spec((1, G_COLS))],
        out_specs=[_row_spec(tm, D)] * 3 + [_row_spec(tm, G_COLS), _full_spec((SUBLANES, G_COLS))],
        out_shape=[jax.ShapeDtypeStruct((t_rows, D), BF16)] * 3
        + [jax.ShapeDtypeStruct((t_rows, G_COLS), BF16), jax.ShapeDtypeStruct((SUBLANES, G_COLS), F32)],
        compiler_params=_params(("arbitrary",)),
    )(dmerged, t_all, t_all, t_all, z_g, b_gate)


def _ffn_mid_fwd(up, w8, n_lat, *, name, hosted=None):
    hosted = hosted or _NO_EXCHANGE
    t_rows = up.shape[0]
    tm = 256
    n_steps = t_rows // tm
    prev, nxt = _halo_specs(tm, D_FF, t_rows)

    def body(*refs):
        ins, outs, _, h_refs = _split_refs(refs, 5, 2, 0, hosted)
        a_ref, ap_ref, an_ref, g_ref, w_ref = ins
        cv_ref, f_ref = outs
        _run_hosted(hosted, h_refs, pl.program_id(0), n_steps)
        a = a_ref[...]
        au, ad = _shift_rows(a, ap_ref[...], an_ref[...], n_lat, t_rows, tm)
        cv = w_ref[0:1, :] * au + w_ref[1:2, :] * a + w_ref[2:3, :] * ad
        cv_ref[...] = cv
        f_ref[...] = (_silu(cv) * g_ref[...]).astype(BF16)

    outs = pl.pallas_call(
        body, name=name, grid=(n_steps,),
        in_specs=[_row_spec(tm, D_FF), prev, nxt, _row_spec(tm, D_FF, 1), _full_spec((SUBLANES, D_FF))]
        + [ANY] * len(hosted.arrays),
        out_specs=[_row_spec(tm, D_FF), _row_spec(tm, D_FF)] + [ANY] * len(hosted.out_shapes),
        out_shape=[jax.ShapeDtypeStruct((t_rows, D_FF), F32), jax.ShapeDtypeStruct((t_rows, D_FF), BF16)]
        + list(hosted.out_shapes),
        scratch_shapes=list(hosted.scratch),
        compiler_params=_params(("arbitrary",)),
    )(up, up, up, up, w8, *hosted.arrays)
    return outs[0], outs[1], outs[2:]


def _ffn_mid_bwd(up, cv, df, w8, n_lat, *, name):
    t_rows = up.shape[0]
    tm = 256
    prev, nxt = _halo_specs(tm, D_FF, t_rows)
    gprev, gnxt = _halo_specs(tm, D_FF, t_rows, 1)

    def body(a_ref, ap_ref, an_ref, g_ref, gp_ref, gn_ref, cv_ref, cp_ref, cn_ref, df_ref, dfp_ref, dfn_ref,
             w_ref, o_ref, st_ref):
        i = pl.program_id(0)
        a = a_ref[...]
        au, ad = _shift_rows(a, ap_ref[...], an_ref[...], n_lat, t_rows, tm)
        cvv = cv_ref[...]
        dfv = df_ref[...]
        sig = jax.nn.sigmoid(cvv)
        o_ref[:, D_FF:] = (dfv * (cvv * sig)).astype(BF16)
        dcv = dfv * g_ref[...] * (sig * (1.0 + cvv * (1.0 - sig)))
        dcv_p = dfp_ref[...] * gp_ref[...] * _dsilu(cp_ref[...])
        dcv_n = dfn_ref[...] * gn_ref[...] * _dsilu(cn_ref[...])
        du, dd = _shift_rows(dcv, dcv_p, dcv_n, n_lat, t_rows, tm)
        o_ref[:, :D_FF] = (w_ref[0:1, :] * dd + w_ref[1:2, :] * dcv + w_ref[2:3, :] * du).astype(BF16)
        upd = _rows8([(0, _colsum(dcv * au)), (1, _colsum(dcv * a)), (2, _colsum(dcv * ad))], D_FF)

        @pl.when(i == 0)
        def _():
            st_ref[...] = upd

        @pl.when(i > 0)
        def _():
            st_ref[...] += upd

    row = _row_spec(tm, D_FF)
    return pl.pallas_call(
        body, name=name, grid=(t_rows // tm,),
        in_specs=[row, prev, nxt, _row_spec(tm, D_FF, 1), gprev, gnxt, row, prev, nxt, row, prev, nxt,
                  _full_spec((SUBLANES, D_FF))],
        out_specs=[_row_spec(tm, 2 * D_FF), _full_spec((SUBLANES, D_FF))],
        out_shape=[jax.ShapeDtypeStruct((t_rows, 2 * D_FF), BF16), jax.ShapeDtypeStruct((SUBLANES, D_FF), F32)],
        compiler_params=_params(("arbitrary",), VMEM_LIMIT_WIDE),
    )(up, up, up, up, up, up, cv, cv, cv, df, df, df, w8)


def _loss_head(x, target, g_final, n_lat, *, name):
    t_rows = x.shape[0]
    tm = 256
    last = n_lat // tm - 1

    def body(x_ref, t_ref, g_ref, dx_ref, st_ref):
        i = pl.program_id(0)
        is_ctx = i * tm >= n_lat
        xv = x_ref[...]
        gv = g_ref[...]
        rstd = lax.rsqrt(jnp.mean(xv * xv, axis=-1, keepdims=True) + EPS)
        rn = xv * rstd
        err = rn * gv - t_ref[...]
        dy = err / D
        e = dy * gv
        dx = rstd * (e - rn * jnp.mean(e * rn, axis=-1, keepdims=True))
        dx_ref[...] = jnp.where(is_ctx, 0.0, dx)
        keep = jnp.where(is_ctx, 0.0, 1.0)
        upd = _rows8([(0, keep * _colsum(dy * rn)), (1, keep * _colsum(err * err))], D)

        @pl.when(i == 0)
        def _():
            st_ref[...] = upd

        @pl.when(i > 0)
        def _():
            st_ref[...] += upd

    return pl.pallas_call(
        body, name=name, grid=(t_rows // tm,),
        in_specs=[_row_spec(tm, D), pl.BlockSpec((tm, D), lambda i: (jnp.minimum(i, last), 0)), _full_spec((1, D))],
        out_specs=[_row_spec(tm, D), _full_spec((SUBLANES, D))],
        out_shape=[jax.ShapeDtypeStruct((t_rows, D), F32), jax.ShapeDtypeStruct((SUBLANES, D), F32)],
        compiler_params=_params(("arbitrary",)),
    )(x, target, g_final)


def _sum_slabs(x, out_dtype, *, name):
    n_slabs, rows, cols = x.shape
    tm = _pick(rows, (432, 256, 192, 128, 64, 32, 24, 16, 8))

    def body(x_ref, o_ref):
        acc = x_ref[0].astype(F32)
        for s in range(1, n_slabs):
            acc = acc + x_ref[s].astype(F32)
        o_ref[...] = acc.astype(o_ref.dtype)

    return pl.pallas_call(
        body, name=name, grid=(rows // tm,),
        in_specs=[pl.BlockSpec((n_slabs, tm, cols), lambda i: (0, i, 0))],
        out_specs=pl.BlockSpec((tm, cols), lambda i: (i, 0)),
        out_shape=jax.ShapeDtypeStruct((rows, cols), out_dtype),
        compiler_params=_params(("parallel",)),
    )(x)


def _add_half(half_idx, a, b, *, name):
    n_slabs, rows, cols = b.shape
    tm = _pick(rows, (432, 256, 192, 128, 96, 64, 32, 16))
    per_half = rows // tm

    def body(half_ref, a_ref, b_ref, o_ref):
        o_ref[...] = (a_ref[...].astype(F32) + b_ref[...].astype(F32)).astype(BF16)

    spec = pl.BlockSpec((1, tm, cols), lambda s, i, half_ref: (s, i, 0))
    a_spec = pl.BlockSpec((1, tm, cols), lambda s, i, half_ref: (s, half_ref[0] * per_half + i, 0))
    return pl.pallas_call(
        body, name=name,
        grid_spec=pltpu.PrefetchScalarGridSpec(num_scalar_prefetch=1, grid=(n_slabs, per_half),
                                               in_specs=[a_spec, spec], out_specs=spec),
        out_shape=jax.ShapeDtypeStruct(b.shape, BF16), compiler_params=_params(("parallel", "parallel")),
    )(half_idx, a, b)


def _adamw(w, g, m, v, *, name, hosted=None):
    hosted = hosted or _NO_EXCHANGE
    rows, cols = w.shape
    tm = _pick(rows, (256, 128, 64, 32, 16, 8))
    n_steps = rows // tm

    def body(*refs):
        ins, outs, _, h_refs = _split_refs(refs, 4, 3, 0, hosted)
        w_ref, g_ref, m_ref, v_ref = ins
        d_ref, nm_ref, nv_ref = outs
        _run_hosted(hosted, h_refs, pl.program_id(0), n_steps)
        gv = g_ref[...]
        nm = ADAM_B1 * m_ref[...] + (1.0 - ADAM_B1) * gv
        nv = ADAM_B2 * v_ref[...] + (1.0 - ADAM_B2) * jnp.square(gv)
        m_hat = nm / (1.0 - ADAM_B1 ** ADAM_STEP)
        v_hat = nv / (1.0 - ADAM_B2 ** ADAM_STEP)
        d_ref[...] = -ADAM_LR * (m_hat / (jnp.sqrt(v_hat) + ADAM_EPS) + ADAM_WD * w_ref[...])
        nm_ref[...] = nm
        nv_ref[...] = nv

    spec = pl.BlockSpec((tm, cols), lambda i: (i, 0))
    shape = jax.ShapeDtypeStruct((rows, cols), F32)
    outs = pl.pallas_call(
        body, name=name, grid=(n_steps,), in_specs=[spec] * 4 + [ANY] * len(hosted.arrays),
        out_specs=[spec] * 3 + [ANY] * len(hosted.out_shapes), out_shape=[shape] * 3 + list(hosted.out_shapes),
        scratch_shapes=list(hosted.scratch), compiler_params=_params(("arbitrary",)),
    )(w, g, m, v, *hosted.arrays)
    return outs[0], outs[1], outs[2], outs[3:]


def _place():
    x, y, c = lax.axis_index("x"), lax.axis_index("y"), lax.axis_index("c")
    chips = [(1 - x, y), (x, 1 - y), (1 - x, 1 - y)]
    return x, y, c, chips


def _remote(src, dst, send_sems, recv_sems, k, to):
    return pltpu.make_async_remote_copy(src_ref=src, dst_ref=dst, send_sem=send_sems.at[k], recv_sem=recv_sems.at[k],
                                        device_id=to, device_id_type=MESH)


HALF_CHUNKS = 2


def _chunks(ref, n):
    step = ref.shape[0] // n
    tile_rows = SUBLANES if ref.dtype == F32 else 2 * SUBLANES
    assert step * n == ref.shape[0] and step % tile_rows == 0, (ref.shape, n)
    return [ref.at[pl.ds(k * step, step)] for k in range(n)]


def _half(ref, which):
    half = ref.shape[0] // 2
    return ref.at[pl.ds(pl.multiple_of(which * half, 2 * SUBLANES), half)]


def _staged_copy(src, dst, buf, sems):
    step = buf.shape[1]
    n = src.shape[0] // step
    assert n * step == src.shape[0], (src.shape, step)
    ins = [pltpu.make_async_copy(src.at[pl.ds(k * step, step)], buf.at[k % 2], sems.at[k % 2]) for k in range(n)]
    outs = [pltpu.make_async_copy(buf.at[k % 2], dst.at[pl.ds(k * step, step)], sems.at[2 + k % 2]) for k in range(n)]
    ins[0].start()
    for k in range(n):
        ins[k].wait()
        outs[k].start()
        if k + 1 < n:
            if k >= 1:
                outs[k - 1].wait()
            ins[k + 1].start()
    if n >= 2:
        outs[n - 2].wait()
    outs[n - 1].wait()


def _stage_rows(rows):
    return _pick(rows, (256, 432))


def _stage_scratch(slabs):
    return [pltpu.VMEM((2, _stage_rows(s.shape[-2]), s.shape[-1]), s.dtype) for s in slabs] + [pltpu.SemaphoreType.DMA((4,))]


N_LINK_SEMS = (N_CHIPS - 1) * HALF_CHUNKS


def _link_sems(n_groups):
    return [pltpu.SemaphoreType.DMA((n_groups * N_LINK_SEMS,)), pltpu.SemaphoreType.DMA((n_groups * N_LINK_SEMS,))]


def _sem_index(g, j, k):
    return g * N_LINK_SEMS + j * HALF_CHUNKS + k


def _gather_ici_start(*refs):
    n = (len(refs) - 2) // 2
    p_refs, o_refs, (send_sems, recv_sems) = refs[:n], refs[n:2 * n], refs[2 * n:]
    x, y, c, chips = _place()
    for g, (p_ref, o_ref) in enumerate(zip(p_refs, o_refs)):
        src = _chunks(_half(p_ref, c), HALF_CHUNKS)
        dst = _chunks(_half(o_ref.at[2 * x + y], c), HALF_CHUNKS)
        for j, chip in enumerate(chips):
            for k in range(HALF_CHUNKS):
                _remote(src[k], dst[k], send_sems, recv_sems, _sem_index(g, j, k), (*chip, c)).start()


def _gather_ici_finish(*refs):
    n = (len(refs) - 2) // 2
    p_refs, o_refs, (send_sems, recv_sems) = refs[:n], refs[n:2 * n], refs[2 * n:]
    x, y, c, chips = _place()
    for g, (p_ref, o_ref) in enumerate(zip(p_refs, o_refs)):
        src = _chunks(_half(p_ref, c), HALF_CHUNKS)
        for j, (cx, cy) in enumerate(chips):
            for k, landed in enumerate(_chunks(_half(o_ref.at[2 * cx + cy], c), HALF_CHUNKS)):
                _remote(src[k], landed, send_sems, recv_sems, _sem_index(g, j, k), (x, y, c)).wait_recv()
        for j in range(len(chips)):
            for k in range(HALF_CHUNKS):
                _remote(src[k], src[k], send_sems, recv_sems, _sem_index(g, j, k), (x, y, c)).wait_send()


def _gathered_shapes(slabs):
    return [jax.ShapeDtypeStruct((N_CHIPS,) + s.shape, s.dtype) for s in slabs]


def _gather_ici_hosted(slabs):
    return _Hosted(list(slabs), _gathered_shapes(slabs), _link_sems(len(slabs)), _gather_ici_start, _gather_ici_finish)


def _gather_ici(slabs, *, name):
    def body(*refs):
        _gather_ici_start(*refs)
        _gather_ici_finish(*refs)

    return pl.pallas_call(
        body, name=name, in_specs=[ANY] * len(slabs), out_specs=[ANY] * len(slabs),
        out_shape=_gathered_shapes(slabs), scratch_shapes=_link_sems(len(slabs)),
    )(*slabs)


def _gather_finish(partials, slabs, *, name):
    n = len(slabs)

    def body(*refs):
        p_refs, o_refs = refs[n:2 * n], refs[2 * n:3 * n]
        send_sems, recv_sems = refs[3 * n:3 * n + 2]
        bufs, loc_sems = refs[3 * n + 2:4 * n + 2], refs[4 * n + 2]
        x, y, c, chips = _place()
        sib = (x, y, 1 - c)
        passed = []
        for g, o_ref in enumerate(o_refs):
            for j, (cx, cy) in enumerate(chips):
                for k, landed in enumerate(_chunks(_half(o_ref.at[2 * cx + cy], c), HALF_CHUNKS)):
                    passed.append(_remote(landed, landed, send_sems, recv_sems, _sem_index(g, j, k), sib))
        for cp in passed:
            cp.start()
        for p_ref, o_ref, buf in zip(p_refs, o_refs, bufs):
            _staged_copy(p_ref, o_ref.at[2 * x + y], buf, loc_sems)
        for g, o_ref in enumerate(o_refs):
            for j, (cx, cy) in enumerate(chips):
                for k, landed in enumerate(_chunks(_half(o_ref.at[2 * cx + cy], 1 - c), HALF_CHUNKS)):
                    _remote(landed, landed, send_sems, recv_sems, _sem_index(g, j, k), sib).wait_recv()
        for cp in passed:
            cp.wait_send()

    return pl.pallas_call(
        body, name=name, in_specs=[ANY] * (2 * n), out_specs=[ANY] * n,
        out_shape=[jax.ShapeDtypeStruct(p.shape, p.dtype) for p in partials],
        input_output_aliases={g: g for g in range(n)}, scratch_shapes=_link_sems(n) + _stage_scratch(slabs),
        compiler_params=_params(),
    )(*partials, *slabs)


def _grad_sibling_swap(g_packs, *, name):
    n = len(g_packs)
    per_group = N_CHIPS * HALF_CHUNKS

    def body(*refs):
        g_refs, got_refs, (send_sems, recv_sems) = refs[:n], refs[n:2 * n], refs[2 * n:]
        x, y, c, _ = _place()
        sib = (x, y, 1 - c)
        swaps = [_remote(src, dst, send_sems, recv_sems, g * per_group + s * HALF_CHUNKS + k, sib)
                 for g, (g_ref, got_ref) in enumerate(zip(g_refs, got_refs))
                 for s in range(N_CHIPS)
                 for k, (src, dst) in enumerate(zip(_chunks(_half(g_ref.at[s], 1 - c), HALF_CHUNKS),
                                                    _chunks(got_ref.at[s], HALF_CHUNKS)))]
        for cp in swaps:
            cp.start()
        for cp in swaps:
            cp.wait_recv()
        for cp in swaps:
            cp.wait_send()

    return pl.pallas_call(
        body, name=name, in_specs=[ANY] * n, out_specs=[ANY] * n,
        out_shape=[jax.ShapeDtypeStruct((N_CHIPS, g.shape[1] // 2, g.shape[2]), g.dtype) for g in g_packs],
        scratch_shapes=[pltpu.SemaphoreType.DMA((n * per_group,)), pltpu.SemaphoreType.DMA((n * per_group,))],
    )(*g_packs)


def _grad_ici_refs(refs):
    n = (len(refs) - 3) // 3
    return refs[:n], refs[n:2 * n], refs[2 * n], refs[2 * n + 1], refs[2 * n + 2:3 * n + 2], refs[3 * n + 2]


def _grad_ici_start(*refs):
    s_refs, o_refs, send_sems, recv_sems, _, _ = _grad_ici_refs(refs)
    x, y, c, chips = _place()
    for g, (s_ref, o_ref) in enumerate(zip(s_refs, o_refs)):
        for j, (cx, cy) in enumerate(chips):
            pairs = zip(_chunks(s_ref.at[2 * cx + cy], HALF_CHUNKS), _chunks(o_ref.at[2 * x + y], HALF_CHUNKS))
            for k, (src, dst) in enumerate(pairs):
                _remote(src, dst, send_sems, recv_sems, _sem_index(g, j, k), (cx, cy, c)).start()


def _grad_ici_finish(*refs):
    s_refs, o_refs, send_sems, recv_sems, bufs, loc_sems = _grad_ici_refs(refs)
    x, y, c, chips = _place()
    me = 2 * x + y
    for s_ref, o_ref, buf in zip(s_refs, o_refs, bufs):
        _staged_copy(s_ref.at[me], o_ref.at[me], buf, loc_sems)
    for g, (s_ref, o_ref) in enumerate(zip(s_refs, o_refs)):
        for j, (cx, cy) in enumerate(chips):
            for k, landed in enumerate(_chunks(o_ref.at[2 * cx + cy], HALF_CHUNKS)):
                _remote(landed, landed, send_sems, recv_sems, _sem_index(g, j, k), (x, y, c)).wait_recv()
        for j, (cx, cy) in enumerate(chips):
            for k, sent in enumerate(_chunks(s_ref.at[2 * cx + cy], HALF_CHUNKS)):
                _remote(sent, sent, send_sems, recv_sems, _sem_index(g, j, k), (x, y, c)).wait_send()


def _grad_ici_hosted(sums):
    return _Hosted(list(sums), [jax.ShapeDtypeStruct(s.shape, s.dtype) for s in sums],
                   _link_sems(len(sums)) + _stage_scratch(sums), _grad_ici_start, _grad_ici_finish)


def _grad_ici(sums, *, name):
    def body(*refs):
        _grad_ici_start(*refs)
        _grad_ici_finish(*refs)

    n = len(sums)
    return pl.pallas_call(
        body, name=name, in_specs=[ANY] * n, out_specs=[ANY] * n,
        out_shape=[jax.ShapeDtypeStruct(s.shape, s.dtype) for s in sums],
        scratch_shapes=_link_sems(n) + _stage_scratch(sums), compiler_params=_params(),
    )(*sums)


def _grad_sibling_share(totals, *, name):
    n = len(totals)
    n_ch = HALF_CHUNKS

    def body(*refs):
        t_refs, o_refs = refs[:n], refs[n:2 * n]
        send_sems, recv_sems = refs[2 * n:2 * n + 2]
        bufs, loc_sems = refs[2 * n + 2:3 * n + 2], refs[3 * n + 2]
        x, y, c, _ = _place()
        sib = (x, y, 1 - c)
        sends = [_remote(src, dst, send_sems, recv_sems, g * n_ch + k, sib)
                 for g, (t_ref, o_ref) in enumerate(zip(t_refs, o_refs))
                 for k, (src, dst) in enumerate(zip(_chunks(t_ref, n_ch), _chunks(_half(o_ref, c), n_ch)))]
        for cp in sends:
            cp.start()
        for t_ref, o_ref, buf in zip(t_refs, o_refs, bufs):
            _staged_copy(t_ref, _half(o_ref, c), buf, loc_sems)
        for g, o_ref in enumerate(o_refs):
            for k, landed in enumerate(_chunks(_half(o_ref, 1 - c), n_ch)):
                _remote(landed, landed, send_sems, recv_sems, g * n_ch + k, sib).wait_recv()
        for cp in sends:
            cp.wait_send()

    return pl.pallas_call(
        body, name=name, in_specs=[ANY] * n, out_specs=[ANY] * n,
        out_shape=[jax.ShapeDtypeStruct((2 * t.shape[0], t.shape[1]), t.dtype) for t in totals],
        scratch_shapes=[pltpu.SemaphoreType.DMA((n * n_ch,)), pltpu.SemaphoreType.DMA((n * n_ch,))] + _stage_scratch(totals),
        compiler_params=_params(),
    )(*totals)


def _allgather8(v, *, name):
    rows, cols = v.shape

    def body(v_ref, o_ref, send_sems, recv_sems, loc_sem):
        x, y, c, chips = _place()
        sib = (x, y, 1 - c)

        def slot(px, py, pc):
            return o_ref.at[4 * px + 2 * py + pc]

        local = pltpu.make_async_copy(v_ref, slot(x, y, c), loc_sem.at[0])
        local.start()
        first = [_remote(v_ref, slot(x, y, c), send_sems, recv_sems, 0, sib)]
        first += [_remote(v_ref, slot(x, y, c), send_sems, recv_sems, 1 + j, (*chip, c)) for j, chip in enumerate(chips)]
        for cp in first:
            cp.start()
        passed = [_remote(slot(*chip, c), slot(*chip, c), send_sems, recv_sems, 4 + j, sib)
                  for j, chip in enumerate(chips)]
        for j, chip in enumerate(chips):
            _remote(v_ref, slot(*chip, c), send_sems, recv_sems, 1 + j, sib).wait_recv()
            passed[j].start()
        _remote(v_ref, slot(x, y, 1 - c), send_sems, recv_sems, 0, sib).wait_recv()
        for j, chip in enumerate(chips):
            _remote(v_ref, slot(*chip, 1 - c), send_sems, recv_sems, 4 + j, sib).wait_recv()
        for cp in first + passed:
            cp.wait_send()
        local.wait()

    return pl.pallas_call(
        body, name=name, in_specs=[ANY], out_specs=ANY, out_shape=jax.ShapeDtypeStruct((N_DEV, rows, cols), v.dtype),
        scratch_shapes=[pltpu.SemaphoreType.DMA((7,)), pltpu.SemaphoreType.DMA((7,)), pltpu.SemaphoreType.DMA((1,))],
    )(v)


_BIG = (("w_mod", (D, 6 * D), 1), ("w_in", (D, IN_W), 1), ("w_branch", (3 * D, D), None), ("w_out", (D, D), 0),
        ("w_up", (D, 2 * D_FF), 1), ("w_down", (D_FF, D), 0))
_COL_SHARDED = ("w_mod", "w_in", "w_up")
_ROW_SHARDED = (("w_branch", 3 * D // N_CHIPS), ("w_out", D // N_CHIPS), ("w_down", D_FF // N_CHIPS))


def _pack_shards(shards, layer):
    rows = jnp.concatenate([shards[n][layer].reshape(r, D) for n, r in _ROW_SHARDED], axis=0)
    return [shards[n][layer] for n in _COL_SHARDED] + [rows]


def _unpack_cols(blk):
    return blk.transpose(1, 0, 2).reshape(blk.shape[1], N_CHIPS * blk.shape[2])


def _unpack_rows(stack):
    out, off = {}, 0
    for name, r in _ROW_SHARDED:
        blk = stack[:, off:off + r, :]
        off += r
        if name == "w_branch":
            out[name] = blk.reshape(N_CHIPS, 3, D // N_CHIPS, D).transpose(1, 0, 2, 3).reshape(3, D, D)
        else:
            out[name] = blk.reshape(N_CHIPS * r, D)
    return out


def _unpack_full(gathered):
    out = {name: _unpack_cols(blk) for name, blk in zip(_COL_SHARDED, gathered)}
    out.update(_unpack_rows(gathered[-1]))
    return out


def _pack_grad_cols(g):
    return g.reshape(g.shape[0], N_CHIPS, g.shape[1] // N_CHIPS).transpose(1, 0, 2)


def _pack_grad_rows(grads):
    parts = []
    for name, r in _ROW_SHARDED:
        g = grads[name]
        if name == "w_branch":
            g = g.reshape(3, N_CHIPS, D // N_CHIPS, D).transpose(1, 0, 2, 3)
        parts.append(g.reshape(N_CHIPS, r, D))
    return jnp.concatenate(parts, axis=1)


def _pack_grads(grads):
    return [_pack_grad_cols(grads[n]) for n in _COL_SHARDED] + [_pack_grad_rows(grads)]


def _unpack_shards(totals, like):
    out = {n: jnp.stack([totals[l][g] for l in range(DEPTH)]) for g, n in enumerate(_COL_SHARDED)}
    off = 0
    for name, r in _ROW_SHARDED:
        out[name] = jnp.stack([totals[l][-1][off:off + r] for l in range(DEPTH)]).reshape(like[name].shape)
        off += r
    return out


def _pad_rows(v, rows):
    return jnp.concatenate([v, jnp.zeros((rows - v.shape[0],) + v.shape[1:], v.dtype)], axis=0)


def _local_step(x_tok, target, c_vec, c_ctx, wfull, small, n_lat, n_ctx):
    ctx = _step_context(c_vec, c_ctx, n_lat, n_ctx)
    saved = []
    xs = x_tok
    for l in range(DEPTH):
        xs, s, _ = _layer_fwd(l, xs, wfull[l], {k: v[l] for k, v in small.items() if k != "g_final"}, ctx)
        saved.append(s)
    dx, sq_err, d_g_final = _loss_bwd(xs, target, small["g_final"], n_lat)
    wgrads, lgrads, d_a128 = [None] * DEPTH, [None] * DEPTH, [None] * DEPTH
    for l in reversed(range(DEPTH)):
        dx, wgrads[l], lgrads[l], d_a128[l], _ = _layer_bwd(l, saved[l], wfull[l], dx, ctx)
    return sq_err, dx, wgrads, _small_grads(lgrads, d_a128, d_g_final, ctx)


def _step_context(c_vec, c_ctx, n_lat, n_ctx):
    cos_t, sin_t = _rope_tables(n_lat, n_ctx)
    a_in = _pad_rows(jnp.stack([c_vec, c_ctx]), LANES)
    a128 = _small(_silu, (LANES, D), a_in, name="cond_silu")
    return dict(cos_t=cos_t, sin_t=sin_t, a_in=a_in, a128=a128, n_lat=n_lat, n_ctx=n_ctx)


def _loss_bwd(xs, target, g_final, n_lat):
    dx, st = _loss_head(xs, target, g_final[None, :], n_lat, name="loss_head")
    return dx, st[1], st[0]


def _small_grads(lgrads, d_a128, d_g_final, ctx):
    d_cond = _small(lambda a, b, cin: (a + b) * _dsilu(cin), (LANES, D), d_a128[0], d_a128[1], ctx["a_in"],
                    name="cond_bwd")
    out = {k: jnp.stack([lgrads[l][k] for l in range(DEPTH)]) for k in lgrads[0]}
    out["c_ctx"] = d_cond[1]
    out["g_final"] = d_g_final
    return out


def _layer_fwd(l, xs, w, sm, ctx, hosted=_NO_EXCHANGE, hosted_gating=_NO_EXCHANGE, hosted_ffn=_NO_EXCHANGE,
               late_weights=None):
    n_lat, n_ctx, cos_t, sin_t, a128 = ctx["n_lat"], ctx["n_ctx"], ctx["cos_t"], ctx["sin_t"], ctx["a128"]
    mod128 = _mm(a128, w["w_mod"], name=f"mod{l}")
    mod8 = _small(lambda m, b: m + b, (SUBLANES, 6 * D), mod128[:SUBLANES], sm["b_mod"][None, :], name=f"mod_bias{l}")
    g_mix = sm["g_mix"][None, :]
    g_ffn = sm["g_ffn"][None, :]
    g_v = sm["g_v"][None, :]
    b_gate = sm["b_gate"][None, :]
    sink_b = jnp.broadcast_to(sm["sink"][:, None], (N_HEADS, LANES))
    b_sb = jnp.broadcast_to(sm["b_spatial"][:, :, None], (A_GROUPS, BLK, LANES))
    w_sconv8 = _pad_rows(sm["w_sconv"], SUBLANES)
    w_fconv8 = _pad_rows(sm["w_fconv"], SUBLANES)
    w_in = w["w_in"]
    w_seg = [w_in[:, SEG[k]:SEG[k + 1]] for k in range(4)]

    h = _norm_mod_fwd(xs, g_mix, mod8, 0, 1, n_lat, name=f"norm1_{l}")
    q, kd, vd = _qkv_proj(h, w_seg[0], cos_t, sin_t, name=f"in_proj_qkv{l}")
    z_a, z_b, z_g = [_mm(h, w_seg[k], name=f"in_proj{k}_{l}") for k in range(1, 4)]
    y_attn, carried = _attention_fwd(q, kd, vd, sink_b, n_lat, n_ctx, name=f"attn{l}", hosted=hosted)
    if late_weights is not None:
        w = dict(w, **late_weights(carried))
    y_a, carried_gating = _gating_fwd(z_a, sm["w_spatial"], b_sb, g_v, name=f"gating{l}", hosted=hosted_gating)
    y_b = _sconv_fwd(z_b, w_sconv8, n_lat, name=f"sconv{l}")
    ys = (y_attn, y_a, y_b)
    ts, merged = _branch_merge_fwd(ys, w["w_branch"], z_g, b_gate, name=f"branch_merge{l}")
    mix_out, x1, h2 = _mm_residual(merged, w["w_out"], xs, mod8, 2, n_lat, name=f"out_proj_res1_norm2_{l}",
                                   norm=(g_ffn, 3, 4))
    up = _mm(h2, w["w_up"], name=f"up_proj{l}")
    cv, f, carried_ffn = _ffn_mid_fwd(up, w_fconv8, n_lat, name=f"ffn_mid{l}", hosted=hosted_ffn)
    ffn_out, x2, _ = _mm_residual(f, w["w_down"], x1, mod8, 5, n_lat, name=f"down_proj_res2_{l}")
    saved = dict(x0=xs, mod8=mod8, h=h, z_a=z_a, z_b=z_b, z_g=z_g, q=q, kd=kd, vd=vd, ys=ys, ts=ts,
                 merged=merged, mix_out=mix_out, x1=x1, h2=h2, up=up, cv=cv, f=f, ffn_out=ffn_out, w_seg=w_seg,
                 g_mix=g_mix, g_ffn=g_ffn, g_v=g_v, b_gate=b_gate, sink_b=sink_b, b_sb=b_sb,
                 w_sconv8=w_sconv8, w_fconv8=w_fconv8, w_spatial=sm["w_spatial"])
    return x2, saved, (carried, carried_gating, carried_ffn)


def _layer_bwd(l, s, w, dx, ctx, hosts=None):
    n_lat, n_ctx, cos_t, sin_t, a128 = ctx["n_lat"], ctx["n_ctx"], ctx["cos_t"], ctx["sin_t"], ctx["a128"]
    mod8 = s["mod8"]
    d_ffn, st_gt2 = _residual_bwd(dx, s["ffn_out"], mod8, 5, n_lat, name=f"res2_bwd{l}")
    df = _mm(d_ffn, w["w_down"], tb=True, name=f"down_bwd_x{l}")
    g_down = _mm(s["f"], d_ffn, ta=True, out_dtype=BF16, name=f"down_bwd_w{l}")
    d_up, st_fc = _ffn_mid_bwd(s["up"], s["cv"], df, s["w_fconv8"], n_lat, name=f"ffn_mid_bwd{l}")
    dh2 = _mm(d_up, w["w_up"], tb=True, name=f"up_bwd_x{l}")
    g_up = _mm(s["h2"], d_up, ta=True, out_dtype=BF16, name=f"up_bwd_w{l}")
    dx1, st_n2, _, d_out = _norm_mod_bwd(s["x1"], [dh2], dx, s["g_ffn"], mod8, 4, n_lat, name=f"norm2_res1_bwd{l}",
                                         res=(s["mix_out"], 2))
    st_gt1 = st_n2[5:7]
    d_merged = _mm(d_out, w["w_out"], tb=True, name=f"out_bwd_x{l}")
    g_out = _mm(s["merged"], d_out, ta=True, out_dtype=BF16, name=f"out_bwd_w{l}")
    dt0, dt1, dt2, dz_g, st_bg = _merge_bwd(d_merged, s["ts"], s["z_g"], s["b_gate"], name=f"merge_bwd{l}")
    dts = (dt0, dt1, dt2)
    dys = [_mm(dts[k], w["w_branch"][k], tb=True, name=f"branch{k}_bwd_x{l}") for k in range(3)]
    g_branch = jnp.stack([_mm(s["ys"][k], dts[k], ta=True, out_dtype=BF16, name=f"branch{k}_bwd_w{l}")
                          for k in range(3)])
    early = dict(w_branch=g_branch.reshape(3 * D, D), w_out=g_out, w_up=g_up, w_down=g_down)
    hosts = hosts or {}
    in_attn, in_gating = hosts["early"](early) if "early" in hosts else (_NO_EXCHANGE, _NO_EXCHANGE)
    carried = {}
    dq, dk, dv, d_sink, carried["attn"] = _attention_bwd(s["q"], s["kd"], s["vd"], s["sink_b"], dys[0], n_lat, n_ctx,
                                                         name=f"attn_bwd{l}", hosted=in_attn)
    dz_qkv = _qkv_unprep(dq, dk, dv, cos_t, sin_t, name=f"qkv_unprep{l}")
    dz_a, d_ws, d_bs, st_gv, carried["gating"] = _gating_bwd(s["z_a"], dys[1], s["w_spatial"], s["b_sb"], s["g_v"],
                                                             name=f"gating_bwd{l}", hosted=in_gating)
    dz_b, st_sc = _sconv_bwd(s["z_b"], dys[2], s["w_sconv8"], n_lat, name=f"sconv_bwd{l}")
    dzs = (dz_qkv, dz_a, dz_b, dz_g)
    g_in = jnp.concatenate([_mm(s["h"], dzs[k], ta=True, out_dtype=BF16, name=f"in_bwd_w{k}_{l}")
                            for k in range(4)], axis=1)
    dh_parts = [_mm(dzs[k], s["w_seg"][k], tb=True, name=f"in_bwd_x{k}_{l}") for k in range(4)]
    in_norm1 = hosts["w_in"](g_in) if "w_in" in hosts else _NO_EXCHANGE
    dx0, st_n1, carried["norm1"], _ = _norm_mod_bwd(s["x0"], dh_parts, dx1, s["g_mix"], mod8, 1, n_lat,
                                                    name=f"norm1_bwd{l}", hosted=in_norm1)
    dmod = jnp.concatenate([st_n1[0:2], st_n1[2:4], st_gt1[0:2], st_n2[0:2], st_n2[2:4], st_gt2[0:2]], axis=1)
    dmod128 = _pad_rows(dmod, LANES)
    g_mod = _mm(a128, dmod128, ta=True, out_dtype=BF16, name=f"mod_bwd_w{l}")
    d_a128 = _mm(dmod128, w["w_mod"], tb=True, name=f"mod_bwd_x{l}")
    wgrads = dict(early, w_mod=g_mod, w_in=g_in)
    lgrads = dict(b_mod=dmod[0] + dmod[1], g_mix=st_n1[4], g_ffn=st_n2[4], b_gate=st_bg[0], sink=d_sink[:, 0],
                  w_spatial=d_ws, b_spatial=d_bs[:, :, 0], g_v=st_gv[0], w_sconv=st_sc[0:3], w_fconv=st_fc[0:3])
    return dx0, wgrads, lgrads, d_a128, carried


_SMALL_ORDER = ("c_ctx", "b_mod", "g_mix", "b_gate", "sink", "w_spatial", "b_spatial", "g_v", "w_sconv", "g_ffn",
                "w_fconv", "g_final")


def _flat_pack(parts, width):
    flat = jnp.concatenate([p.reshape(-1).astype(F32) for p in parts])
    rows = -(-flat.shape[0] // (width * SUBLANES)) * SUBLANES
    flat = jnp.concatenate([flat, jnp.zeros((rows * width - flat.shape[0],), F32)])
    return flat.reshape(rows, width)


def _flat_unpack(packed, likes):
    flat = packed.reshape(-1)
    out, off = [], 0
    for like in likes:
        n = math.prod(like.shape)
        out.append(flat[off:off + n].reshape(like.shape))
        off += n
    return out


def kernel(x, c, ctx, c_ctx, w_mod, b_mod, g_mix, w_in, b_gate, sink, w_spatial, b_spatial, g_v, w_sconv, w_branch, w_out, g_ffn, w_up, w_fconv, w_down, g_final, loss_target, m_c_ctx, m_w_mod, m_b_mod, m_g_mix, m_w_in, m_b_gate, m_sink, m_w_spatial, m_b_spatial, m_g_v, m_w_sconv, m_w_branch, m_w_out, m_g_ffn, m_w_up, m_w_fconv, m_w_down, m_g_final, v_c_ctx, v_w_mod, v_b_mod, v_g_mix, v_w_in, v_b_gate, v_sink, v_w_spatial, v_b_spatial, v_g_v, v_w_sconv, v_w_branch, v_w_out, v_g_ffn, v_w_up, v_w_fconv, v_w_down, v_g_final):
    n_lat, n_ctx = x.shape[1], ctx.shape[1]
    chip = 2 * lax.axis_index("x") + lax.axis_index("y")
    weights = dict(c_ctx=c_ctx, w_mod=w_mod, b_mod=b_mod, g_mix=g_mix, w_in=w_in, b_gate=b_gate, sink=sink,
                   w_spatial=w_spatial, b_spatial=b_spatial, g_v=g_v, w_sconv=w_sconv, w_branch=w_branch, w_out=w_out,
                   g_ffn=g_ffn, w_up=w_up, w_fconv=w_fconv, w_down=w_down, g_final=g_final)
    m_in = dict(c_ctx=m_c_ctx, w_mod=m_w_mod, b_mod=m_b_mod, g_mix=m_g_mix, w_in=m_w_in, b_gate=m_b_gate, sink=m_sink,
                w_spatial=m_w_spatial, b_spatial=m_b_spatial, g_v=m_g_v, w_sconv=m_w_sconv, w_branch=m_w_branch,
                w_out=m_w_out, g_ffn=m_g_ffn, w_up=m_w_up, w_fconv=m_w_fconv, w_down=m_w_down, g_final=m_g_final)
    v_in = dict(c_ctx=v_c_ctx, w_mod=v_w_mod, b_mod=v_b_mod, g_mix=v_g_mix, w_in=v_w_in, b_gate=v_b_gate, sink=v_sink,
                w_spatial=v_w_spatial, b_spatial=v_b_spatial, g_v=v_g_v, w_sconv=v_w_sconv, w_branch=v_w_branch,
                w_out=v_w_out, g_ffn=v_g_ffn, w_up=v_w_up, w_fconv=v_w_fconv, w_down=v_w_down, g_final=v_g_final)
    big_names = [n for n, _, _ in _BIG]

    conv_pack = _flat_pack([w_sconv, w_fconv], LANES)
    conv_all = _allgather8(conv_pack, name="gather_conv_weights")
    conv_parts = [_flat_unpack(conv_all[2 * p], [w_sconv, w_fconv]) for p in range(N_CHIPS)]
    w_sconv_full = jnp.concatenate([cp[0] for cp in conv_parts], axis=-1)
    w_fconv_full = jnp.concatenate([cp[1] for cp in conv_parts], axis=-1)

    small = dict(b_mod=b_mod, g_mix=g_mix, b_gate=b_gate, sink=sink, w_spatial=w_spatial, b_spatial=b_spatial, g_v=g_v,
                 w_sconv=w_sconv_full, g_ffn=g_ffn, w_fconv=w_fconv_full, g_final=g_final)
    x_tok = jnp.concatenate([x[0], ctx[0]], axis=0)
    step = _step_context(c[0], c_ctx, n_lat, n_ctx)
    layer_small = [{k: v[l] for k, v in small.items() if k != "g_final"} for l in range(DEPTH)]
    my_half = lax.axis_index("c").astype(jnp.int32).reshape(1)

    shards = {n: weights[n].astype(BF16) for n in big_names}
    pack = [_pack_shards(shards, l) for l in range(DEPTH)]
    first = _gather_finish(_gather_ici(pack[0][:2], name="gather_ici0"), pack[0][:2], name="gather_finish0")
    w0 = dict(w_mod=_unpack_cols(first[0]), w_in=_unpack_cols(first[1]))

    def layer0_late_weights(carried):
        rest = _gather_finish(list(carried[1:]), pack[0][2:], name="gather_finish0_late")
        w0.update(w_up=_unpack_cols(rest[0]), **_unpack_rows(rest[1]))
        return w0

    xs, saved0, (part_attn, part_gating, part_ffn) = _layer_fwd(
        0, x_tok, w0, layer_small[0], step, hosted=_gather_ici_hosted(pack[1][1:2] + pack[0][2:]),
        hosted_gating=_gather_ici_hosted(pack[1][:1]), hosted_ffn=_gather_ici_hosted(pack[1][2:]),
        late_weights=layer0_late_weights)
    partial1 = list(part_gating) + list(part_attn[:1]) + list(part_ffn)
    w1 = _unpack_full(_gather_finish(partial1, pack[1], name="gather_finish1"))
    xs, saved1, _ = _layer_fwd(1, xs, w1, layer_small[1], step)
    dx, sq_err, d_g_final = _loss_bwd(xs, loss_target[0], g_final, n_lat)
    loss = lax.psum(0.5 * jnp.sum(sq_err) / D, ("x", "y", "c"))

    def reduce_start(g_packs, tag):
        got = _grad_sibling_swap(g_packs, name=f"grad_sibling_swap{tag}")
        return [_add_half(my_half, a, b, name=f"grad_pair_sum{tag}_{g}") for g, (a, b) in enumerate(zip(g_packs, got))]

    def reduce_finish(exchanged, tag):
        sums = [_sum_slabs(e, F32, name=f"grad_chip_sum{tag}_{g}") for g, e in enumerate(exchanged)]
        return _grad_sibling_share(sums, name=f"grad_sibling_share{tag}")

    dx, wgrads1, lgrads1, d_a1, _ = _layer_bwd(1, saved1, w1, dx, step)
    pair_sum1 = reduce_start(_pack_grads(wgrads1), "1")

    def carried_early(early):
        pair_sum0_early = reduce_start([_pack_grad_cols(early["w_up"]), _pack_grad_rows(early)], "0_early")
        return _grad_ici_hosted(pair_sum1 + pair_sum0_early[1:]), _grad_ici_hosted(pair_sum0_early[:1])

    def carried_w_in(g_in):
        return _grad_ici_hosted(reduce_start([_pack_grad_cols(g_in)], "0_in"))

    dx, wgrads0, lgrads0, d_a0, exchanged = _layer_bwd(0, saved0, w0, dx, step,
                                                       hosts=dict(early=carried_early, w_in=carried_w_in))
    total1 = reduce_finish(exchanged["attn"][:4], "1")
    total0_early = reduce_finish(list(exchanged["gating"]) + list(exchanged["attn"][4:]), "0_early")
    total0_in = reduce_finish(exchanged["norm1"], "0_in")
    mod_sum = reduce_start([_pack_grad_cols(wgrads0["w_mod"])], "0_mod")
    sgrads = _small_grads([lgrads0, lgrads1], [d_a0, d_a1], d_g_final, step)
    grad_x = dx[:n_lat][None]

    s_likes = [sgrads[n] for n in _SMALL_ORDER]
    s_all = _allgather8(_flat_pack(s_likes, D), name="gather_small_grads")
    s_tot = _flat_unpack(_sum_slabs(s_all, F32, name="small_grad_sum"), s_likes)
    grads = dict(zip(_SMALL_ORDER, s_tot))
    grads["w_sconv"] = lax.dynamic_slice_in_dim(grads["w_sconv"], chip * w_sconv.shape[-1], w_sconv.shape[-1], axis=2)
    grads["w_fconv"] = lax.dynamic_slice_in_dim(grads["w_fconv"], chip * w_fconv.shape[-1], w_fconv.shape[-1], axis=2)

    delta, new_m, new_v = {}, {}, {}

    def adamw(n, hosted=None):
        cols = weights[n].shape[-1]
        view = lambda a: a.reshape(-1, cols)
        d_, m_, v_, carried = _adamw(view(weights[n]), view(grads[n]), view(m_in[n]), view(v_in[n]), name=f"adamw_{n}",
                                     hosted=hosted)
        delta[n], new_m[n], new_v[n] = (t.reshape(weights[n].shape) for t in (d_, m_, v_))
        return carried

    grads["w_in"] = jnp.stack([total0_in[0], total1[1]])
    total0_mod = reduce_finish(adamw("w_in", _grad_ici_hosted(mod_sum)), "0_mod")
    total0 = list(total0_mod) + list(total0_in) + list(total0_early)
    rest = _unpack_shards([total0, total1], {n: weights[n] for n in big_names})
    grads.update({n: g for n, g in rest.items() if n != "w_in"})
    for n in big_names:
        if n != "w_in":
            adamw(n)
    likes = [weights[n] for n in _SMALL_ORDER]
    packs = [_flat_pack([src[n] for n in _SMALL_ORDER], D) for src in (weights, grads, m_in, v_in)]
    outs = _adamw(*packs, name="adamw_small")[:3]
    for dst, packed in zip((delta, new_m, new_v), outs):
        for n, val in zip(_SMALL_ORDER, _flat_unpack(packed, likes)):
            dst[n] = val

    order = ("c_ctx", "w_mod", "b_mod", "g_mix", "w_in", "b_gate", "sink", "w_spatial", "b_spatial", "g_v", "w_sconv",
             "w_branch", "w_out", "g_ffn", "w_up", "w_fconv", "w_down", "g_final")
    return (loss, grad_x, *[grads[n] for n in order], *[delta[n] for n in order], *[new_m[n] for n in order],
            *[new_v[n] for n in order])
```

```python
import functools
import math

import jax
import jax.numpy as jnp
from jax import lax
from jax.experimental import pallas as pl
from jax.experimental.pallas import tpu as pltpu

F32 = jnp.float32
BF16 = jnp.bfloat16

D = 1024
DEPTH = 2
GRID_W = 64
N_HEADS = 16
N_KV = 4
GRP = N_HEADS // N_KV
HEAD_DIM = 64
KV_W = N_KV * HEAD_DIM
WINDOW = 128
BLK = 128
ROPE_THETA = 10000.0
A_GROUPS = 8
D_FF = 2816
EPS = 1e-6
NEG = -1e30
QKV_W = D + 2 * KV_W
A_COLS = 2 * D
B_COLS = 3 * D
G_COLS = 3 * D
IN_W = QKV_W + A_COLS + B_COLS + G_COLS
SEG = (0, QKV_W, QKV_W + A_COLS, QKV_W + A_COLS + B_COLS, IN_W)
N_CHIPS = 4
N_DEV = 8
LANES = 128
SUBLANES = 8
VMEM_LIMIT = 48 * 1024 * 1024
VMEM_LIMIT_WIDE = 56 * 1024 * 1024
ADAM_LR = 0.001
ADAM_B1 = 0.9
ADAM_B2 = 0.999
ADAM_EPS = 1e-08
ADAM_WD = 0.01
ADAM_STEP = 10
MESH = pl.DeviceIdType.MESH
ANY = pl.BlockSpec(memory_space=pl.ANY)


def _params(sem=None, vmem=VMEM_LIMIT):
    return pltpu.CompilerParams(dimension_semantics=sem, vmem_limit_bytes=vmem)


def _pick(n, cands):
    for c in cands:
        if n % c == 0:
            return c
    return n


def _rows8(rows, width):
    r = lax.broadcasted_iota(jnp.int32, (SUBLANES, width), 0)
    out = jnp.zeros((SUBLANES, width), F32)
    for idx, v in rows:
        out = out + jnp.where(r == idx, v, 0.0)
    return out


def _sel(mod_ref, k, is_ctx):
    return jnp.where(is_ctx, mod_ref[1:2, k * D:(k + 1) * D], mod_ref[0:1, k * D:(k + 1) * D])


def _colsum(v):
    return jnp.sum(v, axis=0, keepdims=True)


def _mm(a, b, *, name, ta=False, tb=False, out_dtype=F32):
    if ta:
        k_dim, m = a.shape
    else:
        m, k_dim = a.shape
    if tb:
        n, kb = b.shape
    else:
        kb, n = b.shape
    assert k_dim == kb, (a.shape, b.shape, ta, tb)
    tm = _pick(m, (1056, 1024, 1408, 768, 512, 256, 128))
    tn = _pick(n, (1536, 1408, 1024, 768, 512, 256, 128))
    tk = _pick(k_dim, (2048, 1536, 1408, 1024, 768, 512, 256, 128))
    nk = k_dim // tk
    dims = (((0 if ta else 1,), (1 if tb else 0,)), ((), ()))

    def product(a_ref, b_ref):
        return lax.dot_general(a_ref[...].astype(BF16), b_ref[...].astype(BF16), dims, preferred_element_type=F32)

    def body_single(a_ref, b_ref, o_ref):
        o_ref[...] = product(a_ref, b_ref).astype(o_ref.dtype)

    def body_acc(a_ref, b_ref, o_ref, acc_ref):
        k = pl.program_id(2)

        @pl.when(k == 0)
        def _():
            acc_ref[...] = product(a_ref, b_ref)

        @pl.when(k > 0)
        def _():
            acc_ref[...] += product(a_ref, b_ref)

        @pl.when(k == nk - 1)
        def _():
            o_ref[...] = acc_ref[...].astype(o_ref.dtype)

    a_spec = pl.BlockSpec((tk, tm), lambda i, j, k: (k, i)) if ta else pl.BlockSpec((tm, tk), lambda i, j, k: (i, k))
    b_spec = pl.BlockSpec((tn, tk), lambda i, j, k: (j, k)) if tb else pl.BlockSpec((tk, tn), lambda i, j, k: (k, j))
    return pl.pallas_call(
        body_single if nk == 1 else body_acc, name=name, grid=(m // tm, n // tn, nk),
        in_specs=[a_spec, b_spec], out_specs=pl.BlockSpec((tm, tn), lambda i, j, k: (i, j)),
        out_shape=jax.ShapeDtypeStruct((m, n), out_dtype),
        scratch_shapes=[] if nk == 1 else [pltpu.VMEM((tm, tn), F32)],
        compiler_params=_params(("parallel", "parallel", "arbitrary")),
    )(a, b)


def _mm_residual(a, b, x, mod8, gt_idx, n_lat, *, name, norm=None):
    t_rows, k_dim = a.shape
    tm = _pick(t_rows, (768, 512, 256))
    tk = _pick(k_dim, (1408, 1024, 512, 256, 128))
    nk = k_dim // tk
    fused = norm is not None

    def body(a_ref, b_ref, x_ref, mod_ref, *rest):
        g_ref = rest[0] if fused else None
        br_ref, xo_ref = rest[fused:fused + 2]
        acc_ref = rest[-1]
        i, k = pl.program_id(0), pl.program_id(1)
        part = jnp.dot(a_ref[...].astype(BF16), b_ref[...].astype(BF16), preferred_element_type=F32)

        @pl.when(k == 0)
        def _():
            acc_ref[...] = part

        @pl.when(k > 0)
        def _():
            acc_ref[...] += part

        @pl.when(k == nk - 1)
        def _():
            row_is_ctx = i * tm + lax.broadcasted_iota(jnp.int32, (tm, 1), 0) >= n_lat

            def sel(idx):
                return jnp.where(row_is_ctx, mod_ref[1:2, idx * D:(idx + 1) * D], mod_ref[0:1, idx * D:(idx + 1) * D])

            branch = acc_ref[...]
            br_ref[...] = branch
            xv = x_ref[...] + sel(gt_idx) * branch
            xo_ref[...] = xv
            if fused:
                rstd = lax.rsqrt(jnp.mean(xv * xv, axis=-1, keepdims=True) + EPS)
                y = xv * rstd * g_ref[...]
                rest[3][...] = (y * (1.0 + sel(norm[2])) + sel(norm[1])).astype(BF16)

    row = pl.BlockSpec((tm, D), lambda i, k: (i, 0))
    whole = lambda shape: pl.BlockSpec(shape, lambda i, k: (0, 0))
    outs = pl.pallas_call(
        body, name=name, grid=(t_rows // tm, nk),
        in_specs=[pl.BlockSpec((tm, tk), lambda i, k: (i, k)), pl.BlockSpec((tk, D), lambda i, k: (k, 0)), row,
                  whole((SUBLANES, 6 * D))] + [whole((1, D))] * fused,
        out_specs=[row, row] + [row] * fused,
        out_shape=[jax.ShapeDtypeStruct((t_rows, D), F32)] * 2 + [jax.ShapeDtypeStruct((t_rows, D), BF16)] * fused,
        scratch_shapes=[pltpu.VMEM((tm, D), F32)],
        compiler_params=_params(("parallel", "arbitrary")),
    )(a, b, x, mod8, *([norm[0]] if fused else []))
    return outs[0], outs[1], (outs[2] if fused else None)


def _small(fn, out_shape, *arrays, name):
    def body(*refs):
        refs[-1][...] = fn(*[r[...] for r in refs[:-1]]).astype(refs[-1].dtype)

    return pl.pallas_call(body, name=name, out_shape=jax.ShapeDtypeStruct(out_shape, F32))(*arrays)


def _silu(v):
    return v * jax.nn.sigmoid(v)


def _dsilu(v):
    s = jax.nn.sigmoid(v)
    return s * (1.0 + v * (1.0 - s))


def _row_spec(tm, width, col=0):
    return pl.BlockSpec((tm, width), lambda i: (i, col))


def _full_spec(shape):
    nd = len(shape)
    return pl.BlockSpec(shape, lambda i: (0,) * nd)


def _halo_specs(tm, width, t_rows, col=0):
    per = tm // SUBLANES
    last = t_rows // SUBLANES - 1
    prev = pl.BlockSpec((SUBLANES, width), lambda i: (jnp.maximum(i * per - 1, 0), col))
    nxt = pl.BlockSpec((SUBLANES, width), lambda i: (jnp.minimum((i + 1) * per, last), col))
    return prev, nxt


def _shift_rows(cur, prev8, next8, n_lat, t_rows, tm):
    i = pl.program_id(0)
    row = lax.broadcasted_iota(jnp.int32, (tm, 1), 0)
    g = row + i * tm
    up = pltpu.roll(cur, 1, 0)
    up = jnp.where(row == 0, prev8[SUBLANES - 1:SUBLANES, :], up)
    up = jnp.where((g == 0) | (g == n_lat), 0.0, up)
    dn = pltpu.roll(cur, tm - 1, 0)
    dn = jnp.where(row == tm - 1, next8[0:1, :], dn)
    dn = jnp.where((g == n_lat - 1) | (g == t_rows - 1), 0.0, dn)
    return up, dn


def _norm_mod_fwd(x, g, mod8, sh_idx, sc_idx, n_lat, *, name, hosted=None):
    hosted = hosted or _NO_EXCHANGE
    t_rows = x.shape[0]
    tm = 256
    n_steps = t_rows // tm

    def body(*refs):
        ins, outs, _, h_refs = _split_refs(refs, 3, 1, 0, hosted)
        x_ref, g_ref, mod_ref = ins
        o_ref, = outs
        _run_hosted(hosted, h_refs, pl.program_id(0), n_steps)
        is_ctx = pl.program_id(0) * tm >= n_lat
        xv = x_ref[...]
        rstd = lax.rsqrt(jnp.mean(xv * xv, axis=-1, keepdims=True) + EPS)
        y = xv * rstd * g_ref[...]
        o_ref[...] = (y * (1.0 + _sel(mod_ref, sc_idx, is_ctx)) + _sel(mod_ref, sh_idx, is_ctx)).astype(BF16)

    outs = pl.pallas_call(
        body, name=name, grid=(n_steps,),
        in_specs=[_row_spec(tm, D), _full_spec((1, D)), _full_spec((SUBLANES, 6 * D))] + [ANY] * len(hosted.arrays),
        out_specs=[_row_spec(tm, D)] + [ANY] * len(hosted.out_shapes),
        out_shape=[jax.ShapeDtypeStruct((t_rows, D), BF16)] + list(hosted.out_shapes),
        scratch_shapes=list(hosted.scratch),
        compiler_params=_params(("arbitrary",)),
    )(x, g, mod8, *hosted.arrays)
    return outs[0], outs[1:]


def _norm_mod_bwd(x, dh_parts, dres, g, mod8, sc_idx, n_lat, *, name, hosted=None, res=None):
    hosted = hosted or _NO_EXCHANGE
    t_rows = x.shape[0]
    tm = 256
    n_parts = len(dh_parts)
    n_steps = t_rows // tm
    fused = res is not None

    def body(*refs):
        ins, outs, _, h_refs = _split_refs(refs, 4 + n_parts + fused, 2 + fused, 0, hosted)
        x_ref, dres_ref, g_ref, mod_ref = ins[:4]
        part_refs = ins[4:4 + n_parts]
        dx_ref, st_ref = outs[:2]
        i = pl.program_id(0)
        _run_hosted(hosted, h_refs, i, n_steps)
        is_ctx = i * tm >= n_lat
        dh = part_refs[0][...]
        for p in part_refs[1:]:
            dh = dh + p[...]
        xv = x_ref[...]
        gv = g_ref[...]
        rstd = lax.rsqrt(jnp.mean(xv * xv, axis=-1, keepdims=True) + EPS)
        rn = xv * rstd
        dy = dh * (1.0 + _sel(mod_ref, sc_idx, is_ctx))
        e = dy * gv
        dxv = dres_ref[...] + rstd * (e - rn * jnp.mean(e * rn, axis=-1, keepdims=True))
        dx_ref[...] = dxv
        dsh = _colsum(dh)
        dsc = _colsum(dh * (rn * gv))
        dg = _colsum(dy * rn)
        zero = jnp.zeros_like(dsh)
        rows = [(0, jnp.where(is_ctx, zero, dsh)), (1, jnp.where(is_ctx, dsh, zero)),
                (2, jnp.where(is_ctx, zero, dsc)), (3, jnp.where(is_ctx, dsc, zero)), (4, dg)]
        if fused:
            outs[2][...] = (dxv * _sel(mod_ref, res[1], is_ctx)).astype(BF16)
            dgt = _colsum(dxv * ins[4 + n_parts][...])
            rows += [(5, jnp.where(is_ctx, zero, dgt)), (6, jnp.where(is_ctx, dgt, zero))]
        upd = _rows8(rows, D)

        @pl.when(i == 0)
        def _():
            st_ref[...] = upd

        @pl.when(i > 0)
        def _():
            st_ref[...] += upd

    outs = pl.pallas_call(
        body, name=name, grid=(n_steps,),
        in_specs=[_row_spec(tm, D), _row_spec(tm, D), _full_spec((1, D)), _full_spec((SUBLANES, 6 * D))]
        + [_row_spec(tm, D)] * (n_parts + fused) + [ANY] * len(hosted.arrays),
        out_specs=[_row_spec(tm, D), _full_spec((SUBLANES, D))] + [_row_spec(tm, D)] * fused
        + [ANY] * len(hosted.out_shapes),
        out_shape=[jax.ShapeDtypeStruct((t_rows, D), F32), jax.ShapeDtypeStruct((SUBLANES, D), F32)]
        + [jax.ShapeDtypeStruct((t_rows, D), BF16)] * fused + list(hosted.out_shapes),
        scratch_shapes=list(hosted.scratch),
        compiler_params=_params(("arbitrary",)),
    )(x, dres, g, mod8, *dh_parts, *([res[0]] if fused else []), *hosted.arrays)
    return outs[0], outs[1], outs[2 + fused:], (outs[2] if fused else None)


def _residual_bwd(dx, branch, mod8, gt_idx, n_lat, *, name):
    t_rows = dx.shape[0]
    tm = 256

    def body(dx_ref, b_ref, mod_ref, o_ref, st_ref):
        i = pl.program_id(0)
        is_ctx = i * tm >= n_lat
        dxv = dx_ref[...]
        o_ref[...] = (dxv * _sel(mod_ref, gt_idx, is_ctx)).astype(BF16)
        dgt = _colsum(dxv * b_ref[...])
        zero = jnp.zeros_like(dgt)
        upd = _rows8([(0, jnp.where(is_ctx, zero, dgt)), (1, jnp.where(is_ctx, dgt, zero))], D)

        @pl.when(i == 0)
        def _():
            st_ref[...] = upd

        @pl.when(i > 0)
        def _():
            st_ref[...] += upd

    return pl.pallas_call(
        body, name=name, grid=(t_rows // tm,),
        in_specs=[_row_spec(tm, D), _row_spec(tm, D), _full_spec((SUBLANES, 6 * D))],
        out_specs=[_row_spec(tm, D), _full_spec((SUBLANES, D))],
        out_shape=[jax.ShapeDtypeStruct((t_rows, D), BF16), jax.ShapeDtypeStruct((SUBLANES, D), F32)],
        compiler_params=_params(("arbitrary",)),
    )(dx, branch, mod8)


def _rope_tables(n_lat, n_ctx):
    rows = n_lat // GRID_W
    row = jnp.broadcast_to(jnp.arange(rows, dtype=F32)[:, None], (rows, GRID_W)).reshape(n_lat)
    col = jnp.broadcast_to(jnp.arange(GRID_W, dtype=F32)[None, :], (rows, GRID_W)).reshape(n_lat)
    half = HEAD_DIM // 2
    inv = ROPE_THETA ** (-jnp.arange(0, half, 2, dtype=F32) / half)
    ang = jnp.concatenate([row[:, None] * inv, col[:, None] * inv], axis=-1)
    cos, sin = jnp.cos(ang), jnp.sin(ang)
    c64 = jnp.concatenate([cos, cos], axis=-1)
    s64 = jnp.concatenate([-sin, sin], axis=-1)
    c64 = jnp.concatenate([c64, jnp.ones((n_ctx, HEAD_DIM), F32)], axis=0)
    s64 = jnp.concatenate([s64, jnp.zeros((n_ctx, HEAD_DIM), F32)], axis=0)
    return jnp.tile(c64, (1, 2)), jnp.tile(s64, (1, 2))


def _swap_halves(v):
    lane = lax.broadcasted_iota(jnp.int32, v.shape, 1)
    return jnp.where(lane % HEAD_DIM < HEAD_DIM // 2, pltpu.roll(v, LANES - HEAD_DIM // 2, 1),
                     pltpu.roll(v, HEAD_DIM // 2, 1))


def _low_half(shape):
    return lax.broadcasted_iota(jnp.int32, shape, 1) < HEAD_DIM


def _qkv_proj(h, w_qkv, cos_t, sin_t, *, name):
    t_rows = h.shape[0]
    tm = _pick(t_rows, (768, 512, 256))

    def body(h_ref, w_ref, c_ref, s_ref, q_ref, k_ref, v_ref):
        z = jnp.dot(h_ref[...], w_ref[...], preferred_element_type=F32)
        cv, sv = c_ref[...], s_ref[...]

        def rope(chunk):
            return chunk * cv + _swap_halves(chunk) * sv

        for ch in range(D // LANES):
            roped = rope(z[:, ch * LANES:(ch + 1) * LANES])
            q_ref[:, ch * LANES:(ch + 1) * LANES] = (roped * (HEAD_DIM ** -0.5)).astype(BF16)
        low = _low_half((tm, LANES))
        for pair in range(N_KV // 2):
            for which, ref, roped in ((0, k_ref, True), (1, v_ref, False)):
                off = D + which * KV_W + pair * LANES
                chunk = z[:, off:off + LANES]
                if roped:
                    chunk = rope(chunk)
                other = pltpu.roll(chunk, HEAD_DIM, 1)
                even = jnp.where(low, chunk, other)
                odd = jnp.where(low, other, chunk)
                ref[:, (2 * pair) * LANES:(2 * pair + 1) * LANES] = even.astype(BF16)
                ref[:, (2 * pair + 1) * LANES:(2 * pair + 2) * LANES] = odd.astype(BF16)

    dup_w = N_KV * LANES
    return pl.pallas_call(
        body, name=name, grid=(t_rows // tm,),
        in_specs=[_row_spec(tm, D), _full_spec((D, QKV_W)), _row_spec(tm, LANES), _row_spec(tm, LANES)],
        out_specs=[_row_spec(tm, D), _row_spec(tm, dup_w), _row_spec(tm, dup_w)],
        out_shape=[jax.ShapeDtypeStruct((t_rows, D), BF16), jax.ShapeDtypeStruct((t_rows, dup_w), BF16),
                   jax.ShapeDtypeStruct((t_rows, dup_w), BF16)],
        compiler_params=_params(("parallel",)),
    )(h, w_qkv, cos_t, sin_t)


def _qkv_unprep(dq, dk, dv, cos_t, sin_t, *, name):
    t_rows = dq.shape[0]
    tm = 256

    def body(dq_ref, dk_ref, dv_ref, c_ref, s_ref, o_ref):
        cv, sv = c_ref[...], s_ref[...]

        def unrope(chunk):
            return chunk * cv + _swap_halves(chunk * sv)

        for ch in range(D // LANES):
            o_ref[:, ch * LANES:(ch + 1) * LANES] = unrope(dq_ref[:, ch * LANES:(ch + 1) * LANES]).astype(BF16)
        for pair in range(N_KV // 2):
            for which, ref, roped in ((0, dk_ref, True), (1, dv_ref, False)):
                chunk = ref[:, pair * LANES:(pair + 1) * LANES]
                if roped:
                    chunk = unrope(chunk)
                off = D + which * KV_W + pair * LANES
                o_ref[:, off:off + LANES] = chunk.astype(BF16)

    return pl.pallas_call(
        body, name=name, grid=(t_rows // tm,),
        in_specs=[_row_spec(tm, D), _row_spec(tm, KV_W), _row_spec(tm, KV_W), _row_spec(tm, LANES),
                  _row_spec(tm, LANES)],
        out_specs=_row_spec(tm, QKV_W), out_shape=jax.ShapeDtypeStruct((t_rows, QKV_W), BF16),
        compiler_params=_params(("parallel",)),
    )(dq, dk, dv, cos_t, sin_t)


def _attn_specs(n_lat, n_ctx):
    nb = n_lat // BLK
    dup_w = N_KV * LANES

    def ws(j):
        return jnp.clip(j - 1, 0, nb - 3)

    win = [pl.BlockSpec((BLK, dup_w), functools.partial(lambda j, o: (ws(j) + o, 0), o=o)) for o in range(3)]
    ctx = pl.BlockSpec((n_ctx, dup_w), lambda j: (n_lat // n_ctx, 0))
    return nb, ws, win, ctx


def _attn_bias(j, ws_j, nb, n_ctx):
    n_keys = 3 * BLK + n_ctx
    row = lax.broadcasted_iota(jnp.int32, (BLK, n_keys), 0)
    col = lax.broadcasted_iota(jnp.int32, (BLK, n_keys), 1)
    rel = (ws_j - j) * BLK + col - row
    valid = (col >= 3 * BLK) | ((jnp.abs(rel) <= WINDOW) & (j < nb))
    bias = jnp.where(valid, 0.0, NEG)
    return jnp.concatenate([bias] * GRP, axis=0)


def _attn_probs(q_ref, kk, kh, bias, sink_ref):
    low = _low_half((BLK, LANES))
    qs = []
    for g in range(GRP):
        h = GRP * kh + g
        chunk = q_ref[:, (h // 2) * LANES:(h // 2 + 1) * LANES]
        qs.append(jnp.where(low if h % 2 == 0 else ~low, chunk, jnp.zeros_like(chunk)))
    qs = jnp.concatenate(qs, axis=0)
    s = lax.dot_general(qs, kk, (((1,), (1,)), ((), ())), preferred_element_type=F32) + bias
    snk = jnp.concatenate(
        [jnp.broadcast_to(jnp.max(sink_ref[GRP * kh + g:GRP * kh + g + 1, :], axis=1, keepdims=True), (BLK, 1))
         for g in range(GRP)], axis=0)
    m = jnp.maximum(jnp.max(s, axis=-1, keepdims=True), snk)
    p = jnp.exp(s - m)
    p_snk = jnp.exp(snk - m)
    inv = 1.0 / (jnp.sum(p, axis=-1, keepdims=True) + p_snk)
    return qs, p, p_snk, inv


class _Hosted:
    def __init__(self, arrays, out_shapes, scratch, start, finish):
        self.arrays, self.out_shapes, self.scratch, self.start, self.finish = arrays, out_shapes, scratch, start, finish


_NO_EXCHANGE = _Hosted([], [], [], None, None)


def _split_refs(refs, n_in, n_out, n_scratch, hosted):
    hi, ho, hs = len(hosted.arrays), len(hosted.out_shapes), len(hosted.scratch)
    a = n_in + hi
    b = a + n_out + ho
    ins, h_ins = refs[:n_in], refs[n_in:a]
    outs, h_outs = refs[a:a + n_out], refs[a + n_out:b]
    scr, h_scr = refs[b:b + n_scratch], refs[b + n_scratch:b + n_scratch + hs]
    return ins, outs, scr, (h_ins, h_outs, h_scr)


def _run_hosted(hosted, h_refs, step, n_steps):
    if hosted.start is None:
        return

    flat = [r for group in h_refs for r in group]

    @pl.when(step == 0)
    def _():
        hosted.start(*flat)

    @pl.when(step == n_steps - 1)
    def _():
        hosted.finish(*flat)


def _attention_fwd(q, kd, vd, sink_b, n_lat, n_ctx, *, name, hosted=_NO_EXCHANGE):
    t_rows = q.shape[0]
    nb, ws, win, ctx = _attn_specs(n_lat, n_ctx)
    n_steps = t_rows // BLK

    def body(*refs):
        ins, outs, _, h_refs = _split_refs(refs, 10, 1, 0, hosted)
        q_ref, k0, k1, k2, kc, v0, v1, v2, vc, sink_ref = ins
        o_ref, = outs
        j = pl.program_id(0)
        _run_hosted(hosted, h_refs, j, n_steps)
        ws_j = ws(j)
        low = _low_half((BLK, LANES))
        bias = _attn_bias(j, ws_j, nb, n_ctx)
        for kh in range(N_KV):
            sl = slice(kh * LANES, (kh + 1) * LANES)
            kk = jnp.concatenate([k0[:, sl], k1[:, sl], k2[:, sl], kc[:, sl]], axis=0)
            vv = jnp.concatenate([v0[:, sl], v1[:, sl], v2[:, sl], vc[:, sl]], axis=0)
            _, p, _, inv = _attn_probs(q_ref, kk, kh, bias, sink_ref)
            o = jnp.dot(p.astype(BF16), vv, preferred_element_type=F32) * inv
            for half in range(2):
                even = o[(2 * half) * BLK:(2 * half + 1) * BLK]
                odd = o[(2 * half + 1) * BLK:(2 * half + 2) * BLK]
                ch = 2 * kh + half
                o_ref[:, ch * LANES:(ch + 1) * LANES] = jnp.where(low, even, odd).astype(BF16)

    outs = pl.pallas_call(
        body, name=name, grid=(n_steps,),
        in_specs=[_row_spec(BLK, D)] + win + [ctx] + win + [ctx] + [_full_spec((N_HEADS, LANES))]
        + [ANY] * len(hosted.arrays),
        out_specs=[_row_spec(BLK, D)] + [ANY] * len(hosted.out_shapes),
        out_shape=[jax.ShapeDtypeStruct((t_rows, D), BF16)] + list(hosted.out_shapes),
        scratch_shapes=list(hosted.scratch),
        compiler_params=_params(("arbitrary",)),
    )(q, kd, kd, kd, kd, vd, vd, vd, vd, sink_b, *hosted.arrays)
    return outs[0], outs[1:]


def _attention_bwd(q, kd, vd, sink_b, dy, n_lat, n_ctx, *, name, hosted=_NO_EXCHANGE):
    t_rows = q.shape[0]
    nb, ws, win, ctx = _attn_specs(n_lat, n_ctx)
    n_steps = t_rows // BLK

    def body(*refs):
        ins, outs, scr, h_refs = _split_refs(refs, 11, 4, 3, hosted)
        q_ref, k0, k1, k2, kc, v0, v1, v2, vc, sink_ref, dy_ref = ins
        dq_ref, dk_hbm, dv_hbm, ds_ref = outs
        dk_acc, dv_acc, sem = scr
        j = pl.program_id(0)
        _run_hosted(hosted, h_refs, j, n_steps)
        ws_j = ws(j)

        @pl.when(j == 0)
        def _():
            dk_acc[...] = jnp.zeros_like(dk_acc)
            dv_acc[...] = jnp.zeros_like(dv_acc)
            ds_ref[...] = jnp.zeros_like(ds_ref)

        low = _low_half((BLK, LANES))
        low_keys = _low_half((3 * BLK + n_ctx, LANES))
        win_start = pl.multiple_of(ws_j * BLK, BLK)
        scale = HEAD_DIM ** -0.5
        dk_heads, dv_heads = [], []
        bias = _attn_bias(j, ws_j, nb, n_ctx)
        for kh in range(N_KV):
            sl = slice(kh * LANES, (kh + 1) * LANES)
            kk = jnp.concatenate([k0[:, sl], k1[:, sl], k2[:, sl], kc[:, sl]], axis=0)
            vv = jnp.concatenate([v0[:, sl], v1[:, sl], v2[:, sl], vc[:, sl]], axis=0)
            qs, p, p_snk, inv = _attn_probs(q_ref, kk, kh, bias, sink_ref)
            dos = []
            for g in range(GRP):
                h = GRP * kh + g
                chunk = dy_ref[:, (h // 2) * LANES:(h // 2 + 1) * LANES]
                dos.append(jnp.where(low if h % 2 == 0 else ~low, chunk, jnp.zeros_like(chunk)).astype(BF16))
            dos = jnp.concatenate(dos, axis=0)
            dp = lax.dot_general(dos, vv, (((1,), (1,)), ((), ())), preferred_element_type=F32)
            dsum = jnp.sum(p * dp, axis=-1, keepdims=True) * inv
            ds = (p * ((dp - dsum) * inv)).astype(BF16)
            snk_term = p_snk * inv * dsum
            for g in range(GRP):
                contrib = -jnp.sum(snk_term[g * BLK:(g + 1) * BLK], axis=0, keepdims=True)
                ds_ref[GRP * kh + g:GRP * kh + g + 1, :] += jnp.broadcast_to(contrib, (1, LANES))
            dqs = jnp.dot(ds, kk, preferred_element_type=F32) * scale
            for half in range(2):
                even = dqs[(2 * half) * BLK:(2 * half + 1) * BLK]
                odd = dqs[(2 * half + 1) * BLK:(2 * half + 2) * BLK]
                ch = 2 * kh + half
                dq_ref[:, ch * LANES:(ch + 1) * LANES] = jnp.where(low, even, odd)
            dkk = lax.dot_general(ds, qs, (((0,), (0,)), ((), ())), preferred_element_type=F32)
            dvv = lax.dot_general((p * inv).astype(BF16), dos, (((0,), (0,)), ((), ())), preferred_element_type=F32)
            dk_heads.append(dkk + pltpu.roll(dkk, HEAD_DIM, 1))
            dv_heads.append(dvv + pltpu.roll(dvv, HEAD_DIM, 1))
        for pair in range(N_KV // 2):
            sl = slice(pair * LANES, (pair + 1) * LANES)
            for acc, heads in ((dk_acc, dk_heads), (dv_acc, dv_heads)):
                both = jnp.where(low_keys, heads[2 * pair], heads[2 * pair + 1])
                acc[pl.ds(win_start, 3 * BLK), sl] += both[:3 * BLK]
                acc[n_lat:n_lat + n_ctx, sl] += both[3 * BLK:]

        @pl.when(j == n_steps - 1)
        def _():
            ck = pltpu.make_async_copy(dk_acc, dk_hbm, sem.at[0])
            cv = pltpu.make_async_copy(dv_acc, dv_hbm, sem.at[1])
            ck.start()
            cv.start()
            ck.wait()
            cv.wait()

    outs = pl.pallas_call(
        body, name=name, grid=(n_steps,),
        in_specs=[_row_spec(BLK, D)] + win + [ctx] + win + [ctx] + [_full_spec((N_HEADS, LANES)), _row_spec(BLK, D)]
        + [ANY] * len(hosted.arrays),
        out_specs=[_row_spec(BLK, D), ANY, ANY, _full_spec((N_HEADS, LANES))] + [ANY] * len(hosted.out_shapes),
        out_shape=[jax.ShapeDtypeStruct((t_rows, D), F32), jax.ShapeDtypeStruct((t_rows, KV_W), F32),
                   jax.ShapeDtypeStruct((t_rows, KV_W), F32), jax.ShapeDtypeStruct((N_HEADS, LANES), F32)]
        + list(hosted.out_shapes),
        scratch_shapes=[pltpu.VMEM((t_rows, KV_W), F32), pltpu.VMEM((t_rows, KV_W), F32),
                        pltpu.SemaphoreType.DMA((2,))] + list(hosted.scratch),
        compiler_params=_params(("arbitrary",)),
    )(q, kd, kd, kd, kd, vd, vd, vd, vd, sink_b, dy, *hosted.arrays)
    return outs[0], outs[1], outs[2], outs[3], outs[4:]


_GELU_K = math.sqrt(2.0 / math.pi)


def _gelu(v):
    return jax.nn.gelu(v)


def _gelu_and_grad(v):
    t = jnp.tanh(_GELU_K * (v + 0.044715 * (v * v * v)))
    cdf = 0.5 * (1.0 + t)
    return v * cdf, cdf + 0.5 * v * (1.0 - t * t) * _GELU_K * (1.0 + 3.0 * 0.044715 * v * v)


def _gating_fwd(z_a, w_s, b_sb, g_v, *, name, hosted=None):
    hosted = hosted or _NO_EXCHANGE
    t_rows = z_a.shape[0]
    n_steps = t_rows // BLK

    def body(*refs):
        ins, outs, _, h_refs = _split_refs(refs, 4, 1, 0, hosted)
        z_ref, w_ref, b_ref, g_ref = ins
        o_ref, = outs
        _run_hosted(hosted, h_refs, pl.program_id(0), n_steps)
        u = _gelu(z_ref[:, :D])
        v = _gelu(z_ref[:, D:])
        vn = v * lax.rsqrt(jnp.mean(v * v, axis=-1, keepdims=True) + EPS) * g_ref[...]
        for g in range(A_GROUPS):
            sl = slice(g * LANES, (g + 1) * LANES)
            mixed = jnp.dot(w_ref[g].astype(BF16), vn[:, sl].astype(BF16), preferred_element_type=F32) + b_ref[g]
            o_ref[:, sl] = (u[:, sl] * mixed).astype(BF16)

    outs = pl.pallas_call(
        body, name=name, grid=(n_steps,),
        in_specs=[_row_spec(BLK, A_COLS), _full_spec((A_GROUPS, BLK, BLK)), _full_spec((A_GROUPS, BLK, LANES)),
                  _full_spec((1, D))] + [ANY] * len(hosted.arrays),
        out_specs=[_row_spec(BLK, D)] + [ANY] * len(hosted.out_shapes),
        out_shape=[jax.ShapeDtypeStruct((t_rows, D), BF16)] + list(hosted.out_shapes),
        scratch_shapes=list(hosted.scratch),
        compiler_params=_params(("arbitrary",)),
    )(z_a, w_s, b_sb, g_v, *hosted.arrays)
    return outs[0], outs[1:]


def _gating_bwd(z_a, dy, w_s, b_sb, g_v, *, name, hosted=None):
    hosted = hosted or _NO_EXCHANGE
    t_rows = z_a.shape[0]
    n_steps = t_rows // BLK

    def body(*refs):
        ins, outs, _, h_refs = _split_refs(refs, 5, 4, 0, hosted)
        z_ref, dy_ref, w_ref, b_ref, g_ref = ins
        dz_ref, dw_ref, db_ref, st_ref = outs
        i = pl.program_id(0)
        _run_hosted(hosted, h_refs, i, n_steps)

        @pl.when(i == 0)
        def _():
            dw_ref[...] = jnp.zeros_like(dw_ref)
            db_ref[...] = jnp.zeros_like(db_ref)
            st_ref[...] = jnp.zeros_like(st_ref)

        u, du_dz = _gelu_and_grad(z_ref[:, :D])
        v, dv_dz = _gelu_and_grad(z_ref[:, D:])
        gv = g_ref[...]
        rstd = lax.rsqrt(jnp.mean(v * v, axis=-1, keepdims=True) + EPS)
        vh = v * rstd
        vn = vh * gv
        dyv = dy_ref[...]
        dvn = []
        for g in range(A_GROUPS):
            sl = slice(g * LANES, (g + 1) * LANES)
            wg = w_ref[g].astype(BF16)
            vg = vn[:, sl].astype(BF16)
            mixed = jnp.dot(wg, vg, preferred_element_type=F32) + b_ref[g]
            dz_ref[:, sl] = (dyv[:, sl] * mixed * du_dz[:, sl]).astype(BF16)
            dmixed = dyv[:, sl] * u[:, sl]
            dmb = dmixed.astype(BF16)
            dvn.append(lax.dot_general(wg, dmb, (((0,), (0,)), ((), ())), preferred_element_type=F32))
            dw_ref[g] += lax.dot_general(dmb, vg, (((1,), (1,)), ((), ())), preferred_element_type=F32)
            db_ref[g] += jnp.broadcast_to(jnp.sum(dmixed, axis=-1, keepdims=True), (BLK, LANES))
        dvn = jnp.concatenate(dvn, axis=1)
        st_ref[...] += _rows8([(0, _colsum(dvn * vh))], D)
        e = dvn * gv
        dv = rstd * (e - vh * jnp.mean(e * vh, axis=-1, keepdims=True))
        dz_ref[:, D:] = (dv * dv_dz).astype(BF16)

    outs = pl.pallas_call(
        body, name=name, grid=(n_steps,),
        in_specs=[_row_spec(BLK, A_COLS), _row_spec(BLK, D), _full_spec((A_GROUPS, BLK, BLK)),
                  _full_spec((A_GROUPS, BLK, LANES)), _full_spec((1, D))] + [ANY] * len(hosted.arrays),
        out_specs=[_row_spec(BLK, A_COLS), _full_spec((A_GROUPS, BLK, BLK)), _full_spec((A_GROUPS, BLK, LANES)),
                   _full_spec((SUBLANES, D))] + [ANY] * len(hosted.out_shapes),
        out_shape=[jax.ShapeDtypeStruct((t_rows, A_COLS), BF16), jax.ShapeDtypeStruct((A_GROUPS, BLK, BLK), F32),
                   jax.ShapeDtypeStruct((A_GROUPS, BLK, LANES), F32), jax.ShapeDtypeStruct((SUBLANES, D), F32)]
        + list(hosted.out_shapes),
        scratch_shapes=list(hosted.scratch),
        compiler_params=_params(("arbitrary",)),
    )(z_a, dy, w_s, b_sb, g_v, *hosted.arrays)
    return outs[0], outs[1], outs[2], outs[3], outs[4:]


def _sconv_fwd(z_b, w8, n_lat, *, name):
    t_rows = z_b.shape[0]
    tm = 256
    prev, nxt = _halo_specs(tm, B_COLS, t_rows)

    def body(z_ref, zp_ref, zn_ref, w_ref, o_ref):
        p = z_ref[:, D:2 * D] * z_ref[:, 2 * D:]
        pp = zp_ref[:, D:2 * D] * zp_ref[:, 2 * D:]
        pn = zn_ref[:, D:2 * D] * zn_ref[:, 2 * D:]
        up, dn = _shift_rows(p, pp, pn, n_lat, t_rows, tm)
        conv = w_ref[0:1, :] * up + w_ref[1:2, :] * p + w_ref[2:3, :] * dn
        o_ref[...] = (z_ref[:, :D] * conv).astype(BF16)

    return pl.pallas_call(
        body, name=name, grid=(t_rows // tm,),
        in_specs=[_row_spec(tm, B_COLS), prev, nxt, _full_spec((SUBLANES, D))],
        out_specs=_row_spec(tm, D), out_shape=jax.ShapeDtypeStruct((t_rows, D), BF16),
        compiler_params=_params(("parallel",)),
    )(z_b, z_b, z_b, w8)


def _sconv_bwd(z_b, dy, w8, n_lat, *, name):
    t_rows = z_b.shape[0]
    tm = 256
    prev, nxt = _halo_specs(tm, B_COLS, t_rows)
    dprev, dnxt = _halo_specs(tm, D, t_rows)

    def body(z_ref, zp_ref, zn_ref, dy_ref, dyp_ref, dyn_ref, w_ref, dz_ref, st_ref):
        i = pl.program_id(0)
        bg, cg, hb = z_ref[:, :D], z_ref[:, D:2 * D], z_ref[:, 2 * D:]
        p = cg * hb
        pp = zp_ref[:, D:2 * D] * zp_ref[:, 2 * D:]
        pn = zn_ref[:, D:2 * D] * zn_ref[:, 2 * D:]
        up, dn = _shift_rows(p, pp, pn, n_lat, t_rows, tm)
        w0, w1, w2 = w_ref[0:1, :], w_ref[1:2, :], w_ref[2:3, :]
        conv = w0 * up + w1 * p + w2 * dn
        dyv = dy_ref[...]
        dz_ref[:, :D] = (dyv * conv).astype(BF16)
        dcv = dyv * bg
        dcv_up, dcv_dn = _shift_rows(dcv, dyp_ref[...] * zp_ref[:, :D], dyn_ref[...] * zn_ref[:, :D], n_lat, t_rows, tm)
        dp = w0 * dcv_dn + w1 * dcv + w2 * dcv_up
        dz_ref[:, D:2 * D] = (dp * hb).astype(BF16)
        dz_ref[:, 2 * D:] = (dp * cg).astype(BF16)
        upd = _rows8([(0, _colsum(dcv * up)), (1, _colsum(dcv * p)), (2, _colsum(dcv * dn))], D)

        @pl.when(i == 0)
        def _():
            st_ref[...] = upd

        @pl.when(i > 0)
        def _():
            st_ref[...] += upd

    return pl.pallas_call(
        body, name=name, grid=(t_rows // tm,),
        in_specs=[_row_spec(tm, B_COLS), prev, nxt, _row_spec(tm, D), dprev, dnxt, _full_spec((SUBLANES, D))],
        out_specs=[_row_spec(tm, B_COLS), _full_spec((SUBLANES, D))],
        out_shape=[jax.ShapeDtypeStruct((t_rows, B_COLS), BF16), jax.ShapeDtypeStruct((SUBLANES, D), F32)],
        compiler_params=_params(("arbitrary",)),
    )(z_b, z_b, z_b, dy, dy, dy, w8)


def _branch_merge_fwd(ys, w_branch, z_g, b_gate, *, name):
    t_rows = z_g.shape[0]
    tm = _pick(t_rows, (768, 512, 256))

    def body(y0_ref, y1_ref, y2_ref, w_ref, z_ref, b_ref, t_ref, o_ref, acc_ref):
        k = pl.program_id(1)
        for which, y_ref in enumerate((y0_ref, y1_ref, y2_ref)):
            @pl.when(k == which)
            def _():
                t_ref[...] = jnp.dot(y_ref[...], w_ref[...], preferred_element_type=F32)

        term = jax.nn.sigmoid(z_ref[...] + b_ref[...]) * t_ref[...]

        @pl.when(k == 0)
        def _():
            acc_ref[...] = term

        @pl.when(k > 0)
        def _():
            acc_ref[...] += term

        @pl.when(k == 2)
        def _():
            o_ref[...] = acc_ref[...].astype(BF16)

    y_spec = pl.BlockSpec((tm, D), lambda i, k: (i, 0))
    return pl.pallas_call(
        body, name=name, grid=(t_rows // tm, 3),
        in_specs=[y_spec, y_spec, y_spec, pl.BlockSpec((None, D, D), lambda i, k: (k, 0, 0)),
                  pl.BlockSpec((tm, D), lambda i, k: (i, k)), pl.BlockSpec((None, 1, D), lambda i, k: (k, 0, 0))],
        out_specs=[pl.BlockSpec((None, tm, D), lambda i, k: (k, i, 0)), y_spec],
        out_shape=[jax.ShapeDtypeStruct((3, t_rows, D), F32), jax.ShapeDtypeStruct((t_rows, D), BF16)],
        scratch_shapes=[pltpu.VMEM((tm, D), F32)],
        compiler_params=_params(("parallel", "arbitrary")),
    )(*ys, w_branch, z_g, b_gate.reshape(3, 1, D))


def _merge_bwd(dmerged, t_all, z_g, b_gate, *, name):
    t_rows = dmerged.shape[0]
    tm = 256
    t_specs = [pl.BlockSpec((None, tm, D), functools.partial(lambda i, k: (k, i, 0), k=k)) for k in range(3)]

    def body(dm_ref, t0_ref, t1_ref, t2_ref, z_ref, b_ref, d0_ref, d1_ref, d2_ref, dz_ref, st_ref):
        i = pl.program_id(0)
        dm = dm_ref[...]
        sums = []
        for k, (t_ref, d_ref) in enumerate(((t0_ref, d0_ref), (t1_ref, d1_ref), (t2_ref, d2_ref))):
            gate = jax.nn.sigmoid(z_ref[:, k * D:(k + 1) * D] + b_ref[:, k * D:(k + 1) * D])
            d_ref[...] = (dm * gate).astype(BF16)
            dzg = dm * t_ref[...] * gate * (1.0 - gate)
            dz_ref[:, k * D:(k + 1) * D] = dzg.astype(BF16)
            sums.append(_colsum(dzg))
        upd = _rows8([(0, jnp.concatenate(sums, axis=1))], G_COLS)

        @pl.when(i == 0)
        def _():
            st_ref[...] = upd

        @pl.when(i > 0)
        def _():
            st_ref[...] += upd

    return pl.pallas_call(
        body, name=name, grid=(t_rows // tm,),
        in_specs=[_row_spec(tm, D)] + t_specs + [_row_spec(tm, G_COLS), _full_spec((1, G_COLS))],
        out_specs=[_row_spec(tm, D)] * 3 + [_row_spec(tm, G_COLS), _full_spec((SUBLANES, G_COLS))],
        out_shape=[jax.ShapeDtypeStruct((t_rows, D), BF16)] * 3
        + [jax.ShapeDtypeStruct((t_rows, G_COLS), BF16), jax.ShapeDtypeStruct((SUBLANES, G_COLS), F32)],
        compiler_params=_params(("arbitrary",)),
    )(dmerged, t_all, t_all, t_all, z_g, b_gate)


def _ffn_mid_fwd(up, w8, n_lat, *, name, hosted=None):
    hosted = hosted or _NO_EXCHANGE
    t_rows = up.shape[0]
    tm = 256
    n_steps = t_rows // tm
    prev, nxt = _halo_specs(tm, D_FF, t_rows)

    def body(*refs):
        ins, outs, _, h_refs = _split_refs(refs, 5, 2, 0, hosted)
        a_ref, ap_ref, an_ref, g_ref, w_ref = ins
        cv_ref, f_ref = outs
        _run_hosted(hosted, h_refs, pl.program_id(0), n_steps)
        a = a_ref[...]
        au, ad = _shift_rows(a, ap_ref[...], an_ref[...], n_lat, t_rows, tm)
        cv = w_ref[0:1, :] * au + w_ref[1:2, :] * a + w_ref[2:3, :] * ad
        cv_ref[...] = cv
        f_ref[...] = (_silu(cv) * g_ref[...]).astype(BF16)

    outs = pl.pallas_call(
        body, name=name, grid=(n_steps,),
        in_specs=[_row_spec(tm, D_FF), prev, nxt, _row_spec(tm, D_FF, 1), _full_spec((SUBLANES, D_FF))]
        + [ANY] * len(hosted.arrays),
        out_specs=[_row_spec(tm, D_FF), _row_spec(tm, D_FF)] + [ANY] * len(hosted.out_shapes),
        out_shape=[jax.ShapeDtypeStruct((t_rows, D_FF), F32), jax.ShapeDtypeStruct((t_rows, D_FF), BF16)]
        + list(hosted.out_shapes),
        scratch_shapes=list(hosted.scratch),
        compiler_params=_params(("arbitrary",)),
    )(up, up, up, up, w8, *hosted.arrays)
    return outs[0], outs[1], outs[2:]


def _ffn_mid_bwd(up, cv, df, w8, n_lat, *, name):
    t_rows = up.shape[0]
    tm = 256
    prev, nxt = _halo_specs(tm, D_FF, t_rows)
    gprev, gnxt = _halo_specs(tm, D_FF, t_rows, 1)

    def body(a_ref, ap_ref, an_ref, g_ref, gp_ref, gn_ref, cv_ref, cp_ref, cn_ref, df_ref, dfp_ref, dfn_ref,
             w_ref, o_ref, st_ref):
        i = pl.program_id(0)
        a = a_ref[...]
        au, ad = _shift_rows(a, ap_ref[...], an_ref[...], n_lat, t_rows, tm)
        cvv = cv_ref[...]
        dfv = df_ref[...]
        sig = jax.nn.sigmoid(cvv)
        o_ref[:, D_FF:] = (dfv * (cvv * sig)).astype(BF16)
        dcv = dfv * g_ref[...] * (sig * (1.0 + cvv * (1.0 - sig)))
        dcv_p = dfp_ref[...] * gp_ref[...] * _dsilu(cp_ref[...])
        dcv_n = dfn_ref[...] * gn_ref[...] * _dsilu(cn_ref[...])
        du, dd = _shift_rows(dcv, dcv_p, dcv_n, n_lat, t_rows, tm)
        o_ref[:, :D_FF] = (w_ref[0:1, :] * dd + w_ref[1:2, :] * dcv + w_ref[2:3, :] * du).astype(BF16)
        upd = _rows8([(0, _colsum(dcv * au)), (1, _colsum(dcv * a)), (2, _colsum(dcv * ad))], D_FF)

        @pl.when(i == 0)
        def _():
            st_ref[...] = upd

        @pl.when(i > 0)
        def _():
            st_ref[...] += upd

    row = _row_spec(tm, D_FF)
    return pl.pallas_call(
        body, name=name, grid=(t_rows // tm,),
        in_specs=[row, prev, nxt, _row_spec(tm, D_FF, 1), gprev, gnxt, row, prev, nxt, row, prev, nxt,
                  _full_spec((SUBLANES, D_FF))],
        out_specs=[_row_spec(tm, 2 * D_FF), _full_spec((SUBLANES, D_FF))],
        out_shape=[jax.ShapeDtypeStruct((t_rows, 2 * D_FF), BF16), jax.ShapeDtypeStruct((SUBLANES, D_FF), F32)],
        compiler_params=_params(("arbitrary",), VMEM_LIMIT_WIDE),
    )(up, up, up, up, up, up, cv, cv, cv, df, df, df, w8)


def _loss_head(x, target, g_final, n_lat, *, name):
    t_rows = x.shape[0]
    tm = 256
    last = n_lat // tm - 1

    def body(x_ref, t_ref, g_ref, dx_ref, st_ref):
        i = pl.program_id(0)
        is_ctx = i * tm >= n_lat
        xv = x_ref[...]
        gv = g_ref[...]
        rstd = lax.rsqrt(jnp.mean(xv * xv, axis=-1, keepdims=True) + EPS)
        rn = xv * rstd
        err = rn * gv - t_ref[...]
        dy = err / D
        e = dy * gv
        dx = rstd * (e - rn * jnp.mean(e * rn, axis=-1, keepdims=True))
        dx_ref[...] = jnp.where(is_ctx, 0.0, dx)
        keep = jnp.where(is_ctx, 0.0, 1.0)
        upd = _rows8([(0, keep * _colsum(dy * rn)), (1, keep * _colsum(err * err))], D)

        @pl.when(i == 0)
        def _():
            st_ref[...] = upd

        @pl.when(i > 0)
        def _():
            st_ref[...] += upd

    return pl.pallas_call(
        body, name=name, grid=(t_rows // tm,),
        in_specs=[_row_spec(tm, D), pl.BlockSpec((tm, D), lambda i: (jnp.minimum(i, last), 0)), _full_spec((1, D))],
        out_specs=[_row_spec(tm, D), _full_spec((SUBLANES, D))],
        out_shape=[jax.ShapeDtypeStruct((t_rows, D), F32), jax.ShapeDtypeStruct((SUBLANES, D), F32)],
        compiler_params=_params(("arbitrary",)),
    )(x, target, g_final)


def _sum_slabs(x, out_dtype, *, name):
    n_slabs, rows, cols = x.shape
    tm = _pick(rows, (432, 256, 192, 128, 64, 32, 24, 16, 8))

    def body(x_ref, o_ref):
        acc = x_ref[0].astype(F32)
        for s in range(1, n_slabs):
            acc = acc + x_ref[s].astype(F32)
        o_ref[...] = acc.astype(o_ref.dtype)

    return pl.pallas_call(
        body, name=name, grid=(rows // tm,),
        in_specs=[pl.BlockSpec((n_slabs, tm, cols), lambda i: (0, i, 0))],
        out_specs=pl.BlockSpec((tm, cols), lambda i: (i, 0)),
        out_shape=jax.ShapeDtypeStruct((rows, cols), out_dtype),
        compiler_params=_params(("parallel",)),
    )(x)


def _add_half(half_idx, a, b, *, name):
    n_slabs, rows, cols = b.shape
    tm = _pick(rows, (432, 256, 192, 128, 96, 64, 32, 16))
    per_half = rows // tm

    def body(half_ref, a_ref, b_ref, o_ref):
        o_ref[...] = (a_ref[...].astype(F32) + b_ref[...].astype(F32)).astype(BF16)

    spec = pl.BlockSpec((1, tm, cols), lambda s, i, half_ref: (s, i, 0))
    a_spec = pl.BlockSpec((1, tm, cols), lambda s, i, half_ref: (s, half_ref[0] * per_half + i, 0))
    return pl.pallas_call(
        body, name=name,
        grid_spec=pltpu.PrefetchScalarGridSpec(num_scalar_prefetch=1, grid=(n_slabs, per_half),
                                               in_specs=[a_spec, spec], out_specs=spec),
        out_shape=jax.ShapeDtypeStruct(b.shape, BF16), compiler_params=_params(("parallel", "parallel")),
    )(half_idx, a, b)


def _adamw(w, g, m, v, *, name, hosted=None):
    hosted = hosted or _NO_EXCHANGE
    rows, cols = w.shape
    tm = _pick(rows, (256, 128, 64, 32, 16, 8))
    n_steps = rows // tm

    def body(*refs):
        ins, outs, _, h_refs = _split_refs(refs, 4, 3, 0, hosted)
        w_ref, g_ref, m_ref, v_ref = ins
        d_ref, nm_ref, nv_ref = outs
        _run_hosted(hosted, h_refs, pl.program_id(0), n_steps)
        gv = g_ref[...]
        nm = ADAM_B1 * m_ref[...] + (1.0 - ADAM_B1) * gv
        nv = ADAM_B2 * v_ref[...] + (1.0 - ADAM_B2) * jnp.square(gv)
        m_hat = nm / (1.0 - ADAM_B1 ** ADAM_STEP)
        v_hat = nv / (1.0 - ADAM_B2 ** ADAM_STEP)
        d_ref[...] = -ADAM_LR * (m_hat / (jnp.sqrt(v_hat) + ADAM_EPS) + ADAM_WD * w_ref[...])
        nm_ref[...] = nm
        nv_ref[...] = nv

    spec = pl.BlockSpec((tm, cols), lambda i: (i, 0))
    shape = jax.ShapeDtypeStruct((rows, cols), F32)
    outs = pl.pallas_call(
        body, name=name, grid=(n_steps,), in_specs=[spec] * 4 + [ANY] * len(hosted.arrays),
        out_specs=[spec] * 3 + [ANY] * len(hosted.out_shapes), out_shape=[shape] * 3 + list(hosted.out_shapes),
        scratch_shapes=list(hosted.scratch), compiler_params=_params(("arbitrary",)),
    )(w, g, m, v, *hosted.arrays)
    return outs[0], outs[1], outs[2], outs[3:]


def _place():
    x, y, c = lax.axis_index("x"), lax.axis_index("y"), lax.axis_index("c")
    chips = [(1 - x, y), (x, 1 - y), (1 - x, 1 - y)]
    return x, y, c, chips


def _remote(src, dst, send_sems, recv_sems, k, to):
    return pltpu.make_async_remote_copy(src_ref=src, dst_ref=dst, send_sem=send_sems.at[k], recv_sem=recv_sems.at[k],
                                        device_id=to, device_id_type=MESH)


HALF_CHUNKS = 2


def _chunks(ref, n):
    step = ref.shape[0] // n
    tile_rows = SUBLANES if ref.dtype == F32 else 2 * SUBLANES
    assert step * n == ref.shape[0] and step % tile_rows == 0, (ref.shape, n)
    return [ref.at[pl.ds(k * step, step)] for k in range(n)]


def _half(ref, which):
    half = ref.shape[0] // 2
    return ref.at[pl.ds(pl.multiple_of(which * half, 2 * SUBLANES), half)]


def _staged_copy(src, dst, buf, sems):
    step = buf.shape[1]
    n = src.shape[0] // step
    assert n * step == src.shape[0], (src.shape, step)
    ins = [pltpu.make_async_copy(src.at[pl.ds(k * step, step)], buf.at[k % 2], sems.at[k % 2]) for k in range(n)]
    outs = [pltpu.make_async_copy(buf.at[k % 2], dst.at[pl.ds(k * step, step)], sems.at[2 + k % 2]) for k in range(n)]
    ins[0].start()
    for k in range(n):
        ins[k].wait()
        outs[k].start()
        if k + 1 < n:
            if k >= 1:
                outs[k - 1].wait()
            ins[k + 1].start()
    if n >= 2:
        outs[n - 2].wait()
    outs[n - 1].wait()


def _stage_rows(rows):
    return _pick(rows, (256, 432))


def _stage_scratch(slabs):
    return [pltpu.VMEM((2, _stage_rows(s.shape[-2]), s.shape[-1]), s.dtype) for s in slabs] + [pltpu.SemaphoreType.DMA((4,))]


N_LINK_SEMS = (N_CHIPS - 1) * HALF_CHUNKS


def _link_sems(n_groups):
    return [pltpu.SemaphoreType.DMA((n_groups * N_LINK_SEMS,)), pltpu.SemaphoreType.DMA((n_groups * N_LINK_SEMS,))]


def _sem_index(g, j, k):
    return g * N_LINK_SEMS + j * HALF_CHUNKS + k


def _gather_ici_start(*refs):
    n = (len(refs) - 2) // 2
    p_refs, o_refs, (send_sems, recv_sems) = refs[:n], refs[n:2 * n], refs[2 * n:]
    x, y, c, chips = _place()
    for g, (p_ref, o_ref) in enumerate(zip(p_refs, o_refs)):
        src = _chunks(_half(p_ref, c), HALF_CHUNKS)
        dst = _chunks(_half(o_ref.at[2 * x + y], c), HALF_CHUNKS)
        for j, chip in enumerate(chips):
            for k in range(HALF_CHUNKS):
                _remote(src[k], dst[k], send_sems, recv_sems, _sem_index(g, j, k), (*chip, c)).start()


def _gather_ici_finish(*refs):
    n = (len(refs) - 2) // 2
    p_refs, o_refs, (send_sems, recv_sems) = refs[:n], refs[n:2 * n], refs[2 * n:]
    x, y, c, chips = _place()
    for g, (p_ref, o_ref) in enumerate(zip(p_refs, o_refs)):
        src = _chunks(_half(p_ref, c), HALF_CHUNKS)
        for j, (cx, cy) in enumerate(chips):
            for k, landed in enumerate(_chunks(_half(o_ref.at[2 * cx + cy], c), HALF_CHUNKS)):
                _remote(src[k], landed, send_sems, recv_sems, _sem_index(g, j, k), (x, y, c)).wait_recv()
        for j in range(len(chips)):
            for k in range(HALF_CHUNKS):
                _remote(src[k], src[k], send_sems, recv_sems, _sem_index(g, j, k), (x, y, c)).wait_send()


def _gathered_shapes(slabs):
    return [jax.ShapeDtypeStruct((N_CHIPS,) + s.shape, s.dtype) for s in slabs]


def _gather_ici_hosted(slabs):
    return _Hosted(list(slabs), _gathered_shapes(slabs), _link_sems(len(slabs)), _gather_ici_start, _gather_ici_finish)


def _gather_ici(slabs, *, name):
    def body(*refs):
        _gather_ici_start(*refs)
        _gather_ici_finish(*refs)

    return pl.pallas_call(
        body, name=name, in_specs=[ANY] * len(slabs), out_specs=[ANY] * len(slabs),
        out_shape=_gathered_shapes(slabs), scratch_shapes=_link_sems(len(slabs)),
    )(*slabs)


def _gather_finish(partials, slabs, *, name):
    n = len(slabs)

    def body(*refs):
        p_refs, o_refs = refs[n:2 * n], refs[2 * n:3 * n]
        send_sems, recv_sems = refs[3 * n:3 * n + 2]
        bufs, loc_sems = refs[3 * n + 2:4 * n + 2], refs[4 * n + 2]
        x, y, c, chips = _place()
        sib = (x, y, 1 - c)
        passed = []
        for g, o_ref in enumerate(o_refs):
            for j, (cx, cy) in enumerate(chips):
                for k, landed in enumerate(_chunks(_half(o_ref.at[2 * cx + cy], c), HALF_CHUNKS)):
                    passed.append(_remote(landed, landed, send_sems, recv_sems, _sem_index(g, j, k), sib))
        for cp in passed:
            cp.start()
        for p_ref, o_ref, buf in zip(p_refs, o_refs, bufs):
            _staged_copy(p_ref, o_ref.at[2 * x + y], buf, loc_sems)
        for g, o_ref in enumerate(o_refs):
            for j, (cx, cy) in enumerate(chips):
                for k, landed in enumerate(_chunks(_half(o_ref.at[2 * cx + cy], 1 - c), HALF_CHUNKS)):
                    _remote(landed, landed, send_sems, recv_sems, _sem_index(g, j, k), sib).wait_recv()
        for cp in passed:
            cp.wait_send()

    return pl.pallas_call(
        body, name=name, in_specs=[ANY] * (2 * n), out_specs=[ANY] * n,
        out_shape=[jax.ShapeDtypeStruct(p.shape, p.dtype) for p in partials],
        input_output_aliases={g: g for g in range(n)}, scratch_shapes=_link_sems(n) + _stage_scratch(slabs),
        compiler_params=_params(),
    )(*partials, *slabs)


def _grad_sibling_swap(g_packs, *, name):
    n = len(g_packs)
    per_group = N_CHIPS * HALF_CHUNKS

    def body(*refs):
        g_refs, got_refs, (send_sems, recv_sems) = refs[:n], refs[n:2 * n], refs[2 * n:]
        x, y, c, _ = _place()
        sib = (x, y, 1 - c)
        swaps = [_remote(src, dst, send_sems, recv_sems, g * per_group + s * HALF_CHUNKS + k, sib)
                 for g, (g_ref, got_ref) in enumerate(zip(g_refs, got_refs))
                 for s in range(N_CHIPS)
                 for k, (src, dst) in enumerate(zip(_chunks(_half(g_ref.at[s], 1 - c), HALF_CHUNKS),
                                                    _chunks(got_ref.at[s], HALF_CHUNKS)))]
        for cp in swaps:
            cp.start()
        for cp in swaps:
            cp.wait_recv()
        for cp in swaps:
            cp.wait_send()

    return pl.pallas_call(
        body, name=name, in_specs=[ANY] * n, out_specs=[ANY] * n,
        out_shape=[jax.ShapeDtypeStruct((N_CHIPS, g.shape[1] // 2, g.shape[2]), g.dtype) for g in g_packs],
        scratch_shapes=[pltpu.SemaphoreType.DMA((n * per_group,)), pltpu.SemaphoreType.DMA((n * per_group,))],
    )(*g_packs)


def _grad_ici_refs(refs):
    n = (len(refs) - 3) // 3
    return refs[:n], refs[n:2 * n], refs[2 * n], refs[2 * n + 1], refs[2 * n + 2:3 * n + 2], refs[3 * n + 2]


def _grad_ici_start(*refs):
    s_refs, o_refs, send_sems, recv_sems, _, _ = _grad_ici_refs(refs)
    x, y, c, chips = _place()
    for g, (s_ref, o_ref) in enumerate(zip(s_refs, o_refs)):
        for j, (cx, cy) in enumerate(chips):
            pairs = zip(_chunks(s_ref.at[2 * cx + cy], HALF_CHUNKS), _chunks(o_ref.at[2 * x + y], HALF_CHUNKS))
            for k, (src, dst) in enumerate(pairs):
                _remote(src, dst, send_sems, recv_sems, _sem_index(g, j, k), (cx, cy, c)).start()


def _grad_ici_finish(*refs):
    s_refs, o_refs, send_sems, recv_sems, bufs, loc_sems = _grad_ici_refs(refs)
    x, y, c, chips = _place()
    me = 2 * x + y
    for s_ref, o_ref, buf in zip(s_refs, o_refs, bufs):
        _staged_copy(s_ref.at[me], o_ref.at[me], buf, loc_sems)
    for g, (s_ref, o_ref) in enumerate(zip(s_refs, o_refs)):
        for j, (cx, cy) in enumerate(chips):
            for k, landed in enumerate(_chunks(o_ref.at[2 * cx + cy], HALF_CHUNKS)):
                _remote(landed, landed, send_sems, recv_sems, _sem_index(g, j, k), (x, y, c)).wait_recv()
        for j, (cx, cy) in enumerate(chips):
            for k, sent in enumerate(_chunks(s_ref.at[2 * cx + cy], HALF_CHUNKS)):
                _remote(sent, sent, send_sems, recv_sems, _sem_index(g, j, k), (x, y, c)).wait_send()


def _grad_ici_hosted(sums):
    return _Hosted(list(sums), [jax.ShapeDtypeStruct(s.shape, s.dtype) for s in sums],
                   _link_sems(len(sums)) + _stage_scratch(sums), _grad_ici_start, _grad_ici_finish)


def _grad_ici(sums, *, name):
    def body(*refs):
        _grad_ici_start(*refs)
        _grad_ici_finish(*refs)

    n = len(sums)
    return pl.pallas_call(
        body, name=name, in_specs=[ANY] * n, out_specs=[ANY] * n,
        out_shape=[jax.ShapeDtypeStruct(s.shape, s.dtype) for s in sums],
        scratch_shapes=_link_sems(n) + _stage_scratch(sums), compiler_params=_params(),
    )(*sums)


def _grad_sibling_share(totals, *, name):
    n = len(totals)
    n_ch = HALF_CHUNKS

    def body(*refs):
        t_refs, o_refs = refs[:n], refs[n:2 * n]
        send_sems, recv_sems = refs[2 * n:2 * n + 2]
        bufs, loc_sems = refs[2 * n + 2:3 * n + 2], refs[3 * n + 2]
        x, y, c, _ = _place()
        sib = (x, y, 1 - c)
        sends = [_remote(src, dst, send_sems, recv_sems, g * n_ch + k, sib)
                 for g, (t_ref, o_ref) in enumerate(zip(t_refs, o_refs))
                 for k, (src, dst) in enumerate(zip(_chunks(t_ref, n_ch), _chunks(_half(o_ref, c), n_ch)))]
        for cp in sends:
            cp.start()
        for t_ref, o_ref, buf in zip(t_refs, o_refs, bufs):
            _staged_copy(t_ref, _half(o_ref, c), buf, loc_sems)
        for g, o_ref in enumerate(o_refs):
            for k, landed in enumerate(_chunks(_half(o_ref, 1 - c), n_ch)):
                _remote(landed, landed, send_sems, recv_sems, g * n_ch + k, sib).wait_recv()
        for cp in sends:
            cp.wait_send()

    return pl.pallas_call(
        body, name=name, in_specs=[ANY] * n, out_specs=[ANY] * n,
        out_shape=[jax.ShapeDtypeStruct((2 * t.shape[0], t.shape[1]), t.dtype) for t in totals],
        scratch_shapes=[pltpu.SemaphoreType.DMA((n * n_ch,)), pltpu.SemaphoreType.DMA((n * n_ch,))] + _stage_scratch(totals),
        compiler_params=_params(),
    )(*totals)


def _allgather8(v, *, name):
    rows, cols = v.shape

    def body(v_ref, o_ref, send_sems, recv_sems, loc_sem):
        x, y, c, chips = _place()
        sib = (x, y, 1 - c)

        def slot(px, py, pc):
            return o_ref.at[4 * px + 2 * py + pc]

        local = pltpu.make_async_copy(v_ref, slot(x, y, c), loc_sem.at[0])
        local.start()
        first = [_remote(v_ref, slot(x, y, c), send_sems, recv_sems, 0, sib)]
        first += [_remote(v_ref, slot(x, y, c), send_sems, recv_sems, 1 + j, (*chip, c)) for j, chip in enumerate(chips)]
        for cp in first:
            cp.start()
        passed = [_remote(slot(*chip, c), slot(*chip, c), send_sems, recv_sems, 4 + j, sib)
                  for j, chip in enumerate(chips)]
        for j, chip in enumerate(chips):
            _remote(v_ref, slot(*chip, c), send_sems, recv_sems, 1 + j, sib).wait_recv()
            passed[j].start()
        _remote(v_ref, slot(x, y, 1 - c), send_sems, recv_sems, 0, sib).wait_recv()
        for j, chip in enumerate(chips):
            _remote(v_ref, slot(*chip, 1 - c), send_sems, recv_sems, 4 + j, sib).wait_recv()
        for cp in first + passed:
            cp.wait_send()
        local.wait()

    return pl.pallas_call(
        body, name=name, in_specs=[ANY], out_specs=ANY, out_shape=jax.ShapeDtypeStruct((N_DEV, rows, cols), v.dtype),
        scratch_shapes=[pltpu.SemaphoreType.DMA((7,)), pltpu.SemaphoreType.DMA((7,)), pltpu.SemaphoreType.DMA((1,))],
    )(v)


_BIG = (("w_mod", (D, 6 * D), 1), ("w_in", (D, IN_W), 1), ("w_branch", (3 * D, D), None), ("w_out", (D, D), 0),
        ("w_up", (D, 2 * D_FF), 1), ("w_down", (D_FF, D), 0))
_COL_SHARDED = ("w_mod", "w_in", "w_up")
_ROW_SHARDED = (("w_branch", 3 * D // N_CHIPS), ("w_out", D // N_CHIPS), ("w_down", D_FF // N_CHIPS))


def _pack_shards(shards, layer):
    rows = jnp.concatenate([shards[n][layer].reshape(r, D) for n, r in _ROW_SHARDED], axis=0)
    return [shards[n][layer] for n in _COL_SHARDED] + [rows]


def _unpack_cols(blk):
    return blk.transpose(1, 0, 2).reshape(blk.shape[1], N_CHIPS * blk.shape[2])


def _unpack_rows(stack):
    out, off = {}, 0
    for name, r in _ROW_SHARDED:
        blk = stack[:, off:off + r, :]
        off += r
        if name == "w_branch":
            out[name] = blk.reshape(N_CHIPS, 3, D // N_CHIPS, D).transpose(1, 0, 2, 3).reshape(3, D, D)
        else:
            out[name] = blk.reshape(N_CHIPS * r, D)
    return out


def _unpack_full(gathered):
    out = {name: _unpack_cols(blk) for name, blk in zip(_COL_SHARDED, gathered)}
    out.update(_unpack_rows(gathered[-1]))
    return out


def _pack_grad_cols(g):
    return g.reshape(g.shape[0], N_CHIPS, g.shape[1] // N_CHIPS).transpose(1, 0, 2)


def _pack_grad_rows(grads):
    parts = []
    for name, r in _ROW_SHARDED:
        g = grads[name]
        if name == "w_branch":
            g = g.reshape(3, N_CHIPS, D // N_CHIPS, D).transpose(1, 0, 2, 3)
        parts.append(g.reshape(N_CHIPS, r, D))
    return jnp.concatenate(parts, axis=1)


def _pack_grads(grads):
    return [_pack_grad_cols(grads[n]) for n in _COL_SHARDED] + [_pack_grad_rows(grads)]


def _unpack_shards(totals, like):
    out = {n: jnp.stack([totals[l][g] for l in range(DEPTH)]) for g, n in enumerate(_COL_SHARDED)}
    off = 0
    for name, r in _ROW_SHARDED:
        out[name] = jnp.stack([totals[l][-1][off:off + r] for l in range(DEPTH)]).reshape(like[name].shape)
        off += r
    return out


def _pad_rows(v, rows):
    return jnp.concatenate([v, jnp.zeros((rows - v.shape[0],) + v.shape[1:], v.dtype)], axis=0)


def _local_step(x_tok, target, c_vec, c_ctx, wfull, small, n_lat, n_ctx):
    ctx = _step_context(c_vec, c_ctx, n_lat, n_ctx)
    saved = []
    xs = x_tok
    for l in range(DEPTH):
        xs, s, _ = _layer_fwd(l, xs, wfull[l], {k: v[l] for k, v in small.items() if k != "g_final"}, ctx)
        saved.append(s)
    dx, sq_err, d_g_final = _loss_bwd(xs, target, small["g_final"], n_lat)
    wgrads, lgrads, d_a128 = [None] * DEPTH, [None] * DEPTH, [None] * DEPTH
    for l in reversed(range(DEPTH)):
        dx, wgrads[l], lgrads[l], d_a128[l], _ = _layer_bwd(l, saved[l], wfull[l], dx, ctx)
    return sq_err, dx, wgrads, _small_grads(lgrads, d_a128, d_g_final, ctx)


def _step_context(c_vec, c_ctx, n_lat, n_ctx):
    cos_t, sin_t = _rope_tables(n_lat, n_ctx)
    a_in = _pad_rows(jnp.stack([c_vec, c_ctx]), LANES)
    a128 = _small(_silu, (LANES, D), a_in, name="cond_silu")
    return dict(cos_t=cos_t, sin_t=sin_t, a_in=a_in, a128=a128, n_lat=n_lat, n_ctx=n_ctx)


def _loss_bwd(xs, target, g_final, n_lat):
    dx, st = _loss_head(xs, target, g_final[None, :], n_lat, name="loss_head")
    return dx, st[1], st[0]


def _small_grads(lgrads, d_a128, d_g_final, ctx):
    d_cond = _small(lambda a, b, cin: (a + b) * _dsilu(cin), (LANES, D), d_a128[0], d_a128[1], ctx["a_in"],
                    name="cond_bwd")
    out = {k: jnp.stack([lgrads[l][k] for l in range(DEPTH)]) for k in lgrads[0]}
    out["c_ctx"] = d_cond[1]
    out["g_final"] = d_g_final
    return out


def _layer_fwd(l, xs, w, sm, ctx, hosted=_NO_EXCHANGE, hosted_gating=_NO_EXCHANGE, hosted_ffn=_NO_EXCHANGE,
               late_weights=None, hosted_norm1=_NO_EXCHANGE, w_in_late=None):
    n_lat, n_ctx, cos_t, sin_t, a128 = ctx["n_lat"], ctx["n_ctx"], ctx["cos_t"], ctx["sin_t"], ctx["a128"]
    mod128 = _mm(a128, w["w_mod"], name=f"mod{l}")
    mod8 = _small(lambda m, b: m + b, (SUBLANES, 6 * D), mod128[:SUBLANES], sm["b_mod"][None, :], name=f"mod_bias{l}")
    g_mix = sm["g_mix"][None, :]
    g_ffn = sm["g_ffn"][None, :]
    g_v = sm["g_v"][None, :]
    b_gate = sm["b_gate"][None, :]
    sink_b = jnp.broadcast_to(sm["sink"][:, None], (N_HEADS, LANES))
    b_sb = jnp.broadcast_to(sm["b_spatial"][:, :, None], (A_GROUPS, BLK, LANES))
    w_sconv8 = _pad_rows(sm["w_sconv"], SUBLANES)
    w_fconv8 = _pad_rows(sm["w_fconv"], SUBLANES)
    h, carried_norm1 = _norm_mod_fwd(xs, g_mix, mod8, 0, 1, n_lat, name=f"norm1_{l}", hosted=hosted_norm1)
    w_in = w_in_late(carried_norm1) if w_in_late is not None else w["w_in"]
    w_seg = [w_in[:, SEG[k]:SEG[k + 1]] for k in range(4)]
    q, kd, vd = _qkv_proj(h, w_seg[0], cos_t, sin_t, name=f"in_proj_qkv{l}")
    z_a, z_b, z_g = [_mm(h, w_seg[k], name=f"in_proj{k}_{l}") for k in range(1, 4)]
    y_attn, carried = _attention_fwd(q, kd, vd, sink_b, n_lat, n_ctx, name=f"attn{l}", hosted=hosted)
    if late_weights is not None:
        w = dict(w, **late_weights(carried))
    y_a, carried_gating = _gating_fwd(z_a, sm["w_spatial"], b_sb, g_v, name=f"gating{l}", hosted=hosted_gating)
    y_b = _sconv_fwd(z_b, w_sconv8, n_lat, name=f"sconv{l}")
    ys = (y_attn, y_a, y_b)
    ts, merged = _branch_merge_fwd(ys, w["w_branch"], z_g, b_gate, name=f"branch_merge{l}")
    mix_out, x1, h2 = _mm_residual(merged, w["w_out"], xs, mod8, 2, n_lat, name=f"out_proj_res1_norm2_{l}",
                                   norm=(g_ffn, 3, 4))
    up = _mm(h2, w["w_up"], name=f"up_proj{l}")
    cv, f, carried_ffn = _ffn_mid_fwd(up, w_fconv8, n_lat, name=f"ffn_mid{l}", hosted=hosted_ffn)
    ffn_out, x2, _ = _mm_residual(f, w["w_down"], x1, mod8, 5, n_lat, name=f"down_proj_res2_{l}")
    saved = dict(x0=xs, mod8=mod8, h=h, z_a=z_a, z_b=z_b, z_g=z_g, q=q, kd=kd, vd=vd, ys=ys, ts=ts,
                 merged=merged, mix_out=mix_out, x1=x1, h2=h2, up=up, cv=cv, f=f, ffn_out=ffn_out, w_seg=w_seg,
                 g_mix=g_mix, g_ffn=g_ffn, g_v=g_v, b_gate=b_gate, sink_b=sink_b, b_sb=b_sb,
                 w_sconv8=w_sconv8, w_fconv8=w_fconv8, w_spatial=sm["w_spatial"])
    return x2, saved, (carried, carried_gating, carried_ffn)


def _layer_bwd(l, s, w, dx, ctx, hosts=None):
    n_lat, n_ctx, cos_t, sin_t, a128 = ctx["n_lat"], ctx["n_ctx"], ctx["cos_t"], ctx["sin_t"], ctx["a128"]
    mod8 = s["mod8"]
    d_ffn, st_gt2 = _residual_bwd(dx, s["ffn_out"], mod8, 5, n_lat, name=f"res2_bwd{l}")
    df = _mm(d_ffn, w["w_down"], tb=True, name=f"down_bwd_x{l}")
    g_down = _mm(s["f"], d_ffn, ta=True, out_dtype=BF16, name=f"down_bwd_w{l}")
    d_up, st_fc = _ffn_mid_bwd(s["up"], s["cv"], df, s["w_fconv8"], n_lat, name=f"ffn_mid_bwd{l}")
    dh2 = _mm(d_up, w["w_up"], tb=True, name=f"up_bwd_x{l}")
    g_up = _mm(s["h2"], d_up, ta=True, out_dtype=BF16, name=f"up_bwd_w{l}")
    dx1, st_n2, _, d_out = _norm_mod_bwd(s["x1"], [dh2], dx, s["g_ffn"], mod8, 4, n_lat, name=f"norm2_res1_bwd{l}",
                                         res=(s["mix_out"], 2))
    st_gt1 = st_n2[5:7]
    d_merged = _mm(d_out, w["w_out"], tb=True, name=f"out_bwd_x{l}")
    g_out = _mm(s["merged"], d_out, ta=True, out_dtype=BF16, name=f"out_bwd_w{l}")
    dt0, dt1, dt2, dz_g, st_bg = _merge_bwd(d_merged, s["ts"], s["z_g"], s["b_gate"], name=f"merge_bwd{l}")
    dts = (dt0, dt1, dt2)
    dys = [_mm(dts[k], w["w_branch"][k], tb=True, name=f"branch{k}_bwd_x{l}") for k in range(3)]
    g_branch = jnp.stack([_mm(s["ys"][k], dts[k], ta=True, out_dtype=BF16, name=f"branch{k}_bwd_w{l}")
                          for k in range(3)])
    early = dict(w_branch=g_branch.reshape(3 * D, D), w_out=g_out, w_up=g_up, w_down=g_down)
    hosts = hosts or {}
    in_attn, in_gating = hosts["early"](early) if "early" in hosts else (_NO_EXCHANGE, _NO_EXCHANGE)
    carried = {}
    dq, dk, dv, d_sink, carried["attn"] = _attention_bwd(s["q"], s["kd"], s["vd"], s["sink_b"], dys[0], n_lat, n_ctx,
                                                         name=f"attn_bwd{l}", hosted=in_attn)
    dz_qkv = _qkv_unprep(dq, dk, dv, cos_t, sin_t, name=f"qkv_unprep{l}")
    dz_a, d_ws, d_bs, st_gv, carried["gating"] = _gating_bwd(s["z_a"], dys[1], s["w_spatial"], s["b_sb"], s["g_v"],
                                                             name=f"gating_bwd{l}", hosted=in_gating)
    dz_b, st_sc = _sconv_bwd(s["z_b"], dys[2], s["w_sconv8"], n_lat, name=f"sconv_bwd{l}")
    dzs = (dz_qkv, dz_a, dz_b, dz_g)
    g_in = jnp.concatenate([_mm(s["h"], dzs[k], ta=True, out_dtype=BF16, name=f"in_bwd_w{k}_{l}")
                            for k in range(4)], axis=1)
    dh_parts = [_mm(dzs[k], s["w_seg"][k], tb=True, name=f"in_bwd_x{k}_{l}") for k in range(4)]
    in_norm1 = hosts["w_in"](g_in) if "w_in" in hosts else _NO_EXCHANGE
    dx0, st_n1, carried["norm1"], _ = _norm_mod_bwd(s["x0"], dh_parts, dx1, s["g_mix"], mod8, 1, n_lat,
                                                    name=f"norm1_bwd{l}", hosted=in_norm1)
    dmod = jnp.concatenate([st_n1[0:2], st_n1[2:4], st_gt1[0:2], st_n2[0:2], st_n2[2:4], st_gt2[0:2]], axis=1)
    dmod128 = _pad_rows(dmod, LANES)
    g_mod = _mm(a128, dmod128, ta=True, out_dtype=BF16, name=f"mod_bwd_w{l}")
    d_a128 = _mm(dmod128, w["w_mod"], tb=True, name=f"mod_bwd_x{l}")
    wgrads = dict(early, w_mod=g_mod, w_in=g_in)
    lgrads = dict(b_mod=dmod[0] + dmod[1], g_mix=st_n1[4], g_ffn=st_n2[4], b_gate=st_bg[0], sink=d_sink[:, 0],
                  w_spatial=d_ws, b_spatial=d_bs[:, :, 0], g_v=st_gv[0], w_sconv=st_sc[0:3], w_fconv=st_fc[0:3])
    return dx0, wgrads, lgrads, d_a128, carried


_SMALL_ORDER = ("c_ctx", "b_mod", "g_mix", "b_gate", "sink", "w_spatial", "b_spatial", "g_v", "w_sconv", "g_ffn",
                "w_fconv", "g_final")


def _flat_pack(parts, width):
    flat = jnp.concatenate([p.reshape(-1).astype(F32) for p in parts])
    rows = -(-flat.shape[0] // (width * SUBLANES)) * SUBLANES
    flat = jnp.concatenate([flat, jnp.zeros((rows * width - flat.shape[0],), F32)])
    return flat.reshape(rows, width)


def _flat_unpack(packed, likes):
    flat = packed.reshape(-1)
    out, off = [], 0
    for like in likes:
        n = math.prod(like.shape)
        out.append(flat[off:off + n].reshape(like.shape))
        off += n
    return out


def kernel(x, c, ctx, c_ctx, w_mod, b_mod, g_mix, w_in, b_gate, sink, w_spatial, b_spatial, g_v, w_sconv, w_branch, w_out, g_ffn, w_up, w_fconv, w_down, g_final, loss_target, m_c_ctx, m_w_mod, m_b_mod, m_g_mix, m_w_in, m_b_gate, m_sink, m_w_spatial, m_b_spatial, m_g_v, m_w_sconv, m_w_branch, m_w_out, m_g_ffn, m_w_up, m_w_fconv, m_w_down, m_g_final, v_c_ctx, v_w_mod, v_b_mod, v_g_mix, v_w_in, v_b_gate, v_sink, v_w_spatial, v_b_spatial, v_g_v, v_w_sconv, v_w_branch, v_w_out, v_g_ffn, v_w_up, v_w_fconv, v_w_down, v_g_final):
    n_lat, n_ctx = x.shape[1], ctx.shape[1]
    chip = 2 * lax.axis_index("x") + lax.axis_index("y")
    weights = dict(c_ctx=c_ctx, w_mod=w_mod, b_mod=b_mod, g_mix=g_mix, w_in=w_in, b_gate=b_gate, sink=sink,
                   w_spatial=w_spatial, b_spatial=b_spatial, g_v=g_v, w_sconv=w_sconv, w_branch=w_branch, w_out=w_out,
                   g_ffn=g_ffn, w_up=w_up, w_fconv=w_fconv, w_down=w_down, g_final=g_final)
    m_in = dict(c_ctx=m_c_ctx, w_mod=m_w_mod, b_mod=m_b_mod, g_mix=m_g_mix, w_in=m_w_in, b_gate=m_b_gate, sink=m_sink,
                w_spatial=m_w_spatial, b_spatial=m_b_spatial, g_v=m_g_v, w_sconv=m_w_sconv, w_branch=m_w_branch,
                w_out=m_w_out, g_ffn=m_g_ffn, w_up=m_w_up, w_fconv=m_w_fconv, w_down=m_w_down, g_final=m_g_final)
    v_in = dict(c_ctx=v_c_ctx, w_mod=v_w_mod, b_mod=v_b_mod, g_mix=v_g_mix, w_in=v_w_in, b_gate=v_b_gate, sink=v_sink,
                w_spatial=v_w_spatial, b_spatial=v_b_spatial, g_v=v_g_v, w_sconv=v_w_sconv, w_branch=v_w_branch,
                w_out=v_w_out, g_ffn=v_g_ffn, w_up=v_w_up, w_fconv=v_w_fconv, w_down=v_w_down, g_final=v_g_final)
    big_names = [n for n, _, _ in _BIG]

    conv_pack = _flat_pack([w_sconv, w_fconv], LANES)
    conv_all = _allgather8(conv_pack, name="gather_conv_weights")
    conv_parts = [_flat_unpack(conv_all[2 * p], [w_sconv, w_fconv]) for p in range(N_CHIPS)]
    w_sconv_full = jnp.concatenate([cp[0] for cp in conv_parts], axis=-1)
    w_fconv_full = jnp.concatenate([cp[1] for cp in conv_parts], axis=-1)

    small = dict(b_mod=b_mod, g_mix=g_mix, b_gate=b_gate, sink=sink, w_spatial=w_spatial, b_spatial=b_spatial, g_v=g_v,
                 w_sconv=w_sconv_full, g_ffn=g_ffn, w_fconv=w_fconv_full, g_final=g_final)
    x_tok = jnp.concatenate([x[0], ctx[0]], axis=0)
    step = _step_context(c[0], c_ctx, n_lat, n_ctx)
    layer_small = [{k: v[l] for k, v in small.items() if k != "g_final"} for l in range(DEPTH)]
    my_half = lax.axis_index("c").astype(jnp.int32).reshape(1)

    shards = {n: weights[n].astype(BF16) for n in big_names}
    pack = [_pack_shards(shards, l) for l in range(DEPTH)]
    first = _gather_finish(_gather_ici(pack[0][:1], name="gather_ici0"), pack[0][:1], name="gather_finish0")
    w0 = dict(w_mod=_unpack_cols(first[0]))

    def layer0_w_in(carried):
        w0["w_in"] = _unpack_cols(_gather_finish(list(carried), pack[0][1:2], name="gather_finish0_in")[0])
        return w0["w_in"]

    def layer0_late_weights(carried):
        rest = _gather_finish(list(carried[1:]), pack[0][2:], name="gather_finish0_late")
        w0.update(w_up=_unpack_cols(rest[0]), **_unpack_rows(rest[1]))
        return w0

    xs, saved0, (part_attn, part_gating, part_ffn) = _layer_fwd(
        0, x_tok, w0, layer_small[0], step, hosted=_gather_ici_hosted(pack[1][1:2] + pack[0][2:]),
        hosted_gating=_gather_ici_hosted(pack[1][:1]), hosted_ffn=_gather_ici_hosted(pack[1][2:]),
        late_weights=layer0_late_weights, hosted_norm1=_gather_ici_hosted(pack[0][1:2]), w_in_late=layer0_w_in)
    partial1 = list(part_gating) + list(part_attn[:1]) + list(part_ffn)
    w1 = _unpack_full(_gather_finish(partial1, pack[1], name="gather_finish1"))
    xs, saved1, _ = _layer_fwd(1, xs, w1, layer_small[1], step)
    dx, sq_err, d_g_final = _loss_bwd(xs, loss_target[0], g_final, n_lat)
    loss = lax.psum(0.5 * jnp.sum(sq_err) / D, ("x", "y", "c"))

    def reduce_start(g_packs, tag):
        got = _grad_sibling_swap(g_packs, name=f"grad_sibling_swap{tag}")
        return [_add_half(my_half, a, b, name=f"grad_pair_sum{tag}_{g}") for g, (a, b) in enumerate(zip(g_packs, got))]

    def reduce_finish(exchanged, tag):
        sums = [_sum_slabs(e, F32, name=f"grad_chip_sum{tag}_{g}") for g, e in enumerate(exchanged)]
        return _grad_sibling_share(sums, name=f"grad_sibling_share{tag}")

    dx, wgrads1, lgrads1, d_a1, _ = _layer_bwd(1, saved1, w1, dx, step)
    pair_sum1 = reduce_start(_pack_grads(wgrads1), "1")

    def carried_early(early):
        pair_sum0_early = reduce_start([_pack_grad_cols(early["w_up"]), _pack_grad_rows(early)], "0_early")
        return _grad_ici_hosted(pair_sum1 + pair_sum0_early[1:]), _grad_ici_hosted(pair_sum0_early[:1])

    def carried_w_in(g_in):
        return _grad_ici_hosted(reduce_start([_pack_grad_cols(g_in)], "0_in"))

    dx, wgrads0, lgrads0, d_a0, exchanged = _layer_bwd(0, saved0, w0, dx, step,
                                                       hosts=dict(early=carried_early, w_in=carried_w_in))
    total1 = reduce_finish(exchanged["attn"][:4], "1")
    total0_early = reduce_finish(list(exchanged["gating"]) + list(exchanged["attn"][4:]), "0_early")
    total0_in = reduce_finish(exchanged["norm1"], "0_in")
    mod_sum = reduce_start([_pack_grad_cols(wgrads0["w_mod"])], "0_mod")
    sgrads = _small_grads([lgrads0, lgrads1], [d_a0, d_a1], d_g_final, step)
    grad_x = dx[:n_lat][None]

    s_likes = [sgrads[n] for n in _SMALL_ORDER]
    s_all = _allgather8(_flat_pack(s_likes, D), name="gather_small_grads")
    s_tot = _flat_unpack(_sum_slabs(s_all, F32, name="small_grad_sum"), s_likes)
    grads = dict(zip(_SMALL_ORDER, s_tot))
    grads["w_sconv"] = lax.dynamic_slice_in_dim(grads["w_sconv"], chip * w_sconv.shape[-1], w_sconv.shape[-1], axis=2)
    grads["w_fconv"] = lax.dynamic_slice_in_dim(grads["w_fconv"], chip * w_fconv.shape[-1], w_fconv.shape[-1], axis=2)

    delta, new_m, new_v = {}, {}, {}

    def adamw(n, hosted=None):
        cols = weights[n].shape[-1]
        view = lambda a: a.reshape(-1, cols)
        d_, m_, v_, carried = _adamw(view(weights[n]), view(grads[n]), view(m_in[n]), view(v_in[n]), name=f"adamw_{n}",
                                     hosted=hosted)
        delta[n], new_m[n], new_v[n] = (t.reshape(weights[n].shape) for t in (d_, m_, v_))
        return carried

    grads["w_in"] = jnp.stack([total0_in[0], total1[1]])
    total0_mod = reduce_finish(adamw("w_in", _grad_ici_hosted(mod_sum)), "0_mod")
    total0 = list(total0_mod) + list(total0_in) + list(total0_early)
    rest = _unpack_shards([total0, total1], {n: weights[n] for n in big_names})
    grads.update({n: g for n, g in rest.items() if n != "w_in"})
    for n in big_names:
        if n != "w_in":
            adamw(n)
    likes = [weights[n] for n in _SMALL_ORDER]
    packs = [_flat_pack([src[n] for n in _SMALL_ORDER], D) for src in (weights, grads, m_in, v_in)]
    outs = _adamw(*packs, name="adamw_small")[:3]
    for dst, packed in zip((delta, new_m, new_v), outs):
        for n, val in zip(_SMALL_ORDER, _flat_unpack(packed, likes)):
            dst[n] = val

    order = ("c_ctx", "w_mod", "b_mod", "g_mix", "w_in", "b_gate", "sink", "w_spatial", "b_spatial", "g_v", "w_sconv",
             "w_branch", "w_out", "g_ffn", "w_up", "w_fconv", "w_down", "g_final")
    return (loss, grad_x, *[grads[n] for n in order], *[delta[n] for n in order], *[new_m[n] for n in order],
            *[new_v[n] for n in order])
```
